```python
import jax, jax.numpy as jnp
from jax import lax
import numpy as np

D_MODEL = 1024
BATCH = 8
SEQ = 8192
DEPTH = 1

CHUNK = 64
Q_BLOCK = 128
SB_HEAD_DIM = 64
SB_HEADS = (D_MODEL // 2) // SB_HEAD_DIM
SB_WIDTH = SB_HEADS * SB_HEAD_DIM
GLA_HEADS = 4
GLA_DV = (D_MODEL // 2) // GLA_HEADS
GLA_DK = GLA_DV // 2
GLA_VWIDTH = GLA_HEADS * GLA_DV
GLA_KWIDTH = GLA_HEADS * GLA_DK
GLA_GATE_RANK = 16
GLA_TAU = 16.0
MIX_WIDTH = SB_WIDTH + GLA_VWIDTH
D_FF = 2816
CONV_WIDTH = 3
N_MOD = 6
EPS = 1e-6
IN_SPLITS = (SB_WIDTH, SB_WIDTH, SB_WIDTH,
             GLA_KWIDTH, GLA_KWIDTH, GLA_VWIDTH,
             GLA_VWIDTH, GLA_GATE_RANK)
IN_WIDTH = sum(IN_SPLITS)

kernel_name = "hybrid_stickbreak_gla_convffn_adaln"


def rms_norm(x, gain):
    xf = x.astype(jnp.float32)
    n = xf * lax.rsqrt(jnp.mean(xf * xf, axis=-1, keepdims=True) + EPS)
    return (n * gain.astype(jnp.float32)).astype(x.dtype)


def modulate(h, shift, scale):
    return h * (1 + scale[:, None, :]) + shift[:, None, :]


def stick_breaking_attention(q, k, v):
    B, S, H, d = q.shape
    q = q.transpose(0, 2, 1, 3)
    k = k.transpose(0, 2, 1, 3)
    v = v.transpose(0, 2, 1, 3)
    scale = d ** -0.5
    outs = []
    for i in range(S // Q_BLOCK):
        start = i * Q_BLOCK
        end = start + Q_BLOCK
        qb = q[:, :, start:end]
        kb = k[:, :, :end]
        vb = v[:, :, :end]
        z = jnp.einsum('bhqd,bhkd->bhqk', qb, kb).astype(jnp.float32) * scale
        t_pos = start + jnp.arange(Q_BLOCK)
        s_pos = jnp.arange(end)
        mask = s_pos[None, :] < t_pos[:, None]
        log_one_minus = jnp.where(mask, jax.nn.log_sigmoid(-z), 0.0)
        shifted = jnp.concatenate(
            [log_one_minus[..., 1:], jnp.zeros_like(log_one_minus[..., :1])], axis=-1)
        between = lax.cumsum(shifted, axis=3, reverse=True)
        weights = jnp.where(mask, jnp.exp(jax.nn.log_sigmoid(z) + between), 0.0)
        outs.append(jnp.einsum('bhqk,bhkd->bhqd', weights.astype(vb.dtype), vb))
    o = jnp.concatenate(outs, axis=2)
    return o.transpose(0, 2, 1, 3).reshape(B, S, H * d)


def gla_chunk_causal(q, k, v, log_alpha):
    B, S, H, dk = q.shape
    dv = v.shape[-1]
    nc = S // CHUNK
    qc = q.reshape(B, nc, CHUNK, H, dk).astype(jnp.float32) * (dk ** -0.5)
    kc = k.reshape(B, nc, CHUNK, H, dk).astype(jnp.float32)
    vc = v.reshape(B, nc, CHUNK, H, dv).astype(jnp.float32)
    la = log_alpha.reshape(B, nc, CHUNK, H, dk).astype(jnp.float32)
    cum = jnp.cumsum(la, axis=2)
    total = cum[:, :, -1]
    k_dec = kc * jnp.exp(total[:, :, None] - cum)
    chunk_kv = jnp.einsum('bnchk,bnchv->nbhkv', k_dec, vc)
    chunk_decay = jnp.exp(total).transpose(1, 0, 2, 3)

    def step(state, inp):
        decay, kv = inp
        state = decay[..., None] * state + kv
        return state, state

    s0 = jnp.zeros((B, H, dk, dv), jnp.float32)
    _, states = lax.scan(step, s0, (chunk_decay, chunk_kv))
    o = jnp.einsum('bnchk,nbhkv->bnchv', qc, states)
    return o.reshape(B, S, H * dv)


def causal_depthwise_conv(u, w, b):
    S = u.shape[1]
    up = jnp.pad(u, ((0, 0), (CONV_WIDTH - 1, 0), (0, 0)))
    out = b
    for j in range(CONV_WIDTH):
        out = out + w[j] * up[:, j:j + S]
    return out


def _fwd_setup_inputs(seed: int = 0) -> dict:
    key = jax.random.key(seed)
    ks = jax.random.split(key, 18)

    def nrm(k, shape, scale):
        return jax.random.normal(k, shape, jnp.float32) * scale

    L, D = DEPTH, D_MODEL
    return {
        "x": nrm(ks[0], (BATCH, SEQ, D), 1.0),
        "c": nrm(ks[1], (BATCH, D), 1.0),
        "w_ada": nrm(ks[2], (L, D, N_MOD * D), 0.5 * D ** -0.5),
        "b_ada": nrm(ks[3], (L, N_MOD * D), 0.01),
        "g_norm1": 1.0 + nrm(ks[4], (L, D), 0.02),
        "w_in": nrm(ks[5], (L, D, IN_WIDTH), D ** -0.5),
        "w_fg2": nrm(ks[6], (L, GLA_GATE_RANK, GLA_KWIDTH), GLA_GATE_RANK ** -0.5),
        "b_fg2": nrm(ks[7], (L, GLA_KWIDTH), 0.1),
        "g_gla_out": 1.0 + nrm(ks[8], (L, GLA_VWIDTH), 0.02),
        "w_out": nrm(ks[9], (L, MIX_WIDTH, D), MIX_WIDTH ** -0.5),
        "g_norm2": 1.0 + nrm(ks[10], (L, D), 0.02),
        "w_up": nrm(ks[11], (L, D, 2 * D_FF), D ** -0.5),
        "w_conv": nrm(ks[12], (L, CONV_WIDTH, 2 * D_FF), CONV_WIDTH ** -0.5),
        "b_conv": nrm(ks[13], (L, 2 * D_FF), 0.01),
        "w_down": nrm(ks[14], (L, D_FF, D), D_FF ** -0.5),
        "g_final": 1.0 + nrm(ks[15], (D,), 0.02),
    }


def _fwd_reference(x, c, w_ada, b_ada, g_norm1, w_in, w_fg2, b_fg2, g_gla_out, w_out,
              g_norm2, w_up, w_conv, b_conv, w_down, g_final):
    B, S, _ = x.shape
    offsets = np.cumsum(IN_SPLITS)[:-1].tolist()
    for l in range(DEPTH):
        mod = jax.nn.silu(c) @ w_ada[l] + b_ada[l]
        shift1, scale1, gate1, shift2, scale2, gate2 = jnp.split(mod, N_MOD, axis=-1)

        h = modulate(rms_norm(x, g_norm1[l]), shift1, scale1)
        proj = h @ w_in[l]
        sb_q, sb_k, sb_v, gq, gk, gv, gg, gf = jnp.split(proj, offsets, axis=-1)

        o_sb = stick_breaking_attention(
            sb_q.reshape(B, S, SB_HEADS, SB_HEAD_DIM),
            sb_k.reshape(B, S, SB_HEADS, SB_HEAD_DIM),
            sb_v.reshape(B, S, SB_HEADS, SB_HEAD_DIM))

        log_alpha = jax.nn.log_sigmoid(
            (gf @ w_fg2[l] + b_fg2[l]).astype(jnp.float32)) / GLA_TAU
        o_gla = gla_chunk_causal(
            gq.reshape(B, S, GLA_HEADS, GLA_DK),
            gk.reshape(B, S, GLA_HEADS, GLA_DK),
            gv.reshape(B, S, GLA_HEADS, GLA_DV),
            log_alpha.reshape(B, S, GLA_HEADS, GLA_DK))
        oh = o_gla.reshape(B, S, GLA_HEADS, GLA_DV)
        oh = oh * lax.rsqrt(jnp.mean(oh * oh, axis=-1, keepdims=True) + EPS)
        o_gla = (oh.reshape(B, S, GLA_VWIDTH) * g_gla_out[l].astype(jnp.float32)
                 ).astype(x.dtype) * jax.nn.silu(gg)

        mixed = jnp.concatenate([o_sb.astype(x.dtype), o_gla], axis=-1) @ w_out[l]
        x = x + (1 + gate1[:, None, :]) * mixed

        h2 = modulate(rms_norm(x, g_norm2[l]), shift2, scale2)
        u = causal_depthwise_conv(h2 @ w_up[l], w_conv[l], b_conv[l])
        val, gte = jnp.split(u, 2, axis=-1)
        x = x + (1 + gate2[:, None, :]) * ((val * jax.nn.silu(gte)) @ w_down[l])
    return rms_norm(x, g_final)


import jax as _jax
import jax.numpy as _jnp

TWIN_FORMAT = 'train_step'
FWD_PARAMS = ['x', 'c', 'w_ada', 'b_ada', 'g_norm1', 'w_in', 'w_fg2', 'b_fg2', 'g_gla_out', 'w_out', 'g_norm2', 'w_up', 'w_conv', 'b_conv', 'w_down', 'g_final']
TWIN_WEIGHTS = ['w_ada', 'b_ada', 'g_norm1', 'w_in', 'w_fg2', 'b_fg2', 'g_gla_out', 'w_out', 'g_norm2', 'w_up', 'w_conv', 'b_conv', 'w_down', 'g_final']
TWIN_DIFF_INPUT = 'x'
TWIN_INPUTS = ['x', 'c', 'w_ada', 'b_ada', 'g_norm1', 'w_in', 'w_fg2', 'b_fg2', 'g_gla_out', 'w_out', 'g_norm2', 'w_up', 'w_conv', 'b_conv', 'w_down', 'g_final', 'loss_target', 'm_w_ada', 'm_b_ada', 'm_g_norm1', 'm_w_in', 'm_w_fg2', 'm_b_fg2', 'm_g_gla_out', 'm_w_out', 'm_g_norm2', 'm_w_up', 'm_w_conv', 'm_b_conv', 'm_w_down', 'm_g_final', 'v_w_ada', 'v_b_ada', 'v_g_norm1', 'v_w_in', 'v_w_fg2', 'v_b_fg2', 'v_g_gla_out', 'v_w_out', 'v_g_norm2', 'v_w_up', 'v_w_conv', 'v_b_conv', 'v_w_down', 'v_g_final']
TWIN_OUTPUTS = ['loss', 'grad_x', 'grad_w_ada', 'grad_b_ada', 'grad_g_norm1', 'grad_w_in', 'grad_w_fg2', 'grad_b_fg2', 'grad_g_gla_out', 'grad_w_out', 'grad_g_norm2', 'grad_w_up', 'grad_w_conv', 'grad_b_conv', 'grad_w_down', 'grad_g_final', 'delta_w_ada', 'delta_b_ada', 'delta_g_norm1', 'delta_w_in', 'delta_w_fg2', 'delta_b_fg2', 'delta_g_gla_out', 'delta_w_out', 'delta_g_norm2', 'delta_w_up', 'delta_w_conv', 'delta_b_conv', 'delta_w_down', 'delta_g_final', 'new_m_w_ada', 'new_m_b_ada', 'new_m_g_norm1', 'new_m_w_in', 'new_m_w_fg2', 'new_m_b_fg2', 'new_m_g_gla_out', 'new_m_w_out', 'new_m_g_norm2', 'new_m_w_up', 'new_m_w_conv', 'new_m_b_conv', 'new_m_w_down', 'new_m_g_final', 'new_v_w_ada', 'new_v_b_ada', 'new_v_g_norm1', 'new_v_w_in', 'new_v_w_fg2', 'new_v_b_fg2', 'new_v_g_gla_out', 'new_v_w_out', 'new_v_g_norm2', 'new_v_w_up', 'new_v_w_conv', 'new_v_b_conv', 'new_v_w_down', 'new_v_g_final']
TWIN_LEAF_KINDS = {'loss': 'loss', 'grad_x': 'grad_x', 'grad_w_ada': 'grad_w', 'grad_b_ada': 'grad_w', 'grad_g_norm1': 'grad_w', 'grad_w_in': 'grad_w', 'grad_w_fg2': 'grad_w', 'grad_b_fg2': 'grad_w', 'grad_g_gla_out': 'grad_w', 'grad_w_out': 'grad_w', 'grad_g_norm2': 'grad_w', 'grad_w_up': 'grad_w', 'grad_w_conv': 'grad_w', 'grad_b_conv': 'grad_w', 'grad_w_down': 'grad_w', 'grad_g_final': 'grad_w', 'delta_w_ada': 'delta_w', 'delta_b_ada': 'delta_w', 'delta_g_norm1': 'delta_w', 'delta_w_in': 'delta_w', 'delta_w_fg2': 'delta_w', 'delta_b_fg2': 'delta_w', 'delta_g_gla_out': 'delta_w', 'delta_w_out': 'delta_w', 'delta_g_norm2': 'delta_w', 'delta_w_up': 'delta_w', 'delta_w_conv': 'delta_w', 'delta_b_conv': 'delta_w', 'delta_w_down': 'delta_w', 'delta_g_final': 'delta_w', 'new_m_w_ada': 'new_m', 'new_m_b_ada': 'new_m', 'new_m_g_norm1': 'new_m', 'new_m_w_in': 'new_m', 'new_m_w_fg2': 'new_m', 'new_m_b_fg2': 'new_m', 'new_m_g_gla_out': 'new_m', 'new_m_w_out': 'new_m', 'new_m_g_norm2': 'new_m', 'new_m_w_up': 'new_m', 'new_m_w_conv': 'new_m', 'new_m_b_conv': 'new_m', 'new_m_w_down': 'new_m', 'new_m_g_final': 'new_m', 'new_v_w_ada': 'new_v', 'new_v_b_ada': 'new_v', 'new_v_g_norm1': 'new_v', 'new_v_w_in': 'new_v', 'new_v_w_fg2': 'new_v', 'new_v_b_fg2': 'new_v', 'new_v_g_gla_out': 'new_v', 'new_v_w_out': 'new_v', 'new_v_g_norm2': 'new_v', 'new_v_w_up': 'new_v', 'new_v_w_conv': 'new_v', 'new_v_b_conv': 'new_v', 'new_v_w_down': 'new_v', 'new_v_g_final': 'new_v'}


def _forward(args):
    return _fwd_reference(*[args[k] for k in FWD_PARAMS])


def _output_shape():
    def fwd():
        inp = _fwd_setup_inputs(0)
        return _fwd_reference(*[inp[k] for k in FWD_PARAMS])
    out = _jax.eval_shape(fwd)
    return out.shape, out.dtype

N_MICROBATCH = 1
ADAM_LR = 0.001
ADAM_B1 = 0.9
ADAM_B2 = 0.999
ADAM_EPS = 1e-08
ADAM_WD = 0.01
ADAM_STEP = 10
PER_EXAMPLE_BATCH_AXIS = {'x': 0, 'c': 0, 'loss_target': 0}
SHARED_INPUTS = []
_WEIGHT_DTYPES = {'w_ada': _jnp.float32, 'b_ada': _jnp.float32, 'g_norm1': _jnp.float32, 'w_in': _jnp.float32, 'w_fg2': _jnp.float32, 'b_fg2': _jnp.float32, 'g_gla_out': _jnp.float32, 'w_out': _jnp.float32, 'g_norm2': _jnp.float32, 'w_up': _jnp.float32, 'w_conv': _jnp.float32, 'b_conv': _jnp.float32, 'w_down': _jnp.float32, 'g_final': _jnp.float32}
MOMENT_SCALE = {'w_ada': 1.422688e-01, 'b_ada': 2.982869e-01, 'g_norm1': 2.554039e-01, 'w_in': 1.603667e-01, 'w_fg2': 4.174256e-02, 'b_fg2': 1.352351e-01, 'g_gla_out': 1.615884e-01, 'w_out': 1.688107e-01, 'g_norm2': 1.786856e-01, 'w_up': 8.063960e-02, 'w_conv': 8.053999e-02, 'b_conv': 7.112309e-02, 'w_down': 1.324211e-01, 'g_final': 6.426262e+01}


def _to_microbatches(a, axis):
    t = _jnp.moveaxis(a, axis, 0)
    t = t.reshape((N_MICROBATCH, t.shape[0] // N_MICROBATCH) + t.shape[1:])
    return _jnp.moveaxis(t, 1, axis + 1)


def setup_inputs(seed: int = 0) -> dict:
    inp = _fwd_setup_inputs(seed)
    key = _jax.random.fold_in(_jax.random.key(seed), 7919)
    shape, _ = _output_shape()
    out = dict(inp)
    out["loss_target"] = _jax.random.normal(_jax.random.fold_in(key, 0), shape, _jnp.float32)
    for i, name in enumerate(TWIN_WEIGHTS):
        w = inp[name].astype(_jnp.float32)
        if MOMENT_SCALE is None:
            s = _jnp.sqrt(_jnp.mean(_jnp.square(w)) + 1e-30)
        else:
            s = MOMENT_SCALE[name]
        km, kv = _jax.random.split(_jax.random.fold_in(key, i + 1))
        out[name] = w
        out["m_" + name] = s * _jax.random.normal(km, w.shape, _jnp.float32)
        out["v_" + name] = (s * s) * _jax.random.uniform(kv, w.shape, _jnp.float32, 0.5, 1.5)
    if N_MICROBATCH > 1:
        for name, axis in PER_EXAMPLE_BATCH_AXIS.items():
            out[name] = _to_microbatches(out[name], axis)
    return {'x': out['x'], 'c': out['c'], 'w_ada': out['w_ada'], 'b_ada': out['b_ada'], 'g_norm1': out['g_norm1'], 'w_in': out['w_in'], 'w_fg2': out['w_fg2'], 'b_fg2': out['b_fg2'], 'g_gla_out': out['g_gla_out'], 'w_out': out['w_out'], 'g_norm2': out['g_norm2'], 'w_up': out['w_up'], 'w_conv': out['w_conv'], 'b_conv': out['b_conv'], 'w_down': out['w_down'], 'g_final': out['g_final'], 'loss_target': out['loss_target'], 'm_w_ada': out['m_w_ada'], 'm_b_ada': out['m_b_ada'], 'm_g_norm1': out['m_g_norm1'], 'm_w_in': out['m_w_in'], 'm_w_fg2': out['m_w_fg2'], 'm_b_fg2': out['m_b_fg2'], 'm_g_gla_out': out['m_g_gla_out'], 'm_w_out': out['m_w_out'], 'm_g_norm2': out['m_g_norm2'], 'm_w_up': out['m_w_up'], 'm_w_conv': out['m_w_conv'], 'm_b_conv': out['m_b_conv'], 'm_w_down': out['m_w_down'], 'm_g_final': out['m_g_final'], 'v_w_ada': out['v_w_ada'], 'v_b_ada': out['v_b_ada'], 'v_g_norm1': out['v_g_norm1'], 'v_w_in': out['v_w_in'], 'v_w_fg2': out['v_w_fg2'], 'v_b_fg2': out['v_b_fg2'], 'v_g_gla_out': out['v_g_gla_out'], 'v_w_out': out['v_w_out'], 'v_g_norm2': out['v_g_norm2'], 'v_w_up': out['v_w_up'], 'v_w_conv': out['v_w_conv'], 'v_b_conv': out['v_b_conv'], 'v_w_down': out['v_w_down'], 'v_g_final': out['v_g_final']}


def _loss(weights, diff, rest, loss_target):
    with _jax.named_scope("forward"):
        args = {**rest, TWIN_DIFF_INPUT: diff, **{k: w.astype(_WEIGHT_DTYPES[k]) for k, w in weights.items()}}
        y = _forward(args)
    with _jax.named_scope("loss_head"):
        err = _jnp.square(y.astype(_jnp.float32) - loss_target)
        return 0.5 * _jnp.sum(_jnp.mean(err, axis=-1)) if err.ndim else 0.5 * err


def _adamw(w, g, m, v):
    m = ADAM_B1 * m + (1.0 - ADAM_B1) * g
    v = ADAM_B2 * v + (1.0 - ADAM_B2) * _jnp.square(g)
    m_hat = m / (1.0 - ADAM_B1 ** ADAM_STEP)
    v_hat = v / (1.0 - ADAM_B2 ** ADAM_STEP)
    delta = -ADAM_LR * (m_hat / (_jnp.sqrt(v_hat) + ADAM_EPS) + ADAM_WD * w)
    return delta, m, v


def reference(x, c, w_ada, b_ada, g_norm1, w_in, w_fg2, b_fg2, g_gla_out, w_out, g_norm2, w_up, w_conv, b_conv, w_down, g_final, loss_target, m_w_ada, m_b_ada, m_g_norm1, m_w_in, m_w_fg2, m_b_fg2, m_g_gla_out, m_w_out, m_g_norm2, m_w_up, m_w_conv, m_b_conv, m_w_down, m_g_final, v_w_ada, v_b_ada, v_g_norm1, v_w_in, v_w_fg2, v_b_fg2, v_g_gla_out, v_w_out, v_g_norm2, v_w_up, v_w_conv, v_b_conv, v_w_down, v_g_final):
    given = dict(x=x, c=c, w_ada=w_ada, b_ada=b_ada, g_norm1=g_norm1, w_in=w_in, w_fg2=w_fg2, b_fg2=b_fg2, g_gla_out=g_gla_out, w_out=w_out, g_norm2=g_norm2, w_up=w_up, w_conv=w_conv, b_conv=b_conv, w_down=w_down, g_final=g_final, loss_target=loss_target, m_w_ada=m_w_ada, m_b_ada=m_b_ada, m_g_norm1=m_g_norm1, m_w_in=m_w_in, m_w_fg2=m_w_fg2, m_b_fg2=m_b_fg2, m_g_gla_out=m_g_gla_out, m_w_out=m_w_out, m_g_norm2=m_g_norm2, m_w_up=m_w_up, m_w_conv=m_w_conv, m_b_conv=m_b_conv, m_w_down=m_w_down, m_g_final=m_g_final, v_w_ada=v_w_ada, v_b_ada=v_b_ada, v_g_norm1=v_g_norm1, v_w_in=v_w_in, v_w_fg2=v_w_fg2, v_b_fg2=v_b_fg2, v_g_gla_out=v_g_gla_out, v_w_out=v_w_out, v_g_norm2=v_g_norm2, v_w_up=v_w_up, v_w_conv=v_w_conv, v_b_conv=v_b_conv, v_w_down=v_w_down, v_g_final=v_g_final)
    weights = {n: given[n] for n in TWIN_WEIGHTS}
    shared = {n: given[n] for n in SHARED_INPUTS}
    per_example = {n: given[n] for n in ['x', 'c']}
    grad_fn = _jax.value_and_grad(_loss, argnums=(0, 1))

    def one_microbatch(ex, loss_target):
        ex = dict(ex)
        diff = ex.pop(TWIN_DIFF_INPUT)
        return grad_fn(weights, diff, {**shared, **ex}, loss_target)

    if N_MICROBATCH == 1:
        loss, (grad_w, grad_x) = one_microbatch(per_example, given["loss_target"])
    else:
        def body(carry, xs):
            loss_sum, grad_sum = carry
            l_k, (gw_k, gx_k) = one_microbatch(xs[0], xs[1])
            with _jax.named_scope("update"):
                return (loss_sum + l_k, _jax.tree.map(_jnp.add, grad_sum, gw_k)), gx_k

        init = (_jnp.zeros((), _jnp.float32), _jax.tree.map(_jnp.zeros_like, weights))
        (loss, grad_w), grad_x = _jax.lax.scan(body, init, (per_example, given["loss_target"]))
    with _jax.named_scope("update"):
        delta_w, new_m, new_v = {}, {}, {}
        for n in TWIN_WEIGHTS:
            delta_w[n], new_m[n], new_v[n] = _adamw(weights[n], grad_w[n], given["m_" + n], given["v_" + n])
    return (loss, grad_x, *[grad_w[n] for n in TWIN_WEIGHTS], *[delta_w[n] for n in TWIN_WEIGHTS],
            *[new_m[n] for n in TWIN_WEIGHTS], *[new_v[n] for n in TWIN_WEIGHTS])
```

```python
import functools

import jax
import jax.numpy as jnp
from jax import lax
from jax.experimental import pallas as pl
from jax.experimental.pallas import tpu as pltpu

F32 = jnp.float32
BF16 = jnp.bfloat16
MESH = pl.DeviceIdType.MESH

D = 1024
H_SB = 8
DK = 64
DV = 128
H_GLA = 4
CHUNK = 64
RANK = 16
N_IN = 3088
N_IN_P = 3200
D_FF = 2816
CB = 256
NCB = D_FF // CB
EPS = 1e-6
QB = 128
SB_SKIP = -120.0

LR, B1, B2, EPS_A, WD, STEP = 0.001, 0.9, 0.999, 1e-08, 0.01, 10

ANY = pl.BlockSpec(memory_space=pl.ANY)
VMEM_SPEC = pl.BlockSpec(memory_space=pltpu.VMEM)


def _cp(ndim=0, vmem_mb=None):
    kw = {}
    if ndim:
        kw["dimension_semantics"] = ("arbitrary",) * ndim
    if vmem_mb:
        kw["vmem_limit_bytes"] = vmem_mb * 1024 * 1024
    return pltpu.CompilerParams(**kw)


def _dot(a, b):
    return jnp.dot(a, b, preferred_element_type=F32)


def _dot_nt(a, b):
    return lax.dot_general(a, b, (((1,), (1,)), ((), ())), preferred_element_type=F32)


def _dot_tn(a, b):
    return lax.dot_general(a, b, (((0,), (0,)), ((), ())), preferred_element_type=F32)


def _split(x):
    hi = x.astype(BF16)
    lo = (x - hi.astype(F32)).astype(BF16)
    return hi, lo


def _sigmoid(x):
    return jax.nn.sigmoid(x)


def _log_sigmoid_parts(z):
    e = jnp.exp(-jnp.abs(z))
    sp = jnp.log1p(e)
    return -(jnp.maximum(z, 0.0) + sp), jnp.minimum(z, 0.0) - sp, e


def _row_tile(rows, cols, budget=512 * 1024):
    best = None
    for t in range(8, rows + 1, 8):
        if rows % t == 0 and t * cols * 4 <= budget:
            best = t
    return best if best is not None else rows


def _flip(v, bit):
    return 1 - v if bit else v


def _allgather8(a, name):
    def body(a_ref, o_ref, ssem, rsem, lsem):
        x, y, c = lax.axis_index("x"), lax.axis_index("y"), lax.axis_index("c")
        me = 4 * x + 2 * y + c
        loc = pltpu.make_async_copy(a_ref, o_ref.at[me], lsem)
        loc.start()
        sends = []
        for r in range(1, 8):
            peer = (_flip(x, r & 4), _flip(y, r & 2), _flip(c, r & 1))
            cp = pltpu.make_async_remote_copy(
                src_ref=a_ref, dst_ref=o_ref.at[me], send_sem=ssem.at[r - 1], recv_sem=rsem.at[r - 1],
                device_id=peer, device_id_type=MESH)
            cp.start()
            sends.append(cp)
        for r in range(1, 8):
            peer = (_flip(x, r & 4), _flip(y, r & 2), _flip(c, r & 1))
            pidx = 4 * peer[0] + 2 * peer[1] + peer[2]
            pltpu.make_async_remote_copy(
                src_ref=a_ref, dst_ref=o_ref.at[pidx], send_sem=ssem.at[r - 1], recv_sem=rsem.at[r - 1],
                device_id=peer, device_id_type=MESH).wait_recv()
        for cp in sends:
            cp.wait_send()
        loc.wait()

    return pl.pallas_call(
        body, name=name,
        out_shape=jax.ShapeDtypeStruct((8,) + a.shape, a.dtype),
        in_specs=[VMEM_SPEC], out_specs=VMEM_SPEC,
        scratch_shapes=[pltpu.SemaphoreType.DMA((7,)), pltpu.SemaphoreType.DMA((7,)), pltpu.SemaphoreType.DMA],
    )(a)


def _allgather4(a, name):
    def body(a_ref, o_ref, ssem, rsem, lsem):
        x, y, c = lax.axis_index("x"), lax.axis_index("y"), lax.axis_index("c")
        me = 2 * x + y
        loc = pltpu.make_async_copy(a_ref, o_ref.at[me], lsem)
        loc.start()
        sends = []
        for r in range(1, 4):
            peer = (_flip(x, r & 2), _flip(y, r & 1), c)
            cp = pltpu.make_async_remote_copy(
                src_ref=a_ref, dst_ref=o_ref.at[me], send_sem=ssem.at[r - 1], recv_sem=rsem.at[r - 1],
                device_id=peer, device_id_type=MESH)
            cp.start()
            sends.append(cp)
        for r in range(1, 4):
            peer = (_flip(x, r & 2), _flip(y, r & 1), c)
            pidx = 2 * peer[0] + peer[1]
            pltpu.make_async_remote_copy(
                src_ref=a_ref, dst_ref=o_ref.at[pidx], send_sem=ssem.at[r - 1], recv_sem=rsem.at[r - 1],
                device_id=peer, device_id_type=MESH).wait_recv()
        for cp in sends:
            cp.wait_send()
        loc.wait()

    return pl.pallas_call(
        body, name=name,
        out_shape=jax.ShapeDtypeStruct((4,) + a.shape, a.dtype),
        in_specs=[ANY], out_specs=ANY,
        scratch_shapes=[pltpu.SemaphoreType.DMA((3,)), pltpu.SemaphoreType.DMA((3,)), pltpu.SemaphoreType.DMA],
    )(a)


def _scatter4(slots, name):
    def body(s_ref, o_ref, ssem, rsem):
        x, y, c = lax.axis_index("x"), lax.axis_index("y"), lax.axis_index("c")
        sends = []
        for r in range(1, 4):
            peer = (_flip(x, r & 2), _flip(y, r & 1), c)
            pidx = 2 * peer[0] + peer[1]
            cp = pltpu.make_async_remote_copy(
                src_ref=s_ref.at[pidx], dst_ref=o_ref.at[r - 1], send_sem=ssem.at[r - 1], recv_sem=rsem.at[r - 1],
                device_id=peer, device_id_type=MESH)
            cp.start()
            sends.append(cp)
        for cp in sends:
            cp.wait_recv()
        for cp in sends:
            cp.wait_send()

    return pl.pallas_call(
        body, name=name,
        out_shape=jax.ShapeDtypeStruct((3,) + slots.shape[1:], slots.dtype),
        in_specs=[ANY], out_specs=ANY,
        scratch_shapes=[pltpu.SemaphoreType.DMA((3,)), pltpu.SemaphoreType.DMA((3,))],
    )(slots)


def _pair_swap(p, name):
    def body(p_ref, o_ref, ssem, rsem):
        x, y, c = lax.axis_index("x"), lax.axis_index("y"), lax.axis_index("c")
        cp = pltpu.make_async_remote_copy(
            src_ref=p_ref, dst_ref=o_ref, send_sem=ssem, recv_sem=rsem,
            device_id=(x, y, 1 - c), device_id_type=MESH)
        cp.start()
        cp.wait()

    return pl.pallas_call(
        body, name=name,
        out_shape=jax.ShapeDtypeStruct(p.shape, p.dtype),
        in_specs=[ANY], out_specs=ANY,
        scratch_shapes=[pltpu.SemaphoreType.DMA, pltpu.SemaphoreType.DMA],
    )(p)


def _mm(a, w, out_dtype, name, tm, tn, vmem_mb=None):
    S, K = a.shape
    N = w.shape[1]
    tm = min(tm, S)

    def body(a_ref, w_ref, o_ref):
        o_ref[...] = _dot(a_ref[...], w_ref[...]).astype(o_ref.dtype)

    return pl.pallas_call(
        body, name=name, grid=(N // tn, S // tm),
        in_specs=[pl.BlockSpec((tm, K), lambda n, i: (i, 0)), pl.BlockSpec((K, tn), lambda n, i: (0, n))],
        out_specs=pl.BlockSpec((tm, tn), lambda n, i: (i, n)),
        out_shape=jax.ShapeDtypeStruct((S, N), out_dtype),
        compiler_params=_cp(2, vmem_mb),
    )(a, w)


def _mm_tn(a, b, name, bn, tk, vmem_mb=None):
    S, M = a.shape
    N = b.shape[1]
    tk = min(tk, S)

    def body(a_ref, b_ref, o_ref):
        @pl.when(pl.program_id(1) == 0)
        def _():
            o_ref[...] = jnp.zeros_like(o_ref)
        o_ref[...] += _dot_tn(a_ref[...], b_ref[...])

    return pl.pallas_call(
        body, name=name, grid=(N // bn, S // tk),
        in_specs=[pl.BlockSpec((tk, M), lambda n, k: (k, 0)), pl.BlockSpec((tk, bn), lambda n, k: (k, n))],
        out_specs=pl.BlockSpec((M, bn), lambda n, k: (0, n)),
        out_shape=jax.ShapeDtypeStruct((M, N), F32),
        compiler_params=_cp(2, vmem_mb),
    )(a, b)


def _ada_fwd(c_all, w_sh, b_sh):
    def body(c_ref, w_ref, b_ref, o_ref):
        cv = c_ref[...]
        sc = (cv * _sigmoid(cv)).astype(BF16)
        o_ref[...] = _dot(sc, w_ref[...].astype(BF16)) + b_ref[...]

    return pl.pallas_call(
        body, name="ada_fwd", out_shape=jax.ShapeDtypeStruct((c_all.shape[0], w_sh.shape[1]), F32),
        in_specs=[VMEM_SPEC] * 3, out_specs=VMEM_SPEC, compiler_params=_cp(0, 40),
    )(c_all, w_sh, b_sh)


def _ada_bwd(c_all, dmod_sh):
    def body(c_ref, d_ref, o_ref):
        cv = c_ref[...]
        sc = (cv * _sigmoid(cv)).astype(BF16)
        o_ref[...] = _dot_tn(sc, d_ref[...].astype(BF16))

    return pl.pallas_call(
        body, name="ada_bwd", out_shape=jax.ShapeDtypeStruct((c_all.shape[1], dmod_sh.shape[1]), F32),
        in_specs=[VMEM_SPEC] * 2, out_specs=VMEM_SPEC, compiler_params=_cp(0, 40),
    )(c_all, dmod_sh)


def _vec(tm_unused=None):
    return pl.BlockSpec((1, D), lambda i: (0, 0))


def _rows(tm, width=D):
    return pl.BlockSpec((tm, width), lambda i: (i, 0))


def _norm_mod(x, g, shift, scale, tm=512):
    S = x.shape[0]
    tm = min(tm, S)

    def body(x_ref, g_ref, sh_ref, sc_ref, h_ref):
        xv = x_ref[...]
        r = lax.rsqrt(jnp.mean(xv * xv, axis=-1, keepdims=True) + EPS)
        hn = (xv * r) * g_ref[...]
        h_ref[...] = (hn * (1.0 + sc_ref[...]) + sh_ref[...]).astype(BF16)

    return pl.pallas_call(
        body, name="norm1_mod", grid=(S // tm,),
        in_specs=[_rows(tm), _vec(), _vec(), _vec()], out_specs=_rows(tm),
        out_shape=jax.ShapeDtypeStruct((S, D), BF16), compiler_params=_cp(1),
    )(x, g, shift, scale)


def _resid_norm_mod(x, mixed, gate, g, shift, scale, tm=512):
    S = x.shape[0]
    tm = min(tm, S)

    def body(x_ref, m_ref, gt_ref, g_ref, sh_ref, sc_ref, x1_ref, h_ref):
        x1 = x_ref[...] + (1.0 + gt_ref[...]) * m_ref[...]
        x1_ref[...] = x1
        r = lax.rsqrt(jnp.mean(x1 * x1, axis=-1, keepdims=True) + EPS)
        hn = (x1 * r) * g_ref[...]
        h_ref[...] = (hn * (1.0 + sc_ref[...]) + sh_ref[...]).astype(BF16)

    return pl.pallas_call(
        body, name="resid_norm2_mod", grid=(S // tm,),
        in_specs=[_rows(tm), _rows(tm), _vec(), _vec(), _vec(), _vec()],
        out_specs=[_rows(tm), _rows(tm)],
        out_shape=[jax.ShapeDtypeStruct((S, D), F32), jax.ShapeDtypeStruct((S, D), BF16)],
        compiler_params=_cp(1),
    )(x, mixed, gate, g, shift, scale)


def _conv_taps(ext, w_ref, b_ref):
    e0 = ext
    e1 = pltpu.roll(ext, 1, 0)
    e2 = pltpu.roll(ext, 2, 0)
    u = b_ref[...] + w_ref[0:1, :] * e2
    u = u + w_ref[1:2, :] * e1
    u = u + w_ref[2:3, :] * e0
    return u, e0, e1, e2


def _conv_glu(u0p, wc_p, bc_p, tm=512):
    S = u0p.shape[0]
    tm = min(tm, S)
    hb = tm // 16

    def body(u_ref, p_ref, w_ref, b_ref, a_ref):
        i = pl.program_id(1)
        prev = jnp.where(i > 0, p_ref[...].astype(F32), 0.0)
        ext = jnp.concatenate([prev, u_ref[...].astype(F32)], axis=0)
        u, _, _, _ = _conv_taps(ext, w_ref, b_ref)
        u = u[16:]
        gt = u[:, CB:]
        a_ref[...] = (u[:, :CB] * (gt * _sigmoid(gt))).astype(BF16)

    return pl.pallas_call(
        body, name="conv_glu", grid=(NCB, S // tm),
        in_specs=[pl.BlockSpec((tm, 2 * CB), lambda j, i: (i, j)),
                  pl.BlockSpec((16, 2 * CB), lambda j, i: (jnp.maximum(i * hb - 1, 0), j)),
                  pl.BlockSpec((3, 2 * CB), lambda j, i: (0, j)),
                  pl.BlockSpec((1, 2 * CB), lambda j, i: (0, j))],
        out_specs=pl.BlockSpec((tm, CB), lambda j, i: (i, j)),
        out_shape=jax.ShapeDtypeStruct((S, D_FF), BF16), compiler_params=_cp(2),
    )(u0p, u0p, wc_p, bc_p)


def _conv_glu_bwd(da, u0p, wc_p, bc_p, tm=512):
    S = u0p.shape[0]
    tm = min(tm, S)
    hb = tm // 16
    nlast = S // 16 - 1

    def body(da_ref, dan_ref, u_ref, p_ref, n_ref, w_ref, b_ref, o_ref, s_ref):
        i = pl.program_id(1)
        ni = pl.num_programs(1)

        @pl.when(i == 0)
        def _():
            s_ref[...] = jnp.zeros_like(s_ref)

        prev = jnp.where(i > 0, p_ref[...].astype(F32), 0.0)
        ext = jnp.concatenate([prev, u_ref[...].astype(F32), n_ref[...].astype(F32)], axis=0)
        u, e0, e1, e2 = _conv_taps(ext, w_ref, b_ref)
        u = u[16:]
        dan = jnp.where(i < ni - 1, dan_ref[...].astype(F32), 0.0)
        dae = jnp.concatenate([da_ref[...].astype(F32), dan], axis=0)
        val, gt = u[:, :CB], u[:, CB:]
        sg = _sigmoid(gt)
        du = jnp.concatenate([dae * (gt * sg), dae * val * (sg * (1.0 + gt * (1.0 - sg)))], axis=1)
        n = tm + 16
        du0 = w_ref[2:3, :] * du + w_ref[1:2, :] * pltpu.roll(du, n - 1, 0) + w_ref[0:1, :] * pltpu.roll(du, n - 2, 0)
        o_ref[...] = du0[:tm].astype(BF16)
        dut = du[:tm]
        s_ref[0:1, :] += jnp.sum(dut, axis=0, keepdims=True)
        s_ref[1:2, :] += jnp.sum(dut * e2[16:16 + tm], axis=0, keepdims=True)
        s_ref[2:3, :] += jnp.sum(dut * e1[16:16 + tm], axis=0, keepdims=True)
        s_ref[3:4, :] += jnp.sum(dut * e0[16:16 + tm], axis=0, keepdims=True)

    return pl.pallas_call(
        body, name="conv_glu_bwd", grid=(NCB, S // tm),
        in_specs=[pl.BlockSpec((tm, CB), lambda j, i: (i, j)),
                  pl.BlockSpec((16, CB), lambda j, i: (jnp.minimum((i + 1) * hb, nlast), j)),
                  pl.BlockSpec((tm, 2 * CB), lambda j, i: (i, j)),
                  pl.BlockSpec((16, 2 * CB), lambda j, i: (jnp.maximum(i * hb - 1, 0), j)),
                  pl.BlockSpec((16, 2 * CB), lambda j, i: (jnp.minimum((i + 1) * hb, nlast), j)),
                  pl.BlockSpec((3, 2 * CB), lambda j, i: (0, j)),
                  pl.BlockSpec((1, 2 * CB), lambda j, i: (0, j))],
        out_specs=[pl.BlockSpec((tm, 2 * CB), lambda j, i: (i, j)),
                   pl.BlockSpec((8, 2 * CB), lambda j, i: (0, j))],
        out_shape=[jax.ShapeDtypeStruct((S, 2 * D_FF), BF16), jax.ShapeDtypeStruct((8, 2 * D_FF), F32)],
        compiler_params=_cp(2),
    )(da, da, u0p, u0p, u0p, wc_p, bc_p)


def _final_loss(x1, y2, gate2, g_final, target, tm=256):
    S = x1.shape[0]
    tm = min(tm, S)

    def body(x1_ref, y2_ref, gt_ref, g_ref, t_ref, dx_ref, dy_ref, s_ref):
        @pl.when(pl.program_id(0) == 0)
        def _():
            s_ref[...] = jnp.zeros_like(s_ref)

        y2 = y2_ref[...]
        og = 1.0 + gt_ref[...]
        x2 = x1_ref[...] + og * y2
        r = lax.rsqrt(jnp.mean(x2 * x2, axis=-1, keepdims=True) + EPS)
        n = x2 * r
        g = g_ref[...]
        err = n * g - t_ref[...]
        dy = err * (1.0 / D)
        dn = dy * g
        dx2 = r * (dn - n * jnp.mean(dn * n, axis=-1, keepdims=True))
        dx_ref[...] = dx2
        dy_ref[...] = (dx2 * og).astype(BF16)
        s_ref[0:1, :] += jnp.sum(dy * n, axis=0, keepdims=True)
        s_ref[1:2, :] += jnp.sum(dx2 * y2, axis=0, keepdims=True)
        s_ref[2:3, :] += jnp.sum(err * err, axis=0, keepdims=True)

    return pl.pallas_call(
        body, name="final_loss", grid=(S // tm,),
        in_specs=[_rows(tm), _rows(tm), _vec(), _vec(), _rows(tm)],
        out_specs=[_rows(tm), _rows(tm), pl.BlockSpec((8, D), lambda i: (0, 0))],
        out_shape=[jax.ShapeDtypeStruct((S, D), F32), jax.ShapeDtypeStruct((S, D), BF16),
                   jax.ShapeDtypeStruct((8, D), F32)],
        compiler_params=_cp(1),
    )(x1, y2, gate2, g_final, target)


def _norm_mod_bwd(dh, xin, dres, g, scale, mixed, gate, name, tm=256):
    S = xin.shape[0]
    tm = min(tm, S)
    with_gate = mixed is not None

    def body(*refs):
        if with_gate:
            dh_ref, x_ref, dr_ref, g_ref, sc_ref, m_ref, gt_ref, dx_ref, dm_ref, s_ref = refs
        else:
            dh_ref, x_ref, dr_ref, g_ref, sc_ref, dx_ref, s_ref = refs

        @pl.when(pl.program_id(0) == 0)
        def _():
            s_ref[...] = jnp.zeros_like(s_ref)

        xv = x_ref[...]
        dhv = dh_ref[...]
        r = lax.rsqrt(jnp.mean(xv * xv, axis=-1, keepdims=True) + EPS)
        n = xv * r
        g = g_ref[...]
        hn = n * g
        dhn = dhv * (1.0 + sc_ref[...])
        dn = dhn * g
        dx = dr_ref[...] + r * (dn - n * jnp.mean(dn * n, axis=-1, keepdims=True))
        dx_ref[...] = dx
        s_ref[0:1, :] += jnp.sum(dhv, axis=0, keepdims=True)
        s_ref[1:2, :] += jnp.sum(dhv * hn, axis=0, keepdims=True)
        s_ref[2:3, :] += jnp.sum(dhn * n, axis=0, keepdims=True)
        if with_gate:
            dm_ref[...] = (dx * (1.0 + gt_ref[...])).astype(BF16)
            s_ref[3:4, :] += jnp.sum(dx * m_ref[...], axis=0, keepdims=True)

    ins = [dh, xin, dres, g, scale]
    in_specs = [_rows(tm), _rows(tm), _rows(tm), _vec(), _vec()]
    out_specs = [_rows(tm)]
    out_shape = [jax.ShapeDtypeStruct((S, D), F32)]
    if with_gate:
        ins += [mixed, gate]
        in_specs += [_rows(tm), _vec()]
        out_specs.append(_rows(tm))
        out_shape.append(jax.ShapeDtypeStruct((S, D), BF16))
    out_specs.append(pl.BlockSpec((8, D), lambda i: (0, 0)))
    out_shape.append(jax.ShapeDtypeStruct((8, D), F32))
    return pl.pallas_call(
        body, name=name, grid=(S // tm,), in_specs=in_specs, out_specs=out_specs, out_shape=out_shape,
        compiler_params=_cp(1),
    )(*ins)


def _tri(n, rel):
    row = lax.broadcasted_iota(jnp.int32, (n, n), 0)
    col = lax.broadcasted_iota(jnp.int32, (n, n), 1)
    m = {"gt": row > col, "ge": row >= col, "lt": row < col, "le": row <= col}[rel]
    return m


def _ones2(mask):
    u = jnp.where(mask, 1.0, 0.0).astype(BF16)
    return jnp.concatenate([u, u], axis=0)


def _sb_fwd(proj):
    S = proj.shape[0]
    nq = S // QB

    def body(q_ref, k_ref, v_ref, o_ref, t_ref):
        i = pl.program_id(1)
        causal = _tri(QB, "gt")
        u2 = _ones2(_tri(QB, "gt"))
        t_ref[...] = jnp.zeros_like(t_ref)
        for hh in range(2):
            sl = slice(hh * DK, (hh + 1) * DK)
            q = q_ref[:, sl] * 0.125

            def block(j, c, acc, diag):
                rows = pl.ds(pl.multiple_of(j * QB, QB), QB)
                kb = k_ref[rows, sl]
                vb = v_ref[rows, sl]
                z = _dot_nt(q, kb)
                l, p, _ = _log_sigmoid_parts(z)
                if diag:
                    l = jnp.where(causal, l, 0.0)
                lh, ll = _split(l)
                b = _dot(jnp.concatenate([lh, ll], axis=1), u2) + c
                w = jnp.exp(p + b)
                if diag:
                    w = jnp.where(causal, w, 0.0)
                acc = acc + _dot(w.astype(BF16), vb)
                c = c + jnp.sum(l, axis=1, keepdims=True)
                return c, acc

            c, acc = block(i, jnp.zeros((QB, 1), F32), jnp.zeros((QB, DK), F32), True)

            def cond(st):
                j, c, _ = st
                return jnp.logical_and(j >= 0, jnp.max(c) > SB_SKIP)

            def step(st):
                j, c, acc = st
                c, acc = block(j, c, acc, False)
                return j - 1, c, acc

            j, c, acc = lax.while_loop(cond, step, (i - 1, c, acc))
            o_ref[:, sl] = acc.astype(BF16)
            t_ref[0, :, hh:hh + 1] = c
            t_ref[0, :, 2 + hh:3 + hh] = jnp.zeros((QB, 1), F32) + (j + 1).astype(F32)

    return pl.pallas_call(
        body, name="sb_fwd", grid=(H_SB // 2, nq),
        in_specs=[pl.BlockSpec((QB, 128), lambda h, i: (i, h)),
                  pl.BlockSpec((S, 128), lambda h, i: (0, 4 + h)),
                  pl.BlockSpec((S, 128), lambda h, i: (0, 8 + h))],
        out_specs=[pl.BlockSpec((QB, 128), lambda h, i: (i, h)),
                   pl.BlockSpec((1, QB, 128), lambda h, i: (h, i, 0))],
        out_shape=[jax.ShapeDtypeStruct((S, 512), BF16), jax.ShapeDtypeStruct((H_SB // 2, S, 128), F32)],
        compiler_params=_cp(2),
    )(proj, proj, proj)


def _sb_bwd(proj, dcat, stats):
    S = proj.shape[0]
    nq = S // QB

    def body(q_ref, k_ref, v_ref, do_ref, t_ref, dq_ref, dk_ref, dv_ref, dk_acc, dv_acc):
        i = pl.program_id(1)

        @pl.when(i == 0)
        def _():
            dk_acc[...] = jnp.zeros_like(dk_acc)
            dv_acc[...] = jnp.zeros_like(dv_acc)

        causal = _tri(QB, "gt")
        uin2 = _ones2(_tri(QB, "le"))
        uex2 = _ones2(_tri(QB, "lt"))
        for hh in range(2):
            sl = slice(hh * DK, (hh + 1) * DK)
            q = q_ref[:, sl] * 0.125
            do = do_ref[:, sl]
            tot = t_ref[0, :, hh:hh + 1]
            jstart = jnp.max(t_ref[0, :, 2 + hh:3 + hh]).astype(jnp.int32)

            def block(j, pl_c, pg_c, dq, diag):
                rows = pl.ds(pl.multiple_of(j * QB, QB), QB)
                kb = k_ref[rows, sl]
                vb = v_ref[rows, sl]
                z = _dot_nt(q, kb)
                l, p, e = _log_sigmoid_parts(z)
                if diag:
                    l = jnp.where(causal, l, 0.0)
                lh, ll = _split(l)
                b = tot - (_dot(jnp.concatenate([lh, ll], axis=1), uin2) + pl_c)
                w = jnp.exp(p + b)
                if diag:
                    w = jnp.where(causal, w, 0.0)
                g = _dot_nt(do, vb) * w
                gh, gl = _split(g)
                gex = _dot(jnp.concatenate([gh, gl], axis=1), uex2) + pg_c
                r = 1.0 / (1.0 + e)
                er = e * r
                pos = z >= 0.0
                dz = g * jnp.where(pos, er, r) - gex * jnp.where(pos, r, er)
                if diag:
                    dz = jnp.where(causal, dz, 0.0)
                dzb = dz.astype(BF16)
                dq = dq + _dot(dzb, kb)
                dk_acc[rows, sl] += _dot_tn(dzb, q)
                dv_acc[rows, sl] += _dot_tn(w.astype(BF16), do)
                return pl_c + jnp.sum(l, axis=1, keepdims=True), pg_c + jnp.sum(g, axis=1, keepdims=True), dq

            def step(j, st):
                return block(j, *st, False)

            z1 = jnp.zeros((QB, 1), F32)
            st = lax.fori_loop(jstart, i, step, (z1, z1, jnp.zeros((QB, DK), F32)))
            _, _, dq = block(i, *st, True)
            dq_ref[:, sl] = (dq * 0.125).astype(BF16)

        @pl.when(i == nq - 1)
        def _():
            dk_ref[...] = dk_acc[...].astype(BF16)
            dv_ref[...] = dv_acc[...].astype(BF16)

    return pl.pallas_call(
        body, name="sb_bwd", grid=(H_SB // 2, nq),
        in_specs=[pl.BlockSpec((QB, 128), lambda h, i: (i, h)),
                  pl.BlockSpec((S, 128), lambda h, i: (0, 4 + h)),
                  pl.BlockSpec((S, 128), lambda h, i: (0, 8 + h)),
                  pl.BlockSpec((QB, 128), lambda h, i: (i, h)),
                  pl.BlockSpec((1, QB, 128), lambda h, i: (h, i, 0))],
        out_specs=[pl.BlockSpec((QB, 128), lambda h, i: (i, h)),
                   pl.BlockSpec((S, 128), lambda h, i: (0, h)),
                   pl.BlockSpec((S, 128), lambda h, i: (0, h))],
        out_shape=[jax.ShapeDtypeStruct((S, 512), BF16)] * 3,
        scratch_shapes=[pltpu.VMEM((S, 128), F32), pltpu.VMEM((S, 128), F32)],
        compiler_params=_cp(2, 48),
    )(proj, proj, proj, dcat, stats)


def _gla_gate(gf_ref, wfg_ref, bfg_ref):
    f = _dot(gf_ref[...], wfg_ref[...]) + bfg_ref[...]
    _, la, _ = _log_sigmoid_parts(f)
    la = la * (1.0 / 16.0)
    lah, lal = _split(la)
    lin = jnp.where(_tri(CHUNK, "ge"), 1.0, 0.0).astype(BF16)
    cum = _dot(jnp.concatenate([lin, lin], axis=1), jnp.concatenate([lah, lal], axis=0))
    total = cum[CHUNK - 1:CHUNK, :]
    return f, jnp.exp(total - cum), jnp.exp(total)


def _gla_specs(nc, rev):
    def ix(c):
        return nc - 1 - c if rev else c
    return [pl.BlockSpec((CHUNK, 256), lambda c: (ix(c), 6)),
            pl.BlockSpec((CHUNK, 256), lambda c: (ix(c), 7)),
            pl.BlockSpec((CHUNK, 512), lambda c: (ix(c), 4)),
            pl.BlockSpec((CHUNK, 512), lambda c: (ix(c), 5)),
            pl.BlockSpec((CHUNK, 128), lambda c: (ix(c), 24))]


def _gla_fwd(proj, wfg_p, bfg, ggla):
    S = proj.shape[0]
    nc = S // CHUNK

    def body(q_ref, k_ref, v_ref, gg_ref, gf_ref, wfg_ref, bfg_ref, ggla_ref, o_ref, st_ref, state):
        @pl.when(pl.program_id(0) == 0)
        def _():
            state[...] = jnp.zeros_like(state)

        _, e, dec = _gla_gate(gf_ref, wfg_ref, bfg_ref)
        kdec = (k_ref[...].astype(F32) * e).astype(BF16)
        for h in range(H_GLA):
            ks = slice(h * DK, (h + 1) * DK)
            vs = slice(h * DV, (h + 1) * DV)
            new = state[:, ks] * dec[:, ks] + _dot_tn(v_ref[:, vs], kdec[:, ks])
            state[:, ks] = new
            o = _dot_nt(q_ref[:, ks] * 0.125, new.astype(BF16))
            ohn = o * lax.rsqrt(jnp.mean(o * o, axis=-1, keepdims=True) + EPS)
            gg = gg_ref[:, vs].astype(F32)
            o_ref[:, vs] = ((ohn * ggla_ref[:, vs]) * (gg * _sigmoid(gg))).astype(BF16)
        st_ref[0] = state[...]

    return pl.pallas_call(
        body, name="gla_fwd", grid=(nc,),
        in_specs=_gla_specs(nc, False) + [pl.BlockSpec((128, 256), lambda c: (0, 0)),
                                          pl.BlockSpec((1, 256), lambda c: (0, 0)),
                                          pl.BlockSpec((1, 512), lambda c: (0, 0))],
        out_specs=[pl.BlockSpec((CHUNK, 512), lambda c: (c, 0)),
                   pl.BlockSpec((1, 128, 256), lambda c: (c, 0, 0))],
        out_shape=[jax.ShapeDtypeStruct((S, 512), BF16), jax.ShapeDtypeStruct((nc, 128, 256), F32)],
        scratch_shapes=[pltpu.VMEM((128, 256), F32)],
        compiler_params=_cp(1),
    )(proj, proj, proj, proj, proj, wfg_p, bfg, ggla)


def _gla_bwd(dcat, proj, states, wfg_p, bfg, ggla):
    S = proj.shape[0]
    nc = S // CHUNK

    def body(do_ref, q_ref, k_ref, v_ref, gg_ref, gf_ref, sc_ref, sp_ref, wfg_ref, bfg_ref, ggla_ref,
             dp_ref, s_ref, dw_ref, carry):
        cr = pl.program_id(0)

        @pl.when(cr == 0)
        def _():
            carry[...] = jnp.zeros_like(carry)
            s_ref[...] = jnp.zeros_like(s_ref)
            dw_ref[...] = jnp.zeros_like(dw_ref)

        f, e, dec = _gla_gate(gf_ref, wfg_ref, bfg_ref)
        kf = k_ref[...].astype(F32) * e
        kdec = kf.astype(BF16)
        s_cur = sc_ref[0]
        s_prev = jnp.where(cr < nc - 1, sp_ref[0], 0.0)
        dkdec, ddec = [], []
        for h in range(H_GLA):
            ks = slice(h * DK, (h + 1) * DK)
            vs = slice(h * DV, (h + 1) * DV)
            qs = q_ref[:, ks] * 0.125
            sb = s_cur[:, ks].astype(BF16)
            o = _dot_nt(qs, sb)
            rr = lax.rsqrt(jnp.mean(o * o, axis=-1, keepdims=True) + EPS)
            ohn = o * rr
            gg = gg_ref[:, vs].astype(F32)
            sg = _sigmoid(gg)
            dout = do_ref[:, vs].astype(F32)
            gl = ggla_ref[:, vs]
            dp_ref[:, 1024 + h * DV:1024 + (h + 1) * DV] = (
                dout * (ohn * gl) * (sg * (1.0 + gg * (1.0 - sg)))).astype(BF16)
            dt1 = dout * (gg * sg)
            s_ref[0:1, vs] += jnp.sum(dt1 * ohn, axis=0, keepdims=True)
            dohn = dt1 * gl
            dob = (rr * (dohn - ohn * jnp.mean(dohn * ohn, axis=-1, keepdims=True))).astype(BF16)
            dp_ref[:, ks] = (_dot(dob, sb) * 0.125).astype(BF16)
            gt = _dot_tn(dob, qs) + carry[:, ks]
            gtb = gt.astype(BF16)
            dkdec.append(_dot(v_ref[:, vs], gtb))
            dp_ref[:, 512 + h * DV:512 + (h + 1) * DV] = _dot_nt(kdec[:, ks], gtb).astype(BF16)
            ddec.append(jnp.sum(gt * s_prev[:, ks], axis=0, keepdims=True))
            carry[:, ks] = gt * dec[:, ks]
        dkd = jnp.concatenate(dkdec, axis=1)
        dp_ref[:, 256:512] = (dkd * e).astype(BF16)
        wh, wl = _split(dkd * kf)
        lex = jnp.where(_tri(CHUNK, "gt"), 1.0, 0.0).astype(BF16)
        dla = _dot(jnp.concatenate([lex, lex], axis=1), jnp.concatenate([wh, wl], axis=0))
        dla = dla + jnp.concatenate(ddec, axis=1) * dec
        df = dla * _sigmoid(-f) * (1.0 / 16.0)
        dfb = df.astype(BF16)
        s_ref[1:2, 0:256] += jnp.sum(df, axis=0, keepdims=True)
        dw_ref[...] += _dot_tn(gf_ref[...], dfb)
        dp_ref[:, 1536:1664] = _dot_nt(dfb, wfg_ref[...]).astype(BF16)

    return pl.pallas_call(
        body, name="gla_bwd", grid=(nc,),
        in_specs=[pl.BlockSpec((CHUNK, 512), lambda c: (nc - 1 - c, 1))] + _gla_specs(nc, True) + [
            pl.BlockSpec((1, 128, 256), lambda c: (nc - 1 - c, 0, 0)),
            pl.BlockSpec((1, 128, 256), lambda c: (jnp.maximum(nc - 2 - c, 0), 0, 0)),
            pl.BlockSpec((128, 256), lambda c: (0, 0)),
            pl.BlockSpec((1, 256), lambda c: (0, 0)),
            pl.BlockSpec((1, 512), lambda c: (0, 0))],
        out_specs=[pl.BlockSpec((CHUNK, 1664), lambda c: (nc - 1 - c, 0)),
                   pl.BlockSpec((8, 512), lambda c: (0, 0)),
                   pl.BlockSpec((128, 256), lambda c: (0, 0))],
        out_shape=[jax.ShapeDtypeStruct((S, 1664), BF16), jax.ShapeDtypeStruct((8, 512), F32),
                   jax.ShapeDtypeStruct((128, 256), F32)],
        scratch_shapes=[pltpu.VMEM((128, 256), F32)],
        compiler_params=_cp(1),
    )(dcat, proj, proj, proj, proj, proj, states, states, wfg_p, bfg, ggla)


def _sum_leading(a, name):
    n = a.shape[0]

    def body(a_ref, o_ref):
        acc = a_ref[0]
        for k in range(1, n):
            acc = acc + a_ref[k]
        o_ref[...] = acc

    return pl.pallas_call(
        body, name=name, out_shape=jax.ShapeDtypeStruct(a.shape[1:], F32),
        in_specs=[VMEM_SPEC], out_specs=VMEM_SPEC,
    )(a)


def _sum_chip(own, recv, name):
    R, C = own.shape
    tr = _row_tile(R, C)

    def body(o_ref, r_ref, p_ref):
        acc = o_ref[...]
        for k in range(3):
            acc = acc + r_ref[k].astype(F32)
        p_ref[...] = acc

    return pl.pallas_call(
        body, name=name, grid=(R // tr,),
        in_specs=[pl.BlockSpec((tr, C), lambda i: (i, 0)), pl.BlockSpec((3, tr, C), lambda i: (0, i, 0))],
        out_specs=pl.BlockSpec((tr, C), lambda i: (i, 0)),
        out_shape=jax.ShapeDtypeStruct((R, C), F32), compiler_params=_cp(1),
    )(own, recv)


def _adamw(w, p, q, m, v, name):
    R, C = w.shape
    tr = _row_tile(R, C, 256 * 1024)
    two = q is not None

    def body(*refs):
        if two:
            w_ref, p_ref, q_ref, m_ref, v_ref, g_out, d_out, m_out, v_out = refs
            g = p_ref[...] + q_ref[...]
        else:
            w_ref, p_ref, m_ref, v_ref, g_out, d_out, m_out, v_out = refs
            g = p_ref[...]
        m2 = B1 * m_ref[...] + (1.0 - B1) * g
        v2 = B2 * v_ref[...] + (1.0 - B2) * (g * g)
        m_hat = m2 / (1.0 - B1 ** STEP)
        v_hat = v2 / (1.0 - B2 ** STEP)
        g_out[...] = g
        d_out[...] = -LR * (m_hat / (jnp.sqrt(v_hat) + EPS_A) + WD * w_ref[...])
        m_out[...] = m2
        v_out[...] = v2

    spec = pl.BlockSpec((tr, C), lambda i: (i, 0))
    ins = [w, p, q, m, v] if two else [w, p, m, v]
    return pl.pallas_call(
        body, name=name, grid=(R // tr,),
        in_specs=[spec] * len(ins), out_specs=[spec] * 4,
        out_shape=[jax.ShapeDtypeStruct((R, C), F32)] * 4, compiler_params=_cp(1),
    )(*ins)


def _reduce_big(full_chip_major, cidx, name):
    recv = _scatter4(full_chip_major.astype(BF16), name + "_scatter")
    own = lax.dynamic_index_in_dim(full_chip_major, cidx, axis=0, keepdims=False)
    p = _sum_chip(own, recv, name + "_sum")
    return p, _pair_swap(p, name + "_swap")


def _interleave_cols(a):
    lead = a.shape[:-1]
    return a.reshape(lead + (2, NCB, CB)).swapaxes(-3, -2).reshape(lead + (2 * D_FF,))


def _deinterleave_cols(a):
    lead = a.shape[:-1]
    return a.reshape(lead + (NCB, 2, CB)).swapaxes(-3, -2).reshape(lead + (2 * D_FF,))


def _cols_to_chips(a, width):
    return a.reshape(a.shape[0], 4, width).swapaxes(0, 1)


def _chips_to_cols(a):
    return a.swapaxes(0, 1).reshape(a.shape[1], 4 * a.shape[2])


def kernel(x, c, w_ada, b_ada, g_norm1, w_in, w_fg2, b_fg2, g_gla_out, w_out, g_norm2, w_up, w_conv, b_conv, w_down, g_final, loss_target, m_w_ada, m_b_ada, m_g_norm1, m_w_in, m_w_fg2, m_b_fg2, m_g_gla_out, m_w_out, m_g_norm2, m_w_up, m_w_conv, m_b_conv, m_w_down, m_g_final, v_w_ada, v_b_ada, v_g_norm1, v_w_in, v_w_fg2, v_b_fg2, v_g_gla_out, v_w_out, v_g_norm2, v_w_up, v_w_conv, v_b_conv, v_w_down, v_g_final):
    xi, yi, ci = lax.axis_index("x"), lax.axis_index("y"), lax.axis_index("c")
    cidx = 2 * xi + yi
    didx = 4 * xi + 2 * yi + ci
    xs = x[0]
    tgt = loss_target[0]
    gfin = g_final.reshape(1, D)
    AW = D * 6 // 4
    UW = 2 * D_FF // 4

    c_all = _allgather8(c, "gather_c").reshape(8, D)
    c_pad = jnp.concatenate([c_all, jnp.zeros((8, D), F32)], axis=0)
    mod_part = _ada_fwd(c_pad, w_ada[0], lax.dynamic_slice(b_ada, (0, cidx * AW), (1, AW)))[:8]
    small = jnp.concatenate([mod_part.reshape(-1), w_conv.reshape(-1), w_fg2.reshape(-1)]).reshape(-1, 128)
    small_g = _allgather4(small, "gather_small").reshape(4, -1)
    mod = lax.dynamic_index_in_dim(small_g[:, :8 * AW].reshape(4, 8, AW), didx, axis=1, keepdims=False).reshape(1, 6 * D)
    shift1, scale1, gate1, shift2, scale2, gate2 = [mod[:, k * D:(k + 1) * D] for k in range(6)]
    o1 = 8 * AW
    o2 = o1 + 3 * UW
    wconv_full = _chips_to_cols(small_g[:, o1:o2].reshape(4, 3, UW))
    wfg_full = _chips_to_cols(small_g[:, o2:].reshape(4, RANK, 64))
    wc_p = _interleave_cols(wconv_full)
    bc_p = _interleave_cols(b_conv)
    wfg_p = jnp.concatenate([wfg_full, jnp.zeros((128 - RANK, 256), F32)], axis=0).astype(BF16)

    w_in_f = _chips_to_cols(_allgather4(w_in[0].astype(BF16), "gather_w_in"))
    w_in_p = jnp.concatenate([w_in_f, jnp.zeros((D, N_IN_P - N_IN), BF16)], axis=1)
    w_out_f = _allgather4(w_out[0].astype(BF16), "gather_w_out").reshape(D, D)
    w_up_p = _interleave_cols(_chips_to_cols(_allgather4(w_up[0].astype(BF16), "gather_w_up")))
    w_down_f = _allgather4(w_down[0].astype(BF16), "gather_w_down").reshape(D_FF, D)

    h = _norm_mod(xs, g_norm1, shift1, scale1)
    proj = _mm(h, w_in_p, BF16, "mm_in", 256, N_IN_P, 48)
    o_sb, stats = _sb_fwd(proj)
    o_gla, states = _gla_fwd(proj, wfg_p, b_fg2, g_gla_out)
    cat = jnp.concatenate([o_sb, o_gla], axis=1)
    mixed = _mm(cat, w_out_f, F32, "mm_out", 512, D)
    x1, h2 = _resid_norm_mod(xs, mixed, gate1, g_norm2, shift2, scale2)
    u0p = _mm(h2, w_up_p, BF16, "mm_up", 512, UW, 48)
    a = _conv_glu(u0p, wc_p, bc_p)
    y2 = _mm(a, w_down_f, F32, "mm_down", 512, D, 48)
    dx2, dy2, s_fin = _final_loss(x1, y2, gate2, gfin, tgt)
    loss = lax.psum(0.5 / D * jnp.sum(s_fin[2]), ("x", "y", "c"))

    da = _mm(dy2, w_down_f.T, BF16, "mm_down_t", 512, D_FF, 48)
    dw_down = _mm_tn(a, dy2, "mm_dw_down", D, 512, 56)
    du0p, s_conv = _conv_glu_bwd(da, u0p, wc_p, bc_p)
    dh2 = _mm(du0p, w_up_p.T, F32, "mm_up_t", 256, D, 56)
    dw_up_p = _mm_tn(h2, du0p, "mm_dw_up", UW, 512, 48)
    dx1, dmixed, s_n2 = _norm_mod_bwd(dh2, x1, dx2, g_norm2, scale2, mixed, gate1, "norm2_bwd")
    dcat = _mm(dmixed, w_out_f.T, BF16, "mm_out_t", 512, D)
    dw_out = _mm_tn(cat, dmixed, "mm_dw_out", D, 512)
    dq, dk, dv = _sb_bwd(proj, dcat, stats)
    dp_gla, s_gla, dwfg = _gla_bwd(dcat, proj, states, wfg_p, b_fg2, g_gla_out)
    dproj = jnp.concatenate([dq, dk, dv, dp_gla], axis=1)
    dh = _mm(dproj, w_in_p.T, F32, "mm_in_t", 256, D, 48)
    dw_in_p = _mm_tn(h, dproj, "mm_dw_in", 640, 512, 48)
    gx, s_n1 = _norm_mod_bwd(dh, xs, dx1, g_norm1, scale1, None, None, "norm1_bwd")

    dmod = jnp.concatenate([s_n1[0], s_n1[1], s_n2[3], s_n2[0], s_n2[1], s_fin[1]])
    s_conv_n = _deinterleave_cols(s_conv[:4])
    part = jnp.concatenate([dmod, s_n1[2], s_n2[2], s_fin[0], s_gla[0], s_gla[1, :256], s_conv_n[0],
                            s_conv_n[1:4].reshape(-1), dwfg[:RANK].reshape(-1)]).reshape(-1, 128)
    parts = _allgather8(part, "gather_small_grads")
    tot = _sum_leading(parts, "sum_small_grads").reshape(-1)
    dmod_all = parts.reshape(8, -1)[:, :6 * D]
    offs = [0]
    for n in (6 * D, D, D, D, 512, 256, 2 * D_FF, 3 * 2 * D_FF, RANK * 256):
        offs.append(offs[-1] + n)
    g_b_ada, g_g1, g_g2, g_gf, g_ggla, g_bfg, g_bconv, g_wconv_full, g_wfg_full = [
        tot[offs[k]:offs[k + 1]] for k in range(9)]
    g_wconv = lax.dynamic_index_in_dim(_cols_to_chips(g_wconv_full.reshape(3, 2 * D_FF), UW), cidx, 0, keepdims=False)
    g_wfg = lax.dynamic_index_in_dim(_cols_to_chips(g_wfg_full.reshape(RANK, 256), 64), cidx, 0, keepdims=False)

    dmod_pad = jnp.concatenate([dmod_all, jnp.zeros((8, 6 * D), F32)], axis=0)
    g_w_ada = _ada_bwd(c_pad, lax.dynamic_slice(dmod_pad, (0, cidx * AW), (16, AW)))

    p_in, q_in = _reduce_big(_cols_to_chips(dw_in_p[:, :N_IN], N_IN // 4), cidx, "rs_w_in")
    p_out, q_out = _reduce_big(dw_out.reshape(4, D // 4, D), cidx, "rs_w_out")
    p_up, q_up = _reduce_big(_cols_to_chips(_deinterleave_cols(dw_up_p), UW), cidx, "rs_w_up")
    p_down, q_down = _reduce_big(dw_down.reshape(4, D_FF // 4, D), cidx, "rs_w_down")

    out = {}
    out["w_ada"] = _adamw(w_ada[0], g_w_ada, None, m_w_ada[0], v_w_ada[0], "adamw_w_ada")
    out["w_in"] = _adamw(w_in[0], p_in, q_in, m_w_in[0], v_w_in[0], "adamw_w_in")
    out["w_out"] = _adamw(w_out[0], p_out, q_out, m_w_out[0], v_w_out[0], "adamw_w_out")
    out["w_up"] = _adamw(w_up[0], p_up, q_up, m_w_up[0], v_w_up[0], "adamw_w_up")
    out["w_down"] = _adamw(w_down[0], p_down, q_down, m_w_down[0], v_w_down[0], "adamw_w_down")
    small_names = ["b_ada", "g_norm1", "w_fg2", "b_fg2", "g_gla_out", "g_norm2", "w_conv", "b_conv", "g_final"]
    small_w = [b_ada, g_norm1, w_fg2, b_fg2, g_gla_out, g_norm2, w_conv, b_conv, g_final]
    small_m = [m_b_ada, m_g_norm1, m_w_fg2, m_b_fg2, m_g_gla_out, m_g_norm2, m_w_conv, m_b_conv, m_g_final]
    small_v = [v_b_ada, v_g_norm1, v_w_fg2, v_b_fg2, v_g_gla_out, v_g_norm2, v_w_conv, v_b_conv, v_g_final]
    small_gr = [g_b_ada, g_g1, g_wfg, g_bfg, g_ggla, g_g2, g_wconv, g_bconv, g_gf]

    def pack(arrs):
        flat = jnp.concatenate([t.reshape(-1) for t in arrs])
        return jnp.concatenate([flat, jnp.zeros((-flat.shape[0]) % 1024, F32)]).reshape(-1, 128)

    packed = _adamw(pack(small_w), pack(small_gr), None, pack(small_m), pack(small_v), "adamw_small")
    off = 0
    for nm, wt in zip(small_names, small_w):
        n = wt.size
        out[nm] = [t.reshape(-1)[off:off + n].reshape(wt.shape) for t in packed]
        off += n
    for nm in ("w_ada", "w_in", "w_out", "w_up", "w_down"):
        out[nm] = [t[None] for t in out[nm]]

    names = ["w_ada", "b_ada", "g_norm1", "w_in", "w_fg2", "b_fg2", "g_gla_out", "w_out", "g_norm2", "w_up",
             "w_conv", "b_conv", "w_down", "g_final"]
    res = [loss, gx[None]]
    for k in range(4):
        res += [out[nm][k] for nm in names]
    return tuple(res)
```

```python
import functools

import jax
import jax.numpy as jnp
from jax import lax
from jax.experimental import pallas as pl
from jax.experimental.pallas import tpu as pltpu

F32 = jnp.float32
BF16 = jnp.bfloat16
MESH = pl.DeviceIdType.MESH

D = 1024
H_SB = 8
DK = 64
DV = 128
H_GLA = 4
CHUNK = 64
RANK = 16
N_IN = 3088
N_IN_P = 3200
D_FF = 2816
HB = D_FF // 2
LANES = 128
EPS = 1e-6
QB = 128
SB_SKIP = -120.0

LR, B1, B2, EPS_A, WD, STEP = 0.001, 0.9, 0.999, 1e-08, 0.01, 10

ANY = pl.BlockSpec(memory_space=pl.ANY)
VMEM_SPEC = pl.BlockSpec(memory_space=pltpu.VMEM)
ONE_BUF = pl.Buffered(1)


def _cp(ndim=0, vmem_mb=None):
    kw = {}
    if ndim:
        kw["dimension_semantics"] = ("arbitrary",) * ndim
    if vmem_mb:
        kw["vmem_limit_bytes"] = vmem_mb * 1024 * 1024
    return pltpu.CompilerParams(**kw)


def _dot(a, b):
    return jnp.dot(a, b, preferred_element_type=F32)


def _dot_nt(a, b):
    return lax.dot_general(a, b, (((1,), (1,)), ((), ())), preferred_element_type=F32)


def _dot_tn(a, b):
    return lax.dot_general(a, b, (((0,), (0,)), ((), ())), preferred_element_type=F32)


def _split(x):
    hi = x.astype(BF16)
    lo = (x - hi.astype(F32)).astype(BF16)
    return hi, lo


def _sigmoid(x):
    return jax.nn.sigmoid(x)


def _log_sigmoid_parts(z):
    e = jnp.exp(-jnp.abs(z))
    sp = jnp.log1p(e)
    return -(jnp.maximum(z, 0.0) + sp), jnp.minimum(z, 0.0) - sp, e


def _row_tile(rows, cols, budget=512 * 1024):
    best = None
    for t in range(8, rows + 1, 8):
        if rows % t == 0 and t * cols * 4 <= budget:
            best = t
    return best if best is not None else rows


def _flip(v, bit):
    return 1 - v if bit else v


def _allgather8(a, name):
    def body(a_ref, o_ref, ssem, rsem, lsem):
        x, y, c = lax.axis_index("x"), lax.axis_index("y"), lax.axis_index("c")
        me = 4 * x + 2 * y + c
        loc = pltpu.make_async_copy(a_ref, o_ref.at[me], lsem)
        loc.start()
        sends = []
        for r in range(1, 8):
            peer = (_flip(x, r & 4), _flip(y, r & 2), _flip(c, r & 1))
            cp = pltpu.make_async_remote_copy(
                src_ref=a_ref, dst_ref=o_ref.at[me], send_sem=ssem.at[r - 1], recv_sem=rsem.at[r - 1],
                device_id=peer, device_id_type=MESH)
            cp.start()
            sends.append(cp)
        for r in range(1, 8):
            peer = (_flip(x, r & 4), _flip(y, r & 2), _flip(c, r & 1))
            pidx = 4 * peer[0] + 2 * peer[1] + peer[2]
            pltpu.make_async_remote_copy(
                src_ref=a_ref, dst_ref=o_ref.at[pidx], send_sem=ssem.at[r - 1], recv_sem=rsem.at[r - 1],
                device_id=peer, device_id_type=MESH).wait_recv()
        for cp in sends:
            cp.wait_send()
        loc.wait()

    return pl.pallas_call(
        body, name=name,
        out_shape=jax.ShapeDtypeStruct((8,) + a.shape, a.dtype),
        in_specs=[VMEM_SPEC], out_specs=VMEM_SPEC,
        scratch_shapes=[pltpu.SemaphoreType.DMA((7,)), pltpu.SemaphoreType.DMA((7,)), pltpu.SemaphoreType.DMA],
    )(a)


def _allgather4(a, name):
    def body(a_ref, o_ref, ssem, rsem, lsem):
        x, y, c = lax.axis_index("x"), lax.axis_index("y"), lax.axis_index("c")
        me = 2 * x + y
        loc = pltpu.make_async_copy(a_ref, o_ref.at[me], lsem)
        loc.start()
        sends = []
        for r in range(1, 4):
            peer = (_flip(x, r & 2), _flip(y, r & 1), c)
            cp = pltpu.make_async_remote_copy(
                src_ref=a_ref, dst_ref=o_ref.at[me], send_sem=ssem.at[r - 1], recv_sem=rsem.at[r - 1],
                device_id=peer, device_id_type=MESH)
            cp.start()
            sends.append(cp)
        for r in range(1, 4):
            peer = (_flip(x, r & 2), _flip(y, r & 1), c)
            pidx = 2 * peer[0] + peer[1]
            pltpu.make_async_remote_copy(
                src_ref=a_ref, dst_ref=o_ref.at[pidx], send_sem=ssem.at[r - 1], recv_sem=rsem.at[r - 1],
                device_id=peer, device_id_type=MESH).wait_recv()
        for cp in sends:
            cp.wait_send()
        loc.wait()

    return pl.pallas_call(
        body, name=name,
        out_shape=jax.ShapeDtypeStruct((4,) + a.shape, a.dtype),
        in_specs=[ANY], out_specs=ANY,
        scratch_shapes=[pltpu.SemaphoreType.DMA((3,)), pltpu.SemaphoreType.DMA((3,)), pltpu.SemaphoreType.DMA],
    )(a)


def _slot(chip, swapped):
    return 2 * (chip % 2) + chip // 2 if swapped else chip


def _pair_swap(p, name):
    def body(p_ref, o_ref, ssem, rsem):
        x, y, c = lax.axis_index("x"), lax.axis_index("y"), lax.axis_index("c")
        cp = pltpu.make_async_remote_copy(
            src_ref=p_ref, dst_ref=o_ref, send_sem=ssem, recv_sem=rsem,
            device_id=(x, y, 1 - c), device_id_type=MESH)
        cp.start()
        cp.wait()

    return pl.pallas_call(
        body, name=name,
        out_shape=jax.ShapeDtypeStruct(p.shape, p.dtype),
        in_specs=[ANY], out_specs=ANY,
        scratch_shapes=[pltpu.SemaphoreType.DMA, pltpu.SemaphoreType.DMA],
    )(p)


def _carried_copies(kind, src_ref, dst_ref, sems, swapped):
    ssem, rsem, lsem = sems
    x, y, c = lax.axis_index("x"), lax.axis_index("y"), lax.axis_index("c")
    me = 2 * x + y
    starts, recvs = [], []
    if kind == "gather":
        starts.append(pltpu.make_async_copy(src_ref, dst_ref.at[me], lsem))
    for r in range(1, 4):
        peer = (_flip(x, r & 2), _flip(y, r & 1), c)
        pidx = 2 * peer[0] + peer[1]
        if kind == "gather":
            src, dst, landed = src_ref, dst_ref.at[me], dst_ref.at[pidx]
        else:
            src = src_ref.at[2 * peer[1] + peer[0] if swapped else pidx]
            dst = landed = dst_ref.at[r - 1]
        starts.append(pltpu.make_async_remote_copy(
            src_ref=src, dst_ref=dst, send_sem=ssem.at[r - 1], recv_sem=rsem.at[r - 1],
            device_id=peer, device_id_type=MESH))
        recvs.append(pltpu.make_async_remote_copy(
            src_ref=src, dst_ref=landed, send_sem=ssem.at[r - 1], recv_sem=rsem.at[r - 1],
            device_id=peer, device_id_type=MESH))
    return starts, recvs


def _call(body, carried, operands, *, name, grid, in_specs, out_specs, out_shape, scratch_shapes=(),
          compiler_params=None):
    single = not isinstance(out_shape, (list, tuple))
    out_specs = [out_specs] if single else list(out_specs)
    out_shape = [out_shape] if single else list(out_shape)
    n_in, n_out, n_sc, nh = len(operands), len(out_shape), len(scratch_shapes), len(carried)

    def full(*refs):
        ins, h_in = refs[:n_in], refs[n_in:n_in + nh]
        o0 = n_in + nh
        outs, h_out = refs[o0:o0 + n_out], refs[o0 + n_out:o0 + n_out + nh]
        s0 = o0 + n_out + nh
        scratch, sems = refs[s0:s0 + n_sc], refs[s0 + n_sc:]
        first = last = None
        for d in range(len(grid)):
            f = pl.program_id(d) == 0
            l = pl.program_id(d) == pl.num_programs(d) - 1
            first = f if first is None else jnp.logical_and(first, f)
            last = l if last is None else jnp.logical_and(last, l)

        def copies(t):
            return _carried_copies(carried[t][0], h_in[t], h_out[t], sems[3 * t:3 * t + 3], carried[t][2])

        if nh:
            @pl.when(first)
            def _():
                for t in range(nh):
                    for cp in copies(t)[0]:
                        cp.start()

        body(*ins, *outs, *scratch)

        if nh:
            @pl.when(last)
            def _():
                for t in range(nh):
                    starts, recvs = copies(t)
                    for cp in recvs:
                        cp.wait_recv()
                    for cp in starts:
                        if carried[t][0] == "gather" and cp is starts[0]:
                            cp.wait()
                        else:
                            cp.wait_send()

    h_shapes = [jax.ShapeDtypeStruct(((4,) + arr.shape) if kind == "gather" else ((3,) + arr.shape[1:]), arr.dtype)
                for kind, arr, _ in carried]
    sem_shapes = [pltpu.SemaphoreType.DMA((3,)), pltpu.SemaphoreType.DMA((3,)), pltpu.SemaphoreType.DMA] * nh
    res = pl.pallas_call(
        full, name=name, grid=grid, in_specs=list(in_specs) + [ANY] * nh, out_specs=out_specs + [ANY] * nh,
        out_shape=out_shape + h_shapes, scratch_shapes=list(scratch_shapes) + sem_shapes,
        compiler_params=compiler_params,
    )(*operands, *[arr for _, arr, _ in carried])
    main = res[:n_out]
    return (main[0] if single else main), list(res[n_out:])


def _mm(pairs, out_dtype, name, tm, tn, vmem_mb=None, carried=()):
    S = pairs[0][0].shape[0]
    N = pairs[0][1].shape[1]
    tm = min(tm, S)
    np_ = len(pairs)

    def body(*refs):
        acc = _dot(refs[0][...], refs[1][...])
        for t in range(1, np_):
            acc = acc + _dot(refs[2 * t][...], refs[2 * t + 1][...])
        refs[-1][...] = acc.astype(refs[-1].dtype)

    in_specs, ops = [], []
    for a, w in pairs:
        in_specs += [pl.BlockSpec((tm, a.shape[1]), lambda n, i: (i, 0)),
                     pl.BlockSpec((w.shape[0], tn), lambda n, i: (0, n))]
        ops += [a, w]
    out, got = _call(
        body, carried, ops, name=name, grid=(N // tn, S // tm), in_specs=in_specs,
        out_specs=pl.BlockSpec((tm, tn), lambda n, i: (i, n)),
        out_shape=jax.ShapeDtypeStruct((S, N), out_dtype),
        compiler_params=_cp(2, vmem_mb))
    return (out, got) if carried else out


def _mm_tn(a_list, b, name, bn, tk, vmem_mb=None):
    S, N = b.shape
    ms = [a.shape[1] for a in a_list]
    M = sum(ms)
    tk = min(tk, S)
    na = len(a_list)

    def body(*refs):
        b_ref, o_ref = refs[na], refs[na + 1]

        @pl.when(pl.program_id(1) == 0)
        def _():
            o_ref[...] = jnp.zeros_like(o_ref)
        off = 0
        for t in range(na):
            o_ref[off:off + ms[t], :] += _dot_tn(refs[t][...], b_ref[...])
            off += ms[t]

    return pl.pallas_call(
        body, name=name, grid=(N // bn, S // tk),
        in_specs=[pl.BlockSpec((tk, m), lambda n, k: (k, 0)) for m in ms] + [pl.BlockSpec((tk, bn), lambda n, k: (k, n))],
        out_specs=pl.BlockSpec((None, M, bn), lambda n, k: (n, 0, 0)),
        out_shape=jax.ShapeDtypeStruct((N // bn, M, bn), F32),
        compiler_params=_cp(2, vmem_mb),
    )(*a_list, b)


def _mm_tn_multi(a, b_list, name, tk, vmem_mb=None):
    S, M = a.shape
    tk = min(tk, S)
    nb = len(b_list)

    def body(*refs):
        a_ref = refs[0]

        @pl.when(pl.program_id(0) == 0)
        def _():
            for t in range(nb):
                refs[1 + nb + t][...] = jnp.zeros_like(refs[1 + nb + t])
        av = a_ref[...]
        for t in range(nb):
            refs[1 + nb + t][...] += _dot_tn(av, refs[1 + t][...])

    return pl.pallas_call(
        body, name=name, grid=(S // tk,),
        in_specs=[pl.BlockSpec((tk, M), lambda k: (k, 0))] + [pl.BlockSpec((tk, b.shape[1]), lambda k: (k, 0)) for b in b_list],
        out_specs=[pl.BlockSpec((M, b.shape[1]), lambda k: (0, 0)) for b in b_list],
        out_shape=[jax.ShapeDtypeStruct((M, b.shape[1]), F32) for b in b_list],
        compiler_params=_cp(1, vmem_mb),
    )(a, *b_list)


def _ada_fwd(c_all, w_sh, b_sh):
    def body(c_ref, w_ref, b_ref, o_ref):
        cv = c_ref[...]
        sc = (cv * _sigmoid(cv)).astype(BF16)
        o_ref[...] = _dot(sc, w_ref[...].astype(BF16)) + b_ref[...]

    return pl.pallas_call(
        body, name="ada_fwd", out_shape=jax.ShapeDtypeStruct((c_all.shape[0], w_sh.shape[1]), F32),
        in_specs=[VMEM_SPEC] * 3, out_specs=VMEM_SPEC, compiler_params=_cp(0, 40),
    )(c_all, w_sh, b_sh)


def _ada_bwd(c_all, dmod_sh):
    def body(c_ref, d_ref, o_ref):
        cv = c_ref[...]
        sc = (cv * _sigmoid(cv)).astype(BF16)
        o_ref[...] = _dot_tn(sc, d_ref[...].astype(BF16))

    return pl.pallas_call(
        body, name="ada_bwd", out_shape=jax.ShapeDtypeStruct((c_all.shape[1], dmod_sh.shape[1]), F32),
        in_specs=[VMEM_SPEC] * 2, out_specs=VMEM_SPEC, compiler_params=_cp(0, 40),
    )(c_all, dmod_sh)


def _vec(tm_unused=None):
    return pl.BlockSpec((1, D), lambda i: (0, 0))


def _rows(tm, width=D):
    return pl.BlockSpec((tm, width), lambda i: (i, 0))


def _norm_mod(x, g, shift, scale, tm=512):
    S = x.shape[0]
    tm = min(tm, S)

    def body(x_ref, g_ref, sh_ref, sc_ref, h_ref):
        xv = x_ref[...]
        r = lax.rsqrt(jnp.mean(xv * xv, axis=-1, keepdims=True) + EPS)
        hn = (xv * r) * g_ref[...]
        h_ref[...] = (hn * (1.0 + sc_ref[...]) + sh_ref[...]).astype(BF16)

    return pl.pallas_call(
        body, name="norm1_mod", grid=(S // tm,),
        in_specs=[_rows(tm), _vec(), _vec(), _vec()], out_specs=_rows(tm),
        out_shape=jax.ShapeDtypeStruct((S, D), BF16), compiler_params=_cp(1),
    )(x, g, shift, scale)


def _resid_norm_mod(x, mixed, gate, g, shift, scale, tm=512):
    S = x.shape[0]
    tm = min(tm, S)

    def body(x_ref, m_ref, gt_ref, g_ref, sh_ref, sc_ref, x1_ref, h_ref):
        x1 = x_ref[...] + (1.0 + gt_ref[...]) * m_ref[...]
        x1_ref[...] = x1
        r = lax.rsqrt(jnp.mean(x1 * x1, axis=-1, keepdims=True) + EPS)
        hn = (x1 * r) * g_ref[...]
        h_ref[...] = (hn * (1.0 + sc_ref[...]) + sh_ref[...]).astype(BF16)

    return pl.pallas_call(
        body, name="resid_norm2_mod", grid=(S // tm,),
        in_specs=[_rows(tm), _rows(tm), _vec(), _vec(), _vec(), _vec()],
        out_specs=[_rows(tm), _rows(tm)],
        out_shape=[jax.ShapeDtypeStruct((S, D), F32), jax.ShapeDtypeStruct((S, D), BF16)],
        compiler_params=_cp(1),
    )(x, mixed, gate, g, shift, scale)


def _conv3(ext, w_ref, b_ref, cs):
    e1 = pltpu.roll(ext, 1, 0)
    e2 = pltpu.roll(ext, 2, 0)
    u = b_ref[:, cs] + w_ref[0:1, cs] * e2
    u = u + w_ref[1:2, cs] * e1
    u = u + w_ref[2:3, cs] * ext
    return u, e1, e2


def _conv_glu(u0p, wc_p, bc_p, tm=256):
    S = u0p.shape[0]
    tm = min(tm, S)
    hb = tm // 16

    def body(u_ref, p_ref, w_ref, b_ref, a_ref):
        first = pl.program_id(1) == 0
        for k in range(HB // LANES):
            us = []
            for off in (k * LANES, HB + k * LANES):
                cs = slice(off, off + LANES)
                prev = jnp.where(first, 0.0, p_ref[:, cs].astype(F32))
                ext = jnp.concatenate([prev, u_ref[:, cs].astype(F32)], axis=0)
                us.append(_conv3(ext, w_ref, b_ref, cs)[0][16:])
            a_ref[:, k * LANES:(k + 1) * LANES] = (us[0] * (us[1] * _sigmoid(us[1]))).astype(BF16)

    return pl.pallas_call(
        body, name="conv_glu", grid=(2, S // tm),
        in_specs=[pl.BlockSpec((tm, 2 * HB), lambda j, i: (i, j)),
                  pl.BlockSpec((16, 2 * HB), lambda j, i: (jnp.maximum(i * hb - 1, 0), j)),
                  pl.BlockSpec((3, 2 * HB), lambda j, i: (0, j)),
                  pl.BlockSpec((1, 2 * HB), lambda j, i: (0, j))],
        out_specs=pl.BlockSpec((tm, HB), lambda j, i: (i, j)),
        out_shape=jax.ShapeDtypeStruct((S, D_FF), BF16), compiler_params=_cp(2),
    )(u0p, u0p, wc_p, bc_p)


def _conv_glu_bwd(da, u0p, wc_p, bc_p, tm=256):
    S = u0p.shape[0]
    tm = min(tm, S)
    hb = tm // 16
    nlast = S // 16 - 1

    def body(da_ref, dan_ref, u_ref, p_ref, n_ref, w_ref, b_ref, o_ref, s_ref):
        i = pl.program_id(1)
        first = i == 0
        last = i == pl.num_programs(1) - 1

        @pl.when(first)
        def _():
            s_ref[...] = jnp.zeros_like(s_ref)

        n = tm + 16
        for k in range(HB // LANES):
            kc = slice(k * LANES, (k + 1) * LANES)
            dae = jnp.concatenate([da_ref[:, kc].astype(F32),
                                   jnp.where(last, 0.0, dan_ref[:, kc].astype(F32))], axis=0)
            halves = []
            for off in (k * LANES, HB + k * LANES):
                cs = slice(off, off + LANES)
                ext = jnp.concatenate([jnp.where(first, 0.0, p_ref[:, cs].astype(F32)),
                                       u_ref[:, cs].astype(F32), n_ref[:, cs].astype(F32)], axis=0)
                u, e1, e2 = _conv3(ext, w_ref, b_ref, cs)
                halves.append((u[16:], ext[16:16 + tm], e1[16:16 + tm], e2[16:16 + tm], cs))
            val, gt = halves[0][0], halves[1][0]
            sg = _sigmoid(gt)
            dus = (dae * (gt * sg), dae * val * (sg * (1.0 + gt * (1.0 - sg))))
            for du, (_, x0, x1, x2, cs) in zip(dus, halves):
                du0 = (w_ref[2:3, cs] * du + w_ref[1:2, cs] * pltpu.roll(du, n - 1, 0)
                       + w_ref[0:1, cs] * pltpu.roll(du, n - 2, 0))
                o_ref[:, cs] = du0[:tm].astype(BF16)
                dut = du[:tm]
                s_ref[0:1, cs] += jnp.sum(dut, axis=0, keepdims=True)
                s_ref[1:2, cs] += jnp.sum(dut * x2, axis=0, keepdims=True)
                s_ref[2:3, cs] += jnp.sum(dut * x1, axis=0, keepdims=True)
                s_ref[3:4, cs] += jnp.sum(dut * x0, axis=0, keepdims=True)

    return pl.pallas_call(
        body, name="conv_glu_bwd", grid=(2, S // tm),
        in_specs=[pl.BlockSpec((tm, HB), lambda j, i: (i, j)),
                  pl.BlockSpec((16, HB), lambda j, i: (jnp.minimum((i + 1) * hb, nlast), j)),
                  pl.BlockSpec((tm, 2 * HB), lambda j, i: (i, j)),
                  pl.BlockSpec((16, 2 * HB), lambda j, i: (jnp.maximum(i * hb - 1, 0), j)),
                  pl.BlockSpec((16, 2 * HB), lambda j, i: (jnp.minimum((i + 1) * hb, nlast), j)),
                  pl.BlockSpec((3, 2 * HB), lambda j, i: (0, j)),
                  pl.BlockSpec((1, 2 * HB), lambda j, i: (0, j))],
        out_specs=[pl.BlockSpec((tm, 2 * HB), lambda j, i: (i, j)),
                   pl.BlockSpec((8, 2 * HB), lambda j, i: (0, j))],
        out_shape=[jax.ShapeDtypeStruct((S, 2 * D_FF), BF16), jax.ShapeDtypeStruct((8, 2 * D_FF), F32)],
        compiler_params=_cp(2),
    )(da, da, u0p, u0p, u0p, wc_p, bc_p)


def _final_loss(x1, y2, gate2, g_final, target, tm=256):
    S = x1.shape[0]
    tm = min(tm, S)

    def body(x1_ref, y2_ref, gt_ref, g_ref, t_ref, dx_ref, dy_ref, s_ref):
        @pl.when(pl.program_id(0) == 0)
        def _():
            s_ref[...] = jnp.zeros_like(s_ref)

        y2 = y2_ref[...]
        og = 1.0 + gt_ref[...]
        x2 = x1_ref[...] + og * y2
        r = lax.rsqrt(jnp.mean(x2 * x2, axis=-1, keepdims=True) + EPS)
        n = x2 * r
        g = g_ref[...]
        err = n * g - t_ref[...]
        dy = err * (1.0 / D)
        dn = dy * g
        dx2 = r * (dn - n * jnp.mean(dn * n, axis=-1, keepdims=True))
        dx_ref[...] = dx2
        dy_ref[...] = (dx2 * og).astype(BF16)
        s_ref[0:1, :] += jnp.sum(dy * n, axis=0, keepdims=True)
        s_ref[1:2, :] += jnp.sum(dx2 * y2, axis=0, keepdims=True)
        s_ref[2:3, :] += jnp.sum(err * err, axis=0, keepdims=True)

    return pl.pallas_call(
        body, name="final_loss", grid=(S // tm,),
        in_specs=[_rows(tm), _rows(tm), _vec(), _vec(), _rows(tm)],
        out_specs=[_rows(tm), _rows(tm), pl.BlockSpec((8, D), lambda i: (0, 0))],
        out_shape=[jax.ShapeDtypeStruct((S, D), F32), jax.ShapeDtypeStruct((S, D), BF16),
                   jax.ShapeDtypeStruct((8, D), F32)],
        compiler_params=_cp(1),
    )(x1, y2, gate2, g_final, target)


def _norm_mod_bwd(dh, xin, dres, g, scale, mixed, gate, name, tm=256, carried=()):
    S = xin.shape[0]
    tm = min(tm, S)
    with_gate = mixed is not None

    def body(*refs):
        if with_gate:
            dh_ref, x_ref, dr_ref, g_ref, sc_ref, m_ref, gt_ref, dx_ref, dm_ref, s_ref = refs
        else:
            dh_ref, x_ref, dr_ref, g_ref, sc_ref, dx_ref, s_ref = refs

        @pl.when(pl.program_id(0) == 0)
        def _():
            s_ref[...] = jnp.zeros_like(s_ref)

        xv = x_ref[...]
        dhv = dh_ref[...]
        r = lax.rsqrt(jnp.mean(xv * xv, axis=-1, keepdims=True) + EPS)
        n = xv * r
        g = g_ref[...]
        hn = n * g
        dhn = dhv * (1.0 + sc_ref[...])
        dn = dhn * g
        dx = dr_ref[...] + r * (dn - n * jnp.mean(dn * n, axis=-1, keepdims=True))
        dx_ref[...] = dx
        s_ref[0:1, :] += jnp.sum(dhv, axis=0, keepdims=True)
        s_ref[1:2, :] += jnp.sum(dhv * hn, axis=0, keepdims=True)
        s_ref[2:3, :] += jnp.sum(dhn * n, axis=0, keepdims=True)
        if with_gate:
            dm_ref[...] = (dx * (1.0 + gt_ref[...])).astype(BF16)
            s_ref[3:4, :] += jnp.sum(dx * m_ref[...], axis=0, keepdims=True)

    ins = [dh, xin, dres, g, scale]
    in_specs = [_rows(tm), _rows(tm), _rows(tm), _vec(), _vec()]
    out_specs = [_rows(tm)]
    out_shape = [jax.ShapeDtypeStruct((S, D), F32)]
    if with_gate:
        ins += [mixed, gate]
        in_specs += [_rows(tm), _vec()]
        out_specs.append(_rows(tm))
        out_shape.append(jax.ShapeDtypeStruct((S, D), BF16))
    out_specs.append(pl.BlockSpec((8, D), lambda i: (0, 0)))
    out_shape.append(jax.ShapeDtypeStruct((8, D), F32))
    return _call(body, carried, ins, name=name, grid=(S // tm,), in_specs=in_specs, out_specs=out_specs,
                 out_shape=out_shape, compiler_params=_cp(1))


def _tri(n, rel):
    row = lax.broadcasted_iota(jnp.int32, (n, n), 0)
    col = lax.broadcasted_iota(jnp.int32, (n, n), 1)
    return {"gt": row > col, "ge": row >= col, "lt": row < col, "le": row <= col}[rel]


def _pair_diag(mask):
    u = jnp.where(mask, 1.0, 0.0).astype(BF16)
    z = jnp.zeros_like(u)
    return jnp.concatenate([jnp.concatenate([u, z], axis=1), jnp.concatenate([z, u], axis=1)], axis=0)


def _pair_rows(xp, lo_half):
    z = jnp.zeros_like(xp)
    return jnp.concatenate([jnp.where(lo_half, xp, z), jnp.where(lo_half, z, xp)], axis=0)


def _sb_scores(z, causal, diag):
    ls, ps, es = [], [], []
    for hh in range(2):
        zz = z[:, hh * QB:(hh + 1) * QB]
        e = jnp.exp(-jnp.abs(zz))
        l = -(jnp.maximum(zz, 0.0) + jnp.log1p(e))
        ps.append(l + zz)
        ls.append(jnp.where(causal, l, 0.0) if diag else l)
        es.append(e)
    return ls, ps, es


def _sb_fwd(proj):
    S = proj.shape[0]
    nq = S // QB

    def body(q_ref, k_ref, v_ref, o_ref, t_ref, c_ref, acc_ref, qs_ref):
        i = pl.program_id(0)
        causal = _tri(QB, "gt")
        usuf = _pair_diag(_tri(QB, "gt"))
        lo_half = lax.broadcasted_iota(jnp.int32, (QB, 128), 1) < DK
        qs_ref[...] = q_ref[...] * 0.125

        def block(j, diag):
            rows = pl.ds(pl.multiple_of(j * QB, QB), QB)
            pairs = range(H_SB // 2)
            cols = [slice(pr * 128, (pr + 1) * 128) for pr in pairs]
            zs = [_dot_nt(qs_ref[:, cols[pr]], _pair_rows(k_ref[rows, cols[pr]], lo_half)) for pr in pairs]
            sc = [_sb_scores(zs[pr], causal, diag) for pr in pairs]
            sufs = []
            for pr in pairs:
                lh, ll = _split(jnp.concatenate(sc[pr][0], axis=1))
                sufs.append(_dot(lh, usuf) + _dot(ll, usuf))
            cmax = None
            wps = []
            for pr in pairs:
                ws = []
                for hh in range(2):
                    h = 2 * pr + hh
                    b = sufs[pr][:, hh * QB:(hh + 1) * QB]
                    if not diag:
                        b = b + c_ref[h, :, 0:1]
                    w = jnp.exp(sc[pr][1][hh] + b)
                    ws.append((jnp.where(causal, w, 0.0) if diag else w).astype(BF16))
                    cn = b[:, 0:1] + sc[pr][0][hh][:, 0:1]
                    c_ref[h, :, 0:1] = cn
                    cmax = cn if cmax is None else jnp.maximum(cmax, cn)
                wps.append(jnp.concatenate(ws, axis=1))
            for pr in pairs:
                upd = _dot(wps[pr], _pair_rows(v_ref[rows, cols[pr]], lo_half))
                if diag:
                    acc_ref[:, cols[pr]] = upd
                else:
                    acc_ref[:, cols[pr]] += upd
            return jnp.max(cmax)

        def cond(st):
            return jnp.logical_and(st[0] >= 0, st[1] > SB_SKIP)

        def step(st):
            return st[0] - 1, block(st[0], False)

        j, _ = lax.while_loop(cond, step, (i - 1, block(i, True)))
        o_ref[...] = acc_ref[...].astype(BF16)
        t_ref[...] = jnp.zeros_like(t_ref)
        for h in range(H_SB):
            t_ref[h // 4, :, h % 4:h % 4 + 1] = c_ref[h, :, 0:1]
        t_ref[:, :, 8:9] = jnp.zeros((2, QB, 1), F32) + (j + 1).astype(F32)

    return pl.pallas_call(
        body, name="sb_fwd", grid=(nq,),
        in_specs=[pl.BlockSpec((QB, 512), lambda i: (i, 0)),
                  pl.BlockSpec((S, 512), lambda i: (0, 1), pipeline_mode=ONE_BUF),
                  pl.BlockSpec((S, 512), lambda i: (0, 2), pipeline_mode=ONE_BUF)],
        out_specs=[pl.BlockSpec((QB, 512), lambda i: (i, 0)),
                   pl.BlockSpec((2, QB, 128), lambda i: (0, i, 0))],
        out_shape=[jax.ShapeDtypeStruct((S, 512), BF16), jax.ShapeDtypeStruct((2, S, 128), F32)],
        scratch_shapes=[pltpu.VMEM((H_SB, QB, 128), F32), pltpu.VMEM((QB, 512), F32), pltpu.VMEM((QB, 512), BF16)],
        compiler_params=_cp(1, 40),
    )(proj, proj, proj)


def _sb_bwd(proj, dcat, stats):
    S = proj.shape[0]
    nq = S // QB

    def body(q_ref, k_ref, v_ref, do_ref, t_ref, dq_ref, dk_ref, dv_ref, dk_acc, dv_acc, dq_acc, pc_ref, qs_ref):
        i = pl.program_id(1)

        @pl.when(i == 0)
        def _():
            dk_acc[...] = jnp.zeros_like(dk_acc)
            dv_acc[...] = jnp.zeros_like(dv_acc)

        causal = _tri(QB, "gt")
        uin = _pair_diag(_tri(QB, "le"))
        uex = _pair_diag(_tri(QB, "lt"))
        lo_half = lax.broadcasted_iota(jnp.int32, (QB, 128), 1) < DK
        qs_ref[...] = q_ref[...] * 0.125
        pc_ref[...] = jnp.zeros_like(pc_ref)
        dq_acc[...] = jnp.zeros_like(dq_acc)
        jstart = jnp.max(t_ref[:, 8:9]).astype(jnp.int32)

        def block(j, diag):
            rows = pl.ds(pl.multiple_of(j * QB, QB), QB)
            pairs = range(2)
            cols = [slice(pr * 128, (pr + 1) * 128) for pr in pairs]
            kbds = [_pair_rows(k_ref[rows, cols[pr]], lo_half) for pr in pairs]
            zs = [_dot_nt(qs_ref[:, cols[pr]], kbds[pr]) for pr in pairs]
            dws = [_dot_nt(do_ref[:, cols[pr]], _pair_rows(v_ref[rows, cols[pr]], lo_half)) for pr in pairs]
            sc = [_sb_scores(zs[pr], causal, diag) for pr in pairs]
            plins = []
            for pr in pairs:
                lh, ll = _split(jnp.concatenate(sc[pr][0], axis=1))
                plins.append(_dot(lh, uin) + _dot(ll, uin))
            wss, gss, gexs = [], [], []
            for pr in pairs:
                ws, gs = [], []
                for hh in range(2):
                    h = 2 * pr + hh
                    half = slice(hh * QB, (hh + 1) * QB)
                    b = (t_ref[:, h:h + 1] - pc_ref[h, :, 0:1]) - plins[pr][:, half]
                    w = jnp.exp(sc[pr][1][hh] + b)
                    if diag:
                        w = jnp.where(causal, w, 0.0)
                    ws.append(w)
                    gs.append(dws[pr][:, half] * w)
                wss.append(ws)
                gss.append(gs)
            for pr in pairs:
                gh, gl = _split(jnp.concatenate(gss[pr], axis=1))
                gexs.append(_dot(gh, uex) + _dot(gl, uex))
            dzbs = []
            for pr in pairs:
                dzs = []
                for hh in range(2):
                    h = 2 * pr + hh
                    half = slice(hh * QB, (hh + 1) * QB)
                    e = sc[pr][2][hh]
                    r = 1.0 / (1.0 + e)
                    er = e * r
                    pos = zs[pr][:, half] >= 0.0
                    gx = gexs[pr][:, half]
                    g = gss[pr][hh]
                    dz = g * jnp.where(pos, er, r) - (gx + pc_ref[4 + h, :, 0:1]) * jnp.where(pos, r, er)
                    dzs.append(jnp.where(causal, dz, 0.0) if diag else dz)
                    pc_ref[h, :, 0:1] += plins[pr][:, half][:, QB - 1:QB]
                    pc_ref[4 + h, :, 0:1] += gx[:, QB - 1:QB] + g[:, QB - 1:QB]
                dzbs.append(jnp.concatenate(dzs, axis=1).astype(BF16))
            for pr in pairs:
                dq_acc[:, cols[pr]] += _dot(dzbs[pr], kbds[pr])
                r1 = _dot_tn(dzbs[pr], qs_ref[:, cols[pr]])
                dk_acc[rows, cols[pr]] += jnp.where(lo_half, r1[:QB], r1[QB:])
                r2 = _dot_tn(jnp.concatenate(wss[pr], axis=1).astype(BF16), do_ref[:, cols[pr]])
                dv_acc[rows, cols[pr]] += jnp.where(lo_half, r2[:QB], r2[QB:])

        def step(j, carry):
            block(j, False)
            return carry

        lax.fori_loop(jstart, i, step, 0)
        block(i, True)
        dq_ref[...] = (dq_acc[...] * 0.125).astype(BF16)

        @pl.when(i == nq - 1)
        def _():
            dk_ref[...] = dk_acc[...].astype(BF16)
            dv_ref[...] = dv_acc[...].astype(BF16)

    return pl.pallas_call(
        body, name="sb_bwd", grid=(2, nq),
        in_specs=[pl.BlockSpec((QB, 256), lambda g, i: (i, g)),
                  pl.BlockSpec((S, 256), lambda g, i: (0, 2 + g), pipeline_mode=ONE_BUF),
                  pl.BlockSpec((S, 256), lambda g, i: (0, 4 + g), pipeline_mode=ONE_BUF),
                  pl.BlockSpec((QB, 256), lambda g, i: (i, g)),
                  pl.BlockSpec((None, QB, 128), lambda g, i: (g, i, 0))],
        out_specs=[pl.BlockSpec((QB, 256), lambda g, i: (i, g)),
                   pl.BlockSpec((S, 256), lambda g, i: (0, g)),
                   pl.BlockSpec((S, 256), lambda g, i: (0, g))],
        out_shape=[jax.ShapeDtypeStruct((S, 512), BF16)] * 3,
        scratch_shapes=[pltpu.VMEM((S, 256), F32), pltpu.VMEM((S, 256), F32), pltpu.VMEM((QB, 256), F32),
                        pltpu.VMEM((8, QB, 128), F32), pltpu.VMEM((QB, 256), BF16)],
        compiler_params=_cp(2, 56),
    )(proj, proj, proj, dcat, stats)


def _gla_gate(gf_ref, wfg_ref, bfg_ref):
    f = _dot(gf_ref[...], wfg_ref[...]) + bfg_ref[...]
    _, la, _ = _log_sigmoid_parts(f)
    la = la * (1.0 / 16.0)
    lah, lal = _split(la)
    lin = jnp.where(_tri(CHUNK, "ge"), 1.0, 0.0).astype(BF16)
    cum = _dot(jnp.concatenate([lin, lin], axis=1), jnp.concatenate([lah, lal], axis=0))
    total = cum[CHUNK - 1:CHUNK, :]
    return f, jnp.exp(total - cum), jnp.exp(total)


def _gla_specs(nc, rev):
    def ix(c):
        return nc - 1 - c if rev else c
    return [pl.BlockSpec((CHUNK, 256), lambda c: (ix(c), 6)),
            pl.BlockSpec((CHUNK, 256), lambda c: (ix(c), 7)),
            pl.BlockSpec((CHUNK, 512), lambda c: (ix(c), 4)),
            pl.BlockSpec((CHUNK, 512), lambda c: (ix(c), 5)),
            pl.BlockSpec((CHUNK, 128), lambda c: (ix(c), 24))]


def _gla_fwd(proj, wfg_p, bfg, ggla, carried=()):
    S = proj.shape[0]
    nc = S // CHUNK

    def body(q_ref, k_ref, v_ref, gg_ref, gf_ref, wfg_ref, bfg_ref, ggla_ref, o_ref, st_ref, state):
        @pl.when(pl.program_id(0) == 0)
        def _():
            state[...] = jnp.zeros_like(state)

        _, e, dec = _gla_gate(gf_ref, wfg_ref, bfg_ref)
        kdec = (k_ref[...].astype(F32) * e).astype(BF16)
        for h in range(H_GLA):
            ks = slice(h * DK, (h + 1) * DK)
            vs = slice(h * DV, (h + 1) * DV)
            new = state[:, ks] * dec[:, ks] + _dot_tn(v_ref[:, vs], kdec[:, ks])
            state[:, ks] = new
            o = _dot_nt(q_ref[:, ks] * 0.125, new.astype(BF16))
            ohn = o * lax.rsqrt(jnp.mean(o * o, axis=-1, keepdims=True) + EPS)
            gg = gg_ref[:, vs].astype(F32)
            o_ref[:, vs] = ((ohn * ggla_ref[:, vs]) * (gg * _sigmoid(gg))).astype(BF16)
        st_ref[0] = state[...]

    return _call(
        body, carried, [proj, proj, proj, proj, proj, wfg_p, bfg, ggla], name="gla_fwd", grid=(nc,),
        in_specs=_gla_specs(nc, False) + [pl.BlockSpec((128, 256), lambda c: (0, 0)),
                                          pl.BlockSpec((1, 256), lambda c: (0, 0)),
                                          pl.BlockSpec((1, 512), lambda c: (0, 0))],
        out_specs=[pl.BlockSpec((CHUNK, 512), lambda c: (c, 0)),
                   pl.BlockSpec((1, 128, 256), lambda c: (c, 0, 0))],
        out_shape=[jax.ShapeDtypeStruct((S, 512), BF16), jax.ShapeDtypeStruct((nc, 128, 256), F32)],
        scratch_shapes=[pltpu.VMEM((128, 256), F32)],
        compiler_params=_cp(1))


def _gla_bwd(dcat, proj, states, wfg_p, bfg, ggla, carried=()):
    S = proj.shape[0]
    nc = S // CHUNK

    def body(do_ref, q_ref, k_ref, v_ref, gg_ref, gf_ref, sc_ref, sp_ref, wfg_ref, bfg_ref, ggla_ref,
             dp_ref, s_ref, dw_ref, carry):
        cr = pl.program_id(0)

        @pl.when(cr == 0)
        def _():
            carry[...] = jnp.zeros_like(carry)
            s_ref[...] = jnp.zeros_like(s_ref)
            dw_ref[...] = jnp.zeros_like(dw_ref)

        f, e, dec = _gla_gate(gf_ref, wfg_ref, bfg_ref)
        kf = k_ref[...].astype(F32) * e
        kdec = kf.astype(BF16)
        s_cur = sc_ref[0]
        s_prev = jnp.where(cr < nc - 1, sp_ref[0], 0.0)
        dkdec, ddec = [], []
        for h in range(H_GLA):
            ks = slice(h * DK, (h + 1) * DK)
            vs = slice(h * DV, (h + 1) * DV)
            qs = q_ref[:, ks] * 0.125
            sb = s_cur[:, ks].astype(BF16)
            o = _dot_nt(qs, sb)
            rr = lax.rsqrt(jnp.mean(o * o, axis=-1, keepdims=True) + EPS)
            ohn = o * rr
            gg = gg_ref[:, vs].astype(F32)
            sg = _sigmoid(gg)
            dout = do_ref[:, vs].astype(F32)
            gl = ggla_ref[:, vs]
            dp_ref[:, 1024 + h * DV:1024 + (h + 1) * DV] = (
                dout * (ohn * gl) * (sg * (1.0 + gg * (1.0 - sg)))).astype(BF16)
            dt1 = dout * (gg * sg)
            s_ref[0:1, vs] += jnp.sum(dt1 * ohn, axis=0, keepdims=True)
            dohn = dt1 * gl
            dob = (rr * (dohn - ohn * jnp.mean(dohn * ohn, axis=-1, keepdims=True))).astype(BF16)
            dp_ref[:, ks] = (_dot(dob, sb) * 0.125).astype(BF16)
            gt = _dot_tn(dob, qs) + carry[:, ks]
            gtb = gt.astype(BF16)
            dkdec.append(_dot(v_ref[:, vs], gtb))
            dp_ref[:, 512 + h * DV:512 + (h + 1) * DV] = _dot_nt(kdec[:, ks], gtb).astype(BF16)
            ddec.append(jnp.sum(gt * s_prev[:, ks], axis=0, keepdims=True))
            carry[:, ks] = gt * dec[:, ks]
        dkd = jnp.concatenate(dkdec, axis=1)
        dp_ref[:, 256:512] = (dkd * e).astype(BF16)
        wh, wl = _split(dkd * kf)
        lex = jnp.where(_tri(CHUNK, "gt"), 1.0, 0.0).astype(BF16)
        dla = _dot(jnp.concatenate([lex, lex], axis=1), jnp.concatenate([wh, wl], axis=0))
        dla = dla + jnp.concatenate(ddec, axis=1) * dec
        df = dla * _sigmoid(-f) * (1.0 / 16.0)
        dfb = df.astype(BF16)
        s_ref[1:2, 0:256] += jnp.sum(df, axis=0, keepdims=True)
        dw_ref[...] += _dot_tn(gf_ref[...], dfb)
        dp_ref[:, 1536:1664] = _dot_nt(dfb, wfg_ref[...]).astype(BF16)

    return _call(
        body, carried, [dcat, proj, proj, proj, proj, proj, states, states, wfg_p, bfg, ggla],
        name="gla_bwd", grid=(nc,),
        in_specs=[pl.BlockSpec((CHUNK, 512), lambda c: (nc - 1 - c, 1))] + _gla_specs(nc, True) + [
            pl.BlockSpec((1, 128, 256), lambda c: (nc - 1 - c, 0, 0)),
            pl.BlockSpec((1, 128, 256), lambda c: (jnp.maximum(nc - 2 - c, 0), 0, 0)),
            pl.BlockSpec((128, 256), lambda c: (0, 0)),
            pl.BlockSpec((1, 256), lambda c: (0, 0)),
            pl.BlockSpec((1, 512), lambda c: (0, 0))],
        out_specs=[pl.BlockSpec((CHUNK, 1664), lambda c: (nc - 1 - c, 0)),
                   pl.BlockSpec((8, 512), lambda c: (0, 0)),
                   pl.BlockSpec((128, 256), lambda c: (0, 0))],
        out_shape=[jax.ShapeDtypeStruct((S, 1664), BF16), jax.ShapeDtypeStruct((8, 512), F32),
                   jax.ShapeDtypeStruct((128, 256), F32)],
        scratch_shapes=[pltpu.VMEM((128, 256), F32)],
        compiler_params=_cp(1))


def _sum_leading(a, name):
    n = a.shape[0]

    def body(a_ref, o_ref):
        acc = a_ref[0]
        for k in range(1, n):
            acc = acc + a_ref[k]
        o_ref[...] = acc

    return pl.pallas_call(
        body, name=name, out_shape=jax.ShapeDtypeStruct(a.shape[1:], F32),
        in_specs=[VMEM_SPEC], out_specs=VMEM_SPEC,
    )(a)


def _sum_chip(own, recv, name):
    R, C = own.shape
    tr = _row_tile(R, C)

    def body(o_ref, r_ref, p_ref):
        acc = o_ref[...]
        for k in range(3):
            acc = acc + r_ref[k].astype(F32)
        p_ref[...] = acc

    return pl.pallas_call(
        body, name=name, grid=(R // tr,),
        in_specs=[pl.BlockSpec((tr, C), lambda i: (i, 0)), pl.BlockSpec((3, tr, C), lambda i: (0, i, 0))],
        out_specs=pl.BlockSpec((tr, C), lambda i: (i, 0)),
        out_shape=jax.ShapeDtypeStruct((R, C), F32), compiler_params=_cp(1),
    )(own, recv)


def _adamw(w, p, q, m, v, name):
    R, C = w.shape
    tr = _row_tile(R, C, 256 * 1024)
    two = q is not None

    def body(*refs):
        if two:
            w_ref, p_ref, q_ref, m_ref, v_ref, g_out, d_out, m_out, v_out = refs
            g = p_ref[...] + q_ref[...]
        else:
            w_ref, p_ref, m_ref, v_ref, g_out, d_out, m_out, v_out = refs
            g = p_ref[...]
        m2 = B1 * m_ref[...] + (1.0 - B1) * g
        v2 = B2 * v_ref[...] + (1.0 - B2) * (g * g)
        m_hat = m2 / (1.0 - B1 ** STEP)
        v_hat = v2 / (1.0 - B2 ** STEP)
        g_out[...] = g
        d_out[...] = -LR * (m_hat / (jnp.sqrt(v_hat) + EPS_A) + WD * w_ref[...])
        m_out[...] = m2
        v_out[...] = v2

    spec = pl.BlockSpec((tr, C), lambda i: (i, 0))
    ins = [w, p, q, m, v] if two else [w, p, m, v]
    return pl.pallas_call(
        body, name=name, grid=(R // tr,),
        in_specs=[spec] * len(ins), out_specs=[spec] * 4,
        out_shape=[jax.ShapeDtypeStruct((R, C), F32)] * 4, compiler_params=_cp(1),
    )(*ins)


def _reduce_big(blocks, recv, cidx, name, swapped=False):
    own = lax.dynamic_index_in_dim(blocks, _slot(cidx, swapped), axis=0, keepdims=False)
    p = _sum_chip(own, recv, name + "_sum")
    return p, _pair_swap(p, name + "_swap")


def _cols_to_chips(a, width):
    return a.reshape(a.shape[0], 4, width).swapaxes(0, 1)


def _chips_to_cols(a):
    return a.swapaxes(0, 1).reshape(a.shape[1], 4 * a.shape[2])


def _swap_mid(a):
    lead = a.shape[:-1]
    return a.reshape(lead + (2, 2, HB)).swapaxes(-3, -2).reshape(lead + (4 * HB,))


def kernel(x, c, w_ada, b_ada, g_norm1, w_in, w_fg2, b_fg2, g_gla_out, w_out, g_norm2, w_up, w_conv, b_conv, w_down, g_final, loss_target, m_w_ada, m_b_ada, m_g_norm1, m_w_in, m_w_fg2, m_b_fg2, m_g_gla_out, m_w_out, m_g_norm2, m_w_up, m_w_conv, m_b_conv, m_w_down, m_g_final, v_w_ada, v_b_ada, v_g_norm1, v_w_in, v_w_fg2, v_b_fg2, v_g_gla_out, v_w_out, v_g_norm2, v_w_up, v_w_conv, v_b_conv, v_w_down, v_g_final):
    xi, yi, ci = lax.axis_index("x"), lax.axis_index("y"), lax.axis_index("c")
    cidx = 2 * xi + yi
    didx = 4 * xi + 2 * yi + ci
    xs = x[0]
    tgt = loss_target[0]
    gfin = g_final.reshape(1, D)
    AW = D * 6 // 4

    c_all = _allgather8(c, "gather_c").reshape(8, D)
    c_pad = jnp.concatenate([c_all, jnp.zeros((8, D), F32)], axis=0)
    mod_part = _ada_fwd(c_pad, w_ada[0], lax.dynamic_slice(b_ada, (0, cidx * AW), (1, AW)))[:8]
    small = jnp.concatenate([mod_part.reshape(-1), w_conv.reshape(-1), w_fg2.reshape(-1)]).reshape(-1, 128)
    small_g = _allgather4(small, "gather_small").reshape(4, -1)
    mod = lax.dynamic_index_in_dim(small_g[:, :8 * AW].reshape(4, 8, AW), didx, axis=1, keepdims=False).reshape(1, 6 * D)
    shift1, scale1, gate1, shift2, scale2, gate2 = [mod[:, k * D:(k + 1) * D] for k in range(6)]
    o1 = 8 * AW
    o2 = o1 + 3 * HB
    wc_p = _swap_mid(_chips_to_cols(small_g[:, o1:o2].reshape(4, 3, HB)))
    bc_p = _swap_mid(b_conv)
    wfg_full = _chips_to_cols(small_g[:, o2:].reshape(4, RANK, 64))
    wfg_p = jnp.concatenate([wfg_full, jnp.zeros((128 - RANK, 256), F32)], axis=0).astype(BF16)

    w_in_f = _chips_to_cols(_allgather4(w_in[0].astype(BF16), "gather_w_in"))
    w_in_p = jnp.concatenate([w_in_f, jnp.zeros((D, N_IN_P - N_IN), BF16)], axis=1)

    h = _norm_mod(xs, g_norm1, shift1, scale1)
    proj, (w_out_g, w_down_g) = _mm(
        [(h, w_in_p)], BF16, "mm_in", 256, N_IN_P, 48,
        carried=[("gather", w_out[0].astype(BF16), False), ("gather", w_down[0].astype(BF16), False)])
    w_out_f = w_out_g.reshape(D, D)
    w_down_f = w_down_g.reshape(D_FF, D)
    (o_gla, states), (w_up_g,) = _gla_fwd(proj, wfg_p, b_fg2, g_gla_out,
                                          carried=[("gather", w_up[0].astype(BF16), False)])
    w_up_p = _swap_mid(_chips_to_cols(w_up_g))
    o_sb, stats = _sb_fwd(proj)
    mixed =_mm([(o_sb, w_out_f[:512]), (o_gla, w_out_f[512:])], F32, "mm_out", 512, D)
    x1, h2 = _resid_norm_mod(xs, mixed, gate1, g_norm2, shift2, scale2)
    u0p = _mm([(h2, w_up_p)], BF16, "mm_up", 512, HB, 48)
    a = _conv_glu(u0p, wc_p, bc_p)
    y2 = _mm([(a, w_down_f)], F32, "mm_down", 512, D, 48)
    dx2, dy2, s_fin = _final_loss(x1, y2, gate2, gfin, tgt)
    loss = lax.psum(0.5 / D * jnp.sum(s_fin[2]), ("x", "y", "c"))

    da = _mm([(dy2, w_down_f.T)], BF16, "mm_down_t", 512, D_FF, 48)
    dw_down = _mm_tn([a], dy2, "mm_dw_down", D, 512, 56).reshape(4, D_FF // 4, D)
    du0p, s_conv = _conv_glu_bwd(da, u0p, wc_p, bc_p)
    dh2 = _mm([(du0p, w_up_p.T)], F32, "mm_up_t", 256, D, 56)
    dw_up = _mm_tn([h2], du0p, "mm_dw_up", HB, 512, 48)
    (dx1, dmixed, s_n2), _ = _norm_mod_bwd(dh2, x1, dx2, g_norm2, scale2, mixed, gate1, "norm2_bwd")
    dcat = _mm([(dmixed, w_out_f.T)], BF16, "mm_out_t", 512, D)
    dw_out = _mm_tn([o_sb, o_gla], dmixed, "mm_dw_out", D, 512).reshape(4, D // 4, D)
    dq, dk, dv = _sb_bwd(proj, dcat, stats)
    (dp_gla, s_gla, dwfg), (rc_down, rc_up) = _gla_bwd(
        dcat, proj, states, wfg_p, b_fg2, g_gla_out,
        carried=[("scatter", dw_down.astype(BF16), False), ("scatter", dw_up.astype(BF16), True)])
    w_in_t = w_in_p.T
    dh, (rc_out,) = _mm(
        [(dq, w_in_t[:512]), (dk, w_in_t[512:1024]), (dv, w_in_t[1024:1536]), (dp_gla, w_in_t[1536:])],
        F32, "mm_in_t", 256, D, 48, carried=[("scatter", dw_out.astype(BF16), False)])
    dw_in_parts = _mm_tn_multi(h, [dq, dk, dv, dp_gla], "mm_dw_in", 512, 56)
    dw_in = _cols_to_chips(jnp.concatenate(dw_in_parts, axis=1)[:, :N_IN], N_IN // 4)
    (gx, s_n1), (rc_in,) = _norm_mod_bwd(dh, xs, dx1, g_norm1, scale1, None, None, "norm1_bwd",
                                        carried=[("scatter", dw_in.astype(BF16), False)])

    dmod = jnp.concatenate([s_n1[0], s_n1[1], s_n2[3], s_n2[0], s_n2[1], s_fin[1]])
    s_conv_n = _swap_mid(s_conv[:4])
    part = jnp.concatenate([dmod, s_n1[2], s_n2[2], s_fin[0], s_gla[0], s_gla[1, :256], s_conv_n[0],
                            s_conv_n[1:4].reshape(-1), dwfg[:RANK].reshape(-1)]).reshape(-1, 128)
    parts = _allgather8(part, "gather_small_grads")
    tot = _sum_leading(parts, "sum_small_grads").reshape(-1)
    dmod_all = parts.reshape(8, -1)[:, :6 * D]
    offs = [0]
    for n in (6 * D, D, D, D, 512, 256, 2 * D_FF, 3 * 2 * D_FF, RANK * 256):
        offs.append(offs[-1] + n)
    g_b_ada, g_g1, g_g2, g_gf, g_ggla, g_bfg, g_bconv, g_wconv_full, g_wfg_full = [
        tot[offs[k]:offs[k + 1]] for k in range(9)]
    g_wconv = lax.dynamic_index_in_dim(_cols_to_chips(g_wconv_full.reshape(3, 2 * D_FF), HB), cidx, 0, keepdims=False)
    g_wfg = lax.dynamic_index_in_dim(_cols_to_chips(g_wfg_full.reshape(RANK, 256), 64), cidx, 0, keepdims=False)

    dmod_pad = jnp.concatenate([dmod_all, jnp.zeros((8, 6 * D), F32)], axis=0)
    g_w_ada = _ada_bwd(c_pad, lax.dynamic_slice(dmod_pad, (0, cidx * AW), (16, AW)))

    p_in, q_in = _reduce_big(dw_in, rc_in, cidx, "rs_w_in")
    p_out, q_out = _reduce_big(dw_out, rc_out, cidx, "rs_w_out")
    p_up, q_up = _reduce_big(dw_up, rc_up, cidx, "rs_w_up", swapped=True)
    p_down, q_down = _reduce_big(dw_down, rc_down, cidx, "rs_w_down")

    out = {}
    out["w_ada"] = _adamw(w_ada[0], g_w_ada, None, m_w_ada[0], v_w_ada[0], "adamw_w_ada")
    out["w_in"] = _adamw(w_in[0], p_in, q_in, m_w_in[0], v_w_in[0], "adamw_w_in")
    out["w_out"] = _adamw(w_out[0], p_out, q_out, m_w_out[0], v_w_out[0], "adamw_w_out")
    out["w_up"] = _adamw(w_up[0], p_up, q_up, m_w_up[0], v_w_up[0], "adamw_w_up")
    out["w_down"] = _adamw(w_down[0], p_down, q_down, m_w_down[0], v_w_down[0], "adamw_w_down")
    small_names = ["b_ada", "g_norm1", "w_fg2", "b_fg2", "g_gla_out", "g_norm2", "w_conv", "b_conv", "g_final"]
    small_w = [b_ada, g_norm1, w_fg2, b_fg2, g_gla_out, g_norm2, w_conv, b_conv, g_final]
    small_m = [m_b_ada, m_g_norm1, m_w_fg2, m_b_fg2, m_g_gla_out, m_g_norm2, m_w_conv, m_b_conv, m_g_final]
    small_v = [v_b_ada, v_g_norm1, v_w_fg2, v_b_fg2, v_g_gla_out, v_g_norm2, v_w_conv, v_b_conv, v_g_final]
    small_gr = [g_b_ada, g_g1, g_wfg, g_bfg, g_ggla, g_g2, g_wconv, g_bconv, g_gf]

    def pack(arrs):
        flat = jnp.concatenate([t.reshape(-1) for t in arrs])
        return jnp.concatenate([flat, jnp.zeros((-flat.shape[0]) % 1024, F32)]).reshape(-1, 128)

    packed = _adamw(pack(small_w), pack(small_gr), None, pack(small_m), pack(small_v), "adamw_small")
    off = 0
    for nm, wt in zip(small_names, small_w):
        n = wt.size
        out[nm] = [t.reshape(-1)[off:off + n].reshape(wt.shape) for t in packed]
        off += n
    for nm in ("w_ada", "w_in", "w_out", "w_up", "w_down"):
        out[nm] = [t[None] for t in out[nm]]

    names = ["w_ada", "b_ada", "g_norm1", "w_in", "w_fg2", "b_fg2", "g_gla_out", "w_out", "g_norm2", "w_up",
             "w_conv", "b_conv", "w_down", "g_final"]
    res = [loss, gx[None]]
    for k in range(4):
        res += [out[nm][k] for nm in names]
    return tuple(res)
```

```python
import functools

import jax
import jax.numpy as jnp
from jax import lax
from jax.experimental import pallas as pl
from jax.experimental.pallas import tpu as pltpu

F32 = jnp.float32
BF16 = jnp.bfloat16
MESH = pl.DeviceIdType.MESH

D = 1024
H_SB = 8
DK = 64
DV = 128
H_GLA = 4
CHUNK = 64
RANK = 16
N_IN = 3088
N_IN_P = 3200
D_FF = 2816
HB = D_FF // 2
LANES = 128
EPS = 1e-6
QB = 128
SB_SKIP = -120.0

LR, B1, B2, EPS_A, WD, STEP = 0.001, 0.9, 0.999, 1e-08, 0.01, 10

ANY = pl.BlockSpec(memory_space=pl.ANY)
VMEM_SPEC = pl.BlockSpec(memory_space=pltpu.VMEM)
ONE_BUF = pl.Buffered(1)


def _cp(ndim=0, vmem_mb=None):
    kw = {}
    if ndim:
        kw["dimension_semantics"] = ("arbitrary",) * ndim
    if vmem_mb:
        kw["vmem_limit_bytes"] = vmem_mb * 1024 * 1024
    return pltpu.CompilerParams(**kw)


def _dot(a, b):
    return jnp.dot(a, b, preferred_element_type=F32)


def _dot_nt(a, b):
    return lax.dot_general(a, b, (((1,), (1,)), ((), ())), preferred_element_type=F32)


def _dot_tn(a, b):
    return lax.dot_general(a, b, (((0,), (0,)), ((), ())), preferred_element_type=F32)


def _split(x):
    hi = x.astype(BF16)
    lo = (x - hi.astype(F32)).astype(BF16)
    return hi, lo


def _sigmoid(x):
    return jax.nn.sigmoid(x)


def _log_sigmoid_parts(z):
    e = jnp.exp(-jnp.abs(z))
    sp = jnp.log1p(e)
    return -(jnp.maximum(z, 0.0) + sp), jnp.minimum(z, 0.0) - sp, e


def _tile2d(rows, cols, budget=512 * 1024):
    best = None
    for t in range(8, rows + 1, 8):
        if rows % t == 0 and t * cols * 4 <= budget:
            best = t
    if best is not None:
        return best, cols
    best = LANES if cols % LANES == 0 else cols
    for t in range(LANES, cols + 1, LANES):
        if cols % t == 0 and rows * t * 4 <= budget:
            best = t
    return rows, best


def _flip(v, bit):
    return 1 - v if bit else v


def _allgather8(a, name):
    def body(a_ref, o_ref, ssem, rsem, lsem):
        x, y, c = lax.axis_index("x"), lax.axis_index("y"), lax.axis_index("c")
        me = 4 * x + 2 * y + c
        loc = pltpu.make_async_copy(a_ref, o_ref.at[me], lsem)
        loc.start()
        sends = []
        for r in range(1, 8):
            peer = (_flip(x, r & 4), _flip(y, r & 2), _flip(c, r & 1))
            cp = pltpu.make_async_remote_copy(
                src_ref=a_ref, dst_ref=o_ref.at[me], send_sem=ssem.at[r - 1], recv_sem=rsem.at[r - 1],
                device_id=peer, device_id_type=MESH)
            cp.start()
            sends.append(cp)
        for r in range(1, 8):
            peer = (_flip(x, r & 4), _flip(y, r & 2), _flip(c, r & 1))
            pidx = 4 * peer[0] + 2 * peer[1] + peer[2]
            pltpu.make_async_remote_copy(
                src_ref=a_ref, dst_ref=o_ref.at[pidx], send_sem=ssem.at[r - 1], recv_sem=rsem.at[r - 1],
                device_id=peer, device_id_type=MESH).wait_recv()
        for cp in sends:
            cp.wait_send()
        loc.wait()

    return pl.pallas_call(
        body, name=name,
        out_shape=jax.ShapeDtypeStruct((8,) + a.shape, a.dtype),
        in_specs=[VMEM_SPEC], out_specs=VMEM_SPEC,
        scratch_shapes=[pltpu.SemaphoreType.DMA((7,)), pltpu.SemaphoreType.DMA((7,)), pltpu.SemaphoreType.DMA],
    )(a)


def _allgather4(a, name):
    def body(a_ref, o_ref, ssem, rsem, lsem):
        x, y, c = lax.axis_index("x"), lax.axis_index("y"), lax.axis_index("c")
        me = 2 * x + y
        loc = pltpu.make_async_copy(a_ref, o_ref.at[me], lsem)
        loc.start()
        sends = []
        for r in range(1, 4):
            peer = (_flip(x, r & 2), _flip(y, r & 1), c)
            cp = pltpu.make_async_remote_copy(
                src_ref=a_ref, dst_ref=o_ref.at[me], send_sem=ssem.at[r - 1], recv_sem=rsem.at[r - 1],
                device_id=peer, device_id_type=MESH)
            cp.start()
            sends.append(cp)
        for r in range(1, 4):
            peer = (_flip(x, r & 2), _flip(y, r & 1), c)
            pidx = 2 * peer[0] + peer[1]
            pltpu.make_async_remote_copy(
                src_ref=a_ref, dst_ref=o_ref.at[pidx], send_sem=ssem.at[r - 1], recv_sem=rsem.at[r - 1],
                device_id=peer, device_id_type=MESH).wait_recv()
        for cp in sends:
            cp.wait_send()
        loc.wait()

    return pl.pallas_call(
        body, name=name,
        out_shape=jax.ShapeDtypeStruct((4,) + a.shape, a.dtype),
        in_specs=[ANY], out_specs=ANY,
        scratch_shapes=[pltpu.SemaphoreType.DMA((3,)), pltpu.SemaphoreType.DMA((3,)), pltpu.SemaphoreType.DMA],
    )(a)


def _slot(chip, swapped):
    return 2 * (chip % 2) + chip // 2 if swapped else chip


def _pair_swap(p, name):
    def body(p_ref, o_ref, ssem, rsem):
        x, y, c = lax.axis_index("x"), lax.axis_index("y"), lax.axis_index("c")
        cp = pltpu.make_async_remote_copy(
            src_ref=p_ref, dst_ref=o_ref, send_sem=ssem, recv_sem=rsem,
            device_id=(x, y, 1 - c), device_id_type=MESH)
        cp.start()
        cp.wait()

    return pl.pallas_call(
        body, name=name,
        out_shape=jax.ShapeDtypeStruct(p.shape, p.dtype),
        in_specs=[ANY], out_specs=ANY,
        scratch_shapes=[pltpu.SemaphoreType.DMA, pltpu.SemaphoreType.DMA],
    )(p)


def _carried_copies(kind, src_ref, dst_ref, sems, swapped):
    ssem, rsem, lsem = sems
    x, y, c = lax.axis_index("x"), lax.axis_index("y"), lax.axis_index("c")
    me = 2 * x + y
    starts, recvs = [], []
    if kind == "gather":
        starts.append(pltpu.make_async_copy(src_ref, dst_ref.at[me], lsem))
    for r in range(1, 4):
        peer = (_flip(x, r & 2), _flip(y, r & 1), c)
        pidx = 2 * peer[0] + peer[1]
        if kind == "gather":
            src, dst, landed = src_ref, dst_ref.at[me], dst_ref.at[pidx]
        else:
            src = src_ref.at[2 * peer[1] + peer[0] if swapped else pidx]
            dst = landed = dst_ref.at[r - 1]
        starts.append(pltpu.make_async_remote_copy(
            src_ref=src, dst_ref=dst, send_sem=ssem.at[r - 1], recv_sem=rsem.at[r - 1],
            device_id=peer, device_id_type=MESH))
        recvs.append(pltpu.make_async_remote_copy(
            src_ref=src, dst_ref=landed, send_sem=ssem.at[r - 1], recv_sem=rsem.at[r - 1],
            device_id=peer, device_id_type=MESH))
    return starts, recvs


def _call(body, carried, operands, *, name, grid, in_specs, out_specs, out_shape, scratch_shapes=(),
          compiler_params=None):
    single = not isinstance(out_shape, (list, tuple))
    out_specs = [out_specs] if single else list(out_specs)
    out_shape = [out_shape] if single else list(out_shape)
    n_in, n_out, n_sc, nh = len(operands), len(out_shape), len(scratch_shapes), len(carried)

    def full(*refs):
        ins, h_in = refs[:n_in], refs[n_in:n_in + nh]
        o0 = n_in + nh
        outs, h_out = refs[o0:o0 + n_out], refs[o0 + n_out:o0 + n_out + nh]
        s0 = o0 + n_out + nh
        scratch, sems = refs[s0:s0 + n_sc], refs[s0 + n_sc:]
        first = last = None
        for d in range(len(grid)):
            f = pl.program_id(d) == 0
            l = pl.program_id(d) == pl.num_programs(d) - 1
            first = f if first is None else jnp.logical_and(first, f)
            last = l if last is None else jnp.logical_and(last, l)

        def copies(t):
            return _carried_copies(carried[t][0], h_in[t], h_out[t], sems[3 * t:3 * t + 3], carried[t][2])

        if nh:
            @pl.when(first)
            def _():
                for t in range(nh):
                    for cp in copies(t)[0]:
                        cp.start()

        body(*ins, *outs, *scratch)

        if nh:
            @pl.when(last)
            def _():
                for t in range(nh):
                    starts, recvs = copies(t)
                    for cp in recvs:
                        cp.wait_recv()
                    for cp in starts:
                        if carried[t][0] == "gather" and cp is starts[0]:
                            cp.wait()
                        else:
                            cp.wait_send()

    h_shapes = [jax.ShapeDtypeStruct(((4,) + arr.shape) if kind == "gather" else ((3,) + arr.shape[1:]), arr.dtype)
                for kind, arr, _ in carried]
    sem_shapes = [pltpu.SemaphoreType.DMA((3,)), pltpu.SemaphoreType.DMA((3,)), pltpu.SemaphoreType.DMA] * nh
    res = pl.pallas_call(
        full, name=name, grid=grid, in_specs=list(in_specs) + [ANY] * nh, out_specs=out_specs + [ANY] * nh,
        out_shape=out_shape + h_shapes, scratch_shapes=list(scratch_shapes) + sem_shapes,
        compiler_params=compiler_params,
    )(*operands, *[arr for _, arr, _ in carried])
    main = res[:n_out]
    return (main[0] if single else main), list(res[n_out:])


def _mm(pairs, out_dtype, name, tm, tn, vmem_mb=None, carried=()):
    S = pairs[0][0].shape[0]
    N = pairs[0][1].shape[1]
    tm = min(tm, S)
    np_ = len(pairs)

    def body(*refs):
        acc = _dot(refs[0][...], refs[1][...])
        for t in range(1, np_):
            acc = acc + _dot(refs[2 * t][...], refs[2 * t + 1][...])
        refs[-1][...] = acc.astype(refs[-1].dtype)

    in_specs, ops = [], []
    for a, w in pairs:
        in_specs += [pl.BlockSpec((tm, a.shape[1]), lambda n, i: (i, 0)),
                     pl.BlockSpec((w.shape[0], tn), lambda n, i: (0, n))]
        ops += [a, w]
    out, got = _call(
        body, carried, ops, name=name, grid=(N // tn, S // tm), in_specs=in_specs,
        out_specs=pl.BlockSpec((tm, tn), lambda n, i: (i, n)),
        out_shape=jax.ShapeDtypeStruct((S, N), out_dtype),
        compiler_params=_cp(2, vmem_mb))
    return (out, got) if carried else out


def _mm_tn(a_list, b, name, bn, tk, vmem_mb=None):
    S, N = b.shape
    ms = [a.shape[1] for a in a_list]
    M = sum(ms)
    tk = min(tk, S)
    na = len(a_list)

    def body(*refs):
        b_ref, o_ref = refs[na], refs[na + 1]

        @pl.when(pl.program_id(1) == 0)
        def _():
            o_ref[...] = jnp.zeros_like(o_ref)
        off = 0
        for t in range(na):
            o_ref[off:off + ms[t], :] += _dot_tn(refs[t][...], b_ref[...])
            off += ms[t]

    return pl.pallas_call(
        body, name=name, grid=(N // bn, S // tk),
        in_specs=[pl.BlockSpec((tk, m), lambda n, k: (k, 0)) for m in ms] + [pl.BlockSpec((tk, bn), lambda n, k: (k, n))],
        out_specs=pl.BlockSpec((None, M, bn), lambda n, k: (n, 0, 0)),
        out_shape=jax.ShapeDtypeStruct((N // bn, M, bn), F32),
        compiler_params=_cp(2, vmem_mb),
    )(*a_list, b)


def _ada_fwd(c_all, w_sh, b_sh):
    def body(c_ref, w_ref, b_ref, o_ref):
        cv = c_ref[...]
        sc = (cv * _sigmoid(cv)).astype(BF16)
        o_ref[...] = _dot(sc, w_ref[...].astype(BF16)) + b_ref[...]

    return pl.pallas_call(
        body, name="ada_fwd", out_shape=jax.ShapeDtypeStruct((c_all.shape[0], w_sh.shape[1]), F32),
        in_specs=[VMEM_SPEC] * 3, out_specs=VMEM_SPEC, compiler_params=_cp(0, 40),
    )(c_all, w_sh, b_sh)


def _ada_bwd(c_all, dmod_sh):
    def body(c_ref, d_ref, o_ref):
        cv = c_ref[...]
        sc = (cv * _sigmoid(cv)).astype(BF16)
        o_ref[...] = _dot_tn(sc, d_ref[...].astype(BF16))

    return pl.pallas_call(
        body, name="ada_bwd", out_shape=jax.ShapeDtypeStruct((c_all.shape[1], dmod_sh.shape[1]), F32),
        in_specs=[VMEM_SPEC] * 2, out_specs=VMEM_SPEC, compiler_params=_cp(0, 40),
    )(c_all, dmod_sh)


def _vec(tm_unused=None):
    return pl.BlockSpec((1, D), lambda i: (0, 0))


def _rows(tm, width=D):
    return pl.BlockSpec((tm, width), lambda i: (i, 0))


def _norm_mod(x, g, shift, scale, tm=512):
    S = x.shape[0]
    tm = min(tm, S)

    def body(x_ref, g_ref, sh_ref, sc_ref, h_ref):
        xv = x_ref[...]
        r = lax.rsqrt(jnp.mean(xv * xv, axis=-1, keepdims=True) + EPS)
        hn = (xv * r) * g_ref[...]
        h_ref[...] = (hn * (1.0 + sc_ref[...]) + sh_ref[...]).astype(BF16)

    return pl.pallas_call(
        body, name="norm1_mod", grid=(S // tm,),
        in_specs=[_rows(tm), _vec(), _vec(), _vec()], out_specs=_rows(tm),
        out_shape=jax.ShapeDtypeStruct((S, D), BF16), compiler_params=_cp(1),
    )(x, g, shift, scale)


def _resid_norm_mod(x, mixed, gate, g, shift, scale, tm=512):
    S = x.shape[0]
    tm = min(tm, S)

    def body(x_ref, m_ref, gt_ref, g_ref, sh_ref, sc_ref, x1_ref, h_ref):
        x1 = x_ref[...] + (1.0 + gt_ref[...]) * m_ref[...]
        x1_ref[...] = x1
        r = lax.rsqrt(jnp.mean(x1 * x1, axis=-1, keepdims=True) + EPS)
        hn = (x1 * r) * g_ref[...]
        h_ref[...] = (hn * (1.0 + sc_ref[...]) + sh_ref[...]).astype(BF16)

    return pl.pallas_call(
        body, name="resid_norm2_mod", grid=(S // tm,),
        in_specs=[_rows(tm), _rows(tm), _vec(), _vec(), _vec(), _vec()],
        out_specs=[_rows(tm), _rows(tm)],
        out_shape=[jax.ShapeDtypeStruct((S, D), F32), jax.ShapeDtypeStruct((S, D), BF16)],
        compiler_params=_cp(1),
    )(x, mixed, gate, g, shift, scale)


def _conv3(ext, w_ref, b_ref, cs):
    e1 = pltpu.roll(ext, 1, 0)
    e2 = pltpu.roll(ext, 2, 0)
    u = b_ref[:, cs] + w_ref[0:1, cs] * e2
    u = u + w_ref[1:2, cs] * e1
    u = u + w_ref[2:3, cs] * ext
    return u, e1, e2


def _conv_glu(u0p, wc_p, bc_p, tm=256):
    S = u0p.shape[0]
    tm = min(tm, S)
    hb = tm // 16

    def body(u_ref, p_ref, w_ref, b_ref, a_ref):
        first = pl.program_id(1) == 0
        for k in range(HB // LANES):
            us = []
            for off in (k * LANES, HB + k * LANES):
                cs = slice(off, off + LANES)
                prev = jnp.where(first, 0.0, p_ref[:, cs].astype(F32))
                ext = jnp.concatenate([prev, u_ref[:, cs].astype(F32)], axis=0)
                us.append(_conv3(ext, w_ref, b_ref, cs)[0][16:])
            a_ref[:, k * LANES:(k + 1) * LANES] = (us[0] * (us[1] * _sigmoid(us[1]))).astype(BF16)

    return pl.pallas_call(
        body, name="conv_glu", grid=(2, S // tm),
        in_specs=[pl.BlockSpec((tm, 2 * HB), lambda j, i: (i, j)),
                  pl.BlockSpec((16, 2 * HB), lambda j, i: (jnp.maximum(i * hb - 1, 0), j)),
                  pl.BlockSpec((3, 2 * HB), lambda j, i: (0, j)),
                  pl.BlockSpec((1, 2 * HB), lambda j, i: (0, j))],
        out_specs=pl.BlockSpec((tm, HB), lambda j, i: (i, j)),
        out_shape=jax.ShapeDtypeStruct((S, D_FF), BF16), compiler_params=_cp(2),
    )(u0p, u0p, wc_p, bc_p)


def _conv_glu_bwd(da, u0p, wc_p, bc_p, tm=256, carried=()):
    S = u0p.shape[0]
    tm = min(tm, S)
    hb = tm // 16
    nlast = S // 16 - 1

    def body(da_ref, dan_ref, u_ref, p_ref, n_ref, w_ref, b_ref, o_ref, s_ref):
        i = pl.program_id(1)
        first = i == 0
        last = i == pl.num_programs(1) - 1

        @pl.when(first)
        def _():
            s_ref[...] = jnp.zeros_like(s_ref)

        n = tm + 16
        for k in range(HB // LANES):
            kc = slice(k * LANES, (k + 1) * LANES)
            dae = jnp.concatenate([da_ref[:, kc].astype(F32),
                                   jnp.where(last, 0.0, dan_ref[:, kc].astype(F32))], axis=0)
            halves = []
            for off in (k * LANES, HB + k * LANES):
                cs = slice(off, off + LANES)
                ext = jnp.concatenate([jnp.where(first, 0.0, p_ref[:, cs].astype(F32)),
                                       u_ref[:, cs].astype(F32), n_ref[:, cs].astype(F32)], axis=0)
                u, e1, e2 = _conv3(ext, w_ref, b_ref, cs)
                halves.append((u[16:], ext[16:16 + tm], e1[16:16 + tm], e2[16:16 + tm], cs))
            val, gt = halves[0][0], halves[1][0]
            sg = _sigmoid(gt)
            dus = (dae * (gt * sg), dae * val * (sg * (1.0 + gt * (1.0 - sg))))
            for du, (_, x0, x1, x2, cs) in zip(dus, halves):
                du0 = (w_ref[2:3, cs] * du + w_ref[1:2, cs] * pltpu.roll(du, n - 1, 0)
                       + w_ref[0:1, cs] * pltpu.roll(du, n - 2, 0))
                o_ref[:, cs] = du0[:tm].astype(BF16)
                dut = du[:tm]
                s_ref[0:1, cs] += jnp.sum(dut, axis=0, keepdims=True)
                s_ref[1:2, cs] += jnp.sum(dut * x2, axis=0, keepdims=True)
                s_ref[2:3, cs] += jnp.sum(dut * x1, axis=0, keepdims=True)
                s_ref[3:4, cs] += jnp.sum(dut * x0, axis=0, keepdims=True)

    return _call(
        body, carried, [da, da, u0p, u0p, u0p, wc_p, bc_p], name="conv_glu_bwd", grid=(2, S // tm),
        in_specs=[pl.BlockSpec((tm, HB), lambda j, i: (i, j)),
                  pl.BlockSpec((16, HB), lambda j, i: (jnp.minimum((i + 1) * hb, nlast), j)),
                  pl.BlockSpec((tm, 2 * HB), lambda j, i: (i, j)),
                  pl.BlockSpec((16, 2 * HB), lambda j, i: (jnp.maximum(i * hb - 1, 0), j)),
                  pl.BlockSpec((16, 2 * HB), lambda j, i: (jnp.minimum((i + 1) * hb, nlast), j)),
                  pl.BlockSpec((3, 2 * HB), lambda j, i: (0, j)),
                  pl.BlockSpec((1, 2 * HB), lambda j, i: (0, j))],
        out_specs=[pl.BlockSpec((tm, 2 * HB), lambda j, i: (i, j)),
                   pl.BlockSpec((8, 2 * HB), lambda j, i: (0, j))],
        out_shape=[jax.ShapeDtypeStruct((S, 2 * D_FF), BF16), jax.ShapeDtypeStruct((8, 2 * D_FF), F32)],
        compiler_params=_cp(2))


def _final_loss(x1, y2, gate2, g_final, target, tm=256):
    S = x1.shape[0]
    tm = min(tm, S)

    def body(x1_ref, y2_ref, gt_ref, g_ref, t_ref, dx_ref, dy_ref, s_ref):
        @pl.when(pl.program_id(0) == 0)
        def _():
            s_ref[...] = jnp.zeros_like(s_ref)

        y2 = y2_ref[...]
        og = 1.0 + gt_ref[...]
        x2 = x1_ref[...] + og * y2
        r = lax.rsqrt(jnp.mean(x2 * x2, axis=-1, keepdims=True) + EPS)
        n = x2 * r
        g = g_ref[...]
        err = n * g - t_ref[...]
        dy = err * (1.0 / D)
        dn = dy * g
        dx2 = r * (dn - n * jnp.mean(dn * n, axis=-1, keepdims=True))
        dx_ref[...] = dx2
        dy_ref[...] = (dx2 * og).astype(BF16)
        s_ref[0:1, :] += jnp.sum(dy * n, axis=0, keepdims=True)
        s_ref[1:2, :] += jnp.sum(dx2 * y2, axis=0, keepdims=True)
        s_ref[2:3, :] += jnp.sum(err * err, axis=0, keepdims=True)

    return pl.pallas_call(
        body, name="final_loss", grid=(S // tm,),
        in_specs=[_rows(tm), _rows(tm), _vec(), _vec(), _rows(tm)],
        out_specs=[_rows(tm), _rows(tm), pl.BlockSpec((8, D), lambda i: (0, 0))],
        out_shape=[jax.ShapeDtypeStruct((S, D), F32), jax.ShapeDtypeStruct((S, D), BF16),
                   jax.ShapeDtypeStruct((8, D), F32)],
        compiler_params=_cp(1),
    )(x1, y2, gate2, g_final, target)


def _norm_mod_bwd(dh, xin, dres, g, scale, mixed, gate, name, tm=256, carried=()):
    S = xin.shape[0]
    tm = min(tm, S)
    with_gate = mixed is not None

    def body(*refs):
        if with_gate:
            dh_ref, x_ref, dr_ref, g_ref, sc_ref, m_ref, gt_ref, dx_ref, dm_ref, s_ref = refs
        else:
            dh_ref, x_ref, dr_ref, g_ref, sc_ref, dx_ref, s_ref = refs

        @pl.when(pl.program_id(0) == 0)
        def _():
            s_ref[...] = jnp.zeros_like(s_ref)

        xv = x_ref[...]
        dhv = dh_ref[...]
        r = lax.rsqrt(jnp.mean(xv * xv, axis=-1, keepdims=True) + EPS)
        n = xv * r
        g = g_ref[...]
        hn = n * g
        dhn = dhv * (1.0 + sc_ref[...])
        dn = dhn * g
        dx = dr_ref[...] + r * (dn - n * jnp.mean(dn * n, axis=-1, keepdims=True))
        dx_ref[...] = dx
        s_ref[0:1, :] += jnp.sum(dhv, axis=0, keepdims=True)
        s_ref[1:2, :] += jnp.sum(dhv * hn, axis=0, keepdims=True)
        s_ref[2:3, :] += jnp.sum(dhn * n, axis=0, keepdims=True)
        if with_gate:
            dm_ref[...] = (dx * (1.0 + gt_ref[...])).astype(BF16)
            s_ref[3:4, :] += jnp.sum(dx * m_ref[...], axis=0, keepdims=True)

    ins = [dh, xin, dres, g, scale]
    in_specs = [_rows(tm), _rows(tm), _rows(tm), _vec(), _vec()]
    out_specs = [_rows(tm)]
    out_shape = [jax.ShapeDtypeStruct((S, D), F32)]
    if with_gate:
        ins += [mixed, gate]
        in_specs += [_rows(tm), _vec()]
        out_specs.append(_rows(tm))
        out_shape.append(jax.ShapeDtypeStruct((S, D), BF16))
    out_specs.append(pl.BlockSpec((8, D), lambda i: (0, 0)))
    out_shape.append(jax.ShapeDtypeStruct((8, D), F32))
    return _call(body, carried, ins, name=name, grid=(S // tm,), in_specs=in_specs, out_specs=out_specs,
                 out_shape=out_shape, compiler_params=_cp(1))


def _tri(n, rel):
    row = lax.broadcasted_iota(jnp.int32, (n, n), 0)
    col = lax.broadcasted_iota(jnp.int32, (n, n), 1)
    return {"gt": row > col, "ge": row >= col, "lt": row < col, "le": row <= col}[rel]


def _pair_diag(mask):
    u = jnp.where(mask, 1.0, 0.0).astype(BF16)
    z = jnp.zeros_like(u)
    return jnp.concatenate([jnp.concatenate([u, z], axis=1), jnp.concatenate([z, u], axis=1)], axis=0)


def _pair_rows(xp, lo_half):
    z = jnp.zeros_like(xp)
    return jnp.concatenate([jnp.where(lo_half, xp, z), jnp.where(lo_half, z, xp)], axis=0)


def _sb_scores(z, causal, diag):
    ls, ps, es = [], [], []
    for hh in range(2):
        zz = z[:, hh * QB:(hh + 1) * QB]
        e = jnp.exp(-jnp.abs(zz))
        l = -(jnp.maximum(zz, 0.0) + jnp.log(1.0 + e))
        ps.append(l + zz)
        ls.append(jnp.where(causal, l, 0.0) if diag else l)
        es.append(e)
    return ls, ps, es


def _sb_fwd(proj):
    S = proj.shape[0]
    nq = S // QB

    def body(q_ref, k_ref, v_ref, o_ref, t_ref, c_ref, acc_ref, qs_ref):
        i = pl.program_id(0)
        causal = _tri(QB, "gt")
        usuf = _pair_diag(_tri(QB, "gt"))
        lo_half = lax.broadcasted_iota(jnp.int32, (QB, 128), 1) < DK
        qs_ref[...] = q_ref[...] * 0.125

        def block(j, diag):
            rows = pl.ds(pl.multiple_of(j * QB, QB), QB)
            pairs = range(H_SB // 2)
            cols = [slice(pr * 128, (pr + 1) * 128) for pr in pairs]
            zs = [_dot_nt(qs_ref[:, cols[pr]], _pair_rows(k_ref[rows, cols[pr]], lo_half)) for pr in pairs]
            sc = [_sb_scores(zs[pr], causal, diag) for pr in pairs]
            sufs = []
            for pr in pairs:
                lh, ll = _split(jnp.concatenate(sc[pr][0], axis=1))
                sufs.append(_dot(lh, usuf) + _dot(ll, usuf))
            cmax = None
            wps = []
            for pr in pairs:
                ws = []
                for hh in range(2):
                    h = 2 * pr + hh
                    b = sufs[pr][:, hh * QB:(hh + 1) * QB]
                    if not diag:
                        b = b + c_ref[h, :, 0:1]
                    w = jnp.exp(sc[pr][1][hh] + b)
                    ws.append((jnp.where(causal, w, 0.0) if diag else w).astype(BF16))
                    cn = b[:, 0:1] + sc[pr][0][hh][:, 0:1]
                    c_ref[h, :, 0:1] = cn
                    cmax = cn if cmax is None else jnp.maximum(cmax, cn)
                wps.append(jnp.concatenate(ws, axis=1))
            for pr in pairs:
                upd = _dot(wps[pr], _pair_rows(v_ref[rows, cols[pr]], lo_half))
                if diag:
                    acc_ref[:, cols[pr]] = upd
                else:
                    acc_ref[:, cols[pr]] += upd
            return jnp.max(cmax)

        def cond(st):
            return jnp.logical_and(st[0] >= 0, st[1] > SB_SKIP)

        def step(st):
            return st[0] - 1, block(st[0], False)

        j, _ = lax.while_loop(cond, step, (i - 1, block(i, True)))
        o_ref[...] = acc_ref[...].astype(BF16)
        t_ref[...] = jnp.zeros_like(t_ref)
        for h in range(H_SB):
            t_ref[h // 4, :, h % 4:h % 4 + 1] = c_ref[h, :, 0:1]
        t_ref[:, :, 8:9] = jnp.zeros((2, QB, 1), F32) + (j + 1).astype(F32)

    return pl.pallas_call(
        body, name="sb_fwd", grid=(nq,),
        in_specs=[pl.BlockSpec((QB, 512), lambda i: (i, 0)),
                  pl.BlockSpec((S, 512), lambda i: (0, 1), pipeline_mode=ONE_BUF),
                  pl.BlockSpec((S, 512), lambda i: (0, 2), pipeline_mode=ONE_BUF)],
        out_specs=[pl.BlockSpec((QB, 512), lambda i: (i, 0)),
                   pl.BlockSpec((2, QB, 128), lambda i: (0, i, 0))],
        out_shape=[jax.ShapeDtypeStruct((S, 512), BF16), jax.ShapeDtypeStruct((2, S, 128), F32)],
        scratch_shapes=[pltpu.VMEM((H_SB, QB, 128), F32), pltpu.VMEM((QB, 512), F32), pltpu.VMEM((QB, 512), BF16)],
        compiler_params=_cp(1, 40),
    )(proj, proj, proj)


def _sb_bwd(proj, dcat, stats, carried=()):
    S = proj.shape[0]
    nq = S // QB

    def body(q_ref, k_ref, v_ref, do_ref, t_ref, dq_ref, dk_ref, dv_ref, dk_acc, dv_acc, dq_acc, pc_ref, qs_ref):
        i = pl.program_id(1)

        @pl.when(i == 0)
        def _():
            dk_acc[...] = jnp.zeros_like(dk_acc)
            dv_acc[...] = jnp.zeros_like(dv_acc)

        causal = _tri(QB, "gt")
        uin = _pair_diag(_tri(QB, "le"))
        uex = _pair_diag(_tri(QB, "lt"))
        lo_half = lax.broadcasted_iota(jnp.int32, (QB, 128), 1) < DK
        qs_ref[...] = q_ref[...] * 0.125
        pc_ref[...] = jnp.zeros_like(pc_ref)
        dq_acc[...] = jnp.zeros_like(dq_acc)
        jstart = jnp.max(t_ref[:, 8:9]).astype(jnp.int32)

        def block(j, diag):
            rows = pl.ds(pl.multiple_of(j * QB, QB), QB)
            pairs = range(2)
            cols = [slice(pr * 128, (pr + 1) * 128) for pr in pairs]
            kbds = [_pair_rows(k_ref[rows, cols[pr]], lo_half) for pr in pairs]
            zs = [_dot_nt(qs_ref[:, cols[pr]], kbds[pr]) for pr in pairs]
            dws = [_dot_nt(do_ref[:, cols[pr]], _pair_rows(v_ref[rows, cols[pr]], lo_half)) for pr in pairs]
            sc = [_sb_scores(zs[pr], causal, diag) for pr in pairs]
            plins = []
            for pr in pairs:
                lh, ll = _split(jnp.concatenate(sc[pr][0], axis=1))
                plins.append(_dot(lh, uin) + _dot(ll, uin))
            wss, gss, gexs = [], [], []
            for pr in pairs:
                ws, gs = [], []
                for hh in range(2):
                    h = 2 * pr + hh
                    half = slice(hh * QB, (hh + 1) * QB)
                    b = (t_ref[:, h:h + 1] - pc_ref[h, :, 0:1]) - plins[pr][:, half]
                    w = jnp.exp(sc[pr][1][hh] + b)
                    if diag:
                        w = jnp.where(causal, w, 0.0)
                    ws.append(w)
                    gs.append(dws[pr][:, half] * w)
                wss.append(ws)
                gss.append(gs)
            for pr in pairs:
                gh, gl = _split(jnp.concatenate(gss[pr], axis=1))
                gexs.append(_dot(gh, uex) + _dot(gl, uex))
            dzbs = []
            for pr in pairs:
                dzs = []
                for hh in range(2):
                    h = 2 * pr + hh
                    half = slice(hh * QB, (hh + 1) * QB)
                    e = sc[pr][2][hh]
                    r = pl.reciprocal(1.0 + e, approx=True)
                    er = e * r
                    pos = zs[pr][:, half] >= 0.0
                    gx = gexs[pr][:, half]
                    g = gss[pr][hh]
                    dz = g * jnp.where(pos, er, r) - (gx + pc_ref[4 + h, :, 0:1]) * jnp.where(pos, r, er)
                    dzs.append(jnp.where(causal, dz, 0.0) if diag else dz)
                    pc_ref[h, :, 0:1] += plins[pr][:, half][:, QB - 1:QB]
                    pc_ref[4 + h, :, 0:1] += gx[:, QB - 1:QB] + g[:, QB - 1:QB]
                dzbs.append(jnp.concatenate(dzs, axis=1).astype(BF16))
            for pr in pairs:
                dq_acc[:, cols[pr]] += _dot(dzbs[pr], kbds[pr])
                r1 = _dot_tn(dzbs[pr], qs_ref[:, cols[pr]])
                dk_acc[rows, cols[pr]] += jnp.where(lo_half, r1[:QB], r1[QB:])
                r2 = _dot_tn(jnp.concatenate(wss[pr], axis=1).astype(BF16), do_ref[:, cols[pr]])
                dv_acc[rows, cols[pr]] += jnp.where(lo_half, r2[:QB], r2[QB:])

        def step(j, carry):
            block(j, False)
            return carry

        lax.fori_loop(jstart, i, step, 0)
        block(i, True)
        dq_ref[...] = (dq_acc[...] * 0.125).astype(BF16)

        @pl.when(i == nq - 1)
        def _():
            dk_ref[...] = dk_acc[...].astype(BF16)
            dv_ref[...] = dv_acc[...].astype(BF16)

    return _call(
        body, carried, [proj, proj, proj, dcat, stats], name="sb_bwd", grid=(2, nq),
        in_specs=[pl.BlockSpec((QB, 256), lambda g, i: (i, g)),
                  pl.BlockSpec((S, 256), lambda g, i: (0, 2 + g), pipeline_mode=ONE_BUF),
                  pl.BlockSpec((S, 256), lambda g, i: (0, 4 + g), pipeline_mode=ONE_BUF),
                  pl.BlockSpec((QB, 256), lambda g, i: (i, g)),
                  pl.BlockSpec((None, QB, 128), lambda g, i: (g, i, 0))],
        out_specs=[pl.BlockSpec((QB, 256), lambda g, i: (i, g)),
                   pl.BlockSpec((S, 256), lambda g, i: (0, g)),
                   pl.BlockSpec((S, 256), lambda g, i: (0, g))],
        out_shape=[jax.ShapeDtypeStruct((S, 512), BF16)] * 3,
        scratch_shapes=[pltpu.VMEM((S, 256), F32), pltpu.VMEM((S, 256), F32), pltpu.VMEM((QB, 256), F32),
                        pltpu.VMEM((8, QB, 128), F32), pltpu.VMEM((QB, 256), BF16)],
        compiler_params=_cp(2, 56))


GLA_NC = 4
GLA_R = GLA_NC * CHUNK


def _chunk_tri(strict):
    row = lax.broadcasted_iota(jnp.int32, (GLA_R, GLA_R), 0)
    col = lax.broadcasted_iota(jnp.int32, (GLA_R, GLA_R), 1)
    m = jnp.logical_and(row // CHUNK == col // CHUNK, row > col if strict else row >= col)
    u = jnp.where(m, 1.0, 0.0).astype(BF16)
    return jnp.concatenate([u, u], axis=1)


def _per_chunk_rows(vals):
    return jnp.concatenate([jnp.broadcast_to(v, (CHUNK, v.shape[1])) for v in vals], axis=0)


def _head_blocks(st):
    row = lax.broadcasted_iota(jnp.int32, (H_GLA * DV, H_GLA * DK), 0)
    col = lax.broadcasted_iota(jnp.int32, (H_GLA * DV, H_GLA * DK), 1)
    t = jnp.concatenate([st.astype(BF16)] * H_GLA, axis=0)
    return jnp.where(row // DV == col // DK, t, jnp.zeros_like(t))


def _head_diag(big):
    head = lax.broadcasted_iota(jnp.int32, (DV, H_GLA * DK), 1) // DK
    out = big[0:DV]
    for h in range(1, H_GLA):
        out = jnp.where(head == h, big[h * DV:(h + 1) * DV], out)
    return out


def _gla_gate4(gf_ref, wfg_ref, bfg_ref):
    f = _dot(gf_ref[...], wfg_ref[...]) + bfg_ref[...]
    _, la, _ = _log_sigmoid_parts(f)
    lah, lal = _split(la * (1.0 / 16.0))
    cum = _dot(_chunk_tri(False), jnp.concatenate([lah, lal], axis=0))
    tots = [cum[(c + 1) * CHUNK - 1:(c + 1) * CHUNK, :] for c in range(GLA_NC)]
    return f, jnp.exp(_per_chunk_rows(tots) - cum), [jnp.exp(t) for t in tots]


def _gla_specs4(ns, rev):
    def ix(i):
        return ns - 1 - i if rev else i
    return [pl.BlockSpec((GLA_R, 256), lambda i: (ix(i), 6)),
            pl.BlockSpec((GLA_R, 256), lambda i: (ix(i), 7)),
            pl.BlockSpec((GLA_R, 512), lambda i: (ix(i), 4)),
            pl.BlockSpec((GLA_R, 512), lambda i: (ix(i), 5)),
            pl.BlockSpec((GLA_R, 128), lambda i: (ix(i), 24))]


def _gla_fwd(proj, wfg_p, bfg, ggla, carried=()):
    S = proj.shape[0]
    ns = S // GLA_R

    def body(q_ref, k_ref, v_ref, gg_ref, gf_ref, wfg_ref, bfg_ref, ggla_ref, o_ref, st_ref, state):
        @pl.when(pl.program_id(0) == 0)
        def _():
            state[...] = jnp.zeros_like(state)

        _, e, decs = _gla_gate4(gf_ref, wfg_ref, bfg_ref)
        kdec = (k_ref[...].astype(F32) * e).astype(BF16)
        rows = [slice(c * CHUNK, (c + 1) * CHUNK) for c in range(GLA_NC)]
        kvs = [_head_diag(_dot_tn(v_ref[rows[c], :], kdec[rows[c]])) for c in range(GLA_NC)]
        st = state[...]
        sts = []
        for c in range(GLA_NC):
            st = st * decs[c] + kvs[c]
            st_ref[c] = st
            sts.append(st)
        state[...] = st
        o = jnp.concatenate([_dot_nt(q_ref[rows[c], :] * 0.125, _head_blocks(sts[c])) for c in range(GLA_NC)], axis=0)
        for h in range(H_GLA):
            vs = slice(h * DV, (h + 1) * DV)
            oh = o[:, vs]
            ohn = oh * lax.rsqrt(jnp.mean(oh * oh, axis=-1, keepdims=True) + EPS)
            gg = gg_ref[:, vs].astype(F32)
            o_ref[:, vs] = ((ohn * ggla_ref[:, vs]) * (gg * _sigmoid(gg))).astype(BF16)

    return _call(
        body, carried, [proj, proj, proj, proj, proj, wfg_p, bfg, ggla], name="gla_fwd", grid=(ns,),
        in_specs=_gla_specs4(ns, False) + [pl.BlockSpec((128, 256), lambda i: (0, 0)),
                                           pl.BlockSpec((1, 256), lambda i: (0, 0)),
                                           pl.BlockSpec((1, 512), lambda i: (0, 0))],
        out_specs=[pl.BlockSpec((GLA_R, 512), lambda i: (i, 0)),
                   pl.BlockSpec((GLA_NC, 128, 256), lambda i: (i, 0, 0))],
        out_shape=[jax.ShapeDtypeStruct((S, 512), BF16), jax.ShapeDtypeStruct((S // CHUNK, 128, 256), F32)],
        scratch_shapes=[pltpu.VMEM((128, 256), F32)],
        compiler_params=_cp(1))


def _gla_bwd(dcat, proj, states, wfg_p, bfg, ggla, carried=()):
    S = proj.shape[0]
    ns = S // GLA_R

    def body(do_ref, q_ref, k_ref, v_ref, gg_ref, gf_ref, sc_ref, sp_ref, wfg_ref, bfg_ref, ggla_ref,
             dp_ref, s_ref, dw_ref, carry):
        sr = pl.program_id(0)

        @pl.when(sr == 0)
        def _():
            carry[...] = jnp.zeros_like(carry)
            s_ref[...] = jnp.zeros_like(s_ref)
            dw_ref[...] = jnp.zeros_like(dw_ref)

        f, e, decs = _gla_gate4(gf_ref, wfg_ref, bfg_ref)
        kf = k_ref[...].astype(F32) * e
        kdec = kf.astype(BF16)
        rows = [slice(c * CHUNK, (c + 1) * CHUNK) for c in range(GLA_NC)]
        sts = [sc_ref[c] for c in range(GLA_NC)]
        st_before = jnp.where(sr < ns - 1, sp_ref[0], 0.0)
        sbd = [_head_blocks(sts[c]) for c in range(GLA_NC)]
        qs = q_ref[...] * 0.125
        o = jnp.concatenate([_dot_nt(qs[rows[c]], sbd[c]) for c in range(GLA_NC)], axis=0)
        dobs = []
        for h in range(H_GLA):
            vs = slice(h * DV, (h + 1) * DV)
            oh = o[:, vs]
            rr = lax.rsqrt(jnp.mean(oh * oh, axis=-1, keepdims=True) + EPS)
            ohn = oh * rr
            gg = gg_ref[:, vs].astype(F32)
            sg = _sigmoid(gg)
            dout = do_ref[:, vs].astype(F32)
            gl = ggla_ref[:, vs]
            dp_ref[:, 1024 + h * DV:1024 + (h + 1) * DV] = (
                dout * (ohn * gl) * (sg * (1.0 + gg * (1.0 - sg)))).astype(BF16)
            dt1 = dout * (gg * sg)
            s_ref[0:1, vs] += jnp.sum(dt1 * ohn, axis=0, keepdims=True)
            dohn = dt1 * gl
            dobs.append((rr * (dohn - ohn * jnp.mean(dohn * ohn, axis=-1, keepdims=True))).astype(BF16))
        dob = jnp.concatenate(dobs, axis=1)
        dsout = []
        for c in range(GLA_NC):
            dp_ref[rows[c], 0:256] = (_dot(dob[rows[c]], sbd[c]) * 0.125).astype(BF16)
            dsout.append(_head_diag(_dot_tn(dob[rows[c]], qs[rows[c]])))
        g = carry[...]
        gts, ddecs = [None] * GLA_NC, [None] * GLA_NC
        for c in reversed(range(GLA_NC)):
            gts[c] = dsout[c] + g
            ddecs[c] = jnp.sum(gts[c] * (sts[c - 1] if c > 0 else st_before), axis=0, keepdims=True) * decs[c]
            g = gts[c] * decs[c]
        carry[...] = g
        dkds = []
        for c in range(GLA_NC):
            gbd = _head_blocks(gts[c])
            dkds.append(_dot(v_ref[rows[c], :], gbd))
            dp_ref[rows[c], 512:1024] = _dot_nt(kdec[rows[c]], gbd).astype(BF16)
        dkd = jnp.concatenate(dkds, axis=0)
        dp_ref[:, 256:512] = (dkd * e).astype(BF16)
        wh, wl = _split(dkd * kf)
        dla = _dot(_chunk_tri(True), jnp.concatenate([wh, wl], axis=0)) + _per_chunk_rows(ddecs)
        df = dla * _sigmoid(-f) * (1.0 / 16.0)
        dfb = df.astype(BF16)
        s_ref[1:2, 0:256] += jnp.sum(df, axis=0, keepdims=True)
        dw_ref[...] += _dot_tn(gf_ref[...], dfb)
        dp_ref[:, 1536:1664] = _dot_nt(dfb, wfg_ref[...]).astype(BF16)

    return _call(
        body, carried, [dcat, proj, proj, proj, proj, proj, states, states, wfg_p, bfg, ggla],
        name="gla_bwd", grid=(ns,),
        in_specs=[pl.BlockSpec((GLA_R, 512), lambda i: (ns - 1 - i, 1))] + _gla_specs4(ns, True) + [
            pl.BlockSpec((GLA_NC, 128, 256), lambda i: (ns - 1 - i, 0, 0)),
            pl.BlockSpec((1, 128, 256), lambda i: (jnp.maximum((ns - 1 - i) * GLA_NC - 1, 0), 0, 0)),
            pl.BlockSpec((128, 256), lambda i: (0, 0)),
            pl.BlockSpec((1, 256), lambda i: (0, 0)),
            pl.BlockSpec((1, 512), lambda i: (0, 0))],
        out_specs=[pl.BlockSpec((GLA_R, 1664), lambda i: (ns - 1 - i, 0)),
                   pl.BlockSpec((8, 512), lambda i: (0, 0)),
                   pl.BlockSpec((128, 256), lambda i: (0, 0))],
        out_shape=[jax.ShapeDtypeStruct((S, 1664), BF16), jax.ShapeDtypeStruct((8, 512), F32),
                   jax.ShapeDtypeStruct((128, 256), F32)],
        scratch_shapes=[pltpu.VMEM((128, 256), F32)],
        compiler_params=_cp(1))


def _sum_leading(a, name):
    n = a.shape[0]

    def body(a_ref, o_ref):
        acc = a_ref[0]
        for k in range(1, n):
            acc = acc + a_ref[k]
        o_ref[...] = acc

    return pl.pallas_call(
        body, name=name, out_shape=jax.ShapeDtypeStruct(a.shape[1:], F32),
        in_specs=[VMEM_SPEC], out_specs=VMEM_SPEC,
    )(a)


def _sum_chip(own, recv, name):
    R, C = own.shape
    tr, tc = _tile2d(R, C)

    def body(o_ref, r_ref, p_ref):
        acc = o_ref[...]
        for k in range(3):
            acc = acc + r_ref[k].astype(F32)
        p_ref[...] = acc

    return pl.pallas_call(
        body, name=name, grid=(R // tr, C // tc),
        in_specs=[pl.BlockSpec((tr, tc), lambda i, j: (i, j)), pl.BlockSpec((3, tr, tc), lambda i, j: (0, i, j))],
        out_specs=pl.BlockSpec((tr, tc), lambda i, j: (i, j)),
        out_shape=jax.ShapeDtypeStruct((R, C), F32), compiler_params=_cp(2),
    )(own, recv)


def _adamw(w, p, q, m, v, name):
    R, C = w.shape
    tr, tc = _tile2d(R, C, 256 * 1024)
    two = q is not None

    def body(*refs):
        if two:
            w_ref, p_ref, q_ref, m_ref, v_ref, g_out, d_out, m_out, v_out = refs
            g = p_ref[...] + q_ref[...]
        else:
            w_ref, p_ref, m_ref, v_ref, g_out, d_out, m_out, v_out = refs
            g = p_ref[...]
        m2 = B1 * m_ref[...] + (1.0 - B1) * g
        v2 = B2 * v_ref[...] + (1.0 - B2) * (g * g)
        m_hat = m2 / (1.0 - B1 ** STEP)
        v_hat = v2 / (1.0 - B2 ** STEP)
        g_out[...] = g
        d_out[...] = -LR * (m_hat / (jnp.sqrt(v_hat) + EPS_A) + WD * w_ref[...])
        m_out[...] = m2
        v_out[...] = v2

    spec = pl.BlockSpec((tr, tc), lambda i, j: (i, j))
    ins = [w, p, q, m, v] if two else [w, p, m, v]
    return pl.pallas_call(
        body, name=name, grid=(R // tr, C // tc),
        in_specs=[spec] * len(ins), out_specs=[spec] * 4,
        out_shape=[jax.ShapeDtypeStruct((R, C), F32)] * 4, compiler_params=_cp(2),
    )(*ins)


def _reduce_big(blocks, recv, cidx, name, swapped=False):
    own = lax.dynamic_index_in_dim(blocks, _slot(cidx, swapped), axis=0, keepdims=False)
    p = _sum_chip(own, recv, name + "_sum")
    return p, _pair_swap(p, name + "_swap")


def _cols_to_chips(a, width):
    return a.reshape(a.shape[0], 4, width).swapaxes(0, 1)


def _chips_to_cols(a):
    return a.swapaxes(0, 1).reshape(a.shape[1], 4 * a.shape[2])


def _swap_mid(a):
    lead = a.shape[:-1]
    return a.reshape(lead + (2, 2, HB)).swapaxes(-3, -2).reshape(lead + (4 * HB,))


def kernel(x, c, w_ada, b_ada, g_norm1, w_in, w_fg2, b_fg2, g_gla_out, w_out, g_norm2, w_up, w_conv, b_conv, w_down, g_final, loss_target, m_w_ada, m_b_ada, m_g_norm1, m_w_in, m_w_fg2, m_b_fg2, m_g_gla_out, m_w_out, m_g_norm2, m_w_up, m_w_conv, m_b_conv, m_w_down, m_g_final, v_w_ada, v_b_ada, v_g_norm1, v_w_in, v_w_fg2, v_b_fg2, v_g_gla_out, v_w_out, v_g_norm2, v_w_up, v_w_conv, v_b_conv, v_w_down, v_g_final):
    xi, yi, ci = lax.axis_index("x"), lax.axis_index("y"), lax.axis_index("c")
    cidx = 2 * xi + yi
    didx = 4 * xi + 2 * yi + ci
    xs = x[0]
    tgt = loss_target[0]
    gfin = g_final.reshape(1, D)
    AW = D * 6 // 4

    c_all = _allgather8(c, "gather_c").reshape(8, D)
    c_pad = jnp.concatenate([c_all, jnp.zeros((8, D), F32)], axis=0)
    mod_part = _ada_fwd(c_pad, w_ada[0], lax.dynamic_slice(b_ada, (0, cidx * AW), (1, AW)))[:8]
    small = jnp.concatenate([mod_part.reshape(-1), w_conv.reshape(-1), w_fg2.reshape(-1)]).reshape(-1, 128)
    small_g = _allgather4(small, "gather_small").reshape(4, -1)
    mod = lax.dynamic_index_in_dim(small_g[:, :8 * AW].reshape(4, 8, AW), didx, axis=1, keepdims=False).reshape(1, 6 * D)
    shift1, scale1, gate1, shift2, scale2, gate2 = [mod[:, k * D:(k + 1) * D] for k in range(6)]
    o1 = 8 * AW
    o2 = o1 + 3 * HB
    wc_p = _swap_mid(_chips_to_cols(small_g[:, o1:o2].reshape(4, 3, HB)))
    bc_p = _swap_mid(b_conv)
    wfg_full = _chips_to_cols(small_g[:, o2:].reshape(4, RANK, 64))
    wfg_p = jnp.concatenate([wfg_full, jnp.zeros((128 - RANK, 256), F32)], axis=0).astype(BF16)

    w_in_t = _allgather4(w_in[0].T.astype(BF16), "gather_w_in").reshape(N_IN, D)
    w_in_t = jnp.concatenate([w_in_t, jnp.zeros((N_IN_P - N_IN, D), BF16)], axis=0)
    w_in_p = w_in_t.T

    h = _norm_mod(xs, g_norm1, shift1, scale1)
    proj, (w_out_g, w_down_g) = _mm(
        [(h, w_in_p)], BF16, "mm_in", 256, N_IN_P, 48,
        carried=[("gather", w_out[0].astype(BF16), False), ("gather", w_down[0].astype(BF16), False)])
    w_out_f = w_out_g.reshape(D, D)
    w_down_f = w_down_g.reshape(D_FF, D)
    (o_gla, states), (w_up_g,) = _gla_fwd(proj, wfg_p, b_fg2, g_gla_out,
                                          carried=[("gather", w_up[0].astype(BF16), False)])
    w_up_p = _swap_mid(_chips_to_cols(w_up_g))
    o_sb, stats = _sb_fwd(proj)
    mixed =_mm([(o_sb, w_out_f[:512]), (o_gla, w_out_f[512:])], F32, "mm_out", 512, D)
    x1, h2 = _resid_norm_mod(xs, mixed, gate1, g_norm2, shift2, scale2)
    u0p = _mm([(h2, w_up_p)], BF16, "mm_up", 512, HB, 48)
    a = _conv_glu(u0p, wc_p, bc_p)
    y2 = _mm([(a, w_down_f)], F32, "mm_down", 512, D, 48)
    dx2, dy2, s_fin = _final_loss(x1, y2, gate2, gfin, tgt)
    loss = lax.psum(0.5 / D * jnp.sum(s_fin[2]), ("x", "y", "c"))

    da = _mm([(dy2, w_down_f.T)], BF16, "mm_down_t", 512, D_FF, 48)
    dw_down = _mm_tn([a], dy2, "mm_dw_down", D, 512, 56).reshape(4, D_FF // 4, D)
    (du0p, s_conv), (rc_down,) = _conv_glu_bwd(da, u0p, wc_p, bc_p,
                                               carried=[("scatter", dw_down.astype(BF16), False)])
    dh2 = _mm([(du0p, w_up_p.T)], F32, "mm_up_t", 256, D, 56)
    dw_up = _mm_tn([h2], du0p, "mm_dw_up", HB, 512, 48)
    (dx1, dmixed, s_n2), _ = _norm_mod_bwd(dh2, x1, dx2, g_norm2, scale2, mixed, gate1, "norm2_bwd")
    dcat = _mm([(dmixed, w_out_f.T)], BF16, "mm_out_t", 512, D)
    dw_out = _mm_tn([o_sb, o_gla], dmixed, "mm_dw_out", D, 512).reshape(4, D // 4, D)
    (dq, dk, dv), (rc_up,) = _sb_bwd(proj, dcat, stats, carried=[("scatter", dw_up.astype(BF16), True)])
    (dp_gla, s_gla, dwfg), (rc_out,) = _gla_bwd(dcat, proj, states, wfg_p, b_fg2, g_gla_out,
                                                carried=[("scatter", dw_out.astype(BF16), False)])
    dh = _mm([(dq, w_in_t[:512]), (dk, w_in_t[512:1024]), (dv, w_in_t[1024:1536]), (dp_gla, w_in_t[1536:])],
             F32, "mm_in_t", 256, D, 48)
    dw_in = _mm_tn([dq, dk, dv, dp_gla], h, "mm_dw_in", D, 512, 56)[0, :N_IN].reshape(4, N_IN // 4, D)
    (gx, s_n1), (rc_in,) = _norm_mod_bwd(dh, xs, dx1, g_norm1, scale1, None, None, "norm1_bwd",
                                        carried=[("scatter", dw_in.astype(BF16), False)])

    dmod = jnp.concatenate([s_n1[0], s_n1[1], s_n2[3], s_n2[0], s_n2[1], s_fin[1]])
    s_conv_n = _swap_mid(s_conv[:4])
    part = jnp.concatenate([dmod, s_n1[2], s_n2[2], s_fin[0], s_gla[0], s_gla[1, :256], s_conv_n[0],
                            s_conv_n[1:4].reshape(-1), dwfg[:RANK].reshape(-1)]).reshape(-1, 128)
    parts = _allgather8(part, "gather_small_grads")
    tot = _sum_leading(parts, "sum_small_grads").reshape(-1)
    dmod_all = parts.reshape(8, -1)[:, :6 * D]
    offs = [0]
    for n in (6 * D, D, D, D, 512, 256, 2 * D_FF, 3 * 2 * D_FF, RANK * 256):
        offs.append(offs[-1] + n)
    g_b_ada, g_g1, g_g2, g_gf, g_ggla, g_bfg, g_bconv, g_wconv_full, g_wfg_full = [
        tot[offs[k]:offs[k + 1]] for k in range(9)]
    g_wconv = lax.dynamic_index_in_dim(_cols_to_chips(g_wconv_full.reshape(3, 2 * D_FF), HB), cidx, 0, keepdims=False)
    g_wfg = lax.dynamic_index_in_dim(_cols_to_chips(g_wfg_full.reshape(RANK, 256), 64), cidx, 0, keepdims=False)

    dmod_pad = jnp.concatenate([dmod_all, jnp.zeros((8, 6 * D), F32)], axis=0)
    g_w_ada = _ada_bwd(c_pad, lax.dynamic_slice(dmod_pad, (0, cidx * AW), (16, AW)))

    p_in, q_in = _reduce_big(dw_in, rc_in, cidx, "rs_w_in")
    p_out, q_out = _reduce_big(dw_out, rc_out, cidx, "rs_w_out")
    p_up, q_up = _reduce_big(dw_up, rc_up, cidx, "rs_w_up", swapped=True)
    p_down, q_down = _reduce_big(dw_down, rc_down, cidx, "rs_w_down")

    out = {}
    out["w_ada"] = _adamw(w_ada[0], g_w_ada, None, m_w_ada[0], v_w_ada[0], "adamw_w_ada")
    out["w_in"] = [t.T for t in _adamw(w_in[0].T, p_in, q_in, m_w_in[0].T, v_w_in[0].T, "adamw_w_in")]
    out["w_out"] = _adamw(w_out[0], p_out, q_out, m_w_out[0], v_w_out[0], "adamw_w_out")
    out["w_up"] = _adamw(w_up[0], p_up, q_up, m_w_up[0], v_w_up[0], "adamw_w_up")
    out["w_down"] = _adamw(w_down[0], p_down, q_down, m_w_down[0], v_w_down[0], "adamw_w_down")
    small_names = ["b_ada", "g_norm1", "w_fg2", "b_fg2", "g_gla_out", "g_norm2", "w_conv", "b_conv", "g_final"]
    small_w = [b_ada, g_norm1, w_fg2, b_fg2, g_gla_out, g_norm2, w_conv, b_conv, g_final]
    small_m = [m_b_ada, m_g_norm1, m_w_fg2, m_b_fg2, m_g_gla_out, m_g_norm2, m_w_conv, m_b_conv, m_g_final]
    small_v = [v_b_ada, v_g_norm1, v_w_fg2, v_b_fg2, v_g_gla_out, v_g_norm2, v_w_conv, v_b_conv, v_g_final]
    small_gr = [g_b_ada, g_g1, g_wfg, g_bfg, g_ggla, g_g2, g_wconv, g_bconv, g_gf]

    def pack(arrs):
        flat = jnp.concatenate([t.reshape(-1) for t in arrs])
        return jnp.concatenate([flat, jnp.zeros((-flat.shape[0]) % 1024, F32)]).reshape(-1, 128)

    packed = _adamw(pack(small_w), pack(small_gr), None, pack(small_m), pack(small_v), "adamw_small")
    off = 0
    for nm, wt in zip(small_names, small_w):
        n = wt.size
        out[nm] = [t.reshape(-1)[off:off + n].reshape(wt.shape) for t in packed]
        off += n
    for nm in ("w_ada", "w_in", "w_out", "w_up", "w_down"):
        out[nm] = [t[None] for t in out[nm]]

    names = ["w_ada", "b_ada", "g_norm1", "w_in", "w_fg2", "b_fg2", "g_gla_out", "w_out", "g_norm2", "w_up",
             "w_conv", "b_conv", "w_down", "g_final"]
    res = [loss, gx[None]]
    for k in range(4):
        res += [out[nm][k] for nm in names]
    return tuple(res)
```

```python
import functools

import jax
import jax.numpy as jnp
from jax import lax
from jax.experimental import pallas as pl
from jax.experimental.pallas import tpu as pltpu

F32 = jnp.float32
BF16 = jnp.bfloat16
MESH = pl.DeviceIdType.MESH

D = 1024
H_SB = 8
DK = 64
DV = 128
H_GLA = 4
CHUNK = 64
RANK = 16
N_IN = 3088
N_IN_P = 3200
D_FF = 2816
HB = D_FF // 2
LANES = 128
EPS = 1e-6
QB = 128
SB_SKIP = -120.0

LR, B1, B2, EPS_A, WD, STEP = 0.001, 0.9, 0.999, 1e-08, 0.01, 10

ANY = pl.BlockSpec(memory_space=pl.ANY)
VMEM_SPEC = pl.BlockSpec(memory_space=pltpu.VMEM)
ONE_BUF = pl.Buffered(1)


def _cp(ndim=0, vmem_mb=None):
    kw = {}
    if ndim:
        kw["dimension_semantics"] = ("arbitrary",) * ndim
    if vmem_mb:
        kw["vmem_limit_bytes"] = vmem_mb * 1024 * 1024
    return pltpu.CompilerParams(**kw)


def _dot(a, b):
    return jnp.dot(a, b, preferred_element_type=F32)


def _dot_nt(a, b):
    return lax.dot_general(a, b, (((1,), (1,)), ((), ())), preferred_element_type=F32)


def _dot_tn(a, b):
    return lax.dot_general(a, b, (((0,), (0,)), ((), ())), preferred_element_type=F32)


def _split(x):
    hi = x.astype(BF16)
    lo = (x - hi.astype(F32)).astype(BF16)
    return hi, lo


def _sigmoid(x):
    return jax.nn.sigmoid(x)


def _log_sigmoid_parts(z):
    e = jnp.exp(-jnp.abs(z))
    sp = jnp.log1p(e)
    return -(jnp.maximum(z, 0.0) + sp), jnp.minimum(z, 0.0) - sp, e


def _tile2d(rows, cols, budget=512 * 1024):
    best = None
    for t in range(8, rows + 1, 8):
        if rows % t == 0 and t * cols * 4 <= budget:
            best = t
    if best is not None:
        return best, cols
    best = LANES if cols % LANES == 0 else cols
    for t in range(LANES, cols + 1, LANES):
        if cols % t == 0 and rows * t * 4 <= budget:
            best = t
    return rows, best


def _flip(v, bit):
    return 1 - v if bit else v


def _allgather8(a, name):
    def body(a_ref, o_ref, ssem, rsem, lsem):
        x, y, c = lax.axis_index("x"), lax.axis_index("y"), lax.axis_index("c")
        me = 4 * x + 2 * y + c
        loc = pltpu.make_async_copy(a_ref, o_ref.at[me], lsem)
        loc.start()
        sends = []
        for r in range(1, 8):
            peer = (_flip(x, r & 4), _flip(y, r & 2), _flip(c, r & 1))
            cp = pltpu.make_async_remote_copy(
                src_ref=a_ref, dst_ref=o_ref.at[me], send_sem=ssem.at[r - 1], recv_sem=rsem.at[r - 1],
                device_id=peer, device_id_type=MESH)
            cp.start()
            sends.append(cp)
        for r in range(1, 8):
            peer = (_flip(x, r & 4), _flip(y, r & 2), _flip(c, r & 1))
            pidx = 4 * peer[0] + 2 * peer[1] + peer[2]
            pltpu.make_async_remote_copy(
                src_ref=a_ref, dst_ref=o_ref.at[pidx], send_sem=ssem.at[r - 1], recv_sem=rsem.at[r - 1],
                device_id=peer, device_id_type=MESH).wait_recv()
        for cp in sends:
            cp.wait_send()
        loc.wait()

    return pl.pallas_call(
        body, name=name,
        out_shape=jax.ShapeDtypeStruct((8,) + a.shape, a.dtype),
        in_specs=[VMEM_SPEC], out_specs=VMEM_SPEC,
        scratch_shapes=[pltpu.SemaphoreType.DMA((7,)), pltpu.SemaphoreType.DMA((7,)), pltpu.SemaphoreType.DMA],
    )(a)


def _allgather4(a, name):
    def body(a_ref, o_ref, ssem, rsem, lsem):
        x, y, c = lax.axis_index("x"), lax.axis_index("y"), lax.axis_index("c")
        me = 2 * x + y
        loc = pltpu.make_async_copy(a_ref, o_ref.at[me], lsem)
        loc.start()
        sends = []
        for r in range(1, 4):
            peer = (_flip(x, r & 2), _flip(y, r & 1), c)
            cp = pltpu.make_async_remote_copy(
                src_ref=a_ref, dst_ref=o_ref.at[me], send_sem=ssem.at[r - 1], recv_sem=rsem.at[r - 1],
                device_id=peer, device_id_type=MESH)
            cp.start()
            sends.append(cp)
        for r in range(1, 4):
            peer = (_flip(x, r & 2), _flip(y, r & 1), c)
            pidx = 2 * peer[0] + peer[1]
            pltpu.make_async_remote_copy(
                src_ref=a_ref, dst_ref=o_ref.at[pidx], send_sem=ssem.at[r - 1], recv_sem=rsem.at[r - 1],
                device_id=peer, device_id_type=MESH).wait_recv()
        for cp in sends:
            cp.wait_send()
        loc.wait()

    return pl.pallas_call(
        body, name=name,
        out_shape=jax.ShapeDtypeStruct((4,) + a.shape, a.dtype),
        in_specs=[ANY], out_specs=ANY,
        scratch_shapes=[pltpu.SemaphoreType.DMA((3,)), pltpu.SemaphoreType.DMA((3,)), pltpu.SemaphoreType.DMA],
    )(a)


def _slot(chip, swapped):
    return 2 * (chip % 2) + chip // 2 if swapped else chip


def _pair_swap(p, name):
    def body(p_ref, o_ref, ssem, rsem):
        x, y, c = lax.axis_index("x"), lax.axis_index("y"), lax.axis_index("c")
        cp = pltpu.make_async_remote_copy(
            src_ref=p_ref, dst_ref=o_ref, send_sem=ssem, recv_sem=rsem,
            device_id=(x, y, 1 - c), device_id_type=MESH)
        cp.start()
        cp.wait()

    return pl.pallas_call(
        body, name=name,
        out_shape=jax.ShapeDtypeStruct(p.shape, p.dtype),
        in_specs=[ANY], out_specs=ANY,
        scratch_shapes=[pltpu.SemaphoreType.DMA, pltpu.SemaphoreType.DMA],
    )(p)


def _carried_copies(kind, src_ref, dst_ref, sems, swapped):
    ssem, rsem, lsem = sems
    x, y, c = lax.axis_index("x"), lax.axis_index("y"), lax.axis_index("c")
    me = 2 * x + y
    starts, recvs = [], []
    if kind == "gather":
        starts.append(pltpu.make_async_copy(src_ref, dst_ref.at[me], lsem))
    for r in range(1, 4):
        peer = (_flip(x, r & 2), _flip(y, r & 1), c)
        pidx = 2 * peer[0] + peer[1]
        if kind == "gather":
            src, dst, landed = src_ref, dst_ref.at[me], dst_ref.at[pidx]
        else:
            src = src_ref.at[2 * peer[1] + peer[0] if swapped else pidx]
            dst = landed = dst_ref.at[r - 1]
        starts.append(pltpu.make_async_remote_copy(
            src_ref=src, dst_ref=dst, send_sem=ssem.at[r - 1], recv_sem=rsem.at[r - 1],
            device_id=peer, device_id_type=MESH))
        recvs.append(pltpu.make_async_remote_copy(
            src_ref=src, dst_ref=landed, send_sem=ssem.at[r - 1], recv_sem=rsem.at[r - 1],
            device_id=peer, device_id_type=MESH))
    return starts, recvs


def _call(body, carried, operands, *, name, grid, in_specs, out_specs, out_shape, scratch_shapes=(),
          compiler_params=None):
    single = not isinstance(out_shape, (list, tuple))
    out_specs = [out_specs] if single else list(out_specs)
    out_shape = [out_shape] if single else list(out_shape)
    n_in, n_out, n_sc, nh = len(operands), len(out_shape), len(scratch_shapes), len(carried)

    def full(*refs):
        ins, h_in = refs[:n_in], refs[n_in:n_in + nh]
        o0 = n_in + nh
        outs, h_out = refs[o0:o0 + n_out], refs[o0 + n_out:o0 + n_out + nh]
        s0 = o0 + n_out + nh
        scratch, sems = refs[s0:s0 + n_sc], refs[s0 + n_sc:]
        first = last = None
        for d in range(len(grid)):
            f = pl.program_id(d) == 0
            l = pl.program_id(d) == pl.num_programs(d) - 1
            first = f if first is None else jnp.logical_and(first, f)
            last = l if last is None else jnp.logical_and(last, l)

        def copies(t):
            return _carried_copies(carried[t][0], h_in[t], h_out[t], sems[3 * t:3 * t + 3], carried[t][2])

        if nh:
            @pl.when(first)
            def _():
                for t in range(nh):
                    for cp in copies(t)[0]:
                        cp.start()

        body(*ins, *outs, *scratch)

        if nh:
            @pl.when(last)
            def _():
                for t in range(nh):
                    starts, recvs = copies(t)
                    for cp in recvs:
                        cp.wait_recv()
                    for cp in starts:
                        if carried[t][0] == "gather" and cp is starts[0]:
                            cp.wait()
                        else:
                            cp.wait_send()

    h_shapes = [jax.ShapeDtypeStruct(((4,) + arr.shape) if kind == "gather" else ((3,) + arr.shape[1:]), arr.dtype)
                for kind, arr, _ in carried]
    sem_shapes = [pltpu.SemaphoreType.DMA((3,)), pltpu.SemaphoreType.DMA((3,)), pltpu.SemaphoreType.DMA] * nh
    res = pl.pallas_call(
        full, name=name, grid=grid, in_specs=list(in_specs) + [ANY] * nh, out_specs=out_specs + [ANY] * nh,
        out_shape=out_shape + h_shapes, scratch_shapes=list(scratch_shapes) + sem_shapes,
        compiler_params=compiler_params,
    )(*operands, *[arr for _, arr, _ in carried])
    main = res[:n_out]
    return (main[0] if single else main), list(res[n_out:])


def _mm(pairs, out_dtype, name, tm, tn, vmem_mb=None, carried=()):
    S = pairs[0][0].shape[0]
    N = pairs[0][1].shape[1]
    tm = min(tm, S)
    np_ = len(pairs)

    def body(*refs):
        acc = _dot(refs[0][...], refs[1][...])
        for t in range(1, np_):
            acc = acc + _dot(refs[2 * t][...], refs[2 * t + 1][...])
        refs[-1][...] = acc.astype(refs[-1].dtype)

    in_specs, ops = [], []
    for a, w in pairs:
        in_specs += [pl.BlockSpec((tm, a.shape[1]), lambda n, i: (i, 0)),
                     pl.BlockSpec((w.shape[0], tn), lambda n, i: (0, n))]
        ops += [a, w]
    out, got = _call(
        body, carried, ops, name=name, grid=(N // tn, S // tm), in_specs=in_specs,
        out_specs=pl.BlockSpec((tm, tn), lambda n, i: (i, n)),
        out_shape=jax.ShapeDtypeStruct((S, N), out_dtype),
        compiler_params=_cp(2, vmem_mb))
    return (out, got) if carried else out


def _mm_tn(a_list, b, name, bn, tk, vmem_mb=None):
    S, N = b.shape
    ms = [a.shape[1] for a in a_list]
    M = sum(ms)
    tk = min(tk, S)
    na = len(a_list)

    def body(*refs):
        b_ref, o_ref, o16_ref = refs[na], refs[na + 1], refs[na + 2]

        @pl.when(pl.program_id(1) == 0)
        def _():
            o_ref[...] = jnp.zeros_like(o_ref)
        off = 0
        for t in range(na):
            o_ref[off:off + ms[t], :] += _dot_tn(refs[t][...], b_ref[...])
            off += ms[t]

        @pl.when(pl.program_id(1) == pl.num_programs(1) - 1)
        def _():
            o16_ref[...] = o_ref[...].astype(BF16)

    spec = pl.BlockSpec((None, M, bn), lambda n, k: (n, 0, 0))
    return pl.pallas_call(
        body, name=name, grid=(N // bn, S // tk),
        in_specs=[pl.BlockSpec((tk, m), lambda n, k: (k, 0)) for m in ms] + [pl.BlockSpec((tk, bn), lambda n, k: (k, n))],
        out_specs=[spec, spec],
        out_shape=[jax.ShapeDtypeStruct((N // bn, M, bn), F32), jax.ShapeDtypeStruct((N // bn, M, bn), BF16)],
        compiler_params=_cp(2, vmem_mb),
    )(*a_list, b)


def _ada_fwd(c_all, w_sh, b_sh):
    def body(c_ref, w_ref, b_ref, o_ref):
        cv = c_ref[...]
        sc = (cv * _sigmoid(cv)).astype(BF16)
        o_ref[...] = _dot(sc, w_ref[...].astype(BF16)) + b_ref[...]

    return pl.pallas_call(
        body, name="ada_fwd", out_shape=jax.ShapeDtypeStruct((c_all.shape[0], w_sh.shape[1]), F32),
        in_specs=[VMEM_SPEC] * 3, out_specs=VMEM_SPEC, compiler_params=_cp(0, 40),
    )(c_all, w_sh, b_sh)


def _ada_bwd(c_all, dmod_sh):
    def body(c_ref, d_ref, o_ref):
        cv = c_ref[...]
        sc = (cv * _sigmoid(cv)).astype(BF16)
        o_ref[...] = _dot_tn(sc, d_ref[...].astype(BF16))

    return pl.pallas_call(
        body, name="ada_bwd", out_shape=jax.ShapeDtypeStruct((c_all.shape[1], dmod_sh.shape[1]), F32),
        in_specs=[VMEM_SPEC] * 2, out_specs=VMEM_SPEC, compiler_params=_cp(0, 40),
    )(c_all, dmod_sh)


def _vec(tm_unused=None):
    return pl.BlockSpec((1, D), lambda i: (0, 0))


def _rows(tm, width=D):
    return pl.BlockSpec((tm, width), lambda i: (i, 0))


def _norm_mod(x, g, shift, scale, tm=512):
    S = x.shape[0]
    tm = min(tm, S)

    def body(x_ref, g_ref, sh_ref, sc_ref, h_ref):
        xv = x_ref[...]
        r = lax.rsqrt(jnp.mean(xv * xv, axis=-1, keepdims=True) + EPS)
        hn = (xv * r) * g_ref[...]
        h_ref[...] = (hn * (1.0 + sc_ref[...]) + sh_ref[...]).astype(BF16)

    return pl.pallas_call(
        body, name="norm1_mod", grid=(S // tm,),
        in_specs=[_rows(tm), _vec(), _vec(), _vec()], out_specs=_rows(tm),
        out_shape=jax.ShapeDtypeStruct((S, D), BF16), compiler_params=_cp(1),
    )(x, g, shift, scale)


def _mm_rows(pairs, tm):
    ops, specs = [], []
    for a, w in pairs:
        ops += [a, w]
        specs += [pl.BlockSpec((tm, a.shape[1]), lambda i: (i, 0)),
                  pl.BlockSpec(w.shape, lambda i: (0, 0), pipeline_mode=ONE_BUF)]
    return ops, specs


def _mm_rows_value(refs, npairs):
    acc = _dot(refs[0][...], refs[1][...])
    for t in range(1, npairs):
        acc = acc + _dot(refs[2 * t][...], refs[2 * t + 1][...])
    return acc


def _resid_norm_mod(x, mm, gate, g, shift, scale, tm=256):
    S = x.shape[0]
    tm = min(tm, S)
    skip = 2 * len(mm)

    def body(*refs):
        x_ref, gt_ref, g_ref, sh_ref, sc_ref, x1_ref, h_ref, m_ref = refs[skip:]
        mixed = _mm_rows_value(refs, len(mm))
        m_ref[...] = mixed
        x1 = x_ref[...] + (1.0 + gt_ref[...]) * mixed
        x1_ref[...] = x1
        r = lax.rsqrt(jnp.mean(x1 * x1, axis=-1, keepdims=True) + EPS)
        hn = (x1 * r) * g_ref[...]
        h_ref[...] = (hn * (1.0 + sc_ref[...]) + sh_ref[...]).astype(BF16)

    ops, specs = _mm_rows(mm, tm)
    return pl.pallas_call(
        body, name="mm_out_resid_norm2_mod", grid=(S // tm,),
        in_specs=specs + [_rows(tm), _vec(), _vec(), _vec(), _vec()],
        out_specs=[_rows(tm), _rows(tm), _rows(tm)],
        out_shape=[jax.ShapeDtypeStruct((S, D), F32), jax.ShapeDtypeStruct((S, D), BF16),
                   jax.ShapeDtypeStruct((S, D), F32)],
        compiler_params=_cp(1, 40),
    )(*ops, x, gate, g, shift, scale)


def _conv3(ext, w_ref, b_ref, cs):
    e1 = pltpu.roll(ext, 1, 0)
    e2 = pltpu.roll(ext, 2, 0)
    u = b_ref[:, cs] + w_ref[0:1, cs] * e2
    u = u + w_ref[1:2, cs] * e1
    u = u + w_ref[2:3, cs] * ext
    return u, e1, e2


def _conv_glu(u0p, wc_p, bc_p, tm=256):
    S = u0p.shape[0]
    tm = min(tm, S)
    hb = tm // 16

    def body(u_ref, p_ref, w_ref, b_ref, a_ref):
        first = pl.program_id(1) == 0
        for k in range(HB // LANES):
            us = []
            for off in (k * LANES, HB + k * LANES):
                cs = slice(off, off + LANES)
                prev = jnp.where(first, 0.0, p_ref[:, cs].astype(F32))
                ext = jnp.concatenate([prev, u_ref[:, cs].astype(F32)], axis=0)
                us.append(_conv3(ext, w_ref, b_ref, cs)[0][16:])
            a_ref[:, k * LANES:(k + 1) * LANES] = (us[0] * (us[1] * _sigmoid(us[1]))).astype(BF16)

    return pl.pallas_call(
        body, name="conv_glu", grid=(2, S // tm),
        in_specs=[pl.BlockSpec((tm, 2 * HB), lambda j, i: (i, j)),
                  pl.BlockSpec((16, 2 * HB), lambda j, i: (jnp.maximum(i * hb - 1, 0), j)),
                  pl.BlockSpec((3, 2 * HB), lambda j, i: (0, j)),
                  pl.BlockSpec((1, 2 * HB), lambda j, i: (0, j))],
        out_specs=pl.BlockSpec((tm, HB), lambda j, i: (i, j)),
        out_shape=jax.ShapeDtypeStruct((S, D_FF), BF16), compiler_params=_cp(2),
    )(u0p, u0p, wc_p, bc_p)


def _conv_glu_bwd(da, u0p, wc_p, bc_p, tm=256, carried=()):
    S = u0p.shape[0]
    tm = min(tm, S)
    hb = tm // 16
    nlast = S // 16 - 1

    def body(da_ref, dan_ref, u_ref, p_ref, n_ref, w_ref, b_ref, o_ref, s_ref):
        i = pl.program_id(1)
        first = i == 0
        last = i == pl.num_programs(1) - 1

        @pl.when(first)
        def _():
            s_ref[...] = jnp.zeros_like(s_ref)

        n = tm + 16
        for k in range(HB // LANES):
            kc = slice(k * LANES, (k + 1) * LANES)
            dae = jnp.concatenate([da_ref[:, kc].astype(F32),
                                   jnp.where(last, 0.0, dan_ref[:, kc].astype(F32))], axis=0)
            halves = []
            for off in (k * LANES, HB + k * LANES):
                cs = slice(off, off + LANES)
                ext = jnp.concatenate([jnp.where(first, 0.0, p_ref[:, cs].astype(F32)),
                                       u_ref[:, cs].astype(F32), n_ref[:, cs].astype(F32)], axis=0)
                u, e1, e2 = _conv3(ext, w_ref, b_ref, cs)
                halves.append((u[16:], ext[16:16 + tm], e1[16:16 + tm], e2[16:16 + tm], cs))
            val, gt = halves[0][0], halves[1][0]
            sg = _sigmoid(gt)
            dus = (dae * (gt * sg), dae * val * (sg * (1.0 + gt * (1.0 - sg))))
            for du, (_, x0, x1, x2, cs) in zip(dus, halves):
                du0 = (w_ref[2:3, cs] * du + w_ref[1:2, cs] * pltpu.roll(du, n - 1, 0)
                       + w_ref[0:1, cs] * pltpu.roll(du, n - 2, 0))
                o_ref[:, cs] = du0[:tm].astype(BF16)
                dut = du[:tm]
                s_ref[0:1, cs] += jnp.sum(dut, axis=0, keepdims=True)
                s_ref[1:2, cs] += jnp.sum(dut * x2, axis=0, keepdims=True)
                s_ref[2:3, cs] += jnp.sum(dut * x1, axis=0, keepdims=True)
                s_ref[3:4, cs] += jnp.sum(dut * x0, axis=0, keepdims=True)

    return _call(
        body, carried, [da, da, u0p, u0p, u0p, wc_p, bc_p], name="conv_glu_bwd", grid=(2, S // tm),
        in_specs=[pl.BlockSpec((tm, HB), lambda j, i: (i, j)),
                  pl.BlockSpec((16, HB), lambda j, i: (jnp.minimum((i + 1) * hb, nlast), j)),
                  pl.BlockSpec((tm, 2 * HB), lambda j, i: (i, j)),
                  pl.BlockSpec((16, 2 * HB), lambda j, i: (jnp.maximum(i * hb - 1, 0), j)),
                  pl.BlockSpec((16, 2 * HB), lambda j, i: (jnp.minimum((i + 1) * hb, nlast), j)),
                  pl.BlockSpec((3, 2 * HB), lambda j, i: (0, j)),
                  pl.BlockSpec((1, 2 * HB), lambda j, i: (0, j))],
        out_specs=[pl.BlockSpec((tm, 2 * HB), lambda j, i: (i, j)),
                   pl.BlockSpec((8, 2 * HB), lambda j, i: (0, j))],
        out_shape=[jax.ShapeDtypeStruct((S, 2 * D_FF), BF16), jax.ShapeDtypeStruct((8, 2 * D_FF), F32)],
        compiler_params=_cp(2))


def _final_loss(x1, mm, gate2, g_final, target, tm=256):
    S = x1.shape[0]
    tm = min(tm, S)
    skip = 2 * len(mm)

    def body(*refs):
        x1_ref, gt_ref, g_ref, t_ref, dx_ref, dy_ref, s_ref = refs[skip:]

        @pl.when(pl.program_id(0) == 0)
        def _():
            s_ref[...] = jnp.zeros_like(s_ref)

        y2 = _mm_rows_value(refs, len(mm))
        og = 1.0 + gt_ref[...]
        x2 = x1_ref[...] + og * y2
        r = lax.rsqrt(jnp.mean(x2 * x2, axis=-1, keepdims=True) + EPS)
        n = x2 * r
        g = g_ref[...]
        err = n * g - t_ref[...]
        dy = err * (1.0 / D)
        dn = dy * g
        dx2 = r * (dn - n * jnp.mean(dn * n, axis=-1, keepdims=True))
        dx_ref[...] = dx2
        dy_ref[...] = (dx2 * og).astype(BF16)
        s_ref[0:1, :] += jnp.sum(dy * n, axis=0, keepdims=True)
        s_ref[1:2, :] += jnp.sum(dx2 * y2, axis=0, keepdims=True)
        s_ref[2:3, :] += jnp.sum(err * err, axis=0, keepdims=True)

    ops, specs = _mm_rows(mm, tm)
    return pl.pallas_call(
        body, name="mm_down_final_loss", grid=(S // tm,),
        in_specs=specs + [_rows(tm), _vec(), _vec(), _rows(tm)],
        out_specs=[_rows(tm), _rows(tm), pl.BlockSpec((8, D), lambda i: (0, 0))],
        out_shape=[jax.ShapeDtypeStruct((S, D), F32), jax.ShapeDtypeStruct((S, D), BF16),
                   jax.ShapeDtypeStruct((8, D), F32)],
        compiler_params=_cp(1, 40),
    )(*ops, x1, gate2, g_final, target)


def _norm_mod_bwd(dh, xin, dres, g, scale, mixed, gate, name, tm=256, carried=()):
    S = xin.shape[0]
    tm = min(tm, S)
    with_gate = mixed is not None
    fused = isinstance(dh, list)
    skip = 2 * len(dh) if fused else 1

    def body(*refs):
        if with_gate:
            x_ref, dr_ref, g_ref, sc_ref, m_ref, gt_ref, dx_ref, dm_ref, s_ref = refs[skip:]
        else:
            x_ref, dr_ref, g_ref, sc_ref, dx_ref, s_ref = refs[skip:]

        @pl.when(pl.program_id(0) == 0)
        def _():
            s_ref[...] = jnp.zeros_like(s_ref)

        xv = x_ref[...]
        dhv = _mm_rows_value(refs, len(dh)) if fused else refs[0][...]
        r = lax.rsqrt(jnp.mean(xv * xv, axis=-1, keepdims=True) + EPS)
        n = xv * r
        g = g_ref[...]
        hn = n * g
        dhn = dhv * (1.0 + sc_ref[...])
        dn = dhn * g
        dx = dr_ref[...] + r * (dn - n * jnp.mean(dn * n, axis=-1, keepdims=True))
        dx_ref[...] = dx
        s_ref[0:1, :] += jnp.sum(dhv, axis=0, keepdims=True)
        s_ref[1:2, :] += jnp.sum(dhv * hn, axis=0, keepdims=True)
        s_ref[2:3, :] += jnp.sum(dhn * n, axis=0, keepdims=True)
        if with_gate:
            dm_ref[...] = (dx * (1.0 + gt_ref[...])).astype(BF16)
            s_ref[3:4, :] += jnp.sum(dx * m_ref[...], axis=0, keepdims=True)

    ins, in_specs = _mm_rows(dh, tm) if fused else ([dh], [_rows(tm)])
    ins += [xin, dres, g, scale]
    in_specs += [_rows(tm), _rows(tm), _vec(), _vec()]
    out_specs = [_rows(tm)]
    out_shape = [jax.ShapeDtypeStruct((S, D), F32)]
    if with_gate:
        ins += [mixed, gate]
        in_specs += [_rows(tm), _vec()]
        out_specs.append(_rows(tm))
        out_shape.append(jax.ShapeDtypeStruct((S, D), BF16))
    out_specs.append(pl.BlockSpec((8, D), lambda i: (0, 0)))
    out_shape.append(jax.ShapeDtypeStruct((8, D), F32))
    return _call(body, carried, ins, name=name, grid=(S // tm,), in_specs=in_specs, out_specs=out_specs,
                 out_shape=out_shape, compiler_params=_cp(1, 48 if fused else None))


def _tri(n, rel):
    row = lax.broadcasted_iota(jnp.int32, (n, n), 0)
    col = lax.broadcasted_iota(jnp.int32, (n, n), 1)
    return {"gt": row > col, "ge": row >= col, "lt": row < col, "le": row <= col}[rel]


def _pair_diag(mask):
    u = jnp.where(mask, 1.0, 0.0).astype(BF16)
    z = jnp.zeros_like(u)
    return jnp.concatenate([jnp.concatenate([u, z], axis=1), jnp.concatenate([z, u], axis=1)], axis=0)


def _pair_rows(xp, lo_half):
    z = jnp.zeros_like(xp)
    return jnp.concatenate([jnp.where(lo_half, xp, z), jnp.where(lo_half, z, xp)], axis=0)


def _sb_scores(z, causal, diag):
    ls, ps, es = [], [], []
    for hh in range(2):
        zz = z[:, hh * QB:(hh + 1) * QB]
        e = jnp.exp(-jnp.abs(zz))
        l = -(jnp.maximum(zz, 0.0) + jnp.log(1.0 + e))
        ps.append(l + zz)
        ls.append(jnp.where(causal, l, 0.0) if diag else l)
        es.append(e)
    return ls, ps, es


def _sb_fwd(proj, carried=()):
    S = proj.shape[0]
    nq = S // QB

    def body(q_ref, k_ref, v_ref, o_ref, t_ref, c_ref, acc_ref, qs_ref):
        i = pl.program_id(0)
        causal = _tri(QB, "gt")
        usuf = _pair_diag(_tri(QB, "gt"))
        lo_half = lax.broadcasted_iota(jnp.int32, (QB, 128), 1) < DK
        qs_ref[...] = q_ref[...] * 0.125

        def block(j, diag):
            rows = pl.ds(pl.multiple_of(j * QB, QB), QB)
            pairs = range(H_SB // 2)
            cols = [slice(pr * 128, (pr + 1) * 128) for pr in pairs]
            zs = [_dot_nt(qs_ref[:, cols[pr]], _pair_rows(k_ref[rows, cols[pr]], lo_half)) for pr in pairs]
            sc = [_sb_scores(zs[pr], causal, diag) for pr in pairs]
            sufs = []
            for pr in pairs:
                lh, ll = _split(jnp.concatenate(sc[pr][0], axis=1))
                sufs.append(_dot(lh, usuf) + _dot(ll, usuf))
            cmax = None
            wps = []
            for pr in pairs:
                ws = []
                for hh in range(2):
                    h = 2 * pr + hh
                    b = sufs[pr][:, hh * QB:(hh + 1) * QB]
                    if not diag:
                        b = b + c_ref[h, :, 0:1]
                    w = jnp.exp(sc[pr][1][hh] + b)
                    ws.append((jnp.where(causal, w, 0.0) if diag else w).astype(BF16))
                    cn = b[:, 0:1] + sc[pr][0][hh][:, 0:1]
                    c_ref[h, :, 0:1] = cn
                    cmax = cn if cmax is None else jnp.maximum(cmax, cn)
                wps.append(jnp.concatenate(ws, axis=1))
            for pr in pairs:
                upd = _dot(wps[pr], _pair_rows(v_ref[rows, cols[pr]], lo_half))
                if diag:
                    acc_ref[:, cols[pr]] = upd
                else:
                    acc_ref[:, cols[pr]] += upd
            return jnp.max(cmax)

        def cond(st):
            return jnp.logical_and(st[0] >= 0, st[1] > SB_SKIP)

        def step(st):
            return st[0] - 1, block(st[0], False)

        j, _ = lax.while_loop(cond, step, (i - 1, block(i, True)))
        o_ref[...] = acc_ref[...].astype(BF16)
        t_ref[...] = jnp.zeros_like(t_ref)
        for h in range(H_SB):
            t_ref[h // 4, :, h % 4:h % 4 + 1] = c_ref[h, :, 0:1]
        t_ref[:, :, 8:9] = jnp.zeros((2, QB, 1), F32) + (j + 1).astype(F32)

    return _call(
        body, carried, [proj, proj, proj], name="sb_fwd", grid=(nq,),
        in_specs=[pl.BlockSpec((QB, 512), lambda i: (i, 0)),
                  pl.BlockSpec((S, 512), lambda i: (0, 1), pipeline_mode=ONE_BUF),
                  pl.BlockSpec((S, 512), lambda i: (0, 2), pipeline_mode=ONE_BUF)],
        out_specs=[pl.BlockSpec((QB, 512), lambda i: (i, 0)),
                   pl.BlockSpec((2, QB, 128), lambda i: (0, i, 0))],
        out_shape=[jax.ShapeDtypeStruct((S, 512), BF16), jax.ShapeDtypeStruct((2, S, 128), F32)],
        scratch_shapes=[pltpu.VMEM((H_SB, QB, 128), F32), pltpu.VMEM((QB, 512), F32), pltpu.VMEM((QB, 512), BF16)],
        compiler_params=_cp(1, 40))


def _sb_bwd(proj, dcat, stats, carried=()):
    S = proj.shape[0]
    nq = S // QB

    def body(q_ref, k_ref, v_ref, do_ref, t_ref, dq_ref, dk_ref, dv_ref, dk_acc, dv_acc, dq_acc, pc_ref, qs_ref):
        i = pl.program_id(1)

        @pl.when(i == 0)
        def _():
            dk_acc[...] = jnp.zeros_like(dk_acc)
            dv_acc[...] = jnp.zeros_like(dv_acc)

        causal = _tri(QB, "gt")
        uin = _pair_diag(_tri(QB, "le"))
        uex = _pair_diag(_tri(QB, "lt"))
        lo_half = lax.broadcasted_iota(jnp.int32, (QB, 128), 1) < DK
        qs_ref[...] = q_ref[...] * 0.125
        pc_ref[...] = jnp.zeros_like(pc_ref)
        dq_acc[...] = jnp.zeros_like(dq_acc)
        jstart = jnp.max(t_ref[:, 8:9]).astype(jnp.int32)

        def block(j, diag):
            rows = pl.ds(pl.multiple_of(j * QB, QB), QB)
            pairs = range(2)
            cols = [slice(pr * 128, (pr + 1) * 128) for pr in pairs]
            kbds = [_pair_rows(k_ref[rows, cols[pr]], lo_half) for pr in pairs]
            zs = [_dot_nt(qs_ref[:, cols[pr]], kbds[pr]) for pr in pairs]
            dws = [_dot_nt(do_ref[:, cols[pr]], _pair_rows(v_ref[rows, cols[pr]], lo_half)) for pr in pairs]
            sc = [_sb_scores(zs[pr], causal, diag) for pr in pairs]
            plins = []
            for pr in pairs:
                lh, ll = _split(jnp.concatenate(sc[pr][0], axis=1))
                plins.append(_dot(lh, uin) + _dot(ll, uin))
            wss, gss, gexs = [], [], []
            for pr in pairs:
                ws, gs = [], []
                for hh in range(2):
                    h = 2 * pr + hh
                    half = slice(hh * QB, (hh + 1) * QB)
                    b = (t_ref[:, h:h + 1] - pc_ref[h, :, 0:1]) - plins[pr][:, half]
                    w = jnp.exp(sc[pr][1][hh] + b)
                    if diag:
                        w = jnp.where(causal, w, 0.0)
                    ws.append(w)
                    gs.append(dws[pr][:, half] * w)
                wss.append(ws)
                gss.append(gs)
            for pr in pairs:
                gh, gl = _split(jnp.concatenate(gss[pr], axis=1))
                gexs.append(_dot(gh, uex) + _dot(gl, uex))
            dzbs = []
            for pr in pairs:
                dzs = []
                for hh in range(2):
                    h = 2 * pr + hh
                    half = slice(hh * QB, (hh + 1) * QB)
                    e = sc[pr][2][hh]
                    r = pl.reciprocal(1.0 + e, approx=True)
                    er = e * r
                    pos = zs[pr][:, half] >= 0.0
                    gx = gexs[pr][:, half]
                    g = gss[pr][hh]
                    dz = g * jnp.where(pos, er, r) - (gx + pc_ref[4 + h, :, 0:1]) * jnp.where(pos, r, er)
                    dzs.append(jnp.where(causal, dz, 0.0) if diag else dz)
                    pc_ref[h, :, 0:1] += plins[pr][:, half][:, QB - 1:QB]
                    pc_ref[4 + h, :, 0:1] += gx[:, QB - 1:QB] + g[:, QB - 1:QB]
                dzbs.append(jnp.concatenate(dzs, axis=1).astype(BF16))
            for pr in pairs:
                dq_acc[:, cols[pr]] += _dot(dzbs[pr], kbds[pr])
                r1 = _dot_tn(dzbs[pr], qs_ref[:, cols[pr]])
                dk_acc[rows, cols[pr]] += jnp.where(lo_half, r1[:QB], r1[QB:])
                r2 = _dot_tn(jnp.concatenate(wss[pr], axis=1).astype(BF16), do_ref[:, cols[pr]])
                dv_acc[rows, cols[pr]] += jnp.where(lo_half, r2[:QB], r2[QB:])

        def step(j, carry):
            block(j, False)
            return carry

        lax.fori_loop(jstart, i, step, 0)
        block(i, True)
        dq_ref[...] = (dq_acc[...] * 0.125).astype(BF16)

        @pl.when(i == nq - 1)
        def _():
            dk_ref[...] = dk_acc[...].astype(BF16)
            dv_ref[...] = dv_acc[...].astype(BF16)

    return _call(
        body, carried, [proj, proj, proj, dcat, stats], name="sb_bwd", grid=(2, nq),
        in_specs=[pl.BlockSpec((QB, 256), lambda g, i: (i, g)),
                  pl.BlockSpec((S, 256), lambda g, i: (0, 2 + g), pipeline_mode=ONE_BUF),
                  pl.BlockSpec((S, 256), lambda g, i: (0, 4 + g), pipeline_mode=ONE_BUF),
                  pl.BlockSpec((QB, 256), lambda g, i: (i, g)),
                  pl.BlockSpec((None, QB, 128), lambda g, i: (g, i, 0))],
        out_specs=[pl.BlockSpec((QB, 256), lambda g, i: (i, g)),
                   pl.BlockSpec((S, 256), lambda g, i: (0, g)),
                   pl.BlockSpec((S, 256), lambda g, i: (0, g))],
        out_shape=[jax.ShapeDtypeStruct((S, 512), BF16)] * 3,
        scratch_shapes=[pltpu.VMEM((S, 256), F32), pltpu.VMEM((S, 256), F32), pltpu.VMEM((QB, 256), F32),
                        pltpu.VMEM((8, QB, 128), F32), pltpu.VMEM((QB, 256), BF16)],
        compiler_params=_cp(2, 56))


GLA_NC = 4
GLA_R = GLA_NC * CHUNK


def _chunk_tri(strict):
    row = lax.broadcasted_iota(jnp.int32, (GLA_R, GLA_R), 0)
    col = lax.broadcasted_iota(jnp.int32, (GLA_R, GLA_R), 1)
    m = jnp.logical_and(row // CHUNK == col // CHUNK, row > col if strict else row >= col)
    u = jnp.where(m, 1.0, 0.0).astype(BF16)
    return jnp.concatenate([u, u], axis=1)


def _per_chunk_rows(vals):
    return jnp.concatenate([jnp.broadcast_to(v, (CHUNK, v.shape[1])) for v in vals], axis=0)


def _head_blocks(st):
    row = lax.broadcasted_iota(jnp.int32, (H_GLA * DV, H_GLA * DK), 0)
    col = lax.broadcasted_iota(jnp.int32, (H_GLA * DV, H_GLA * DK), 1)
    t = jnp.concatenate([st.astype(BF16)] * H_GLA, axis=0)
    return jnp.where(row // DV == col // DK, t, jnp.zeros_like(t))


def _head_diag(big):
    head = lax.broadcasted_iota(jnp.int32, (DV, H_GLA * DK), 1) // DK
    out = big[0:DV]
    for h in range(1, H_GLA):
        out = jnp.where(head == h, big[h * DV:(h + 1) * DV], out)
    return out


def _gla_gate4(gf_ref, wfg_ref, bfg_ref):
    f = _dot(gf_ref[...], wfg_ref[...]) + bfg_ref[...]
    _, la, _ = _log_sigmoid_parts(f)
    lah, lal = _split(la * (1.0 / 16.0))
    cum = _dot(_chunk_tri(False), jnp.concatenate([lah, lal], axis=0))
    tots = [cum[(c + 1) * CHUNK - 1:(c + 1) * CHUNK, :] for c in range(GLA_NC)]
    return f, jnp.exp(_per_chunk_rows(tots) - cum), [jnp.exp(t) for t in tots]


def _gla_specs4(ns, rev):
    def ix(i):
        return ns - 1 - i if rev else i
    return [pl.BlockSpec((GLA_R, 256), lambda i: (ix(i), 6)),
            pl.BlockSpec((GLA_R, 256), lambda i: (ix(i), 7)),
            pl.BlockSpec((GLA_R, 512), lambda i: (ix(i), 4)),
            pl.BlockSpec((GLA_R, 512), lambda i: (ix(i), 5)),
            pl.BlockSpec((GLA_R, 128), lambda i: (ix(i), 24))]


def _gla_fwd(proj, wfg_p, bfg, ggla, carried=()):
    S = proj.shape[0]
    ns = S // GLA_R

    def body(q_ref, k_ref, v_ref, gg_ref, gf_ref, wfg_ref, bfg_ref, ggla_ref, o_ref, st_ref, state):
        @pl.when(pl.program_id(0) == 0)
        def _():
            state[...] = jnp.zeros_like(state)

        _, e, decs = _gla_gate4(gf_ref, wfg_ref, bfg_ref)
        kdec = (k_ref[...].astype(F32) * e).astype(BF16)
        rows = [slice(c * CHUNK, (c + 1) * CHUNK) for c in range(GLA_NC)]
        kvs = [_head_diag(_dot_tn(v_ref[rows[c], :], kdec[rows[c]])) for c in range(GLA_NC)]
        st = state[...]
        sts = []
        for c in range(GLA_NC):
            st = st * decs[c] + kvs[c]
            st_ref[c] = st
            sts.append(st)
        state[...] = st
        o = jnp.concatenate([_dot_nt(q_ref[rows[c], :] * 0.125, _head_blocks(sts[c])) for c in range(GLA_NC)], axis=0)
        for h in range(H_GLA):
            vs = slice(h * DV, (h + 1) * DV)
            oh = o[:, vs]
            ohn = oh * lax.rsqrt(jnp.mean(oh * oh, axis=-1, keepdims=True) + EPS)
            gg = gg_ref[:, vs].astype(F32)
            o_ref[:, vs] = ((ohn * ggla_ref[:, vs]) * (gg * _sigmoid(gg))).astype(BF16)

    return _call(
        body, carried, [proj, proj, proj, proj, proj, wfg_p, bfg, ggla], name="gla_fwd", grid=(ns,),
        in_specs=_gla_specs4(ns, False) + [pl.BlockSpec((128, 256), lambda i: (0, 0)),
                                           pl.BlockSpec((1, 256), lambda i: (0, 0)),
                                           pl.BlockSpec((1, 512), lambda i: (0, 0))],
        out_specs=[pl.BlockSpec((GLA_R, 512), lambda i: (i, 0)),
                   pl.BlockSpec((GLA_NC, 128, 256), lambda i: (i, 0, 0))],
        out_shape=[jax.ShapeDtypeStruct((S, 512), BF16), jax.ShapeDtypeStruct((S // CHUNK, 128, 256), F32)],
        scratch_shapes=[pltpu.VMEM((128, 256), F32)],
        compiler_params=_cp(1))


def _gla_bwd(dcat, proj, states, wfg_p, bfg, ggla, carried=()):
    S = proj.shape[0]
    ns = S // GLA_R

    def body(do_ref, q_ref, k_ref, v_ref, gg_ref, gf_ref, sc_ref, sp_ref, wfg_ref, bfg_ref, ggla_ref,
             dp_ref, s_ref, dw_ref, carry):
        sr = pl.program_id(0)

        @pl.when(sr == 0)
        def _():
            carry[...] = jnp.zeros_like(carry)
            s_ref[...] = jnp.zeros_like(s_ref)
            dw_ref[...] = jnp.zeros_like(dw_ref)

        f, e, decs = _gla_gate4(gf_ref, wfg_ref, bfg_ref)
        kf = k_ref[...].astype(F32) * e
        kdec = kf.astype(BF16)
        rows = [slice(c * CHUNK, (c + 1) * CHUNK) for c in range(GLA_NC)]
        sts = [sc_ref[c] for c in range(GLA_NC)]
        st_before = jnp.where(sr < ns - 1, sp_ref[0], 0.0)
        sbd = [_head_blocks(sts[c]) for c in range(GLA_NC)]
        qs = q_ref[...] * 0.125
        o = jnp.concatenate([_dot_nt(qs[rows[c]], sbd[c]) for c in range(GLA_NC)], axis=0)
        dobs = []
        for h in range(H_GLA):
            vs = slice(h * DV, (h + 1) * DV)
            oh = o[:, vs]
            rr = lax.rsqrt(jnp.mean(oh * oh, axis=-1, keepdims=True) + EPS)
            ohn = oh * rr
            gg = gg_ref[:, vs].astype(F32)
            sg = _sigmoid(gg)
            dout = do_ref[:, vs].astype(F32)
            gl = ggla_ref[:, vs]
            dp_ref[:, 1024 + h * DV:1024 + (h + 1) * DV] = (
                dout * (ohn * gl) * (sg * (1.0 + gg * (1.0 - sg)))).astype(BF16)
            dt1 = dout * (gg * sg)
            s_ref[0:1, vs] += jnp.sum(dt1 * ohn, axis=0, keepdims=True)
            dohn = dt1 * gl
            dobs.append((rr * (dohn - ohn * jnp.mean(dohn * ohn, axis=-1, keepdims=True))).astype(BF16))
        dob = jnp.concatenate(dobs, axis=1)
        dsout = []
        for c in range(GLA_NC):
            dp_ref[rows[c], 0:256] = (_dot(dob[rows[c]], sbd[c]) * 0.125).astype(BF16)
            dsout.append(_head_diag(_dot_tn(dob[rows[c]], qs[rows[c]])))
        g = carry[...]
        gts, ddecs = [None] * GLA_NC, [None] * GLA_NC
        for c in reversed(range(GLA_NC)):
            gts[c] = dsout[c] + g
            ddecs[c] = jnp.sum(gts[c] * (sts[c - 1] if c > 0 else st_before), axis=0, keepdims=True) * decs[c]
            g = gts[c] * decs[c]
        carry[...] = g
        dkds = []
        for c in range(GLA_NC):
            gbd = _head_blocks(gts[c])
            dkds.append(_dot(v_ref[rows[c], :], gbd))
            dp_ref[rows[c], 512:1024] = _dot_nt(kdec[rows[c]], gbd).astype(BF16)
        dkd = jnp.concatenate(dkds, axis=0)
        dp_ref[:, 256:512] = (dkd * e).astype(BF16)
        wh, wl = _split(dkd * kf)
        dla = _dot(_chunk_tri(True), jnp.concatenate([wh, wl], axis=0)) + _per_chunk_rows(ddecs)
        df = dla * _sigmoid(-f) * (1.0 / 16.0)
        dfb = df.astype(BF16)
        s_ref[1:2, 0:256] += jnp.sum(df, axis=0, keepdims=True)
        dw_ref[...] += _dot_tn(gf_ref[...], dfb)
        dp_ref[:, 1536:1664] = _dot_nt(dfb, wfg_ref[...]).astype(BF16)

    return _call(
        body, carried, [dcat, proj, proj, proj, proj, proj, states, states, wfg_p, bfg, ggla],
        name="gla_bwd", grid=(ns,),
        in_specs=[pl.BlockSpec((GLA_R, 512), lambda i: (ns - 1 - i, 1))] + _gla_specs4(ns, True) + [
            pl.BlockSpec((GLA_NC, 128, 256), lambda i: (ns - 1 - i, 0, 0)),
            pl.BlockSpec((1, 128, 256), lambda i: (jnp.maximum((ns - 1 - i) * GLA_NC - 1, 0), 0, 0)),
            pl.BlockSpec((128, 256), lambda i: (0, 0)),
            pl.BlockSpec((1, 256), lambda i: (0, 0)),
            pl.BlockSpec((1, 512), lambda i: (0, 0))],
        out_specs=[pl.BlockSpec((GLA_R, 1664), lambda i: (ns - 1 - i, 0)),
                   pl.BlockSpec((8, 512), lambda i: (0, 0)),
                   pl.BlockSpec((128, 256), lambda i: (0, 0))],
        out_shape=[jax.ShapeDtypeStruct((S, 1664), BF16), jax.ShapeDtypeStruct((8, 512), F32),
                   jax.ShapeDtypeStruct((128, 256), F32)],
        scratch_shapes=[pltpu.VMEM((128, 256), F32)],
        compiler_params=_cp(1))


def _sum_leading(a, name):
    n = a.shape[0]

    def body(a_ref, o_ref):
        acc = a_ref[0]
        for k in range(1, n):
            acc = acc + a_ref[k]
        o_ref[...] = acc

    return pl.pallas_call(
        body, name=name, out_shape=jax.ShapeDtypeStruct(a.shape[1:], F32),
        in_specs=[VMEM_SPEC], out_specs=VMEM_SPEC,
    )(a)


def _sum_chip(own, recv, name):
    R, C = own.shape
    tr, tc = _tile2d(R, C)

    def body(o_ref, r_ref, p_ref):
        acc = o_ref[...]
        for k in range(3):
            acc = acc + r_ref[k].astype(F32)
        p_ref[...] = acc

    return pl.pallas_call(
        body, name=name, grid=(R // tr, C // tc),
        in_specs=[pl.BlockSpec((tr, tc), lambda i, j: (i, j)), pl.BlockSpec((3, tr, tc), lambda i, j: (0, i, j))],
        out_specs=pl.BlockSpec((tr, tc), lambda i, j: (i, j)),
        out_shape=jax.ShapeDtypeStruct((R, C), F32), compiler_params=_cp(2),
    )(own, recv)


def _adamw(w, p, q, m, v, name):
    R, C = w.shape
    tr, tc = _tile2d(R, C, 256 * 1024)
    two = q is not None

    def body(*refs):
        if two:
            w_ref, p_ref, q_ref, m_ref, v_ref, g_out, d_out, m_out, v_out = refs
            g = p_ref[...] + q_ref[...]
        else:
            w_ref, p_ref, m_ref, v_ref, g_out, d_out, m_out, v_out = refs
            g = p_ref[...]
        m2 = B1 * m_ref[...] + (1.0 - B1) * g
        v2 = B2 * v_ref[...] + (1.0 - B2) * (g * g)
        m_hat = m2 / (1.0 - B1 ** STEP)
        v_hat = v2 / (1.0 - B2 ** STEP)
        g_out[...] = g
        d_out[...] = -LR * (m_hat / (jnp.sqrt(v_hat) + EPS_A) + WD * w_ref[...])
        m_out[...] = m2
        v_out[...] = v2

    spec = pl.BlockSpec((tr, tc), lambda i, j: (i, j))
    ins = [w, p, q, m, v] if two else [w, p, m, v]
    return pl.pallas_call(
        body, name=name, grid=(R // tr, C // tc),
        in_specs=[spec] * len(ins), out_specs=[spec] * 4,
        out_shape=[jax.ShapeDtypeStruct((R, C), F32)] * 4, compiler_params=_cp(2),
    )(*ins)


def _reduce_big(own, recv, name):
    p = _sum_chip(own, recv, name + "_sum")
    return p, _pair_swap(p, name + "_swap")


def _cols_to_chips(a, width):
    return a.reshape(a.shape[0], 4, width).swapaxes(0, 1)


def _chips_to_cols(a):
    return a.swapaxes(0, 1).reshape(a.shape[1], 4 * a.shape[2])


def _swap_mid(a):
    lead = a.shape[:-1]
    return a.reshape(lead + (2, 2, HB)).swapaxes(-3, -2).reshape(lead + (4 * HB,))


def kernel(x, c, w_ada, b_ada, g_norm1, w_in, w_fg2, b_fg2, g_gla_out, w_out, g_norm2, w_up, w_conv, b_conv, w_down, g_final, loss_target, m_w_ada, m_b_ada, m_g_norm1, m_w_in, m_w_fg2, m_b_fg2, m_g_gla_out, m_w_out, m_g_norm2, m_w_up, m_w_conv, m_b_conv, m_w_down, m_g_final, v_w_ada, v_b_ada, v_g_norm1, v_w_in, v_w_fg2, v_b_fg2, v_g_gla_out, v_w_out, v_g_norm2, v_w_up, v_w_conv, v_b_conv, v_w_down, v_g_final):
    xi, yi, ci = lax.axis_index("x"), lax.axis_index("y"), lax.axis_index("c")
    cidx = 2 * xi + yi
    didx = 4 * xi + 2 * yi + ci
    xs = x[0]
    tgt = loss_target[0]
    gfin = g_final.reshape(1, D)
    AW = D * 6 // 4

    c_all = _allgather8(c, "gather_c").reshape(8, D)
    c_pad = jnp.concatenate([c_all, jnp.zeros((8, D), F32)], axis=0)
    mod_part = _ada_fwd(c_pad, w_ada[0], lax.dynamic_slice(b_ada, (0, cidx * AW), (1, AW)))[:8]
    small = jnp.concatenate([mod_part.reshape(-1), w_conv.reshape(-1), w_fg2.reshape(-1)]).reshape(-1, 128)
    small_g = _allgather4(small, "gather_small").reshape(4, -1)
    mod = lax.dynamic_index_in_dim(small_g[:, :8 * AW].reshape(4, 8, AW), didx, axis=1, keepdims=False).reshape(1, 6 * D)
    shift1, scale1, gate1, shift2, scale2, gate2 = [mod[:, k * D:(k + 1) * D] for k in range(6)]
    o1 = 8 * AW
    o2 = o1 + 3 * HB
    wc_p = _swap_mid(_chips_to_cols(small_g[:, o1:o2].reshape(4, 3, HB)))
    bc_p = _swap_mid(b_conv)
    wfg_full = _chips_to_cols(small_g[:, o2:].reshape(4, RANK, 64))
    wfg_p = jnp.concatenate([wfg_full, jnp.zeros((128 - RANK, 256), F32)], axis=0).astype(BF16)

    w_in_t = _allgather4(w_in[0].T.astype(BF16), "gather_w_in").reshape(N_IN, D)
    w_in_t = jnp.concatenate([w_in_t, jnp.zeros((N_IN_P - N_IN, D), BF16)], axis=0)
    w_in_p = w_in_t.T

    h = _norm_mod(xs, g_norm1, shift1, scale1)
    proj, (w_down_g,) = _mm([(h, w_in_p)], BF16, "mm_in", 256, N_IN_P, 48,
                            carried=[("gather", w_down[0].astype(BF16), False)])
    w_down_f = w_down_g.reshape(D_FF, D)
    (o_gla, states), (w_out_g,) = _gla_fwd(proj, wfg_p, b_fg2, g_gla_out,
                                           carried=[("gather", w_out[0].astype(BF16), False)])
    w_out_f = w_out_g.reshape(D, D)
    (o_sb, stats), (w_up_g,) = _sb_fwd(proj, carried=[("gather", w_up[0].astype(BF16), False)])
    w_up_p = _swap_mid(_chips_to_cols(w_up_g))
    x1, h2, mixed = _resid_norm_mod(xs, [(o_sb, w_out_f[:512]), (o_gla, w_out_f[512:])],
                                    gate1, g_norm2, shift2, scale2)
    u0p = _mm([(h2, w_up_p)], BF16, "mm_up", 512, HB, 48)
    a = _conv_glu(u0p, wc_p, bc_p)
    dx2, dy2, s_fin = _final_loss(x1, [(a, w_down_f)], gate2, gfin, tgt)
    loss = lax.psum(0.5 / D * jnp.sum(s_fin[2]), ("x", "y", "c"))

    da = _mm([(dy2, w_down_f.T)], BF16, "mm_down_t", 512, D_FF, 48)
    dw_down, dw_down_h = [t.reshape(4, D_FF // 4, D) for t in _mm_tn([a], dy2, "mm_dw_down", D, 512, 56)]
    (du0p, s_conv), (rc_down,) = _conv_glu_bwd(da, u0p, wc_p, bc_p, carried=[("scatter", dw_down_h, False)])
    dw_up, dw_up_h = _mm_tn([h2], du0p, "mm_dw_up", HB, 512, 48)
    (dx1, dmixed, s_n2), _ = _norm_mod_bwd([(du0p, w_up_p.T)], x1, dx2, g_norm2, scale2, mixed, gate1,
                                           "mm_up_t_norm2_bwd")
    dcat = _mm([(dmixed, w_out_f.T)], BF16, "mm_out_t", 512, D)
    dw_out, dw_out_h = [t.reshape(4, D // 4, D) for t in _mm_tn([o_sb, o_gla], dmixed, "mm_dw_out", D, 512)]
    (dq, dk, dv), (rc_up,) = _sb_bwd(proj, dcat, stats, carried=[("scatter", dw_up_h, True)])
    (dp_gla, s_gla, dwfg), (rc_out,) = _gla_bwd(dcat, proj, states, wfg_p, b_fg2, g_gla_out,
                                                carried=[("scatter", dw_out_h, False)])
    dw_in, dw_in_h = _mm_tn([dq, dk, dv, dp_gla], h, "mm_dw_in", D, 512, 56)
    dw_in_h = dw_in_h[0, :N_IN].reshape(4, N_IN // 4, D)
    dw_in_own = lax.dynamic_slice(dw_in[0], (cidx * (N_IN // 4), 0), (N_IN // 4, D))
    dh, (rc_in,) = _mm(
        [(dq, w_in_t[:512]), (dk, w_in_t[512:1024]), (dv, w_in_t[1024:1536]), (dp_gla, w_in_t[1536:])],
        F32, "mm_in_t", 256, D, 48, carried=[("scatter", dw_in_h, False)])
    (gx, s_n1), _ = _norm_mod_bwd(dh, xs, dx1, g_norm1, scale1, None, None, "norm1_bwd")

    dmod = jnp.concatenate([s_n1[0], s_n1[1], s_n2[3], s_n2[0], s_n2[1], s_fin[1]])
    s_conv_n = _swap_mid(s_conv[:4])
    part = jnp.concatenate([dmod, s_n1[2], s_n2[2], s_fin[0], s_gla[0], s_gla[1, :256], s_conv_n[0],
                            s_conv_n[1:4].reshape(-1), dwfg[:RANK].reshape(-1)]).reshape(-1, 128)
    parts = _allgather8(part, "gather_small_grads")
    tot = _sum_leading(parts, "sum_small_grads").reshape(-1)
    dmod_all = parts.reshape(8, -1)[:, :6 * D]
    offs = [0]
    for n in (6 * D, D, D, D, 512, 256, 2 * D_FF, 3 * 2 * D_FF, RANK * 256):
        offs.append(offs[-1] + n)
    g_b_ada, g_g1, g_g2, g_gf, g_ggla, g_bfg, g_bconv, g_wconv_full, g_wfg_full = [
        tot[offs[k]:offs[k + 1]] for k in range(9)]
    g_wconv = lax.dynamic_index_in_dim(_cols_to_chips(g_wconv_full.reshape(3, 2 * D_FF), HB), cidx, 0, keepdims=False)
    g_wfg = lax.dynamic_index_in_dim(_cols_to_chips(g_wfg_full.reshape(RANK, 256), 64), cidx, 0, keepdims=False)

    dmod_pad = jnp.concatenate([dmod_all, jnp.zeros((8, 6 * D), F32)], axis=0)
    g_w_ada = _ada_bwd(c_pad, lax.dynamic_slice(dmod_pad, (0, cidx * AW), (16, AW)))

    def own(blocks, swapped=False):
        return lax.dynamic_index_in_dim(blocks, _slot(cidx, swapped), axis=0, keepdims=False)

    p_in, q_in = _reduce_big(dw_in_own, rc_in, "rs_w_in")
    p_out, q_out = _reduce_big(own(dw_out), rc_out, "rs_w_out")
    p_up, q_up = _reduce_big(own(dw_up, True), rc_up, "rs_w_up")
    p_down, q_down = _reduce_big(own(dw_down), rc_down, "rs_w_down")

    out = {}
    out["w_ada"] = _adamw(w_ada[0], g_w_ada, None, m_w_ada[0], v_w_ada[0], "adamw_w_ada")
    out["w_in"] = [t.T for t in _adamw(w_in[0].T, p_in, q_in, m_w_in[0].T, v_w_in[0].T, "adamw_w_in")]
    out["w_out"] = _adamw(w_out[0], p_out, q_out, m_w_out[0], v_w_out[0], "adamw_w_out")
    out["w_up"] = _adamw(w_up[0], p_up, q_up, m_w_up[0], v_w_up[0], "adamw_w_up")
    out["w_down"] = _adamw(w_down[0], p_down, q_down, m_w_down[0], v_w_down[0], "adamw_w_down")
    small_names = ["b_ada", "g_norm1", "w_fg2", "b_fg2", "g_gla_out", "g_norm2", "w_conv", "b_conv", "g_final"]
    small_w = [b_ada, g_norm1, w_fg2, b_fg2, g_gla_out, g_norm2, w_conv, b_conv, g_final]
    small_m = [m_b_ada, m_g_norm1, m_w_fg2, m_b_fg2, m_g_gla_out, m_g_norm2, m_w_conv, m_b_conv, m_g_final]
    small_v = [v_b_ada, v_g_norm1, v_w_fg2, v_b_fg2, v_g_gla_out, v_g_norm2, v_w_conv, v_b_conv, v_g_final]
    small_gr = [g_b_ada, g_g1, g_wfg, g_bfg, g_ggla, g_g2, g_wconv, g_bconv, g_gf]

    def pack(arrs):
        flat = jnp.concatenate([t.reshape(-1) for t in arrs])
        return jnp.concatenate([flat, jnp.zeros((-flat.shape[0]) % 1024, F32)]).reshape(-1, 128)

    packed = _adamw(pack(small_w), pack(small_gr), None, pack(small_m), pack(small_v), "adamw_small")
    off = 0
    for nm, wt in zip(small_names, small_w):
        n = wt.size
        out[nm] = [t.reshape(-1)[off:off + n].reshape(wt.shape) for t in packed]
        off += n
    for nm in ("w_ada", "w_in", "w_out", "w_up", "w_down"):
        out[nm] = [t[None] for t in out[nm]]

    names = ["w_ada", "b_ada", "g_norm1", "w_in", "w_fg2", "b_fg2", "g_gla_out", "w_out", "g_norm2", "w_up",
             "w_conv", "b_conv", "w_down", "g_final"]
    res = [loss, gx[None]]
    for k in range(4):
        res += [out[nm][k] for nm in names]
    return tuple(res)
```

```python
import functools

import jax
import jax.numpy as jnp
from jax import lax
from jax.experimental import pallas as pl
from jax.experimental.pallas import tpu as pltpu

F32 = jnp.float32
BF16 = jnp.bfloat16
MESH = pl.DeviceIdType.MESH

D = 1024
H_SB = 8
DK = 64
DV = 128
H_GLA = 4
CHUNK = 64
RANK = 16
N_IN = 3088
N_IN_P = 3200
D_FF = 2816
HB = D_FF // 2
LANES = 128
EPS = 1e-6
QB = 128
SB_SKIP = -120.0
SB_HEAD_ROWS = 64

LR, B1, B2, EPS_A, WD, STEP = 0.001, 0.9, 0.999, 1e-08, 0.01, 10

ANY = pl.BlockSpec(memory_space=pl.ANY)
VMEM_SPEC = pl.BlockSpec(memory_space=pltpu.VMEM)
ONE_BUF = pl.Buffered(1)


def _cp(ndim=0, vmem_mb=None):
    kw = {}
    if ndim:
        kw["dimension_semantics"] = ("arbitrary",) * ndim
    if vmem_mb:
        kw["vmem_limit_bytes"] = vmem_mb * 1024 * 1024
    return pltpu.CompilerParams(**kw)


def _dot(a, b):
    return jnp.dot(a, b, preferred_element_type=F32)


def _dot_nt(a, b):
    return lax.dot_general(a, b, (((1,), (1,)), ((), ())), preferred_element_type=F32)


def _dot_tn(a, b):
    return lax.dot_general(a, b, (((0,), (0,)), ((), ())), preferred_element_type=F32)


def _split(x):
    hi = x.astype(BF16)
    lo = (x - hi.astype(F32)).astype(BF16)
    return hi, lo


def _sigmoid(x):
    return jax.nn.sigmoid(x)


def _sigmoid_fast(x):
    return pl.reciprocal(1.0 + jnp.exp(-x), approx=True)


def _log_sigmoid_parts(z):
    e = jnp.exp(-jnp.abs(z))
    sp = jnp.log1p(e)
    return -(jnp.maximum(z, 0.0) + sp), jnp.minimum(z, 0.0) - sp, e


def _tile2d(rows, cols, budget=512 * 1024):
    best = None
    for t in range(8, rows + 1, 8):
        if rows % t == 0 and t * cols * 4 <= budget:
            best = t
    if best is not None:
        return best, cols
    best = LANES if cols % LANES == 0 else cols
    for t in range(LANES, cols + 1, LANES):
        if cols % t == 0 and rows * t * 4 <= budget:
            best = t
    return rows, best


def _flip(v, bit):
    return 1 - v if bit else v


def _allgather8(a, name):
    def body(a_ref, o_ref, ssem, rsem, lsem):
        x, y, c = lax.axis_index("x"), lax.axis_index("y"), lax.axis_index("c")
        me = 4 * x + 2 * y + c
        loc = pltpu.make_async_copy(a_ref, o_ref.at[me], lsem)
        loc.start()
        sends = []
        for r in range(1, 8):
            peer = (_flip(x, r & 4), _flip(y, r & 2), _flip(c, r & 1))
            cp = pltpu.make_async_remote_copy(
                src_ref=a_ref, dst_ref=o_ref.at[me], send_sem=ssem.at[r - 1], recv_sem=rsem.at[r - 1],
                device_id=peer, device_id_type=MESH)
            cp.start()
            sends.append(cp)
        for r in range(1, 8):
            peer = (_flip(x, r & 4), _flip(y, r & 2), _flip(c, r & 1))
            pidx = 4 * peer[0] + 2 * peer[1] + peer[2]
            pltpu.make_async_remote_copy(
                src_ref=a_ref, dst_ref=o_ref.at[pidx], send_sem=ssem.at[r - 1], recv_sem=rsem.at[r - 1],
                device_id=peer, device_id_type=MESH).wait_recv()
        for cp in sends:
            cp.wait_send()
        loc.wait()

    return pl.pallas_call(
        body, name=name,
        out_shape=jax.ShapeDtypeStruct((8,) + a.shape, a.dtype),
        in_specs=[VMEM_SPEC], out_specs=VMEM_SPEC,
        scratch_shapes=[pltpu.SemaphoreType.DMA((7,)), pltpu.SemaphoreType.DMA((7,)), pltpu.SemaphoreType.DMA],
    )(a)


def _allgather4(a, name):
    def body(a_ref, o_ref, ssem, rsem, lsem):
        x, y, c = lax.axis_index("x"), lax.axis_index("y"), lax.axis_index("c")
        me = 2 * x + y
        loc = pltpu.make_async_copy(a_ref, o_ref.at[me], lsem)
        loc.start()
        sends = []
        for r in range(1, 4):
            peer = (_flip(x, r & 2), _flip(y, r & 1), c)
            cp = pltpu.make_async_remote_copy(
                src_ref=a_ref, dst_ref=o_ref.at[me], send_sem=ssem.at[r - 1], recv_sem=rsem.at[r - 1],
                device_id=peer, device_id_type=MESH)
            cp.start()
            sends.append(cp)
        for r in range(1, 4):
            peer = (_flip(x, r & 2), _flip(y, r & 1), c)
            pidx = 2 * peer[0] + peer[1]
            pltpu.make_async_remote_copy(
                src_ref=a_ref, dst_ref=o_ref.at[pidx], send_sem=ssem.at[r - 1], recv_sem=rsem.at[r - 1],
                device_id=peer, device_id_type=MESH).wait_recv()
        for cp in sends:
            cp.wait_send()
        loc.wait()

    return pl.pallas_call(
        body, name=name,
        out_shape=jax.ShapeDtypeStruct((4,) + a.shape, a.dtype),
        in_specs=[ANY], out_specs=ANY,
        scratch_shapes=[pltpu.SemaphoreType.DMA((3,)), pltpu.SemaphoreType.DMA((3,)), pltpu.SemaphoreType.DMA],
    )(a)


def _slot(chip, swapped):
    return 2 * (chip % 2) + chip // 2 if swapped else chip


def _pair_swap(p, name):
    def body(p_ref, o_ref, ssem, rsem):
        x, y, c = lax.axis_index("x"), lax.axis_index("y"), lax.axis_index("c")
        cp = pltpu.make_async_remote_copy(
            src_ref=p_ref, dst_ref=o_ref, send_sem=ssem, recv_sem=rsem,
            device_id=(x, y, 1 - c), device_id_type=MESH)
        cp.start()
        cp.wait()

    return pl.pallas_call(
        body, name=name,
        out_shape=jax.ShapeDtypeStruct(p.shape, p.dtype),
        in_specs=[ANY], out_specs=ANY,
        scratch_shapes=[pltpu.SemaphoreType.DMA, pltpu.SemaphoreType.DMA],
    )(p)


def _carried_copies(kind, src_ref, dst_ref, sems, swapped):
    ssem, rsem, lsem = sems
    x, y, c = lax.axis_index("x"), lax.axis_index("y"), lax.axis_index("c")
    me = 2 * x + y
    starts, recvs = [], []
    if kind == "gather":
        starts.append(pltpu.make_async_copy(src_ref, dst_ref.at[me], lsem))
    for r in range(1, 4):
        peer = (_flip(x, r & 2), _flip(y, r & 1), c)
        pidx = 2 * peer[0] + peer[1]
        if kind == "gather":
            src, dst, landed = src_ref, dst_ref.at[me], dst_ref.at[pidx]
        else:
            src = src_ref.at[2 * peer[1] + peer[0] if swapped else pidx]
            dst = landed = dst_ref.at[r - 1]
        starts.append(pltpu.make_async_remote_copy(
            src_ref=src, dst_ref=dst, send_sem=ssem.at[r - 1], recv_sem=rsem.at[r - 1],
            device_id=peer, device_id_type=MESH))
        recvs.append(pltpu.make_async_remote_copy(
            src_ref=src, dst_ref=landed, send_sem=ssem.at[r - 1], recv_sem=rsem.at[r - 1],
            device_id=peer, device_id_type=MESH))
    return starts, recvs


def _call(body, carried, operands, *, name, grid, in_specs, out_specs, out_shape, scratch_shapes=(),
          compiler_params=None):
    single = not isinstance(out_shape, (list, tuple))
    out_specs = [out_specs] if single else list(out_specs)
    out_shape = [out_shape] if single else list(out_shape)
    n_in, n_out, n_sc, nh = len(operands), len(out_shape), len(scratch_shapes), len(carried)

    def full(*refs):
        ins, h_in = refs[:n_in], refs[n_in:n_in + nh]
        o0 = n_in + nh
        outs, h_out = refs[o0:o0 + n_out], refs[o0 + n_out:o0 + n_out + nh]
        s0 = o0 + n_out + nh
        scratch, sems = refs[s0:s0 + n_sc], refs[s0 + n_sc:]
        first = last = None
        for d in range(len(grid)):
            f = pl.program_id(d) == 0
            l = pl.program_id(d) == pl.num_programs(d) - 1
            first = f if first is None else jnp.logical_and(first, f)
            last = l if last is None else jnp.logical_and(last, l)

        def copies(t):
            return _carried_copies(carried[t][0], h_in[t], h_out[t], sems[3 * t:3 * t + 3], carried[t][2])

        if nh:
            @pl.when(first)
            def _():
                for t in range(nh):
                    for cp in copies(t)[0]:
                        cp.start()

        body(*ins, *outs, *scratch)

        if nh:
            @pl.when(last)
            def _():
                for t in range(nh):
                    starts, recvs = copies(t)
                    for cp in recvs:
                        cp.wait_recv()
                    for cp in starts:
                        if carried[t][0] == "gather" and cp is starts[0]:
                            cp.wait()
                        else:
                            cp.wait_send()

    h_shapes = [jax.ShapeDtypeStruct(((4,) + arr.shape) if kind == "gather" else ((3,) + arr.shape[1:]), arr.dtype)
                for kind, arr, _ in carried]
    sem_shapes = [pltpu.SemaphoreType.DMA((3,)), pltpu.SemaphoreType.DMA((3,)), pltpu.SemaphoreType.DMA] * nh
    res = pl.pallas_call(
        full, name=name, grid=grid, in_specs=list(in_specs) + [ANY] * nh, out_specs=out_specs + [ANY] * nh,
        out_shape=out_shape + h_shapes, scratch_shapes=list(scratch_shapes) + sem_shapes,
        compiler_params=compiler_params,
    )(*operands, *[arr for _, arr, _ in carried])
    main = res[:n_out]
    return (main[0] if single else main), list(res[n_out:])


def _mm(pairs, out_dtype, name, tm, tn, vmem_mb=None, carried=()):
    S = pairs[0][0].shape[0]
    N = pairs[0][1].shape[1]
    tm = min(tm, S)
    np_ = len(pairs)

    def body(*refs):
        acc = _dot(refs[0][...], refs[1][...])
        for t in range(1, np_):
            acc = acc + _dot(refs[2 * t][...], refs[2 * t + 1][...])
        refs[-1][...] = acc.astype(refs[-1].dtype)

    in_specs, ops = [], []
    for a, w in pairs:
        in_specs += [pl.BlockSpec((tm, a.shape[1]), lambda n, i: (i, 0)),
                     pl.BlockSpec((w.shape[0], tn), lambda n, i: (0, n))]
        ops += [a, w]
    out, got = _call(
        body, carried, ops, name=name, grid=(N // tn, S // tm), in_specs=in_specs,
        out_specs=pl.BlockSpec((tm, tn), lambda n, i: (i, n)),
        out_shape=jax.ShapeDtypeStruct((S, N), out_dtype),
        compiler_params=_cp(2, vmem_mb))
    return (out, got) if carried else out


def _mm_tn(a_list, b, name, bn, tk, vmem_mb=None):
    S, N = b.shape
    ms = [a.shape[1] for a in a_list]
    M = sum(ms)
    tk = min(tk, S)
    na = len(a_list)

    def body(*refs):
        b_ref, o_ref, o16_ref = refs[na], refs[na + 1], refs[na + 2]

        @pl.when(pl.program_id(1) == 0)
        def _():
            o_ref[...] = jnp.zeros_like(o_ref)
        off = 0
        for t in range(na):
            o_ref[off:off + ms[t], :] += _dot_tn(refs[t][...], b_ref[...])
            off += ms[t]

        @pl.when(pl.program_id(1) == pl.num_programs(1) - 1)
        def _():
            o16_ref[...] = o_ref[...].astype(BF16)

    spec = pl.BlockSpec((None, M, bn), lambda n, k: (n, 0, 0))
    return pl.pallas_call(
        body, name=name, grid=(N // bn, S // tk),
        in_specs=[pl.BlockSpec((tk, m), lambda n, k: (k, 0)) for m in ms] + [pl.BlockSpec((tk, bn), lambda n, k: (k, n))],
        out_specs=[spec, spec],
        out_shape=[jax.ShapeDtypeStruct((N // bn, M, bn), F32), jax.ShapeDtypeStruct((N // bn, M, bn), BF16)],
        compiler_params=_cp(2, vmem_mb),
    )(*a_list, b)


def _ada_fwd(c_all, w_sh, b_sh):
    def body(c_ref, w_ref, b_ref, o_ref):
        cv = c_ref[...]
        sc = (cv * _sigmoid(cv)).astype(BF16)
        o_ref[...] = _dot(sc, w_ref[...].astype(BF16)) + b_ref[...]

    return pl.pallas_call(
        body, name="ada_fwd", out_shape=jax.ShapeDtypeStruct((c_all.shape[0], w_sh.shape[1]), F32),
        in_specs=[VMEM_SPEC] * 3, out_specs=VMEM_SPEC, compiler_params=_cp(0, 40),
    )(c_all, w_sh, b_sh)


def _ada_bwd(c_all, dmod_sh):
    def body(c_ref, d_ref, o_ref):
        cv = c_ref[...]
        sc = (cv * _sigmoid(cv)).astype(BF16)
        o_ref[...] = _dot_tn(sc, d_ref[...].astype(BF16))

    return pl.pallas_call(
        body, name="ada_bwd", out_shape=jax.ShapeDtypeStruct((c_all.shape[1], dmod_sh.shape[1]), F32),
        in_specs=[VMEM_SPEC] * 2, out_specs=VMEM_SPEC, compiler_params=_cp(0, 40),
    )(c_all, dmod_sh)


def _vec(tm_unused=None):
    return pl.BlockSpec((1, D), lambda i: (0, 0))


def _rows(tm, width=D):
    return pl.BlockSpec((tm, width), lambda i: (i, 0))


def _norm_mod(x, g, shift, scale, tm=512):
    S = x.shape[0]
    tm = min(tm, S)

    def body(x_ref, g_ref, sh_ref, sc_ref, h_ref):
        xv = x_ref[...]
        r = lax.rsqrt(jnp.mean(xv * xv, axis=-1, keepdims=True) + EPS)
        hn = (xv * r) * g_ref[...]
        h_ref[...] = (hn * (1.0 + sc_ref[...]) + sh_ref[...]).astype(BF16)

    return pl.pallas_call(
        body, name="norm1_mod", grid=(S // tm,),
        in_specs=[_rows(tm), _vec(), _vec(), _vec()], out_specs=_rows(tm),
        out_shape=jax.ShapeDtypeStruct((S, D), BF16), compiler_params=_cp(1),
    )(x, g, shift, scale)


def _mm_rows(pairs, tm):
    ops, specs = [], []
    for a, w in pairs:
        ops += [a, w]
        specs += [pl.BlockSpec((tm, a.shape[1]), lambda i: (i, 0)),
                  pl.BlockSpec(w.shape, lambda i: (0, 0), pipeline_mode=ONE_BUF)]
    return ops, specs


def _mm_rows_value(refs, npairs):
    acc = _dot(refs[0][...], refs[1][...])
    for t in range(1, npairs):
        acc = acc + _dot(refs[2 * t][...], refs[2 * t + 1][...])
    return acc


def _resid_norm_mod(x, mm, gate, g, shift, scale, tm=256):
    S = x.shape[0]
    tm = min(tm, S)
    skip = 2 * len(mm)

    def body(*refs):
        x_ref, gt_ref, g_ref, sh_ref, sc_ref, x1_ref, h_ref, m_ref = refs[skip:]
        mixed = _mm_rows_value(refs, len(mm))
        m_ref[...] = mixed
        x1 = x_ref[...] + (1.0 + gt_ref[...]) * mixed
        x1_ref[...] = x1
        r = lax.rsqrt(jnp.mean(x1 * x1, axis=-1, keepdims=True) + EPS)
        hn = (x1 * r) * g_ref[...]
        h_ref[...] = (hn * (1.0 + sc_ref[...]) + sh_ref[...]).astype(BF16)

    ops, specs = _mm_rows(mm, tm)
    return pl.pallas_call(
        body, name="mm_out_resid_norm2_mod", grid=(S // tm,),
        in_specs=specs + [_rows(tm), _vec(), _vec(), _vec(), _vec()],
        out_specs=[_rows(tm), _rows(tm), _rows(tm)],
        out_shape=[jax.ShapeDtypeStruct((S, D), F32), jax.ShapeDtypeStruct((S, D), BF16),
                   jax.ShapeDtypeStruct((S, D), F32)],
        compiler_params=_cp(1, 40),
    )(*ops, x, gate, g, shift, scale)


def _conv3(ext, w_ref, b_ref, cs):
    e1 = pltpu.roll(ext, 1, 0)
    e2 = pltpu.roll(ext, 2, 0)
    u = b_ref[:, cs] + w_ref[0:1, cs] * e2
    u = u + w_ref[1:2, cs] * e1
    u = u + w_ref[2:3, cs] * ext
    return u, e1, e2


def _conv_glu(u0p, wc_p, bc_p, tm=256):
    S = u0p.shape[0]
    tm = min(tm, S)
    hb = tm // 16

    def body(u_ref, p_ref, w_ref, b_ref, a_ref):
        first = pl.program_id(1) == 0
        for k in range(HB // LANES):
            us = []
            for off in (k * LANES, HB + k * LANES):
                cs = slice(off, off + LANES)
                prev = jnp.where(first, 0.0, p_ref[:, cs].astype(F32))
                ext = jnp.concatenate([prev, u_ref[:, cs].astype(F32)], axis=0)
                us.append(_conv3(ext, w_ref, b_ref, cs)[0][16:])
            a_ref[:, k * LANES:(k + 1) * LANES] = (us[0] * (us[1] * _sigmoid_fast(us[1]))).astype(BF16)

    return pl.pallas_call(
        body, name="conv_glu", grid=(2, S // tm),
        in_specs=[pl.BlockSpec((tm, 2 * HB), lambda j, i: (i, j)),
                  pl.BlockSpec((16, 2 * HB), lambda j, i: (jnp.maximum(i * hb - 1, 0), j)),
                  pl.BlockSpec((3, 2 * HB), lambda j, i: (0, j)),
                  pl.BlockSpec((1, 2 * HB), lambda j, i: (0, j))],
        out_specs=pl.BlockSpec((tm, HB), lambda j, i: (i, j)),
        out_shape=jax.ShapeDtypeStruct((S, D_FF), BF16), compiler_params=_cp(2),
    )(u0p, u0p, wc_p, bc_p)


def _conv_glu_bwd(da, u0p, wc_p, bc_p, tm=256, carried=()):
    S = u0p.shape[0]
    tm = min(tm, S)
    hb = tm // 16
    nlast = S // 16 - 1

    def body(da_ref, dan_ref, u_ref, p_ref, n_ref, w_ref, b_ref, o_ref, s_ref):
        i = pl.program_id(1)
        first = i == 0
        last = i == pl.num_programs(1) - 1

        @pl.when(first)
        def _():
            s_ref[...] = jnp.zeros_like(s_ref)

        n = tm + 16
        for k in range(HB // LANES):
            kc = slice(k * LANES, (k + 1) * LANES)
            dae = jnp.concatenate([da_ref[:, kc].astype(F32),
                                   jnp.where(last, 0.0, dan_ref[:, kc].astype(F32))], axis=0)
            halves = []
            for off in (k * LANES, HB + k * LANES):
                cs = slice(off, off + LANES)
                ext = jnp.concatenate([jnp.where(first, 0.0, p_ref[:, cs].astype(F32)),
                                       u_ref[:, cs].astype(F32), n_ref[:, cs].astype(F32)], axis=0)
                u, e1, e2 = _conv3(ext, w_ref, b_ref, cs)
                halves.append((u[16:], ext[16:16 + tm], e1[16:16 + tm], e2[16:16 + tm], cs))
            val, gt = halves[0][0], halves[1][0]
            sg = _sigmoid_fast(gt)
            dus =(dae * (gt * sg), dae * val * (sg * (1.0 + gt * (1.0 - sg))))
            for du, (_, x0, x1, x2, cs) in zip(dus, halves):
                du0 = (w_ref[2:3, cs] * du + w_ref[1:2, cs] * pltpu.roll(du, n - 1, 0)
                       + w_ref[0:1, cs] * pltpu.roll(du, n - 2, 0))
                o_ref[:, cs] = du0[:tm].astype(BF16)
                dut = du[:tm]
                s_ref[0:1, cs] += jnp.sum(dut, axis=0, keepdims=True)
                s_ref[1:2, cs] += jnp.sum(dut * x2, axis=0, keepdims=True)
                s_ref[2:3, cs] += jnp.sum(dut * x1, axis=0, keepdims=True)
                s_ref[3:4, cs] += jnp.sum(dut * x0, axis=0, keepdims=True)

    return _call(
        body, carried, [da, da, u0p, u0p, u0p, wc_p, bc_p], name="conv_glu_bwd", grid=(2, S // tm),
        in_specs=[pl.BlockSpec((tm, HB), lambda j, i: (i, j)),
                  pl.BlockSpec((16, HB), lambda j, i: (jnp.minimum((i + 1) * hb, nlast), j)),
                  pl.BlockSpec((tm, 2 * HB), lambda j, i: (i, j)),
                  pl.BlockSpec((16, 2 * HB), lambda j, i: (jnp.maximum(i * hb - 1, 0), j)),
                  pl.BlockSpec((16, 2 * HB), lambda j, i: (jnp.minimum((i + 1) * hb, nlast), j)),
                  pl.BlockSpec((3, 2 * HB), lambda j, i: (0, j)),
                  pl.BlockSpec((1, 2 * HB), lambda j, i: (0, j))],
        out_specs=[pl.BlockSpec((tm, 2 * HB), lambda j, i: (i, j)),
                   pl.BlockSpec((8, 2 * HB), lambda j, i: (0, j))],
        out_shape=[jax.ShapeDtypeStruct((S, 2 * D_FF), BF16), jax.ShapeDtypeStruct((8, 2 * D_FF), F32)],
        compiler_params=_cp(2))


def _final_loss(x1, mm, gate2, g_final, target, tm=256):
    S = x1.shape[0]
    tm = min(tm, S)
    skip = 2 * len(mm)

    def body(*refs):
        x1_ref, gt_ref, g_ref, t_ref, dx_ref, dy_ref, s_ref = refs[skip:]

        @pl.when(pl.program_id(0) == 0)
        def _():
            s_ref[...] = jnp.zeros_like(s_ref)

        y2 = _mm_rows_value(refs, len(mm))
        og = 1.0 + gt_ref[...]
        x2 = x1_ref[...] + og * y2
        r = lax.rsqrt(jnp.mean(x2 * x2, axis=-1, keepdims=True) + EPS)
        n = x2 * r
        g = g_ref[...]
        err = n * g - t_ref[...]
        dy = err * (1.0 / D)
        dn = dy * g
        dx2 = r * (dn - n * jnp.mean(dn * n, axis=-1, keepdims=True))
        dx_ref[...] = dx2
        dy_ref[...] = (dx2 * og).astype(BF16)
        s_ref[0:1, :] += jnp.sum(dy * n, axis=0, keepdims=True)
        s_ref[1:2, :] += jnp.sum(dx2 * y2, axis=0, keepdims=True)
        s_ref[2:3, :] += jnp.sum(err * err, axis=0, keepdims=True)

    ops, specs = _mm_rows(mm, tm)
    return pl.pallas_call(
        body, name="mm_down_final_loss", grid=(S // tm,),
        in_specs=specs + [_rows(tm), _vec(), _vec(), _rows(tm)],
        out_specs=[_rows(tm), _rows(tm), pl.BlockSpec((8, D), lambda i: (0, 0))],
        out_shape=[jax.ShapeDtypeStruct((S, D), F32), jax.ShapeDtypeStruct((S, D), BF16),
                   jax.ShapeDtypeStruct((8, D), F32)],
        compiler_params=_cp(1, 40),
    )(*ops, x1, gate2, g_final, target)


def _norm_mod_bwd(dh, xin, dres, g, scale, mixed, gate, name, tm=256, carried=()):
    S = xin.shape[0]
    tm = min(tm, S)
    with_gate = mixed is not None
    fused = isinstance(dh, list)
    skip = 2 * len(dh) if fused else 1

    def body(*refs):
        if with_gate:
            x_ref, dr_ref, g_ref, sc_ref, m_ref, gt_ref, dx_ref, dm_ref, s_ref = refs[skip:]
        else:
            x_ref, dr_ref, g_ref, sc_ref, dx_ref, s_ref = refs[skip:]

        @pl.when(pl.program_id(0) == 0)
        def _():
            s_ref[...] = jnp.zeros_like(s_ref)

        xv = x_ref[...]
        dhv = _mm_rows_value(refs, len(dh)) if fused else refs[0][...]
        r = lax.rsqrt(jnp.mean(xv * xv, axis=-1, keepdims=True) + EPS)
        n = xv * r
        g = g_ref[...]
        hn = n * g
        dhn = dhv * (1.0 + sc_ref[...])
        dn = dhn * g
        dx = dr_ref[...] + r * (dn - n * jnp.mean(dn * n, axis=-1, keepdims=True))
        dx_ref[...] = dx
        s_ref[0:1, :] += jnp.sum(dhv, axis=0, keepdims=True)
        s_ref[1:2, :] += jnp.sum(dhv * hn, axis=0, keepdims=True)
        s_ref[2:3, :] += jnp.sum(dhn * n, axis=0, keepdims=True)
        if with_gate:
            dm_ref[...] = (dx * (1.0 + gt_ref[...])).astype(BF16)
            s_ref[3:4, :] += jnp.sum(dx * m_ref[...], axis=0, keepdims=True)

    ins, in_specs = _mm_rows(dh, tm) if fused else ([dh], [_rows(tm)])
    ins += [xin, dres, g, scale]
    in_specs += [_rows(tm), _rows(tm), _vec(), _vec()]
    out_specs = [_rows(tm)]
    out_shape = [jax.ShapeDtypeStruct((S, D), F32)]
    if with_gate:
        ins += [mixed, gate]
        in_specs += [_rows(tm), _vec()]
        out_specs.append(_rows(tm))
        out_shape.append(jax.ShapeDtypeStruct((S, D), BF16))
    out_specs.append(pl.BlockSpec((8, D), lambda i: (0, 0)))
    out_shape.append(jax.ShapeDtypeStruct((8, D), F32))
    return _call(body, carried, ins, name=name, grid=(S // tm,), in_specs=in_specs, out_specs=out_specs,
                 out_shape=out_shape, compiler_params=_cp(1, 48 if fused else None))


def _tri(n, rel):
    row = lax.broadcasted_iota(jnp.int32, (n, n), 0)
    col = lax.broadcasted_iota(jnp.int32, (n, n), 1)
    return {"gt": row > col, "ge": row >= col, "lt": row < col, "le": row <= col}[rel]


def _pair_diag(mask):
    u = jnp.where(mask, 1.0, 0.0).astype(BF16)
    z = jnp.zeros_like(u)
    return jnp.concatenate([jnp.concatenate([u, z], axis=1), jnp.concatenate([z, u], axis=1)], axis=0)


def _pair_rows(xp, lo_half):
    z = jnp.zeros_like(xp)
    return jnp.concatenate([jnp.where(lo_half, xp, z), jnp.where(lo_half, z, xp)], axis=0)


def _sb_scores(z, causal, diag):
    ls, ps, es = [], [], []
    for hh in range(2):
        zz = z[:, hh * QB:(hh + 1) * QB]
        e = jnp.exp(-jnp.abs(zz))
        l = -(jnp.maximum(zz, 0.0) + jnp.log(1.0 + e))
        ps.append(l + zz)
        ls.append(jnp.where(causal, l, 0.0) if diag else l)
        es.append(e)
    return ls, ps, es


def _sb_fwd(proj, carried=()):
    S = proj.shape[0]
    nq = S // QB

    def body(q_ref, k_ref, v_ref, o_ref, t_ref, c_ref, acc_ref, qs_ref):
        i = pl.program_id(0)
        causal = _tri(QB, "gt")
        usuf = _pair_diag(_tri(QB, "gt"))
        lo_half = lax.broadcasted_iota(jnp.int32, (QB, 128), 1) < DK
        qs_ref[...] = q_ref[...] * 0.125

        def block(j, diag, nr):
            rows = pl.ds(pl.multiple_of(j * QB, QB), QB)
            rs = slice(0, nr)
            pairs = range(H_SB // 2)
            cols = [slice(pr * 128, (pr + 1) * 128) for pr in pairs]
            zs = [_dot_nt(qs_ref[rs, cols[pr]], _pair_rows(k_ref[rows, cols[pr]], lo_half)) for pr in pairs]
            sc = [_sb_scores(zs[pr], causal, diag) for pr in pairs]
            sufs = []
            for pr in pairs:
                lh, ll = _split(jnp.concatenate(sc[pr][0], axis=1))
                sufs.append(_dot(lh, usuf) + _dot(ll, usuf))
            cmax = None
            wps = []
            for pr in pairs:
                ws = []
                for hh in range(2):
                    h = 2 * pr + hh
                    b = sufs[pr][:, hh * QB:(hh + 1) * QB]
                    if not diag:
                        b = b + c_ref[h, rs, 0:1]
                    w = jnp.exp(sc[pr][1][hh] + b)
                    ws.append((jnp.where(causal, w, 0.0) if diag else w).astype(BF16))
                    cn = b[:, 0:1] + sc[pr][0][hh][:, 0:1]
                    c_ref[h, rs, 0:1] = cn
                    cmax = cn if cmax is None else jnp.maximum(cmax, cn)
                wps.append(jnp.concatenate(ws, axis=1))
            for pr in pairs:
                upd = _dot(wps[pr], _pair_rows(v_ref[rows, cols[pr]], lo_half))
                if diag:
                    acc_ref[rs, cols[pr]] = upd
                else:
                    acc_ref[rs, cols[pr]] += upd
            lo = jnp.max(cmax[:SB_HEAD_ROWS])
            return (jnp.max(cmax[SB_HEAD_ROWS:]) if nr > SB_HEAD_ROWS else None), lo

        def cond_full(st):
            return jnp.logical_and(st[0] >= 0, st[1] > SB_SKIP)

        def step_full(st):
            return (st[0] - 1,) + block(st[0], False, QB)

        def cond_head(st):
            return jnp.logical_and(st[0] >= 0, st[1] > SB_SKIP)

        def step_head(st):
            return st[0] - 1, block(st[0], False, SB_HEAD_ROWS)[1]

        j, _, lo = lax.while_loop(cond_full, step_full, (i - 1,) + block(i, True, QB))
        jfull = j + 1
        j, _ = lax.while_loop(cond_head, step_head, (j, lo))
        o_ref[...] = acc_ref[...].astype(BF16)
        t_ref[...] = jnp.zeros_like(t_ref)
        for h in range(H_SB):
            t_ref[h // 4, :, h % 4:h % 4 + 1] = c_ref[h, :, 0:1]
        t_ref[:, :, 8:9] = jnp.zeros((2, QB, 1), F32) + (j + 1).astype(F32)
        t_ref[:, :, 9:10] = jnp.zeros((2, QB, 1), F32) + jfull.astype(F32)

    return _call(
        body, carried, [proj, proj, proj], name="sb_fwd", grid=(nq,),
        in_specs=[pl.BlockSpec((QB, 512), lambda i: (i, 0)),
                  pl.BlockSpec((S, 512), lambda i: (0, 1), pipeline_mode=ONE_BUF),
                  pl.BlockSpec((S, 512), lambda i: (0, 2), pipeline_mode=ONE_BUF)],
        out_specs=[pl.BlockSpec((QB, 512), lambda i: (i, 0)),
                   pl.BlockSpec((2, QB, 128), lambda i: (0, i, 0))],
        out_shape=[jax.ShapeDtypeStruct((S, 512), BF16), jax.ShapeDtypeStruct((2, S, 128), F32)],
        scratch_shapes=[pltpu.VMEM((H_SB, QB, 128), F32), pltpu.VMEM((QB, 512), F32), pltpu.VMEM((QB, 512), BF16)],
        compiler_params=_cp(1, 40))


def _sb_bwd(proj, dcat, stats, carried=()):
    S = proj.shape[0]
    nq = S // QB

    def body(q_ref, k_ref, v_ref, do_ref, t_ref, dq_ref, dk_ref, dv_ref, dk_acc, dv_acc, dq_acc, pc_ref, qs_ref,
             qt_ref, dot_ref):
        i = pl.program_id(1)

        @pl.when(i == 0)
        def _():
            dk_acc[...] = jnp.zeros_like(dk_acc)
            dv_acc[...] = jnp.zeros_like(dv_acc)

        causal = _tri(QB, "gt")
        uin = _pair_diag(_tri(QB, "le"))
        uex = _pair_diag(_tri(QB, "lt"))
        lo_half = lax.broadcasted_iota(jnp.int32, (QB, 128), 1) < DK
        qs_ref[...] = q_ref[...] * 0.125
        lo_rows = lax.broadcasted_iota(jnp.int32, (128, QB), 0) < DK
        for pr in range(2):
            qt_ref[pr] = (q_ref[:, pr * 128:(pr + 1) * 128].astype(F32) * 0.125).T.astype(BF16)
            dot_ref[pr] = do_ref[:, pr * 128:(pr + 1) * 128].astype(F32).T.astype(BF16)
        pc_ref[...] = jnp.zeros_like(pc_ref)
        dq_acc[...] = jnp.zeros_like(dq_acc)
        jstart = jnp.max(t_ref[:, 8:9]).astype(jnp.int32)
        jfull = jnp.max(t_ref[:, 9:10]).astype(jnp.int32)

        def block(j, diag, nr):
            rows = pl.ds(pl.multiple_of(j * QB, QB), QB)
            rs = slice(0, nr)
            pairs = range(2)
            cols = [slice(pr * 128, (pr + 1) * 128) for pr in pairs]
            kbds = [_pair_rows(k_ref[rows, cols[pr]], lo_half) for pr in pairs]
            zs = [_dot_nt(qs_ref[rs, cols[pr]], kbds[pr]) for pr in pairs]
            dws = [_dot_nt(do_ref[rs, cols[pr]], _pair_rows(v_ref[rows, cols[pr]], lo_half)) for pr in pairs]
            sc = [_sb_scores(zs[pr], causal, diag) for pr in pairs]
            plins = []
            for pr in pairs:
                lh, ll = _split(jnp.concatenate(sc[pr][0], axis=1))
                plins.append(_dot(lh, uin) + _dot(ll, uin))
            wss, gss, gexs = [], [], []
            for pr in pairs:
                ws, gs = [], []
                for hh in range(2):
                    h = 2 * pr + hh
                    half = slice(hh * QB, (hh + 1) * QB)
                    b = (t_ref[rs, h:h + 1] - pc_ref[h, rs, 0:1]) - plins[pr][:, half]
                    w = jnp.exp(sc[pr][1][hh] + b)
                    if diag:
                        w = jnp.where(causal, w, 0.0)
                    ws.append(w)
                    gs.append(dws[pr][:, half] * w)
                wss.append(ws)
                gss.append(gs)
            for pr in pairs:
                gh, gl = _split(jnp.concatenate(gss[pr], axis=1))
                gexs.append(_dot(gh, uex) + _dot(gl, uex))
            dzbs = []
            for pr in pairs:
                dzs = []
                for hh in range(2):
                    h = 2 * pr + hh
                    half = slice(hh * QB, (hh + 1) * QB)
                    e = sc[pr][2][hh]
                    r = pl.reciprocal(1.0 + e, approx=True)
                    er = e * r
                    pos = zs[pr][:, half] >= 0.0
                    gx = gexs[pr][:, half]
                    g = gss[pr][hh]
                    dz = g * jnp.where(pos, er, r) - (gx + pc_ref[4 + h, rs, 0:1]) * jnp.where(pos, r, er)
                    dzs.append(jnp.where(causal, dz, 0.0) if diag else dz)
                    pc_ref[h, rs, 0:1] += plins[pr][:, half][:, QB - 1:QB]
                    pc_ref[4 + h, rs, 0:1] += gx[:, QB - 1:QB] + g[:, QB - 1:QB]
                dzbs.append(jnp.concatenate(dzs, axis=1).astype(BF16))
            for pr in pairs:
                dq_acc[rs, cols[pr]] += _dot(dzbs[pr], kbds[pr])
                r1 = _dot(qt_ref[pr, :, rs], dzbs[pr])
                dk_acc[pr, j] += jnp.where(lo_rows, r1[:, :QB], r1[:, QB:])
                r2 = _dot(dot_ref[pr, :, rs], jnp.concatenate(wss[pr], axis=1).astype(BF16))
                dv_acc[pr, j] += jnp.where(lo_rows, r2[:, :QB], r2[:, QB:])

        def step_head(j, carry):
            block(j, False, SB_HEAD_ROWS)
            return carry

        def step_full(j, carry):
            block(j, False, QB)
            return carry

        lax.fori_loop(jstart, jfull, step_head, 0)
        lax.fori_loop(jfull, i, step_full, 0)
        block(i, True, QB)
        dq_ref[...] = (dq_acc[...] * 0.125).astype(BF16)

        @pl.when(i == nq - 1)
        def _():
            def put(jj, carry):
                krows = pl.ds(pl.multiple_of(jj * QB, QB), QB)
                for pr in range(2):
                    dk_ref[krows, pr * 128:(pr + 1) * 128] = dk_acc[pr, jj].T.astype(BF16)
                    dv_ref[krows, pr * 128:(pr + 1) * 128] = dv_acc[pr, jj].T.astype(BF16)
                return carry
            lax.fori_loop(0, nq, put, 0)

    return _call(
        body, carried, [proj, proj, proj, dcat, stats], name="sb_bwd", grid=(2, nq),
        in_specs=[pl.BlockSpec((QB, 256), lambda g, i: (i, g)),
                  pl.BlockSpec((S, 256), lambda g, i: (0, 2 + g), pipeline_mode=ONE_BUF),
                  pl.BlockSpec((S, 256), lambda g, i: (0, 4 + g), pipeline_mode=ONE_BUF),
                  pl.BlockSpec((QB, 256), lambda g, i: (i, g)),
                  pl.BlockSpec((None, QB, 128), lambda g, i: (g, i, 0))],
        out_specs=[pl.BlockSpec((QB, 256), lambda g, i: (i, g)),
                   pl.BlockSpec((S, 256), lambda g, i: (0, g)),
                   pl.BlockSpec((S, 256), lambda g, i: (0, g))],
        out_shape=[jax.ShapeDtypeStruct((S, 512), BF16)] * 3,
        scratch_shapes=[pltpu.VMEM((2, nq, 128, QB), F32), pltpu.VMEM((2, nq, 128, QB), F32),
                        pltpu.VMEM((QB, 256), F32), pltpu.VMEM((8, QB, 128), F32), pltpu.VMEM((QB, 256), BF16),
                        pltpu.VMEM((2, 128, QB), BF16), pltpu.VMEM((2, 128, QB), BF16)],
        compiler_params=_cp(2, 56))


GLA_NC = 4
GLA_R = GLA_NC * CHUNK


def _chunk_tri(strict):
    row = lax.broadcasted_iota(jnp.int32, (GLA_R, GLA_R), 0)
    col = lax.broadcasted_iota(jnp.int32, (GLA_R, GLA_R), 1)
    m = jnp.logical_and(row // CHUNK == col // CHUNK, row > col if strict else row >= col)
    u = jnp.where(m, 1.0, 0.0).astype(BF16)
    return jnp.concatenate([u, u], axis=1)


def _per_chunk_rows(vals):
    return jnp.concatenate([jnp.broadcast_to(v, (CHUNK, v.shape[1])) for v in vals], axis=0)


def _head_blocks(st):
    row = lax.broadcasted_iota(jnp.int32, (H_GLA * DV, H_GLA * DK), 0)
    col = lax.broadcasted_iota(jnp.int32, (H_GLA * DV, H_GLA * DK), 1)
    t = jnp.concatenate([st.astype(BF16)] * H_GLA, axis=0)
    return jnp.where(row // DV == col // DK, t, jnp.zeros_like(t))


def _head_diag(big):
    head = lax.broadcasted_iota(jnp.int32, (DV, H_GLA * DK), 1) // DK
    out = big[0:DV]
    for h in range(1, H_GLA):
        out = jnp.where(head == h, big[h * DV:(h + 1) * DV], out)
    return out


def _gla_gate4(gf_ref, wfg_ref, bfg_ref):
    f = _dot(gf_ref[...], wfg_ref[...]) + bfg_ref[...]
    _, la, _ = _log_sigmoid_parts(f)
    lah, lal = _split(la * (1.0 / 16.0))
    cum = _dot(_chunk_tri(False), jnp.concatenate([lah, lal], axis=0))
    tots = [cum[(c + 1) * CHUNK - 1:(c + 1) * CHUNK, :] for c in range(GLA_NC)]
    return f, jnp.exp(_per_chunk_rows(tots) - cum), [jnp.exp(t) for t in tots]


def _gla_specs4(ns, rev):
    def ix(i):
        return ns - 1 - i if rev else i
    return [pl.BlockSpec((GLA_R, 256), lambda i: (ix(i), 6)),
            pl.BlockSpec((GLA_R, 256), lambda i: (ix(i), 7)),
            pl.BlockSpec((GLA_R, 512), lambda i: (ix(i), 4)),
            pl.BlockSpec((GLA_R, 512), lambda i: (ix(i), 5)),
            pl.BlockSpec((GLA_R, 128), lambda i: (ix(i), 24))]


def _gla_fwd(proj, wfg_p, bfg, ggla, carried=()):
    S = proj.shape[0]
    ns = S // GLA_R

    def body(q_ref, k_ref, v_ref, gg_ref, gf_ref, wfg_ref, bfg_ref, ggla_ref, o_ref, st_ref, state):
        @pl.when(pl.program_id(0) == 0)
        def _():
            state[...] = jnp.zeros_like(state)

        _, e, decs = _gla_gate4(gf_ref, wfg_ref, bfg_ref)
        kdec = (k_ref[...].astype(F32) * e).astype(BF16)
        rows = [slice(c * CHUNK, (c + 1) * CHUNK) for c in range(GLA_NC)]
        kvs = [_head_diag(_dot_tn(v_ref[rows[c], :], kdec[rows[c]])) for c in range(GLA_NC)]
        st = state[...]
        sts = []
        for c in range(GLA_NC):
            st = st * decs[c] + kvs[c]
            st_ref[c] = st
            sts.append(st)
        state[...] = st
        o = jnp.concatenate([_dot_nt(q_ref[rows[c], :] * 0.125, _head_blocks(sts[c])) for c in range(GLA_NC)], axis=0)
        for h in range(H_GLA):
            vs = slice(h * DV, (h + 1) * DV)
            oh = o[:, vs]
            ohn = oh * lax.rsqrt(jnp.mean(oh * oh, axis=-1, keepdims=True) + EPS)
            gg = gg_ref[:, vs].astype(F32)
            o_ref[:, vs] = ((ohn * ggla_ref[:, vs]) * (gg * _sigmoid(gg))).astype(BF16)

    return _call(
        body, carried, [proj, proj, proj, proj, proj, wfg_p, bfg, ggla], name="gla_fwd", grid=(ns,),
        in_specs=_gla_specs4(ns, False) + [pl.BlockSpec((128, 256), lambda i: (0, 0)),
                                           pl.BlockSpec((1, 256), lambda i: (0, 0)),
                                           pl.BlockSpec((1, 512), lambda i: (0, 0))],
        out_specs=[pl.BlockSpec((GLA_R, 512), lambda i: (i, 0)),
                   pl.BlockSpec((GLA_NC, 128, 256), lambda i: (i, 0, 0))],
        out_shape=[jax.ShapeDtypeStruct((S, 512), BF16), jax.ShapeDtypeStruct((S // CHUNK, 128, 256), F32)],
        scratch_shapes=[pltpu.VMEM((128, 256), F32)],
        compiler_params=_cp(1))


def _gla_bwd(dcat, proj, states, wfg_p, bfg, ggla, carried=()):
    S = proj.shape[0]
    ns = S // GLA_R

    def body(do_ref, q_ref, k_ref, v_ref, gg_ref, gf_ref, sc_ref, sp_ref, wfg_ref, bfg_ref, ggla_ref,
             dp_ref, s_ref, dw_ref, carry):
        sr = pl.program_id(0)

        @pl.when(sr == 0)
        def _():
            carry[...] = jnp.zeros_like(carry)
            s_ref[...] = jnp.zeros_like(s_ref)
            dw_ref[...] = jnp.zeros_like(dw_ref)

        f, e, decs = _gla_gate4(gf_ref, wfg_ref, bfg_ref)
        kf = k_ref[...].astype(F32) * e
        kdec = kf.astype(BF16)
        rows = [slice(c * CHUNK, (c + 1) * CHUNK) for c in range(GLA_NC)]
        sts = [sc_ref[c] for c in range(GLA_NC)]
        st_before = jnp.where(sr < ns - 1, sp_ref[0], 0.0)
        sbd = [_head_blocks(sts[c]) for c in range(GLA_NC)]
        qs = q_ref[...] * 0.125
        o = jnp.concatenate([_dot_nt(qs[rows[c]], sbd[c]) for c in range(GLA_NC)], axis=0)
        dobs = []
        for h in range(H_GLA):
            vs = slice(h * DV, (h + 1) * DV)
            oh = o[:, vs]
            rr = lax.rsqrt(jnp.mean(oh * oh, axis=-1, keepdims=True) + EPS)
            ohn = oh * rr
            gg = gg_ref[:, vs].astype(F32)
            sg = _sigmoid(gg)
            dout = do_ref[:, vs].astype(F32)
            gl = ggla_ref[:, vs]
            dp_ref[:, 1024 + h * DV:1024 + (h + 1) * DV] = (
                dout * (ohn * gl) * (sg * (1.0 + gg * (1.0 - sg)))).astype(BF16)
            dt1 = dout * (gg * sg)
            s_ref[0:1, vs] += jnp.sum(dt1 * ohn, axis=0, keepdims=True)
            dohn = dt1 * gl
            dobs.append((rr * (dohn - ohn * jnp.mean(dohn * ohn, axis=-1, keepdims=True))).astype(BF16))
        dob = jnp.concatenate(dobs, axis=1)
        dsout = []
        for c in range(GLA_NC):
            dp_ref[rows[c], 0:256] = (_dot(dob[rows[c]], sbd[c]) * 0.125).astype(BF16)
            dsout.append(_head_diag(_dot_tn(dob[rows[c]], qs[rows[c]])))
        g = carry[...]
        gts, ddecs = [None] * GLA_NC, [None] * GLA_NC
        for c in reversed(range(GLA_NC)):
            gts[c] = dsout[c] + g
            ddecs[c] = jnp.sum(gts[c] * (sts[c - 1] if c > 0 else st_before), axis=0, keepdims=True) * decs[c]
            g = gts[c] * decs[c]
        carry[...] = g
        dkds = []
        for c in range(GLA_NC):
            gbd = _head_blocks(gts[c])
            dkds.append(_dot(v_ref[rows[c], :], gbd))
            dp_ref[rows[c], 512:1024] = _dot_nt(kdec[rows[c]], gbd).astype(BF16)
        dkd = jnp.concatenate(dkds, axis=0)
        dp_ref[:, 256:512] = (dkd * e).astype(BF16)
        wh, wl = _split(dkd * kf)
        dla = _dot(_chunk_tri(True), jnp.concatenate([wh, wl], axis=0)) + _per_chunk_rows(ddecs)
        df = dla * _sigmoid(-f) * (1.0 / 16.0)
        dfb = df.astype(BF16)
        s_ref[1:2, 0:256] += jnp.sum(df, axis=0, keepdims=True)
        dw_ref[...] += _dot_tn(gf_ref[...], dfb)
        dp_ref[:, 1536:1664] = _dot_nt(dfb, wfg_ref[...]).astype(BF16)

    return _call(
        body, carried, [dcat, proj, proj, proj, proj, proj, states, states, wfg_p, bfg, ggla],
        name="gla_bwd", grid=(ns,),
        in_specs=[pl.BlockSpec((GLA_R, 512), lambda i: (ns - 1 - i, 1))] + _gla_specs4(ns, True) + [
            pl.BlockSpec((GLA_NC, 128, 256), lambda i: (ns - 1 - i, 0, 0)),
            pl.BlockSpec((1, 128, 256), lambda i: (jnp.maximum((ns - 1 - i) * GLA_NC - 1, 0), 0, 0)),
            pl.BlockSpec((128, 256), lambda i: (0, 0)),
            pl.BlockSpec((1, 256), lambda i: (0, 0)),
            pl.BlockSpec((1, 512), lambda i: (0, 0))],
        out_specs=[pl.BlockSpec((GLA_R, 1664), lambda i: (ns - 1 - i, 0)),
                   pl.BlockSpec((8, 512), lambda i: (0, 0)),
                   pl.BlockSpec((128, 256), lambda i: (0, 0))],
        out_shape=[jax.ShapeDtypeStruct((S, 1664), BF16), jax.ShapeDtypeStruct((8, 512), F32),
                   jax.ShapeDtypeStruct((128, 256), F32)],
        scratch_shapes=[pltpu.VMEM((128, 256), F32)],
        compiler_params=_cp(1))


def _sum_leading(a, name):
    n = a.shape[0]

    def body(a_ref, o_ref):
        acc = a_ref[0]
        for k in range(1, n):
            acc = acc + a_ref[k]
        o_ref[...] = acc

    return pl.pallas_call(
        body, name=name, out_shape=jax.ShapeDtypeStruct(a.shape[1:], F32),
        in_specs=[VMEM_SPEC], out_specs=VMEM_SPEC,
    )(a)


def _sum_chip(own, recv, name):
    R, C = own.shape
    tr, tc = _tile2d(R, C, 1024 * 1024)

    def body(o_ref, r_ref, p_ref):
        acc = o_ref[...]
        for k in range(3):
            acc = acc + r_ref[k].astype(F32)
        p_ref[...] = acc

    return pl.pallas_call(
        body, name=name, grid=(R // tr, C // tc),
        in_specs=[pl.BlockSpec((tr, tc), lambda i, j: (i, j)), pl.BlockSpec((3, tr, tc), lambda i, j: (0, i, j))],
        out_specs=pl.BlockSpec((tr, tc), lambda i, j: (i, j)),
        out_shape=jax.ShapeDtypeStruct((R, C), F32), compiler_params=_cp(2, 40),
    )(own, recv)


def _adamw(w, p, q, m, v, name):
    R, C = w.shape
    tr, tc = _tile2d(R, C, 1024 * 1024)
    two = q is not None

    def body(*refs):
        if two:
            w_ref, p_ref, q_ref, m_ref, v_ref, g_out, d_out, m_out, v_out = refs
            g = p_ref[...] + q_ref[...]
        else:
            w_ref, p_ref, m_ref, v_ref, g_out, d_out, m_out, v_out = refs
            g = p_ref[...]
        m2 = B1 * m_ref[...] + (1.0 - B1) * g
        v2 = B2 * v_ref[...] + (1.0 - B2) * (g * g)
        m_hat = m2 / (1.0 - B1 ** STEP)
        v_hat = v2 / (1.0 - B2 ** STEP)
        g_out[...] = g
        d_out[...] = -LR * (m_hat / (jnp.sqrt(v_hat) + EPS_A) + WD * w_ref[...])
        m_out[...] = m2
        v_out[...] = v2

    spec = pl.BlockSpec((tr, tc), lambda i, j: (i, j))
    ins = [w, p, q, m, v] if two else [w, p, m, v]
    return pl.pallas_call(
        body, name=name, grid=(R // tr, C // tc),
        in_specs=[spec] * len(ins), out_specs=[spec] * 4,
        out_shape=[jax.ShapeDtypeStruct((R, C), F32)] * 4, compiler_params=_cp(2, 40),
    )(*ins)


def _reduce_big(own, recv, name):
    p = _sum_chip(own, recv, name + "_sum")
    return p, _pair_swap(p, name + "_swap")


def _cols_to_chips(a, width):
    return a.reshape(a.shape[0], 4, width).swapaxes(0, 1)


def _chips_to_cols(a):
    return a.swapaxes(0, 1).reshape(a.shape[1], 4 * a.shape[2])


def _swap_mid(a):
    lead = a.shape[:-1]
    return a.reshape(lead + (2, 2, HB)).swapaxes(-3, -2).reshape(lead + (4 * HB,))


def kernel(x, c, w_ada, b_ada, g_norm1, w_in, w_fg2, b_fg2, g_gla_out, w_out, g_norm2, w_up, w_conv, b_conv, w_down, g_final, loss_target, m_w_ada, m_b_ada, m_g_norm1, m_w_in, m_w_fg2, m_b_fg2, m_g_gla_out, m_w_out, m_g_norm2, m_w_up, m_w_conv, m_b_conv, m_w_down, m_g_final, v_w_ada, v_b_ada, v_g_norm1, v_w_in, v_w_fg2, v_b_fg2, v_g_gla_out, v_w_out, v_g_norm2, v_w_up, v_w_conv, v_b_conv, v_w_down, v_g_final):
    xi, yi, ci = lax.axis_index("x"), lax.axis_index("y"), lax.axis_index("c")
    cidx = 2 * xi + yi
    didx = 4 * xi + 2 * yi + ci
    xs = x[0]
    tgt = loss_target[0]
    gfin = g_final.reshape(1, D)
    AW = D * 6 // 4

    c_all = _allgather8(c, "gather_c").reshape(8, D)
    c_pad = jnp.concatenate([c_all, jnp.zeros((8, D), F32)], axis=0)
    mod_part = _ada_fwd(c_pad, w_ada[0], lax.dynamic_slice(b_ada, (0, cidx * AW), (1, AW)))[:8]
    small = jnp.concatenate([mod_part.reshape(-1), w_conv.reshape(-1), w_fg2.reshape(-1)]).reshape(-1, 128)
    small_g = _allgather4(small, "gather_small").reshape(4, -1)
    mod = lax.dynamic_index_in_dim(small_g[:, :8 * AW].reshape(4, 8, AW), didx, axis=1, keepdims=False).reshape(1, 6 * D)
    shift1, scale1, gate1, shift2, scale2, gate2 = [mod[:, k * D:(k + 1) * D] for k in range(6)]
    o1 = 8 * AW
    o2 = o1 + 3 * HB
    wc_p = _swap_mid(_chips_to_cols(small_g[:, o1:o2].reshape(4, 3, HB)))
    bc_p = _swap_mid(b_conv)
    wfg_full = _chips_to_cols(small_g[:, o2:].reshape(4, RANK, 64))
    wfg_p = jnp.concatenate([wfg_full, jnp.zeros((128 - RANK, 256), F32)], axis=0).astype(BF16)

    w_in_t = _allgather4(w_in[0].T.astype(BF16), "gather_w_in").reshape(N_IN, D)
    w_in_t = jnp.concatenate([w_in_t, jnp.zeros((N_IN_P - N_IN, D), BF16)], axis=0)
    w_in_p = w_in_t.T

    h = _norm_mod(xs, g_norm1, shift1, scale1)
    proj, (w_down_g,) = _mm([(h, w_in_p)], BF16, "mm_in", 256, N_IN_P, 48,
                            carried=[("gather", w_down[0].astype(BF16), False)])
    w_down_f = w_down_g.reshape(D_FF, D)
    (o_gla, states), (w_out_g,) = _gla_fwd(proj, wfg_p, b_fg2, g_gla_out,
                                           carried=[("gather", w_out[0].astype(BF16), False)])
    w_out_f = w_out_g.reshape(D, D)
    (o_sb, stats), (w_up_g,) = _sb_fwd(proj, carried=[("gather", w_up[0].astype(BF16), False)])
    w_up_p = _swap_mid(_chips_to_cols(w_up_g))
    x1, h2, mixed = _resid_norm_mod(xs, [(o_sb, w_out_f[:512]), (o_gla, w_out_f[512:])],
                                    gate1, g_norm2, shift2, scale2)
    u0p = _mm([(h2, w_up_p)], BF16, "mm_up", 512, HB, 48)
    a = _conv_glu(u0p, wc_p, bc_p)
    dx2, dy2, s_fin = _final_loss(x1, [(a, w_down_f)], gate2, gfin, tgt)
    loss = lax.psum(0.5 / D * jnp.sum(s_fin[2]), ("x", "y", "c"))

    da = _mm([(dy2, w_down_f.T)], BF16, "mm_down_t", 512, D_FF, 48)
    dw_down, dw_down_h = [t.reshape(4, D_FF // 4, D) for t in _mm_tn([a], dy2, "mm_dw_down", D, 512, 56)]
    (du0p, s_conv), (rc_down,) = _conv_glu_bwd(da, u0p, wc_p, bc_p, carried=[("scatter", dw_down_h, False)])
    dw_up, dw_up_h = _mm_tn([h2], du0p, "mm_dw_up", HB, 512, 48)
    (dx1, dmixed, s_n2), _ = _norm_mod_bwd([(du0p, w_up_p.T)], x1, dx2, g_norm2, scale2, mixed, gate1,
                                           "mm_up_t_norm2_bwd")
    dcat = _mm([(dmixed, w_out_f.T)], BF16, "mm_out_t", 512, D)
    dw_out, dw_out_h = [t.reshape(4, D // 4, D) for t in _mm_tn([o_sb, o_gla], dmixed, "mm_dw_out", D, 512)]
    (dq, dk, dv), (rc_up,) = _sb_bwd(proj, dcat, stats, carried=[("scatter", dw_up_h, True)])
    (dp_gla, s_gla, dwfg), (rc_out,) = _gla_bwd(dcat, proj, states, wfg_p, b_fg2, g_gla_out,
                                                carried=[("scatter", dw_out_h, False)])
    dw_in, dw_in_h = _mm_tn([dq, dk, dv, dp_gla], h, "mm_dw_in", D, 512, 56)
    dw_in_h = dw_in_h[0, :N_IN].reshape(4, N_IN // 4, D)
    dw_in_own = lax.dynamic_slice(dw_in[0], (cidx * (N_IN // 4), 0), (N_IN // 4, D))
    dh, (rc_in,) = _mm(
        [(dq, w_in_t[:512]), (dk, w_in_t[512:1024]), (dv, w_in_t[1024:1536]), (dp_gla, w_in_t[1536:])],
        F32, "mm_in_t", 256, D, 48, carried=[("scatter", dw_in_h, False)])
    (gx, s_n1), _ = _norm_mod_bwd(dh, xs, dx1, g_norm1, scale1, None, None, "norm1_bwd")

    dmod = jnp.concatenate([s_n1[0], s_n1[1], s_n2[3], s_n2[0], s_n2[1], s_fin[1]])
    s_conv_n = _swap_mid(s_conv[:4])
    part = jnp.concatenate([dmod, s_n1[2], s_n2[2], s_fin[0], s_gla[0], s_gla[1, :256], s_conv_n[0],
                            s_conv_n[1:4].reshape(-1), dwfg[:RANK].reshape(-1)]).reshape(-1, 128)
    parts = _allgather8(part, "gather_small_grads")
    tot = _sum_leading(parts, "sum_small_grads").reshape(-1)
    dmod_all = parts.reshape(8, -1)[:, :6 * D]
    offs = [0]
    for n in (6 * D, D, D, D, 512, 256, 2 * D_FF, 3 * 2 * D_FF, RANK * 256):
        offs.append(offs[-1] + n)
    g_b_ada, g_g1, g_g2, g_gf, g_ggla, g_bfg, g_bconv, g_wconv_full, g_wfg_full = [
        tot[offs[k]:offs[k + 1]] for k in range(9)]
    g_wconv = lax.dynamic_index_in_dim(_cols_to_chips(g_wconv_full.reshape(3, 2 * D_FF), HB), cidx, 0, keepdims=False)
    g_wfg = lax.dynamic_index_in_dim(_cols_to_chips(g_wfg_full.reshape(RANK, 256), 64), cidx, 0, keepdims=False)

    dmod_pad = jnp.concatenate([dmod_all, jnp.zeros((8, 6 * D), F32)], axis=0)
    g_w_ada = _ada_bwd(c_pad, lax.dynamic_slice(dmod_pad, (0, cidx * AW), (16, AW)))

    def own(blocks, swapped=False):
        return lax.dynamic_index_in_dim(blocks, _slot(cidx, swapped), axis=0, keepdims=False)

    p_in, q_in = _reduce_big(dw_in_own, rc_in, "rs_w_in")
    p_out, q_out = _reduce_big(own(dw_out), rc_out, "rs_w_out")
    p_up, q_up = _reduce_big(own(dw_up, True), rc_up, "rs_w_up")
    p_down, q_down = _reduce_big(own(dw_down), rc_down, "rs_w_down")

    out = {}
    out["w_ada"] = _adamw(w_ada[0], g_w_ada, None, m_w_ada[0], v_w_ada[0], "adamw_w_ada")
    out["w_in"] = [t.T for t in _adamw(w_in[0].T, p_in, q_in, m_w_in[0].T, v_w_in[0].T, "adamw_w_in")]
    out["w_out"] = _adamw(w_out[0], p_out, q_out, m_w_out[0], v_w_out[0], "adamw_w_out")
    out["w_up"] = _adamw(w_up[0], p_up, q_up, m_w_up[0], v_w_up[0], "adamw_w_up")
    out["w_down"] = _adamw(w_down[0], p_down, q_down, m_w_down[0], v_w_down[0], "adamw_w_down")
    small_names = ["b_ada", "g_norm1", "w_fg2", "b_fg2", "g_gla_out", "g_norm2", "w_conv", "b_conv", "g_final"]
    small_w = [b_ada, g_norm1, w_fg2, b_fg2, g_gla_out, g_norm2, w_conv, b_conv, g_final]
    small_m = [m_b_ada, m_g_norm1, m_w_fg2, m_b_fg2, m_g_gla_out, m_g_norm2, m_w_conv, m_b_conv, m_g_final]
    small_v = [v_b_ada, v_g_norm1, v_w_fg2, v_b_fg2, v_g_gla_out, v_g_norm2, v_w_conv, v_b_conv, v_g_final]
    small_gr = [g_b_ada, g_g1, g_wfg, g_bfg, g_ggla, g_g2, g_wconv, g_bconv, g_gf]

    def pack(arrs):
        flat = jnp.concatenate([t.reshape(-1) for t in arrs])
        return jnp.concatenate([flat, jnp.zeros((-flat.shape[0]) % 1024, F32)]).reshape(-1, 128)

    packed = _adamw(pack(small_w), pack(small_gr), None, pack(small_m), pack(small_v), "adamw_small")
    off = 0
    for nm, wt in zip(small_names, small_w):
        n = wt.size
        out[nm] = [t.reshape(-1)[off:off + n].reshape(wt.shape) for t in packed]
        off += n
    for nm in ("w_ada", "w_in", "w_out", "w_up", "w_down"):
        out[nm] = [t[None] for t in out[nm]]

    names = ["w_ada", "b_ada", "g_norm1", "w_in", "w_fg2", "b_fg2", "g_gla_out", "w_out", "g_norm2", "w_up",
             "w_conv", "b_conv", "w_down", "g_final"]
    res = [loss, gx[None]]
    for k in range(4):
        res += [out[nm][k] for nm in names]
    return tuple(res)
```

```python
import functools

import jax
import jax.numpy as jnp
from jax import lax
from jax.experimental import pallas as pl
from jax.experimental.pallas import tpu as pltpu

F32 = jnp.float32
BF16 = jnp.bfloat16
MESH = pl.DeviceIdType.MESH

D = 1024
H_SB = 8
DK = 64
DV = 128
H_GLA = 4
CHUNK = 64
RANK = 16
N_IN = 3088
N_IN_P = 3200
D_FF = 2816
HB = D_FF // 2
LANES = 128
EPS = 1e-6
QB = 128
SB_SKIP = -120.0
SB_HEAD_ROWS = 64

LR, B1, B2, EPS_A, WD, STEP = 0.001, 0.9, 0.999, 1e-08, 0.01, 10

ANY = pl.BlockSpec(memory_space=pl.ANY)
VMEM_SPEC = pl.BlockSpec(memory_space=pltpu.VMEM)
ONE_BUF = pl.Buffered(1)


def _cp(ndim=0, vmem_mb=None):
    kw = {}
    if ndim:
        kw["dimension_semantics"] = ("arbitrary",) * ndim
    if vmem_mb:
        kw["vmem_limit_bytes"] = vmem_mb * 1024 * 1024
    return pltpu.CompilerParams(**kw)


def _dot(a, b):
    return jnp.dot(a, b, preferred_element_type=F32)


def _dot_nt(a, b):
    return lax.dot_general(a, b, (((1,), (1,)), ((), ())), preferred_element_type=F32)


def _dot_tn(a, b):
    return lax.dot_general(a, b, (((0,), (0,)), ((), ())), preferred_element_type=F32)


def _split(x):
    hi = x.astype(BF16)
    lo = (x - hi.astype(F32)).astype(BF16)
    return hi, lo


def _sigmoid(x):
    return jax.nn.sigmoid(x)


def _sigmoid_fast(x):
    return pl.reciprocal(1.0 + jnp.exp(-x), approx=True)


def _log_sigmoid_parts(z):
    e = jnp.exp(-jnp.abs(z))
    sp = jnp.log1p(e)
    return -(jnp.maximum(z, 0.0) + sp), jnp.minimum(z, 0.0) - sp, e


def _tile2d(rows, cols, budget=512 * 1024):
    best = None
    for t in range(8, rows + 1, 8):
        if rows % t == 0 and t * cols * 4 <= budget:
            best = t
    if best is not None:
        return best, cols
    best = LANES if cols % LANES == 0 else cols
    for t in range(LANES, cols + 1, LANES):
        if cols % t == 0 and rows * t * 4 <= budget:
            best = t
    return rows, best


def _flip(v, bit):
    return 1 - v if bit else v


def _allgather8(a, name):
    def body(a_ref, o_ref, ssem, rsem, lsem):
        x, y, c = lax.axis_index("x"), lax.axis_index("y"), lax.axis_index("c")
        me = 4 * x + 2 * y + c
        loc = pltpu.make_async_copy(a_ref, o_ref.at[me], lsem)
        loc.start()
        sends = []
        for r in range(1, 8):
            peer = (_flip(x, r & 4), _flip(y, r & 2), _flip(c, r & 1))
            cp = pltpu.make_async_remote_copy(
                src_ref=a_ref, dst_ref=o_ref.at[me], send_sem=ssem.at[r - 1], recv_sem=rsem.at[r - 1],
                device_id=peer, device_id_type=MESH)
            cp.start()
            sends.append(cp)
        for r in range(1, 8):
            peer = (_flip(x, r & 4), _flip(y, r & 2), _flip(c, r & 1))
            pidx = 4 * peer[0] + 2 * peer[1] + peer[2]
            pltpu.make_async_remote_copy(
                src_ref=a_ref, dst_ref=o_ref.at[pidx], send_sem=ssem.at[r - 1], recv_sem=rsem.at[r - 1],
                device_id=peer, device_id_type=MESH).wait_recv()
        for cp in sends:
            cp.wait_send()
        loc.wait()

    return pl.pallas_call(
        body, name=name,
        out_shape=jax.ShapeDtypeStruct((8,) + a.shape, a.dtype),
        in_specs=[VMEM_SPEC], out_specs=VMEM_SPEC,
        scratch_shapes=[pltpu.SemaphoreType.DMA((7,)), pltpu.SemaphoreType.DMA((7,)), pltpu.SemaphoreType.DMA],
    )(a)


def _allgather4(a, name):
    def body(a_ref, o_ref, ssem, rsem, lsem):
        x, y, c = lax.axis_index("x"), lax.axis_index("y"), lax.axis_index("c")
        me = 2 * x + y
        loc = pltpu.make_async_copy(a_ref, o_ref.at[me], lsem)
        loc.start()
        sends = []
        for r in range(1, 4):
            peer = (_flip(x, r & 2), _flip(y, r & 1), c)
            cp = pltpu.make_async_remote_copy(
                src_ref=a_ref, dst_ref=o_ref.at[me], send_sem=ssem.at[r - 1], recv_sem=rsem.at[r - 1],
                device_id=peer, device_id_type=MESH)
            cp.start()
            sends.append(cp)
        for r in range(1, 4):
            peer = (_flip(x, r & 2), _flip(y, r & 1), c)
            pidx = 2 * peer[0] + peer[1]
            pltpu.make_async_remote_copy(
                src_ref=a_ref, dst_ref=o_ref.at[pidx], send_sem=ssem.at[r - 1], recv_sem=rsem.at[r - 1],
                device_id=peer, device_id_type=MESH).wait_recv()
        for cp in sends:
            cp.wait_send()
        loc.wait()

    return pl.pallas_call(
        body, name=name,
        out_shape=jax.ShapeDtypeStruct((4,) + a.shape, a.dtype),
        in_specs=[ANY], out_specs=ANY,
        scratch_shapes=[pltpu.SemaphoreType.DMA((3,)), pltpu.SemaphoreType.DMA((3,)), pltpu.SemaphoreType.DMA],
    )(a)


def _allgather4_split(a, name):
    R, C = a.shape
    hc = C // 2

    def body(a_ref, o_ref, ssem, rsem, fssem, frsem, lsem):
        x, y, c = lax.axis_index("x"), lax.axis_index("y"), lax.axis_index("c")
        me = 2 * x + y
        sibling = (x, y, 1 - c)
        mine = pl.ds(pl.multiple_of(c * hc, hc), hc)
        theirs = pl.ds(pl.multiple_of((1 - c) * hc, hc), hc)
        loc = pltpu.make_async_copy(a_ref, o_ref.at[me], lsem)
        loc.start()
        peers = [(_flip(x, r & 2), _flip(y, r & 1), c) for r in range(1, 4)]
        pidx = [2 * p[0] + p[1] for p in peers]
        sends = []
        for k in range(3):
            cp = pltpu.make_async_remote_copy(
                src_ref=a_ref.at[:, mine], dst_ref=o_ref.at[me, :, mine], send_sem=ssem.at[k], recv_sem=rsem.at[k],
                device_id=peers[k], device_id_type=MESH)
            cp.start()
            sends.append(cp)
        for k in range(3):
            landed = o_ref.at[pidx[k], :, mine]
            pltpu.make_async_remote_copy(
                src_ref=landed, dst_ref=landed, send_sem=ssem.at[k], recv_sem=rsem.at[k],
                device_id=peers[k], device_id_type=MESH).wait_recv()
            cp = pltpu.make_async_remote_copy(
                src_ref=landed, dst_ref=landed, send_sem=fssem.at[k], recv_sem=frsem.at[k],
                device_id=sibling, device_id_type=MESH)
            cp.start()
            sends.append(cp)
        for k in range(3):
            got = o_ref.at[pidx[k], :, theirs]
            pltpu.make_async_remote_copy(
                src_ref=got, dst_ref=got, send_sem=fssem.at[k], recv_sem=frsem.at[k],
                device_id=sibling, device_id_type=MESH).wait_recv()
        for cp in sends:
            cp.wait_send()
        loc.wait()

    return pl.pallas_call(
        body, name=name,
        out_shape=jax.ShapeDtypeStruct((4,) + a.shape, a.dtype),
        in_specs=[ANY], out_specs=ANY,
        scratch_shapes=[pltpu.SemaphoreType.DMA((3,))] * 4 + [pltpu.SemaphoreType.DMA],
    )(a)


def _slot(chip, swapped):
    return 2 * (chip % 2) + chip // 2 if swapped else chip


def _pair_swap(p, name):
    def body(p_ref, o_ref, ssem, rsem):
        x, y, c = lax.axis_index("x"), lax.axis_index("y"), lax.axis_index("c")
        cp = pltpu.make_async_remote_copy(
            src_ref=p_ref, dst_ref=o_ref, send_sem=ssem, recv_sem=rsem,
            device_id=(x, y, 1 - c), device_id_type=MESH)
        cp.start()
        cp.wait()

    return pl.pallas_call(
        body, name=name,
        out_shape=jax.ShapeDtypeStruct(p.shape, p.dtype),
        in_specs=[ANY], out_specs=ANY,
        scratch_shapes=[pltpu.SemaphoreType.DMA, pltpu.SemaphoreType.DMA],
    )(p)


def _carried_copies(kind, src_ref, dst_ref, sems, swapped):
    ssem, rsem, lsem = sems
    x, y, c = lax.axis_index("x"), lax.axis_index("y"), lax.axis_index("c")
    me = 2 * x + y
    starts, recvs = [], []
    if kind == "gather":
        starts.append(pltpu.make_async_copy(src_ref, dst_ref.at[me], lsem))
    for r in range(1, 4):
        peer = (_flip(x, r & 2), _flip(y, r & 1), c)
        pidx = 2 * peer[0] + peer[1]
        if kind == "gather":
            src, dst, landed = src_ref, dst_ref.at[me], dst_ref.at[pidx]
        else:
            src = src_ref.at[2 * peer[1] + peer[0] if swapped else pidx]
            dst = landed = dst_ref.at[r - 1]
        starts.append(pltpu.make_async_remote_copy(
            src_ref=src, dst_ref=dst, send_sem=ssem.at[r - 1], recv_sem=rsem.at[r - 1],
            device_id=peer, device_id_type=MESH))
        recvs.append(pltpu.make_async_remote_copy(
            src_ref=src, dst_ref=landed, send_sem=ssem.at[r - 1], recv_sem=rsem.at[r - 1],
            device_id=peer, device_id_type=MESH))
    return starts, recvs


def _call(body, carried, operands, *, name, grid, in_specs, out_specs, out_shape, scratch_shapes=(),
          compiler_params=None):
    single = not isinstance(out_shape, (list, tuple))
    out_specs = [out_specs] if single else list(out_specs)
    out_shape = [out_shape] if single else list(out_shape)
    n_in, n_out, n_sc, nh = len(operands), len(out_shape), len(scratch_shapes), len(carried)

    def full(*refs):
        ins, h_in = refs[:n_in], refs[n_in:n_in + nh]
        o0 = n_in + nh
        outs, h_out = refs[o0:o0 + n_out], refs[o0 + n_out:o0 + n_out + nh]
        s0 = o0 + n_out + nh
        scratch, sems = refs[s0:s0 + n_sc], refs[s0 + n_sc:]
        first = last = None
        for d in range(len(grid)):
            f = pl.program_id(d) == 0
            l = pl.program_id(d) == pl.num_programs(d) - 1
            first = f if first is None else jnp.logical_and(first, f)
            last = l if last is None else jnp.logical_and(last, l)

        def copies(t):
            return _carried_copies(carried[t][0], h_in[t], h_out[t], sems[3 * t:3 * t + 3], carried[t][2])

        if nh:
            @pl.when(first)
            def _():
                for t in range(nh):
                    for cp in copies(t)[0]:
                        cp.start()

        body(*ins, *outs, *scratch)

        if nh:
            @pl.when(last)
            def _():
                for t in range(nh):
                    starts, recvs = copies(t)
                    for cp in recvs:
                        cp.wait_recv()
                    for cp in starts:
                        if carried[t][0] == "gather" and cp is starts[0]:
                            cp.wait()
                        else:
                            cp.wait_send()

    h_shapes = [jax.ShapeDtypeStruct(((4,) + arr.shape) if kind == "gather" else ((3,) + arr.shape[1:]), arr.dtype)
                for kind, arr, _ in carried]
    sem_shapes = [pltpu.SemaphoreType.DMA((3,)), pltpu.SemaphoreType.DMA((3,)), pltpu.SemaphoreType.DMA] * nh
    res = pl.pallas_call(
        full, name=name, grid=grid, in_specs=list(in_specs) + [ANY] * nh, out_specs=out_specs + [ANY] * nh,
        out_shape=out_shape + h_shapes, scratch_shapes=list(scratch_shapes) + sem_shapes,
        compiler_params=compiler_params,
    )(*operands, *[arr for _, arr, _ in carried])
    main = res[:n_out]
    return (main[0] if single else main), list(res[n_out:])


def _mm(pairs, out_dtype, name, tm, tn, vmem_mb=None, carried=()):
    S = pairs[0][0].shape[0]
    N = pairs[0][1].shape[1]
    tm = min(tm, S)
    np_ = len(pairs)

    def body(*refs):
        acc = _dot(refs[0][...], refs[1][...])
        for t in range(1, np_):
            acc = acc + _dot(refs[2 * t][...], refs[2 * t + 1][...])
        refs[-1][...] = acc.astype(refs[-1].dtype)

    in_specs, ops = [], []
    for a, w in pairs:
        in_specs += [pl.BlockSpec((tm, a.shape[1]), lambda n, i: (i, 0)),
                     pl.BlockSpec((w.shape[0], tn), lambda n, i: (0, n))]
        ops += [a, w]
    out, got = _call(
        body, carried, ops, name=name, grid=(N // tn, S // tm), in_specs=in_specs,
        out_specs=pl.BlockSpec((tm, tn), lambda n, i: (i, n)),
        out_shape=jax.ShapeDtypeStruct((S, N), out_dtype),
        compiler_params=_cp(2, vmem_mb))
    return (out, got) if carried else out


def _mm_tn(a_list, b, name, bn, tk, vmem_mb=None):
    S, N = b.shape
    ms = [a.shape[1] for a in a_list]
    M = sum(ms)
    tk = min(tk, S)
    na = len(a_list)

    def body(*refs):
        b_ref, o_ref, o16_ref = refs[na], refs[na + 1], refs[na + 2]

        @pl.when(pl.program_id(1) == 0)
        def _():
            o_ref[...] = jnp.zeros_like(o_ref)
        off = 0
        for t in range(na):
            o_ref[off:off + ms[t], :] += _dot_tn(refs[t][...], b_ref[...])
            off += ms[t]

        @pl.when(pl.program_id(1) == pl.num_programs(1) - 1)
        def _():
            o16_ref[...] = o_ref[...].astype(BF16)

    spec = pl.BlockSpec((None, M, bn), lambda n, k: (n, 0, 0))
    return pl.pallas_call(
        body, name=name, grid=(N // bn, S // tk),
        in_specs=[pl.BlockSpec((tk, m), lambda n, k: (k, 0)) for m in ms] + [pl.BlockSpec((tk, bn), lambda n, k: (k, n))],
        out_specs=[spec, spec],
        out_shape=[jax.ShapeDtypeStruct((N // bn, M, bn), F32), jax.ShapeDtypeStruct((N // bn, M, bn), BF16)],
        compiler_params=_cp(2, vmem_mb),
    )(*a_list, b)


def _ada_fwd(c_all, w_sh, b_sh):
    def body(c_ref, w_ref, b_ref, o_ref):
        cv = c_ref[...]
        sc = (cv * _sigmoid(cv)).astype(BF16)
        o_ref[...] = _dot(sc, w_ref[...].astype(BF16)) + b_ref[...]

    return pl.pallas_call(
        body, name="ada_fwd", out_shape=jax.ShapeDtypeStruct((c_all.shape[0], w_sh.shape[1]), F32),
        in_specs=[VMEM_SPEC] * 3, out_specs=VMEM_SPEC, compiler_params=_cp(0, 40),
    )(c_all, w_sh, b_sh)


def _ada_bwd(c_all, dmod_sh):
    def body(c_ref, d_ref, o_ref):
        cv = c_ref[...]
        sc = (cv * _sigmoid(cv)).astype(BF16)
        o_ref[...] = _dot_tn(sc, d_ref[...].astype(BF16))

    return pl.pallas_call(
        body, name="ada_bwd", out_shape=jax.ShapeDtypeStruct((c_all.shape[1], dmod_sh.shape[1]), F32),
        in_specs=[VMEM_SPEC] * 2, out_specs=VMEM_SPEC, compiler_params=_cp(0, 40),
    )(c_all, dmod_sh)


def _vec(tm_unused=None):
    return pl.BlockSpec((1, D), lambda i: (0, 0))


def _rows(tm, width=D):
    return pl.BlockSpec((tm, width), lambda i: (i, 0))


def _norm_mod(x, g, shift, scale, tm=512):
    S = x.shape[0]
    tm = min(tm, S)

    def body(x_ref, g_ref, sh_ref, sc_ref, h_ref):
        xv = x_ref[...]
        r = lax.rsqrt(jnp.mean(xv * xv, axis=-1, keepdims=True) + EPS)
        hn = (xv * r) * g_ref[...]
        h_ref[...] = (hn * (1.0 + sc_ref[...]) + sh_ref[...]).astype(BF16)

    return pl.pallas_call(
        body, name="norm1_mod", grid=(S // tm,),
        in_specs=[_rows(tm), _vec(), _vec(), _vec()], out_specs=_rows(tm),
        out_shape=jax.ShapeDtypeStruct((S, D), BF16), compiler_params=_cp(1),
    )(x, g, shift, scale)


def _mm_rows(pairs, tm):
    ops, specs = [], []
    for a, w in pairs:
        ops += [a, w]
        specs += [pl.BlockSpec((tm, a.shape[1]), lambda i: (i, 0)),
                  pl.BlockSpec(w.shape, lambda i: (0, 0), pipeline_mode=ONE_BUF)]
    return ops, specs


def _mm_rows_value(refs, npairs):
    acc = _dot(refs[0][...], refs[1][...])
    for t in range(1, npairs):
        acc = acc + _dot(refs[2 * t][...], refs[2 * t + 1][...])
    return acc


def _resid_norm_mod(x, mm, gate, g, shift, scale, tm=256):
    S = x.shape[0]
    tm = min(tm, S)
    skip = 2 * len(mm)

    def body(*refs):
        x_ref, gt_ref, g_ref, sh_ref, sc_ref, x1_ref, h_ref, m_ref = refs[skip:]
        mixed = _mm_rows_value(refs, len(mm))
        m_ref[...] = mixed
        x1 = x_ref[...] + (1.0 + gt_ref[...]) * mixed
        x1_ref[...] = x1
        r = lax.rsqrt(jnp.mean(x1 * x1, axis=-1, keepdims=True) + EPS)
        hn = (x1 * r) * g_ref[...]
        h_ref[...] = (hn * (1.0 + sc_ref[...]) + sh_ref[...]).astype(BF16)

    ops, specs = _mm_rows(mm, tm)
    return pl.pallas_call(
        body, name="mm_out_resid_norm2_mod", grid=(S // tm,),
        in_specs=specs + [_rows(tm), _vec(), _vec(), _vec(), _vec()],
        out_specs=[_rows(tm), _rows(tm), _rows(tm)],
        out_shape=[jax.ShapeDtypeStruct((S, D), F32), jax.ShapeDtypeStruct((S, D), BF16),
                   jax.ShapeDtypeStruct((S, D), F32)],
        compiler_params=_cp(1, 40),
    )(*ops, x, gate, g, shift, scale)


def _conv3(ext, w_ref, b_ref, cs):
    e1 = pltpu.roll(ext, 1, 0)
    e2 = pltpu.roll(ext, 2, 0)
    u = b_ref[:, cs] + w_ref[0:1, cs] * e2
    u = u + w_ref[1:2, cs] * e1
    u = u + w_ref[2:3, cs] * ext
    return u, e1, e2


def _up_conv_glu(h2, w_up_p, wc_p, bc_p, tm=256):
    S = h2.shape[0]
    tm = min(tm, S)
    widths = [2 * LANES] * (HB // (2 * LANES)) + ([LANES] if HB % (2 * LANES) else [])

    def body(h_ref, wu_ref, w_ref, b_ref, u_ref, a_ref, prev_ref):
        @pl.when(pl.program_id(0) == 0)
        def _():
            prev_ref[...] = jnp.zeros_like(prev_ref)

        hv = h_ref[...]
        for j in range(2):
            base = 0
            for wd in widths:
                us = []
                for off in (2 * j * HB + base, 2 * j * HB + HB + base):
                    cb = _dot(hv, wu_ref[:, off:off + wd]).astype(BF16)
                    u_ref[:, off:off + wd] = cb
                    for q in range(wd // LANES):
                        cs = slice(off + q * LANES, off + (q + 1) * LANES)
                        cq = cb[:, q * LANES:(q + 1) * LANES]
                        ext = jnp.concatenate([prev_ref[:, cs].astype(F32), cq.astype(F32)], axis=0)
                        us.append(_conv3(ext, w_ref, b_ref, cs)[0][16:])
                        prev_ref[:, cs] = cq[tm - 16:]
                nq = wd // LANES
                for q in range(nq):
                    val, gt = us[q], us[nq + q]
                    a_ref[:, j * HB + base + q * LANES:j * HB + base + (q + 1) * LANES] = (
                        val * (gt * _sigmoid_fast(gt))).astype(BF16)
                base += wd

    return pl.pallas_call(
        body, name="mm_up_conv_glu", grid=(S // tm,),
        in_specs=[pl.BlockSpec((tm, D), lambda i: (i, 0)),
                  pl.BlockSpec((D, 2 * D_FF), lambda i: (0, 0), pipeline_mode=ONE_BUF),
                  pl.BlockSpec((3, 2 * D_FF), lambda i: (0, 0)),
                  pl.BlockSpec((1, 2 * D_FF), lambda i: (0, 0))],
        out_specs=[pl.BlockSpec((tm, 2 * D_FF), lambda i: (i, 0)), pl.BlockSpec((tm, D_FF), lambda i: (i, 0))],
        out_shape=[jax.ShapeDtypeStruct((S, 2 * D_FF), BF16), jax.ShapeDtypeStruct((S, D_FF), BF16)],
        scratch_shapes=[pltpu.VMEM((16, 2 * D_FF), BF16)],
        compiler_params=_cp(1, 48),
    )(h2, w_up_p, wc_p, bc_p)


def _conv_glu_bwd(da, u0p, wc_p, bc_p, tm=256, carried=()):
    S = u0p.shape[0]
    tm = min(tm, S)
    hb = tm // 16
    nlast = S // 16 - 1

    def body(da_ref, dan_ref, u_ref, p_ref, n_ref, w_ref, b_ref, o_ref, s_ref):
        i = pl.program_id(1)
        first = i == 0
        last = i == pl.num_programs(1) - 1

        @pl.when(first)
        def _():
            s_ref[...] = jnp.zeros_like(s_ref)

        n = tm + 16
        for k in range(HB // LANES):
            kc = slice(k * LANES, (k + 1) * LANES)
            dae = jnp.concatenate([da_ref[:, kc].astype(F32),
                                   jnp.where(last, 0.0, dan_ref[:, kc].astype(F32))], axis=0)
            halves = []
            for off in (k * LANES, HB + k * LANES):
                cs = slice(off, off + LANES)
                ext = jnp.concatenate([jnp.where(first, 0.0, p_ref[:, cs].astype(F32)),
                                       u_ref[:, cs].astype(F32), n_ref[:, cs].astype(F32)], axis=0)
                u, e1, e2 = _conv3(ext, w_ref, b_ref, cs)
                halves.append((u[16:], ext[16:16 + tm], e1[16:16 + tm], e2[16:16 + tm], cs))
            val, gt = halves[0][0], halves[1][0]
            sg = _sigmoid_fast(gt)
            dus =(dae * (gt * sg), dae * val * (sg * (1.0 + gt * (1.0 - sg))))
            for du, (_, x0, x1, x2, cs) in zip(dus, halves):
                du0 = (w_ref[2:3, cs] * du + w_ref[1:2, cs] * pltpu.roll(du, n - 1, 0)
                       + w_ref[0:1, cs] * pltpu.roll(du, n - 2, 0))
                o_ref[:, cs] = du0[:tm].astype(BF16)
                dut = du[:tm]
                s_ref[0:1, cs] += jnp.sum(dut, axis=0, keepdims=True)
                s_ref[1:2, cs] += jnp.sum(dut * x2, axis=0, keepdims=True)
                s_ref[2:3, cs] += jnp.sum(dut * x1, axis=0, keepdims=True)
                s_ref[3:4, cs] += jnp.sum(dut * x0, axis=0, keepdims=True)

    return _call(
        body, carried, [da, da, u0p, u0p, u0p, wc_p, bc_p], name="conv_glu_bwd", grid=(2, S // tm),
        in_specs=[pl.BlockSpec((tm, HB), lambda j, i: (i, j)),
                  pl.BlockSpec((16, HB), lambda j, i: (jnp.minimum((i + 1) * hb, nlast), j)),
                  pl.BlockSpec((tm, 2 * HB), lambda j, i: (i, j)),
                  pl.BlockSpec((16, 2 * HB), lambda j, i: (jnp.maximum(i * hb - 1, 0), j)),
                  pl.BlockSpec((16, 2 * HB), lambda j, i: (jnp.minimum((i + 1) * hb, nlast), j)),
                  pl.BlockSpec((3, 2 * HB), lambda j, i: (0, j)),
                  pl.BlockSpec((1, 2 * HB), lambda j, i: (0, j))],
        out_specs=[pl.BlockSpec((tm, 2 * HB), lambda j, i: (i, j)),
                   pl.BlockSpec((8, 2 * HB), lambda j, i: (0, j))],
        out_shape=[jax.ShapeDtypeStruct((S, 2 * D_FF), BF16), jax.ShapeDtypeStruct((8, 2 * D_FF), F32)],
        compiler_params=_cp(2))


def _final_loss(x1, mm, gate2, g_final, target, tm=256):
    S = x1.shape[0]
    tm = min(tm, S)
    skip = 2 * len(mm)

    def body(*refs):
        x1_ref, gt_ref, g_ref, t_ref, dx_ref, dy_ref, s_ref = refs[skip:]

        @pl.when(pl.program_id(0) == 0)
        def _():
            s_ref[...] = jnp.zeros_like(s_ref)

        y2 = _mm_rows_value(refs, len(mm))
        og = 1.0 + gt_ref[...]
        x2 = x1_ref[...] + og * y2
        r = lax.rsqrt(jnp.mean(x2 * x2, axis=-1, keepdims=True) + EPS)
        n = x2 * r
        g = g_ref[...]
        err = n * g - t_ref[...]
        dy = err * (1.0 / D)
        dn = dy * g
        dx2 = r * (dn - n * jnp.mean(dn * n, axis=-1, keepdims=True))
        dx_ref[...] = dx2
        dy_ref[...] = (dx2 * og).astype(BF16)
        s_ref[0:1, :] += jnp.sum(dy * n, axis=0, keepdims=True)
        s_ref[1:2, :] += jnp.sum(dx2 * y2, axis=0, keepdims=True)
        s_ref[2:3, :] += jnp.sum(err * err, axis=0, keepdims=True)

    ops, specs = _mm_rows(mm, tm)
    return pl.pallas_call(
        body, name="mm_down_final_loss", grid=(S // tm,),
        in_specs=specs + [_rows(tm), _vec(), _vec(), _rows(tm)],
        out_specs=[_rows(tm), _rows(tm), pl.BlockSpec((8, D), lambda i: (0, 0))],
        out_shape=[jax.ShapeDtypeStruct((S, D), F32), jax.ShapeDtypeStruct((S, D), BF16),
                   jax.ShapeDtypeStruct((8, D), F32)],
        compiler_params=_cp(1, 40),
    )(*ops, x1, gate2, g_final, target)


def _norm_mod_bwd(dh, xin, dres, g, scale, mixed, gate, name, tm=256, carried=()):
    S = xin.shape[0]
    tm = min(tm, S)
    with_gate = mixed is not None
    fused = isinstance(dh, list)
    skip = 2 * len(dh) if fused else 1

    def body(*refs):
        if with_gate:
            x_ref, dr_ref, g_ref, sc_ref, m_ref, gt_ref, dx_ref, dm_ref, s_ref = refs[skip:]
        else:
            x_ref, dr_ref, g_ref, sc_ref, dx_ref, s_ref = refs[skip:]

        @pl.when(pl.program_id(0) == 0)
        def _():
            s_ref[...] = jnp.zeros_like(s_ref)

        xv = x_ref[...]
        dhv = _mm_rows_value(refs, len(dh)) if fused else refs[0][...]
        r = lax.rsqrt(jnp.mean(xv * xv, axis=-1, keepdims=True) + EPS)
        n = xv * r
        g = g_ref[...]
        hn = n * g
        dhn = dhv * (1.0 + sc_ref[...])
        dn = dhn * g
        dx = dr_ref[...] + r * (dn - n * jnp.mean(dn * n, axis=-1, keepdims=True))
        dx_ref[...] = dx
        s_ref[0:1, :] += jnp.sum(dhv, axis=0, keepdims=True)
        s_ref[1:2, :] += jnp.sum(dhv * hn, axis=0, keepdims=True)
        s_ref[2:3, :] += jnp.sum(dhn * n, axis=0, keepdims=True)
        if with_gate:
            dm_ref[...] = (dx * (1.0 + gt_ref[...])).astype(BF16)
            s_ref[3:4, :] += jnp.sum(dx * m_ref[...], axis=0, keepdims=True)

    ins, in_specs = _mm_rows(dh, tm) if fused else ([dh], [_rows(tm)])
    ins += [xin, dres, g, scale]
    in_specs += [_rows(tm), _rows(tm), _vec(), _vec()]
    out_specs = [_rows(tm)]
    out_shape = [jax.ShapeDtypeStruct((S, D), F32)]
    if with_gate:
        ins += [mixed, gate]
        in_specs += [_rows(tm), _vec()]
        out_specs.append(_rows(tm))
        out_shape.append(jax.ShapeDtypeStruct((S, D), BF16))
    out_specs.append(pl.BlockSpec((8, D), lambda i: (0, 0)))
    out_shape.append(jax.ShapeDtypeStruct((8, D), F32))
    return _call(body, carried, ins, name=name, grid=(S // tm,), in_specs=in_specs, out_specs=out_specs,
                 out_shape=out_shape, compiler_params=_cp(1, 48 if fused else None))


def _tri(n, rel):
    row = lax.broadcasted_iota(jnp.int32, (n, n), 0)
    col = lax.broadcasted_iota(jnp.int32, (n, n), 1)
    return {"gt": row > col, "ge": row >= col, "lt": row < col, "le": row <= col}[rel]


def _pair_diag(mask):
    u = jnp.where(mask, 1.0, 0.0).astype(BF16)
    z = jnp.zeros_like(u)
    return jnp.concatenate([jnp.concatenate([u, z], axis=1), jnp.concatenate([z, u], axis=1)], axis=0)


def _pair_rows(xp, lo_half):
    z = jnp.zeros_like(xp)
    return jnp.concatenate([jnp.where(lo_half, xp, z), jnp.where(lo_half, z, xp)], axis=0)


def _sb_scores(z, causal, diag):
    ls, ps, es = [], [], []
    for hh in range(2):
        zz = z[:, hh * QB:(hh + 1) * QB]
        e = jnp.exp(-jnp.abs(zz))
        l = -(jnp.maximum(zz, 0.0) + jnp.log(1.0 + e))
        ps.append(l + zz)
        ls.append(jnp.where(causal, l, 0.0) if diag else l)
        es.append(e)
    return ls, ps, es


def _sb_fwd(proj, carried=()):
    S = proj.shape[0]
    nq = S // QB

    def body(q_ref, k_ref, v_ref, o_ref, t_ref, c_ref, acc_ref, qs_ref):
        i = pl.program_id(0)
        causal = _tri(QB, "gt")
        usuf = _pair_diag(_tri(QB, "gt"))
        lo_half = lax.broadcasted_iota(jnp.int32, (QB, 128), 1) < DK
        qs_ref[...] = q_ref[...] * 0.125

        def block(j, diag, nr):
            rows = pl.ds(pl.multiple_of(j * QB, QB), QB)
            rs = slice(0, nr)
            pairs = range(H_SB // 2)
            cols = [slice(pr * 128, (pr + 1) * 128) for pr in pairs]
            zs = [_dot_nt(qs_ref[rs, cols[pr]], _pair_rows(k_ref[rows, cols[pr]], lo_half)) for pr in pairs]
            sc = [_sb_scores(zs[pr], causal, diag) for pr in pairs]
            sufs = []
            for pr in pairs:
                lh, ll = _split(jnp.concatenate(sc[pr][0], axis=1))
                sufs.append(_dot(lh, usuf) + _dot(ll, usuf))
            cmax = None
            wps = []
            for pr in pairs:
                ws = []
                for hh in range(2):
                    h = 2 * pr + hh
                    b = sufs[pr][:, hh * QB:(hh + 1) * QB]
                    if not diag:
                        b = b + c_ref[h, rs, 0:1]
                    w = jnp.exp(sc[pr][1][hh] + b)
                    ws.append((jnp.where(causal, w, 0.0) if diag else w).astype(BF16))
                    cn = b[:, 0:1] + sc[pr][0][hh][:, 0:1]
                    c_ref[h, rs, 0:1] = cn
                    cmax = cn if cmax is None else jnp.maximum(cmax, cn)
                wps.append(jnp.concatenate(ws, axis=1))
            for pr in pairs:
                upd = _dot(wps[pr], _pair_rows(v_ref[rows, cols[pr]], lo_half))
                if diag:
                    acc_ref[rs, cols[pr]] = upd
                else:
                    acc_ref[rs, cols[pr]] += upd
            lo = jnp.max(cmax[:SB_HEAD_ROWS])
            return (jnp.max(cmax[SB_HEAD_ROWS:]) if nr > SB_HEAD_ROWS else None), lo

        def cond_full(st):
            return jnp.logical_and(st[0] >= 0, st[1] > SB_SKIP)

        def step_full(st):
            return (st[0] - 1,) + block(st[0], False, QB)

        def cond_head(st):
            return jnp.logical_and(st[0] >= 0, st[1] > SB_SKIP)

        def step_head(st):
            return st[0] - 1, block(st[0], False, SB_HEAD_ROWS)[1]

        j, _, lo = lax.while_loop(cond_full, step_full, (i - 1,) + block(i, True, QB))
        jfull = j + 1
        j, _ = lax.while_loop(cond_head, step_head, (j, lo))
        o_ref[...] = acc_ref[...].astype(BF16)
        t_ref[...] = jnp.zeros_like(t_ref)
        for h in range(H_SB):
            t_ref[h // 4, :, h % 4:h % 4 + 1] = c_ref[h, :, 0:1]
        t_ref[:, :, 8:9] = jnp.zeros((2, QB, 1), F32) + (j + 1).astype(F32)
        t_ref[:, :, 9:10] = jnp.zeros((2, QB, 1), F32) + jfull.astype(F32)

    return _call(
        body, carried, [proj, proj, proj], name="sb_fwd", grid=(nq,),
        in_specs=[pl.BlockSpec((QB, 512), lambda i: (i, 0)),
                  pl.BlockSpec((S, 512), lambda i: (0, 1), pipeline_mode=ONE_BUF),
                  pl.BlockSpec((S, 512), lambda i: (0, 2), pipeline_mode=ONE_BUF)],
        out_specs=[pl.BlockSpec((QB, 512), lambda i: (i, 0)),
                   pl.BlockSpec((2, QB, 128), lambda i: (0, i, 0))],
        out_shape=[jax.ShapeDtypeStruct((S, 512), BF16), jax.ShapeDtypeStruct((2, S, 128), F32)],
        scratch_shapes=[pltpu.VMEM((H_SB, QB, 128), F32), pltpu.VMEM((QB, 512), F32), pltpu.VMEM((QB, 512), BF16)],
        compiler_params=_cp(1, 40))


def _sb_bwd(proj, dcat, stats, carried=()):
    S = proj.shape[0]
    nq = S // QB

    def body(q_ref, k_ref, v_ref, do_ref, t_ref, dq_ref, dk_ref, dv_ref, dk_acc, dv_acc, dq_acc, pc_ref, qs_ref,
             qt_ref, dot_ref):
        i = pl.program_id(1)

        @pl.when(i == 0)
        def _():
            dk_acc[...] = jnp.zeros_like(dk_acc)
            dv_acc[...] = jnp.zeros_like(dv_acc)

        causal = _tri(QB, "gt")
        uin = _pair_diag(_tri(QB, "le"))
        uex = _pair_diag(_tri(QB, "lt"))
        lo_half = lax.broadcasted_iota(jnp.int32, (QB, 128), 1) < DK
        qs_ref[...] = q_ref[...] * 0.125
        lo_rows = lax.broadcasted_iota(jnp.int32, (128, QB), 0) < DK
        for pr in range(2):
            qt_ref[pr] = (q_ref[:, pr * 128:(pr + 1) * 128].astype(F32) * 0.125).T.astype(BF16)
            dot_ref[pr] = do_ref[:, pr * 128:(pr + 1) * 128].astype(F32).T.astype(BF16)
        pc_ref[...] = jnp.zeros_like(pc_ref)
        dq_acc[...] = jnp.zeros_like(dq_acc)
        jstart = jnp.max(t_ref[:, 8:9]).astype(jnp.int32)
        jfull = jnp.max(t_ref[:, 9:10]).astype(jnp.int32)

        def block(j, diag, nr):
            rows = pl.ds(pl.multiple_of(j * QB, QB), QB)
            rs = slice(0, nr)
            pairs = range(2)
            cols = [slice(pr * 128, (pr + 1) * 128) for pr in pairs]
            kbds = [_pair_rows(k_ref[rows, cols[pr]], lo_half) for pr in pairs]
            zs = [_dot_nt(qs_ref[rs, cols[pr]], kbds[pr]) for pr in pairs]
            dws = [_dot_nt(do_ref[rs, cols[pr]], _pair_rows(v_ref[rows, cols[pr]], lo_half)) for pr in pairs]
            sc = [_sb_scores(zs[pr], causal, diag) for pr in pairs]
            plins = []
            for pr in pairs:
                lh, ll = _split(jnp.concatenate(sc[pr][0], axis=1))
                plins.append(_dot(lh, uin) + _dot(ll, uin))
            wss, gss, gexs = [], [], []
            for pr in pairs:
                ws, gs = [], []
                for hh in range(2):
                    h = 2 * pr + hh
                    half = slice(hh * QB, (hh + 1) * QB)
                    b = (t_ref[rs, h:h + 1] - pc_ref[h, rs, 0:1]) - plins[pr][:, half]
                    w = jnp.exp(sc[pr][1][hh] + b)
                    if diag:
                        w = jnp.where(causal, w, 0.0)
                    ws.append(w)
                    gs.append(dws[pr][:, half] * w)
                wss.append(ws)
                gss.append(gs)
            for pr in pairs:
                gh, gl = _split(jnp.concatenate(gss[pr], axis=1))
                gexs.append(_dot(gh, uex) + _dot(gl, uex))
            dzbs = []
            for pr in pairs:
                dzs = []
                for hh in range(2):
                    h = 2 * pr + hh
                    half = slice(hh * QB, (hh + 1) * QB)
                    e = sc[pr][2][hh]
                    r = pl.reciprocal(1.0 + e, approx=True)
                    er = e * r
                    pos = zs[pr][:, half] >= 0.0
                    gx = gexs[pr][:, half]
                    g = gss[pr][hh]
                    dz = g * jnp.where(pos, er, r) - (gx + pc_ref[4 + h, rs, 0:1]) * jnp.where(pos, r, er)
                    dzs.append(jnp.where(causal, dz, 0.0) if diag else dz)
                    pc_ref[h, rs, 0:1] += plins[pr][:, half][:, QB - 1:QB]
                    pc_ref[4 + h, rs, 0:1] += gx[:, QB - 1:QB] + g[:, QB - 1:QB]
                dzbs.append(jnp.concatenate(dzs, axis=1).astype(BF16))
            for pr in pairs:
                dq_acc[rs, cols[pr]] += _dot(dzbs[pr], kbds[pr])
                r1 = _dot(qt_ref[pr, :, rs], dzbs[pr])
                dk_acc[pr, j] += jnp.where(lo_rows, r1[:, :QB], r1[:, QB:])
                r2 = _dot(dot_ref[pr, :, rs], jnp.concatenate(wss[pr], axis=1).astype(BF16))
                dv_acc[pr, j] += jnp.where(lo_rows, r2[:, :QB], r2[:, QB:])

        def step_head(j, carry):
            block(j, False, SB_HEAD_ROWS)
            return carry

        def step_full(j, carry):
            block(j, False, QB)
            return carry

        lax.fori_loop(jstart, jfull, step_head, 0)
        lax.fori_loop(jfull, i, step_full, 0)
        block(i, True, QB)
        dq_ref[...] = (dq_acc[...] * 0.125).astype(BF16)

        @pl.when(i == nq - 1)
        def _():
            def put(jj, carry):
                krows = pl.ds(pl.multiple_of(jj * QB, QB), QB)
                for pr in range(2):
                    dk_ref[krows, pr * 128:(pr + 1) * 128] = dk_acc[pr, jj].T.astype(BF16)
                    dv_ref[krows, pr * 128:(pr + 1) * 128] = dv_acc[pr, jj].T.astype(BF16)
                return carry
            lax.fori_loop(0, nq, put, 0)

    return _call(
        body, carried, [proj, proj, proj, dcat, stats], name="sb_bwd", grid=(2, nq),
        in_specs=[pl.BlockSpec((QB, 256), lambda g, i: (i, g)),
                  pl.BlockSpec((S, 256), lambda g, i: (0, 2 + g), pipeline_mode=ONE_BUF),
                  pl.BlockSpec((S, 256), lambda g, i: (0, 4 + g), pipeline_mode=ONE_BUF),
                  pl.BlockSpec((QB, 256), lambda g, i: (i, g)),
                  pl.BlockSpec((None, QB, 128), lambda g, i: (g, i, 0))],
        out_specs=[pl.BlockSpec((QB, 256), lambda g, i: (i, g)),
                   pl.BlockSpec((S, 256), lambda g, i: (0, g)),
                   pl.BlockSpec((S, 256), lambda g, i: (0, g))],
        out_shape=[jax.ShapeDtypeStruct((S, 512), BF16)] * 3,
        scratch_shapes=[pltpu.VMEM((2, nq, 128, QB), F32), pltpu.VMEM((2, nq, 128, QB), F32),
                        pltpu.VMEM((QB, 256), F32), pltpu.VMEM((8, QB, 128), F32), pltpu.VMEM((QB, 256), BF16),
                        pltpu.VMEM((2, 128, QB), BF16), pltpu.VMEM((2, 128, QB), BF16)],
        compiler_params=_cp(2, 56))


GLA_NC = 4
GLA_R = GLA_NC * CHUNK


def _chunk_tri(strict):
    row = lax.broadcasted_iota(jnp.int32, (GLA_R, GLA_R), 0)
    col = lax.broadcasted_iota(jnp.int32, (GLA_R, GLA_R), 1)
    m = jnp.logical_and(row // CHUNK == col // CHUNK, row > col if strict else row >= col)
    u = jnp.where(m, 1.0, 0.0).astype(BF16)
    return jnp.concatenate([u, u], axis=1)


def _per_chunk_rows(vals):
    return jnp.concatenate([jnp.broadcast_to(v, (CHUNK, v.shape[1])) for v in vals], axis=0)


def _head_blocks(st):
    row = lax.broadcasted_iota(jnp.int32, (H_GLA * DV, H_GLA * DK), 0)
    col = lax.broadcasted_iota(jnp.int32, (H_GLA * DV, H_GLA * DK), 1)
    t = jnp.concatenate([st.astype(BF16)] * H_GLA, axis=0)
    return jnp.where(row // DV == col // DK, t, jnp.zeros_like(t))


def _head_diag(big):
    head = lax.broadcasted_iota(jnp.int32, (DV, H_GLA * DK), 1) // DK
    out = big[0:DV]
    for h in range(1, H_GLA):
        out = jnp.where(head == h, big[h * DV:(h + 1) * DV], out)
    return out


def _gla_gate4(gf_ref, wfg_ref, bfg_ref):
    f = _dot(gf_ref[...], wfg_ref[...]) + bfg_ref[...]
    _, la, _ = _log_sigmoid_parts(f)
    lah, lal = _split(la * (1.0 / 16.0))
    cum = _dot(_chunk_tri(False), jnp.concatenate([lah, lal], axis=0))
    tots = [cum[(c + 1) * CHUNK - 1:(c + 1) * CHUNK, :] for c in range(GLA_NC)]
    return f, jnp.exp(_per_chunk_rows(tots) - cum), [jnp.exp(t) for t in tots]


def _gla_specs4(ns, rev):
    def ix(i):
        return ns - 1 - i if rev else i
    return [pl.BlockSpec((GLA_R, 256), lambda i: (ix(i), 6)),
            pl.BlockSpec((GLA_R, 256), lambda i: (ix(i), 7)),
            pl.BlockSpec((GLA_R, 512), lambda i: (ix(i), 4)),
            pl.BlockSpec((GLA_R, 512), lambda i: (ix(i), 5)),
            pl.BlockSpec((GLA_R, 128), lambda i: (ix(i), 24))]


def _gla_fwd(proj, wfg_p, bfg, ggla, carried=()):
    S = proj.shape[0]
    ns = S // GLA_R

    def body(q_ref, k_ref, v_ref, gg_ref, gf_ref, wfg_ref, bfg_ref, ggla_ref, o_ref, st_ref, state):
        @pl.when(pl.program_id(0) == 0)
        def _():
            state[...] = jnp.zeros_like(state)

        _, e, decs = _gla_gate4(gf_ref, wfg_ref, bfg_ref)
        kdec = (k_ref[...].astype(F32) * e).astype(BF16)
        rows = [slice(c * CHUNK, (c + 1) * CHUNK) for c in range(GLA_NC)]
        kvs = [_head_diag(_dot_tn(v_ref[rows[c], :], kdec[rows[c]])) for c in range(GLA_NC)]
        st = state[...]
        sts = []
        for c in range(GLA_NC):
            st = st * decs[c] + kvs[c]
            st_ref[c] = st
            sts.append(st)
        state[...] = st
        o = jnp.concatenate([_dot_nt(q_ref[rows[c], :] * 0.125, _head_blocks(sts[c])) for c in range(GLA_NC)], axis=0)
        for h in range(H_GLA):
            vs = slice(h * DV, (h + 1) * DV)
            oh = o[:, vs]
            ohn = oh * lax.rsqrt(jnp.mean(oh * oh, axis=-1, keepdims=True) + EPS)
            gg = gg_ref[:, vs].astype(F32)
            o_ref[:, vs] = ((ohn * ggla_ref[:, vs]) * (gg * _sigmoid(gg))).astype(BF16)

    return _call(
        body, carried, [proj, proj, proj, proj, proj, wfg_p, bfg, ggla], name="gla_fwd", grid=(ns,),
        in_specs=_gla_specs4(ns, False) + [pl.BlockSpec((128, 256), lambda i: (0, 0)),
                                           pl.BlockSpec((1, 256), lambda i: (0, 0)),
                                           pl.BlockSpec((1, 512), lambda i: (0, 0))],
        out_specs=[pl.BlockSpec((GLA_R, 512), lambda i: (i, 0)),
                   pl.BlockSpec((GLA_NC, 128, 256), lambda i: (i, 0, 0))],
        out_shape=[jax.ShapeDtypeStruct((S, 512), BF16), jax.ShapeDtypeStruct((S // CHUNK, 128, 256), F32)],
        scratch_shapes=[pltpu.VMEM((128, 256), F32)],
        compiler_params=_cp(1))


def _gla_bwd(dcat, proj, states, wfg_p, bfg, ggla, carried=()):
    S = proj.shape[0]
    ns = S // GLA_R

    def body(do_ref, q_ref, k_ref, v_ref, gg_ref, gf_ref, sc_ref, sp_ref, wfg_ref, bfg_ref, ggla_ref,
             dp_ref, s_ref, dw_ref, carry):
        sr = pl.program_id(0)

        @pl.when(sr == 0)
        def _():
            carry[...] = jnp.zeros_like(carry)
            s_ref[...] = jnp.zeros_like(s_ref)
            dw_ref[...] = jnp.zeros_like(dw_ref)

        f, e, decs = _gla_gate4(gf_ref, wfg_ref, bfg_ref)
        kf = k_ref[...].astype(F32) * e
        kdec = kf.astype(BF16)
        rows = [slice(c * CHUNK, (c + 1) * CHUNK) for c in range(GLA_NC)]
        sts = [sc_ref[c] for c in range(GLA_NC)]
        st_before = jnp.where(sr < ns - 1, sp_ref[0], 0.0)
        sbd = [_head_blocks(sts[c]) for c in range(GLA_NC)]
        qs = q_ref[...] * 0.125
        o = jnp.concatenate([_dot_nt(qs[rows[c]], sbd[c]) for c in range(GLA_NC)], axis=0)
        dobs = []
        for h in range(H_GLA):
            vs = slice(h * DV, (h + 1) * DV)
            oh = o[:, vs]
            rr = lax.rsqrt(jnp.mean(oh * oh, axis=-1, keepdims=True) + EPS)
            ohn = oh * rr
            gg = gg_ref[:, vs].astype(F32)
            sg = _sigmoid(gg)
            dout = do_ref[:, vs].astype(F32)
            gl = ggla_ref[:, vs]
            dp_ref[:, 1024 + h * DV:1024 + (h + 1) * DV] = (
                dout * (ohn * gl) * (sg * (1.0 + gg * (1.0 - sg)))).astype(BF16)
            dt1 = dout * (gg * sg)
            s_ref[0:1, vs] += jnp.sum(dt1 * ohn, axis=0, keepdims=True)
            dohn = dt1 * gl
            dobs.append((rr * (dohn - ohn * jnp.mean(dohn * ohn, axis=-1, keepdims=True))).astype(BF16))
        dob = jnp.concatenate(dobs, axis=1)
        dsout = []
        for c in range(GLA_NC):
            dp_ref[rows[c], 0:256] = (_dot(dob[rows[c]], sbd[c]) * 0.125).astype(BF16)
            dsout.append(_head_diag(_dot_tn(dob[rows[c]], qs[rows[c]])))
        g = carry[...]
        gts, ddecs = [None] * GLA_NC, [None] * GLA_NC
        for c in reversed(range(GLA_NC)):
            gts[c] = dsout[c] + g
            ddecs[c] = jnp.sum(gts[c] * (sts[c - 1] if c > 0 else st_before), axis=0, keepdims=True) * decs[c]
            g = gts[c] * decs[c]
        carry[...] = g
        dkds = []
        for c in range(GLA_NC):
            gbd = _head_blocks(gts[c])
            dkds.append(_dot(v_ref[rows[c], :], gbd))
            dp_ref[rows[c], 512:1024] = _dot_nt(kdec[rows[c]], gbd).astype(BF16)
        dkd = jnp.concatenate(dkds, axis=0)
        dp_ref[:, 256:512] = (dkd * e).astype(BF16)
        wh, wl = _split(dkd * kf)
        dla = _dot(_chunk_tri(True), jnp.concatenate([wh, wl], axis=0)) + _per_chunk_rows(ddecs)
        df = dla * _sigmoid(-f) * (1.0 / 16.0)
        dfb = df.astype(BF16)
        s_ref[1:2, 0:256] += jnp.sum(df, axis=0, keepdims=True)
        dw_ref[...] += _dot_tn(gf_ref[...], dfb)
        dp_ref[:, 1536:1664] = _dot_nt(dfb, wfg_ref[...]).astype(BF16)

    return _call(
        body, carried, [dcat, proj, proj, proj, proj, proj, states, states, wfg_p, bfg, ggla],
        name="gla_bwd", grid=(ns,),
        in_specs=[pl.BlockSpec((GLA_R, 512), lambda i: (ns - 1 - i, 1))] + _gla_specs4(ns, True) + [
            pl.BlockSpec((GLA_NC, 128, 256), lambda i: (ns - 1 - i, 0, 0)),
            pl.BlockSpec((1, 128, 256), lambda i: (jnp.maximum((ns - 1 - i) * GLA_NC - 1, 0), 0, 0)),
            pl.BlockSpec((128, 256), lambda i: (0, 0)),
            pl.BlockSpec((1, 256), lambda i: (0, 0)),
            pl.BlockSpec((1, 512), lambda i: (0, 0))],
        out_specs=[pl.BlockSpec((GLA_R, 1664), lambda i: (ns - 1 - i, 0)),
                   pl.BlockSpec((8, 512), lambda i: (0, 0)),
                   pl.BlockSpec((128, 256), lambda i: (0, 0))],
        out_shape=[jax.ShapeDtypeStruct((S, 1664), BF16), jax.ShapeDtypeStruct((8, 512), F32),
                   jax.ShapeDtypeStruct((128, 256), F32)],
        scratch_shapes=[pltpu.VMEM((128, 256), F32)],
        compiler_params=_cp(1))


def _sum_leading(a, name):
    n = a.shape[0]

    def body(a_ref, o_ref):
        acc = a_ref[0]
        for k in range(1, n):
            acc = acc + a_ref[k]
        o_ref[...] = acc

    return pl.pallas_call(
        body, name=name, out_shape=jax.ShapeDtypeStruct(a.shape[1:], F32),
        in_specs=[VMEM_SPEC], out_specs=VMEM_SPEC,
    )(a)


def _sum_chip(own, recv, name):
    R, C = own.shape
    tr, tc = _tile2d(R, C, 1024 * 1024)

    def body(o_ref, r_ref, p_ref):
        acc = o_ref[...]
        for k in range(3):
            acc = acc + r_ref[k].astype(F32)
        p_ref[...] = acc

    return pl.pallas_call(
        body, name=name, grid=(R // tr, C // tc),
        in_specs=[pl.BlockSpec((tr, tc), lambda i, j: (i, j)), pl.BlockSpec((3, tr, tc), lambda i, j: (0, i, j))],
        out_specs=pl.BlockSpec((tr, tc), lambda i, j: (i, j)),
        out_shape=jax.ShapeDtypeStruct((R, C), F32), compiler_params=_cp(2, 40),
    )(own, recv)


def _adamw(w, p, q, m, v, name):
    R, C = w.shape
    tr, tc = _tile2d(R, C, 1024 * 1024)
    two = q is not None

    def body(*refs):
        if two:
            w_ref, p_ref, q_ref, m_ref, v_ref, g_out, d_out, m_out, v_out = refs
            g = p_ref[...] + q_ref[...]
        else:
            w_ref, p_ref, m_ref, v_ref, g_out, d_out, m_out, v_out = refs
            g = p_ref[...]
        m2 = B1 * m_ref[...] + (1.0 - B1) * g
        v2 = B2 * v_ref[...] + (1.0 - B2) * (g * g)
        m_hat = m2 / (1.0 - B1 ** STEP)
        v_hat = v2 / (1.0 - B2 ** STEP)
        g_out[...] = g
        d_out[...] = -LR * (m_hat / (jnp.sqrt(v_hat) + EPS_A) + WD * w_ref[...])
        m_out[...] = m2
        v_out[...] = v2

    spec = pl.BlockSpec((tr, tc), lambda i, j: (i, j))
    ins = [w, p, q, m, v] if two else [w, p, m, v]
    return pl.pallas_call(
        body, name=name, grid=(R // tr, C // tc),
        in_specs=[spec] * len(ins), out_specs=[spec] * 4,
        out_shape=[jax.ShapeDtypeStruct((R, C), F32)] * 4, compiler_params=_cp(2, 40),
    )(*ins)


def _reduce_big(own, recv, name):
    p = _sum_chip(own, recv, name + "_sum")
    return p, _pair_swap(p, name + "_swap")


def _cols_to_chips(a, width):
    return a.reshape(a.shape[0], 4, width).swapaxes(0, 1)


def _chips_to_cols(a):
    return a.swapaxes(0, 1).reshape(a.shape[1], 4 * a.shape[2])


def _swap_mid(a):
    lead = a.shape[:-1]
    return a.reshape(lead + (2, 2, HB)).swapaxes(-3, -2).reshape(lead + (4 * HB,))


def kernel(x, c, w_ada, b_ada, g_norm1, w_in, w_fg2, b_fg2, g_gla_out, w_out, g_norm2, w_up, w_conv, b_conv, w_down, g_final, loss_target, m_w_ada, m_b_ada, m_g_norm1, m_w_in, m_w_fg2, m_b_fg2, m_g_gla_out, m_w_out, m_g_norm2, m_w_up, m_w_conv, m_b_conv, m_w_down, m_g_final, v_w_ada, v_b_ada, v_g_norm1, v_w_in, v_w_fg2, v_b_fg2, v_g_gla_out, v_w_out, v_g_norm2, v_w_up, v_w_conv, v_b_conv, v_w_down, v_g_final):
    xi, yi, ci = lax.axis_index("x"), lax.axis_index("y"), lax.axis_index("c")
    cidx = 2 * xi + yi
    didx = 4 * xi + 2 * yi + ci
    xs = x[0]
    tgt = loss_target[0]
    gfin = g_final.reshape(1, D)
    AW = D * 6 // 4

    c_all = _allgather8(c, "gather_c").reshape(8, D)
    c_pad = jnp.concatenate([c_all, jnp.zeros((8, D), F32)], axis=0)
    mod_part = _ada_fwd(c_pad, w_ada[0], lax.dynamic_slice(b_ada, (0, cidx * AW), (1, AW)))[:8]
    small = jnp.concatenate([mod_part.reshape(-1), w_conv.reshape(-1), w_fg2.reshape(-1)]).reshape(-1, 128)
    small_g = _allgather4(small, "gather_small").reshape(4, -1)
    mod = lax.dynamic_index_in_dim(small_g[:, :8 * AW].reshape(4, 8, AW), didx, axis=1, keepdims=False).reshape(1, 6 * D)
    shift1, scale1, gate1, shift2, scale2, gate2 = [mod[:, k * D:(k + 1) * D] for k in range(6)]
    o1 = 8 * AW
    o2 = o1 + 3 * HB
    wc_p = _swap_mid(_chips_to_cols(small_g[:, o1:o2].reshape(4, 3, HB)))
    bc_p = _swap_mid(b_conv)
    wfg_full = _chips_to_cols(small_g[:, o2:].reshape(4, RANK, 64))
    wfg_p = jnp.concatenate([wfg_full, jnp.zeros((128 - RANK, 256), F32)], axis=0).astype(BF16)

    w_in_t = _allgather4_split(w_in[0].T.astype(BF16), "gather_w_in").reshape(N_IN, D)
    w_in_t = jnp.concatenate([w_in_t, jnp.zeros((N_IN_P - N_IN, D), BF16)], axis=0)
    w_in_p = w_in_t.T

    h = _norm_mod(xs, g_norm1, shift1, scale1)
    proj, (w_down_g,) = _mm([(h, w_in_p)], BF16, "mm_in", 256, N_IN_P, 48,
                            carried=[("gather", w_down[0].astype(BF16), False)])
    w_down_f = w_down_g.reshape(D_FF, D)
    (o_gla, states), (w_out_g,) = _gla_fwd(proj, wfg_p, b_fg2, g_gla_out,
                                           carried=[("gather", w_out[0].astype(BF16), False)])
    w_out_f = w_out_g.reshape(D, D)
    (o_sb, stats), (w_up_g,) = _sb_fwd(proj, carried=[("gather", w_up[0].astype(BF16), False)])
    w_up_p = _swap_mid(_chips_to_cols(w_up_g))
    x1, h2, mixed = _resid_norm_mod(xs, [(o_sb, w_out_f[:512]), (o_gla, w_out_f[512:])],
                                    gate1, g_norm2, shift2, scale2)
    u0p, a = _up_conv_glu(h2, w_up_p, wc_p, bc_p)
    dx2, dy2, s_fin = _final_loss(x1, [(a, w_down_f)], gate2, gfin, tgt)
    loss = lax.psum(0.5 / D * jnp.sum(s_fin[2]), ("x", "y", "c"))

    da = _mm([(dy2, w_down_f.T)], BF16, "mm_down_t", 512, D_FF, 48)
    dw_down, dw_down_h = [t.reshape(4, D_FF // 4, D) for t in _mm_tn([a], dy2, "mm_dw_down", D, 512, 56)]
    (du0p, s_conv), (rc_down,) = _conv_glu_bwd(da, u0p, wc_p, bc_p, carried=[("scatter", dw_down_h, False)])
    dw_up, dw_up_h = _mm_tn([h2], du0p, "mm_dw_up", HB, 512, 48)
    (dx1, dmixed, s_n2), _ = _norm_mod_bwd([(du0p, w_up_p.T)], x1, dx2, g_norm2, scale2, mixed, gate1,
                                           "mm_up_t_norm2_bwd")
    dcat = _mm([(dmixed, w_out_f.T)], BF16, "mm_out_t", 512, D)
    dw_out, dw_out_h = [t.reshape(4, D // 4, D) for t in _mm_tn([o_sb, o_gla], dmixed, "mm_dw_out", D, 512)]
    (dq, dk, dv), (rc_up,) = _sb_bwd(proj, dcat, stats, carried=[("scatter", dw_up_h, True)])
    (dp_gla, s_gla, dwfg), (rc_out,) = _gla_bwd(dcat, proj, states, wfg_p, b_fg2, g_gla_out,
                                                carried=[("scatter", dw_out_h, False)])
    dw_in, dw_in_h = _mm_tn([dq, dk, dv, dp_gla], h, "mm_dw_in", D, 512, 56)
    dw_in_h = dw_in_h[0, :N_IN].reshape(4, N_IN // 4, D)
    dw_in_own = lax.dynamic_slice(dw_in[0], (cidx * (N_IN // 4), 0), (N_IN // 4, D))
    dh, (rc_in,) = _mm(
        [(dq, w_in_t[:512]), (dk, w_in_t[512:1024]), (dv, w_in_t[1024:1536]), (dp_gla, w_in_t[1536:])],
        F32, "mm_in_t", 256, D, 48, carried=[("scatter", dw_in_h, False)])
    (gx, s_n1), _ = _norm_mod_bwd(dh, xs, dx1, g_norm1, scale1, None, None, "norm1_bwd")

    dmod = jnp.concatenate([s_n1[0], s_n1[1], s_n2[3], s_n2[0], s_n2[1], s_fin[1]])
    s_conv_n = _swap_mid(s_conv[:4])
    part = jnp.concatenate([dmod, s_n1[2], s_n2[2], s_fin[0], s_gla[0], s_gla[1, :256], s_conv_n[0],
                            s_conv_n[1:4].reshape(-1), dwfg[:RANK].reshape(-1)]).reshape(-1, 128)
    parts = _allgather8(part, "gather_small_grads")
    tot = _sum_leading(parts, "sum_small_grads").reshape(-1)
    dmod_all = parts.reshape(8, -1)[:, :6 * D]
    offs = [0]
    for n in (6 * D, D, D, D, 512, 256, 2 * D_FF, 3 * 2 * D_FF, RANK * 256):
        offs.append(offs[-1] + n)
    g_b_ada, g_g1, g_g2, g_gf, g_ggla, g_bfg, g_bconv, g_wconv_full, g_wfg_full = [
        tot[offs[k]:offs[k + 1]] for k in range(9)]
    g_wconv = lax.dynamic_index_in_dim(_cols_to_chips(g_wconv_full.reshape(3, 2 * D_FF), HB), cidx, 0, keepdims=False)
    g_wfg = lax.dynamic_index_in_dim(_cols_to_chips(g_wfg_full.reshape(RANK, 256), 64), cidx, 0, keepdims=False)

    dmod_pad = jnp.concatenate([dmod_all, jnp.zeros((8, 6 * D), F32)], axis=0)
    g_w_ada = _ada_bwd(c_pad, lax.dynamic_slice(dmod_pad, (0, cidx * AW), (16, AW)))

    def own(blocks, swapped=False):
        return lax.dynamic_index_in_dim(blocks, _slot(cidx, swapped), axis=0, keepdims=False)

    p_in, q_in = _reduce_big(dw_in_own, rc_in, "rs_w_in")
    p_out, q_out = _reduce_big(own(dw_out), rc_out, "rs_w_out")
    p_up, q_up = _reduce_big(own(dw_up, True), rc_up, "rs_w_up")
    p_down, q_down = _reduce_big(own(dw_down), rc_down, "rs_w_down")

    out = {}
    out["w_ada"] = _adamw(w_ada[0], g_w_ada, None, m_w_ada[0], v_w_ada[0], "adamw_w_ada")
    out["w_in"] = [t.T for t in _adamw(w_in[0].T, p_in, q_in, m_w_in[0].T, v_w_in[0].T, "adamw_w_in")]
    out["w_out"] = _adamw(w_out[0], p_out, q_out, m_w_out[0], v_w_out[0], "adamw_w_out")
    out["w_up"] = _adamw(w_up[0], p_up, q_up, m_w_up[0], v_w_up[0], "adamw_w_up")
    out["w_down"] = _adamw(w_down[0], p_down, q_down, m_w_down[0], v_w_down[0], "adamw_w_down")
    small_names = ["b_ada", "g_norm1", "w_fg2", "b_fg2", "g_gla_out", "g_norm2", "w_conv", "b_conv", "g_final"]
    small_w = [b_ada, g_norm1, w_fg2, b_fg2, g_gla_out, g_norm2, w_conv, b_conv, g_final]
    small_m = [m_b_ada, m_g_norm1, m_w_fg2, m_b_fg2, m_g_gla_out, m_g_norm2, m_w_conv, m_b_conv, m_g_final]
    small_v = [v_b_ada, v_g_norm1, v_w_fg2, v_b_fg2, v_g_gla_out, v_g_norm2, v_w_conv, v_b_conv, v_g_final]
    small_gr = [g_b_ada, g_g1, g_wfg, g_bfg, g_ggla, g_g2, g_wconv, g_bconv, g_gf]

    def pack(arrs):
        flat = jnp.concatenate([t.reshape(-1) for t in arrs])
        return jnp.concatenate([flat, jnp.zeros((-flat.shape[0]) % 1024, F32)]).reshape(-1, 128)

    packed = _adamw(pack(small_w), pack(small_gr), None, pack(small_m), pack(small_v), "adamw_small")
    off = 0
    for nm, wt in zip(small_names, small_w):
        n = wt.size
        out[nm] = [t.reshape(-1)[off:off + n].reshape(wt.shape) for t in packed]
        off += n
    for nm in ("w_ada", "w_in", "w_out", "w_up", "w_down"):
        out[nm] = [t[None] for t in out[nm]]

    names = ["w_ada", "b_ada", "g_norm1", "w_in", "w_fg2", "b_fg2", "g_gla_out", "w_out", "g_norm2", "w_up",
             "w_conv", "b_conv", "w_down", "g_final"]
    res = [loss, gx[None]]
    for k in range(4):
        res += [out[nm][k] for nm in names]
    return tuple(res)
```

```python
import functools

import jax
import jax.numpy as jnp
from jax import lax
from jax.experimental import pallas as pl
from jax.experimental.pallas import tpu as pltpu

F32 = jnp.float32
BF16 = jnp.bfloat16
MESH = pl.DeviceIdType.MESH

D = 1024
H_SB = 8
DK = 64
DV = 128
H_GLA = 4
CHUNK = 64
RANK = 16
N_IN = 3088
N_IN_P = 3200
D_FF = 2816
HB = D_FF // 2
LANES = 128
EPS = 1e-6
QB = 128
SB_SKIP = -120.0
SB_HEAD_ROWS = 64

LR, B1, B2, EPS_A, WD, STEP = 0.001, 0.9, 0.999, 1e-08, 0.01, 10

ANY = pl.BlockSpec(memory_space=pl.ANY)
VMEM_SPEC = pl.BlockSpec(memory_space=pltpu.VMEM)
ONE_BUF = pl.Buffered(1)


def _cp(ndim=0, vmem_mb=None):
    kw = {}
    if ndim:
        kw["dimension_semantics"] = ("arbitrary",) * ndim
    if vmem_mb:
        kw["vmem_limit_bytes"] = vmem_mb * 1024 * 1024
    return pltpu.CompilerParams(**kw)


def _dot(a, b):
    return jnp.dot(a, b, preferred_element_type=F32)


def _dot_nt(a, b):
    return lax.dot_general(a, b, (((1,), (1,)), ((), ())), preferred_element_type=F32)


def _dot_tn(a, b):
    return lax.dot_general(a, b, (((0,), (0,)), ((), ())), preferred_element_type=F32)


def _split(x):
    hi = x.astype(BF16)
    lo = (x - hi.astype(F32)).astype(BF16)
    return hi, lo


def _sigmoid(x):
    return jax.nn.sigmoid(x)


def _sigmoid_fast(x):
    return pl.reciprocal(1.0 + jnp.exp(-x), approx=True)


def _log_sigmoid_parts(z):
    e = jnp.exp(-jnp.abs(z))
    sp = jnp.log1p(e)
    return -(jnp.maximum(z, 0.0) + sp), jnp.minimum(z, 0.0) - sp, e


def _tile2d(rows, cols, budget=512 * 1024):
    best = None
    for t in range(8, rows + 1, 8):
        if rows % t == 0 and t * cols * 4 <= budget:
            best = t
    if best is not None:
        return best, cols
    best = LANES if cols % LANES == 0 else cols
    for t in range(LANES, cols + 1, LANES):
        if cols % t == 0 and rows * t * 4 <= budget:
            best = t
    return rows, best


def _flip(v, bit):
    return 1 - v if bit else v


def _allgather8(a, name):
    def body(a_ref, o_ref, ssem, rsem, lsem):
        x, y, c = lax.axis_index("x"), lax.axis_index("y"), lax.axis_index("c")
        me = 4 * x + 2 * y + c
        loc = pltpu.make_async_copy(a_ref, o_ref.at[me], lsem)
        loc.start()
        sends = []
        for r in range(1, 8):
            peer = (_flip(x, r & 4), _flip(y, r & 2), _flip(c, r & 1))
            cp = pltpu.make_async_remote_copy(
                src_ref=a_ref, dst_ref=o_ref.at[me], send_sem=ssem.at[r - 1], recv_sem=rsem.at[r - 1],
                device_id=peer, device_id_type=MESH)
            cp.start()
            sends.append(cp)
        for r in range(1, 8):
            peer = (_flip(x, r & 4), _flip(y, r & 2), _flip(c, r & 1))
            pidx = 4 * peer[0] + 2 * peer[1] + peer[2]
            pltpu.make_async_remote_copy(
                src_ref=a_ref, dst_ref=o_ref.at[pidx], send_sem=ssem.at[r - 1], recv_sem=rsem.at[r - 1],
                device_id=peer, device_id_type=MESH).wait_recv()
        for cp in sends:
            cp.wait_send()
        loc.wait()

    return pl.pallas_call(
        body, name=name,
        out_shape=jax.ShapeDtypeStruct((8,) + a.shape, a.dtype),
        in_specs=[VMEM_SPEC], out_specs=VMEM_SPEC,
        scratch_shapes=[pltpu.SemaphoreType.DMA((7,)), pltpu.SemaphoreType.DMA((7,)), pltpu.SemaphoreType.DMA],
    )(a)


def _allgather4(a, name):
    def body(a_ref, o_ref, ssem, rsem, lsem):
        x, y, c = lax.axis_index("x"), lax.axis_index("y"), lax.axis_index("c")
        me = 2 * x + y
        loc = pltpu.make_async_copy(a_ref, o_ref.at[me], lsem)
        loc.start()
        sends = []
        for r in range(1, 4):
            peer = (_flip(x, r & 2), _flip(y, r & 1), c)
            cp = pltpu.make_async_remote_copy(
                src_ref=a_ref, dst_ref=o_ref.at[me], send_sem=ssem.at[r - 1], recv_sem=rsem.at[r - 1],
                device_id=peer, device_id_type=MESH)
            cp.start()
            sends.append(cp)
        for r in range(1, 4):
            peer = (_flip(x, r & 2), _flip(y, r & 1), c)
            pidx = 2 * peer[0] + peer[1]
            pltpu.make_async_remote_copy(
                src_ref=a_ref, dst_ref=o_ref.at[pidx], send_sem=ssem.at[r - 1], recv_sem=rsem.at[r - 1],
                device_id=peer, device_id_type=MESH).wait_recv()
        for cp in sends:
            cp.wait_send()
        loc.wait()

    return pl.pallas_call(
        body, name=name,
        out_shape=jax.ShapeDtypeStruct((4,) + a.shape, a.dtype),
        in_specs=[ANY], out_specs=ANY,
        scratch_shapes=[pltpu.SemaphoreType.DMA((3,)), pltpu.SemaphoreType.DMA((3,)), pltpu.SemaphoreType.DMA],
    )(a)


def _allgather4_split(a, name):
    R, C = a.shape
    hc = C // 2

    def body(a_ref, o_ref, ssem, rsem, fssem, frsem, lsem):
        x, y, c = lax.axis_index("x"), lax.axis_index("y"), lax.axis_index("c")
        me = 2 * x + y
        sibling = (x, y, 1 - c)
        mine = pl.ds(pl.multiple_of(c * hc, hc), hc)
        theirs = pl.ds(pl.multiple_of((1 - c) * hc, hc), hc)
        loc = pltpu.make_async_copy(a_ref, o_ref.at[me], lsem)
        loc.start()
        peers = [(_flip(x, r & 2), _flip(y, r & 1), c) for r in range(1, 4)]
        pidx = [2 * p[0] + p[1] for p in peers]
        sends = []
        for k in range(3):
            cp = pltpu.make_async_remote_copy(
                src_ref=a_ref.at[:, mine], dst_ref=o_ref.at[me, :, mine], send_sem=ssem.at[k], recv_sem=rsem.at[k],
                device_id=peers[k], device_id_type=MESH)
            cp.start()
            sends.append(cp)
        for k in range(3):
            landed = o_ref.at[pidx[k], :, mine]
            pltpu.make_async_remote_copy(
                src_ref=landed, dst_ref=landed, send_sem=ssem.at[k], recv_sem=rsem.at[k],
                device_id=peers[k], device_id_type=MESH).wait_recv()
            cp = pltpu.make_async_remote_copy(
                src_ref=landed, dst_ref=landed, send_sem=fssem.at[k], recv_sem=frsem.at[k],
                device_id=sibling, device_id_type=MESH)
            cp.start()
            sends.append(cp)
        for k in range(3):
            got = o_ref.at[pidx[k], :, theirs]
            pltpu.make_async_remote_copy(
                src_ref=got, dst_ref=got, send_sem=fssem.at[k], recv_sem=frsem.at[k],
                device_id=sibling, device_id_type=MESH).wait_recv()
        for cp in sends:
            cp.wait_send()
        loc.wait()

    return pl.pallas_call(
        body, name=name,
        out_shape=jax.ShapeDtypeStruct((4,) + a.shape, a.dtype),
        in_specs=[ANY], out_specs=ANY,
        scratch_shapes=[pltpu.SemaphoreType.DMA((3,))] * 4 + [pltpu.SemaphoreType.DMA],
    )(a)


def _slot(chip, swapped):
    return 2 * (chip % 2) + chip // 2 if swapped else chip


def _pair_swap(p, name):
    def body(p_ref, o_ref, ssem, rsem):
        x, y, c = lax.axis_index("x"), lax.axis_index("y"), lax.axis_index("c")
        cp = pltpu.make_async_remote_copy(
            src_ref=p_ref, dst_ref=o_ref, send_sem=ssem, recv_sem=rsem,
            device_id=(x, y, 1 - c), device_id_type=MESH)
        cp.start()
        cp.wait()

    return pl.pallas_call(
        body, name=name,
        out_shape=jax.ShapeDtypeStruct(p.shape, p.dtype),
        in_specs=[ANY], out_specs=ANY,
        scratch_shapes=[pltpu.SemaphoreType.DMA, pltpu.SemaphoreType.DMA],
    )(p)


def _carried_copies(kind, src_ref, dst_ref, sems, swapped):
    ssem, rsem, lsem = sems
    x, y, c = lax.axis_index("x"), lax.axis_index("y"), lax.axis_index("c")
    me = 2 * x + y
    starts, recvs = [], []
    if kind == "gather":
        starts.append(pltpu.make_async_copy(src_ref, dst_ref.at[me], lsem))
    for r in range(1, 4):
        peer = (_flip(x, r & 2), _flip(y, r & 1), c)
        pidx = 2 * peer[0] + peer[1]
        if kind == "gather":
            src, dst, landed = src_ref, dst_ref.at[me], dst_ref.at[pidx]
        else:
            src = src_ref.at[2 * peer[1] + peer[0] if swapped else pidx]
            dst = landed = dst_ref.at[r - 1]
        starts.append(pltpu.make_async_remote_copy(
            src_ref=src, dst_ref=dst, send_sem=ssem.at[r - 1], recv_sem=rsem.at[r - 1],
            device_id=peer, device_id_type=MESH))
        recvs.append(pltpu.make_async_remote_copy(
            src_ref=src, dst_ref=landed, send_sem=ssem.at[r - 1], recv_sem=rsem.at[r - 1],
            device_id=peer, device_id_type=MESH))
    return starts, recvs


def _call(body, carried, operands, *, name, grid, in_specs, out_specs, out_shape, scratch_shapes=(),
          compiler_params=None):
    single = not isinstance(out_shape, (list, tuple))
    out_specs = [out_specs] if single else list(out_specs)
    out_shape = [out_shape] if single else list(out_shape)
    n_in, n_out, n_sc, nh = len(operands), len(out_shape), len(scratch_shapes), len(carried)

    def full(*refs):
        ins, h_in = refs[:n_in], refs[n_in:n_in + nh]
        o0 = n_in + nh
        outs, h_out = refs[o0:o0 + n_out], refs[o0 + n_out:o0 + n_out + nh]
        s0 = o0 + n_out + nh
        scratch, sems = refs[s0:s0 + n_sc], refs[s0 + n_sc:]
        first = last = None
        for d in range(len(grid)):
            f = pl.program_id(d) == 0
            l = pl.program_id(d) == pl.num_programs(d) - 1
            first = f if first is None else jnp.logical_and(first, f)
            last = l if last is None else jnp.logical_and(last, l)

        def copies(t):
            return _carried_copies(carried[t][0], h_in[t], h_out[t], sems[3 * t:3 * t + 3], carried[t][2])

        if nh:
            @pl.when(first)
            def _():
                for t in range(nh):
                    for cp in copies(t)[0]:
                        cp.start()

        body(*ins, *outs, *scratch)

        if nh:
            @pl.when(last)
            def _():
                for t in range(nh):
                    starts, recvs = copies(t)
                    for cp in recvs:
                        cp.wait_recv()
                    for cp in starts:
                        if carried[t][0] == "gather" and cp is starts[0]:
                            cp.wait()
                        else:
                            cp.wait_send()

    h_shapes = [jax.ShapeDtypeStruct(((4,) + arr.shape) if kind == "gather" else ((3,) + arr.shape[1:]), arr.dtype)
                for kind, arr, _ in carried]
    sem_shapes = [pltpu.SemaphoreType.DMA((3,)), pltpu.SemaphoreType.DMA((3,)), pltpu.SemaphoreType.DMA] * nh
    res = pl.pallas_call(
        full, name=name, grid=grid, in_specs=list(in_specs) + [ANY] * nh, out_specs=out_specs + [ANY] * nh,
        out_shape=out_shape + h_shapes, scratch_shapes=list(scratch_shapes) + sem_shapes,
        compiler_params=compiler_params,
    )(*operands, *[arr for _, arr, _ in carried])
    main = res[:n_out]
    return (main[0] if single else main), list(res[n_out:])


def _mm(pairs, out_dtype, name, tm, tn, vmem_mb=None, carried=()):
    S = pairs[0][0].shape[0]
    N = pairs[0][1].shape[1]
    tm = min(tm, S)
    np_ = len(pairs)

    def body(*refs):
        acc = _dot(refs[0][...], refs[1][...])
        for t in range(1, np_):
            acc = acc + _dot(refs[2 * t][...], refs[2 * t + 1][...])
        refs[-1][...] = acc.astype(refs[-1].dtype)

    in_specs, ops = [], []
    for a, w in pairs:
        in_specs += [pl.BlockSpec((tm, a.shape[1]), lambda n, i: (i, 0)),
                     pl.BlockSpec((w.shape[0], tn), lambda n, i: (0, n))]
        ops += [a, w]
    out, got = _call(
        body, carried, ops, name=name, grid=(N // tn, S // tm), in_specs=in_specs,
        out_specs=pl.BlockSpec((tm, tn), lambda n, i: (i, n)),
        out_shape=jax.ShapeDtypeStruct((S, N), out_dtype),
        compiler_params=_cp(2, vmem_mb))
    return (out, got) if carried else out


def _mm_tn(a_list, b, name, bn, tk, vmem_mb=None):
    S, N = b.shape
    ms = [a.shape[1] for a in a_list]
    M = sum(ms)
    tk = min(tk, S)
    na = len(a_list)

    def body(*refs):
        b_ref, o_ref, o16_ref = refs[na], refs[na + 1], refs[na + 2]

        @pl.when(pl.program_id(1) == 0)
        def _():
            o_ref[...] = jnp.zeros_like(o_ref)
        off = 0
        for t in range(na):
            o_ref[off:off + ms[t], :] += _dot_tn(refs[t][...], b_ref[...])
            off += ms[t]

        @pl.when(pl.program_id(1) == pl.num_programs(1) - 1)
        def _():
            o16_ref[...] = o_ref[...].astype(BF16)

    spec = pl.BlockSpec((None, M, bn), lambda n, k: (n, 0, 0))
    return pl.pallas_call(
        body, name=name, grid=(N // bn, S // tk),
        in_specs=[pl.BlockSpec((tk, m), lambda n, k: (k, 0)) for m in ms] + [pl.BlockSpec((tk, bn), lambda n, k: (k, n))],
        out_specs=[spec, spec],
        out_shape=[jax.ShapeDtypeStruct((N // bn, M, bn), F32), jax.ShapeDtypeStruct((N // bn, M, bn), BF16)],
        compiler_params=_cp(2, vmem_mb),
    )(*a_list, b)


def _ada_fwd(c_all, w_sh, b_sh):
    def body(c_ref, w_ref, b_ref, o_ref):
        cv = c_ref[...]
        sc = (cv * _sigmoid(cv)).astype(BF16)
        o_ref[...] = _dot(sc, w_ref[...].astype(BF16)) + b_ref[...]

    return pl.pallas_call(
        body, name="ada_fwd", out_shape=jax.ShapeDtypeStruct((c_all.shape[0], w_sh.shape[1]), F32),
        in_specs=[VMEM_SPEC] * 3, out_specs=VMEM_SPEC, compiler_params=_cp(0, 40),
    )(c_all, w_sh, b_sh)


def _ada_bwd(c_all, dmod_sh):
    def body(c_ref, d_ref, o_ref):
        cv = c_ref[...]
        sc = (cv * _sigmoid(cv)).astype(BF16)
        o_ref[...] = _dot_tn(sc, d_ref[...].astype(BF16))

    return pl.pallas_call(
        body, name="ada_bwd", out_shape=jax.ShapeDtypeStruct((c_all.shape[1], dmod_sh.shape[1]), F32),
        in_specs=[VMEM_SPEC] * 2, out_specs=VMEM_SPEC, compiler_params=_cp(0, 40),
    )(c_all, dmod_sh)


def _vec(tm_unused=None):
    return pl.BlockSpec((1, D), lambda i: (0, 0))


def _rows(tm, width=D):
    return pl.BlockSpec((tm, width), lambda i: (i, 0))


def _norm_mod(x, g, shift, scale, tm=512):
    S = x.shape[0]
    tm = min(tm, S)

    def body(x_ref, g_ref, sh_ref, sc_ref, h_ref):
        xv = x_ref[...]
        r = lax.rsqrt(jnp.mean(xv * xv, axis=-1, keepdims=True) + EPS)
        hn = (xv * r) * g_ref[...]
        h_ref[...] = (hn * (1.0 + sc_ref[...]) + sh_ref[...]).astype(BF16)

    return pl.pallas_call(
        body, name="norm1_mod", grid=(S // tm,),
        in_specs=[_rows(tm), _vec(), _vec(), _vec()], out_specs=_rows(tm),
        out_shape=jax.ShapeDtypeStruct((S, D), BF16), compiler_params=_cp(1),
    )(x, g, shift, scale)


def _mm_rows(pairs, tm):
    ops, specs = [], []
    for a, w in pairs:
        ops += [a, w]
        specs += [pl.BlockSpec((tm, a.shape[1]), lambda i: (i, 0)),
                  pl.BlockSpec(w.shape, lambda i: (0, 0), pipeline_mode=ONE_BUF)]
    return ops, specs


def _mm_rows_value(refs, npairs):
    acc = _dot(refs[0][...], refs[1][...])
    for t in range(1, npairs):
        acc = acc + _dot(refs[2 * t][...], refs[2 * t + 1][...])
    return acc


def _resid_norm_mod(x, mm, gate, g, shift, scale, tm=256):
    S = x.shape[0]
    tm = min(tm, S)
    skip = 2 * len(mm)

    def body(*refs):
        x_ref, gt_ref, g_ref, sh_ref, sc_ref, x1_ref, h_ref, m_ref = refs[skip:]
        mixed = _mm_rows_value(refs, len(mm))
        m_ref[...] = mixed
        x1 = x_ref[...] + (1.0 + gt_ref[...]) * mixed
        x1_ref[...] = x1
        r = lax.rsqrt(jnp.mean(x1 * x1, axis=-1, keepdims=True) + EPS)
        hn = (x1 * r) * g_ref[...]
        h_ref[...] = (hn * (1.0 + sc_ref[...]) + sh_ref[...]).astype(BF16)

    ops, specs = _mm_rows(mm, tm)
    return pl.pallas_call(
        body, name="mm_out_resid_norm2_mod", grid=(S // tm,),
        in_specs=specs + [_rows(tm), _vec(), _vec(), _vec(), _vec()],
        out_specs=[_rows(tm), _rows(tm), _rows(tm)],
        out_shape=[jax.ShapeDtypeStruct((S, D), F32), jax.ShapeDtypeStruct((S, D), BF16),
                   jax.ShapeDtypeStruct((S, D), F32)],
        compiler_params=_cp(1, 40),
    )(*ops, x, gate, g, shift, scale)


def _conv3(ext, w_ref, b_ref, cs):
    e1 = pltpu.roll(ext, 1, 0)
    e2 = pltpu.roll(ext, 2, 0)
    u = b_ref[:, cs] + w_ref[0:1, cs] * e2
    u = u + w_ref[1:2, cs] * e1
    u = u + w_ref[2:3, cs] * ext
    return u, e1, e2


def _up_conv_glu(h2, w_up_p, wc_p, bc_p, tm=256):
    S = h2.shape[0]
    tm = min(tm, S)
    widths = [2 * LANES] * (HB // (2 * LANES)) + ([LANES] if HB % (2 * LANES) else [])

    def body(h_ref, wu_ref, w_ref, b_ref, u_ref, a_ref, prev_ref):
        @pl.when(pl.program_id(0) == 0)
        def _():
            prev_ref[...] = jnp.zeros_like(prev_ref)

        hv = h_ref[...]
        for j in range(2):
            base = 0
            for wd in widths:
                us = []
                for off in (2 * j * HB + base, 2 * j * HB + HB + base):
                    cb = _dot(hv, wu_ref[:, off:off + wd]).astype(BF16)
                    u_ref[:, off:off + wd] = cb
                    for q in range(wd // LANES):
                        cs = slice(off + q * LANES, off + (q + 1) * LANES)
                        cq = cb[:, q * LANES:(q + 1) * LANES]
                        ext = jnp.concatenate([prev_ref[:, cs].astype(F32), cq.astype(F32)], axis=0)
                        us.append(_conv3(ext, w_ref, b_ref, cs)[0][16:])
                        prev_ref[:, cs] = cq[tm - 16:]
                nq = wd // LANES
                for q in range(nq):
                    val, gt = us[q], us[nq + q]
                    a_ref[:, j * HB + base + q * LANES:j * HB + base + (q + 1) * LANES] = (
                        val * (gt * _sigmoid_fast(gt))).astype(BF16)
                base += wd

    return pl.pallas_call(
        body, name="mm_up_conv_glu", grid=(S // tm,),
        in_specs=[pl.BlockSpec((tm, D), lambda i: (i, 0)),
                  pl.BlockSpec((D, 2 * D_FF), lambda i: (0, 0), pipeline_mode=ONE_BUF),
                  pl.BlockSpec((3, 2 * D_FF), lambda i: (0, 0)),
                  pl.BlockSpec((1, 2 * D_FF), lambda i: (0, 0))],
        out_specs=[pl.BlockSpec((tm, 2 * D_FF), lambda i: (i, 0)), pl.BlockSpec((tm, D_FF), lambda i: (i, 0))],
        out_shape=[jax.ShapeDtypeStruct((S, 2 * D_FF), BF16), jax.ShapeDtypeStruct((S, D_FF), BF16)],
        scratch_shapes=[pltpu.VMEM((16, 2 * D_FF), BF16)],
        compiler_params=_cp(1, 48),
    )(h2, w_up_p, wc_p, bc_p)


def _conv_glu_up_norm2_bwd(da, u0p, wc_p, bc_p, w_up_t, x1, dx2, g, scale, mixed, gate, tm=256, carried=()):
    S = u0p.shape[0]
    tm = min(tm, S)
    hb = tm // 16
    nlast = S // 16 - 1
    widths = [2 * LANES] * (HB // (2 * LANES)) + ([LANES] if HB % (2 * LANES) else [])

    def body(da_ref, dan_ref, u_ref, p_ref, n_ref, w_ref, b_ref, wt_ref, x_ref, dr_ref, g_ref, sc_ref, m_ref, gt_ref,
             o_ref, s_ref, dx_ref, dm_ref, s2_ref, acc_ref):
        i = pl.program_id(0)
        first = i == 0
        last = i == pl.num_programs(0) - 1

        @pl.when(first)
        def _():
            s_ref[...] = jnp.zeros_like(s_ref)
            s2_ref[...] = jnp.zeros_like(s2_ref)

        n = tm + 16
        started = False
        for j in range(2):
            base = 0
            for wd in widths:
                du0s = ([], [])
                for q in range(wd // LANES):
                    k0 = base + q * LANES
                    kc = slice(j * HB + k0, j * HB + k0 + LANES)
                    dae = jnp.concatenate([da_ref[:, kc].astype(F32),
                                           jnp.where(last, 0.0, dan_ref[:, kc].astype(F32))], axis=0)
                    halves = []
                    for off in (2 * j * HB + k0, 2 * j * HB + HB + k0):
                        cs = slice(off, off + LANES)
                        ext = jnp.concatenate([jnp.where(first, 0.0, p_ref[:, cs].astype(F32)),
                                               u_ref[:, cs].astype(F32), n_ref[:, cs].astype(F32)], axis=0)
                        u, e1, e2 = _conv3(ext, w_ref, b_ref, cs)
                        halves.append((u[16:], ext[16:16 + tm], e1[16:16 + tm], e2[16:16 + tm], cs))
                    val, gt = halves[0][0], halves[1][0]
                    sg = _sigmoid_fast(gt)
                    dus = (dae * (gt * sg), dae * val * (sg * (1.0 + gt * (1.0 - sg))))
                    for t, (du, (_, x0, x1_, x2, cs)) in enumerate(zip(dus, halves)):
                        du0 = (w_ref[2:3, cs] * du + w_ref[1:2, cs] * pltpu.roll(du, n - 1, 0)
                               + w_ref[0:1, cs] * pltpu.roll(du, n - 2, 0))[:tm].astype(BF16)
                        o_ref[:, cs] = du0
                        du0s[t].append(du0)
                        dut = du[:tm]
                        s_ref[0:1, cs] += jnp.sum(dut, axis=0, keepdims=True)
                        s_ref[1:2, cs] += jnp.sum(dut * x2, axis=0, keepdims=True)
                        s_ref[2:3, cs] += jnp.sum(dut * x1_, axis=0, keepdims=True)
                        s_ref[3:4, cs] += jnp.sum(dut * x0, axis=0, keepdims=True)
                for t, off in enumerate((2 * j * HB + base, 2 * j * HB + HB + base)):
                    lhs = du0s[t][0] if len(du0s[t]) == 1 else jnp.concatenate(du0s[t], axis=1)
                    part = _dot(lhs, wt_ref[off:off + wd, :])
                    if started:
                        acc_ref[...] += part
                    else:
                        acc_ref[...] = part
                        started = True
                base += wd

        xv = x_ref[...]
        dhv = acc_ref[...]
        r = lax.rsqrt(jnp.mean(xv * xv, axis=-1, keepdims=True) + EPS)
        nv = xv * r
        gv = g_ref[...]
        hn = nv * gv
        dhn = dhv * (1.0 + sc_ref[...])
        dn = dhn * gv
        dx = dr_ref[...] + r * (dn - nv * jnp.mean(dn * nv, axis=-1, keepdims=True))
        dx_ref[...] = dx
        dm_ref[...] = (dx * (1.0 + gt_ref[...])).astype(BF16)
        s2_ref[0:1, :] += jnp.sum(dhv, axis=0, keepdims=True)
        s2_ref[1:2, :] += jnp.sum(dhv * hn, axis=0, keepdims=True)
        s2_ref[2:3, :] += jnp.sum(dhn * nv, axis=0, keepdims=True)
        s2_ref[3:4, :] += jnp.sum(dx * m_ref[...], axis=0, keepdims=True)

    full = 2 * D_FF
    return _call(
        body, carried, [da, da, u0p, u0p, u0p, wc_p, bc_p, w_up_t, x1, dx2, g, scale, mixed, gate],
        name="conv_glu_up_norm2_bwd", grid=(S // tm,),
        in_specs=[pl.BlockSpec((tm, D_FF), lambda i: (i, 0)),
                  pl.BlockSpec((16, D_FF), lambda i: (jnp.minimum((i + 1) * hb, nlast), 0)),
                  pl.BlockSpec((tm, full), lambda i: (i, 0)),
                  pl.BlockSpec((16, full), lambda i: (jnp.maximum(i * hb - 1, 0), 0)),
                  pl.BlockSpec((16, full), lambda i: (jnp.minimum((i + 1) * hb, nlast), 0)),
                  pl.BlockSpec((3, full), lambda i: (0, 0)),
                  pl.BlockSpec((1, full), lambda i: (0, 0)),
                  pl.BlockSpec((full, D), lambda i: (0, 0), pipeline_mode=ONE_BUF),
                  _rows(tm), _rows(tm), _vec(), _vec(), _rows(tm), _vec()],
        out_specs=[pl.BlockSpec((tm, full), lambda i: (i, 0)), pl.BlockSpec((8, full), lambda i: (0, 0)),
                   _rows(tm), _rows(tm), pl.BlockSpec((8, D), lambda i: (0, 0))],
        out_shape=[jax.ShapeDtypeStruct((S, full), BF16), jax.ShapeDtypeStruct((8, full), F32),
                   jax.ShapeDtypeStruct((S, D), F32), jax.ShapeDtypeStruct((S, D), BF16),
                   jax.ShapeDtypeStruct((8, D), F32)],
        scratch_shapes=[pltpu.VMEM((tm, D), F32)],
        compiler_params=_cp(1, 56))


def _final_loss(x1, mm, gate2, g_final, target, tm=256):
    S = x1.shape[0]
    tm = min(tm, S)
    skip = 2 * len(mm)

    def body(*refs):
        x1_ref, gt_ref, g_ref, t_ref, dx_ref, dy_ref, s_ref = refs[skip:]

        @pl.when(pl.program_id(0) == 0)
        def _():
            s_ref[...] = jnp.zeros_like(s_ref)

        y2 = _mm_rows_value(refs, len(mm))
        og = 1.0 + gt_ref[...]
        x2 = x1_ref[...] + og * y2
        r = lax.rsqrt(jnp.mean(x2 * x2, axis=-1, keepdims=True) + EPS)
        n = x2 * r
        g = g_ref[...]
        err = n * g - t_ref[...]
        dy = err * (1.0 / D)
        dn = dy * g
        dx2 = r * (dn - n * jnp.mean(dn * n, axis=-1, keepdims=True))
        dx_ref[...] = dx2
        dy_ref[...] = (dx2 * og).astype(BF16)
        s_ref[0:1, :] += jnp.sum(dy * n, axis=0, keepdims=True)
        s_ref[1:2, :] += jnp.sum(dx2 * y2, axis=0, keepdims=True)
        s_ref[2:3, :] += jnp.sum(err * err, axis=0, keepdims=True)

    ops, specs = _mm_rows(mm, tm)
    return pl.pallas_call(
        body, name="mm_down_final_loss", grid=(S // tm,),
        in_specs=specs + [_rows(tm), _vec(), _vec(), _rows(tm)],
        out_specs=[_rows(tm), _rows(tm), pl.BlockSpec((8, D), lambda i: (0, 0))],
        out_shape=[jax.ShapeDtypeStruct((S, D), F32), jax.ShapeDtypeStruct((S, D), BF16),
                   jax.ShapeDtypeStruct((8, D), F32)],
        compiler_params=_cp(1, 40),
    )(*ops, x1, gate2, g_final, target)


def _norm_mod_bwd(dh, xin, dres, g, scale, mixed, gate, name, tm=256, carried=()):
    S = xin.shape[0]
    tm = min(tm, S)
    with_gate = mixed is not None
    fused = isinstance(dh, list)
    skip = 2 * len(dh) if fused else 1

    def body(*refs):
        if with_gate:
            x_ref, dr_ref, g_ref, sc_ref, m_ref, gt_ref, dx_ref, dm_ref, s_ref = refs[skip:]
        else:
            x_ref, dr_ref, g_ref, sc_ref, dx_ref, s_ref = refs[skip:]

        @pl.when(pl.program_id(0) == 0)
        def _():
            s_ref[...] = jnp.zeros_like(s_ref)

        xv = x_ref[...]
        dhv = _mm_rows_value(refs, len(dh)) if fused else refs[0][...]
        r = lax.rsqrt(jnp.mean(xv * xv, axis=-1, keepdims=True) + EPS)
        n = xv * r
        g = g_ref[...]
        hn = n * g
        dhn = dhv * (1.0 + sc_ref[...])
        dn = dhn * g
        dx = dr_ref[...] + r * (dn - n * jnp.mean(dn * n, axis=-1, keepdims=True))
        dx_ref[...] = dx
        s_ref[0:1, :] += jnp.sum(dhv, axis=0, keepdims=True)
        s_ref[1:2, :] += jnp.sum(dhv * hn, axis=0, keepdims=True)
        s_ref[2:3, :] += jnp.sum(dhn * n, axis=0, keepdims=True)
        if with_gate:
            dm_ref[...] = (dx * (1.0 + gt_ref[...])).astype(BF16)
            s_ref[3:4, :] += jnp.sum(dx * m_ref[...], axis=0, keepdims=True)

    ins, in_specs = _mm_rows(dh, tm) if fused else ([dh], [_rows(tm)])
    ins += [xin, dres, g, scale]
    in_specs += [_rows(tm), _rows(tm), _vec(), _vec()]
    out_specs = [_rows(tm)]
    out_shape = [jax.ShapeDtypeStruct((S, D), F32)]
    if with_gate:
        ins += [mixed, gate]
        in_specs += [_rows(tm), _vec()]
        out_specs.append(_rows(tm))
        out_shape.append(jax.ShapeDtypeStruct((S, D), BF16))
    out_specs.append(pl.BlockSpec((8, D), lambda i: (0, 0)))
    out_shape.append(jax.ShapeDtypeStruct((8, D), F32))
    return _call(body, carried, ins, name=name, grid=(S // tm,), in_specs=in_specs, out_specs=out_specs,
                 out_shape=out_shape, compiler_params=_cp(1, 48 if fused else None))


def _tri(n, rel):
    row = lax.broadcasted_iota(jnp.int32, (n, n), 0)
    col = lax.broadcasted_iota(jnp.int32, (n, n), 1)
    return {"gt": row > col, "ge": row >= col, "lt": row < col, "le": row <= col}[rel]


def _pair_diag(mask):
    u = jnp.where(mask, 1.0, 0.0).astype(BF16)
    z = jnp.zeros_like(u)
    return jnp.concatenate([jnp.concatenate([u, z], axis=1), jnp.concatenate([z, u], axis=1)], axis=0)


def _pair_rows(xp, lo_half):
    z = jnp.zeros_like(xp)
    return jnp.concatenate([jnp.where(lo_half, xp, z), jnp.where(lo_half, z, xp)], axis=0)


def _sb_scores(z, causal, diag):
    ls, ps, es = [], [], []
    for hh in range(2):
        zz = z[:, hh * QB:(hh + 1) * QB]
        e = jnp.exp(-jnp.abs(zz))
        l = -(jnp.maximum(zz, 0.0) + jnp.log(1.0 + e))
        ps.append(l + zz)
        ls.append(jnp.where(causal, l, 0.0) if diag else l)
        es.append(e)
    return ls, ps, es


def _sb_fwd(proj, carried=()):
    S = proj.shape[0]
    nq = S // QB

    def body(q_ref, k_ref, v_ref, o_ref, t_ref, c_ref, acc_ref, qs_ref):
        i = pl.program_id(0)
        causal = _tri(QB, "gt")
        usuf = _pair_diag(_tri(QB, "gt"))
        lo_half = lax.broadcasted_iota(jnp.int32, (QB, 128), 1) < DK
        qs_ref[...] = q_ref[...] * 0.125

        def block(j, diag, nr):
            rows = pl.ds(pl.multiple_of(j * QB, QB), QB)
            rs = slice(0, nr)
            pairs = range(H_SB // 2)
            cols = [slice(pr * 128, (pr + 1) * 128) for pr in pairs]
            zs = [_dot_nt(qs_ref[rs, cols[pr]], _pair_rows(k_ref[rows, cols[pr]], lo_half)) for pr in pairs]
            sc = [_sb_scores(zs[pr], causal, diag) for pr in pairs]
            sufs = []
            for pr in pairs:
                lh, ll = _split(jnp.concatenate(sc[pr][0], axis=1))
                sufs.append(_dot(lh, usuf) + _dot(ll, usuf))
            cmax = None
            wps = []
            for pr in pairs:
                ws = []
                for hh in range(2):
                    h = 2 * pr + hh
                    b = sufs[pr][:, hh * QB:(hh + 1) * QB]
                    if not diag:
                        b = b + c_ref[h, rs, 0:1]
                    w = jnp.exp(sc[pr][1][hh] + b)
                    ws.append((jnp.where(causal, w, 0.0) if diag else w).astype(BF16))
                    cn = b[:, 0:1] + sc[pr][0][hh][:, 0:1]
                    c_ref[h, rs, 0:1] = cn
                    cmax = cn if cmax is None else jnp.maximum(cmax, cn)
                wps.append(jnp.concatenate(ws, axis=1))
            for pr in pairs:
                upd = _dot(wps[pr], _pair_rows(v_ref[rows, cols[pr]], lo_half))
                if diag:
                    acc_ref[rs, cols[pr]] = upd
                else:
                    acc_ref[rs, cols[pr]] += upd
            lo = jnp.max(cmax[:SB_HEAD_ROWS])
            return (jnp.max(cmax[SB_HEAD_ROWS:]) if nr > SB_HEAD_ROWS else None), lo

        def cond_full(st):
            return jnp.logical_and(st[0] >= 0, st[1] > SB_SKIP)

        def step_full(st):
            return (st[0] - 1,) + block(st[0], False, QB)

        def cond_head(st):
            return jnp.logical_and(st[0] >= 0, st[1] > SB_SKIP)

        def step_head(st):
            return st[0] - 1, block(st[0], False, SB_HEAD_ROWS)[1]

        j, _, lo = lax.while_loop(cond_full, step_full, (i - 1,) + block(i, True, QB))
        jfull = j + 1
        j, _ = lax.while_loop(cond_head, step_head, (j, lo))
        o_ref[...] = acc_ref[...].astype(BF16)
        t_ref[...] = jnp.zeros_like(t_ref)
        for h in range(H_SB):
            t_ref[h // 4, :, h % 4:h % 4 + 1] = c_ref[h, :, 0:1]
        t_ref[:, :, 8:9] = jnp.zeros((2, QB, 1), F32) + (j + 1).astype(F32)
        t_ref[:, :, 9:10] = jnp.zeros((2, QB, 1), F32) + jfull.astype(F32)

    return _call(
        body, carried, [proj, proj, proj], name="sb_fwd", grid=(nq,),
        in_specs=[pl.BlockSpec((QB, 512), lambda i: (i, 0)),
                  pl.BlockSpec((S, 512), lambda i: (0, 1), pipeline_mode=ONE_BUF),
                  pl.BlockSpec((S, 512), lambda i: (0, 2), pipeline_mode=ONE_BUF)],
        out_specs=[pl.BlockSpec((QB, 512), lambda i: (i, 0)),
                   pl.BlockSpec((2, QB, 128), lambda i: (0, i, 0))],
        out_shape=[jax.ShapeDtypeStruct((S, 512), BF16), jax.ShapeDtypeStruct((2, S, 128), F32)],
        scratch_shapes=[pltpu.VMEM((H_SB, QB, 128), F32), pltpu.VMEM((QB, 512), F32), pltpu.VMEM((QB, 512), BF16)],
        compiler_params=_cp(1, 40))


def _sb_bwd(proj, dcat, stats, carried=()):
    S = proj.shape[0]
    nq = S // QB

    def body(q_ref, k_ref, v_ref, do_ref, t_ref, dq_ref, dk_ref, dv_ref, dk_acc, dv_acc, dq_acc, pc_ref, qs_ref,
             qt_ref, dot_ref):
        i = pl.program_id(1)

        @pl.when(i == 0)
        def _():
            dk_acc[...] = jnp.zeros_like(dk_acc)
            dv_acc[...] = jnp.zeros_like(dv_acc)

        causal = _tri(QB, "gt")
        uin = _pair_diag(_tri(QB, "le"))
        uex = _pair_diag(_tri(QB, "lt"))
        lo_half = lax.broadcasted_iota(jnp.int32, (QB, 128), 1) < DK
        qs_ref[...] = q_ref[...] * 0.125
        lo_rows = lax.broadcasted_iota(jnp.int32, (128, QB), 0) < DK
        for pr in range(2):
            qt_ref[pr] = (q_ref[:, pr * 128:(pr + 1) * 128].astype(F32) * 0.125).T.astype(BF16)
            dot_ref[pr] = do_ref[:, pr * 128:(pr + 1) * 128].astype(F32).T.astype(BF16)
        pc_ref[...] = jnp.zeros_like(pc_ref)
        dq_acc[...] = jnp.zeros_like(dq_acc)
        jstart = jnp.max(t_ref[:, 8:9]).astype(jnp.int32)
        jfull = jnp.max(t_ref[:, 9:10]).astype(jnp.int32)

        def block(j, diag, nr):
            rows = pl.ds(pl.multiple_of(j * QB, QB), QB)
            rs = slice(0, nr)
            pairs = range(2)
            cols = [slice(pr * 128, (pr + 1) * 128) for pr in pairs]
            kbds = [_pair_rows(k_ref[rows, cols[pr]], lo_half) for pr in pairs]
            zs = [_dot_nt(qs_ref[rs, cols[pr]], kbds[pr]) for pr in pairs]
            dws = [_dot_nt(do_ref[rs, cols[pr]], _pair_rows(v_ref[rows, cols[pr]], lo_half)) for pr in pairs]
            sc = [_sb_scores(zs[pr], causal, diag) for pr in pairs]
            plins = []
            for pr in pairs:
                lh, ll = _split(jnp.concatenate(sc[pr][0], axis=1))
                plins.append(_dot(lh, uin) + _dot(ll, uin))
            wss, gss, gexs = [], [], []
            for pr in pairs:
                ws, gs = [], []
                for hh in range(2):
                    h = 2 * pr + hh
                    half = slice(hh * QB, (hh + 1) * QB)
                    b = (t_ref[rs, h:h + 1] - pc_ref[h, rs, 0:1]) - plins[pr][:, half]
                    w = jnp.exp(sc[pr][1][hh] + b)
                    if diag:
                        w = jnp.where(causal, w, 0.0)
                    ws.append(w)
                    gs.append(dws[pr][:, half] * w)
                wss.append(ws)
                gss.append(gs)
            for pr in pairs:
                gh, gl = _split(jnp.concatenate(gss[pr], axis=1))
                gexs.append(_dot(gh, uex) + _dot(gl, uex))
            dzbs = []
            for pr in pairs:
                dzs = []
                for hh in range(2):
                    h = 2 * pr + hh
                    half = slice(hh * QB, (hh + 1) * QB)
                    e = sc[pr][2][hh]
                    r = pl.reciprocal(1.0 + e, approx=True)
                    er = e * r
                    pos = zs[pr][:, half] >= 0.0
                    gx = gexs[pr][:, half]
                    g = gss[pr][hh]
                    dz = g * jnp.where(pos, er, r) - (gx + pc_ref[4 + h, rs, 0:1]) * jnp.where(pos, r, er)
                    dzs.append(jnp.where(causal, dz, 0.0) if diag else dz)
                    pc_ref[h, rs, 0:1] += plins[pr][:, half][:, QB - 1:QB]
                    pc_ref[4 + h, rs, 0:1] += gx[:, QB - 1:QB] + g[:, QB - 1:QB]
                dzbs.append(jnp.concatenate(dzs, axis=1).astype(BF16))
            for pr in pairs:
                dq_acc[rs, cols[pr]] += _dot(dzbs[pr], kbds[pr])
                r1 = _dot(qt_ref[pr, :, rs], dzbs[pr])
                dk_acc[pr, j] += jnp.where(lo_rows, r1[:, :QB], r1[:, QB:])
                r2 = _dot(dot_ref[pr, :, rs], jnp.concatenate(wss[pr], axis=1).astype(BF16))
                dv_acc[pr, j] += jnp.where(lo_rows, r2[:, :QB], r2[:, QB:])

        def step_head(j, carry):
            block(j, False, SB_HEAD_ROWS)
            return carry

        def step_full(j, carry):
            block(j, False, QB)
            return carry

        lax.fori_loop(jstart, jfull, step_head, 0)
        lax.fori_loop(jfull, i, step_full, 0)
        block(i, True, QB)
        dq_ref[...] = (dq_acc[...] * 0.125).astype(BF16)

        @pl.when(i == nq - 1)
        def _():
            def put(jj, carry):
                krows = pl.ds(pl.multiple_of(jj * QB, QB), QB)
                for pr in range(2):
                    dk_ref[krows, pr * 128:(pr + 1) * 128] = dk_acc[pr, jj].T.astype(BF16)
                    dv_ref[krows, pr * 128:(pr + 1) * 128] = dv_acc[pr, jj].T.astype(BF16)
                return carry
            lax.fori_loop(0, nq, put, 0)

    return _call(
        body, carried, [proj, proj, proj, dcat, stats], name="sb_bwd", grid=(2, nq),
        in_specs=[pl.BlockSpec((QB, 256), lambda g, i: (i, g)),
                  pl.BlockSpec((S, 256), lambda g, i: (0, 2 + g), pipeline_mode=ONE_BUF),
                  pl.BlockSpec((S, 256), lambda g, i: (0, 4 + g), pipeline_mode=ONE_BUF),
                  pl.BlockSpec((QB, 256), lambda g, i: (i, g)),
                  pl.BlockSpec((None, QB, 128), lambda g, i: (g, i, 0))],
        out_specs=[pl.BlockSpec((QB, 256), lambda g, i: (i, g)),
                   pl.BlockSpec((S, 256), lambda g, i: (0, g)),
                   pl.BlockSpec((S, 256), lambda g, i: (0, g))],
        out_shape=[jax.ShapeDtypeStruct((S, 512), BF16)] * 3,
        scratch_shapes=[pltpu.VMEM((2, nq, 128, QB), F32), pltpu.VMEM((2, nq, 128, QB), F32),
                        pltpu.VMEM((QB, 256), F32), pltpu.VMEM((8, QB, 128), F32), pltpu.VMEM((QB, 256), BF16),
                        pltpu.VMEM((2, 128, QB), BF16), pltpu.VMEM((2, 128, QB), BF16)],
        compiler_params=_cp(2, 56))


GLA_NC = 4
GLA_R = GLA_NC * CHUNK


def _chunk_tri(strict):
    row = lax.broadcasted_iota(jnp.int32, (GLA_R, GLA_R), 0)
    col = lax.broadcasted_iota(jnp.int32, (GLA_R, GLA_R), 1)
    m = jnp.logical_and(row // CHUNK == col // CHUNK, row > col if strict else row >= col)
    u = jnp.where(m, 1.0, 0.0).astype(BF16)
    return jnp.concatenate([u, u], axis=1)


def _per_chunk_rows(vals):
    return jnp.concatenate([jnp.broadcast_to(v, (CHUNK, v.shape[1])) for v in vals], axis=0)


def _head_blocks(st):
    row = lax.broadcasted_iota(jnp.int32, (H_GLA * DV, H_GLA * DK), 0)
    col = lax.broadcasted_iota(jnp.int32, (H_GLA * DV, H_GLA * DK), 1)
    t = jnp.concatenate([st.astype(BF16)] * H_GLA, axis=0)
    return jnp.where(row // DV == col // DK, t, jnp.zeros_like(t))


def _head_diag(big):
    head = lax.broadcasted_iota(jnp.int32, (DV, H_GLA * DK), 1) // DK
    out = big[0:DV]
    for h in range(1, H_GLA):
        out = jnp.where(head == h, big[h * DV:(h + 1) * DV], out)
    return out


def _gla_gate4(gf_ref, wfg_ref, bfg_ref):
    f = _dot(gf_ref[...], wfg_ref[...]) + bfg_ref[...]
    _, la, _ = _log_sigmoid_parts(f)
    lah, lal = _split(la * (1.0 / 16.0))
    cum = _dot(_chunk_tri(False), jnp.concatenate([lah, lal], axis=0))
    tots = [cum[(c + 1) * CHUNK - 1:(c + 1) * CHUNK, :] for c in range(GLA_NC)]
    return f, jnp.exp(_per_chunk_rows(tots) - cum), [jnp.exp(t) for t in tots]


def _gla_specs4(ns, rev):
    def ix(i):
        return ns - 1 - i if rev else i
    return [pl.BlockSpec((GLA_R, 256), lambda i: (ix(i), 6)),
            pl.BlockSpec((GLA_R, 256), lambda i: (ix(i), 7)),
            pl.BlockSpec((GLA_R, 512), lambda i: (ix(i), 4)),
            pl.BlockSpec((GLA_R, 512), lambda i: (ix(i), 5)),
            pl.BlockSpec((GLA_R, 128), lambda i: (ix(i), 24))]


def _gla_fwd(proj, wfg_p, bfg, ggla, carried=()):
    S = proj.shape[0]
    ns = S // GLA_R

    def body(q_ref, k_ref, v_ref, gg_ref, gf_ref, wfg_ref, bfg_ref, ggla_ref, o_ref, st_ref, state):
        @pl.when(pl.program_id(0) == 0)
        def _():
            state[...] = jnp.zeros_like(state)

        _, e, decs = _gla_gate4(gf_ref, wfg_ref, bfg_ref)
        kdec = (k_ref[...].astype(F32) * e).astype(BF16)
        rows = [slice(c * CHUNK, (c + 1) * CHUNK) for c in range(GLA_NC)]
        kvs = [_head_diag(_dot_tn(v_ref[rows[c], :], kdec[rows[c]])) for c in range(GLA_NC)]
        st = state[...]
        sts = []
        for c in range(GLA_NC):
            st = st * decs[c] + kvs[c]
            st_ref[c] = st
            sts.append(st)
        state[...] = st
        o = jnp.concatenate([_dot_nt(q_ref[rows[c], :] * 0.125, _head_blocks(sts[c])) for c in range(GLA_NC)], axis=0)
        for h in range(H_GLA):
            vs = slice(h * DV, (h + 1) * DV)
            oh = o[:, vs]
            ohn = oh * lax.rsqrt(jnp.mean(oh * oh, axis=-1, keepdims=True) + EPS)
            gg = gg_ref[:, vs].astype(F32)
            o_ref[:, vs] = ((ohn * ggla_ref[:, vs]) * (gg * _sigmoid(gg))).astype(BF16)

    return _call(
        body, carried, [proj, proj, proj, proj, proj, wfg_p, bfg, ggla], name="gla_fwd", grid=(ns,),
        in_specs=_gla_specs4(ns, False) + [pl.BlockSpec((128, 256), lambda i: (0, 0)),
                                           pl.BlockSpec((1, 256), lambda i: (0, 0)),
                                           pl.BlockSpec((1, 512), lambda i: (0, 0))],
        out_specs=[pl.BlockSpec((GLA_R, 512), lambda i: (i, 0)),
                   pl.BlockSpec((GLA_NC, 128, 256), lambda i: (i, 0, 0))],
        out_shape=[jax.ShapeDtypeStruct((S, 512), BF16), jax.ShapeDtypeStruct((S // CHUNK, 128, 256), F32)],
        scratch_shapes=[pltpu.VMEM((128, 256), F32)],
        compiler_params=_cp(1))


def _gla_bwd(dcat, proj, states, wfg_p, bfg, ggla, carried=()):
    S = proj.shape[0]
    ns = S // GLA_R

    def body(do_ref, q_ref, k_ref, v_ref, gg_ref, gf_ref, sc_ref, sp_ref, wfg_ref, bfg_ref, ggla_ref,
             dp_ref, s_ref, dw_ref, carry):
        sr = pl.program_id(0)

        @pl.when(sr == 0)
        def _():
            carry[...] = jnp.zeros_like(carry)
            s_ref[...] = jnp.zeros_like(s_ref)
            dw_ref[...] = jnp.zeros_like(dw_ref)

        f, e, decs = _gla_gate4(gf_ref, wfg_ref, bfg_ref)
        kf = k_ref[...].astype(F32) * e
        kdec = kf.astype(BF16)
        rows = [slice(c * CHUNK, (c + 1) * CHUNK) for c in range(GLA_NC)]
        sts = [sc_ref[c] for c in range(GLA_NC)]
        st_before = jnp.where(sr < ns - 1, sp_ref[0], 0.0)
        sbd = [_head_blocks(sts[c]) for c in range(GLA_NC)]
        qs = q_ref[...] * 0.125
        o = jnp.concatenate([_dot_nt(qs[rows[c]], sbd[c]) for c in range(GLA_NC)], axis=0)
        dobs = []
        for h in range(H_GLA):
            vs = slice(h * DV, (h + 1) * DV)
            oh = o[:, vs]
            rr = lax.rsqrt(jnp.mean(oh * oh, axis=-1, keepdims=True) + EPS)
            ohn = oh * rr
            gg = gg_ref[:, vs].astype(F32)
            sg = _sigmoid(gg)
            dout = do_ref[:, vs].astype(F32)
            gl = ggla_ref[:, vs]
            dp_ref[:, 1024 + h * DV:1024 + (h + 1) * DV] = (
                dout * (ohn * gl) * (sg * (1.0 + gg * (1.0 - sg)))).astype(BF16)
            dt1 = dout * (gg * sg)
            s_ref[0:1, vs] += jnp.sum(dt1 * ohn, axis=0, keepdims=True)
            dohn = dt1 * gl
            dobs.append((rr * (dohn - ohn * jnp.mean(dohn * ohn, axis=-1, keepdims=True))).astype(BF16))
        dob = jnp.concatenate(dobs, axis=1)
        dsout = []
        for c in range(GLA_NC):
            dp_ref[rows[c], 0:256] = (_dot(dob[rows[c]], sbd[c]) * 0.125).astype(BF16)
            dsout.append(_head_diag(_dot_tn(dob[rows[c]], qs[rows[c]])))
        g = carry[...]
        gts, ddecs = [None] * GLA_NC, [None] * GLA_NC
        for c in reversed(range(GLA_NC)):
            gts[c] = dsout[c] + g
            ddecs[c] = jnp.sum(gts[c] * (sts[c - 1] if c > 0 else st_before), axis=0, keepdims=True) * decs[c]
            g = gts[c] * decs[c]
        carry[...] = g
        dkds = []
        for c in range(GLA_NC):
            gbd = _head_blocks(gts[c])
            dkds.append(_dot(v_ref[rows[c], :], gbd))
            dp_ref[rows[c], 512:1024] = _dot_nt(kdec[rows[c]], gbd).astype(BF16)
        dkd = jnp.concatenate(dkds, axis=0)
        dp_ref[:, 256:512] = (dkd * e).astype(BF16)
        wh, wl = _split(dkd * kf)
        dla = _dot(_chunk_tri(True), jnp.concatenate([wh, wl], axis=0)) + _per_chunk_rows(ddecs)
        df = dla * _sigmoid(-f) * (1.0 / 16.0)
        dfb = df.astype(BF16)
        s_ref[1:2, 0:256] += jnp.sum(df, axis=0, keepdims=True)
        dw_ref[...] += _dot_tn(gf_ref[...], dfb)
        dp_ref[:, 1536:1664] = _dot_nt(dfb, wfg_ref[...]).astype(BF16)

    return _call(
        body, carried, [dcat, proj, proj, proj, proj, proj, states, states, wfg_p, bfg, ggla],
        name="gla_bwd", grid=(ns,),
        in_specs=[pl.BlockSpec((GLA_R, 512), lambda i: (ns - 1 - i, 1))] + _gla_specs4(ns, True) + [
            pl.BlockSpec((GLA_NC, 128, 256), lambda i: (ns - 1 - i, 0, 0)),
            pl.BlockSpec((1, 128, 256), lambda i: (jnp.maximum((ns - 1 - i) * GLA_NC - 1, 0), 0, 0)),
            pl.BlockSpec((128, 256), lambda i: (0, 0)),
            pl.BlockSpec((1, 256), lambda i: (0, 0)),
            pl.BlockSpec((1, 512), lambda i: (0, 0))],
        out_specs=[pl.BlockSpec((GLA_R, 1664), lambda i: (ns - 1 - i, 0)),
                   pl.BlockSpec((8, 512), lambda i: (0, 0)),
                   pl.BlockSpec((128, 256), lambda i: (0, 0))],
        out_shape=[jax.ShapeDtypeStruct((S, 1664), BF16), jax.ShapeDtypeStruct((8, 512), F32),
                   jax.ShapeDtypeStruct((128, 256), F32)],
        scratch_shapes=[pltpu.VMEM((128, 256), F32)],
        compiler_params=_cp(1))


def _sum_leading(a, name):
    n = a.shape[0]

    def body(a_ref, o_ref):
        acc = a_ref[0]
        for k in range(1, n):
            acc = acc + a_ref[k]
        o_ref[...] = acc

    return pl.pallas_call(
        body, name=name, out_shape=jax.ShapeDtypeStruct(a.shape[1:], F32),
        in_specs=[VMEM_SPEC], out_specs=VMEM_SPEC,
    )(a)


def _sum_chip(own, recv, name):
    R, C = own.shape
    tr, tc = _tile2d(R, C, 1024 * 1024)

    def body(o_ref, r_ref, p_ref):
        acc = o_ref[...]
        for k in range(3):
            acc = acc + r_ref[k].astype(F32)
        p_ref[...] = acc

    return pl.pallas_call(
        body, name=name, grid=(R // tr, C // tc),
        in_specs=[pl.BlockSpec((tr, tc), lambda i, j: (i, j)), pl.BlockSpec((3, tr, tc), lambda i, j: (0, i, j))],
        out_specs=pl.BlockSpec((tr, tc), lambda i, j: (i, j)),
        out_shape=jax.ShapeDtypeStruct((R, C), F32), compiler_params=_cp(2, 40),
    )(own, recv)


def _adamw(w, p, q, m, v, name):
    R, C = w.shape
    tr, tc = _tile2d(R, C, 1024 * 1024)
    two = q is not None

    def body(*refs):
        if two:
            w_ref, p_ref, q_ref, m_ref, v_ref, g_out, d_out, m_out, v_out = refs
            g = p_ref[...] + q_ref[...]
        else:
            w_ref, p_ref, m_ref, v_ref, g_out, d_out, m_out, v_out = refs
            g = p_ref[...]
        m2 = B1 * m_ref[...] + (1.0 - B1) * g
        v2 = B2 * v_ref[...] + (1.0 - B2) * (g * g)
        m_hat = m2 / (1.0 - B1 ** STEP)
        v_hat = v2 / (1.0 - B2 ** STEP)
        g_out[...] = g
        d_out[...] = -LR * (m_hat / (jnp.sqrt(v_hat) + EPS_A) + WD * w_ref[...])
        m_out[...] = m2
        v_out[...] = v2

    spec = pl.BlockSpec((tr, tc), lambda i, j: (i, j))
    ins = [w, p, q, m, v] if two else [w, p, m, v]
    return pl.pallas_call(
        body, name=name, grid=(R // tr, C // tc),
        in_specs=[spec] * len(ins), out_specs=[spec] * 4,
        out_shape=[jax.ShapeDtypeStruct((R, C), F32)] * 4, compiler_params=_cp(2, 40),
    )(*ins)


def _reduce_big(own, recv, name):
    p = _sum_chip(own, recv, name + "_sum")
    return p, _pair_swap(p, name + "_swap")


def _cols_to_chips(a, width):
    return a.reshape(a.shape[0], 4, width).swapaxes(0, 1)


def _chips_to_cols(a):
    return a.swapaxes(0, 1).reshape(a.shape[1], 4 * a.shape[2])


def _swap_mid(a):
    lead = a.shape[:-1]
    return a.reshape(lead + (2, 2, HB)).swapaxes(-3, -2).reshape(lead + (4 * HB,))


def kernel(x, c, w_ada, b_ada, g_norm1, w_in, w_fg2, b_fg2, g_gla_out, w_out, g_norm2, w_up, w_conv, b_conv, w_down, g_final, loss_target, m_w_ada, m_b_ada, m_g_norm1, m_w_in, m_w_fg2, m_b_fg2, m_g_gla_out, m_w_out, m_g_norm2, m_w_up, m_w_conv, m_b_conv, m_w_down, m_g_final, v_w_ada, v_b_ada, v_g_norm1, v_w_in, v_w_fg2, v_b_fg2, v_g_gla_out, v_w_out, v_g_norm2, v_w_up, v_w_conv, v_b_conv, v_w_down, v_g_final):
    xi, yi, ci = lax.axis_index("x"), lax.axis_index("y"), lax.axis_index("c")
    cidx = 2 * xi + yi
    didx = 4 * xi + 2 * yi + ci
    xs = x[0]
    tgt = loss_target[0]
    gfin = g_final.reshape(1, D)
    AW = D * 6 // 4

    c_all = _allgather8(c, "gather_c").reshape(8, D)
    c_pad = jnp.concatenate([c_all, jnp.zeros((8, D), F32)], axis=0)
    mod_part = _ada_fwd(c_pad, w_ada[0], lax.dynamic_slice(b_ada, (0, cidx * AW), (1, AW)))[:8]
    small = jnp.concatenate([mod_part.reshape(-1), w_conv.reshape(-1), w_fg2.reshape(-1)]).reshape(-1, 128)
    small_g = _allgather4(small, "gather_small").reshape(4, -1)
    mod = lax.dynamic_index_in_dim(small_g[:, :8 * AW].reshape(4, 8, AW), didx, axis=1, keepdims=False).reshape(1, 6 * D)
    shift1, scale1, gate1, shift2, scale2, gate2 = [mod[:, k * D:(k + 1) * D] for k in range(6)]
    o1 = 8 * AW
    o2 = o1 + 3 * HB
    wc_p = _swap_mid(_chips_to_cols(small_g[:, o1:o2].reshape(4, 3, HB)))
    bc_p = _swap_mid(b_conv)
    wfg_full = _chips_to_cols(small_g[:, o2:].reshape(4, RANK, 64))
    wfg_p = jnp.concatenate([wfg_full, jnp.zeros((128 - RANK, 256), F32)], axis=0).astype(BF16)

    w_in_t = _allgather4_split(w_in[0].T.astype(BF16), "gather_w_in").reshape(N_IN, D)
    w_in_t = jnp.concatenate([w_in_t, jnp.zeros((N_IN_P - N_IN, D), BF16)], axis=0)
    w_in_p = w_in_t.T

    h = _norm_mod(xs, g_norm1, shift1, scale1)
    proj, (w_down_g,) = _mm([(h, w_in_p)], BF16, "mm_in", 256, N_IN_P, 48,
                            carried=[("gather", w_down[0].astype(BF16), False)])
    w_down_f = w_down_g.reshape(D_FF, D)
    (o_gla, states), (w_out_g,) = _gla_fwd(proj, wfg_p, b_fg2, g_gla_out,
                                           carried=[("gather", w_out[0].astype(BF16), False)])
    w_out_f = w_out_g.reshape(D, D)
    (o_sb, stats), (w_up_g,) = _sb_fwd(proj, carried=[("gather", w_up[0].astype(BF16), False)])
    w_up_p = _swap_mid(_chips_to_cols(w_up_g))
    x1, h2, mixed = _resid_norm_mod(xs, [(o_sb, w_out_f[:512]), (o_gla, w_out_f[512:])],
                                    gate1, g_norm2, shift2, scale2)
    u0p, a = _up_conv_glu(h2, w_up_p, wc_p, bc_p)
    dx2, dy2, s_fin = _final_loss(x1, [(a, w_down_f)], gate2, gfin, tgt)
    loss = lax.psum(0.5 / D * jnp.sum(s_fin[2]), ("x", "y", "c"))

    da = _mm([(dy2, w_down_f.T)], BF16, "mm_down_t", 512, D_FF, 48)
    dw_down, dw_down_h = [t.reshape(4, D_FF // 4, D) for t in _mm_tn([a], dy2, "mm_dw_down", D, 512, 56)]
    (du0p, s_conv, dx1, dmixed, s_n2), (rc_down,) = _conv_glu_up_norm2_bwd(
        da, u0p, wc_p, bc_p, w_up_p.T, x1, dx2, g_norm2, scale2, mixed, gate1,
        carried=[("scatter", dw_down_h, False)])
    dw_up, dw_up_h = _mm_tn([h2], du0p, "mm_dw_up", HB, 512, 48)
    dcat = _mm([(dmixed, w_out_f.T)], BF16, "mm_out_t", 512, D)
    dw_out, dw_out_h = [t.reshape(4, D // 4, D) for t in _mm_tn([o_sb, o_gla], dmixed, "mm_dw_out", D, 512)]
    (dq, dk, dv), (rc_up,) = _sb_bwd(proj, dcat, stats, carried=[("scatter", dw_up_h, True)])
    (dp_gla, s_gla, dwfg), (rc_out,) = _gla_bwd(dcat, proj, states, wfg_p, b_fg2, g_gla_out,
                                                carried=[("scatter", dw_out_h, False)])
    dw_in, dw_in_h = _mm_tn([dq, dk, dv, dp_gla], h, "mm_dw_in", D, 512, 56)
    dw_in_h = dw_in_h[0, :N_IN].reshape(4, N_IN // 4, D)
    dw_in_own = lax.dynamic_slice(dw_in[0], (cidx * (N_IN // 4), 0), (N_IN // 4, D))
    dh, (rc_in,) = _mm(
        [(dq, w_in_t[:512]), (dk, w_in_t[512:1024]), (dv, w_in_t[1024:1536]), (dp_gla, w_in_t[1536:])],
        F32, "mm_in_t", 256, D, 48, carried=[("scatter", dw_in_h, False)])
    (gx, s_n1), _ = _norm_mod_bwd(dh, xs, dx1, g_norm1, scale1, None, None, "norm1_bwd")

    dmod = jnp.concatenate([s_n1[0], s_n1[1], s_n2[3], s_n2[0], s_n2[1], s_fin[1]])
    s_conv_n = _swap_mid(s_conv[:4])
    part = jnp.concatenate([dmod, s_n1[2], s_n2[2], s_fin[0], s_gla[0], s_gla[1, :256], s_conv_n[0],
                            s_conv_n[1:4].reshape(-1), dwfg[:RANK].reshape(-1)]).reshape(-1, 128)
    parts = _allgather8(part, "gather_small_grads")
    tot = _sum_leading(parts, "sum_small_grads").reshape(-1)
    dmod_all = parts.reshape(8, -1)[:, :6 * D]
    offs = [0]
    for n in (6 * D, D, D, D, 512, 256, 2 * D_FF, 3 * 2 * D_FF, RANK * 256):
        offs.append(offs[-1] + n)
    g_b_ada, g_g1, g_g2, g_gf, g_ggla, g_bfg, g_bconv, g_wconv_full, g_wfg_full = [
        tot[offs[k]:offs[k + 1]] for k in range(9)]
    g_wconv = lax.dynamic_index_in_dim(_cols_to_chips(g_wconv_full.reshape(3, 2 * D_FF), HB), cidx, 0, keepdims=False)
    g_wfg = lax.dynamic_index_in_dim(_cols_to_chips(g_wfg_full.reshape(RANK, 256), 64), cidx, 0, keepdims=False)

    dmod_pad = jnp.concatenate([dmod_all, jnp.zeros((8, 6 * D), F32)], axis=0)
    g_w_ada = _ada_bwd(c_pad, lax.dynamic_slice(dmod_pad, (0, cidx * AW), (16, AW)))

    def own(blocks, swapped=False):
        return lax.dynamic_index_in_dim(blocks, _slot(cidx, swapped), axis=0, keepdims=False)

    p_in, q_in = _reduce_big(dw_in_own, rc_in, "rs_w_in")
    p_out, q_out = _reduce_big(own(dw_out), rc_out, "rs_w_out")
    p_up, q_up = _reduce_big(own(dw_up, True), rc_up, "rs_w_up")
    p_down, q_down = _reduce_big(own(dw_down), rc_down, "rs_w_down")

    out = {}
    out["w_ada"] = _adamw(w_ada[0], g_w_ada, None, m_w_ada[0], v_w_ada[0], "adamw_w_ada")
    out["w_in"] = [t.T for t in _adamw(w_in[0].T, p_in, q_in, m_w_in[0].T, v_w_in[0].T, "adamw_w_in")]
    out["w_out"] = _adamw(w_out[0], p_out, q_out, m_w_out[0], v_w_out[0], "adamw_w_out")
    out["w_up"] = _adamw(w_up[0], p_up, q_up, m_w_up[0], v_w_up[0], "adamw_w_up")
    out["w_down"] = _adamw(w_down[0], p_down, q_down, m_w_down[0], v_w_down[0], "adamw_w_down")
    small_names = ["b_ada", "g_norm1", "w_fg2", "b_fg2", "g_gla_out", "g_norm2", "w_conv", "b_conv", "g_final"]
    small_w = [b_ada, g_norm1, w_fg2, b_fg2, g_gla_out, g_norm2, w_conv, b_conv, g_final]
    small_m = [m_b_ada, m_g_norm1, m_w_fg2, m_b_fg2, m_g_gla_out, m_g_norm2, m_w_conv, m_b_conv, m_g_final]
    small_v = [v_b_ada, v_g_norm1, v_w_fg2, v_b_fg2, v_g_gla_out, v_g_norm2, v_w_conv, v_b_conv, v_g_final]
    small_gr = [g_b_ada, g_g1, g_wfg, g_bfg, g_ggla, g_g2, g_wconv, g_bconv, g_gf]

    def pack(arrs):
        flat = jnp.concatenate([t.reshape(-1) for t in arrs])
        return jnp.concatenate([flat, jnp.zeros((-flat.shape[0]) % 1024, F32)]).reshape(-1, 128)

    packed = _adamw(pack(small_w), pack(small_gr), None, pack(small_m), pack(small_v), "adamw_small")
    off = 0
    for nm, wt in zip(small_names, small_w):
        n = wt.size
        out[nm] = [t.reshape(-1)[off:off + n].reshape(wt.shape) for t in packed]
        off += n
    for nm in ("w_ada", "w_in", "w_out", "w_up", "w_down"):
        out[nm] = [t[None] for t in out[nm]]

    names = ["w_ada", "b_ada", "g_norm1", "w_in", "w_fg2", "b_fg2", "g_gla_out", "w_out", "g_norm2", "w_up",
             "w_conv", "b_conv", "w_down", "g_final"]
    res = [loss, gx[None]]
    for k in range(4):
        res += [out[nm][k] for nm in names]
    return tuple(res)
```

```python
import functools

import jax
import jax.numpy as jnp
from jax import lax
from jax.experimental import pallas as pl
from jax.experimental.pallas import tpu as pltpu

F32 = jnp.float32
BF16 = jnp.bfloat16
MESH = pl.DeviceIdType.MESH

D = 1024
H_SB = 8
DK = 64
DV = 128
H_GLA = 4
CHUNK = 64
RANK = 16
N_IN = 3088
N_IN_P = 3200
D_FF = 2816
HB = D_FF // 2
LANES = 128
EPS = 1e-6
QB = 128
SB_SKIP = -120.0
SB_HEAD_ROWS = 64

LR, B1, B2, EPS_A, WD, STEP = 0.001, 0.9, 0.999, 1e-08, 0.01, 10

ANY = pl.BlockSpec(memory_space=pl.ANY)
VMEM_SPEC = pl.BlockSpec(memory_space=pltpu.VMEM)
ONE_BUF = pl.Buffered(1)


def _cp(ndim=0, vmem_mb=None):
    kw = {}
    if ndim:
        kw["dimension_semantics"] = ("arbitrary",) * ndim
    if vmem_mb:
        kw["vmem_limit_bytes"] = vmem_mb * 1024 * 1024
    return pltpu.CompilerParams(**kw)


def _dot(a, b):
    return jnp.dot(a, b, preferred_element_type=F32)


def _dot_nt(a, b):
    return lax.dot_general(a, b, (((1,), (1,)), ((), ())), preferred_element_type=F32)


def _dot_tn(a, b):
    return lax.dot_general(a, b, (((0,), (0,)), ((), ())), preferred_element_type=F32)


def _split(x):
    hi = x.astype(BF16)
    lo = (x - hi.astype(F32)).astype(BF16)
    return hi, lo


def _sigmoid(x):
    return jax.nn.sigmoid(x)


def _sigmoid_fast(x):
    return pl.reciprocal(1.0 + jnp.exp(-x), approx=True)


def _log_sigmoid_parts(z):
    e = jnp.exp(-jnp.abs(z))
    sp = jnp.log1p(e)
    return -(jnp.maximum(z, 0.0) + sp), jnp.minimum(z, 0.0) - sp, e


def _tile2d(rows, cols, budget=512 * 1024):
    best = None
    for t in range(8, rows + 1, 8):
        if rows % t == 0 and t * cols * 4 <= budget:
            best = t
    if best is not None:
        return best, cols
    best = LANES if cols % LANES == 0 else cols
    for t in range(LANES, cols + 1, LANES):
        if cols % t == 0 and rows * t * 4 <= budget:
            best = t
    return rows, best


def _flip(v, bit):
    return 1 - v if bit else v


def _allgather8(a, name):
    def body(a_ref, o_ref, ssem, rsem, lsem):
        x, y, c = lax.axis_index("x"), lax.axis_index("y"), lax.axis_index("c")
        me = 4 * x + 2 * y + c
        loc = pltpu.make_async_copy(a_ref, o_ref.at[me], lsem)
        loc.start()
        sends = []
        for r in range(1, 8):
            peer = (_flip(x, r & 4), _flip(y, r & 2), _flip(c, r & 1))
            cp = pltpu.make_async_remote_copy(
                src_ref=a_ref, dst_ref=o_ref.at[me], send_sem=ssem.at[r - 1], recv_sem=rsem.at[r - 1],
                device_id=peer, device_id_type=MESH)
            cp.start()
            sends.append(cp)
        for r in range(1, 8):
            peer = (_flip(x, r & 4), _flip(y, r & 2), _flip(c, r & 1))
            pidx = 4 * peer[0] + 2 * peer[1] + peer[2]
            pltpu.make_async_remote_copy(
                src_ref=a_ref, dst_ref=o_ref.at[pidx], send_sem=ssem.at[r - 1], recv_sem=rsem.at[r - 1],
                device_id=peer, device_id_type=MESH).wait_recv()
        for cp in sends:
            cp.wait_send()
        loc.wait()

    return pl.pallas_call(
        body, name=name,
        out_shape=jax.ShapeDtypeStruct((8,) + a.shape, a.dtype),
        in_specs=[VMEM_SPEC], out_specs=VMEM_SPEC,
        scratch_shapes=[pltpu.SemaphoreType.DMA((7,)), pltpu.SemaphoreType.DMA((7,)), pltpu.SemaphoreType.DMA],
    )(a)


def _allgather4(a, name):
    def body(a_ref, o_ref, ssem, rsem, lsem):
        x, y, c = lax.axis_index("x"), lax.axis_index("y"), lax.axis_index("c")
        me = 2 * x + y
        loc = pltpu.make_async_copy(a_ref, o_ref.at[me], lsem)
        loc.start()
        sends = []
        for r in range(1, 4):
            peer = (_flip(x, r & 2), _flip(y, r & 1), c)
            cp = pltpu.make_async_remote_copy(
                src_ref=a_ref, dst_ref=o_ref.at[me], send_sem=ssem.at[r - 1], recv_sem=rsem.at[r - 1],
                device_id=peer, device_id_type=MESH)
            cp.start()
            sends.append(cp)
        for r in range(1, 4):
            peer = (_flip(x, r & 2), _flip(y, r & 1), c)
            pidx = 2 * peer[0] + peer[1]
            pltpu.make_async_remote_copy(
                src_ref=a_ref, dst_ref=o_ref.at[pidx], send_sem=ssem.at[r - 1], recv_sem=rsem.at[r - 1],
                device_id=peer, device_id_type=MESH).wait_recv()
        for cp in sends:
            cp.wait_send()
        loc.wait()

    return pl.pallas_call(
        body, name=name,
        out_shape=jax.ShapeDtypeStruct((4,) + a.shape, a.dtype),
        in_specs=[ANY], out_specs=ANY,
        scratch_shapes=[pltpu.SemaphoreType.DMA((3,)), pltpu.SemaphoreType.DMA((3,)), pltpu.SemaphoreType.DMA],
    )(a)


def _allgather4_split(a, name):
    R, C = a.shape
    hc = C // 2

    def body(a_ref, o_ref, ssem, rsem, fssem, frsem, lsem):
        x, y, c = lax.axis_index("x"), lax.axis_index("y"), lax.axis_index("c")
        me = 2 * x + y
        sibling = (x, y, 1 - c)
        mine = pl.ds(pl.multiple_of(c * hc, hc), hc)
        theirs = pl.ds(pl.multiple_of((1 - c) * hc, hc), hc)
        loc = pltpu.make_async_copy(a_ref, o_ref.at[me], lsem)
        loc.start()
        peers = [(_flip(x, r & 2), _flip(y, r & 1), c) for r in range(1, 4)]
        pidx = [2 * p[0] + p[1] for p in peers]
        sends = []
        for k in range(3):
            cp = pltpu.make_async_remote_copy(
                src_ref=a_ref.at[:, mine], dst_ref=o_ref.at[me, :, mine], send_sem=ssem.at[k], recv_sem=rsem.at[k],
                device_id=peers[k], device_id_type=MESH)
            cp.start()
            sends.append(cp)
        for k in range(3):
            landed = o_ref.at[pidx[k], :, mine]
            pltpu.make_async_remote_copy(
                src_ref=landed, dst_ref=landed, send_sem=ssem.at[k], recv_sem=rsem.at[k],
                device_id=peers[k], device_id_type=MESH).wait_recv()
            cp = pltpu.make_async_remote_copy(
                src_ref=landed, dst_ref=landed, send_sem=fssem.at[k], recv_sem=frsem.at[k],
                device_id=sibling, device_id_type=MESH)
            cp.start()
            sends.append(cp)
        for k in range(3):
            got = o_ref.at[pidx[k], :, theirs]
            pltpu.make_async_remote_copy(
                src_ref=got, dst_ref=got, send_sem=fssem.at[k], recv_sem=frsem.at[k],
                device_id=sibling, device_id_type=MESH).wait_recv()
        for cp in sends:
            cp.wait_send()
        loc.wait()

    return pl.pallas_call(
        body, name=name,
        out_shape=jax.ShapeDtypeStruct((4,) + a.shape, a.dtype),
        in_specs=[ANY], out_specs=ANY,
        scratch_shapes=[pltpu.SemaphoreType.DMA((3,))] * 4 + [pltpu.SemaphoreType.DMA],
    )(a)


def _slot(chip, swapped):
    return 2 * (chip % 2) + chip // 2 if swapped else chip


def _pair_swap(ps, name):
    n = len(ps)

    def body(*refs):
        x, y, c = lax.axis_index("x"), lax.axis_index("y"), lax.axis_index("c")
        ssem, rsem = refs[2 * n], refs[2 * n + 1]
        cps = [pltpu.make_async_remote_copy(
            src_ref=refs[k], dst_ref=refs[n + k], send_sem=ssem.at[k], recv_sem=rsem.at[k],
            device_id=(x, y, 1 - c), device_id_type=MESH) for k in range(n)]
        for cp in cps:
            cp.start()
        for cp in cps:
            cp.wait()

    return pl.pallas_call(
        body, name=name,
        out_shape=[jax.ShapeDtypeStruct(p.shape, p.dtype) for p in ps],
        in_specs=[ANY] * n, out_specs=[ANY] * n,
        scratch_shapes=[pltpu.SemaphoreType.DMA((n,)), pltpu.SemaphoreType.DMA((n,))],
    )(*ps)


def _carried_copies(kind, src_ref, dst_ref, sems, swapped):
    ssem, rsem, lsem = sems
    x, y, c = lax.axis_index("x"), lax.axis_index("y"), lax.axis_index("c")
    me = 2 * x + y
    starts, recvs = [], []
    if kind == "gather":
        starts.append(pltpu.make_async_copy(src_ref, dst_ref.at[me], lsem))
    for r in range(1, 4):
        peer = (_flip(x, r & 2), _flip(y, r & 1), c)
        pidx = 2 * peer[0] + peer[1]
        if kind == "gather":
            src, dst, landed = src_ref, dst_ref.at[me], dst_ref.at[pidx]
        else:
            src = src_ref.at[2 * peer[1] + peer[0] if swapped else pidx]
            dst = landed = dst_ref.at[r - 1]
        starts.append(pltpu.make_async_remote_copy(
            src_ref=src, dst_ref=dst, send_sem=ssem.at[r - 1], recv_sem=rsem.at[r - 1],
            device_id=peer, device_id_type=MESH))
        recvs.append(pltpu.make_async_remote_copy(
            src_ref=src, dst_ref=landed, send_sem=ssem.at[r - 1], recv_sem=rsem.at[r - 1],
            device_id=peer, device_id_type=MESH))
    return starts, recvs


def _call(body, carried, operands, *, name, grid, in_specs, out_specs, out_shape, scratch_shapes=(),
          compiler_params=None):
    single = not isinstance(out_shape, (list, tuple))
    out_specs = [out_specs] if single else list(out_specs)
    out_shape = [out_shape] if single else list(out_shape)
    n_in, n_out, n_sc, nh = len(operands), len(out_shape), len(scratch_shapes), len(carried)

    def full(*refs):
        ins, h_in = refs[:n_in], refs[n_in:n_in + nh]
        o0 = n_in + nh
        outs, h_out = refs[o0:o0 + n_out], refs[o0 + n_out:o0 + n_out + nh]
        s0 = o0 + n_out + nh
        scratch, sems = refs[s0:s0 + n_sc], refs[s0 + n_sc:]
        first = last = None
        for d in range(len(grid)):
            f = pl.program_id(d) == 0
            l = pl.program_id(d) == pl.num_programs(d) - 1
            first = f if first is None else jnp.logical_and(first, f)
            last = l if last is None else jnp.logical_and(last, l)

        def copies(t):
            return _carried_copies(carried[t][0], h_in[t], h_out[t], sems[3 * t:3 * t + 3], carried[t][2])

        if nh:
            @pl.when(first)
            def _():
                for t in range(nh):
                    for cp in copies(t)[0]:
                        cp.start()

        body(*ins, *outs, *scratch)

        if nh:
            @pl.when(last)
            def _():
                for t in range(nh):
                    starts, recvs = copies(t)
                    for cp in recvs:
                        cp.wait_recv()
                    for cp in starts:
                        if carried[t][0] == "gather" and cp is starts[0]:
                            cp.wait()
                        else:
                            cp.wait_send()

    h_shapes = [jax.ShapeDtypeStruct(((4,) + arr.shape) if kind == "gather" else ((3,) + arr.shape[1:]), arr.dtype)
                for kind, arr, _ in carried]
    sem_shapes = [pltpu.SemaphoreType.DMA((3,)), pltpu.SemaphoreType.DMA((3,)), pltpu.SemaphoreType.DMA] * nh
    res = pl.pallas_call(
        full, name=name, grid=grid, in_specs=list(in_specs) + [ANY] * nh, out_specs=out_specs + [ANY] * nh,
        out_shape=out_shape + h_shapes, scratch_shapes=list(scratch_shapes) + sem_shapes,
        compiler_params=compiler_params,
    )(*operands, *[arr for _, arr, _ in carried])
    main = res[:n_out]
    return (main[0] if single else main), list(res[n_out:])


def _mm(pairs, out_dtype, name, tm, tn, vmem_mb=None, carried=()):
    S = pairs[0][0].shape[0]
    N = pairs[0][1].shape[1]
    tm = min(tm, S)
    np_ = len(pairs)

    def body(*refs):
        acc = _dot(refs[0][...], refs[1][...])
        for t in range(1, np_):
            acc = acc + _dot(refs[2 * t][...], refs[2 * t + 1][...])
        refs[-1][...] = acc.astype(refs[-1].dtype)

    in_specs, ops = [], []
    for a, w in pairs:
        in_specs += [pl.BlockSpec((tm, a.shape[1]), lambda n, i: (i, 0)),
                     pl.BlockSpec((w.shape[0], tn), lambda n, i: (0, n))]
        ops += [a, w]
    out, got = _call(
        body, carried, ops, name=name, grid=(N // tn, S // tm), in_specs=in_specs,
        out_specs=pl.BlockSpec((tm, tn), lambda n, i: (i, n)),
        out_shape=jax.ShapeDtypeStruct((S, N), out_dtype),
        compiler_params=_cp(2, vmem_mb))
    return (out, got) if carried else out


def _mm_tn(a_list, b, name, bn, tk, vmem_mb=None):
    S, N = b.shape
    ms = [a.shape[1] for a in a_list]
    M = sum(ms)
    tk = min(tk, S)
    na = len(a_list)

    def body(*refs):
        b_ref, o_ref, o16_ref = refs[na], refs[na + 1], refs[na + 2]

        @pl.when(pl.program_id(1) == 0)
        def _():
            o_ref[...] = jnp.zeros_like(o_ref)
        off = 0
        for t in range(na):
            o_ref[off:off + ms[t], :] += _dot_tn(refs[t][...], b_ref[...])
            off += ms[t]

        @pl.when(pl.program_id(1) == pl.num_programs(1) - 1)
        def _():
            o16_ref[...] = o_ref[...].astype(BF16)

    spec = pl.BlockSpec((None, M, bn), lambda n, k: (n, 0, 0), pipeline_mode=ONE_BUF)
    return pl.pallas_call(
        body, name=name, grid=(N // bn, S // tk),
        in_specs=[pl.BlockSpec((tk, m), lambda n, k: (k, 0)) for m in ms] + [pl.BlockSpec((tk, bn), lambda n, k: (k, n))],
        out_specs=[spec, spec],
        out_shape=[jax.ShapeDtypeStruct((N // bn, M, bn), F32), jax.ShapeDtypeStruct((N // bn, M, bn), BF16)],
        compiler_params=_cp(2, vmem_mb),
    )(*a_list, b)


def _ada_fwd(c_all, w_sh, b_sh):
    def body(c_ref, w_ref, b_ref, o_ref):
        cv = c_ref[...]
        sc = (cv * _sigmoid(cv)).astype(BF16)
        o_ref[...] = _dot(sc, w_ref[...].astype(BF16)) + b_ref[...]

    return pl.pallas_call(
        body, name="ada_fwd", out_shape=jax.ShapeDtypeStruct((c_all.shape[0], w_sh.shape[1]), F32),
        in_specs=[VMEM_SPEC] * 3, out_specs=VMEM_SPEC, compiler_params=_cp(0, 40),
    )(c_all, w_sh, b_sh)


def _ada_bwd(c_all, dmod_sh):
    def body(c_ref, d_ref, o_ref):
        cv = c_ref[...]
        sc = (cv * _sigmoid(cv)).astype(BF16)
        o_ref[...] = _dot_tn(sc, d_ref[...].astype(BF16))

    return pl.pallas_call(
        body, name="ada_bwd", out_shape=jax.ShapeDtypeStruct((c_all.shape[1], dmod_sh.shape[1]), F32),
        in_specs=[VMEM_SPEC] * 2, out_specs=VMEM_SPEC, compiler_params=_cp(0, 40),
    )(c_all, dmod_sh)


def _vec(tm_unused=None):
    return pl.BlockSpec((1, D), lambda i: (0, 0))


def _rows(tm, width=D):
    return pl.BlockSpec((tm, width), lambda i: (i, 0))


def _norm_mod(x, g, shift, scale, tm=512):
    S = x.shape[0]
    tm = min(tm, S)

    def body(x_ref, g_ref, sh_ref, sc_ref, h_ref):
        xv = x_ref[...]
        r = lax.rsqrt(jnp.mean(xv * xv, axis=-1, keepdims=True) + EPS)
        hn = (xv * r) * g_ref[...]
        h_ref[...] = (hn * (1.0 + sc_ref[...]) + sh_ref[...]).astype(BF16)

    return pl.pallas_call(
        body, name="norm1_mod", grid=(S // tm,),
        in_specs=[_rows(tm), _vec(), _vec(), _vec()], out_specs=_rows(tm),
        out_shape=jax.ShapeDtypeStruct((S, D), BF16), compiler_params=_cp(1),
    )(x, g, shift, scale)


def _mm_rows(pairs, tm):
    ops, specs = [], []
    for a, w in pairs:
        ops += [a, w]
        specs += [pl.BlockSpec((tm, a.shape[1]), lambda i: (i, 0)),
                  pl.BlockSpec(w.shape, lambda i: (0, 0), pipeline_mode=ONE_BUF)]
    return ops, specs


def _mm_rows_value(refs, npairs):
    acc = _dot(refs[0][...], refs[1][...])
    for t in range(1, npairs):
        acc = acc + _dot(refs[2 * t][...], refs[2 * t + 1][...])
    return acc


def _resid_norm_mod(x, mm, gate, g, shift, scale, tm=256):
    S = x.shape[0]
    tm = min(tm, S)
    skip = 2 * len(mm)

    def body(*refs):
        x_ref, gt_ref, g_ref, sh_ref, sc_ref, x1_ref, h_ref, m_ref = refs[skip:]
        mixed = _mm_rows_value(refs, len(mm))
        m_ref[...] = mixed
        x1 = x_ref[...] + (1.0 + gt_ref[...]) * mixed
        x1_ref[...] = x1
        r = lax.rsqrt(jnp.mean(x1 * x1, axis=-1, keepdims=True) + EPS)
        hn = (x1 * r) * g_ref[...]
        h_ref[...] = (hn * (1.0 + sc_ref[...]) + sh_ref[...]).astype(BF16)

    ops, specs = _mm_rows(mm, tm)
    return pl.pallas_call(
        body, name="mm_out_resid_norm2_mod", grid=(S // tm,),
        in_specs=specs + [_rows(tm), _vec(), _vec(), _vec(), _vec()],
        out_specs=[_rows(tm), _rows(tm), _rows(tm)],
        out_shape=[jax.ShapeDtypeStruct((S, D), F32), jax.ShapeDtypeStruct((S, D), BF16),
                   jax.ShapeDtypeStruct((S, D), F32)],
        compiler_params=_cp(1, 40),
    )(*ops, x, gate, g, shift, scale)


def _conv3(ext, w_ref, b_ref, cs):
    e1 = pltpu.roll(ext, 1, 0)
    e2 = pltpu.roll(ext, 2, 0)
    u = b_ref[:, cs] + w_ref[0:1, cs] * e2
    u = u + w_ref[1:2, cs] * e1
    u = u + w_ref[2:3, cs] * ext
    return u, e1, e2


def _up_conv_glu(h2, w_up_p, wc_p, bc_p, tm=256):
    S = h2.shape[0]
    tm = min(tm, S)
    widths = [2 * LANES] * (HB // (2 * LANES)) + ([LANES] if HB % (2 * LANES) else [])

    def body(h_ref, wu_ref, w_ref, b_ref, u_ref, a_ref, prev_ref):
        @pl.when(pl.program_id(0) == 0)
        def _():
            prev_ref[...] = jnp.zeros_like(prev_ref)

        hv = h_ref[...]
        for j in range(2):
            base = 0
            for wd in widths:
                us = []
                for off in (2 * j * HB + base, 2 * j * HB + HB + base):
                    cb = _dot(hv, wu_ref[:, off:off + wd]).astype(BF16)
                    u_ref[:, off:off + wd] = cb
                    for q in range(wd // LANES):
                        cs = slice(off + q * LANES, off + (q + 1) * LANES)
                        cq = cb[:, q * LANES:(q + 1) * LANES]
                        ext = jnp.concatenate([prev_ref[:, cs].astype(F32), cq.astype(F32)], axis=0)
                        us.append(_conv3(ext, w_ref, b_ref, cs)[0][16:])
                        prev_ref[:, cs] = cq[tm - 16:]
                nq = wd // LANES
                for q in range(nq):
                    val, gt = us[q], us[nq + q]
                    a_ref[:, j * HB + base + q * LANES:j * HB + base + (q + 1) * LANES] = (
                        val * (gt * _sigmoid_fast(gt))).astype(BF16)
                base += wd

    return pl.pallas_call(
        body, name="mm_up_conv_glu", grid=(S // tm,),
        in_specs=[pl.BlockSpec((tm, D), lambda i: (i, 0)),
                  pl.BlockSpec((D, 2 * D_FF), lambda i: (0, 0), pipeline_mode=ONE_BUF),
                  pl.BlockSpec((3, 2 * D_FF), lambda i: (0, 0)),
                  pl.BlockSpec((1, 2 * D_FF), lambda i: (0, 0))],
        out_specs=[pl.BlockSpec((tm, 2 * D_FF), lambda i: (i, 0)), pl.BlockSpec((tm, D_FF), lambda i: (i, 0))],
        out_shape=[jax.ShapeDtypeStruct((S, 2 * D_FF), BF16), jax.ShapeDtypeStruct((S, D_FF), BF16)],
        scratch_shapes=[pltpu.VMEM((16, 2 * D_FF), BF16)],
        compiler_params=_cp(1, 48),
    )(h2, w_up_p, wc_p, bc_p)


def _conv_glu_up_norm2_bwd(da, u0p, wc_p, bc_p, w_up_t, x1, dx2, g, scale, mixed, gate, tm=256, carried=()):
    S = u0p.shape[0]
    tm = min(tm, S)
    hb = tm // 16
    nlast = S // 16 - 1
    widths = [2 * LANES] * (HB // (2 * LANES)) + ([LANES] if HB % (2 * LANES) else [])

    def body(da_ref, dan_ref, u_ref, p_ref, n_ref, w_ref, b_ref, wt_ref, x_ref, dr_ref, g_ref, sc_ref, m_ref, gt_ref,
             o_ref, s_ref, dx_ref, dm_ref, s2_ref, acc_ref):
        i = pl.program_id(0)
        first = i == 0
        last = i == pl.num_programs(0) - 1

        @pl.when(first)
        def _():
            s_ref[...] = jnp.zeros_like(s_ref)
            s2_ref[...] = jnp.zeros_like(s2_ref)

        n = tm + 16
        started = False
        for j in range(2):
            base = 0
            for wd in widths:
                du0s = ([], [])
                for q in range(wd // LANES):
                    k0 = base + q * LANES
                    kc = slice(j * HB + k0, j * HB + k0 + LANES)
                    dae = jnp.concatenate([da_ref[:, kc].astype(F32),
                                           jnp.where(last, 0.0, dan_ref[:, kc].astype(F32))], axis=0)
                    halves = []
                    for off in (2 * j * HB + k0, 2 * j * HB + HB + k0):
                        cs = slice(off, off + LANES)
                        ext = jnp.concatenate([jnp.where(first, 0.0, p_ref[:, cs].astype(F32)),
                                               u_ref[:, cs].astype(F32), n_ref[:, cs].astype(F32)], axis=0)
                        u, e1, e2 = _conv3(ext, w_ref, b_ref, cs)
                        halves.append((u[16:], ext[16:16 + tm], e1[16:16 + tm], e2[16:16 + tm], cs))
                    val, gt = halves[0][0], halves[1][0]
                    sg = _sigmoid_fast(gt)
                    dus = (dae * (gt * sg), dae * val * (sg * (1.0 + gt * (1.0 - sg))))
                    for t, (du, (_, x0, x1_, x2, cs)) in enumerate(zip(dus, halves)):
                        du0 = (w_ref[2:3, cs] * du + w_ref[1:2, cs] * pltpu.roll(du, n - 1, 0)
                               + w_ref[0:1, cs] * pltpu.roll(du, n - 2, 0))[:tm].astype(BF16)
                        o_ref[:, cs] = du0
                        du0s[t].append(du0)
                        dut = du[:tm]
                        s_ref[0:1, cs] += jnp.sum(dut, axis=0, keepdims=True)
                        s_ref[1:2, cs] += jnp.sum(dut * x2, axis=0, keepdims=True)
                        s_ref[2:3, cs] += jnp.sum(dut * x1_, axis=0, keepdims=True)
                        s_ref[3:4, cs] += jnp.sum(dut * x0, axis=0, keepdims=True)
                for t, off in enumerate((2 * j * HB + base, 2 * j * HB + HB + base)):
                    lhs = du0s[t][0] if len(du0s[t]) == 1 else jnp.concatenate(du0s[t], axis=1)
                    part = _dot(lhs, wt_ref[off:off + wd, :])
                    if started:
                        acc_ref[...] += part
                    else:
                        acc_ref[...] = part
                        started = True
                base += wd

        xv = x_ref[...]
        dhv = acc_ref[...]
        r = lax.rsqrt(jnp.mean(xv * xv, axis=-1, keepdims=True) + EPS)
        nv = xv * r
        gv = g_ref[...]
        hn = nv * gv
        dhn = dhv * (1.0 + sc_ref[...])
        dn = dhn * gv
        dx = dr_ref[...] + r * (dn - nv * jnp.mean(dn * nv, axis=-1, keepdims=True))
        dx_ref[...] = dx
        dm_ref[...] = (dx * (1.0 + gt_ref[...])).astype(BF16)
        s2_ref[0:1, :] += jnp.sum(dhv, axis=0, keepdims=True)
        s2_ref[1:2, :] += jnp.sum(dhv * hn, axis=0, keepdims=True)
        s2_ref[2:3, :] += jnp.sum(dhn * nv, axis=0, keepdims=True)
        s2_ref[3:4, :] += jnp.sum(dx * m_ref[...], axis=0, keepdims=True)

    full = 2 * D_FF
    return _call(
        body, carried, [da, da, u0p, u0p, u0p, wc_p, bc_p, w_up_t, x1, dx2, g, scale, mixed, gate],
        name="conv_glu_up_norm2_bwd", grid=(S // tm,),
        in_specs=[pl.BlockSpec((tm, D_FF), lambda i: (i, 0)),
                  pl.BlockSpec((16, D_FF), lambda i: (jnp.minimum((i + 1) * hb, nlast), 0)),
                  pl.BlockSpec((tm, full), lambda i: (i, 0)),
                  pl.BlockSpec((16, full), lambda i: (jnp.maximum(i * hb - 1, 0), 0)),
                  pl.BlockSpec((16, full), lambda i: (jnp.minimum((i + 1) * hb, nlast), 0)),
                  pl.BlockSpec((3, full), lambda i: (0, 0)),
                  pl.BlockSpec((1, full), lambda i: (0, 0)),
                  pl.BlockSpec((full, D), lambda i: (0, 0), pipeline_mode=ONE_BUF),
                  _rows(tm), _rows(tm), _vec(), _vec(), _rows(tm), _vec()],
        out_specs=[pl.BlockSpec((tm, full), lambda i: (i, 0)), pl.BlockSpec((8, full), lambda i: (0, 0)),
                   _rows(tm), _rows(tm), pl.BlockSpec((8, D), lambda i: (0, 0))],
        out_shape=[jax.ShapeDtypeStruct((S, full), BF16), jax.ShapeDtypeStruct((8, full), F32),
                   jax.ShapeDtypeStruct((S, D), F32), jax.ShapeDtypeStruct((S, D), BF16),
                   jax.ShapeDtypeStruct((8, D), F32)],
        scratch_shapes=[pltpu.VMEM((tm, D), F32)],
        compiler_params=_cp(1, 56))


def _final_loss(x1, mm, gate2, g_final, target, tm=256):
    S = x1.shape[0]
    tm = min(tm, S)
    skip = 2 * len(mm)

    def body(*refs):
        x1_ref, gt_ref, g_ref, t_ref, dx_ref, dy_ref, s_ref = refs[skip:]

        @pl.when(pl.program_id(0) == 0)
        def _():
            s_ref[...] = jnp.zeros_like(s_ref)

        y2 = _mm_rows_value(refs, len(mm))
        og = 1.0 + gt_ref[...]
        x2 = x1_ref[...] + og * y2
        r = lax.rsqrt(jnp.mean(x2 * x2, axis=-1, keepdims=True) + EPS)
        n = x2 * r
        g = g_ref[...]
        err = n * g - t_ref[...]
        dy = err * (1.0 / D)
        dn = dy * g
        dx2 = r * (dn - n * jnp.mean(dn * n, axis=-1, keepdims=True))
        dx_ref[...] = dx2
        dy_ref[...] = (dx2 * og).astype(BF16)
        s_ref[0:1, :] += jnp.sum(dy * n, axis=0, keepdims=True)
        s_ref[1:2, :] += jnp.sum(dx2 * y2, axis=0, keepdims=True)
        s_ref[2:3, :] += jnp.sum(err * err, axis=0, keepdims=True)

    ops, specs = _mm_rows(mm, tm)
    return pl.pallas_call(
        body, name="mm_down_final_loss", grid=(S // tm,),
        in_specs=specs + [_rows(tm), _vec(), _vec(), _rows(tm)],
        out_specs=[_rows(tm), _rows(tm), pl.BlockSpec((8, D), lambda i: (0, 0))],
        out_shape=[jax.ShapeDtypeStruct((S, D), F32), jax.ShapeDtypeStruct((S, D), BF16),
                   jax.ShapeDtypeStruct((8, D), F32)],
        compiler_params=_cp(1, 40),
    )(*ops, x1, gate2, g_final, target)


def _norm_mod_bwd(dh, xin, dres, g, scale, mixed, gate, name, tm=256, carried=()):
    S = xin.shape[0]
    tm = min(tm, S)
    with_gate = mixed is not None
    fused = isinstance(dh, list)
    skip = 2 * len(dh) if fused else 1

    def body(*refs):
        if with_gate:
            x_ref, dr_ref, g_ref, sc_ref, m_ref, gt_ref, dx_ref, dm_ref, s_ref = refs[skip:]
        else:
            x_ref, dr_ref, g_ref, sc_ref, dx_ref, s_ref = refs[skip:]

        @pl.when(pl.program_id(0) == 0)
        def _():
            s_ref[...] = jnp.zeros_like(s_ref)

        xv = x_ref[...]
        dhv = _mm_rows_value(refs, len(dh)) if fused else refs[0][...]
        r = lax.rsqrt(jnp.mean(xv * xv, axis=-1, keepdims=True) + EPS)
        n = xv * r
        g = g_ref[...]
        hn = n * g
        dhn = dhv * (1.0 + sc_ref[...])
        dn = dhn * g
        dx = dr_ref[...] + r * (dn - n * jnp.mean(dn * n, axis=-1, keepdims=True))
        dx_ref[...] = dx
        s_ref[0:1, :] += jnp.sum(dhv, axis=0, keepdims=True)
        s_ref[1:2, :] += jnp.sum(dhv * hn, axis=0, keepdims=True)
        s_ref[2:3, :] += jnp.sum(dhn * n, axis=0, keepdims=True)
        if with_gate:
            dm_ref[...] = (dx * (1.0 + gt_ref[...])).astype(BF16)
            s_ref[3:4, :] += jnp.sum(dx * m_ref[...], axis=0, keepdims=True)

    ins, in_specs = _mm_rows(dh, tm) if fused else ([dh], [_rows(tm)])
    ins += [xin, dres, g, scale]
    in_specs += [_rows(tm), _rows(tm), _vec(), _vec()]
    out_specs = [_rows(tm)]
    out_shape = [jax.ShapeDtypeStruct((S, D), F32)]
    if with_gate:
        ins += [mixed, gate]
        in_specs += [_rows(tm), _vec()]
        out_specs.append(_rows(tm))
        out_shape.append(jax.ShapeDtypeStruct((S, D), BF16))
    out_specs.append(pl.BlockSpec((8, D), lambda i: (0, 0)))
    out_shape.append(jax.ShapeDtypeStruct((8, D), F32))
    return _call(body, carried, ins, name=name, grid=(S // tm,), in_specs=in_specs, out_specs=out_specs,
                 out_shape=out_shape, compiler_params=_cp(1, 48 if fused else None))


def _tri(n, rel):
    row = lax.broadcasted_iota(jnp.int32, (n, n), 0)
    col = lax.broadcasted_iota(jnp.int32, (n, n), 1)
    return {"gt": row > col, "ge": row >= col, "lt": row < col, "le": row <= col}[rel]


def _pair_diag(mask):
    u = jnp.where(mask, 1.0, 0.0).astype(BF16)
    z = jnp.zeros_like(u)
    return jnp.concatenate([jnp.concatenate([u, z], axis=1), jnp.concatenate([z, u], axis=1)], axis=0)


def _pair_rows(xp, lo_half):
    z = jnp.zeros_like(xp)
    return jnp.concatenate([jnp.where(lo_half, xp, z), jnp.where(lo_half, z, xp)], axis=0)


def _sb_scores(z, causal, diag):
    ls, ps, es = [], [], []
    for hh in range(2):
        zz = z[:, hh * QB:(hh + 1) * QB]
        e = jnp.exp(-jnp.abs(zz))
        l = -(jnp.maximum(zz, 0.0) + jnp.log(1.0 + e))
        ps.append(l + zz)
        ls.append(jnp.where(causal, l, 0.0) if diag else l)
        es.append(e)
    return ls, ps, es


def _sb_fwd(proj, carried=()):
    S = proj.shape[0]
    nq = S // QB

    def body(q_ref, k_ref, v_ref, o_ref, t_ref, c_ref, acc_ref, qs_ref):
        i = pl.program_id(0)
        causal = _tri(QB, "gt")
        usuf = _pair_diag(_tri(QB, "gt"))
        lo_half = lax.broadcasted_iota(jnp.int32, (QB, 128), 1) < DK
        qs_ref[...] = q_ref[...] * 0.125

        def block(j, diag, nr):
            rows = pl.ds(pl.multiple_of(j * QB, QB), QB)
            rs = slice(0, nr)
            pairs = range(H_SB // 2)
            cols = [slice(pr * 128, (pr + 1) * 128) for pr in pairs]
            zs = [_dot_nt(qs_ref[rs, cols[pr]], _pair_rows(k_ref[rows, cols[pr]], lo_half)) for pr in pairs]
            sc = [_sb_scores(zs[pr], causal, diag) for pr in pairs]
            sufs = []
            for pr in pairs:
                lh, ll = _split(jnp.concatenate(sc[pr][0], axis=1))
                sufs.append(_dot(lh, usuf) + _dot(ll, usuf))
            cmax = None
            wps = []
            for pr in pairs:
                ws = []
                for hh in range(2):
                    h = 2 * pr + hh
                    b = sufs[pr][:, hh * QB:(hh + 1) * QB]
                    if not diag:
                        b = b + c_ref[h, rs, 0:1]
                    w = jnp.exp(sc[pr][1][hh] + b)
                    ws.append((jnp.where(causal, w, 0.0) if diag else w).astype(BF16))
                    cn = b[:, 0:1] + sc[pr][0][hh][:, 0:1]
                    c_ref[h, rs, 0:1] = cn
                    cmax = cn if cmax is None else jnp.maximum(cmax, cn)
                wps.append(jnp.concatenate(ws, axis=1))
            for pr in pairs:
                upd = _dot(wps[pr], _pair_rows(v_ref[rows, cols[pr]], lo_half))
                if diag:
                    acc_ref[rs, cols[pr]] = upd
                else:
                    acc_ref[rs, cols[pr]] += upd
            lo = jnp.max(cmax[:SB_HEAD_ROWS])
            return (jnp.max(cmax[SB_HEAD_ROWS:]) if nr > SB_HEAD_ROWS else None), lo

        def cond_full(st):
            return jnp.logical_and(st[0] >= 0, st[1] > SB_SKIP)

        def step_full(st):
            return (st[0] - 1,) + block(st[0], False, QB)

        def cond_head(st):
            return jnp.logical_and(st[0] >= 0, st[1] > SB_SKIP)

        def step_head(st):
            return st[0] - 1, block(st[0], False, SB_HEAD_ROWS)[1]

        j, _, lo = lax.while_loop(cond_full, step_full, (i - 1,) + block(i, True, QB))
        jfull = j + 1
        j, _ = lax.while_loop(cond_head, step_head, (j, lo))
        o_ref[...] = acc_ref[...].astype(BF16)
        t_ref[...] = jnp.zeros_like(t_ref)
        for h in range(H_SB):
            t_ref[h // 4, :, h % 4:h % 4 + 1] = c_ref[h, :, 0:1]
        t_ref[:, :, 8:9] = jnp.zeros((2, QB, 1), F32) + (j + 1).astype(F32)
        t_ref[:, :, 9:10] = jnp.zeros((2, QB, 1), F32) + jfull.astype(F32)

    return _call(
        body, carried, [proj, proj, proj], name="sb_fwd", grid=(nq,),
        in_specs=[pl.BlockSpec((QB, 512), lambda i: (i, 0)),
                  pl.BlockSpec((S, 512), lambda i: (0, 1), pipeline_mode=ONE_BUF),
                  pl.BlockSpec((S, 512), lambda i: (0, 2), pipeline_mode=ONE_BUF)],
        out_specs=[pl.BlockSpec((QB, 512), lambda i: (i, 0)),
                   pl.BlockSpec((2, QB, 128), lambda i: (0, i, 0))],
        out_shape=[jax.ShapeDtypeStruct((S, 512), BF16), jax.ShapeDtypeStruct((2, S, 128), F32)],
        scratch_shapes=[pltpu.VMEM((H_SB, QB, 128), F32), pltpu.VMEM((QB, 512), F32), pltpu.VMEM((QB, 512), BF16)],
        compiler_params=_cp(1, 40))


def _sb_bwd(proj, dcat, stats, carried=()):
    S = proj.shape[0]
    nq = S // QB

    def body(q_ref, k_ref, v_ref, do_ref, t_ref, dq_ref, dk_ref, dv_ref, dk_acc, dv_acc, dq_acc, pc_ref, qs_ref,
             qt_ref, dot_ref):
        i = pl.program_id(1)

        @pl.when(i == 0)
        def _():
            dk_acc[...] = jnp.zeros_like(dk_acc)
            dv_acc[...] = jnp.zeros_like(dv_acc)

        causal = _tri(QB, "gt")
        uin = _pair_diag(_tri(QB, "le"))
        uex = _pair_diag(_tri(QB, "lt"))
        lo_half = lax.broadcasted_iota(jnp.int32, (QB, 128), 1) < DK
        qs_ref[...] = q_ref[...] * 0.125
        lo_rows = lax.broadcasted_iota(jnp.int32, (128, QB), 0) < DK
        for pr in range(2):
            qt_ref[pr] = (q_ref[:, pr * 128:(pr + 1) * 128].astype(F32) * 0.125).T.astype(BF16)
            dot_ref[pr] = do_ref[:, pr * 128:(pr + 1) * 128].astype(F32).T.astype(BF16)
        pc_ref[...] = jnp.zeros_like(pc_ref)
        dq_acc[...] = jnp.zeros_like(dq_acc)
        jstart = jnp.max(t_ref[:, 8:9]).astype(jnp.int32)
        jfull = jnp.max(t_ref[:, 9:10]).astype(jnp.int32)

        def block(j, diag, nr):
            rows = pl.ds(pl.multiple_of(j * QB, QB), QB)
            rs = slice(0, nr)
            pairs = range(2)
            cols = [slice(pr * 128, (pr + 1) * 128) for pr in pairs]
            kbds = [_pair_rows(k_ref[rows, cols[pr]], lo_half) for pr in pairs]
            zs = [_dot_nt(qs_ref[rs, cols[pr]], kbds[pr]) for pr in pairs]
            dws = [_dot_nt(do_ref[rs, cols[pr]], _pair_rows(v_ref[rows, cols[pr]], lo_half)) for pr in pairs]
            sc = [_sb_scores(zs[pr], causal, diag) for pr in pairs]
            plins = []
            for pr in pairs:
                lh, ll = _split(jnp.concatenate(sc[pr][0], axis=1))
                plins.append(_dot(lh, uin) + _dot(ll, uin))
            wss, gss, gexs = [], [], []
            for pr in pairs:
                ws, gs = [], []
                for hh in range(2):
                    h = 2 * pr + hh
                    half = slice(hh * QB, (hh + 1) * QB)
                    b = (t_ref[rs, h:h + 1] - pc_ref[h, rs, 0:1]) - plins[pr][:, half]
                    w = jnp.exp(sc[pr][1][hh] + b)
                    if diag:
                        w = jnp.where(causal, w, 0.0)
                    ws.append(w)
                    gs.append(dws[pr][:, half] * w)
                wss.append(ws)
                gss.append(gs)
            for pr in pairs:
                gh, gl = _split(jnp.concatenate(gss[pr], axis=1))
                gexs.append(_dot(gh, uex) + _dot(gl, uex))
            dzbs = []
            for pr in pairs:
                dzs = []
                for hh in range(2):
                    h = 2 * pr + hh
                    half = slice(hh * QB, (hh + 1) * QB)
                    e = sc[pr][2][hh]
                    r = pl.reciprocal(1.0 + e, approx=True)
                    er = e * r
                    pos = zs[pr][:, half] >= 0.0
                    gx = gexs[pr][:, half]
                    g = gss[pr][hh]
                    dz = g * jnp.where(pos, er, r) - (gx + pc_ref[4 + h, rs, 0:1]) * jnp.where(pos, r, er)
                    dzs.append(jnp.where(causal, dz, 0.0) if diag else dz)
                    pc_ref[h, rs, 0:1] += plins[pr][:, half][:, QB - 1:QB]
                    pc_ref[4 + h, rs, 0:1] += gx[:, QB - 1:QB] + g[:, QB - 1:QB]
                dzbs.append(jnp.concatenate(dzs, axis=1).astype(BF16))
            for pr in pairs:
                dq_acc[rs, cols[pr]] += _dot(dzbs[pr], kbds[pr])
                r1 = _dot(qt_ref[pr, :, rs], dzbs[pr])
                dk_acc[pr, j] += jnp.where(lo_rows, r1[:, :QB], r1[:, QB:])
                r2 = _dot(dot_ref[pr, :, rs], jnp.concatenate(wss[pr], axis=1).astype(BF16))
                dv_acc[pr, j] += jnp.where(lo_rows, r2[:, :QB], r2[:, QB:])

        def step_head(j, carry):
            block(j, False, SB_HEAD_ROWS)
            return carry

        def step_full(j, carry):
            block(j, False, QB)
            return carry

        lax.fori_loop(jstart, jfull, step_head, 0)
        lax.fori_loop(jfull, i, step_full, 0)
        block(i, True, QB)
        dq_ref[...] = (dq_acc[...] * 0.125).astype(BF16)

        @pl.when(i == nq - 1)
        def _():
            def put(jj, carry):
                krows = pl.ds(pl.multiple_of(jj * QB, QB), QB)
                for pr in range(2):
                    dk_ref[krows, pr * 128:(pr + 1) * 128] = dk_acc[pr, jj].T.astype(BF16)
                    dv_ref[krows, pr * 128:(pr + 1) * 128] = dv_acc[pr, jj].T.astype(BF16)
                return carry
            lax.fori_loop(0, nq, put, 0)

    return _call(
        body, carried, [proj, proj, proj, dcat, stats], name="sb_bwd", grid=(2, nq),
        in_specs=[pl.BlockSpec((QB, 256), lambda g, i: (i, g)),
                  pl.BlockSpec((S, 256), lambda g, i: (0, 2 + g), pipeline_mode=ONE_BUF),
                  pl.BlockSpec((S, 256), lambda g, i: (0, 4 + g), pipeline_mode=ONE_BUF),
                  pl.BlockSpec((QB, 256), lambda g, i: (i, g)),
                  pl.BlockSpec((None, QB, 128), lambda g, i: (g, i, 0))],
        out_specs=[pl.BlockSpec((QB, 256), lambda g, i: (i, g)),
                   pl.BlockSpec((S, 256), lambda g, i: (0, g)),
                   pl.BlockSpec((S, 256), lambda g, i: (0, g))],
        out_shape=[jax.ShapeDtypeStruct((S, 512), BF16)] * 3,
        scratch_shapes=[pltpu.VMEM((2, nq, 128, QB), F32), pltpu.VMEM((2, nq, 128, QB), F32),
                        pltpu.VMEM((QB, 256), F32), pltpu.VMEM((8, QB, 128), F32), pltpu.VMEM((QB, 256), BF16),
                        pltpu.VMEM((2, 128, QB), BF16), pltpu.VMEM((2, 128, QB), BF16)],
        compiler_params=_cp(2, 56))


GLA_NC = 4
GLA_R = GLA_NC * CHUNK


def _chunk_tri(strict):
    row = lax.broadcasted_iota(jnp.int32, (GLA_R, GLA_R), 0)
    col = lax.broadcasted_iota(jnp.int32, (GLA_R, GLA_R), 1)
    m = jnp.logical_and(row // CHUNK == col // CHUNK, row > col if strict else row >= col)
    u = jnp.where(m, 1.0, 0.0).astype(BF16)
    return jnp.concatenate([u, u], axis=1)


def _per_chunk_rows(vals):
    return jnp.concatenate([jnp.broadcast_to(v, (CHUNK, v.shape[1])) for v in vals], axis=0)


def _head_blocks(st):
    row = lax.broadcasted_iota(jnp.int32, (H_GLA * DV, H_GLA * DK), 0)
    col = lax.broadcasted_iota(jnp.int32, (H_GLA * DV, H_GLA * DK), 1)
    t = jnp.concatenate([st.astype(BF16)] * H_GLA, axis=0)
    return jnp.where(row // DV == col // DK, t, jnp.zeros_like(t))


def _head_diag(big):
    head = lax.broadcasted_iota(jnp.int32, (DV, H_GLA * DK), 1) // DK
    out = big[0:DV]
    for h in range(1, H_GLA):
        out = jnp.where(head == h, big[h * DV:(h + 1) * DV], out)
    return out


def _gla_gate4(gf_ref, wfg_ref, bfg_ref):
    f = _dot(gf_ref[...], wfg_ref[...]) + bfg_ref[...]
    _, la, _ = _log_sigmoid_parts(f)
    lah, lal = _split(la * (1.0 / 16.0))
    cum = _dot(_chunk_tri(False), jnp.concatenate([lah, lal], axis=0))
    tots = [cum[(c + 1) * CHUNK - 1:(c + 1) * CHUNK, :] for c in range(GLA_NC)]
    return f, jnp.exp(_per_chunk_rows(tots) - cum), [jnp.exp(t) for t in tots]


def _gla_specs4(ns, rev):
    def ix(i):
        return ns - 1 - i if rev else i
    return [pl.BlockSpec((GLA_R, 256), lambda i: (ix(i), 6)),
            pl.BlockSpec((GLA_R, 256), lambda i: (ix(i), 7)),
            pl.BlockSpec((GLA_R, 512), lambda i: (ix(i), 4)),
            pl.BlockSpec((GLA_R, 512), lambda i: (ix(i), 5)),
            pl.BlockSpec((GLA_R, 128), lambda i: (ix(i), 24))]


def _gla_fwd(proj, wfg_p, bfg, ggla, carried=()):
    S = proj.shape[0]
    ns = S // GLA_R

    def body(q_ref, k_ref, v_ref, gg_ref, gf_ref, wfg_ref, bfg_ref, ggla_ref, o_ref, st_ref, state):
        @pl.when(pl.program_id(0) == 0)
        def _():
            state[...] = jnp.zeros_like(state)

        _, e, decs = _gla_gate4(gf_ref, wfg_ref, bfg_ref)
        kdec = (k_ref[...].astype(F32) * e).astype(BF16)
        rows = [slice(c * CHUNK, (c + 1) * CHUNK) for c in range(GLA_NC)]
        kvs = [_head_diag(_dot_tn(v_ref[rows[c], :], kdec[rows[c]])) for c in range(GLA_NC)]
        st = state[...]
        sts = []
        for c in range(GLA_NC):
            st = st * decs[c] + kvs[c]
            st_ref[c] = st
            sts.append(st)
        state[...] = st
        o = jnp.concatenate([_dot_nt(q_ref[rows[c], :] * 0.125, _head_blocks(sts[c])) for c in range(GLA_NC)], axis=0)
        for h in range(H_GLA):
            vs = slice(h * DV, (h + 1) * DV)
            oh = o[:, vs]
            ohn = oh * lax.rsqrt(jnp.mean(oh * oh, axis=-1, keepdims=True) + EPS)
            gg = gg_ref[:, vs].astype(F32)
            o_ref[:, vs] = ((ohn * ggla_ref[:, vs]) * (gg * _sigmoid(gg))).astype(BF16)

    return _call(
        body, carried, [proj, proj, proj, proj, proj, wfg_p, bfg, ggla], name="gla_fwd", grid=(ns,),
        in_specs=_gla_specs4(ns, False) + [pl.BlockSpec((128, 256), lambda i: (0, 0)),
                                           pl.BlockSpec((1, 256), lambda i: (0, 0)),
                                           pl.BlockSpec((1, 512), lambda i: (0, 0))],
        out_specs=[pl.BlockSpec((GLA_R, 512), lambda i: (i, 0)),
                   pl.BlockSpec((GLA_NC, 128, 256), lambda i: (i, 0, 0))],
        out_shape=[jax.ShapeDtypeStruct((S, 512), BF16), jax.ShapeDtypeStruct((S // CHUNK, 128, 256), F32)],
        scratch_shapes=[pltpu.VMEM((128, 256), F32)],
        compiler_params=_cp(1))


def _gla_bwd(dcat, proj, states, wfg_p, bfg, ggla, carried=()):
    S = proj.shape[0]
    ns = S // GLA_R

    def body(do_ref, q_ref, k_ref, v_ref, gg_ref, gf_ref, sc_ref, sp_ref, wfg_ref, bfg_ref, ggla_ref,
             dp_ref, s_ref, dw_ref, carry):
        sr = pl.program_id(0)

        @pl.when(sr == 0)
        def _():
            carry[...] = jnp.zeros_like(carry)
            s_ref[...] = jnp.zeros_like(s_ref)
            dw_ref[...] = jnp.zeros_like(dw_ref)

        f, e, decs = _gla_gate4(gf_ref, wfg_ref, bfg_ref)
        kf = k_ref[...].astype(F32) * e
        kdec = kf.astype(BF16)
        rows = [slice(c * CHUNK, (c + 1) * CHUNK) for c in range(GLA_NC)]
        sts = [sc_ref[c] for c in range(GLA_NC)]
        st_before = jnp.where(sr < ns - 1, sp_ref[0], 0.0)
        sbd = [_head_blocks(sts[c]) for c in range(GLA_NC)]
        qs = q_ref[...] * 0.125
        o = jnp.concatenate([_dot_nt(qs[rows[c]], sbd[c]) for c in range(GLA_NC)], axis=0)
        dobs = []
        for h in range(H_GLA):
            vs = slice(h * DV, (h + 1) * DV)
            oh = o[:, vs]
            rr = lax.rsqrt(jnp.mean(oh * oh, axis=-1, keepdims=True) + EPS)
            ohn = oh * rr
            gg = gg_ref[:, vs].astype(F32)
            sg = _sigmoid(gg)
            dout = do_ref[:, vs].astype(F32)
            gl = ggla_ref[:, vs]
            dp_ref[:, 1024 + h * DV:1024 + (h + 1) * DV] = (
                dout * (ohn * gl) * (sg * (1.0 + gg * (1.0 - sg)))).astype(BF16)
            dt1 = dout * (gg * sg)
            s_ref[0:1, vs] += jnp.sum(dt1 * ohn, axis=0, keepdims=True)
            dohn = dt1 * gl
            dobs.append((rr * (dohn - ohn * jnp.mean(dohn * ohn, axis=-1, keepdims=True))).astype(BF16))
        dob = jnp.concatenate(dobs, axis=1)
        dsout = []
        for c in range(GLA_NC):
            dp_ref[rows[c], 0:256] = (_dot(dob[rows[c]], sbd[c]) * 0.125).astype(BF16)
            dsout.append(_head_diag(_dot_tn(dob[rows[c]], qs[rows[c]])))
        g = carry[...]
        gts, ddecs = [None] * GLA_NC, [None] * GLA_NC
        for c in reversed(range(GLA_NC)):
            gts[c] = dsout[c] + g
            ddecs[c] = jnp.sum(gts[c] * (sts[c - 1] if c > 0 else st_before), axis=0, keepdims=True) * decs[c]
            g = gts[c] * decs[c]
        carry[...] = g
        dkds = []
        for c in range(GLA_NC):
            gbd = _head_blocks(gts[c])
            dkds.append(_dot(v_ref[rows[c], :], gbd))
            dp_ref[rows[c], 512:1024] = _dot_nt(kdec[rows[c]], gbd).astype(BF16)
        dkd = jnp.concatenate(dkds, axis=0)
        dp_ref[:, 256:512] = (dkd * e).astype(BF16)
        wh, wl = _split(dkd * kf)
        dla = _dot(_chunk_tri(True), jnp.concatenate([wh, wl], axis=0)) + _per_chunk_rows(ddecs)
        df = dla * _sigmoid(-f) * (1.0 / 16.0)
        dfb = df.astype(BF16)
        s_ref[1:2, 0:256] += jnp.sum(df, axis=0, keepdims=True)
        dw_ref[...] += _dot_tn(gf_ref[...], dfb)
        dp_ref[:, 1536:1664] = _dot_nt(dfb, wfg_ref[...]).astype(BF16)

    return _call(
        body, carried, [dcat, proj, proj, proj, proj, proj, states, states, wfg_p, bfg, ggla],
        name="gla_bwd", grid=(ns,),
        in_specs=[pl.BlockSpec((GLA_R, 512), lambda i: (ns - 1 - i, 1))] + _gla_specs4(ns, True) + [
            pl.BlockSpec((GLA_NC, 128, 256), lambda i: (ns - 1 - i, 0, 0)),
            pl.BlockSpec((1, 128, 256), lambda i: (jnp.maximum((ns - 1 - i) * GLA_NC - 1, 0), 0, 0)),
            pl.BlockSpec((128, 256), lambda i: (0, 0)),
            pl.BlockSpec((1, 256), lambda i: (0, 0)),
            pl.BlockSpec((1, 512), lambda i: (0, 0))],
        out_specs=[pl.BlockSpec((GLA_R, 1664), lambda i: (ns - 1 - i, 0)),
                   pl.BlockSpec((8, 512), lambda i: (0, 0)),
                   pl.BlockSpec((128, 256), lambda i: (0, 0))],
        out_shape=[jax.ShapeDtypeStruct((S, 1664), BF16), jax.ShapeDtypeStruct((8, 512), F32),
                   jax.ShapeDtypeStruct((128, 256), F32)],
        scratch_shapes=[pltpu.VMEM((128, 256), F32)],
        compiler_params=_cp(1))


def _sum_leading(a, name):
    n = a.shape[0]

    def body(a_ref, o_ref):
        acc = a_ref[0]
        for k in range(1, n):
            acc = acc + a_ref[k]
        o_ref[...] = acc

    return pl.pallas_call(
        body, name=name, out_shape=jax.ShapeDtypeStruct(a.shape[1:], F32),
        in_specs=[VMEM_SPEC], out_specs=VMEM_SPEC,
    )(a)


def _sum_chip(own, recv, name):
    R, C = own.shape
    tr, tc = _tile2d(R, C, 1024 * 1024)

    def body(o_ref, r_ref, p_ref):
        acc = o_ref[...]
        for k in range(3):
            acc = acc + r_ref[k].astype(F32)
        p_ref[...] = acc

    return pl.pallas_call(
        body, name=name, grid=(R // tr, C // tc),
        in_specs=[pl.BlockSpec((tr, tc), lambda i, j: (i, j)), pl.BlockSpec((3, tr, tc), lambda i, j: (0, i, j))],
        out_specs=pl.BlockSpec((tr, tc), lambda i, j: (i, j)),
        out_shape=jax.ShapeDtypeStruct((R, C), F32), compiler_params=_cp(2, 40),
    )(own, recv)


def _adamw(w, p, q, m, v, name):
    R, C = w.shape
    tr, tc = _tile2d(R, C, 1024 * 1024)
    two = q is not None

    def body(*refs):
        if two:
            w_ref, p_ref, q_ref, m_ref, v_ref, g_out, d_out, m_out, v_out = refs
            g = p_ref[...] + q_ref[...]
        else:
            w_ref, p_ref, m_ref, v_ref, g_out, d_out, m_out, v_out = refs
            g = p_ref[...]
        m2 = B1 * m_ref[...] + (1.0 - B1) * g
        v2 = B2 * v_ref[...] + (1.0 - B2) * (g * g)
        m_hat = m2 / (1.0 - B1 ** STEP)
        v_hat = v2 / (1.0 - B2 ** STEP)
        g_out[...] = g
        d_out[...] = -LR * (m_hat / (jnp.sqrt(v_hat) + EPS_A) + WD * w_ref[...])
        m_out[...] = m2
        v_out[...] = v2

    spec = pl.BlockSpec((tr, tc), lambda i, j: (i, j))
    ins = [w, p, q, m, v] if two else [w, p, m, v]
    return pl.pallas_call(
        body, name=name, grid=(R // tr, C // tc),
        in_specs=[spec] * len(ins), out_specs=[spec] * 4,
        out_shape=[jax.ShapeDtypeStruct((R, C), F32)] * 4, compiler_params=_cp(2, 40),
    )(*ins)


def _cols_to_chips(a, width):
    return a.reshape(a.shape[0], 4, width).swapaxes(0, 1)


def _chips_to_cols(a):
    return a.swapaxes(0, 1).reshape(a.shape[1], 4 * a.shape[2])


def _swap_mid(a):
    lead = a.shape[:-1]
    return a.reshape(lead + (2, 2, HB)).swapaxes(-3, -2).reshape(lead + (4 * HB,))


def kernel(x, c, w_ada, b_ada, g_norm1, w_in, w_fg2, b_fg2, g_gla_out, w_out, g_norm2, w_up, w_conv, b_conv, w_down, g_final, loss_target, m_w_ada, m_b_ada, m_g_norm1, m_w_in, m_w_fg2, m_b_fg2, m_g_gla_out, m_w_out, m_g_norm2, m_w_up, m_w_conv, m_b_conv, m_w_down, m_g_final, v_w_ada, v_b_ada, v_g_norm1, v_w_in, v_w_fg2, v_b_fg2, v_g_gla_out, v_w_out, v_g_norm2, v_w_up, v_w_conv, v_b_conv, v_w_down, v_g_final):
    xi, yi, ci = lax.axis_index("x"), lax.axis_index("y"), lax.axis_index("c")
    cidx = 2 * xi + yi
    didx = 4 * xi + 2 * yi + ci
    xs = x[0]
    tgt = loss_target[0]
    gfin = g_final.reshape(1, D)
    AW = D * 6 // 4

    c_all = _allgather8(c, "gather_c").reshape(8, D)
    c_pad = jnp.concatenate([c_all, jnp.zeros((8, D), F32)], axis=0)
    mod_part = _ada_fwd(c_pad, w_ada[0], lax.dynamic_slice(b_ada, (0, cidx * AW), (1, AW)))[:8]
    small = jnp.concatenate([mod_part.reshape(-1), w_conv.reshape(-1), w_fg2.reshape(-1)]).reshape(-1, 128)
    small_g = _allgather4(small, "gather_small").reshape(4, -1)
    mod = lax.dynamic_index_in_dim(small_g[:, :8 * AW].reshape(4, 8, AW), didx, axis=1, keepdims=False).reshape(1, 6 * D)
    shift1, scale1, gate1, shift2, scale2, gate2 = [mod[:, k * D:(k + 1) * D] for k in range(6)]
    o1 = 8 * AW
    o2 = o1 + 3 * HB
    wc_p = _swap_mid(_chips_to_cols(small_g[:, o1:o2].reshape(4, 3, HB)))
    bc_p = _swap_mid(b_conv)
    wfg_full = _chips_to_cols(small_g[:, o2:].reshape(4, RANK, 64))
    wfg_p = jnp.concatenate([wfg_full, jnp.zeros((128 - RANK, 256), F32)], axis=0).astype(BF16)

    w_in_t = _allgather4_split(w_in[0].T.astype(BF16), "gather_w_in").reshape(N_IN, D)
    w_in_t = jnp.concatenate([w_in_t, jnp.zeros((N_IN_P - N_IN, D), BF16)], axis=0)
    w_in_p = w_in_t.T

    h = _norm_mod(xs, g_norm1, shift1, scale1)
    proj, (w_down_g,) = _mm([(h, w_in_p)], BF16, "mm_in", 256, N_IN_P, 48,
                            carried=[("gather", w_down[0].astype(BF16), False)])
    w_down_f = w_down_g.reshape(D_FF, D)
    (o_gla, states), (w_out_g,) = _gla_fwd(proj, wfg_p, b_fg2, g_gla_out,
                                           carried=[("gather", w_out[0].astype(BF16), False)])
    w_out_f = w_out_g.reshape(D, D)
    (o_sb, stats), (w_up_g,) = _sb_fwd(proj, carried=[("gather", w_up[0].astype(BF16), False)])
    w_up_p = _swap_mid(_chips_to_cols(w_up_g))
    x1, h2, mixed = _resid_norm_mod(xs, [(o_sb, w_out_f[:512]), (o_gla, w_out_f[512:])],
                                    gate1, g_norm2, shift2, scale2)
    u0p, a = _up_conv_glu(h2, w_up_p, wc_p, bc_p)
    dx2, dy2, s_fin = _final_loss(x1, [(a, w_down_f)], gate2, gfin, tgt)

    da = _mm([(dy2, w_down_f.T)], BF16, "mm_down_t", 512, D_FF, 48)
    dw_down, dw_down_h = [t.reshape(4, D_FF // 4, D) for t in _mm_tn([a], dy2, "mm_dw_down", D, 1024, 60)]
    (du0p, s_conv, dx1, dmixed, s_n2), (rc_down,) = _conv_glu_up_norm2_bwd(
        da, u0p, wc_p, bc_p, w_up_p.T, x1, dx2, g_norm2, scale2, mixed, gate1,
        carried=[("scatter", dw_down_h, False)])
    dw_up, dw_up_h = _mm_tn([h2], du0p, "mm_dw_up", HB, 1024, 56)
    dcat = _mm([(dmixed, w_out_f.T)], BF16, "mm_out_t", 512, D)
    dw_out, dw_out_h = [t.reshape(4, D // 4, D) for t in _mm_tn([o_sb, o_gla], dmixed, "mm_dw_out", D, 512)]
    (dq, dk, dv), (rc_up,) = _sb_bwd(proj, dcat, stats, carried=[("scatter", dw_up_h, True)])
    (dp_gla, s_gla, dwfg), (rc_out,) = _gla_bwd(dcat, proj, states, wfg_p, b_fg2, g_gla_out,
                                                carried=[("scatter", dw_out_h, False)])
    dw_in, dw_in_h = _mm_tn([dq, dk, dv, dp_gla], h, "mm_dw_in", D, 1024, 60)
    dw_in_h = dw_in_h[0, :N_IN].reshape(4, N_IN // 4, D)
    dw_in_own = lax.dynamic_slice(dw_in[0], (cidx * (N_IN // 4), 0), (N_IN // 4, D))
    dh, (rc_in,) = _mm(
        [(dq, w_in_t[:512]), (dk, w_in_t[512:1024]), (dv, w_in_t[1024:1536]), (dp_gla, w_in_t[1536:])],
        F32, "mm_in_t", 256, D, 48, carried=[("scatter", dw_in_h, False)])
    (gx, s_n1), _ = _norm_mod_bwd(dh, xs, dx1, g_norm1, scale1, None, None, "norm1_bwd")

    dmod = jnp.concatenate([s_n1[0], s_n1[1], s_n2[3], s_n2[0], s_n2[1], s_fin[1]])
    s_conv_n = _swap_mid(s_conv[:4])
    part = jnp.concatenate([dmod, s_n1[2], s_n2[2], s_fin[0], s_gla[0], s_gla[1, :256], s_conv_n[0],
                            s_conv_n[1:4].reshape(-1), dwfg[:RANK].reshape(-1),
                            jnp.broadcast_to(jnp.sum(s_fin[2]), (128,))]).reshape(-1, 128)
    parts = _allgather8(part, "gather_small_grads")
    tot = _sum_leading(parts, "sum_small_grads").reshape(-1)
    loss = 0.5 / D * tot[-1]
    dmod_all = parts.reshape(8, -1)[:, :6 * D]
    offs = [0]
    for n in (6 * D, D, D, D, 512, 256, 2 * D_FF, 3 * 2 * D_FF, RANK * 256):
        offs.append(offs[-1] + n)
    g_b_ada, g_g1, g_g2, g_gf, g_ggla, g_bfg, g_bconv, g_wconv_full, g_wfg_full = [
        tot[offs[k]:offs[k + 1]] for k in range(9)]
    g_wconv = lax.dynamic_index_in_dim(_cols_to_chips(g_wconv_full.reshape(3, 2 * D_FF), HB), cidx, 0, keepdims=False)
    g_wfg = lax.dynamic_index_in_dim(_cols_to_chips(g_wfg_full.reshape(RANK, 256), 64), cidx, 0, keepdims=False)

    dmod_pad = jnp.concatenate([dmod_all, jnp.zeros((8, 6 * D), F32)], axis=0)
    g_w_ada = _ada_bwd(c_pad, lax.dynamic_slice(dmod_pad, (0, cidx * AW), (16, AW)))

    def own(blocks, swapped=False):
        return lax.dynamic_index_in_dim(blocks, _slot(cidx, swapped), axis=0, keepdims=False)

    p_in = _sum_chip(dw_in_own, rc_in, "rs_w_in_sum")
    p_out = _sum_chip(own(dw_out), rc_out, "rs_w_out_sum")
    p_up = _sum_chip(own(dw_up, True), rc_up, "rs_w_up_sum")
    p_down = _sum_chip(own(dw_down), rc_down, "rs_w_down_sum")
    q_in, q_out, q_up, q_down = _pair_swap([p_in, p_out, p_up, p_down], "rs_swap")

    out = {}
    out["w_ada"] = _adamw(w_ada[0], g_w_ada, None, m_w_ada[0], v_w_ada[0], "adamw_w_ada")
    out["w_in"] = [t.T for t in _adamw(w_in[0].T, p_in, q_in, m_w_in[0].T, v_w_in[0].T, "adamw_w_in")]
    out["w_out"] = _adamw(w_out[0], p_out, q_out, m_w_out[0], v_w_out[0], "adamw_w_out")
    out["w_up"] = _adamw(w_up[0], p_up, q_up, m_w_up[0], v_w_up[0], "adamw_w_up")
    out["w_down"] = _adamw(w_down[0], p_down, q_down, m_w_down[0], v_w_down[0], "adamw_w_down")
    small_names = ["b_ada", "g_norm1", "w_fg2", "b_fg2", "g_gla_out", "g_norm2", "w_conv", "b_conv", "g_final"]
    small_w = [b_ada, g_norm1, w_fg2, b_fg2, g_gla_out, g_norm2, w_conv, b_conv, g_final]
    small_m = [m_b_ada, m_g_norm1, m_w_fg2, m_b_fg2, m_g_gla_out, m_g_norm2, m_w_conv, m_b_conv, m_g_final]
    small_v = [v_b_ada, v_g_norm1, v_w_fg2, v_b_fg2, v_g_gla_out, v_g_norm2, v_w_conv, v_b_conv, v_g_final]
    small_gr = [g_b_ada, g_g1, g_wfg, g_bfg, g_ggla, g_g2, g_wconv, g_bconv, g_gf]

    def pack(arrs):
        flat = jnp.concatenate([t.reshape(-1) for t in arrs])
        return jnp.concatenate([flat, jnp.zeros((-flat.shape[0]) % 1024, F32)]).reshape(-1, 128)

    packed = _adamw(pack(small_w), pack(small_gr), None, pack(small_m), pack(small_v), "adamw_small")
    off = 0
    for nm, wt in zip(small_names, small_w):
        n = wt.size
        out[nm] = [t.reshape(-1)[off:off + n].reshape(wt.shape) for t in packed]
        off += n
    for nm in ("w_ada", "w_in", "w_out", "w_up", "w_down"):
        out[nm] = [t[None] for t in out[nm]]

    names = ["w_ada", "b_ada", "g_norm1", "w_in", "w_fg2", "b_fg2", "g_gla_out", "w_out", "g_norm2", "w_up",
             "w_conv", "b_conv", "w_down", "g_final"]
    res = [loss, gx[None]]
    for k in range(4):
        res += [out[nm][k] for nm in names]
    return tuple(res)
```

```python
import functools

import jax
import jax.numpy as jnp
from jax import lax
from jax.experimental import pallas as pl
from jax.experimental.pallas import tpu as pltpu

F32 = jnp.float32
BF16 = jnp.bfloat16
MESH = pl.DeviceIdType.MESH

D = 1024
H_SB = 8
DK = 64
DV = 128
H_GLA = 4
CHUNK = 64
RANK = 16
N_IN = 3088
N_IN_P = 3200
D_FF = 2816
HB = D_FF // 2
LANES = 128
EPS = 1e-6
QB = 128
SB_SKIP = -120.0
SB_HEAD_ROWS = 64

LR, B1, B2, EPS_A, WD, STEP = 0.001, 0.9, 0.999, 1e-08, 0.01, 10

ANY = pl.BlockSpec(memory_space=pl.ANY)
VMEM_SPEC = pl.BlockSpec(memory_space=pltpu.VMEM)
ONE_BUF = pl.Buffered(1)


def _cp(ndim=0, vmem_mb=None):
    kw = {}
    if ndim:
        kw["dimension_semantics"] = ("arbitrary",) * ndim
    if vmem_mb:
        kw["vmem_limit_bytes"] = vmem_mb * 1024 * 1024
    return pltpu.CompilerParams(**kw)


def _dot(a, b):
    return jnp.dot(a, b, preferred_element_type=F32)


def _dot_nt(a, b):
    return lax.dot_general(a, b, (((1,), (1,)), ((), ())), preferred_element_type=F32)


def _dot_tn(a, b):
    return lax.dot_general(a, b, (((0,), (0,)), ((), ())), preferred_element_type=F32)


def _split(x):
    hi = x.astype(BF16)
    lo = (x - hi.astype(F32)).astype(BF16)
    return hi, lo


def _sigmoid(x):
    return jax.nn.sigmoid(x)


def _sigmoid_fast(x):
    return pl.reciprocal(1.0 + jnp.exp(-x), approx=True)


def _log_sigmoid_parts(z):
    e = jnp.exp(-jnp.abs(z))
    sp = jnp.log1p(e)
    return -(jnp.maximum(z, 0.0) + sp), jnp.minimum(z, 0.0) - sp, e


def _tile2d(rows, cols, budget=512 * 1024):
    best = None
    for t in range(8, rows + 1, 8):
        if rows % t == 0 and t * cols * 4 <= budget:
            best = t
    if best is not None:
        return best, cols
    best = LANES if cols % LANES == 0 else cols
    for t in range(LANES, cols + 1, LANES):
        if cols % t == 0 and rows * t * 4 <= budget:
            best = t
    return rows, best


def _flip(v, bit):
    return 1 - v if bit else v


def _allgather8(a, name):
    def body(a_ref, o_ref, ssem, rsem, lsem):
        x, y, c = lax.axis_index("x"), lax.axis_index("y"), lax.axis_index("c")
        me = 4 * x + 2 * y + c
        loc = pltpu.make_async_copy(a_ref, o_ref.at[me], lsem)
        loc.start()
        sends = []
        for r in range(1, 8):
            peer = (_flip(x, r & 4), _flip(y, r & 2), _flip(c, r & 1))
            cp = pltpu.make_async_remote_copy(
                src_ref=a_ref, dst_ref=o_ref.at[me], send_sem=ssem.at[r - 1], recv_sem=rsem.at[r - 1],
                device_id=peer, device_id_type=MESH)
            cp.start()
            sends.append(cp)
        for r in range(1, 8):
            peer = (_flip(x, r & 4), _flip(y, r & 2), _flip(c, r & 1))
            pidx = 4 * peer[0] + 2 * peer[1] + peer[2]
            pltpu.make_async_remote_copy(
                src_ref=a_ref, dst_ref=o_ref.at[pidx], send_sem=ssem.at[r - 1], recv_sem=rsem.at[r - 1],
                device_id=peer, device_id_type=MESH).wait_recv()
        for cp in sends:
            cp.wait_send()
        loc.wait()

    return pl.pallas_call(
        body, name=name,
        out_shape=jax.ShapeDtypeStruct((8,) + a.shape, a.dtype),
        in_specs=[VMEM_SPEC], out_specs=VMEM_SPEC,
        scratch_shapes=[pltpu.SemaphoreType.DMA((7,)), pltpu.SemaphoreType.DMA((7,)), pltpu.SemaphoreType.DMA],
    )(a)


def _allgather4(a, name):
    def body(a_ref, o_ref, ssem, rsem, lsem):
        x, y, c = lax.axis_index("x"), lax.axis_index("y"), lax.axis_index("c")
        me = 2 * x + y
        loc = pltpu.make_async_copy(a_ref, o_ref.at[me], lsem)
        loc.start()
        sends = []
        for r in range(1, 4):
            peer = (_flip(x, r & 2), _flip(y, r & 1), c)
            cp = pltpu.make_async_remote_copy(
                src_ref=a_ref, dst_ref=o_ref.at[me], send_sem=ssem.at[r - 1], recv_sem=rsem.at[r - 1],
                device_id=peer, device_id_type=MESH)
            cp.start()
            sends.append(cp)
        for r in range(1, 4):
            peer = (_flip(x, r & 2), _flip(y, r & 1), c)
            pidx = 2 * peer[0] + peer[1]
            pltpu.make_async_remote_copy(
                src_ref=a_ref, dst_ref=o_ref.at[pidx], send_sem=ssem.at[r - 1], recv_sem=rsem.at[r - 1],
                device_id=peer, device_id_type=MESH).wait_recv()
        for cp in sends:
            cp.wait_send()
        loc.wait()

    return pl.pallas_call(
        body, name=name,
        out_shape=jax.ShapeDtypeStruct((4,) + a.shape, a.dtype),
        in_specs=[ANY], out_specs=ANY,
        scratch_shapes=[pltpu.SemaphoreType.DMA((3,)), pltpu.SemaphoreType.DMA((3,)), pltpu.SemaphoreType.DMA],
    )(a)


def _allgather4_split(a, name):
    R, C = a.shape
    hc = C // 2

    def body(a_ref, o_ref, ssem, rsem, fssem, frsem, lsem):
        x, y, c = lax.axis_index("x"), lax.axis_index("y"), lax.axis_index("c")
        me = 2 * x + y
        sibling = (x, y, 1 - c)
        mine = pl.ds(pl.multiple_of(c * hc, hc), hc)
        theirs = pl.ds(pl.multiple_of((1 - c) * hc, hc), hc)
        loc = pltpu.make_async_copy(a_ref, o_ref.at[me], lsem)
        loc.start()
        peers = [(_flip(x, r & 2), _flip(y, r & 1), c) for r in range(1, 4)]
        pidx = [2 * p[0] + p[1] for p in peers]
        sends = []
        for k in range(3):
            cp = pltpu.make_async_remote_copy(
                src_ref=a_ref.at[:, mine], dst_ref=o_ref.at[me, :, mine], send_sem=ssem.at[k], recv_sem=rsem.at[k],
                device_id=peers[k], device_id_type=MESH)
            cp.start()
            sends.append(cp)
        for k in range(3):
            landed = o_ref.at[pidx[k], :, mine]
            pltpu.make_async_remote_copy(
                src_ref=landed, dst_ref=landed, send_sem=ssem.at[k], recv_sem=rsem.at[k],
                device_id=peers[k], device_id_type=MESH).wait_recv()
            cp = pltpu.make_async_remote_copy(
                src_ref=landed, dst_ref=landed, send_sem=fssem.at[k], recv_sem=frsem.at[k],
                device_id=sibling, device_id_type=MESH)
            cp.start()
            sends.append(cp)
        for k in range(3):
            got = o_ref.at[pidx[k], :, theirs]
            pltpu.make_async_remote_copy(
                src_ref=got, dst_ref=got, send_sem=fssem.at[k], recv_sem=frsem.at[k],
                device_id=sibling, device_id_type=MESH).wait_recv()
        for cp in sends:
            cp.wait_send()
        loc.wait()

    return pl.pallas_call(
        body, name=name,
        out_shape=jax.ShapeDtypeStruct((4,) + a.shape, a.dtype),
        in_specs=[ANY], out_specs=ANY,
        scratch_shapes=[pltpu.SemaphoreType.DMA((3,))] * 4 + [pltpu.SemaphoreType.DMA],
    )(a)


def _slot(chip, swapped):
    return 2 * (chip % 2) + chip // 2 if swapped else chip


def _pair_swap(ps, name):
    n = len(ps)

    def body(*refs):
        x, y, c = lax.axis_index("x"), lax.axis_index("y"), lax.axis_index("c")
        ssem, rsem = refs[2 * n], refs[2 * n + 1]
        cps = [pltpu.make_async_remote_copy(
            src_ref=refs[k], dst_ref=refs[n + k], send_sem=ssem.at[k], recv_sem=rsem.at[k],
            device_id=(x, y, 1 - c), device_id_type=MESH) for k in range(n)]
        for cp in cps:
            cp.start()
        for cp in cps:
            cp.wait()

    return pl.pallas_call(
        body, name=name,
        out_shape=[jax.ShapeDtypeStruct(p.shape, p.dtype) for p in ps],
        in_specs=[ANY] * n, out_specs=[ANY] * n,
        scratch_shapes=[pltpu.SemaphoreType.DMA((n,)), pltpu.SemaphoreType.DMA((n,))],
    )(*ps)


def _carried_copies(kind, src_ref, dst_ref, sems, swapped):
    ssem, rsem, lsem = sems
    x, y, c = lax.axis_index("x"), lax.axis_index("y"), lax.axis_index("c")
    me = 2 * x + y
    starts, recvs = [], []
    if kind == "gather":
        starts.append(pltpu.make_async_copy(src_ref, dst_ref.at[me], lsem))
    for r in range(1, 4):
        peer = (_flip(x, r & 2), _flip(y, r & 1), c)
        pidx = 2 * peer[0] + peer[1]
        if kind == "gather":
            src, dst, landed = src_ref, dst_ref.at[me], dst_ref.at[pidx]
        else:
            src = src_ref.at[2 * peer[1] + peer[0] if swapped else pidx]
            dst = landed = dst_ref.at[r - 1]
        starts.append(pltpu.make_async_remote_copy(
            src_ref=src, dst_ref=dst, send_sem=ssem.at[r - 1], recv_sem=rsem.at[r - 1],
            device_id=peer, device_id_type=MESH))
        recvs.append(pltpu.make_async_remote_copy(
            src_ref=src, dst_ref=landed, send_sem=ssem.at[r - 1], recv_sem=rsem.at[r - 1],
            device_id=peer, device_id_type=MESH))
    return starts, recvs


def _call(body, carried, operands, *, name, grid, in_specs, out_specs, out_shape, scratch_shapes=(),
          compiler_params=None):
    single = not isinstance(out_shape, (list, tuple))
    out_specs = [out_specs] if single else list(out_specs)
    out_shape = [out_shape] if single else list(out_shape)
    n_in, n_out, n_sc, nh = len(operands), len(out_shape), len(scratch_shapes), len(carried)

    def full(*refs):
        ins, h_in = refs[:n_in], refs[n_in:n_in + nh]
        o0 = n_in + nh
        outs, h_out = refs[o0:o0 + n_out], refs[o0 + n_out:o0 + n_out + nh]
        s0 = o0 + n_out + nh
        scratch, sems = refs[s0:s0 + n_sc], refs[s0 + n_sc:]
        first = last = None
        for d in range(len(grid)):
            f = pl.program_id(d) == 0
            l = pl.program_id(d) == pl.num_programs(d) - 1
            first = f if first is None else jnp.logical_and(first, f)
            last = l if last is None else jnp.logical_and(last, l)

        def copies(t):
            return _carried_copies(carried[t][0], h_in[t], h_out[t], sems[3 * t:3 * t + 3], carried[t][2])

        if nh:
            @pl.when(first)
            def _():
                for t in range(nh):
                    for cp in copies(t)[0]:
                        cp.start()

        body(*ins, *outs, *scratch)

        if nh:
            @pl.when(last)
            def _():
                for t in range(nh):
                    starts, recvs = copies(t)
                    for cp in recvs:
                        cp.wait_recv()
                    for cp in starts:
                        if carried[t][0] == "gather" and cp is starts[0]:
                            cp.wait()
                        else:
                            cp.wait_send()

    h_shapes = [jax.ShapeDtypeStruct(((4,) + arr.shape) if kind == "gather" else ((3,) + arr.shape[1:]), arr.dtype)
                for kind, arr, _ in carried]
    sem_shapes = [pltpu.SemaphoreType.DMA((3,)), pltpu.SemaphoreType.DMA((3,)), pltpu.SemaphoreType.DMA] * nh
    res = pl.pallas_call(
        full, name=name, grid=grid, in_specs=list(in_specs) + [ANY] * nh, out_specs=out_specs + [ANY] * nh,
        out_shape=out_shape + h_shapes, scratch_shapes=list(scratch_shapes) + sem_shapes,
        compiler_params=compiler_params,
    )(*operands, *[arr for _, arr, _ in carried])
    main = res[:n_out]
    return (main[0] if single else main), list(res[n_out:])


def _mm(pairs, out_dtype, name, tm, tn, vmem_mb=None, carried=()):
    S = pairs[0][0].shape[0]
    N = pairs[0][1].shape[1]
    tm = min(tm, S)
    np_ = len(pairs)

    def body(*refs):
        acc = _dot(refs[0][...], refs[1][...])
        for t in range(1, np_):
            acc = acc + _dot(refs[2 * t][...], refs[2 * t + 1][...])
        refs[-1][...] = acc.astype(refs[-1].dtype)

    in_specs, ops = [], []
    for a, w in pairs:
        in_specs += [pl.BlockSpec((tm, a.shape[1]), lambda n, i: (i, 0)),
                     pl.BlockSpec((w.shape[0], tn), lambda n, i: (0, n))]
        ops += [a, w]
    out, got = _call(
        body, carried, ops, name=name, grid=(N // tn, S // tm), in_specs=in_specs,
        out_specs=pl.BlockSpec((tm, tn), lambda n, i: (i, n)),
        out_shape=jax.ShapeDtypeStruct((S, N), out_dtype),
        compiler_params=_cp(2, vmem_mb))
    return (out, got) if carried else out


def _mm_tn(a_list, b, name, bn, tk, vmem_mb=None):
    S, N = b.shape
    ms = [a.shape[1] for a in a_list]
    M = sum(ms)
    tk = min(tk, S)
    na = len(a_list)

    def body(*refs):
        b_ref, o_ref, o16_ref = refs[na], refs[na + 1], refs[na + 2]

        @pl.when(pl.program_id(1) == 0)
        def _():
            o_ref[...] = jnp.zeros_like(o_ref)
        off = 0
        for t in range(na):
            o_ref[off:off + ms[t], :] += _dot_tn(refs[t][...], b_ref[...])
            off += ms[t]

        @pl.when(pl.program_id(1) == pl.num_programs(1) - 1)
        def _():
            o16_ref[...] = o_ref[...].astype(BF16)

    spec = pl.BlockSpec((None, M, bn), lambda n, k: (n, 0, 0), pipeline_mode=ONE_BUF)
    return pl.pallas_call(
        body, name=name, grid=(N // bn, S // tk),
        in_specs=[pl.BlockSpec((tk, m), lambda n, k: (k, 0)) for m in ms] + [pl.BlockSpec((tk, bn), lambda n, k: (k, n))],
        out_specs=[spec, spec],
        out_shape=[jax.ShapeDtypeStruct((N // bn, M, bn), F32), jax.ShapeDtypeStruct((N // bn, M, bn), BF16)],
        compiler_params=_cp(2, vmem_mb),
    )(*a_list, b)


def _ada_fwd(c_all, w_sh, b_sh):
    def body(c_ref, w_ref, b_ref, o_ref):
        cv = c_ref[...]
        sc = (cv * _sigmoid(cv)).astype(BF16)
        o_ref[...] = _dot(sc, w_ref[...].astype(BF16)) + b_ref[...]

    return pl.pallas_call(
        body, name="ada_fwd", out_shape=jax.ShapeDtypeStruct((c_all.shape[0], w_sh.shape[1]), F32),
        in_specs=[VMEM_SPEC] * 3, out_specs=VMEM_SPEC, compiler_params=_cp(0, 40),
    )(c_all, w_sh, b_sh)


def _ada_bwd(c_all, dmod_sh):
    def body(c_ref, d_ref, o_ref):
        cv = c_ref[...]
        sc = (cv * _sigmoid(cv)).astype(BF16)
        o_ref[...] = _dot_tn(sc, d_ref[...].astype(BF16))

    return pl.pallas_call(
        body, name="ada_bwd", out_shape=jax.ShapeDtypeStruct((c_all.shape[1], dmod_sh.shape[1]), F32),
        in_specs=[VMEM_SPEC] * 2, out_specs=VMEM_SPEC, compiler_params=_cp(0, 40),
    )(c_all, dmod_sh)


def _vec(tm_unused=None):
    return pl.BlockSpec((1, D), lambda i: (0, 0))


def _rows(tm, width=D):
    return pl.BlockSpec((tm, width), lambda i: (i, 0))


def _norm_mod_mm(x, g, shift, scale, w, tm=256, carried=()):
    S = x.shape[0]
    tm = min(tm, S)
    N = w.shape[1]

    def body(x_ref, g_ref, sh_ref, sc_ref, w_ref, h_ref, p_ref):
        xv = x_ref[...]
        r = lax.rsqrt(jnp.mean(xv * xv, axis=-1, keepdims=True) + EPS)
        hn = (xv * r) * g_ref[...]
        h = (hn * (1.0 + sc_ref[...]) + sh_ref[...]).astype(BF16)
        h_ref[...] = h
        p_ref[...] = _dot(h, w_ref[...]).astype(BF16)

    return _call(
        body, carried, [x, g, shift, scale, w], name="norm1_mod_mm_in", grid=(S // tm,),
        in_specs=[_rows(tm), _vec(), _vec(), _vec(), pl.BlockSpec(w.shape, lambda i: (0, 0), pipeline_mode=ONE_BUF)],
        out_specs=[_rows(tm), _rows(tm, N)],
        out_shape=[jax.ShapeDtypeStruct((S, D), BF16), jax.ShapeDtypeStruct((S, N), BF16)],
        compiler_params=_cp(1, 48))


def _mm_rows(pairs, tm):
    ops, specs = [], []
    for a, w in pairs:
        ops += [a, w]
        specs += [pl.BlockSpec((tm, a.shape[1]), lambda i: (i, 0)),
                  pl.BlockSpec(w.shape, lambda i: (0, 0), pipeline_mode=ONE_BUF)]
    return ops, specs


def _mm_rows_value(refs, npairs):
    acc = _dot(refs[0][...], refs[1][...])
    for t in range(1, npairs):
        acc = acc + _dot(refs[2 * t][...], refs[2 * t + 1][...])
    return acc


def _resid_norm_mod(x, mm, gate, g, shift, scale, tm=256):
    S = x.shape[0]
    tm = min(tm, S)
    skip = 2 * len(mm)

    def body(*refs):
        x_ref, gt_ref, g_ref, sh_ref, sc_ref, x1_ref, h_ref, m_ref = refs[skip:]
        mixed = _mm_rows_value(refs, len(mm))
        m_ref[...] = mixed
        x1 = x_ref[...] + (1.0 + gt_ref[...]) * mixed
        x1_ref[...] = x1
        r = lax.rsqrt(jnp.mean(x1 * x1, axis=-1, keepdims=True) + EPS)
        hn = (x1 * r) * g_ref[...]
        h_ref[...] = (hn * (1.0 + sc_ref[...]) + sh_ref[...]).astype(BF16)

    ops, specs = _mm_rows(mm, tm)
    return pl.pallas_call(
        body, name="mm_out_resid_norm2_mod", grid=(S // tm,),
        in_specs=specs + [_rows(tm), _vec(), _vec(), _vec(), _vec()],
        out_specs=[_rows(tm), _rows(tm), _rows(tm)],
        out_shape=[jax.ShapeDtypeStruct((S, D), F32), jax.ShapeDtypeStruct((S, D), BF16),
                   jax.ShapeDtypeStruct((S, D), F32)],
        compiler_params=_cp(1, 40),
    )(*ops, x, gate, g, shift, scale)


def _conv3(ext, w_ref, b_ref, cs):
    e1 = pltpu.roll(ext, 1, 0)
    e2 = pltpu.roll(ext, 2, 0)
    u = b_ref[:, cs] + w_ref[0:1, cs] * e2
    u = u + w_ref[1:2, cs] * e1
    u = u + w_ref[2:3, cs] * ext
    return u, e1, e2


def _up_conv_glu(h2, w_up_p, wc_p, bc_p, tm=256):
    S = h2.shape[0]
    tm = min(tm, S)
    widths = [2 * LANES] * (HB // (2 * LANES)) + ([LANES] if HB % (2 * LANES) else [])

    def body(h_ref, wu_ref, w_ref, b_ref, u_ref, a_ref, prev_ref):
        @pl.when(pl.program_id(0) == 0)
        def _():
            prev_ref[...] = jnp.zeros_like(prev_ref)

        hv = h_ref[...]
        for j in range(2):
            base = 0
            for wd in widths:
                us = []
                for off in (2 * j * HB + base, 2 * j * HB + HB + base):
                    cb = _dot(hv, wu_ref[:, off:off + wd]).astype(BF16)
                    u_ref[:, off:off + wd] = cb
                    for q in range(wd // LANES):
                        cs = slice(off + q * LANES, off + (q + 1) * LANES)
                        cq = cb[:, q * LANES:(q + 1) * LANES]
                        ext = jnp.concatenate([prev_ref[:, cs].astype(F32), cq.astype(F32)], axis=0)
                        us.append(_conv3(ext, w_ref, b_ref, cs)[0][16:])
                        prev_ref[:, cs] = cq[tm - 16:]
                nq = wd // LANES
                for q in range(nq):
                    val, gt = us[q], us[nq + q]
                    a_ref[:, j * HB + base + q * LANES:j * HB + base + (q + 1) * LANES] = (
                        val * (gt * _sigmoid_fast(gt))).astype(BF16)
                base += wd

    return pl.pallas_call(
        body, name="mm_up_conv_glu", grid=(S // tm,),
        in_specs=[pl.BlockSpec((tm, D), lambda i: (i, 0)),
                  pl.BlockSpec((D, 2 * D_FF), lambda i: (0, 0), pipeline_mode=ONE_BUF),
                  pl.BlockSpec((3, 2 * D_FF), lambda i: (0, 0)),
                  pl.BlockSpec((1, 2 * D_FF), lambda i: (0, 0))],
        out_specs=[pl.BlockSpec((tm, 2 * D_FF), lambda i: (i, 0)), pl.BlockSpec((tm, D_FF), lambda i: (i, 0))],
        out_shape=[jax.ShapeDtypeStruct((S, 2 * D_FF), BF16), jax.ShapeDtypeStruct((S, D_FF), BF16)],
        scratch_shapes=[pltpu.VMEM((16, 2 * D_FF), BF16)],
        compiler_params=_cp(1, 48),
    )(h2, w_up_p, wc_p, bc_p)


def _conv_glu_up_norm2_bwd(da, u0p, wc_p, bc_p, w_up_t, x1, dx2, g, scale, mixed, gate, tm=256, carried=()):
    S = u0p.shape[0]
    tm = min(tm, S)
    hb = tm // 16
    nlast = S // 16 - 1
    widths = [2 * LANES] * (HB // (2 * LANES)) + ([LANES] if HB % (2 * LANES) else [])

    def body(da_ref, dan_ref, u_ref, p_ref, n_ref, w_ref, b_ref, wt_ref, x_ref, dr_ref, g_ref, sc_ref, m_ref, gt_ref,
             o_ref, s_ref, dx_ref, dm_ref, s2_ref, acc_ref):
        i = pl.program_id(0)
        first = i == 0
        last = i == pl.num_programs(0) - 1

        @pl.when(first)
        def _():
            s_ref[...] = jnp.zeros_like(s_ref)
            s2_ref[...] = jnp.zeros_like(s2_ref)

        n = tm + 16
        started = False
        for j in range(2):
            base = 0
            for wd in widths:
                du0s = ([], [])
                for q in range(wd // LANES):
                    k0 = base + q * LANES
                    kc = slice(j * HB + k0, j * HB + k0 + LANES)
                    dae = jnp.concatenate([da_ref[:, kc].astype(F32),
                                           jnp.where(last, 0.0, dan_ref[:, kc].astype(F32))], axis=0)
                    halves = []
                    for off in (2 * j * HB + k0, 2 * j * HB + HB + k0):
                        cs = slice(off, off + LANES)
                        ext = jnp.concatenate([jnp.where(first, 0.0, p_ref[:, cs].astype(F32)),
                                               u_ref[:, cs].astype(F32), n_ref[:, cs].astype(F32)], axis=0)
                        u, e1, e2 = _conv3(ext, w_ref, b_ref, cs)
                        halves.append((u[16:], ext[16:16 + tm], e1[16:16 + tm], e2[16:16 + tm], cs))
                    val, gt = halves[0][0], halves[1][0]
                    sg = _sigmoid_fast(gt)
                    dus = (dae * (gt * sg), dae * val * (sg * (1.0 + gt * (1.0 - sg))))
                    for t, (du, (_, x0, x1_, x2, cs)) in enumerate(zip(dus, halves)):
                        du0 = (w_ref[2:3, cs] * du + w_ref[1:2, cs] * pltpu.roll(du, n - 1, 0)
                               + w_ref[0:1, cs] * pltpu.roll(du, n - 2, 0))[:tm].astype(BF16)
                        o_ref[:, cs] = du0
                        du0s[t].append(du0)
                        dut = du[:tm]
                        s_ref[0:1, cs] += jnp.sum(dut, axis=0, keepdims=True)
                        s_ref[1:2, cs] += jnp.sum(dut * x2, axis=0, keepdims=True)
                        s_ref[2:3, cs] += jnp.sum(dut * x1_, axis=0, keepdims=True)
                        s_ref[3:4, cs] += jnp.sum(dut * x0, axis=0, keepdims=True)
                for t, off in enumerate((2 * j * HB + base, 2 * j * HB + HB + base)):
                    lhs = du0s[t][0] if len(du0s[t]) == 1 else jnp.concatenate(du0s[t], axis=1)
                    part = _dot(lhs, wt_ref[off:off + wd, :])
                    if started:
                        acc_ref[...] += part
                    else:
                        acc_ref[...] = part
                        started = True
                base += wd

        xv = x_ref[...]
        dhv = acc_ref[...]
        r = lax.rsqrt(jnp.mean(xv * xv, axis=-1, keepdims=True) + EPS)
        nv = xv * r
        gv = g_ref[...]
        hn = nv * gv
        dhn = dhv * (1.0 + sc_ref[...])
        dn = dhn * gv
        dx = dr_ref[...] + r * (dn - nv * jnp.mean(dn * nv, axis=-1, keepdims=True))
        dx_ref[...] = dx
        dm_ref[...] = (dx * (1.0 + gt_ref[...])).astype(BF16)
        s2_ref[0:1, :] += jnp.sum(dhv, axis=0, keepdims=True)
        s2_ref[1:2, :] += jnp.sum(dhv * hn, axis=0, keepdims=True)
        s2_ref[2:3, :] += jnp.sum(dhn * nv, axis=0, keepdims=True)
        s2_ref[3:4, :] += jnp.sum(dx * m_ref[...], axis=0, keepdims=True)

    full = 2 * D_FF
    return _call(
        body, carried, [da, da, u0p, u0p, u0p, wc_p, bc_p, w_up_t, x1, dx2, g, scale, mixed, gate],
        name="conv_glu_up_norm2_bwd", grid=(S // tm,),
        in_specs=[pl.BlockSpec((tm, D_FF), lambda i: (i, 0)),
                  pl.BlockSpec((16, D_FF), lambda i: (jnp.minimum((i + 1) * hb, nlast), 0)),
                  pl.BlockSpec((tm, full), lambda i: (i, 0)),
                  pl.BlockSpec((16, full), lambda i: (jnp.maximum(i * hb - 1, 0), 0)),
                  pl.BlockSpec((16, full), lambda i: (jnp.minimum((i + 1) * hb, nlast), 0)),
                  pl.BlockSpec((3, full), lambda i: (0, 0)),
                  pl.BlockSpec((1, full), lambda i: (0, 0)),
                  pl.BlockSpec((full, D), lambda i: (0, 0), pipeline_mode=ONE_BUF),
                  _rows(tm), _rows(tm), _vec(), _vec(), _rows(tm), _vec()],
        out_specs=[pl.BlockSpec((tm, full), lambda i: (i, 0)), pl.BlockSpec((8, full), lambda i: (0, 0)),
                   _rows(tm), _rows(tm), pl.BlockSpec((8, D), lambda i: (0, 0))],
        out_shape=[jax.ShapeDtypeStruct((S, full), BF16), jax.ShapeDtypeStruct((8, full), F32),
                   jax.ShapeDtypeStruct((S, D), F32), jax.ShapeDtypeStruct((S, D), BF16),
                   jax.ShapeDtypeStruct((8, D), F32)],
        scratch_shapes=[pltpu.VMEM((tm, D), F32)],
        compiler_params=_cp(1, 56))


def _final_loss(x1, mm, gate2, g_final, target, tm=256):
    S = x1.shape[0]
    tm = min(tm, S)
    skip = 2 * len(mm)

    def body(*refs):
        x1_ref, gt_ref, g_ref, t_ref, dx_ref, dy_ref, s_ref = refs[skip:]

        @pl.when(pl.program_id(0) == 0)
        def _():
            s_ref[...] = jnp.zeros_like(s_ref)

        y2 = _mm_rows_value(refs, len(mm))
        og = 1.0 + gt_ref[...]
        x2 = x1_ref[...] + og * y2
        r = lax.rsqrt(jnp.mean(x2 * x2, axis=-1, keepdims=True) + EPS)
        n = x2 * r
        g = g_ref[...]
        err = n * g - t_ref[...]
        dy = err * (1.0 / D)
        dn = dy * g
        dx2 = r * (dn - n * jnp.mean(dn * n, axis=-1, keepdims=True))
        dx_ref[...] = dx2
        dy_ref[...] = (dx2 * og).astype(BF16)
        s_ref[0:1, :] += jnp.sum(dy * n, axis=0, keepdims=True)
        s_ref[1:2, :] += jnp.sum(dx2 * y2, axis=0, keepdims=True)
        s_ref[2:3, :] += jnp.sum(err * err, axis=0, keepdims=True)

    ops, specs = _mm_rows(mm, tm)
    return pl.pallas_call(
        body, name="mm_down_final_loss", grid=(S // tm,),
        in_specs=specs + [_rows(tm), _vec(), _vec(), _rows(tm)],
        out_specs=[_rows(tm), _rows(tm), pl.BlockSpec((8, D), lambda i: (0, 0))],
        out_shape=[jax.ShapeDtypeStruct((S, D), F32), jax.ShapeDtypeStruct((S, D), BF16),
                   jax.ShapeDtypeStruct((8, D), F32)],
        compiler_params=_cp(1, 40),
    )(*ops, x1, gate2, g_final, target)


def _norm_mod_bwd(dh, xin, dres, g, scale, mixed, gate, name, tm=256, carried=()):
    S = xin.shape[0]
    tm = min(tm, S)
    with_gate = mixed is not None
    fused = isinstance(dh, list)
    skip = 2 * len(dh) if fused else 1

    def body(*refs):
        if with_gate:
            x_ref, dr_ref, g_ref, sc_ref, m_ref, gt_ref, dx_ref, dm_ref, s_ref = refs[skip:]
        else:
            x_ref, dr_ref, g_ref, sc_ref, dx_ref, s_ref = refs[skip:]

        @pl.when(pl.program_id(0) == 0)
        def _():
            s_ref[...] = jnp.zeros_like(s_ref)

        xv = x_ref[...]
        dhv = _mm_rows_value(refs, len(dh)) if fused else refs[0][...]
        r = lax.rsqrt(jnp.mean(xv * xv, axis=-1, keepdims=True) + EPS)
        n = xv * r
        g = g_ref[...]
        hn = n * g
        dhn = dhv * (1.0 + sc_ref[...])
        dn = dhn * g
        dx = dr_ref[...] + r * (dn - n * jnp.mean(dn * n, axis=-1, keepdims=True))
        dx_ref[...] = dx
        s_ref[0:1, :] += jnp.sum(dhv, axis=0, keepdims=True)
        s_ref[1:2, :] += jnp.sum(dhv * hn, axis=0, keepdims=True)
        s_ref[2:3, :] += jnp.sum(dhn * n, axis=0, keepdims=True)
        if with_gate:
            dm_ref[...] = (dx * (1.0 + gt_ref[...])).astype(BF16)
            s_ref[3:4, :] += jnp.sum(dx * m_ref[...], axis=0, keepdims=True)

    ins, in_specs = _mm_rows(dh, tm) if fused else ([dh], [_rows(tm)])
    ins += [xin, dres, g, scale]
    in_specs += [_rows(tm), _rows(tm), _vec(), _vec()]
    out_specs = [_rows(tm)]
    out_shape = [jax.ShapeDtypeStruct((S, D), F32)]
    if with_gate:
        ins += [mixed, gate]
        in_specs += [_rows(tm), _vec()]
        out_specs.append(_rows(tm))
        out_shape.append(jax.ShapeDtypeStruct((S, D), BF16))
    out_specs.append(pl.BlockSpec((8, D), lambda i: (0, 0)))
    out_shape.append(jax.ShapeDtypeStruct((8, D), F32))
    return _call(body, carried, ins, name=name, grid=(S // tm,), in_specs=in_specs, out_specs=out_specs,
                 out_shape=out_shape, compiler_params=_cp(1, 48 if fused else None))


def _tri(n, rel):
    row = lax.broadcasted_iota(jnp.int32, (n, n), 0)
    col = lax.broadcasted_iota(jnp.int32, (n, n), 1)
    return {"gt": row > col, "ge": row >= col, "lt": row < col, "le": row <= col}[rel]


def _pair_diag(mask):
    u = jnp.where(mask, 1.0, 0.0).astype(BF16)
    z = jnp.zeros_like(u)
    return jnp.concatenate([jnp.concatenate([u, z], axis=1), jnp.concatenate([z, u], axis=1)], axis=0)


def _pair_rows(xp, lo_half):
    z = jnp.zeros_like(xp)
    return jnp.concatenate([jnp.where(lo_half, xp, z), jnp.where(lo_half, z, xp)], axis=0)


def _sb_scores(z, causal, diag):
    ls, ps, es = [], [], []
    for hh in range(2):
        zz = z[:, hh * QB:(hh + 1) * QB]
        e = jnp.exp(-jnp.abs(zz))
        l = -(jnp.maximum(zz, 0.0) + jnp.log(1.0 + e))
        ps.append(l + zz)
        ls.append(jnp.where(causal, l, 0.0) if diag else l)
        es.append(e)
    return ls, ps, es


def _sb_fwd(proj, carried=()):
    S = proj.shape[0]
    nq = S // QB

    def body(q_ref, k_ref, v_ref, o_ref, t_ref, c_ref, acc_ref, qs_ref):
        i = pl.program_id(0)
        causal = _tri(QB, "gt")
        usuf = _pair_diag(_tri(QB, "gt"))
        lo_half = lax.broadcasted_iota(jnp.int32, (QB, 128), 1) < DK
        qs_ref[...] = q_ref[...] * 0.125

        def block(j, diag, nr):
            rows = pl.ds(pl.multiple_of(j * QB, QB), QB)
            rs = slice(0, nr)
            pairs = range(H_SB // 2)
            cols = [slice(pr * 128, (pr + 1) * 128) for pr in pairs]
            zs = [_dot_nt(qs_ref[rs, cols[pr]], _pair_rows(k_ref[rows, cols[pr]], lo_half)) for pr in pairs]
            sc = [_sb_scores(zs[pr], causal, diag) for pr in pairs]
            sufs = []
            for pr in pairs:
                lh, ll = _split(jnp.concatenate(sc[pr][0], axis=1))
                sufs.append(_dot(lh, usuf) + _dot(ll, usuf))
            cmax = None
            wps = []
            for pr in pairs:
                ws = []
                for hh in range(2):
                    h = 2 * pr + hh
                    b = sufs[pr][:, hh * QB:(hh + 1) * QB]
                    if not diag:
                        b = b + c_ref[h, rs, 0:1]
                    w = jnp.exp(sc[pr][1][hh] + b)
                    ws.append((jnp.where(causal, w, 0.0) if diag else w).astype(BF16))
                    cn = b[:, 0:1] + sc[pr][0][hh][:, 0:1]
                    c_ref[h, rs, 0:1] = cn
                    cmax = cn if cmax is None else jnp.maximum(cmax, cn)
                wps.append(jnp.concatenate(ws, axis=1))
            for pr in pairs:
                upd = _dot(wps[pr], _pair_rows(v_ref[rows, cols[pr]], lo_half))
                if diag:
                    acc_ref[rs, cols[pr]] = upd
                else:
                    acc_ref[rs, cols[pr]] += upd
            lo = jnp.max(cmax[:SB_HEAD_ROWS])
            return (jnp.max(cmax[SB_HEAD_ROWS:]) if nr > SB_HEAD_ROWS else None), lo

        def cond_full(st):
            return jnp.logical_and(st[0] >= 0, st[1] > SB_SKIP)

        def step_full(st):
            return (st[0] - 1,) + block(st[0], False, QB)

        def cond_head(st):
            return jnp.logical_and(st[0] >= 0, st[1] > SB_SKIP)

        def step_head(st):
            return st[0] - 1, block(st[0], False, SB_HEAD_ROWS)[1]

        j, _, lo = lax.while_loop(cond_full, step_full, (i - 1,) + block(i, True, QB))
        jfull = j + 1
        j, _ = lax.while_loop(cond_head, step_head, (j, lo))
        o_ref[...] = acc_ref[...].astype(BF16)
        t_ref[...] = jnp.zeros_like(t_ref)
        for h in range(H_SB):
            t_ref[h // 4, :, h % 4:h % 4 + 1] = c_ref[h, :, 0:1]
        t_ref[:, :, 8:9] = jnp.zeros((2, QB, 1), F32) + (j + 1).astype(F32)
        t_ref[:, :, 9:10] = jnp.zeros((2, QB, 1), F32) + jfull.astype(F32)

    return _call(
        body, carried, [proj, proj, proj], name="sb_fwd", grid=(nq,),
        in_specs=[pl.BlockSpec((QB, 512), lambda i: (i, 0)),
                  pl.BlockSpec((S, 512), lambda i: (0, 1), pipeline_mode=ONE_BUF),
                  pl.BlockSpec((S, 512), lambda i: (0, 2), pipeline_mode=ONE_BUF)],
        out_specs=[pl.BlockSpec((QB, 512), lambda i: (i, 0)),
                   pl.BlockSpec((2, QB, 128), lambda i: (0, i, 0))],
        out_shape=[jax.ShapeDtypeStruct((S, 512), BF16), jax.ShapeDtypeStruct((2, S, 128), F32)],
        scratch_shapes=[pltpu.VMEM((H_SB, QB, 128), F32), pltpu.VMEM((QB, 512), F32), pltpu.VMEM((QB, 512), BF16)],
        compiler_params=_cp(1, 40))


def _sb_bwd(proj, dcat, stats, carried=()):
    S = proj.shape[0]
    nq = S // QB

    def body(q_ref, k_ref, v_ref, do_ref, t_ref, dq_ref, dk_ref, dv_ref, dk_acc, dv_acc, dq_acc, pc_ref, qs_ref,
             qt_ref, dot_ref):
        i = pl.program_id(1)

        @pl.when(i == 0)
        def _():
            dk_acc[...] = jnp.zeros_like(dk_acc)
            dv_acc[...] = jnp.zeros_like(dv_acc)

        causal = _tri(QB, "gt")
        uin = _pair_diag(_tri(QB, "le"))
        uex = _pair_diag(_tri(QB, "lt"))
        lo_half = lax.broadcasted_iota(jnp.int32, (QB, 128), 1) < DK
        qs_ref[...] = q_ref[...] * 0.125
        lo_rows = lax.broadcasted_iota(jnp.int32, (128, QB), 0) < DK
        for pr in range(2):
            qt_ref[pr] = (q_ref[:, pr * 128:(pr + 1) * 128].astype(F32) * 0.125).T.astype(BF16)
            dot_ref[pr] = do_ref[:, pr * 128:(pr + 1) * 128].astype(F32).T.astype(BF16)
        pc_ref[...] = jnp.zeros_like(pc_ref)
        dq_acc[...] = jnp.zeros_like(dq_acc)
        jstart = jnp.max(t_ref[:, 8:9]).astype(jnp.int32)
        jfull = jnp.max(t_ref[:, 9:10]).astype(jnp.int32)

        def block(j, diag, nr):
            rows = pl.ds(pl.multiple_of(j * QB, QB), QB)
            rs = slice(0, nr)
            pairs = range(2)
            cols = [slice(pr * 128, (pr + 1) * 128) for pr in pairs]
            kbds = [_pair_rows(k_ref[rows, cols[pr]], lo_half) for pr in pairs]
            zs = [_dot_nt(qs_ref[rs, cols[pr]], kbds[pr]) for pr in pairs]
            dws = [_dot_nt(do_ref[rs, cols[pr]], _pair_rows(v_ref[rows, cols[pr]], lo_half)) for pr in pairs]
            sc = [_sb_scores(zs[pr], causal, diag) for pr in pairs]
            plins = []
            for pr in pairs:
                lh, ll = _split(jnp.concatenate(sc[pr][0], axis=1))
                plins.append(_dot(lh, uin) + _dot(ll, uin))
            wss, gss, gexs = [], [], []
            for pr in pairs:
                ws, gs = [], []
                for hh in range(2):
                    h = 2 * pr + hh
                    half = slice(hh * QB, (hh + 1) * QB)
                    b = (t_ref[rs, h:h + 1] - pc_ref[h, rs, 0:1]) - plins[pr][:, half]
                    w = jnp.exp(sc[pr][1][hh] + b)
                    if diag:
                        w = jnp.where(causal, w, 0.0)
                    ws.append(w)
                    gs.append(dws[pr][:, half] * w)
                wss.append(ws)
                gss.append(gs)
            for pr in pairs:
                gh, gl = _split(jnp.concatenate(gss[pr], axis=1))
                gexs.append(_dot(gh, uex) + _dot(gl, uex))
            dzbs = []
            for pr in pairs:
                dzs = []
                for hh in range(2):
                    h = 2 * pr + hh
                    half = slice(hh * QB, (hh + 1) * QB)
                    e = sc[pr][2][hh]
                    r = pl.reciprocal(1.0 + e, approx=True)
                    er = e * r
                    pos = zs[pr][:, half] >= 0.0
                    gx = gexs[pr][:, half]
                    g = gss[pr][hh]
                    dz = g * jnp.where(pos, er, r) - (gx + pc_ref[4 + h, rs, 0:1]) * jnp.where(pos, r, er)
                    dzs.append(jnp.where(causal, dz, 0.0) if diag else dz)
                    pc_ref[h, rs, 0:1] += plins[pr][:, half][:, QB - 1:QB]
                    pc_ref[4 + h, rs, 0:1] += gx[:, QB - 1:QB] + g[:, QB - 1:QB]
                dzbs.append(jnp.concatenate(dzs, axis=1).astype(BF16))
            for pr in pairs:
                dq_acc[rs, cols[pr]] += _dot(dzbs[pr], kbds[pr])
                r1 = _dot(qt_ref[pr, :, rs], dzbs[pr])
                dk_acc[pr, j] += jnp.where(lo_rows, r1[:, :QB], r1[:, QB:])
                r2 = _dot(dot_ref[pr, :, rs], jnp.concatenate(wss[pr], axis=1).astype(BF16))
                dv_acc[pr, j] += jnp.where(lo_rows, r2[:, :QB], r2[:, QB:])

        def step_head(j, carry):
            block(j, False, SB_HEAD_ROWS)
            return carry

        def step_full(j, carry):
            block(j, False, QB)
            return carry

        lax.fori_loop(jstart, jfull, step_head, 0)
        lax.fori_loop(jfull, i, step_full, 0)
        block(i, True, QB)
        dq_ref[...] = (dq_acc[...] * 0.125).astype(BF16)

        @pl.when(i == nq - 1)
        def _():
            def put(jj, carry):
                krows = pl.ds(pl.multiple_of(jj * QB, QB), QB)
                for pr in range(2):
                    dk_ref[krows, pr * 128:(pr + 1) * 128] = dk_acc[pr, jj].T.astype(BF16)
                    dv_ref[krows, pr * 128:(pr + 1) * 128] = dv_acc[pr, jj].T.astype(BF16)
                return carry
            lax.fori_loop(0, nq, put, 0)

    return _call(
        body, carried, [proj, proj, proj, dcat, stats], name="sb_bwd", grid=(2, nq),
        in_specs=[pl.BlockSpec((QB, 256), lambda g, i: (i, g)),
                  pl.BlockSpec((S, 256), lambda g, i: (0, 2 + g), pipeline_mode=ONE_BUF),
                  pl.BlockSpec((S, 256), lambda g, i: (0, 4 + g), pipeline_mode=ONE_BUF),
                  pl.BlockSpec((QB, 256), lambda g, i: (i, g)),
                  pl.BlockSpec((None, QB, 128), lambda g, i: (g, i, 0))],
        out_specs=[pl.BlockSpec((QB, 256), lambda g, i: (i, g)),
                   pl.BlockSpec((S, 256), lambda g, i: (0, g)),
                   pl.BlockSpec((S, 256), lambda g, i: (0, g))],
        out_shape=[jax.ShapeDtypeStruct((S, 512), BF16)] * 3,
        scratch_shapes=[pltpu.VMEM((2, nq, 128, QB), F32), pltpu.VMEM((2, nq, 128, QB), F32),
                        pltpu.VMEM((QB, 256), F32), pltpu.VMEM((8, QB, 128), F32), pltpu.VMEM((QB, 256), BF16),
                        pltpu.VMEM((2, 128, QB), BF16), pltpu.VMEM((2, 128, QB), BF16)],
        compiler_params=_cp(2, 56))


GLA_NC = 4
GLA_R = GLA_NC * CHUNK


def _chunk_tri(strict):
    row = lax.broadcasted_iota(jnp.int32, (GLA_R, GLA_R), 0)
    col = lax.broadcasted_iota(jnp.int32, (GLA_R, GLA_R), 1)
    m = jnp.logical_and(row // CHUNK == col // CHUNK, row > col if strict else row >= col)
    u = jnp.where(m, 1.0, 0.0).astype(BF16)
    return jnp.concatenate([u, u], axis=1)


def _per_chunk_rows(vals):
    return jnp.concatenate([jnp.broadcast_to(v, (CHUNK, v.shape[1])) for v in vals], axis=0)


def _head_blocks(st):
    row = lax.broadcasted_iota(jnp.int32, (H_GLA * DV, H_GLA * DK), 0)
    col = lax.broadcasted_iota(jnp.int32, (H_GLA * DV, H_GLA * DK), 1)
    t = jnp.concatenate([st.astype(BF16)] * H_GLA, axis=0)
    return jnp.where(row // DV == col // DK, t, jnp.zeros_like(t))


def _head_diag(big):
    head = lax.broadcasted_iota(jnp.int32, (DV, H_GLA * DK), 1) // DK
    out = big[0:DV]
    for h in range(1, H_GLA):
        out = jnp.where(head == h, big[h * DV:(h + 1) * DV], out)
    return out


def _gla_gate4(gf_ref, wfg_ref, bfg_ref):
    f = _dot(gf_ref[...], wfg_ref[...]) + bfg_ref[...]
    _, la, _ = _log_sigmoid_parts(f)
    lah, lal = _split(la * (1.0 / 16.0))
    cum = _dot(_chunk_tri(False), jnp.concatenate([lah, lal], axis=0))
    tots = [cum[(c + 1) * CHUNK - 1:(c + 1) * CHUNK, :] for c in range(GLA_NC)]
    return f, jnp.exp(_per_chunk_rows(tots) - cum), [jnp.exp(t) for t in tots]


def _gla_specs4(ns, rev):
    def ix(i):
        return ns - 1 - i if rev else i
    return [pl.BlockSpec((GLA_R, 256), lambda i: (ix(i), 6)),
            pl.BlockSpec((GLA_R, 256), lambda i: (ix(i), 7)),
            pl.BlockSpec((GLA_R, 512), lambda i: (ix(i), 4)),
            pl.BlockSpec((GLA_R, 512), lambda i: (ix(i), 5)),
            pl.BlockSpec((GLA_R, 128), lambda i: (ix(i), 24))]


def _gla_fwd(proj, wfg_p, bfg, ggla, carried=()):
    S = proj.shape[0]
    ns = S // GLA_R

    def body(q_ref, k_ref, v_ref, gg_ref, gf_ref, wfg_ref, bfg_ref, ggla_ref, o_ref, st_ref, state):
        @pl.when(pl.program_id(0) == 0)
        def _():
            state[...] = jnp.zeros_like(state)

        _, e, decs = _gla_gate4(gf_ref, wfg_ref, bfg_ref)
        kdec = (k_ref[...].astype(F32) * e).astype(BF16)
        rows = [slice(c * CHUNK, (c + 1) * CHUNK) for c in range(GLA_NC)]
        kvs = [_head_diag(_dot_tn(v_ref[rows[c], :], kdec[rows[c]])) for c in range(GLA_NC)]
        st = state[...]
        sts = []
        for c in range(GLA_NC):
            st = st * decs[c] + kvs[c]
            st_ref[c] = st
            sts.append(st)
        state[...] = st
        o = jnp.concatenate([_dot_nt(q_ref[rows[c], :] * 0.125, _head_blocks(sts[c])) for c in range(GLA_NC)], axis=0)
        for h in range(H_GLA):
            vs = slice(h * DV, (h + 1) * DV)
            oh = o[:, vs]
            ohn = oh * lax.rsqrt(jnp.mean(oh * oh, axis=-1, keepdims=True) + EPS)
            gg = gg_ref[:, vs].astype(F32)
            o_ref[:, vs] = ((ohn * ggla_ref[:, vs]) * (gg * _sigmoid(gg))).astype(BF16)

    return _call(
        body, carried, [proj, proj, proj, proj, proj, wfg_p, bfg, ggla], name="gla_fwd", grid=(ns,),
        in_specs=_gla_specs4(ns, False) + [pl.BlockSpec((128, 256), lambda i: (0, 0)),
                                           pl.BlockSpec((1, 256), lambda i: (0, 0)),
                                           pl.BlockSpec((1, 512), lambda i: (0, 0))],
        out_specs=[pl.BlockSpec((GLA_R, 512), lambda i: (i, 0)),
                   pl.BlockSpec((GLA_NC, 128, 256), lambda i: (i, 0, 0))],
        out_shape=[jax.ShapeDtypeStruct((S, 512), BF16), jax.ShapeDtypeStruct((S // CHUNK, 128, 256), F32)],
        scratch_shapes=[pltpu.VMEM((128, 256), F32)],
        compiler_params=_cp(1))


def _gla_bwd(dcat, proj, states, wfg_p, bfg, ggla, carried=()):
    S = proj.shape[0]
    ns = S // GLA_R

    def body(do_ref, q_ref, k_ref, v_ref, gg_ref, gf_ref, sc_ref, sp_ref, wfg_ref, bfg_ref, ggla_ref,
             dp_ref, s_ref, dw_ref, carry):
        sr = pl.program_id(0)

        @pl.when(sr == 0)
        def _():
            carry[...] = jnp.zeros_like(carry)
            s_ref[...] = jnp.zeros_like(s_ref)
            dw_ref[...] = jnp.zeros_like(dw_ref)

        f, e, decs = _gla_gate4(gf_ref, wfg_ref, bfg_ref)
        kf = k_ref[...].astype(F32) * e
        kdec = kf.astype(BF16)
        rows = [slice(c * CHUNK, (c + 1) * CHUNK) for c in range(GLA_NC)]
        sts = [sc_ref[c] for c in range(GLA_NC)]
        st_before = jnp.where(sr < ns - 1, sp_ref[0], 0.0)
        sbd = [_head_blocks(sts[c]) for c in range(GLA_NC)]
        qs = q_ref[...] * 0.125
        o = jnp.concatenate([_dot_nt(qs[rows[c]], sbd[c]) for c in range(GLA_NC)], axis=0)
        dobs = []
        for h in range(H_GLA):
            vs = slice(h * DV, (h + 1) * DV)
            oh = o[:, vs]
            rr = lax.rsqrt(jnp.mean(oh * oh, axis=-1, keepdims=True) + EPS)
            ohn = oh * rr
            gg = gg_ref[:, vs].astype(F32)
            sg = _sigmoid(gg)
            dout = do_ref[:, vs].astype(F32)
            gl = ggla_ref[:, vs]
            dp_ref[:, 1024 + h * DV:1024 + (h + 1) * DV] = (
                dout * (ohn * gl) * (sg * (1.0 + gg * (1.0 - sg)))).astype(BF16)
            dt1 = dout * (gg * sg)
            s_ref[0:1, vs] += jnp.sum(dt1 * ohn, axis=0, keepdims=True)
            dohn = dt1 * gl
            dobs.append((rr * (dohn - ohn * jnp.mean(dohn * ohn, axis=-1, keepdims=True))).astype(BF16))
        dob = jnp.concatenate(dobs, axis=1)
        dsout = []
        for c in range(GLA_NC):
            dp_ref[rows[c], 0:256] = (_dot(dob[rows[c]], sbd[c]) * 0.125).astype(BF16)
            dsout.append(_head_diag(_dot_tn(dob[rows[c]], qs[rows[c]])))
        g = carry[...]
        gts, ddecs = [None] * GLA_NC, [None] * GLA_NC
        for c in reversed(range(GLA_NC)):
            gts[c] = dsout[c] + g
            ddecs[c] = jnp.sum(gts[c] * (sts[c - 1] if c > 0 else st_before), axis=0, keepdims=True) * decs[c]
            g = gts[c] * decs[c]
        carry[...] = g
        dkds = []
        for c in range(GLA_NC):
            gbd = _head_blocks(gts[c])
            dkds.append(_dot(v_ref[rows[c], :], gbd))
            dp_ref[rows[c], 512:1024] = _dot_nt(kdec[rows[c]], gbd).astype(BF16)
        dkd = jnp.concatenate(dkds, axis=0)
        dp_ref[:, 256:512] = (dkd * e).astype(BF16)
        wh, wl = _split(dkd * kf)
        dla = _dot(_chunk_tri(True), jnp.concatenate([wh, wl], axis=0)) + _per_chunk_rows(ddecs)
        df = dla * _sigmoid(-f) * (1.0 / 16.0)
        dfb = df.astype(BF16)
        s_ref[1:2, 0:256] += jnp.sum(df, axis=0, keepdims=True)
        dw_ref[...] += _dot_tn(gf_ref[...], dfb)
        dp_ref[:, 1536:1664] = _dot_nt(dfb, wfg_ref[...]).astype(BF16)

    return _call(
        body, carried, [dcat, proj, proj, proj, proj, proj, states, states, wfg_p, bfg, ggla],
        name="gla_bwd", grid=(ns,),
        in_specs=[pl.BlockSpec((GLA_R, 512), lambda i: (ns - 1 - i, 1))] + _gla_specs4(ns, True) + [
            pl.BlockSpec((GLA_NC, 128, 256), lambda i: (ns - 1 - i, 0, 0)),
            pl.BlockSpec((1, 128, 256), lambda i: (jnp.maximum((ns - 1 - i) * GLA_NC - 1, 0), 0, 0)),
            pl.BlockSpec((128, 256), lambda i: (0, 0)),
            pl.BlockSpec((1, 256), lambda i: (0, 0)),
            pl.BlockSpec((1, 512), lambda i: (0, 0))],
        out_specs=[pl.BlockSpec((GLA_R, 1664), lambda i: (ns - 1 - i, 0)),
                   pl.BlockSpec((8, 512), lambda i: (0, 0)),
                   pl.BlockSpec((128, 256), lambda i: (0, 0))],
        out_shape=[jax.ShapeDtypeStruct((S, 1664), BF16), jax.ShapeDtypeStruct((8, 512), F32),
                   jax.ShapeDtypeStruct((128, 256), F32)],
        scratch_shapes=[pltpu.VMEM((128, 256), F32)],
        compiler_params=_cp(1))


def _sum_leading(a, name):
    n = a.shape[0]

    def body(a_ref, o_ref):
        acc = a_ref[0]
        for k in range(1, n):
            acc = acc + a_ref[k]
        o_ref[...] = acc

    return pl.pallas_call(
        body, name=name, out_shape=jax.ShapeDtypeStruct(a.shape[1:], F32),
        in_specs=[VMEM_SPEC], out_specs=VMEM_SPEC,
    )(a)


def _sum_chip(own, recv, name):
    R, C = own.shape
    tr, tc = _tile2d(R, C, 1024 * 1024)

    def body(o_ref, r_ref, p_ref):
        acc = o_ref[...]
        for k in range(3):
            acc = acc + r_ref[k].astype(F32)
        p_ref[...] = acc

    return pl.pallas_call(
        body, name=name, grid=(R // tr, C // tc),
        in_specs=[pl.BlockSpec((tr, tc), lambda i, j: (i, j)), pl.BlockSpec((3, tr, tc), lambda i, j: (0, i, j))],
        out_specs=pl.BlockSpec((tr, tc), lambda i, j: (i, j)),
        out_shape=jax.ShapeDtypeStruct((R, C), F32), compiler_params=_cp(2, 40),
    )(own, recv)


def _adamw(w, p, q, m, v, name):
    R, C = w.shape
    tr, tc = _tile2d(R, C, 1024 * 1024)
    two = q is not None

    def body(*refs):
        if two:
            w_ref, p_ref, q_ref, m_ref, v_ref, g_out, d_out, m_out, v_out = refs
            g = p_ref[...] + q_ref[...]
        else:
            w_ref, p_ref, m_ref, v_ref, g_out, d_out, m_out, v_out = refs
            g = p_ref[...]
        m2 = B1 * m_ref[...] + (1.0 - B1) * g
        v2 = B2 * v_ref[...] + (1.0 - B2) * (g * g)
        m_hat = m2 / (1.0 - B1 ** STEP)
        v_hat = v2 / (1.0 - B2 ** STEP)
        g_out[...] = g
        d_out[...] = -LR * (m_hat / (jnp.sqrt(v_hat) + EPS_A) + WD * w_ref[...])
        m_out[...] = m2
        v_out[...] = v2

    spec = pl.BlockSpec((tr, tc), lambda i, j: (i, j))
    ins = [w, p, q, m, v] if two else [w, p, m, v]
    return pl.pallas_call(
        body, name=name, grid=(R // tr, C // tc),
        in_specs=[spec] * len(ins), out_specs=[spec] * 4,
        out_shape=[jax.ShapeDtypeStruct((R, C), F32)] * 4, compiler_params=_cp(2, 40),
    )(*ins)


def _cols_to_chips(a, width):
    return a.reshape(a.shape[0], 4, width).swapaxes(0, 1)


def _chips_to_cols(a):
    return a.swapaxes(0, 1).reshape(a.shape[1], 4 * a.shape[2])


def _swap_mid(a):
    lead = a.shape[:-1]
    return a.reshape(lead + (2, 2, HB)).swapaxes(-3, -2).reshape(lead + (4 * HB,))


def kernel(x, c, w_ada, b_ada, g_norm1, w_in, w_fg2, b_fg2, g_gla_out, w_out, g_norm2, w_up, w_conv, b_conv, w_down, g_final, loss_target, m_w_ada, m_b_ada, m_g_norm1, m_w_in, m_w_fg2, m_b_fg2, m_g_gla_out, m_w_out, m_g_norm2, m_w_up, m_w_conv, m_b_conv, m_w_down, m_g_final, v_w_ada, v_b_ada, v_g_norm1, v_w_in, v_w_fg2, v_b_fg2, v_g_gla_out, v_w_out, v_g_norm2, v_w_up, v_w_conv, v_b_conv, v_w_down, v_g_final):
    xi, yi, ci = lax.axis_index("x"), lax.axis_index("y"), lax.axis_index("c")
    cidx = 2 * xi + yi
    didx = 4 * xi + 2 * yi + ci
    xs = x[0]
    tgt = loss_target[0]
    gfin = g_final.reshape(1, D)
    AW = D * 6 // 4

    c_all = _allgather8(c, "gather_c").reshape(8, D)
    c_pad = jnp.concatenate([c_all, jnp.zeros((8, D), F32)], axis=0)
    mod_part = _ada_fwd(c_pad, w_ada[0], lax.dynamic_slice(b_ada, (0, cidx * AW), (1, AW)))[:8]
    small = jnp.concatenate([mod_part.reshape(-1), w_conv.reshape(-1), w_fg2.reshape(-1)]).reshape(-1, 128)
    small_g = _allgather4(small, "gather_small").reshape(4, -1)
    mod = lax.dynamic_index_in_dim(small_g[:, :8 * AW].reshape(4, 8, AW), didx, axis=1, keepdims=False).reshape(1, 6 * D)
    shift1, scale1, gate1, shift2, scale2, gate2 = [mod[:, k * D:(k + 1) * D] for k in range(6)]
    o1 = 8 * AW
    o2 = o1 + 3 * HB
    wc_p = _swap_mid(_chips_to_cols(small_g[:, o1:o2].reshape(4, 3, HB)))
    bc_p = _swap_mid(b_conv)
    wfg_full = _chips_to_cols(small_g[:, o2:].reshape(4, RANK, 64))
    wfg_p = jnp.concatenate([wfg_full, jnp.zeros((128 - RANK, 256), F32)], axis=0).astype(BF16)

    w_in_t = _allgather4_split(w_in[0].T.astype(BF16), "gather_w_in").reshape(N_IN, D)
    w_in_t = jnp.concatenate([w_in_t, jnp.zeros((N_IN_P - N_IN, D), BF16)], axis=0)
    w_in_p = w_in_t.T

    (h, proj), (w_down_g,) = _norm_mod_mm(xs, g_norm1, shift1, scale1, w_in_p,
                                          carried=[("gather", w_down[0].astype(BF16), False)])
    w_down_f = w_down_g.reshape(D_FF, D)
    (o_gla, states), (w_out_g,) = _gla_fwd(proj, wfg_p, b_fg2, g_gla_out,
                                           carried=[("gather", w_out[0].astype(BF16), False)])
    w_out_f = w_out_g.reshape(D, D)
    (o_sb, stats), (w_up_g,) = _sb_fwd(proj, carried=[("gather", w_up[0].astype(BF16), False)])
    w_up_p = _swap_mid(_chips_to_cols(w_up_g))
    x1, h2, mixed = _resid_norm_mod(xs, [(o_sb, w_out_f[:512]), (o_gla, w_out_f[512:])],
                                    gate1, g_norm2, shift2, scale2)
    u0p, a = _up_conv_glu(h2, w_up_p, wc_p, bc_p)
    dx2, dy2, s_fin = _final_loss(x1, [(a, w_down_f)], gate2, gfin, tgt)

    da = _mm([(dy2, w_down_f.T)], BF16, "mm_down_t", 512, D_FF, 48)
    dw_down, dw_down_h = [t.reshape(4, D_FF // 4, D) for t in _mm_tn([a], dy2, "mm_dw_down", D, 1024, 60)]
    (du0p, s_conv, dx1, dmixed, s_n2), (rc_down,) = _conv_glu_up_norm2_bwd(
        da, u0p, wc_p, bc_p, w_up_p.T, x1, dx2, g_norm2, scale2, mixed, gate1,
        carried=[("scatter", dw_down_h, False)])
    dw_up, dw_up_h = _mm_tn([h2], du0p, "mm_dw_up", HB, 1024, 56)
    dcat = _mm([(dmixed, w_out_f.T)], BF16, "mm_out_t", 512, D)
    dw_out, dw_out_h = [t.reshape(4, D // 4, D) for t in _mm_tn([o_sb, o_gla], dmixed, "mm_dw_out", D, 512)]
    (dq, dk, dv), (rc_up,) = _sb_bwd(proj, dcat, stats, carried=[("scatter", dw_up_h, True)])
    (dp_gla, s_gla, dwfg), (rc_out,) = _gla_bwd(dcat, proj, states, wfg_p, b_fg2, g_gla_out,
                                                carried=[("scatter", dw_out_h, False)])
    dw_in, dw_in_h = _mm_tn([dq, dk, dv, dp_gla], h, "mm_dw_in", D, 1024, 60)
    dw_in_h = dw_in_h[0, :N_IN].reshape(4, N_IN // 4, D)
    dw_in_own = lax.dynamic_slice(dw_in[0], (cidx * (N_IN // 4), 0), (N_IN // 4, D))
    (gx, s_n1), (rc_in,) = _norm_mod_bwd(
        [(dq, w_in_t[:512]), (dk, w_in_t[512:1024]), (dv, w_in_t[1024:1536]), (dp_gla, w_in_t[1536:])],
        xs, dx1, g_norm1, scale1, None, None, "mm_in_t_norm1_bwd", carried=[("scatter", dw_in_h, False)])

    dmod = jnp.concatenate([s_n1[0], s_n1[1], s_n2[3], s_n2[0], s_n2[1], s_fin[1]])
    s_conv_n = _swap_mid(s_conv[:4])
    part = jnp.concatenate([dmod, s_n1[2], s_n2[2], s_fin[0], s_gla[0], s_gla[1, :256], s_conv_n[0],
                            s_conv_n[1:4].reshape(-1), dwfg[:RANK].reshape(-1),
                            jnp.broadcast_to(jnp.sum(s_fin[2]), (128,))]).reshape(-1, 128)
    parts = _allgather8(part, "gather_small_grads")
    tot = _sum_leading(parts, "sum_small_grads").reshape(-1)
    loss = 0.5 / D * tot[-1]
    dmod_all = parts.reshape(8, -1)[:, :6 * D]
    offs = [0]
    for n in (6 * D, D, D, D, 512, 256, 2 * D_FF, 3 * 2 * D_FF, RANK * 256):
        offs.append(offs[-1] + n)
    g_b_ada, g_g1, g_g2, g_gf, g_ggla, g_bfg, g_bconv, g_wconv_full, g_wfg_full = [
        tot[offs[k]:offs[k + 1]] for k in range(9)]
    g_wconv = lax.dynamic_index_in_dim(_cols_to_chips(g_wconv_full.reshape(3, 2 * D_FF), HB), cidx, 0, keepdims=False)
    g_wfg = lax.dynamic_index_in_dim(_cols_to_chips(g_wfg_full.reshape(RANK, 256), 64), cidx, 0, keepdims=False)

    dmod_pad = jnp.concatenate([dmod_all, jnp.zeros((8, 6 * D), F32)], axis=0)
    g_w_ada = _ada_bwd(c_pad, lax.dynamic_slice(dmod_pad, (0, cidx * AW), (16, AW)))

    def own(blocks, swapped=False):
        return lax.dynamic_index_in_dim(blocks, _slot(cidx, swapped), axis=0, keepdims=False)

    p_in = _sum_chip(dw_in_own, rc_in, "rs_w_in_sum")
    p_out = _sum_chip(own(dw_out), rc_out, "rs_w_out_sum")
    p_up = _sum_chip(own(dw_up, True), rc_up, "rs_w_up_sum")
    p_down = _sum_chip(own(dw_down), rc_down, "rs_w_down_sum")
    q_in, q_out, q_up, q_down = _pair_swap([p_in, p_out, p_up, p_down], "rs_swap")

    out = {}
    out["w_ada"] = _adamw(w_ada[0], g_w_ada, None, m_w_ada[0], v_w_ada[0], "adamw_w_ada")
    out["w_in"] = [t.T for t in _adamw(w_in[0].T, p_in, q_in, m_w_in[0].T, v_w_in[0].T, "adamw_w_in")]
    out["w_out"] = _adamw(w_out[0], p_out, q_out, m_w_out[0], v_w_out[0], "adamw_w_out")
    out["w_up"] = _adamw(w_up[0], p_up, q_up, m_w_up[0], v_w_up[0], "adamw_w_up")
    out["w_down"] = _adamw(w_down[0], p_down, q_down, m_w_down[0], v_w_down[0], "adamw_w_down")
    small_names = ["b_ada", "g_norm1", "w_fg2", "b_fg2", "g_gla_out", "g_norm2", "w_conv", "b_conv", "g_final"]
    small_w = [b_ada, g_norm1, w_fg2, b_fg2, g_gla_out, g_norm2, w_conv, b_conv, g_final]
    small_m = [m_b_ada, m_g_norm1, m_w_fg2, m_b_fg2, m_g_gla_out, m_g_norm2, m_w_conv, m_b_conv, m_g_final]
    small_v = [v_b_ada, v_g_norm1, v_w_fg2, v_b_fg2, v_g_gla_out, v_g_norm2, v_w_conv, v_b_conv, v_g_final]
    small_gr = [g_b_ada, g_g1, g_wfg, g_bfg, g_ggla, g_g2, g_wconv, g_bconv, g_gf]

    def pack(arrs):
        flat = jnp.concatenate([t.reshape(-1) for t in arrs])
        return jnp.concatenate([flat, jnp.zeros((-flat.shape[0]) % 1024, F32)]).reshape(-1, 128)

    packed = _adamw(pack(small_w), pack(small_gr), None, pack(small_m), pack(small_v), "adamw_small")
    off = 0
    for nm, wt in zip(small_names, small_w):
        n = wt.size
        out[nm] = [t.reshape(-1)[off:off + n].reshape(wt.shape) for t in packed]
        off += n
    for nm in ("w_ada", "w_in", "w_out", "w_up", "w_down"):
        out[nm] = [t[None] for t in out[nm]]

    names = ["w_ada", "b_ada", "g_norm1", "w_in", "w_fg2", "b_fg2", "g_gla_out", "w_out", "g_norm2", "w_up",
             "w_conv", "b_conv", "w_down", "g_final"]
    res = [loss, gx[None]]
    for k in range(4):
        res += [out[nm][k] for nm in names]
    return tuple(res)
```

```python
import functools

import jax
import jax.numpy as jnp
from jax import lax
from jax.experimental import pallas as pl
from jax.experimental.pallas import tpu as pltpu

F32 = jnp.float32
BF16 = jnp.bfloat16
MESH = pl.DeviceIdType.MESH

D = 1024
H_SB = 8
DK = 64
DV = 128
H_GLA = 4
CHUNK = 64
RANK = 16
N_IN = 3088
N_IN_P = 3200
D_FF = 2816
HB = D_FF // 2
LANES = 128
EPS = 1e-6
QB = 128
SB_SKIP = -120.0
SB_HEAD_ROWS = 64

LR, B1, B2, EPS_A, WD, STEP = 0.001, 0.9, 0.999, 1e-08, 0.01, 10

ANY = pl.BlockSpec(memory_space=pl.ANY)
VMEM_SPEC = pl.BlockSpec(memory_space=pltpu.VMEM)
ONE_BUF = pl.Buffered(1)


def _cp(ndim=0, vmem_mb=None):
    kw = {}
    if ndim:
        kw["dimension_semantics"] = ("arbitrary",) * ndim
    if vmem_mb:
        kw["vmem_limit_bytes"] = vmem_mb * 1024 * 1024
    return pltpu.CompilerParams(**kw)


def _dot(a, b):
    return jnp.dot(a, b, preferred_element_type=F32)


def _dot_nt(a, b):
    return lax.dot_general(a, b, (((1,), (1,)), ((), ())), preferred_element_type=F32)


def _dot_tn(a, b):
    return lax.dot_general(a, b, (((0,), (0,)), ((), ())), preferred_element_type=F32)


def _split(x):
    hi = x.astype(BF16)
    lo = (x - hi.astype(F32)).astype(BF16)
    return hi, lo


def _sigmoid(x):
    return jax.nn.sigmoid(x)


def _sigmoid_fast(x):
    return pl.reciprocal(1.0 + jnp.exp(-x), approx=True)


def _log_sigmoid_parts(z):
    e = jnp.exp(-jnp.abs(z))
    sp = jnp.log1p(e)
    return -(jnp.maximum(z, 0.0) + sp), jnp.minimum(z, 0.0) - sp, e


def _tile2d(rows, cols, budget=512 * 1024):
    best = None
    for t in range(8, rows + 1, 8):
        if rows % t == 0 and t * cols * 4 <= budget:
            best = t
    if best is not None:
        return best, cols
    best = LANES if cols % LANES == 0 else cols
    for t in range(LANES, cols + 1, LANES):
        if cols % t == 0 and rows * t * 4 <= budget:
            best = t
    return rows, best


def _flip(v, bit):
    return 1 - v if bit else v


def _allgather8(a, name):
    def body(a_ref, o_ref, ssem, rsem, lsem):
        x, y, c = lax.axis_index("x"), lax.axis_index("y"), lax.axis_index("c")
        me = 4 * x + 2 * y + c
        loc = pltpu.make_async_copy(a_ref, o_ref.at[me], lsem)
        loc.start()
        sends = []
        for r in range(1, 8):
            peer = (_flip(x, r & 4), _flip(y, r & 2), _flip(c, r & 1))
            cp = pltpu.make_async_remote_copy(
                src_ref=a_ref, dst_ref=o_ref.at[me], send_sem=ssem.at[r - 1], recv_sem=rsem.at[r - 1],
                device_id=peer, device_id_type=MESH)
            cp.start()
            sends.append(cp)
        for r in range(1, 8):
            peer = (_flip(x, r & 4), _flip(y, r & 2), _flip(c, r & 1))
            pidx = 4 * peer[0] + 2 * peer[1] + peer[2]
            pltpu.make_async_remote_copy(
                src_ref=a_ref, dst_ref=o_ref.at[pidx], send_sem=ssem.at[r - 1], recv_sem=rsem.at[r - 1],
                device_id=peer, device_id_type=MESH).wait_recv()
        for cp in sends:
            cp.wait_send()
        loc.wait()

    return pl.pallas_call(
        body, name=name,
        out_shape=jax.ShapeDtypeStruct((8,) + a.shape, a.dtype),
        in_specs=[VMEM_SPEC], out_specs=VMEM_SPEC,
        scratch_shapes=[pltpu.SemaphoreType.DMA((7,)), pltpu.SemaphoreType.DMA((7,)), pltpu.SemaphoreType.DMA],
    )(a)


def _allgather4(a, name):
    def body(a_ref, o_ref, ssem, rsem, lsem):
        x, y, c = lax.axis_index("x"), lax.axis_index("y"), lax.axis_index("c")
        me = 2 * x + y
        loc = pltpu.make_async_copy(a_ref, o_ref.at[me], lsem)
        loc.start()
        sends = []
        for r in range(1, 4):
            peer = (_flip(x, r & 2), _flip(y, r & 1), c)
            cp = pltpu.make_async_remote_copy(
                src_ref=a_ref, dst_ref=o_ref.at[me], send_sem=ssem.at[r - 1], recv_sem=rsem.at[r - 1],
                device_id=peer, device_id_type=MESH)
            cp.start()
            sends.append(cp)
        for r in range(1, 4):
            peer = (_flip(x, r & 2), _flip(y, r & 1), c)
            pidx = 2 * peer[0] + peer[1]
            pltpu.make_async_remote_copy(
                src_ref=a_ref, dst_ref=o_ref.at[pidx], send_sem=ssem.at[r - 1], recv_sem=rsem.at[r - 1],
                device_id=peer, device_id_type=MESH).wait_recv()
        for cp in sends:
            cp.wait_send()
        loc.wait()

    return pl.pallas_call(
        body, name=name,
        out_shape=jax.ShapeDtypeStruct((4,) + a.shape, a.dtype),
        in_specs=[ANY], out_specs=ANY,
        scratch_shapes=[pltpu.SemaphoreType.DMA((3,)), pltpu.SemaphoreType.DMA((3,)), pltpu.SemaphoreType.DMA],
    )(a)


def _allgather4_split(a, name):
    R, C = a.shape
    hc = C // 2

    def body(a_ref, o_ref, ssem, rsem, fssem, frsem, lsem):
        x, y, c = lax.axis_index("x"), lax.axis_index("y"), lax.axis_index("c")
        me = 2 * x + y
        sibling = (x, y, 1 - c)
        mine = pl.ds(pl.multiple_of(c * hc, hc), hc)
        theirs = pl.ds(pl.multiple_of((1 - c) * hc, hc), hc)
        loc = pltpu.make_async_copy(a_ref, o_ref.at[me], lsem)
        loc.start()
        peers = [(_flip(x, r & 2), _flip(y, r & 1), c) for r in range(1, 4)]
        pidx = [2 * p[0] + p[1] for p in peers]
        sends = []
        for k in range(3):
            cp = pltpu.make_async_remote_copy(
                src_ref=a_ref.at[:, mine], dst_ref=o_ref.at[me, :, mine], send_sem=ssem.at[k], recv_sem=rsem.at[k],
                device_id=peers[k], device_id_type=MESH)
            cp.start()
            sends.append(cp)
        for k in range(3):
            landed = o_ref.at[pidx[k], :, mine]
            pltpu.make_async_remote_copy(
                src_ref=landed, dst_ref=landed, send_sem=ssem.at[k], recv_sem=rsem.at[k],
                device_id=peers[k], device_id_type=MESH).wait_recv()
            cp = pltpu.make_async_remote_copy(
                src_ref=landed, dst_ref=landed, send_sem=fssem.at[k], recv_sem=frsem.at[k],
                device_id=sibling, device_id_type=MESH)
            cp.start()
            sends.append(cp)
        for k in range(3):
            got = o_ref.at[pidx[k], :, theirs]
            pltpu.make_async_remote_copy(
                src_ref=got, dst_ref=got, send_sem=fssem.at[k], recv_sem=frsem.at[k],
                device_id=sibling, device_id_type=MESH).wait_recv()
        for cp in sends:
            cp.wait_send()
        loc.wait()

    return pl.pallas_call(
        body, name=name,
        out_shape=jax.ShapeDtypeStruct((4,) + a.shape, a.dtype),
        in_specs=[ANY], out_specs=ANY,
        scratch_shapes=[pltpu.SemaphoreType.DMA((3,))] * 4 + [pltpu.SemaphoreType.DMA],
    )(a)


def _slot(chip, swapped):
    return 2 * (chip % 2) + chip // 2 if swapped else chip


def _pair_swap(ps, name):
    n = len(ps)

    def body(*refs):
        x, y, c = lax.axis_index("x"), lax.axis_index("y"), lax.axis_index("c")
        ssem, rsem = refs[2 * n], refs[2 * n + 1]
        cps = [pltpu.make_async_remote_copy(
            src_ref=refs[k], dst_ref=refs[n + k], send_sem=ssem.at[k], recv_sem=rsem.at[k],
            device_id=(x, y, 1 - c), device_id_type=MESH) for k in range(n)]
        for cp in cps:
            cp.start()
        for cp in cps:
            cp.wait()

    return pl.pallas_call(
        body, name=name,
        out_shape=[jax.ShapeDtypeStruct(p.shape, p.dtype) for p in ps],
        in_specs=[ANY] * n, out_specs=[ANY] * n,
        scratch_shapes=[pltpu.SemaphoreType.DMA((n,)), pltpu.SemaphoreType.DMA((n,))],
    )(*ps)


def _carried_copies(kind, src_ref, dst_ref, sems, swapped):
    ssem, rsem, lsem = sems
    x, y, c = lax.axis_index("x"), lax.axis_index("y"), lax.axis_index("c")
    me = 2 * x + y
    starts, recvs = [], []
    if kind == "gather":
        starts.append(pltpu.make_async_copy(src_ref, dst_ref.at[me], lsem))
    for r in range(1, 4):
        peer = (_flip(x, r & 2), _flip(y, r & 1), c)
        pidx = 2 * peer[0] + peer[1]
        if kind == "gather":
            src, dst, landed = src_ref, dst_ref.at[me], dst_ref.at[pidx]
        else:
            src = src_ref.at[2 * peer[1] + peer[0] if swapped else pidx]
            dst = landed = dst_ref.at[r - 1]
        starts.append(pltpu.make_async_remote_copy(
            src_ref=src, dst_ref=dst, send_sem=ssem.at[r - 1], recv_sem=rsem.at[r - 1],
            device_id=peer, device_id_type=MESH))
        recvs.append(pltpu.make_async_remote_copy(
            src_ref=src, dst_ref=landed, send_sem=ssem.at[r - 1], recv_sem=rsem.at[r - 1],
            device_id=peer, device_id_type=MESH))
    return starts, recvs


def _call(body, carried, operands, *, name, grid, in_specs, out_specs, out_shape, scratch_shapes=(),
          compiler_params=None):
    single = not isinstance(out_shape, (list, tuple))
    out_specs = [out_specs] if single else list(out_specs)
    out_shape = [out_shape] if single else list(out_shape)
    n_in, n_out, n_sc, nh = len(operands), len(out_shape), len(scratch_shapes), len(carried)

    def full(*refs):
        ins, h_in = refs[:n_in], refs[n_in:n_in + nh]
        o0 = n_in + nh
        outs, h_out = refs[o0:o0 + n_out], refs[o0 + n_out:o0 + n_out + nh]
        s0 = o0 + n_out + nh
        scratch, sems = refs[s0:s0 + n_sc], refs[s0 + n_sc:]
        first = last = None
        for d in range(len(grid)):
            f = pl.program_id(d) == 0
            l = pl.program_id(d) == pl.num_programs(d) - 1
            first = f if first is None else jnp.logical_and(first, f)
            last = l if last is None else jnp.logical_and(last, l)

        def copies(t):
            return _carried_copies(carried[t][0], h_in[t], h_out[t], sems[3 * t:3 * t + 3], carried[t][2])

        if nh:
            @pl.when(first)
            def _():
                for t in range(nh):
                    for cp in copies(t)[0]:
                        cp.start()

        body(*ins, *outs, *scratch)

        if nh:
            @pl.when(last)
            def _():
                for t in range(nh):
                    starts, recvs = copies(t)
                    for cp in recvs:
                        cp.wait_recv()
                    for cp in starts:
                        if carried[t][0] == "gather" and cp is starts[0]:
                            cp.wait()
                        else:
                            cp.wait_send()

    h_shapes = [jax.ShapeDtypeStruct(((4,) + arr.shape) if kind == "gather" else ((3,) + arr.shape[1:]), arr.dtype)
                for kind, arr, _ in carried]
    sem_shapes = [pltpu.SemaphoreType.DMA((3,)), pltpu.SemaphoreType.DMA((3,)), pltpu.SemaphoreType.DMA] * nh
    res = pl.pallas_call(
        full, name=name, grid=grid, in_specs=list(in_specs) + [ANY] * nh, out_specs=out_specs + [ANY] * nh,
        out_shape=out_shape + h_shapes, scratch_shapes=list(scratch_shapes) + sem_shapes,
        compiler_params=compiler_params,
    )(*operands, *[arr for _, arr, _ in carried])
    main = res[:n_out]
    return (main[0] if single else main), list(res[n_out:])


def _mm(pairs, out_dtype, name, tm, tn, vmem_mb=None, carried=()):
    S = pairs[0][0].shape[0]
    N = pairs[0][1].shape[1]
    tm = min(tm, S)
    np_ = len(pairs)

    def body(*refs):
        acc = _dot(refs[0][...], refs[1][...])
        for t in range(1, np_):
            acc = acc + _dot(refs[2 * t][...], refs[2 * t + 1][...])
        refs[-1][...] = acc.astype(refs[-1].dtype)

    in_specs, ops = [], []
    for a, w in pairs:
        in_specs += [pl.BlockSpec((tm, a.shape[1]), lambda n, i: (i, 0)),
                     pl.BlockSpec((w.shape[0], tn), lambda n, i: (0, n))]
        ops += [a, w]
    out, got = _call(
        body, carried, ops, name=name, grid=(N // tn, S // tm), in_specs=in_specs,
        out_specs=pl.BlockSpec((tm, tn), lambda n, i: (i, n)),
        out_shape=jax.ShapeDtypeStruct((S, N), out_dtype),
        compiler_params=_cp(2, vmem_mb))
    return (out, got) if carried else out


def _mm_tn(a_list, b, name, bn, tk, vmem_mb=None):
    S, N = b.shape
    ms = [a.shape[1] for a in a_list]
    M = sum(ms)
    tk = min(tk, S)
    na = len(a_list)

    def body(*refs):
        b_ref, o_ref, o16_ref = refs[na], refs[na + 1], refs[na + 2]

        @pl.when(pl.program_id(1) == 0)
        def _():
            o_ref[...] = jnp.zeros_like(o_ref)
        off = 0
        for t in range(na):
            o_ref[off:off + ms[t], :] += _dot_tn(refs[t][...], b_ref[...])
            off += ms[t]

        @pl.when(pl.program_id(1) == pl.num_programs(1) - 1)
        def _():
            o16_ref[...] = o_ref[...].astype(BF16)

    spec = pl.BlockSpec((None, M, bn), lambda n, k: (n, 0, 0), pipeline_mode=ONE_BUF)
    return pl.pallas_call(
        body, name=name, grid=(N // bn, S // tk),
        in_specs=[pl.BlockSpec((tk, m), lambda n, k: (k, 0)) for m in ms] + [pl.BlockSpec((tk, bn), lambda n, k: (k, n))],
        out_specs=[spec, spec],
        out_shape=[jax.ShapeDtypeStruct((N // bn, M, bn), F32), jax.ShapeDtypeStruct((N // bn, M, bn), BF16)],
        compiler_params=_cp(2, vmem_mb),
    )(*a_list, b)


def _ada_fwd(c_all, w_sh, b_sh):
    def body(c_ref, w_ref, b_ref, o_ref):
        cv = c_ref[...]
        sc = (cv * _sigmoid(cv)).astype(BF16)
        o_ref[...] = _dot(sc, w_ref[...].astype(BF16)) + b_ref[...]

    return pl.pallas_call(
        body, name="ada_fwd", out_shape=jax.ShapeDtypeStruct((c_all.shape[0], w_sh.shape[1]), F32),
        in_specs=[VMEM_SPEC] * 3, out_specs=VMEM_SPEC, compiler_params=_cp(0, 40),
    )(c_all, w_sh, b_sh)


def _ada_bwd(c_all, dmod_sh):
    def body(c_ref, d_ref, o_ref):
        cv = c_ref[...]
        sc = (cv * _sigmoid(cv)).astype(BF16)
        o_ref[...] = _dot_tn(sc, d_ref[...].astype(BF16))

    return pl.pallas_call(
        body, name="ada_bwd", out_shape=jax.ShapeDtypeStruct((c_all.shape[1], dmod_sh.shape[1]), F32),
        in_specs=[VMEM_SPEC] * 2, out_specs=VMEM_SPEC, compiler_params=_cp(0, 40),
    )(c_all, dmod_sh)


def _vec(tm_unused=None):
    return pl.BlockSpec((1, D), lambda i: (0, 0))


def _rows(tm, width=D):
    return pl.BlockSpec((tm, width), lambda i: (i, 0))


def _norm_mod_mm(x, g, shift, scale, w, tm=256, carried=()):
    S = x.shape[0]
    tm = min(tm, S)
    N = w.shape[1]

    def body(x_ref, g_ref, sh_ref, sc_ref, w_ref, h_ref, p_ref):
        xv = x_ref[...]
        r = lax.rsqrt(jnp.mean(xv * xv, axis=-1, keepdims=True) + EPS)
        hn = (xv * r) * g_ref[...]
        h = (hn * (1.0 + sc_ref[...]) + sh_ref[...]).astype(BF16)
        h_ref[...] = h
        p_ref[...] = _dot(h, w_ref[...]).astype(BF16)

    return _call(
        body, carried, [x, g, shift, scale, w], name="norm1_mod_mm_in", grid=(S // tm,),
        in_specs=[_rows(tm), _vec(), _vec(), _vec(), pl.BlockSpec(w.shape, lambda i: (0, 0), pipeline_mode=ONE_BUF)],
        out_specs=[_rows(tm), _rows(tm, N)],
        out_shape=[jax.ShapeDtypeStruct((S, D), BF16), jax.ShapeDtypeStruct((S, N), BF16)],
        compiler_params=_cp(1, 48))


def _mm_rows(pairs, tm):
    ops, specs = [], []
    for a, w in pairs:
        ops += [a, w]
        specs += [pl.BlockSpec((tm, a.shape[1]), lambda i: (i, 0)),
                  pl.BlockSpec(w.shape, lambda i: (0, 0), pipeline_mode=ONE_BUF)]
    return ops, specs


def _mm_rows_value(refs, npairs):
    acc = _dot(refs[0][...], refs[1][...])
    for t in range(1, npairs):
        acc = acc + _dot(refs[2 * t][...], refs[2 * t + 1][...])
    return acc


def _resid_norm_mod(x, mm, gate, g, shift, scale, tm=256):
    S = x.shape[0]
    tm = min(tm, S)
    skip = 2 * len(mm)

    def body(*refs):
        x_ref, gt_ref, g_ref, sh_ref, sc_ref, x1_ref, h_ref, m_ref = refs[skip:]
        mixed = _mm_rows_value(refs, len(mm))
        m_ref[...] = mixed
        x1 = x_ref[...] + (1.0 + gt_ref[...]) * mixed
        x1_ref[...] = x1
        r = lax.rsqrt(jnp.mean(x1 * x1, axis=-1, keepdims=True) + EPS)
        hn = (x1 * r) * g_ref[...]
        h_ref[...] = (hn * (1.0 + sc_ref[...]) + sh_ref[...]).astype(BF16)

    ops, specs = _mm_rows(mm, tm)
    return pl.pallas_call(
        body, name="mm_out_resid_norm2_mod", grid=(S // tm,),
        in_specs=specs + [_rows(tm), _vec(), _vec(), _vec(), _vec()],
        out_specs=[_rows(tm), _rows(tm), _rows(tm)],
        out_shape=[jax.ShapeDtypeStruct((S, D), F32), jax.ShapeDtypeStruct((S, D), BF16),
                   jax.ShapeDtypeStruct((S, D), F32)],
        compiler_params=_cp(1, 40),
    )(*ops, x, gate, g, shift, scale)


def _conv3(ext, w_ref, b_ref, cs):
    e1 = pltpu.roll(ext, 1, 0)
    e2 = pltpu.roll(ext, 2, 0)
    u = b_ref[:, cs] + w_ref[0:1, cs] * e2
    u = u + w_ref[1:2, cs] * e1
    u = u + w_ref[2:3, cs] * ext
    return u, e1, e2


def _up_conv_glu(h2, w_up_p, wc_p, bc_p, tm=256, carried=()):
    S = h2.shape[0]
    tm = min(tm, S)
    widths = [2 * LANES] * (HB // (2 * LANES)) + ([LANES] if HB % (2 * LANES) else [])

    def body(h_ref, wu_ref, w_ref, b_ref, u_ref, a_ref, prev_ref):
        @pl.when(pl.program_id(0) == 0)
        def _():
            prev_ref[...] = jnp.zeros_like(prev_ref)

        hv = h_ref[...]
        for j in range(2):
            base = 0
            for wd in widths:
                us = []
                for off in (2 * j * HB + base, 2 * j * HB + HB + base):
                    cb = _dot(hv, wu_ref[:, off:off + wd]).astype(BF16)
                    u_ref[:, off:off + wd] = cb
                    for q in range(wd // LANES):
                        cs = slice(off + q * LANES, off + (q + 1) * LANES)
                        cq = cb[:, q * LANES:(q + 1) * LANES]
                        ext = jnp.concatenate([prev_ref[:, cs].astype(F32), cq.astype(F32)], axis=0)
                        us.append(_conv3(ext, w_ref, b_ref, cs)[0][16:])
                        prev_ref[:, cs] = cq[tm - 16:]
                nq = wd // LANES
                for q in range(nq):
                    val, gt = us[q], us[nq + q]
                    a_ref[:, j * HB + base + q * LANES:j * HB + base + (q + 1) * LANES] = (
                        val * (gt * _sigmoid_fast(gt))).astype(BF16)
                base += wd

    return _call(
        body, carried, [h2, w_up_p, wc_p, bc_p], name="mm_up_conv_glu", grid=(S // tm,),
        in_specs=[pl.BlockSpec((tm, D), lambda i: (i, 0)),
                  pl.BlockSpec((D, 2 * D_FF), lambda i: (0, 0), pipeline_mode=ONE_BUF),
                  pl.BlockSpec((3, 2 * D_FF), lambda i: (0, 0)),
                  pl.BlockSpec((1, 2 * D_FF), lambda i: (0, 0))],
        out_specs=[pl.BlockSpec((tm, 2 * D_FF), lambda i: (i, 0)), pl.BlockSpec((tm, D_FF), lambda i: (i, 0))],
        out_shape=[jax.ShapeDtypeStruct((S, 2 * D_FF), BF16), jax.ShapeDtypeStruct((S, D_FF), BF16)],
        scratch_shapes=[pltpu.VMEM((16, 2 * D_FF), BF16)],
        compiler_params=_cp(1, 48))


def _conv_glu_up_norm2_bwd(da, u0p, wc_p, bc_p, w_up_t, x1, dx2, g, scale, mixed, gate, tm=256, carried=()):
    S = u0p.shape[0]
    tm = min(tm, S)
    hb = tm // 16
    nlast = S // 16 - 1
    widths = [2 * LANES] * (HB // (2 * LANES)) + ([LANES] if HB % (2 * LANES) else [])

    def body(da_ref, dan_ref, u_ref, p_ref, n_ref, w_ref, b_ref, wt_ref, x_ref, dr_ref, g_ref, sc_ref, m_ref, gt_ref,
             o_ref, s_ref, dx_ref, dm_ref, s2_ref, acc_ref):
        i = pl.program_id(0)
        first = i == 0
        last = i == pl.num_programs(0) - 1

        @pl.when(first)
        def _():
            s_ref[...] = jnp.zeros_like(s_ref)
            s2_ref[...] = jnp.zeros_like(s2_ref)

        n = tm + 16
        started = False
        for j in range(2):
            base = 0
            for wd in widths:
                du0s = ([], [])
                for q in range(wd // LANES):
                    k0 = base + q * LANES
                    kc = slice(j * HB + k0, j * HB + k0 + LANES)
                    dae = jnp.concatenate([da_ref[:, kc].astype(F32),
                                           jnp.where(last, 0.0, dan_ref[:, kc].astype(F32))], axis=0)
                    halves = []
                    for off in (2 * j * HB + k0, 2 * j * HB + HB + k0):
                        cs = slice(off, off + LANES)
                        ext = jnp.concatenate([jnp.where(first, 0.0, p_ref[:, cs].astype(F32)),
                                               u_ref[:, cs].astype(F32), n_ref[:, cs].astype(F32)], axis=0)
                        u, e1, e2 = _conv3(ext, w_ref, b_ref, cs)
                        halves.append((u[16:], ext[16:16 + tm], e1[16:16 + tm], e2[16:16 + tm], cs))
                    val, gt = halves[0][0], halves[1][0]
                    sg = _sigmoid_fast(gt)
                    dus = (dae * (gt * sg), dae * val * (sg * (1.0 + gt * (1.0 - sg))))
                    for t, (du, (_, x0, x1_, x2, cs)) in enumerate(zip(dus, halves)):
                        du0 = (w_ref[2:3, cs] * du + w_ref[1:2, cs] * pltpu.roll(du, n - 1, 0)
                               + w_ref[0:1, cs] * pltpu.roll(du, n - 2, 0))[:tm].astype(BF16)
                        o_ref[:, cs] = du0
                        du0s[t].append(du0)
                        dut = du[:tm]
                        s_ref[0:1, cs] += jnp.sum(dut, axis=0, keepdims=True)
                        s_ref[1:2, cs] += jnp.sum(dut * x2, axis=0, keepdims=True)
                        s_ref[2:3, cs] += jnp.sum(dut * x1_, axis=0, keepdims=True)
                        s_ref[3:4, cs] += jnp.sum(dut * x0, axis=0, keepdims=True)
                for t, off in enumerate((2 * j * HB + base, 2 * j * HB + HB + base)):
                    lhs = du0s[t][0] if len(du0s[t]) == 1 else jnp.concatenate(du0s[t], axis=1)
                    part = _dot(lhs, wt_ref[off:off + wd, :])
                    if started:
                        acc_ref[...] += part
                    else:
                        acc_ref[...] = part
                        started = True
                base += wd

        xv = x_ref[...]
        dhv = acc_ref[...]
        r = lax.rsqrt(jnp.mean(xv * xv, axis=-1, keepdims=True) + EPS)
        nv = xv * r
        gv = g_ref[...]
        hn = nv * gv
        dhn = dhv * (1.0 + sc_ref[...])
        dn = dhn * gv
        dx = dr_ref[...] + r * (dn - nv * jnp.mean(dn * nv, axis=-1, keepdims=True))
        dx_ref[...] = dx
        dm_ref[...] = (dx * (1.0 + gt_ref[...])).astype(BF16)
        s2_ref[0:1, :] += jnp.sum(dhv, axis=0, keepdims=True)
        s2_ref[1:2, :] += jnp.sum(dhv * hn, axis=0, keepdims=True)
        s2_ref[2:3, :] += jnp.sum(dhn * nv, axis=0, keepdims=True)
        s2_ref[3:4, :] += jnp.sum(dx * m_ref[...], axis=0, keepdims=True)

    full = 2 * D_FF
    return _call(
        body, carried, [da, da, u0p, u0p, u0p, wc_p, bc_p, w_up_t, x1, dx2, g, scale, mixed, gate],
        name="conv_glu_up_norm2_bwd", grid=(S // tm,),
        in_specs=[pl.BlockSpec((tm, D_FF), lambda i: (i, 0)),
                  pl.BlockSpec((16, D_FF), lambda i: (jnp.minimum((i + 1) * hb, nlast), 0)),
                  pl.BlockSpec((tm, full), lambda i: (i, 0)),
                  pl.BlockSpec((16, full), lambda i: (jnp.maximum(i * hb - 1, 0), 0)),
                  pl.BlockSpec((16, full), lambda i: (jnp.minimum((i + 1) * hb, nlast), 0)),
                  pl.BlockSpec((3, full), lambda i: (0, 0)),
                  pl.BlockSpec((1, full), lambda i: (0, 0)),
                  pl.BlockSpec((full, D), lambda i: (0, 0), pipeline_mode=ONE_BUF),
                  _rows(tm), _rows(tm), _vec(), _vec(), _rows(tm), _vec()],
        out_specs=[pl.BlockSpec((tm, full), lambda i: (i, 0)), pl.BlockSpec((8, full), lambda i: (0, 0)),
                   _rows(tm), _rows(tm), pl.BlockSpec((8, D), lambda i: (0, 0))],
        out_shape=[jax.ShapeDtypeStruct((S, full), BF16), jax.ShapeDtypeStruct((8, full), F32),
                   jax.ShapeDtypeStruct((S, D), F32), jax.ShapeDtypeStruct((S, D), BF16),
                   jax.ShapeDtypeStruct((8, D), F32)],
        scratch_shapes=[pltpu.VMEM((tm, D), F32)],
        compiler_params=_cp(1, 56))


def _final_loss(x1, mm, gate2, g_final, target, tm=256):
    S = x1.shape[0]
    tm = min(tm, S)
    skip = 2 * len(mm)

    def body(*refs):
        x1_ref, gt_ref, g_ref, t_ref, dx_ref, dy_ref, s_ref = refs[skip:]

        @pl.when(pl.program_id(0) == 0)
        def _():
            s_ref[...] = jnp.zeros_like(s_ref)

        y2 = _mm_rows_value(refs, len(mm))
        og = 1.0 + gt_ref[...]
        x2 = x1_ref[...] + og * y2
        r = lax.rsqrt(jnp.mean(x2 * x2, axis=-1, keepdims=True) + EPS)
        n = x2 * r
        g = g_ref[...]
        err = n * g - t_ref[...]
        dy = err * (1.0 / D)
        dn = dy * g
        dx2 = r * (dn - n * jnp.mean(dn * n, axis=-1, keepdims=True))
        dx_ref[...] = dx2
        dy_ref[...] = (dx2 * og).astype(BF16)
        s_ref[0:1, :] += jnp.sum(dy * n, axis=0, keepdims=True)
        s_ref[1:2, :] += jnp.sum(dx2 * y2, axis=0, keepdims=True)
        s_ref[2:3, :] += jnp.sum(err * err, axis=0, keepdims=True)

    ops, specs = _mm_rows(mm, tm)
    return pl.pallas_call(
        body, name="mm_down_final_loss", grid=(S // tm,),
        in_specs=specs + [_rows(tm), _vec(), _vec(), _rows(tm)],
        out_specs=[_rows(tm), _rows(tm), pl.BlockSpec((8, D), lambda i: (0, 0))],
        out_shape=[jax.ShapeDtypeStruct((S, D), F32), jax.ShapeDtypeStruct((S, D), BF16),
                   jax.ShapeDtypeStruct((8, D), F32)],
        compiler_params=_cp(1, 40),
    )(*ops, x1, gate2, g_final, target)


def _norm_mod_bwd(dh, xin, dres, g, scale, mixed, gate, name, tm=256, carried=()):
    S = xin.shape[0]
    tm = min(tm, S)
    with_gate = mixed is not None
    fused = isinstance(dh, list)
    skip = 2 * len(dh) if fused else 1

    def body(*refs):
        if with_gate:
            x_ref, dr_ref, g_ref, sc_ref, m_ref, gt_ref, dx_ref, dm_ref, s_ref = refs[skip:]
        else:
            x_ref, dr_ref, g_ref, sc_ref, dx_ref, s_ref = refs[skip:]

        @pl.when(pl.program_id(0) == 0)
        def _():
            s_ref[...] = jnp.zeros_like(s_ref)

        xv = x_ref[...]
        dhv = _mm_rows_value(refs, len(dh)) if fused else refs[0][...]
        r = lax.rsqrt(jnp.mean(xv * xv, axis=-1, keepdims=True) + EPS)
        n = xv * r
        g = g_ref[...]
        hn = n * g
        dhn = dhv * (1.0 + sc_ref[...])
        dn = dhn * g
        dx = dr_ref[...] + r * (dn - n * jnp.mean(dn * n, axis=-1, keepdims=True))
        dx_ref[...] = dx
        s_ref[0:1, :] += jnp.sum(dhv, axis=0, keepdims=True)
        s_ref[1:2, :] += jnp.sum(dhv * hn, axis=0, keepdims=True)
        s_ref[2:3, :] += jnp.sum(dhn * n, axis=0, keepdims=True)
        if with_gate:
            dm_ref[...] = (dx * (1.0 + gt_ref[...])).astype(BF16)
            s_ref[3:4, :] += jnp.sum(dx * m_ref[...], axis=0, keepdims=True)

    ins, in_specs = _mm_rows(dh, tm) if fused else ([dh], [_rows(tm)])
    ins += [xin, dres, g, scale]
    in_specs += [_rows(tm), _rows(tm), _vec(), _vec()]
    out_specs = [_rows(tm)]
    out_shape = [jax.ShapeDtypeStruct((S, D), F32)]
    if with_gate:
        ins += [mixed, gate]
        in_specs += [_rows(tm), _vec()]
        out_specs.append(_rows(tm))
        out_shape.append(jax.ShapeDtypeStruct((S, D), BF16))
    out_specs.append(pl.BlockSpec((8, D), lambda i: (0, 0)))
    out_shape.append(jax.ShapeDtypeStruct((8, D), F32))
    return _call(body, carried, ins, name=name, grid=(S // tm,), in_specs=in_specs, out_specs=out_specs,
                 out_shape=out_shape, compiler_params=_cp(1, 48 if fused else None))


def _tri(n, rel):
    row = lax.broadcasted_iota(jnp.int32, (n, n), 0)
    col = lax.broadcasted_iota(jnp.int32, (n, n), 1)
    return {"gt": row > col, "ge": row >= col, "lt": row < col, "le": row <= col}[rel]


def _pair_diag(mask):
    u = jnp.where(mask, 1.0, 0.0).astype(BF16)
    z = jnp.zeros_like(u)
    return jnp.concatenate([jnp.concatenate([u, z], axis=1), jnp.concatenate([z, u], axis=1)], axis=0)


def _pair_rows(xp, lo_half):
    z = jnp.zeros_like(xp)
    return jnp.concatenate([jnp.where(lo_half, xp, z), jnp.where(lo_half, z, xp)], axis=0)


def _sb_scores(z, causal, diag):
    ls, ps, es = [], [], []
    for hh in range(2):
        zz = z[:, hh * QB:(hh + 1) * QB]
        e = jnp.exp(-jnp.abs(zz))
        l = -(jnp.maximum(zz, 0.0) + jnp.log(1.0 + e))
        ps.append(l + zz)
        ls.append(jnp.where(causal, l, 0.0) if diag else l)
        es.append(e)
    return ls, ps, es


def _sb_fwd(proj, carried=()):
    S = proj.shape[0]
    nq = S // QB

    def body(q_ref, k_ref, v_ref, o_ref, t_ref, c_ref, acc_ref, qs_ref):
        i = pl.program_id(0)
        causal = _tri(QB, "gt")
        usuf = _pair_diag(_tri(QB, "gt"))
        lo_half = lax.broadcasted_iota(jnp.int32, (QB, 128), 1) < DK
        qs_ref[...] = q_ref[...] * 0.125

        def block(j, diag, nr):
            rows = pl.ds(pl.multiple_of(j * QB, QB), QB)
            rs = slice(0, nr)
            pairs = range(H_SB // 2)
            cols = [slice(pr * 128, (pr + 1) * 128) for pr in pairs]
            zs = [_dot_nt(qs_ref[rs, cols[pr]], _pair_rows(k_ref[rows, cols[pr]], lo_half)) for pr in pairs]
            sc = [_sb_scores(zs[pr], causal, diag) for pr in pairs]
            sufs = []
            for pr in pairs:
                lh, ll = _split(jnp.concatenate(sc[pr][0], axis=1))
                sufs.append(_dot(lh, usuf) + _dot(ll, usuf))
            cmax = None
            wps = []
            for pr in pairs:
                ws = []
                for hh in range(2):
                    h = 2 * pr + hh
                    b = sufs[pr][:, hh * QB:(hh + 1) * QB]
                    if not diag:
                        b = b + c_ref[h, rs, 0:1]
                    w = jnp.exp(sc[pr][1][hh] + b)
                    ws.append((jnp.where(causal, w, 0.0) if diag else w).astype(BF16))
                    cn = b[:, 0:1] + sc[pr][0][hh][:, 0:1]
                    c_ref[h, rs, 0:1] = cn
                    cmax = cn if cmax is None else jnp.maximum(cmax, cn)
                wps.append(jnp.concatenate(ws, axis=1))
            for pr in pairs:
                upd = _dot(wps[pr], _pair_rows(v_ref[rows, cols[pr]], lo_half))
                if diag:
                    acc_ref[rs, cols[pr]] = upd
                else:
                    acc_ref[rs, cols[pr]] += upd
            lo = jnp.max(cmax[:SB_HEAD_ROWS])
            return (jnp.max(cmax[SB_HEAD_ROWS:]) if nr > SB_HEAD_ROWS else None), lo

        def cond_full(st):
            return jnp.logical_and(st[0] >= 0, st[1] > SB_SKIP)

        def step_full(st):
            return (st[0] - 1,) + block(st[0], False, QB)

        def cond_head(st):
            return jnp.logical_and(st[0] >= 0, st[1] > SB_SKIP)

        def step_head(st):
            return st[0] - 1, block(st[0], False, SB_HEAD_ROWS)[1]

        j, _, lo = lax.while_loop(cond_full, step_full, (i - 1,) + block(i, True, QB))
        jfull = j + 1
        j, _ = lax.while_loop(cond_head, step_head, (j, lo))
        o_ref[...] = acc_ref[...].astype(BF16)
        t_ref[...] = jnp.zeros_like(t_ref)
        for h in range(H_SB):
            t_ref[h // 4, :, h % 4:h % 4 + 1] = c_ref[h, :, 0:1]
        t_ref[:, :, 8:9] = jnp.zeros((2, QB, 1), F32) + (j + 1).astype(F32)
        t_ref[:, :, 9:10] = jnp.zeros((2, QB, 1), F32) + jfull.astype(F32)

    return _call(
        body, carried, [proj, proj, proj], name="sb_fwd", grid=(nq,),
        in_specs=[pl.BlockSpec((QB, 512), lambda i: (i, 0)),
                  pl.BlockSpec((S, 512), lambda i: (0, 1), pipeline_mode=ONE_BUF),
                  pl.BlockSpec((S, 512), lambda i: (0, 2), pipeline_mode=ONE_BUF)],
        out_specs=[pl.BlockSpec((QB, 512), lambda i: (i, 0)),
                   pl.BlockSpec((2, QB, 128), lambda i: (0, i, 0))],
        out_shape=[jax.ShapeDtypeStruct((S, 512), BF16), jax.ShapeDtypeStruct((2, S, 128), F32)],
        scratch_shapes=[pltpu.VMEM((H_SB, QB, 128), F32), pltpu.VMEM((QB, 512), F32), pltpu.VMEM((QB, 512), BF16)],
        compiler_params=_cp(1, 40))


def _sb_bwd(proj, dcat, stats, carried=()):
    S = proj.shape[0]
    nq = S // QB

    def body(q_ref, k_ref, v_ref, do_ref, t_ref, dq_ref, dk_ref, dv_ref, dk_acc, dv_acc, dq_acc, pc_ref, qs_ref,
             qt_ref, dot_ref):
        i = pl.program_id(1)

        @pl.when(i == 0)
        def _():
            dk_acc[...] = jnp.zeros_like(dk_acc)
            dv_acc[...] = jnp.zeros_like(dv_acc)

        causal = _tri(QB, "gt")
        uin = _pair_diag(_tri(QB, "le"))
        uex = _pair_diag(_tri(QB, "lt"))
        lo_half = lax.broadcasted_iota(jnp.int32, (QB, 128), 1) < DK
        qs_ref[...] = q_ref[...] * 0.125
        lo_rows = lax.broadcasted_iota(jnp.int32, (128, QB), 0) < DK
        for pr in range(2):
            qt_ref[pr] = (q_ref[:, pr * 128:(pr + 1) * 128].astype(F32) * 0.125).T.astype(BF16)
            dot_ref[pr] = do_ref[:, pr * 128:(pr + 1) * 128].astype(F32).T.astype(BF16)
        pc_ref[...] = jnp.zeros_like(pc_ref)
        dq_acc[...] = jnp.zeros_like(dq_acc)
        jstart = jnp.max(t_ref[:, 8:9]).astype(jnp.int32)
        jfull = jnp.max(t_ref[:, 9:10]).astype(jnp.int32)

        def block(j, diag, nr):
            rows = pl.ds(pl.multiple_of(j * QB, QB), QB)
            rs = slice(0, nr)
            pairs = range(2)
            cols = [slice(pr * 128, (pr + 1) * 128) for pr in pairs]
            kbds = [_pair_rows(k_ref[rows, cols[pr]], lo_half) for pr in pairs]
            zs = [_dot_nt(qs_ref[rs, cols[pr]], kbds[pr]) for pr in pairs]
            dws = [_dot_nt(do_ref[rs, cols[pr]], _pair_rows(v_ref[rows, cols[pr]], lo_half)) for pr in pairs]
            sc = [_sb_scores(zs[pr], causal, diag) for pr in pairs]
            plins = []
            for pr in pairs:
                lh, ll = _split(jnp.concatenate(sc[pr][0], axis=1))
                plins.append(_dot(lh, uin) + _dot(ll, uin))
            wss, gss, gexs = [], [], []
            for pr in pairs:
                ws, gs = [], []
                for hh in range(2):
                    h = 2 * pr + hh
                    half = slice(hh * QB, (hh + 1) * QB)
                    b = (t_ref[rs, h:h + 1] - pc_ref[h, rs, 0:1]) - plins[pr][:, half]
                    w = jnp.exp(sc[pr][1][hh] + b)
                    if diag:
                        w = jnp.where(causal, w, 0.0)
                    ws.append(w)
                    gs.append(dws[pr][:, half] * w)
                wss.append(ws)
                gss.append(gs)
            for pr in pairs:
                gh, gl = _split(jnp.concatenate(gss[pr], axis=1))
                gexs.append(_dot(gh, uex) + _dot(gl, uex))
            dzbs = []
            for pr in pairs:
                dzs = []
                for hh in range(2):
                    h = 2 * pr + hh
                    half = slice(hh * QB, (hh + 1) * QB)
                    e = sc[pr][2][hh]
                    r = pl.reciprocal(1.0 + e, approx=True)
                    er = e * r
                    pos = zs[pr][:, half] >= 0.0
                    gx = gexs[pr][:, half]
                    g = gss[pr][hh]
                    dz = g * jnp.where(pos, er, r) - (gx + pc_ref[4 + h, rs, 0:1]) * jnp.where(pos, r, er)
                    dzs.append(jnp.where(causal, dz, 0.0) if diag else dz)
                    pc_ref[h, rs, 0:1] += plins[pr][:, half][:, QB - 1:QB]
                    pc_ref[4 + h, rs, 0:1] += gx[:, QB - 1:QB] + g[:, QB - 1:QB]
                dzbs.append(jnp.concatenate(dzs, axis=1).astype(BF16))
            for pr in pairs:
                dq_acc[rs, cols[pr]] += _dot(dzbs[pr], kbds[pr])
                r1 = _dot(qt_ref[pr, :, rs], dzbs[pr])
                dk_acc[pr, j] += jnp.where(lo_rows, r1[:, :QB], r1[:, QB:])
                r2 = _dot(dot_ref[pr, :, rs], jnp.concatenate(wss[pr], axis=1).astype(BF16))
                dv_acc[pr, j] += jnp.where(lo_rows, r2[:, :QB], r2[:, QB:])

        def step_head(j, carry):
            block(j, False, SB_HEAD_ROWS)
            return carry

        def step_full(j, carry):
            block(j, False, QB)
            return carry

        lax.fori_loop(jstart, jfull, step_head, 0)
        lax.fori_loop(jfull, i, step_full, 0)
        block(i, True, QB)
        dq_ref[...] = (dq_acc[...] * 0.125).astype(BF16)

        @pl.when(i == nq - 1)
        def _():
            def put(jj, carry):
                krows = pl.ds(pl.multiple_of(jj * QB, QB), QB)
                for pr in range(2):
                    dk_ref[krows, pr * 128:(pr + 1) * 128] = dk_acc[pr, jj].T.astype(BF16)
                    dv_ref[krows, pr * 128:(pr + 1) * 128] = dv_acc[pr, jj].T.astype(BF16)
                return carry
            lax.fori_loop(0, nq, put, 0)

    return _call(
        body, carried, [proj, proj, proj, dcat, stats], name="sb_bwd", grid=(2, nq),
        in_specs=[pl.BlockSpec((QB, 256), lambda g, i: (i, g)),
                  pl.BlockSpec((S, 256), lambda g, i: (0, 2 + g), pipeline_mode=ONE_BUF),
                  pl.BlockSpec((S, 256), lambda g, i: (0, 4 + g), pipeline_mode=ONE_BUF),
                  pl.BlockSpec((QB, 256), lambda g, i: (i, g)),
                  pl.BlockSpec((None, QB, 128), lambda g, i: (g, i, 0))],
        out_specs=[pl.BlockSpec((QB, 256), lambda g, i: (i, g)),
                   pl.BlockSpec((S, 256), lambda g, i: (0, g)),
                   pl.BlockSpec((S, 256), lambda g, i: (0, g))],
        out_shape=[jax.ShapeDtypeStruct((S, 512), BF16)] * 3,
        scratch_shapes=[pltpu.VMEM((2, nq, 128, QB), F32), pltpu.VMEM((2, nq, 128, QB), F32),
                        pltpu.VMEM((QB, 256), F32), pltpu.VMEM((8, QB, 128), F32), pltpu.VMEM((QB, 256), BF16),
                        pltpu.VMEM((2, 128, QB), BF16), pltpu.VMEM((2, 128, QB), BF16)],
        compiler_params=_cp(2, 56))


GLA_NC = 4
GLA_R = GLA_NC * CHUNK


def _chunk_tri(strict):
    row = lax.broadcasted_iota(jnp.int32, (GLA_R, GLA_R), 0)
    col = lax.broadcasted_iota(jnp.int32, (GLA_R, GLA_R), 1)
    m = jnp.logical_and(row // CHUNK == col // CHUNK, row > col if strict else row >= col)
    u = jnp.where(m, 1.0, 0.0).astype(BF16)
    return jnp.concatenate([u, u], axis=1)


def _per_chunk_rows(vals):
    return jnp.concatenate([jnp.broadcast_to(v, (CHUNK, v.shape[1])) for v in vals], axis=0)


def _head_blocks(st):
    row = lax.broadcasted_iota(jnp.int32, (H_GLA * DV, H_GLA * DK), 0)
    col = lax.broadcasted_iota(jnp.int32, (H_GLA * DV, H_GLA * DK), 1)
    t = jnp.concatenate([st.astype(BF16)] * H_GLA, axis=0)
    return jnp.where(row // DV == col // DK, t, jnp.zeros_like(t))


def _head_diag(big):
    head = lax.broadcasted_iota(jnp.int32, (DV, H_GLA * DK), 1) // DK
    out = big[0:DV]
    for h in range(1, H_GLA):
        out = jnp.where(head == h, big[h * DV:(h + 1) * DV], out)
    return out


def _gla_gate4(gf_ref, wfg_ref, bfg_ref):
    f = _dot(gf_ref[...], wfg_ref[...]) + bfg_ref[...]
    _, la, _ = _log_sigmoid_parts(f)
    lah, lal = _split(la * (1.0 / 16.0))
    cum = _dot(_chunk_tri(False), jnp.concatenate([lah, lal], axis=0))
    tots = [cum[(c + 1) * CHUNK - 1:(c + 1) * CHUNK, :] for c in range(GLA_NC)]
    return f, jnp.exp(_per_chunk_rows(tots) - cum), [jnp.exp(t) for t in tots]


def _gla_specs4(ns, rev):
    def ix(i):
        return ns - 1 - i if rev else i
    return [pl.BlockSpec((GLA_R, 256), lambda i: (ix(i), 6)),
            pl.BlockSpec((GLA_R, 256), lambda i: (ix(i), 7)),
            pl.BlockSpec((GLA_R, 512), lambda i: (ix(i), 4)),
            pl.BlockSpec((GLA_R, 512), lambda i: (ix(i), 5)),
            pl.BlockSpec((GLA_R, 128), lambda i: (ix(i), 24))]


def _gla_fwd(proj, wfg_p, bfg, ggla, carried=()):
    S = proj.shape[0]
    ns = S // GLA_R

    def body(q_ref, k_ref, v_ref, gg_ref, gf_ref, wfg_ref, bfg_ref, ggla_ref, o_ref, st_ref, state):
        @pl.when(pl.program_id(0) == 0)
        def _():
            state[...] = jnp.zeros_like(state)

        _, e, decs = _gla_gate4(gf_ref, wfg_ref, bfg_ref)
        kdec = (k_ref[...].astype(F32) * e).astype(BF16)
        rows = [slice(c * CHUNK, (c + 1) * CHUNK) for c in range(GLA_NC)]
        kvs = [_head_diag(_dot_tn(v_ref[rows[c], :], kdec[rows[c]])) for c in range(GLA_NC)]
        st = state[...]
        sts = []
        for c in range(GLA_NC):
            st = st * decs[c] + kvs[c]
            st_ref[c] = st
            sts.append(st)
        state[...] = st
        o = jnp.concatenate([_dot_nt(q_ref[rows[c], :] * 0.125, _head_blocks(sts[c])) for c in range(GLA_NC)], axis=0)
        for h in range(H_GLA):
            vs = slice(h * DV, (h + 1) * DV)
            oh = o[:, vs]
            ohn = oh * lax.rsqrt(jnp.mean(oh * oh, axis=-1, keepdims=True) + EPS)
            gg = gg_ref[:, vs].astype(F32)
            o_ref[:, vs] = ((ohn * ggla_ref[:, vs]) * (gg * _sigmoid(gg))).astype(BF16)

    return _call(
        body, carried, [proj, proj, proj, proj, proj, wfg_p, bfg, ggla], name="gla_fwd", grid=(ns,),
        in_specs=_gla_specs4(ns, False) + [pl.BlockSpec((128, 256), lambda i: (0, 0)),
                                           pl.BlockSpec((1, 256), lambda i: (0, 0)),
                                           pl.BlockSpec((1, 512), lambda i: (0, 0))],
        out_specs=[pl.BlockSpec((GLA_R, 512), lambda i: (i, 0)),
                   pl.BlockSpec((GLA_NC, 128, 256), lambda i: (i, 0, 0))],
        out_shape=[jax.ShapeDtypeStruct((S, 512), BF16), jax.ShapeDtypeStruct((S // CHUNK, 128, 256), F32)],
        scratch_shapes=[pltpu.VMEM((128, 256), F32)],
        compiler_params=_cp(1))


def _gla_bwd(dcat, proj, states, wfg_p, bfg, ggla, carried=()):
    S = proj.shape[0]
    ns = S // GLA_R

    def body(do_ref, q_ref, k_ref, v_ref, gg_ref, gf_ref, sc_ref, sp_ref, wfg_ref, bfg_ref, ggla_ref,
             dp_ref, s_ref, dw_ref, carry):
        sr = pl.program_id(0)

        @pl.when(sr == 0)
        def _():
            carry[...] = jnp.zeros_like(carry)
            s_ref[...] = jnp.zeros_like(s_ref)
            dw_ref[...] = jnp.zeros_like(dw_ref)

        f, e, decs = _gla_gate4(gf_ref, wfg_ref, bfg_ref)
        kf = k_ref[...].astype(F32) * e
        kdec = kf.astype(BF16)
        rows = [slice(c * CHUNK, (c + 1) * CHUNK) for c in range(GLA_NC)]
        sts = [sc_ref[c] for c in range(GLA_NC)]
        st_before = jnp.where(sr < ns - 1, sp_ref[0], 0.0)
        sbd = [_head_blocks(sts[c]) for c in range(GLA_NC)]
        qs = q_ref[...] * 0.125
        o = jnp.concatenate([_dot_nt(qs[rows[c]], sbd[c]) for c in range(GLA_NC)], axis=0)
        dobs = []
        for h in range(H_GLA):
            vs = slice(h * DV, (h + 1) * DV)
            oh = o[:, vs]
            rr = lax.rsqrt(jnp.mean(oh * oh, axis=-1, keepdims=True) + EPS)
            ohn = oh * rr
            gg = gg_ref[:, vs].astype(F32)
            sg = _sigmoid(gg)
            dout = do_ref[:, vs].astype(F32)
            gl = ggla_ref[:, vs]
            dp_ref[:, 1024 + h * DV:1024 + (h + 1) * DV] = (
                dout * (ohn * gl) * (sg * (1.0 + gg * (1.0 - sg)))).astype(BF16)
            dt1 = dout * (gg * sg)
            s_ref[0:1, vs] += jnp.sum(dt1 * ohn, axis=0, keepdims=True)
            dohn = dt1 * gl
            dobs.append((rr * (dohn - ohn * jnp.mean(dohn * ohn, axis=-1, keepdims=True))).astype(BF16))
        dob = jnp.concatenate(dobs, axis=1)
        dsout = []
        for c in range(GLA_NC):
            dp_ref[rows[c], 0:256] = (_dot(dob[rows[c]], sbd[c]) * 0.125).astype(BF16)
            dsout.append(_head_diag(_dot_tn(dob[rows[c]], qs[rows[c]])))
        g = carry[...]
        gts, ddecs = [None] * GLA_NC, [None] * GLA_NC
        for c in reversed(range(GLA_NC)):
            gts[c] = dsout[c] + g
            ddecs[c] = jnp.sum(gts[c] * (sts[c - 1] if c > 0 else st_before), axis=0, keepdims=True) * decs[c]
            g = gts[c] * decs[c]
        carry[...] = g
        dkds = []
        for c in range(GLA_NC):
            gbd = _head_blocks(gts[c])
            dkds.append(_dot(v_ref[rows[c], :], gbd))
            dp_ref[rows[c], 512:1024] = _dot_nt(kdec[rows[c]], gbd).astype(BF16)
        dkd = jnp.concatenate(dkds, axis=0)
        dp_ref[:, 256:512] = (dkd * e).astype(BF16)
        wh, wl = _split(dkd * kf)
        dla = _dot(_chunk_tri(True), jnp.concatenate([wh, wl], axis=0)) + _per_chunk_rows(ddecs)
        df = dla * _sigmoid(-f) * (1.0 / 16.0)
        dfb = df.astype(BF16)
        s_ref[1:2, 0:256] += jnp.sum(df, axis=0, keepdims=True)
        dw_ref[...] += _dot_tn(gf_ref[...], dfb)
        dp_ref[:, 1536:1664] = _dot_nt(dfb, wfg_ref[...]).astype(BF16)

    return _call(
        body, carried, [dcat, proj, proj, proj, proj, proj, states, states, wfg_p, bfg, ggla],
        name="gla_bwd", grid=(ns,),
        in_specs=[pl.BlockSpec((GLA_R, 512), lambda i: (ns - 1 - i, 1))] + _gla_specs4(ns, True) + [
            pl.BlockSpec((GLA_NC, 128, 256), lambda i: (ns - 1 - i, 0, 0)),
            pl.BlockSpec((1, 128, 256), lambda i: (jnp.maximum((ns - 1 - i) * GLA_NC - 1, 0), 0, 0)),
            pl.BlockSpec((128, 256), lambda i: (0, 0)),
            pl.BlockSpec((1, 256), lambda i: (0, 0)),
            pl.BlockSpec((1, 512), lambda i: (0, 0))],
        out_specs=[pl.BlockSpec((GLA_R, 1664), lambda i: (ns - 1 - i, 0)),
                   pl.BlockSpec((8, 512), lambda i: (0, 0)),
                   pl.BlockSpec((128, 256), lambda i: (0, 0))],
        out_shape=[jax.ShapeDtypeStruct((S, 1664), BF16), jax.ShapeDtypeStruct((8, 512), F32),
                   jax.ShapeDtypeStruct((128, 256), F32)],
        scratch_shapes=[pltpu.VMEM((128, 256), F32)],
        compiler_params=_cp(1))


def _sum_leading(a, name):
    n = a.shape[0]

    def body(a_ref, o_ref):
        acc = a_ref[0]
        for k in range(1, n):
            acc = acc + a_ref[k]
        o_ref[...] = acc

    return pl.pallas_call(
        body, name=name, out_shape=jax.ShapeDtypeStruct(a.shape[1:], F32),
        in_specs=[VMEM_SPEC], out_specs=VMEM_SPEC,
    )(a)


def _sum_chip(own, recv, name):
    R, C = own.shape
    tr, tc = _tile2d(R, C, 1024 * 1024)

    def body(o_ref, r_ref, p_ref):
        acc = o_ref[...]
        for k in range(3):
            acc = acc + r_ref[k].astype(F32)
        p_ref[...] = acc

    return pl.pallas_call(
        body, name=name, grid=(R // tr, C // tc),
        in_specs=[pl.BlockSpec((tr, tc), lambda i, j: (i, j)), pl.BlockSpec((3, tr, tc), lambda i, j: (0, i, j))],
        out_specs=pl.BlockSpec((tr, tc), lambda i, j: (i, j)),
        out_shape=jax.ShapeDtypeStruct((R, C), F32), compiler_params=_cp(2, 40),
    )(own, recv)


def _adamw(w, p, q, m, v, name):
    R, C = w.shape
    tr, tc = _tile2d(R, C, 1024 * 1024)
    two = q is not None

    def body(*refs):
        if two:
            w_ref, p_ref, q_ref, m_ref, v_ref, g_out, d_out, m_out, v_out = refs
            g = p_ref[...] + q_ref[...]
        else:
            w_ref, p_ref, m_ref, v_ref, g_out, d_out, m_out, v_out = refs
            g = p_ref[...]
        m2 = B1 * m_ref[...] + (1.0 - B1) * g
        v2 = B2 * v_ref[...] + (1.0 - B2) * (g * g)
        m_hat = m2 / (1.0 - B1 ** STEP)
        v_hat = v2 / (1.0 - B2 ** STEP)
        g_out[...] = g
        d_out[...] = -LR * (m_hat / (jnp.sqrt(v_hat) + EPS_A) + WD * w_ref[...])
        m_out[...] = m2
        v_out[...] = v2

    spec = pl.BlockSpec((tr, tc), lambda i, j: (i, j))
    ins = [w, p, q, m, v] if two else [w, p, m, v]
    return pl.pallas_call(
        body, name=name, grid=(R // tr, C // tc),
        in_specs=[spec] * len(ins), out_specs=[spec] * 4,
        out_shape=[jax.ShapeDtypeStruct((R, C), F32)] * 4, compiler_params=_cp(2, 40),
    )(*ins)


def _cols_to_chips(a, width):
    return a.reshape(a.shape[0], 4, width).swapaxes(0, 1)


def _chips_to_cols(a):
    return a.swapaxes(0, 1).reshape(a.shape[1], 4 * a.shape[2])


def _swap_mid(a):
    lead = a.shape[:-1]
    return a.reshape(lead + (2, 2, HB)).swapaxes(-3, -2).reshape(lead + (4 * HB,))


def kernel(x, c, w_ada, b_ada, g_norm1, w_in, w_fg2, b_fg2, g_gla_out, w_out, g_norm2, w_up, w_conv, b_conv, w_down, g_final, loss_target, m_w_ada, m_b_ada, m_g_norm1, m_w_in, m_w_fg2, m_b_fg2, m_g_gla_out, m_w_out, m_g_norm2, m_w_up, m_w_conv, m_b_conv, m_w_down, m_g_final, v_w_ada, v_b_ada, v_g_norm1, v_w_in, v_w_fg2, v_b_fg2, v_g_gla_out, v_w_out, v_g_norm2, v_w_up, v_w_conv, v_b_conv, v_w_down, v_g_final):
    xi, yi, ci = lax.axis_index("x"), lax.axis_index("y"), lax.axis_index("c")
    cidx = 2 * xi + yi
    didx = 4 * xi + 2 * yi + ci
    xs = x[0]
    tgt = loss_target[0]
    gfin = g_final.reshape(1, D)
    AW = D * 6 // 4

    c_all = _allgather8(c, "gather_c").reshape(8, D)
    c_pad = jnp.concatenate([c_all, jnp.zeros((8, D), F32)], axis=0)
    mod_part = _ada_fwd(c_pad, w_ada[0], lax.dynamic_slice(b_ada, (0, cidx * AW), (1, AW)))[:8]
    small = jnp.concatenate([mod_part.reshape(-1), w_conv.reshape(-1), w_fg2.reshape(-1)]).reshape(-1, 128)
    small_g = _allgather4(small, "gather_small").reshape(4, -1)
    mod = lax.dynamic_index_in_dim(small_g[:, :8 * AW].reshape(4, 8, AW), didx, axis=1, keepdims=False).reshape(1, 6 * D)
    shift1, scale1, gate1, shift2, scale2, gate2 = [mod[:, k * D:(k + 1) * D] for k in range(6)]
    o1 = 8 * AW
    o2 = o1 + 3 * HB
    wc_p = _swap_mid(_chips_to_cols(small_g[:, o1:o2].reshape(4, 3, HB)))
    bc_p = _swap_mid(b_conv)
    wfg_full = _chips_to_cols(small_g[:, o2:].reshape(4, RANK, 64))
    wfg_p = jnp.concatenate([wfg_full, jnp.zeros((128 - RANK, 256), F32)], axis=0).astype(BF16)

    w_in_t = _allgather4_split(w_in[0].T.astype(BF16), "gather_w_in").reshape(N_IN, D)
    w_in_t = jnp.concatenate([w_in_t, jnp.zeros((N_IN_P - N_IN, D), BF16)], axis=0)
    w_in_p = w_in_t.T

    (h, proj), _ = _norm_mod_mm(xs, g_norm1, shift1, scale1, w_in_p)
    (o_gla, states), (w_out_g,) = _gla_fwd(proj, wfg_p, b_fg2, g_gla_out,
                                           carried=[("gather", w_out[0].astype(BF16), False)])
    w_out_f = w_out_g.reshape(D, D)
    (o_sb, stats), (w_up_g,) = _sb_fwd(proj, carried=[("gather", w_up[0].astype(BF16), False)])
    w_up_p = _swap_mid(_chips_to_cols(w_up_g))
    x1, h2, mixed = _resid_norm_mod(xs, [(o_sb, w_out_f[:512]), (o_gla, w_out_f[512:])],
                                    gate1, g_norm2, shift2, scale2)
    (u0p, a), (w_down_g,) = _up_conv_glu(h2, w_up_p, wc_p, bc_p,
                                         carried=[("gather", w_down[0].astype(BF16), False)])
    w_down_f = w_down_g.reshape(D_FF, D)
    dx2, dy2, s_fin = _final_loss(x1, [(a, w_down_f)], gate2, gfin, tgt)

    da = _mm([(dy2, w_down_f.T)], BF16, "mm_down_t", 512, D_FF, 48)
    dw_down, dw_down_h = [t.reshape(4, D_FF // 4, D) for t in _mm_tn([a], dy2, "mm_dw_down", D, 1024, 60)]
    (du0p, s_conv, dx1, dmixed, s_n2), (rc_down,) = _conv_glu_up_norm2_bwd(
        da, u0p, wc_p, bc_p, w_up_p.T, x1, dx2, g_norm2, scale2, mixed, gate1,
        carried=[("scatter", dw_down_h, False)])
    dw_up, dw_up_h = _mm_tn([h2], du0p, "mm_dw_up", HB, 1024, 56)
    dcat = _mm([(dmixed, w_out_f.T)], BF16, "mm_out_t", 512, D)
    dw_out, dw_out_h = [t.reshape(4, D // 4, D) for t in _mm_tn([o_sb, o_gla], dmixed, "mm_dw_out", D, 512)]
    (dq, dk, dv), (rc_up,) = _sb_bwd(proj, dcat, stats, carried=[("scatter", dw_up_h, True)])
    (dp_gla, s_gla, dwfg), (rc_out,) = _gla_bwd(dcat, proj, states, wfg_p, b_fg2, g_gla_out,
                                                carried=[("scatter", dw_out_h, False)])
    dw_in, dw_in_h = _mm_tn([dq, dk, dv, dp_gla], h, "mm_dw_in", D, 1024, 60)
    dw_in_h = dw_in_h[0, :N_IN].reshape(4, N_IN // 4, D)
    dw_in_own = lax.dynamic_slice(dw_in[0], (cidx * (N_IN // 4), 0), (N_IN // 4, D))
    (gx, s_n1), (rc_in,) = _norm_mod_bwd(
        [(dq, w_in_t[:512]), (dk, w_in_t[512:1024]), (dv, w_in_t[1024:1536]), (dp_gla, w_in_t[1536:])],
        xs, dx1, g_norm1, scale1, None, None, "mm_in_t_norm1_bwd", carried=[("scatter", dw_in_h, False)])

    dmod = jnp.concatenate([s_n1[0], s_n1[1], s_n2[3], s_n2[0], s_n2[1], s_fin[1]])
    s_conv_n = _swap_mid(s_conv[:4])
    part = jnp.concatenate([dmod, s_n1[2], s_n2[2], s_fin[0], s_gla[0], s_gla[1, :256], s_conv_n[0],
                            s_conv_n[1:4].reshape(-1), dwfg[:RANK].reshape(-1),
                            jnp.broadcast_to(jnp.sum(s_fin[2]), (128,))]).reshape(-1, 128)
    parts = _allgather8(part, "gather_small_grads")
    tot = _sum_leading(parts, "sum_small_grads").reshape(-1)
    loss = 0.5 / D * tot[-1]
    dmod_all = parts.reshape(8, -1)[:, :6 * D]
    offs = [0]
    for n in (6 * D, D, D, D, 512, 256, 2 * D_FF, 3 * 2 * D_FF, RANK * 256):
        offs.append(offs[-1] + n)
    g_b_ada, g_g1, g_g2, g_gf, g_ggla, g_bfg, g_bconv, g_wconv_full, g_wfg_full = [
        tot[offs[k]:offs[k + 1]] for k in range(9)]
    g_wconv = lax.dynamic_index_in_dim(_cols_to_chips(g_wconv_full.reshape(3, 2 * D_FF), HB), cidx, 0, keepdims=False)
    g_wfg = lax.dynamic_index_in_dim(_cols_to_chips(g_wfg_full.reshape(RANK, 256), 64), cidx, 0, keepdims=False)

    dmod_pad = jnp.concatenate([dmod_all, jnp.zeros((8, 6 * D), F32)], axis=0)
    g_w_ada = _ada_bwd(c_pad, lax.dynamic_slice(dmod_pad, (0, cidx * AW), (16, AW)))

    def own(blocks, swapped=False):
        return lax.dynamic_index_in_dim(blocks, _slot(cidx, swapped), axis=0, keepdims=False)

    p_in = _sum_chip(dw_in_own, rc_in, "rs_w_in_sum")
    p_out = _sum_chip(own(dw_out), rc_out, "rs_w_out_sum")
    p_up = _sum_chip(own(dw_up, True), rc_up, "rs_w_up_sum")
    p_down = _sum_chip(own(dw_down), rc_down, "rs_w_down_sum")
    q_in, q_out, q_up, q_down = _pair_swap([p_in, p_out, p_up, p_down], "rs_swap")

    out = {}
    out["w_ada"] = _adamw(w_ada[0], g_w_ada, None, m_w_ada[0], v_w_ada[0], "adamw_w_ada")
    out["w_in"] = [t.T for t in _adamw(w_in[0].T, p_in, q_in, m_w_in[0].T, v_w_in[0].T, "adamw_w_in")]
    out["w_out"] = _adamw(w_out[0], p_out, q_out, m_w_out[0], v_w_out[0], "adamw_w_out")
    out["w_up"] = _adamw(w_up[0], p_up, q_up, m_w_up[0], v_w_up[0], "adamw_w_up")
    out["w_down"] = _adamw(w_down[0], p_down, q_down, m_w_down[0], v_w_down[0], "adamw_w_down")
    small_names = ["b_ada", "g_norm1", "w_fg2", "b_fg2", "g_gla_out", "g_norm2", "w_conv", "b_conv", "g_final"]
    small_w = [b_ada, g_norm1, w_fg2, b_fg2, g_gla_out, g_norm2, w_conv, b_conv, g_final]
    small_m = [m_b_ada, m_g_norm1, m_w_fg2, m_b_fg2, m_g_gla_out, m_g_norm2, m_w_conv, m_b_conv, m_g_final]
    small_v = [v_b_ada, v_g_norm1, v_w_fg2, v_b_fg2, v_g_gla_out, v_g_norm2, v_w_conv, v_b_conv, v_g_final]
    small_gr = [g_b_ada, g_g1, g_wfg, g_bfg, g_ggla, g_g2, g_wconv, g_bconv, g_gf]

    def pack(arrs):
        flat = jnp.concatenate([t.reshape(-1) for t in arrs])
        return jnp.concatenate([flat, jnp.zeros((-flat.shape[0]) % 1024, F32)]).reshape(-1, 128)

    packed = _adamw(pack(small_w), pack(small_gr), None, pack(small_m), pack(small_v), "adamw_small")
    off = 0
    for nm, wt in zip(small_names, small_w):
        n = wt.size
        out[nm] = [t.reshape(-1)[off:off + n].reshape(wt.shape) for t in packed]
        off += n
    for nm in ("w_ada", "w_in", "w_out", "w_up", "w_down"):
        out[nm] = [t[None] for t in out[nm]]

    names = ["w_ada", "b_ada", "g_norm1", "w_in", "w_fg2", "b_fg2", "g_gla_out", "w_out", "g_norm2", "w_up",
             "w_conv", "b_conv", "w_down", "g_final"]
    res = [loss, gx[None]]
    for k in range(4):
        res += [out[nm][k] for nm in names]
    return tuple(res)
```

```python
import functools

import jax
import jax.numpy as jnp
from jax import lax
from jax.experimental import pallas as pl
from jax.experimental.pallas import tpu as pltpu

F32 = jnp.float32
BF16 = jnp.bfloat16
MESH = pl.DeviceIdType.MESH

D = 1024
H_SB = 8
DK = 64
DV = 128
H_GLA = 4
CHUNK = 64
RANK = 16
N_IN = 3088
N_IN_P = 3200
D_FF = 2816
HB = D_FF // 2
LANES = 128
EPS = 1e-6
QB = 128
SB_SKIP = -120.0
SB_HEAD_ROWS = 64

LR, B1, B2, EPS_A, WD, STEP = 0.001, 0.9, 0.999, 1e-08, 0.01, 10

ANY = pl.BlockSpec(memory_space=pl.ANY)
VMEM_SPEC = pl.BlockSpec(memory_space=pltpu.VMEM)
ONE_BUF = pl.Buffered(1)


def _cp(ndim=0, vmem_mb=None):
    kw = {}
    if ndim:
        kw["dimension_semantics"] = ("arbitrary",) * ndim
    if vmem_mb:
        kw["vmem_limit_bytes"] = vmem_mb * 1024 * 1024
    return pltpu.CompilerParams(**kw)


def _dot(a, b):
    return jnp.dot(a, b, preferred_element_type=F32)


def _dot_nt(a, b):
    return lax.dot_general(a, b, (((1,), (1,)), ((), ())), preferred_element_type=F32)


def _dot_tn(a, b):
    return lax.dot_general(a, b, (((0,), (0,)), ((), ())), preferred_element_type=F32)


def _split(x):
    hi = x.astype(BF16)
    lo = (x - hi.astype(F32)).astype(BF16)
    return hi, lo


def _sigmoid(x):
    return jax.nn.sigmoid(x)


def _sigmoid_fast(x):
    return pl.reciprocal(1.0 + jnp.exp(-x), approx=True)


def _log_sigmoid_parts(z):
    e = jnp.exp(-jnp.abs(z))
    sp = jnp.log1p(e)
    return -(jnp.maximum(z, 0.0) + sp), jnp.minimum(z, 0.0) - sp, e


def _tile2d(rows, cols, budget=512 * 1024):
    best = None
    for t in range(8, rows + 1, 8):
        if rows % t == 0 and t * cols * 4 <= budget:
            best = t
    if best is not None:
        return best, cols
    best = LANES if cols % LANES == 0 else cols
    for t in range(LANES, cols + 1, LANES):
        if cols % t == 0 and rows * t * 4 <= budget:
            best = t
    return rows, best


def _flip(v, bit):
    return 1 - v if bit else v


def _allgather8(a, name):
    def body(a_ref, o_ref, ssem, rsem, lsem):
        x, y, c = lax.axis_index("x"), lax.axis_index("y"), lax.axis_index("c")
        me = 4 * x + 2 * y + c
        loc = pltpu.make_async_copy(a_ref, o_ref.at[me], lsem)
        loc.start()
        sends = []
        for r in range(1, 8):
            peer = (_flip(x, r & 4), _flip(y, r & 2), _flip(c, r & 1))
            cp = pltpu.make_async_remote_copy(
                src_ref=a_ref, dst_ref=o_ref.at[me], send_sem=ssem.at[r - 1], recv_sem=rsem.at[r - 1],
                device_id=peer, device_id_type=MESH)
            cp.start()
            sends.append(cp)
        for r in range(1, 8):
            peer = (_flip(x, r & 4), _flip(y, r & 2), _flip(c, r & 1))
            pidx = 4 * peer[0] + 2 * peer[1] + peer[2]
            pltpu.make_async_remote_copy(
                src_ref=a_ref, dst_ref=o_ref.at[pidx], send_sem=ssem.at[r - 1], recv_sem=rsem.at[r - 1],
                device_id=peer, device_id_type=MESH).wait_recv()
        for cp in sends:
            cp.wait_send()
        loc.wait()

    return pl.pallas_call(
        body, name=name,
        out_shape=jax.ShapeDtypeStruct((8,) + a.shape, a.dtype),
        in_specs=[VMEM_SPEC], out_specs=VMEM_SPEC,
        scratch_shapes=[pltpu.SemaphoreType.DMA((7,)), pltpu.SemaphoreType.DMA((7,)), pltpu.SemaphoreType.DMA],
    )(a)


def _allgather4(a, name):
    def body(a_ref, o_ref, ssem, rsem, lsem):
        x, y, c = lax.axis_index("x"), lax.axis_index("y"), lax.axis_index("c")
        me = 2 * x + y
        loc = pltpu.make_async_copy(a_ref, o_ref.at[me], lsem)
        loc.start()
        sends = []
        for r in range(1, 4):
            peer = (_flip(x, r & 2), _flip(y, r & 1), c)
            cp = pltpu.make_async_remote_copy(
                src_ref=a_ref, dst_ref=o_ref.at[me], send_sem=ssem.at[r - 1], recv_sem=rsem.at[r - 1],
                device_id=peer, device_id_type=MESH)
            cp.start()
            sends.append(cp)
        for r in range(1, 4):
            peer = (_flip(x, r & 2), _flip(y, r & 1), c)
            pidx = 2 * peer[0] + peer[1]
            pltpu.make_async_remote_copy(
                src_ref=a_ref, dst_ref=o_ref.at[pidx], send_sem=ssem.at[r - 1], recv_sem=rsem.at[r - 1],
                device_id=peer, device_id_type=MESH).wait_recv()
        for cp in sends:
            cp.wait_send()
        loc.wait()

    return pl.pallas_call(
        body, name=name,
        out_shape=jax.ShapeDtypeStruct((4,) + a.shape, a.dtype),
        in_specs=[ANY], out_specs=ANY,
        scratch_shapes=[pltpu.SemaphoreType.DMA((3,)), pltpu.SemaphoreType.DMA((3,)), pltpu.SemaphoreType.DMA],
    )(a)


def _allgather4_split(a, name):
    R, C = a.shape
    hc = C // 2

    def body(a_ref, o_ref, ssem, rsem, fssem, frsem, lsem):
        x, y, c = lax.axis_index("x"), lax.axis_index("y"), lax.axis_index("c")
        me = 2 * x + y
        sibling = (x, y, 1 - c)
        mine = pl.ds(pl.multiple_of(c * hc, hc), hc)
        theirs = pl.ds(pl.multiple_of((1 - c) * hc, hc), hc)
        loc = pltpu.make_async_copy(a_ref, o_ref.at[me], lsem)
        loc.start()
        peers = [(_flip(x, r & 2), _flip(y, r & 1), c) for r in range(1, 4)]
        pidx = [2 * p[0] + p[1] for p in peers]
        sends = []
        for k in range(3):
            cp = pltpu.make_async_remote_copy(
                src_ref=a_ref.at[:, mine], dst_ref=o_ref.at[me, :, mine], send_sem=ssem.at[k], recv_sem=rsem.at[k],
                device_id=peers[k], device_id_type=MESH)
            cp.start()
            sends.append(cp)
        for k in range(3):
            landed = o_ref.at[pidx[k], :, mine]
            pltpu.make_async_remote_copy(
                src_ref=landed, dst_ref=landed, send_sem=ssem.at[k], recv_sem=rsem.at[k],
                device_id=peers[k], device_id_type=MESH).wait_recv()
            cp = pltpu.make_async_remote_copy(
                src_ref=landed, dst_ref=landed, send_sem=fssem.at[k], recv_sem=frsem.at[k],
                device_id=sibling, device_id_type=MESH)
            cp.start()
            sends.append(cp)
        for k in range(3):
            got = o_ref.at[pidx[k], :, theirs]
            pltpu.make_async_remote_copy(
                src_ref=got, dst_ref=got, send_sem=fssem.at[k], recv_sem=frsem.at[k],
                device_id=sibling, device_id_type=MESH).wait_recv()
        for cp in sends:
            cp.wait_send()
        loc.wait()

    return pl.pallas_call(
        body, name=name,
        out_shape=jax.ShapeDtypeStruct((4,) + a.shape, a.dtype),
        in_specs=[ANY], out_specs=ANY,
        scratch_shapes=[pltpu.SemaphoreType.DMA((3,))] * 4 + [pltpu.SemaphoreType.DMA],
    )(a)


def _slot(chip, swapped):
    return 2 * (chip % 2) + chip // 2 if swapped else chip


def _pair_swap(ps, name):
    n = len(ps)

    def body(*refs):
        x, y, c = lax.axis_index("x"), lax.axis_index("y"), lax.axis_index("c")
        ssem, rsem = refs[2 * n], refs[2 * n + 1]
        cps = [pltpu.make_async_remote_copy(
            src_ref=refs[k], dst_ref=refs[n + k], send_sem=ssem.at[k], recv_sem=rsem.at[k],
            device_id=(x, y, 1 - c), device_id_type=MESH) for k in range(n)]
        for cp in cps:
            cp.start()
        for cp in cps:
            cp.wait()

    return pl.pallas_call(
        body, name=name,
        out_shape=[jax.ShapeDtypeStruct(p.shape, p.dtype) for p in ps],
        in_specs=[ANY] * n, out_specs=[ANY] * n,
        scratch_shapes=[pltpu.SemaphoreType.DMA((n,)), pltpu.SemaphoreType.DMA((n,))],
    )(*ps)


def _carried_copies(kind, src_ref, dst_ref, sems, swapped):
    ssem, rsem, lsem = sems
    x, y, c = lax.axis_index("x"), lax.axis_index("y"), lax.axis_index("c")
    me = 2 * x + y
    starts, recvs = [], []
    if kind == "gather":
        starts.append(pltpu.make_async_copy(src_ref, dst_ref.at[me], lsem))
    for r in range(1, 4):
        peer = (_flip(x, r & 2), _flip(y, r & 1), c)
        pidx = 2 * peer[0] + peer[1]
        if kind == "gather":
            src, dst, landed = src_ref, dst_ref.at[me], dst_ref.at[pidx]
        else:
            src = src_ref.at[2 * peer[1] + peer[0] if swapped else pidx]
            dst = landed = dst_ref.at[r - 1]
        starts.append(pltpu.make_async_remote_copy(
            src_ref=src, dst_ref=dst, send_sem=ssem.at[r - 1], recv_sem=rsem.at[r - 1],
            device_id=peer, device_id_type=MESH))
        recvs.append(pltpu.make_async_remote_copy(
            src_ref=src, dst_ref=landed, send_sem=ssem.at[r - 1], recv_sem=rsem.at[r - 1],
            device_id=peer, device_id_type=MESH))
    return starts, recvs


def _call(body, carried, operands, *, name, grid, in_specs, out_specs, out_shape, scratch_shapes=(),
          compiler_params=None):
    single = not isinstance(out_shape, (list, tuple))
    out_specs = [out_specs] if single else list(out_specs)
    out_shape = [out_shape] if single else list(out_shape)
    n_in, n_out, n_sc, nh = len(operands), len(out_shape), len(scratch_shapes), len(carried)

    def full(*refs):
        ins, h_in = refs[:n_in], refs[n_in:n_in + nh]
        o0 = n_in + nh
        outs, h_out = refs[o0:o0 + n_out], refs[o0 + n_out:o0 + n_out + nh]
        s0 = o0 + n_out + nh
        scratch, sems = refs[s0:s0 + n_sc], refs[s0 + n_sc:]
        first = last = None
        for d in range(len(grid)):
            f = pl.program_id(d) == 0
            l = pl.program_id(d) == pl.num_programs(d) - 1
            first = f if first is None else jnp.logical_and(first, f)
            last = l if last is None else jnp.logical_and(last, l)

        def copies(t):
            return _carried_copies(carried[t][0], h_in[t], h_out[t], sems[3 * t:3 * t + 3], carried[t][2])

        if nh:
            @pl.when(first)
            def _():
                for t in range(nh):
                    for cp in copies(t)[0]:
                        cp.start()

        body(*ins, *outs, *scratch)

        if nh:
            @pl.when(last)
            def _():
                for t in range(nh):
                    starts, recvs = copies(t)
                    for cp in recvs:
                        cp.wait_recv()
                    for cp in starts:
                        if carried[t][0] == "gather" and cp is starts[0]:
                            cp.wait()
                        else:
                            cp.wait_send()

    h_shapes = [jax.ShapeDtypeStruct(((4,) + arr.shape) if kind == "gather" else ((3,) + arr.shape[1:]), arr.dtype)
                for kind, arr, _ in carried]
    sem_shapes = [pltpu.SemaphoreType.DMA((3,)), pltpu.SemaphoreType.DMA((3,)), pltpu.SemaphoreType.DMA] * nh
    res = pl.pallas_call(
        full, name=name, grid=grid, in_specs=list(in_specs) + [ANY] * nh, out_specs=out_specs + [ANY] * nh,
        out_shape=out_shape + h_shapes, scratch_shapes=list(scratch_shapes) + sem_shapes,
        compiler_params=compiler_params,
    )(*operands, *[arr for _, arr, _ in carried])
    main = res[:n_out]
    return (main[0] if single else main), list(res[n_out:])


def _mm(pairs, out_dtype, name, tm, tn, vmem_mb=None, carried=()):
    S = pairs[0][0].shape[0]
    N = pairs[0][1].shape[1]
    tm = min(tm, S)
    np_ = len(pairs)

    def body(*refs):
        acc = _dot(refs[0][...], refs[1][...])
        for t in range(1, np_):
            acc = acc + _dot(refs[2 * t][...], refs[2 * t + 1][...])
        refs[-1][...] = acc.astype(refs[-1].dtype)

    in_specs, ops = [], []
    for a, w in pairs:
        in_specs += [pl.BlockSpec((tm, a.shape[1]), lambda n, i: (i, 0)),
                     pl.BlockSpec((w.shape[0], tn), lambda n, i: (0, n))]
        ops += [a, w]
    out, got = _call(
        body, carried, ops, name=name, grid=(N // tn, S // tm), in_specs=in_specs,
        out_specs=pl.BlockSpec((tm, tn), lambda n, i: (i, n)),
        out_shape=jax.ShapeDtypeStruct((S, N), out_dtype),
        compiler_params=_cp(2, vmem_mb))
    return (out, got) if carried else out


def _mm_tn(a_list, b, name, bn, tk, vmem_mb=None):
    S, N = b.shape
    ms = [a.shape[1] for a in a_list]
    M = sum(ms)
    tk = min(tk, S)
    na = len(a_list)

    def body(*refs):
        b_ref, o_ref, o16_ref = refs[na], refs[na + 1], refs[na + 2]

        @pl.when(pl.program_id(1) == 0)
        def _():
            o_ref[...] = jnp.zeros_like(o_ref)
        off = 0
        for t in range(na):
            o_ref[off:off + ms[t], :] += _dot_tn(refs[t][...], b_ref[...])
            off += ms[t]

        @pl.when(pl.program_id(1) == pl.num_programs(1) - 1)
        def _():
            o16_ref[...] = o_ref[...].astype(BF16)

    spec = pl.BlockSpec((None, M, bn), lambda n, k: (n, 0, 0), pipeline_mode=ONE_BUF)
    return pl.pallas_call(
        body, name=name, grid=(N // bn, S // tk),
        in_specs=[pl.BlockSpec((tk, m), lambda n, k: (k, 0)) for m in ms] + [pl.BlockSpec((tk, bn), lambda n, k: (k, n))],
        out_specs=[spec, spec],
        out_shape=[jax.ShapeDtypeStruct((N // bn, M, bn), F32), jax.ShapeDtypeStruct((N // bn, M, bn), BF16)],
        compiler_params=_cp(2, vmem_mb),
    )(*a_list, b)


def _ada_fwd(c_all, w_sh, b_sh):
    def body(c_ref, w_ref, b_ref, o_ref):
        cv = c_ref[...]
        sc = (cv * _sigmoid(cv)).astype(BF16)
        o_ref[...] = _dot(sc, w_ref[...].astype(BF16)) + b_ref[...]

    return pl.pallas_call(
        body, name="ada_fwd", out_shape=jax.ShapeDtypeStruct((c_all.shape[0], w_sh.shape[1]), F32),
        in_specs=[VMEM_SPEC] * 3, out_specs=VMEM_SPEC, compiler_params=_cp(0, 40),
    )(c_all, w_sh, b_sh)


def _ada_bwd(c_all, dmod_sh):
    def body(c_ref, d_ref, o_ref):
        cv = c_ref[...]
        sc = (cv * _sigmoid(cv)).astype(BF16)
        o_ref[...] = _dot_tn(sc, d_ref[...].astype(BF16))

    return pl.pallas_call(
        body, name="ada_bwd", out_shape=jax.ShapeDtypeStruct((c_all.shape[1], dmod_sh.shape[1]), F32),
        in_specs=[VMEM_SPEC] * 2, out_specs=VMEM_SPEC, compiler_params=_cp(0, 40),
    )(c_all, dmod_sh)


def _vec(tm_unused=None):
    return pl.BlockSpec((1, D), lambda i: (0, 0))


def _rows(tm, width=D):
    return pl.BlockSpec((tm, width), lambda i: (i, 0))


def _norm_mod_mm(x, g, shift, scale, w, tm=256, carried=()):
    S = x.shape[0]
    tm = min(tm, S)
    N = w.shape[1]

    def body(x_ref, g_ref, sh_ref, sc_ref, w_ref, h_ref, p_ref):
        xv = x_ref[...]
        r = lax.rsqrt(jnp.mean(xv * xv, axis=-1, keepdims=True) + EPS)
        hn = (xv * r) * g_ref[...]
        h = (hn * (1.0 + sc_ref[...]) + sh_ref[...]).astype(BF16)
        h_ref[...] = h
        p_ref[...] = _dot(h, w_ref[...]).astype(BF16)

    return _call(
        body, carried, [x, g, shift, scale, w], name="norm1_mod_mm_in", grid=(S // tm,),
        in_specs=[_rows(tm), _vec(), _vec(), _vec(), pl.BlockSpec(w.shape, lambda i: (0, 0), pipeline_mode=ONE_BUF)],
        out_specs=[_rows(tm), _rows(tm, N)],
        out_shape=[jax.ShapeDtypeStruct((S, D), BF16), jax.ShapeDtypeStruct((S, N), BF16)],
        compiler_params=_cp(1, 48))


def _mm_rows(pairs, tm):
    ops, specs = [], []
    for a, w in pairs:
        ops += [a, w]
        specs += [pl.BlockSpec((tm, a.shape[1]), lambda i: (i, 0)),
                  pl.BlockSpec(w.shape, lambda i: (0, 0), pipeline_mode=ONE_BUF)]
    return ops, specs


def _mm_rows_value(refs, npairs):
    acc = _dot(refs[0][...], refs[1][...])
    for t in range(1, npairs):
        acc = acc + _dot(refs[2 * t][...], refs[2 * t + 1][...])
    return acc


def _resid_norm_mod(x, mm, gate, g, shift, scale, tm=256):
    S = x.shape[0]
    tm = min(tm, S)
    skip = 2 * len(mm)

    def body(*refs):
        x_ref, gt_ref, g_ref, sh_ref, sc_ref, x1_ref, h_ref, m_ref = refs[skip:]
        mixed = _mm_rows_value(refs, len(mm))
        m_ref[...] = mixed
        x1 = x_ref[...] + (1.0 + gt_ref[...]) * mixed
        x1_ref[...] = x1
        r = lax.rsqrt(jnp.mean(x1 * x1, axis=-1, keepdims=True) + EPS)
        hn = (x1 * r) * g_ref[...]
        h_ref[...] = (hn * (1.0 + sc_ref[...]) + sh_ref[...]).astype(BF16)

    ops, specs = _mm_rows(mm, tm)
    return pl.pallas_call(
        body, name="mm_out_resid_norm2_mod", grid=(S // tm,),
        in_specs=specs + [_rows(tm), _vec(), _vec(), _vec(), _vec()],
        out_specs=[_rows(tm), _rows(tm), _rows(tm)],
        out_shape=[jax.ShapeDtypeStruct((S, D), F32), jax.ShapeDtypeStruct((S, D), BF16),
                   jax.ShapeDtypeStruct((S, D), F32)],
        compiler_params=_cp(1, 40),
    )(*ops, x, gate, g, shift, scale)


def _conv3(ext, w_ref, b_ref, cs):
    e1 = pltpu.roll(ext, 1, 0)
    e2 = pltpu.roll(ext, 2, 0)
    u = b_ref[:, cs] + w_ref[0:1, cs] * e2
    u = u + w_ref[1:2, cs] * e1
    u = u + w_ref[2:3, cs] * ext
    return u, e1, e2


def _up_conv_glu(h2, w_up_p, wc_p, bc_p, tm=256):
    S = h2.shape[0]
    tm = min(tm, S)
    widths = [2 * LANES] * (HB // (2 * LANES)) + ([LANES] if HB % (2 * LANES) else [])

    def body(h_ref, wu_ref, w_ref, b_ref, u_ref, a_ref, prev_ref):
        @pl.when(pl.program_id(0) == 0)
        def _():
            prev_ref[...] = jnp.zeros_like(prev_ref)

        hv = h_ref[...]
        for j in range(2):
            base = 0
            for wd in widths:
                us = []
                for off in (2 * j * HB + base, 2 * j * HB + HB + base):
                    cb = _dot(hv, wu_ref[:, off:off + wd]).astype(BF16)
                    u_ref[:, off:off + wd] = cb
                    for q in range(wd // LANES):
                        cs = slice(off + q * LANES, off + (q + 1) * LANES)
                        cq = cb[:, q * LANES:(q + 1) * LANES]
                        ext = jnp.concatenate([prev_ref[:, cs].astype(F32), cq.astype(F32)], axis=0)
                        us.append(_conv3(ext, w_ref, b_ref, cs)[0][16:])
                        prev_ref[:, cs] = cq[tm - 16:]
                nq = wd // LANES
                for q in range(nq):
                    val, gt = us[q], us[nq + q]
                    a_ref[:, j * HB + base + q * LANES:j * HB + base + (q + 1) * LANES] = (
                        val * (gt * _sigmoid_fast(gt))).astype(BF16)
                base += wd

    return pl.pallas_call(
        body, name="mm_up_conv_glu", grid=(S // tm,),
        in_specs=[pl.BlockSpec((tm, D), lambda i: (i, 0)),
                  pl.BlockSpec((D, 2 * D_FF), lambda i: (0, 0), pipeline_mode=ONE_BUF),
                  pl.BlockSpec((3, 2 * D_FF), lambda i: (0, 0)),
                  pl.BlockSpec((1, 2 * D_FF), lambda i: (0, 0))],
        out_specs=[pl.BlockSpec((tm, 2 * D_FF), lambda i: (i, 0)), pl.BlockSpec((tm, D_FF), lambda i: (i, 0))],
        out_shape=[jax.ShapeDtypeStruct((S, 2 * D_FF), BF16), jax.ShapeDtypeStruct((S, D_FF), BF16)],
        scratch_shapes=[pltpu.VMEM((16, 2 * D_FF), BF16)],
        compiler_params=_cp(1, 48),
    )(h2, w_up_p, wc_p, bc_p)


def _conv_glu_up_norm2_bwd(da, u0p, wc_p, bc_p, w_up_t, x1, dx2, g, scale, mixed, gate, tm=256, carried=()):
    S = u0p.shape[0]
    tm = min(tm, S)
    hb = tm // 16
    nlast = S // 16 - 1
    widths = [2 * LANES] * (HB // (2 * LANES)) + ([LANES] if HB % (2 * LANES) else [])

    def body(da_ref, dan_ref, u_ref, p_ref, n_ref, w_ref, b_ref, wt_ref, x_ref, dr_ref, g_ref, sc_ref, m_ref, gt_ref,
             o_ref, s_ref, dx_ref, dm_ref, s2_ref, acc_ref):
        i = pl.program_id(0)
        first = i == 0
        last = i == pl.num_programs(0) - 1

        @pl.when(first)
        def _():
            s_ref[...] = jnp.zeros_like(s_ref)
            s2_ref[...] = jnp.zeros_like(s2_ref)

        n = tm + 16
        started = False
        for j in range(2):
            base = 0
            for wd in widths:
                du0s = ([], [])
                for q in range(wd // LANES):
                    k0 = base + q * LANES
                    kc = slice(j * HB + k0, j * HB + k0 + LANES)
                    dae = jnp.concatenate([da_ref[:, kc].astype(F32),
                                           jnp.where(last, 0.0, dan_ref[:, kc].astype(F32))], axis=0)
                    halves = []
                    for off in (2 * j * HB + k0, 2 * j * HB + HB + k0):
                        cs = slice(off, off + LANES)
                        ext = jnp.concatenate([jnp.where(first, 0.0, p_ref[:, cs].astype(F32)),
                                               u_ref[:, cs].astype(F32), n_ref[:, cs].astype(F32)], axis=0)
                        u, e1, e2 = _conv3(ext, w_ref, b_ref, cs)
                        halves.append((u[16:], ext[16:16 + tm], e1[16:16 + tm], e2[16:16 + tm], cs))
                    val, gt = halves[0][0], halves[1][0]
                    sg = _sigmoid_fast(gt)
                    dus = (dae * (gt * sg), dae * val * (sg * (1.0 + gt * (1.0 - sg))))
                    for t, (du, (_, x0, x1_, x2, cs)) in enumerate(zip(dus, halves)):
                        du0 = (w_ref[2:3, cs] * du + w_ref[1:2, cs] * pltpu.roll(du, n - 1, 0)
                               + w_ref[0:1, cs] * pltpu.roll(du, n - 2, 0))[:tm].astype(BF16)
                        o_ref[:, cs] = du0
                        du0s[t].append(du0)
                        dut = du[:tm]
                        s_ref[0:1, cs] += jnp.sum(dut, axis=0, keepdims=True)
                        s_ref[1:2, cs] += jnp.sum(dut * x2, axis=0, keepdims=True)
                        s_ref[2:3, cs] += jnp.sum(dut * x1_, axis=0, keepdims=True)
                        s_ref[3:4, cs] += jnp.sum(dut * x0, axis=0, keepdims=True)
                for t, off in enumerate((2 * j * HB + base, 2 * j * HB + HB + base)):
                    lhs = du0s[t][0] if len(du0s[t]) == 1 else jnp.concatenate(du0s[t], axis=1)
                    part = _dot(lhs, wt_ref[off:off + wd, :])
                    if started:
                        acc_ref[...] += part
                    else:
                        acc_ref[...] = part
                        started = True
                base += wd

        xv = x_ref[...]
        dhv = acc_ref[...]
        r = lax.rsqrt(jnp.mean(xv * xv, axis=-1, keepdims=True) + EPS)
        nv = xv * r
        gv = g_ref[...]
        hn = nv * gv
        dhn = dhv * (1.0 + sc_ref[...])
        dn = dhn * gv
        dx = dr_ref[...] + r * (dn - nv * jnp.mean(dn * nv, axis=-1, keepdims=True))
        dx_ref[...] = dx
        dm_ref[...] = (dx * (1.0 + gt_ref[...])).astype(BF16)
        s2_ref[0:1, :] += jnp.sum(dhv, axis=0, keepdims=True)
        s2_ref[1:2, :] += jnp.sum(dhv * hn, axis=0, keepdims=True)
        s2_ref[2:3, :] += jnp.sum(dhn * nv, axis=0, keepdims=True)
        s2_ref[3:4, :] += jnp.sum(dx * m_ref[...], axis=0, keepdims=True)

    full = 2 * D_FF
    return _call(
        body, carried, [da, da, u0p, u0p, u0p, wc_p, bc_p, w_up_t, x1, dx2, g, scale, mixed, gate],
        name="conv_glu_up_norm2_bwd", grid=(S // tm,),
        in_specs=[pl.BlockSpec((tm, D_FF), lambda i: (i, 0)),
                  pl.BlockSpec((16, D_FF), lambda i: (jnp.minimum((i + 1) * hb, nlast), 0)),
                  pl.BlockSpec((tm, full), lambda i: (i, 0)),
                  pl.BlockSpec((16, full), lambda i: (jnp.maximum(i * hb - 1, 0), 0)),
                  pl.BlockSpec((16, full), lambda i: (jnp.minimum((i + 1) * hb, nlast), 0)),
                  pl.BlockSpec((3, full), lambda i: (0, 0)),
                  pl.BlockSpec((1, full), lambda i: (0, 0)),
                  pl.BlockSpec((full, D), lambda i: (0, 0), pipeline_mode=ONE_BUF),
                  _rows(tm), _rows(tm), _vec(), _vec(), _rows(tm), _vec()],
        out_specs=[pl.BlockSpec((tm, full), lambda i: (i, 0)), pl.BlockSpec((8, full), lambda i: (0, 0)),
                   _rows(tm), _rows(tm), pl.BlockSpec((8, D), lambda i: (0, 0))],
        out_shape=[jax.ShapeDtypeStruct((S, full), BF16), jax.ShapeDtypeStruct((8, full), F32),
                   jax.ShapeDtypeStruct((S, D), F32), jax.ShapeDtypeStruct((S, D), BF16),
                   jax.ShapeDtypeStruct((8, D), F32)],
        scratch_shapes=[pltpu.VMEM((tm, D), F32)],
        compiler_params=_cp(1, 56))


def _final_loss(x1, mm, gate2, g_final, target, tm=256):
    S = x1.shape[0]
    tm = min(tm, S)
    skip = 2 * len(mm)

    def body(*refs):
        x1_ref, gt_ref, g_ref, t_ref, dx_ref, dy_ref, s_ref = refs[skip:]

        @pl.when(pl.program_id(0) == 0)
        def _():
            s_ref[...] = jnp.zeros_like(s_ref)

        y2 = _mm_rows_value(refs, len(mm))
        og = 1.0 + gt_ref[...]
        x2 = x1_ref[...] + og * y2
        r = lax.rsqrt(jnp.mean(x2 * x2, axis=-1, keepdims=True) + EPS)
        n = x2 * r
        g = g_ref[...]
        err = n * g - t_ref[...]
        dy = err * (1.0 / D)
        dn = dy * g
        dx2 = r * (dn - n * jnp.mean(dn * n, axis=-1, keepdims=True))
        dx_ref[...] = dx2
        dy_ref[...] = (dx2 * og).astype(BF16)
        s_ref[0:1, :] += jnp.sum(dy * n, axis=0, keepdims=True)
        s_ref[1:2, :] += jnp.sum(dx2 * y2, axis=0, keepdims=True)
        s_ref[2:3, :] += jnp.sum(err * err, axis=0, keepdims=True)

    ops, specs = _mm_rows(mm, tm)
    return pl.pallas_call(
        body, name="mm_down_final_loss", grid=(S // tm,),
        in_specs=specs + [_rows(tm), _vec(), _vec(), _rows(tm)],
        out_specs=[_rows(tm), _rows(tm), pl.BlockSpec((8, D), lambda i: (0, 0))],
        out_shape=[jax.ShapeDtypeStruct((S, D), F32), jax.ShapeDtypeStruct((S, D), BF16),
                   jax.ShapeDtypeStruct((8, D), F32)],
        compiler_params=_cp(1, 40),
    )(*ops, x1, gate2, g_final, target)


def _norm_mod_bwd(dh, xin, dres, g, scale, mixed, gate, name, tm=256, carried=()):
    S = xin.shape[0]
    tm = min(tm, S)
    with_gate = mixed is not None
    fused = isinstance(dh, list)
    skip = 2 * len(dh) if fused else 1

    def body(*refs):
        if with_gate:
            x_ref, dr_ref, g_ref, sc_ref, m_ref, gt_ref, dx_ref, dm_ref, s_ref = refs[skip:]
        else:
            x_ref, dr_ref, g_ref, sc_ref, dx_ref, s_ref = refs[skip:]

        @pl.when(pl.program_id(0) == 0)
        def _():
            s_ref[...] = jnp.zeros_like(s_ref)

        xv = x_ref[...]
        dhv = _mm_rows_value(refs, len(dh)) if fused else refs[0][...]
        r = lax.rsqrt(jnp.mean(xv * xv, axis=-1, keepdims=True) + EPS)
        n = xv * r
        g = g_ref[...]
        hn = n * g
        dhn = dhv * (1.0 + sc_ref[...])
        dn = dhn * g
        dx = dr_ref[...] + r * (dn - n * jnp.mean(dn * n, axis=-1, keepdims=True))
        dx_ref[...] = dx
        s_ref[0:1, :] += jnp.sum(dhv, axis=0, keepdims=True)
        s_ref[1:2, :] += jnp.sum(dhv * hn, axis=0, keepdims=True)
        s_ref[2:3, :] += jnp.sum(dhn * n, axis=0, keepdims=True)
        if with_gate:
            dm_ref[...] = (dx * (1.0 + gt_ref[...])).astype(BF16)
            s_ref[3:4, :] += jnp.sum(dx * m_ref[...], axis=0, keepdims=True)

    ins, in_specs = _mm_rows(dh, tm) if fused else ([dh], [_rows(tm)])
    ins += [xin, dres, g, scale]
    in_specs += [_rows(tm), _rows(tm), _vec(), _vec()]
    out_specs = [_rows(tm)]
    out_shape = [jax.ShapeDtypeStruct((S, D), F32)]
    if with_gate:
        ins += [mixed, gate]
        in_specs += [_rows(tm), _vec()]
        out_specs.append(_rows(tm))
        out_shape.append(jax.ShapeDtypeStruct((S, D), BF16))
    out_specs.append(pl.BlockSpec((8, D), lambda i: (0, 0)))
    out_shape.append(jax.ShapeDtypeStruct((8, D), F32))
    return _call(body, carried, ins, name=name, grid=(S // tm,), in_specs=in_specs, out_specs=out_specs,
                 out_shape=out_shape, compiler_params=_cp(1, 48 if fused else None))


def _tri(n, rel):
    row = lax.broadcasted_iota(jnp.int32, (n, n), 0)
    col = lax.broadcasted_iota(jnp.int32, (n, n), 1)
    return {"gt": row > col, "ge": row >= col, "lt": row < col, "le": row <= col}[rel]


def _pair_diag(mask):
    u = jnp.where(mask, 1.0, 0.0).astype(BF16)
    z = jnp.zeros_like(u)
    return jnp.concatenate([jnp.concatenate([u, z], axis=1), jnp.concatenate([z, u], axis=1)], axis=0)


def _pair_rows(xp, lo_half):
    z = jnp.zeros_like(xp)
    return jnp.concatenate([jnp.where(lo_half, xp, z), jnp.where(lo_half, z, xp)], axis=0)


def _sb_scores(z, causal, diag):
    ls, ps, es = [], [], []
    for hh in range(2):
        zz = z[:, hh * QB:(hh + 1) * QB]
        e = jnp.exp(-jnp.abs(zz))
        l = -(jnp.maximum(zz, 0.0) + jnp.log(1.0 + e))
        ps.append(l + zz)
        ls.append(jnp.where(causal, l, 0.0) if diag else l)
        es.append(e)
    return ls, ps, es


def _sb_fwd(proj, carried=()):
    S = proj.shape[0]
    nq = S // QB

    def body(q_ref, k_ref, v_ref, o_ref, t_ref, c_ref, acc_ref, qs_ref):
        i = pl.program_id(0)
        causal = _tri(QB, "gt")
        usuf = _pair_diag(_tri(QB, "gt"))
        lo_half = lax.broadcasted_iota(jnp.int32, (QB, 128), 1) < DK
        qs_ref[...] = q_ref[...] * 0.125

        def block(j, diag, nr):
            rows = pl.ds(pl.multiple_of(j * QB, QB), QB)
            rs = slice(0, nr)
            pairs = range(H_SB // 2)
            cols = [slice(pr * 128, (pr + 1) * 128) for pr in pairs]
            zs = [_dot_nt(qs_ref[rs, cols[pr]], _pair_rows(k_ref[rows, cols[pr]], lo_half)) for pr in pairs]
            sc = [_sb_scores(zs[pr], causal, diag) for pr in pairs]
            sufs = []
            for pr in pairs:
                lh, ll = _split(jnp.concatenate(sc[pr][0], axis=1))
                sufs.append(_dot(lh, usuf) + _dot(ll, usuf))
            cmax = None
            wps = []
            for pr in pairs:
                ws = []
                for hh in range(2):
                    h = 2 * pr + hh
                    b = sufs[pr][:, hh * QB:(hh + 1) * QB]
                    if not diag:
                        b = b + c_ref[h, rs, 0:1]
                    w = jnp.exp(sc[pr][1][hh] + b)
                    ws.append((jnp.where(causal, w, 0.0) if diag else w).astype(BF16))
                    cn = b[:, 0:1] + sc[pr][0][hh][:, 0:1]
                    c_ref[h, rs, 0:1] = cn
                    cmax = cn if cmax is None else jnp.maximum(cmax, cn)
                wps.append(jnp.concatenate(ws, axis=1))
            for pr in pairs:
                upd = _dot(wps[pr], _pair_rows(v_ref[rows, cols[pr]], lo_half))
                if diag:
                    acc_ref[rs, cols[pr]] = upd
                else:
                    acc_ref[rs, cols[pr]] += upd
            lo = jnp.max(cmax[:SB_HEAD_ROWS])
            return (jnp.max(cmax[SB_HEAD_ROWS:]) if nr > SB_HEAD_ROWS else None), lo

        def cond_full(st):
            return jnp.logical_and(st[0] >= 0, st[1] > SB_SKIP)

        def step_full(st):
            return (st[0] - 1,) + block(st[0], False, QB)

        def cond_head(st):
            return jnp.logical_and(st[0] >= 0, st[1] > SB_SKIP)

        def step_head(st):
            return st[0] - 1, block(st[0], False, SB_HEAD_ROWS)[1]

        j, _, lo = lax.while_loop(cond_full, step_full, (i - 1,) + block(i, True, QB))
        jfull = j + 1
        j, _ = lax.while_loop(cond_head, step_head, (j, lo))
        o_ref[...] = acc_ref[...].astype(BF16)
        t_ref[...] = jnp.zeros_like(t_ref)
        for h in range(H_SB):
            t_ref[h // 4, :, h % 4:h % 4 + 1] = c_ref[h, :, 0:1]
        t_ref[:, :, 8:9] = jnp.zeros((2, QB, 1), F32) + (j + 1).astype(F32)
        t_ref[:, :, 9:10] = jnp.zeros((2, QB, 1), F32) + jfull.astype(F32)

    return _call(
        body, carried, [proj, proj, proj], name="sb_fwd", grid=(nq,),
        in_specs=[pl.BlockSpec((QB, 512), lambda i: (i, 0)),
                  pl.BlockSpec((S, 512), lambda i: (0, 1), pipeline_mode=ONE_BUF),
                  pl.BlockSpec((S, 512), lambda i: (0, 2), pipeline_mode=ONE_BUF)],
        out_specs=[pl.BlockSpec((QB, 512), lambda i: (i, 0)),
                   pl.BlockSpec((2, QB, 128), lambda i: (0, i, 0))],
        out_shape=[jax.ShapeDtypeStruct((S, 512), BF16), jax.ShapeDtypeStruct((2, S, 128), F32)],
        scratch_shapes=[pltpu.VMEM((H_SB, QB, 128), F32), pltpu.VMEM((QB, 512), F32), pltpu.VMEM((QB, 512), BF16)],
        compiler_params=_cp(1, 40))


def _sb_bwd(proj, dcat, stats, carried=()):
    S = proj.shape[0]
    nq = S // QB

    def body(q_ref, k_ref, v_ref, do_ref, t_ref, dq_ref, dk_ref, dv_ref, dk_acc, dv_acc, dq_acc, pc_ref, qs_ref,
             qt_ref, dot_ref):
        i = pl.program_id(1)

        @pl.when(i == 0)
        def _():
            dk_acc[...] = jnp.zeros_like(dk_acc)
            dv_acc[...] = jnp.zeros_like(dv_acc)

        causal = _tri(QB, "gt")
        uin = _pair_diag(_tri(QB, "le"))
        uex = _pair_diag(_tri(QB, "lt"))
        lo_half = lax.broadcasted_iota(jnp.int32, (QB, 128), 1) < DK
        qs_ref[...] = q_ref[...] * 0.125
        lo_rows = lax.broadcasted_iota(jnp.int32, (128, QB), 0) < DK
        for pr in range(2):
            qt_ref[pr] = (q_ref[:, pr * 128:(pr + 1) * 128].astype(F32) * 0.125).T.astype(BF16)
            dot_ref[pr] = do_ref[:, pr * 128:(pr + 1) * 128].astype(F32).T.astype(BF16)
        pc_ref[...] = jnp.zeros_like(pc_ref)
        dq_acc[...] = jnp.zeros_like(dq_acc)
        jstart = jnp.max(t_ref[:, 8:9]).astype(jnp.int32)
        jfull = jnp.max(t_ref[:, 9:10]).astype(jnp.int32)

        def block(j, diag, nr):
            rows = pl.ds(pl.multiple_of(j * QB, QB), QB)
            rs = slice(0, nr)
            pairs = range(2)
            cols = [slice(pr * 128, (pr + 1) * 128) for pr in pairs]
            kbds = [_pair_rows(k_ref[rows, cols[pr]], lo_half) for pr in pairs]
            zs = [_dot_nt(qs_ref[rs, cols[pr]], kbds[pr]) for pr in pairs]
            dws = [_dot_nt(do_ref[rs, cols[pr]], _pair_rows(v_ref[rows, cols[pr]], lo_half)) for pr in pairs]
            sc = [_sb_scores(zs[pr], causal, diag) for pr in pairs]
            plins = []
            for pr in pairs:
                lh, ll = _split(jnp.concatenate(sc[pr][0], axis=1))
                plins.append(_dot(lh, uin) + _dot(ll, uin))
            wss, gss, gexs = [], [], []
            for pr in pairs:
                ws, gs = [], []
                for hh in range(2):
                    h = 2 * pr + hh
                    half = slice(hh * QB, (hh + 1) * QB)
                    b = (t_ref[rs, h:h + 1] - pc_ref[h, rs, 0:1]) - plins[pr][:, half]
                    w = jnp.exp(sc[pr][1][hh] + b)
                    if diag:
                        w = jnp.where(causal, w, 0.0)
                    ws.append(w)
                    gs.append(dws[pr][:, half] * w)
                wss.append(ws)
                gss.append(gs)
            for pr in pairs:
                gh, gl = _split(jnp.concatenate(gss[pr], axis=1))
                gexs.append(_dot(gh, uex) + _dot(gl, uex))
            dzbs = []
            for pr in pairs:
                dzs = []
                for hh in range(2):
                    h = 2 * pr + hh
                    half = slice(hh * QB, (hh + 1) * QB)
                    e = sc[pr][2][hh]
                    r = pl.reciprocal(1.0 + e, approx=True)
                    er = e * r
                    pos = zs[pr][:, half] >= 0.0
                    gx = gexs[pr][:, half]
                    g = gss[pr][hh]
                    dz = g * jnp.where(pos, er, r) - (gx + pc_ref[4 + h, rs, 0:1]) * jnp.where(pos, r, er)
                    dzs.append(jnp.where(causal, dz, 0.0) if diag else dz)
                    pc_ref[h, rs, 0:1] += plins[pr][:, half][:, QB - 1:QB]
                    pc_ref[4 + h, rs, 0:1] += gx[:, QB - 1:QB] + g[:, QB - 1:QB]
                dzbs.append(jnp.concatenate(dzs, axis=1).astype(BF16))
            for pr in pairs:
                dq_acc[rs, cols[pr]] += _dot(dzbs[pr], kbds[pr])
                r1 = _dot(qt_ref[pr, :, rs], dzbs[pr])
                dk_acc[pr, j] += jnp.where(lo_rows, r1[:, :QB], r1[:, QB:])
                r2 = _dot(dot_ref[pr, :, rs], jnp.concatenate(wss[pr], axis=1).astype(BF16))
                dv_acc[pr, j] += jnp.where(lo_rows, r2[:, :QB], r2[:, QB:])

        def step_head(j, carry):
            block(j, False, SB_HEAD_ROWS)
            return carry

        def step_full(j, carry):
            block(j, False, QB)
            return carry

        lax.fori_loop(jstart, jfull, step_head, 0)
        lax.fori_loop(jfull, i, step_full, 0)
        block(i, True, QB)
        dq_ref[...] = (dq_acc[...] * 0.125).astype(BF16)

        @pl.when(i == nq - 1)
        def _():
            def put(jj, carry):
                krows = pl.ds(pl.multiple_of(jj * QB, QB), QB)
                for pr in range(2):
                    dk_ref[krows, pr * 128:(pr + 1) * 128] = dk_acc[pr, jj].T.astype(BF16)
                    dv_ref[krows, pr * 128:(pr + 1) * 128] = dv_acc[pr, jj].T.astype(BF16)
                return carry
            lax.fori_loop(0, nq, put, 0)

    return _call(
        body, carried, [proj, proj, proj, dcat, stats], name="sb_bwd", grid=(2, nq),
        in_specs=[pl.BlockSpec((QB, 256), lambda g, i: (i, g)),
                  pl.BlockSpec((S, 256), lambda g, i: (0, 2 + g), pipeline_mode=ONE_BUF),
                  pl.BlockSpec((S, 256), lambda g, i: (0, 4 + g), pipeline_mode=ONE_BUF),
                  pl.BlockSpec((QB, 256), lambda g, i: (i, g)),
                  pl.BlockSpec((None, QB, 128), lambda g, i: (g, i, 0))],
        out_specs=[pl.BlockSpec((QB, 256), lambda g, i: (i, g)),
                   pl.BlockSpec((S, 256), lambda g, i: (0, g)),
                   pl.BlockSpec((S, 256), lambda g, i: (0, g))],
        out_shape=[jax.ShapeDtypeStruct((S, 512), BF16)] * 3,
        scratch_shapes=[pltpu.VMEM((2, nq, 128, QB), F32), pltpu.VMEM((2, nq, 128, QB), F32),
                        pltpu.VMEM((QB, 256), F32), pltpu.VMEM((8, QB, 128), F32), pltpu.VMEM((QB, 256), BF16),
                        pltpu.VMEM((2, 128, QB), BF16), pltpu.VMEM((2, 128, QB), BF16)],
        compiler_params=_cp(2, 56))


GLA_NC = 8
GLA_R = GLA_NC * CHUNK


def _chunk_tri(strict):
    row = lax.broadcasted_iota(jnp.int32, (GLA_R, GLA_R), 0)
    col = lax.broadcasted_iota(jnp.int32, (GLA_R, GLA_R), 1)
    m = jnp.logical_and(row // CHUNK == col // CHUNK, row > col if strict else row >= col)
    u = jnp.where(m, 1.0, 0.0).astype(BF16)
    return jnp.concatenate([u, u], axis=1)


def _per_chunk_rows(vals):
    return jnp.concatenate([jnp.broadcast_to(v, (CHUNK, v.shape[1])) for v in vals], axis=0)


def _head_blocks(st):
    row = lax.broadcasted_iota(jnp.int32, (H_GLA * DV, H_GLA * DK), 0)
    col = lax.broadcasted_iota(jnp.int32, (H_GLA * DV, H_GLA * DK), 1)
    t = jnp.concatenate([st.astype(BF16)] * H_GLA, axis=0)
    return jnp.where(row // DV == col // DK, t, jnp.zeros_like(t))


def _head_diag(big):
    head = lax.broadcasted_iota(jnp.int32, (DV, H_GLA * DK), 1) // DK
    out = big[0:DV]
    for h in range(1, H_GLA):
        out = jnp.where(head == h, big[h * DV:(h + 1) * DV], out)
    return out


def _gla_gate4(gf_ref, wfg_ref, bfg_ref):
    f = _dot(gf_ref[...], wfg_ref[...]) + bfg_ref[...]
    _, la, _ = _log_sigmoid_parts(f)
    lah, lal = _split(la * (1.0 / 16.0))
    cum = _dot(_chunk_tri(False), jnp.concatenate([lah, lal], axis=0))
    tots = [cum[(c + 1) * CHUNK - 1:(c + 1) * CHUNK, :] for c in range(GLA_NC)]
    return f, jnp.exp(_per_chunk_rows(tots) - cum), [jnp.exp(t) for t in tots]


def _gla_specs4(ns, rev):
    def ix(i):
        return ns - 1 - i if rev else i
    return [pl.BlockSpec((GLA_R, 256), lambda i: (ix(i), 6)),
            pl.BlockSpec((GLA_R, 256), lambda i: (ix(i), 7)),
            pl.BlockSpec((GLA_R, 512), lambda i: (ix(i), 4)),
            pl.BlockSpec((GLA_R, 512), lambda i: (ix(i), 5)),
            pl.BlockSpec((GLA_R, 128), lambda i: (ix(i), 24))]


def _gla_fwd(proj, wfg_p, bfg, ggla, carried=()):
    S = proj.shape[0]
    ns = S // GLA_R

    def body(q_ref, k_ref, v_ref, gg_ref, gf_ref, wfg_ref, bfg_ref, ggla_ref, o_ref, st_ref, state):
        @pl.when(pl.program_id(0) == 0)
        def _():
            state[...] = jnp.zeros_like(state)

        _, e, decs = _gla_gate4(gf_ref, wfg_ref, bfg_ref)
        kdec = (k_ref[...].astype(F32) * e).astype(BF16)
        rows = [slice(c * CHUNK, (c + 1) * CHUNK) for c in range(GLA_NC)]
        kvs = [_head_diag(_dot_tn(v_ref[rows[c], :], kdec[rows[c]])) for c in range(GLA_NC)]
        st = state[...]
        sts = []
        for c in range(GLA_NC):
            st = st * decs[c] + kvs[c]
            st_ref[c] = st
            sts.append(st)
        state[...] = st
        o = jnp.concatenate([_dot_nt(q_ref[rows[c], :] * 0.125, _head_blocks(sts[c])) for c in range(GLA_NC)], axis=0)
        for h in range(H_GLA):
            vs = slice(h * DV, (h + 1) * DV)
            oh = o[:, vs]
            ohn = oh * lax.rsqrt(jnp.mean(oh * oh, axis=-1, keepdims=True) + EPS)
            gg = gg_ref[:, vs].astype(F32)
            o_ref[:, vs] = ((ohn * ggla_ref[:, vs]) * (gg * _sigmoid(gg))).astype(BF16)

    return _call(
        body, carried, [proj, proj, proj, proj, proj, wfg_p, bfg, ggla], name="gla_fwd", grid=(ns,),
        in_specs=_gla_specs4(ns, False) + [pl.BlockSpec((128, 256), lambda i: (0, 0)),
                                           pl.BlockSpec((1, 256), lambda i: (0, 0)),
                                           pl.BlockSpec((1, 512), lambda i: (0, 0))],
        out_specs=[pl.BlockSpec((GLA_R, 512), lambda i: (i, 0)),
                   pl.BlockSpec((GLA_NC, 128, 256), lambda i: (i, 0, 0))],
        out_shape=[jax.ShapeDtypeStruct((S, 512), BF16), jax.ShapeDtypeStruct((S // CHUNK, 128, 256), F32)],
        scratch_shapes=[pltpu.VMEM((128, 256), F32)],
        compiler_params=_cp(1))


def _gla_bwd(dcat, proj, states, wfg_p, bfg, ggla, carried=()):
    S = proj.shape[0]
    ns = S // GLA_R

    def body(do_ref, q_ref, k_ref, v_ref, gg_ref, gf_ref, sc_ref, sp_ref, wfg_ref, bfg_ref, ggla_ref,
             dp_ref, s_ref, dw_ref, carry):
        sr = pl.program_id(0)

        @pl.when(sr == 0)
        def _():
            carry[...] = jnp.zeros_like(carry)
            s_ref[...] = jnp.zeros_like(s_ref)
            dw_ref[...] = jnp.zeros_like(dw_ref)

        f, e, decs = _gla_gate4(gf_ref, wfg_ref, bfg_ref)
        kf = k_ref[...].astype(F32) * e
        kdec = kf.astype(BF16)
        rows = [slice(c * CHUNK, (c + 1) * CHUNK) for c in range(GLA_NC)]
        sts = [sc_ref[c] for c in range(GLA_NC)]
        st_before = jnp.where(sr < ns - 1, sp_ref[0], 0.0)
        sbd = [_head_blocks(sts[c]) for c in range(GLA_NC)]
        qs = q_ref[...] * 0.125
        o = jnp.concatenate([_dot_nt(qs[rows[c]], sbd[c]) for c in range(GLA_NC)], axis=0)
        dobs = []
        for h in range(H_GLA):
            vs = slice(h * DV, (h + 1) * DV)
            oh = o[:, vs]
            rr = lax.rsqrt(jnp.mean(oh * oh, axis=-1, keepdims=True) + EPS)
            ohn = oh * rr
            gg = gg_ref[:, vs].astype(F32)
            sg = _sigmoid(gg)
            dout = do_ref[:, vs].astype(F32)
            gl = ggla_ref[:, vs]
            dp_ref[:, 1024 + h * DV:1024 + (h + 1) * DV] = (
                dout * (ohn * gl) * (sg * (1.0 + gg * (1.0 - sg)))).astype(BF16)
            dt1 = dout * (gg * sg)
            s_ref[0:1, vs] += jnp.sum(dt1 * ohn, axis=0, keepdims=True)
            dohn = dt1 * gl
            dobs.append((rr * (dohn - ohn * jnp.mean(dohn * ohn, axis=-1, keepdims=True))).astype(BF16))
        dob = jnp.concatenate(dobs, axis=1)
        dsout = []
        for c in range(GLA_NC):
            dp_ref[rows[c], 0:256] = (_dot(dob[rows[c]], sbd[c]) * 0.125).astype(BF16)
            dsout.append(_head_diag(_dot_tn(dob[rows[c]], qs[rows[c]])))
        g = carry[...]
        gts, ddecs = [None] * GLA_NC, [None] * GLA_NC
        for c in reversed(range(GLA_NC)):
            gts[c] = dsout[c] + g
            ddecs[c] = jnp.sum(gts[c] * (sts[c - 1] if c > 0 else st_before), axis=0, keepdims=True) * decs[c]
            g = gts[c] * decs[c]
        carry[...] = g
        dkds = []
        for c in range(GLA_NC):
            gbd = _head_blocks(gts[c])
            dkds.append(_dot(v_ref[rows[c], :], gbd))
            dp_ref[rows[c], 512:1024] = _dot_nt(kdec[rows[c]], gbd).astype(BF16)
        dkd = jnp.concatenate(dkds, axis=0)
        dp_ref[:, 256:512] = (dkd * e).astype(BF16)
        wh, wl = _split(dkd * kf)
        dla = _dot(_chunk_tri(True), jnp.concatenate([wh, wl], axis=0)) + _per_chunk_rows(ddecs)
        df = dla * _sigmoid(-f) * (1.0 / 16.0)
        dfb = df.astype(BF16)
        s_ref[1:2, 0:256] += jnp.sum(df, axis=0, keepdims=True)
        dw_ref[...] += _dot_tn(gf_ref[...], dfb)
        dp_ref[:, 1536:1664] = _dot_nt(dfb, wfg_ref[...]).astype(BF16)

    return _call(
        body, carried, [dcat, proj, proj, proj, proj, proj, states, states, wfg_p, bfg, ggla],
        name="gla_bwd", grid=(ns,),
        in_specs=[pl.BlockSpec((GLA_R, 512), lambda i: (ns - 1 - i, 1))] + _gla_specs4(ns, True) + [
            pl.BlockSpec((GLA_NC, 128, 256), lambda i: (ns - 1 - i, 0, 0)),
            pl.BlockSpec((1, 128, 256), lambda i: (jnp.maximum((ns - 1 - i) * GLA_NC - 1, 0), 0, 0)),
            pl.BlockSpec((128, 256), lambda i: (0, 0)),
            pl.BlockSpec((1, 256), lambda i: (0, 0)),
            pl.BlockSpec((1, 512), lambda i: (0, 0))],
        out_specs=[pl.BlockSpec((GLA_R, 1664), lambda i: (ns - 1 - i, 0)),
                   pl.BlockSpec((8, 512), lambda i: (0, 0)),
                   pl.BlockSpec((128, 256), lambda i: (0, 0))],
        out_shape=[jax.ShapeDtypeStruct((S, 1664), BF16), jax.ShapeDtypeStruct((8, 512), F32),
                   jax.ShapeDtypeStruct((128, 256), F32)],
        scratch_shapes=[pltpu.VMEM((128, 256), F32)],
        compiler_params=_cp(1))


def _sum_leading(a, name):
    n = a.shape[0]

    def body(a_ref, o_ref):
        acc = a_ref[0]
        for k in range(1, n):
            acc = acc + a_ref[k]
        o_ref[...] = acc

    return pl.pallas_call(
        body, name=name, out_shape=jax.ShapeDtypeStruct(a.shape[1:], F32),
        in_specs=[VMEM_SPEC], out_specs=VMEM_SPEC,
    )(a)


def _sum_chip(own, recv, name):
    R, C = own.shape
    tr, tc = _tile2d(R, C, 1024 * 1024)

    def body(o_ref, r_ref, p_ref):
        acc = o_ref[...]
        for k in range(3):
            acc = acc + r_ref[k].astype(F32)
        p_ref[...] = acc

    return pl.pallas_call(
        body, name=name, grid=(R // tr, C // tc),
        in_specs=[pl.BlockSpec((tr, tc), lambda i, j: (i, j)), pl.BlockSpec((3, tr, tc), lambda i, j: (0, i, j))],
        out_specs=pl.BlockSpec((tr, tc), lambda i, j: (i, j)),
        out_shape=jax.ShapeDtypeStruct((R, C), F32), compiler_params=_cp(2, 40),
    )(own, recv)


def _adamw(w, p, q, m, v, name):
    R, C = w.shape
    tr, tc = _tile2d(R, C, 1024 * 1024)
    two = q is not None

    def body(*refs):
        if two:
            w_ref, p_ref, q_ref, m_ref, v_ref, g_out, d_out, m_out, v_out = refs
            g = p_ref[...] + q_ref[...]
        else:
            w_ref, p_ref, m_ref, v_ref, g_out, d_out, m_out, v_out = refs
            g = p_ref[...]
        m2 = B1 * m_ref[...] + (1.0 - B1) * g
        v2 = B2 * v_ref[...] + (1.0 - B2) * (g * g)
        m_hat = m2 / (1.0 - B1 ** STEP)
        v_hat = v2 / (1.0 - B2 ** STEP)
        g_out[...] = g
        d_out[...] = -LR * (m_hat / (jnp.sqrt(v_hat) + EPS_A) + WD * w_ref[...])
        m_out[...] = m2
        v_out[...] = v2

    spec = pl.BlockSpec((tr, tc), lambda i, j: (i, j))
    ins = [w, p, q, m, v] if two else [w, p, m, v]
    return pl.pallas_call(
        body, name=name, grid=(R // tr, C // tc),
        in_specs=[spec] * len(ins), out_specs=[spec] * 4,
        out_shape=[jax.ShapeDtypeStruct((R, C), F32)] * 4, compiler_params=_cp(2, 40),
    )(*ins)


def _cols_to_chips(a, width):
    return a.reshape(a.shape[0], 4, width).swapaxes(0, 1)


def _chips_to_cols(a):
    return a.swapaxes(0, 1).reshape(a.shape[1], 4 * a.shape[2])


def _swap_mid(a):
    lead = a.shape[:-1]
    return a.reshape(lead + (2, 2, HB)).swapaxes(-3, -2).reshape(lead + (4 * HB,))


def kernel(x, c, w_ada, b_ada, g_norm1, w_in, w_fg2, b_fg2, g_gla_out, w_out, g_norm2, w_up, w_conv, b_conv, w_down, g_final, loss_target, m_w_ada, m_b_ada, m_g_norm1, m_w_in, m_w_fg2, m_b_fg2, m_g_gla_out, m_w_out, m_g_norm2, m_w_up, m_w_conv, m_b_conv, m_w_down, m_g_final, v_w_ada, v_b_ada, v_g_norm1, v_w_in, v_w_fg2, v_b_fg2, v_g_gla_out, v_w_out, v_g_norm2, v_w_up, v_w_conv, v_b_conv, v_w_down, v_g_final):
    xi, yi, ci = lax.axis_index("x"), lax.axis_index("y"), lax.axis_index("c")
    cidx = 2 * xi + yi
    didx = 4 * xi + 2 * yi + ci
    xs = x[0]
    tgt = loss_target[0]
    gfin = g_final.reshape(1, D)
    AW = D * 6 // 4

    c_all = _allgather8(c, "gather_c").reshape(8, D)
    c_pad = jnp.concatenate([c_all, jnp.zeros((8, D), F32)], axis=0)
    mod_part = _ada_fwd(c_pad, w_ada[0], lax.dynamic_slice(b_ada, (0, cidx * AW), (1, AW)))[:8]
    small = jnp.concatenate([mod_part.reshape(-1), w_conv.reshape(-1), w_fg2.reshape(-1)]).reshape(-1, 128)
    small_g = _allgather4(small, "gather_small").reshape(4, -1)
    mod = lax.dynamic_index_in_dim(small_g[:, :8 * AW].reshape(4, 8, AW), didx, axis=1, keepdims=False).reshape(1, 6 * D)
    shift1, scale1, gate1, shift2, scale2, gate2 = [mod[:, k * D:(k + 1) * D] for k in range(6)]
    o1 = 8 * AW
    o2 = o1 + 3 * HB
    wc_p = _swap_mid(_chips_to_cols(small_g[:, o1:o2].reshape(4, 3, HB)))
    bc_p = _swap_mid(b_conv)
    wfg_full = _chips_to_cols(small_g[:, o2:].reshape(4, RANK, 64))
    wfg_p = jnp.concatenate([wfg_full, jnp.zeros((128 - RANK, 256), F32)], axis=0).astype(BF16)

    w_in_t = _allgather4_split(w_in[0].T.astype(BF16), "gather_w_in").reshape(N_IN, D)
    w_in_t = jnp.concatenate([w_in_t, jnp.zeros((N_IN_P - N_IN, D), BF16)], axis=0)
    w_in_p = w_in_t.T

    (h, proj), (w_down_g,) = _norm_mod_mm(xs, g_norm1, shift1, scale1, w_in_p,
                                          carried=[("gather", w_down[0].astype(BF16), False)])
    w_down_f = w_down_g.reshape(D_FF, D)
    (o_gla, states), (w_out_g,) = _gla_fwd(proj, wfg_p, b_fg2, g_gla_out,
                                           carried=[("gather", w_out[0].astype(BF16), False)])
    w_out_f = w_out_g.reshape(D, D)
    (o_sb, stats), (w_up_g,) = _sb_fwd(proj, carried=[("gather", w_up[0].astype(BF16), False)])
    w_up_p = _swap_mid(_chips_to_cols(w_up_g))
    x1, h2, mixed = _resid_norm_mod(xs, [(o_sb, w_out_f[:512]), (o_gla, w_out_f[512:])],
                                    gate1, g_norm2, shift2, scale2)
    u0p, a = _up_conv_glu(h2, w_up_p, wc_p, bc_p)
    dx2, dy2, s_fin = _final_loss(x1, [(a, w_down_f)], gate2, gfin, tgt)

    da = _mm([(dy2, w_down_f.T)], BF16, "mm_down_t", 512, D_FF, 48)
    dw_down, dw_down_h = [t.reshape(4, D_FF // 4, D) for t in _mm_tn([a], dy2, "mm_dw_down", D, 1024, 60)]
    (du0p, s_conv, dx1, dmixed, s_n2), (rc_down,) = _conv_glu_up_norm2_bwd(
        da, u0p, wc_p, bc_p, w_up_p.T, x1, dx2, g_norm2, scale2, mixed, gate1,
        carried=[("scatter", dw_down_h, False)])
    dw_up, dw_up_h = _mm_tn([h2], du0p, "mm_dw_up", HB, 1024, 56)
    dcat = _mm([(dmixed, w_out_f.T)], BF16, "mm_out_t", 512, D)
    dw_out, dw_out_h = [t.reshape(4, D // 4, D) for t in _mm_tn([o_sb, o_gla], dmixed, "mm_dw_out", D, 512)]
    (dq, dk, dv), (rc_up,) = _sb_bwd(proj, dcat, stats, carried=[("scatter", dw_up_h, True)])
    (dp_gla, s_gla, dwfg), (rc_out,) = _gla_bwd(dcat, proj, states, wfg_p, b_fg2, g_gla_out,
                                                carried=[("scatter", dw_out_h, False)])
    dw_in, dw_in_h = _mm_tn([dq, dk, dv, dp_gla], h, "mm_dw_in", D, 1024, 60)
    dw_in_h = dw_in_h[0, :N_IN].reshape(4, N_IN // 4, D)
    dw_in_own = lax.dynamic_slice(dw_in[0], (cidx * (N_IN // 4), 0), (N_IN // 4, D))
    (gx, s_n1), (rc_in,) = _norm_mod_bwd(
        [(dq, w_in_t[:512]), (dk, w_in_t[512:1024]), (dv, w_in_t[1024:1536]), (dp_gla, w_in_t[1536:])],
        xs, dx1, g_norm1, scale1, None, None, "mm_in_t_norm1_bwd", carried=[("scatter", dw_in_h, False)])

    dmod = jnp.concatenate([s_n1[0], s_n1[1], s_n2[3], s_n2[0], s_n2[1], s_fin[1]])
    s_conv_n = _swap_mid(s_conv[:4])
    part = jnp.concatenate([dmod, s_n1[2], s_n2[2], s_fin[0], s_gla[0], s_gla[1, :256], s_conv_n[0],
                            s_conv_n[1:4].reshape(-1), dwfg[:RANK].reshape(-1),
                            jnp.broadcast_to(jnp.sum(s_fin[2]), (128,))]).reshape(-1, 128)
    parts = _allgather8(part, "gather_small_grads")
    tot = _sum_leading(parts, "sum_small_grads").reshape(-1)
    loss = 0.5 / D * tot[-1]
    dmod_all = parts.reshape(8, -1)[:, :6 * D]
    offs = [0]
    for n in (6 * D, D, D, D, 512, 256, 2 * D_FF, 3 * 2 * D_FF, RANK * 256):
        offs.append(offs[-1] + n)
    g_b_ada, g_g1, g_g2, g_gf, g_ggla, g_bfg, g_bconv, g_wconv_full, g_wfg_full = [
        tot[offs[k]:offs[k + 1]] for k in range(9)]
    g_wconv = lax.dynamic_index_in_dim(_cols_to_chips(g_wconv_full.reshape(3, 2 * D_FF), HB), cidx, 0, keepdims=False)
    g_wfg = lax.dynamic_index_in_dim(_cols_to_chips(g_wfg_full.reshape(RANK, 256), 64), cidx, 0, keepdims=False)

    dmod_pad = jnp.concatenate([dmod_all, jnp.zeros((8, 6 * D), F32)], axis=0)
    g_w_ada = _ada_bwd(c_pad, lax.dynamic_slice(dmod_pad, (0, cidx * AW), (16, AW)))

    def own(blocks, swapped=False):
        return lax.dynamic_index_in_dim(blocks, _slot(cidx, swapped), axis=0, keepdims=False)

    p_in = _sum_chip(dw_in_own, rc_in, "rs_w_in_sum")
    p_out = _sum_chip(own(dw_out), rc_out, "rs_w_out_sum")
    p_up = _sum_chip(own(dw_up, True), rc_up, "rs_w_up_sum")
    p_down = _sum_chip(own(dw_down), rc_down, "rs_w_down_sum")
    q_in, q_out, q_up, q_down = _pair_swap([p_in, p_out, p_up, p_down], "rs_swap")

    out = {}
    out["w_ada"] = _adamw(w_ada[0], g_w_ada, None, m_w_ada[0], v_w_ada[0], "adamw_w_ada")
    out["w_in"] = [t.T for t in _adamw(w_in[0].T, p_in, q_in, m_w_in[0].T, v_w_in[0].T, "adamw_w_in")]
    out["w_out"] = _adamw(w_out[0], p_out, q_out, m_w_out[0], v_w_out[0], "adamw_w_out")
    out["w_up"] = _adamw(w_up[0], p_up, q_up, m_w_up[0], v_w_up[0], "adamw_w_up")
    out["w_down"] = _adamw(w_down[0], p_down, q_down, m_w_down[0], v_w_down[0], "adamw_w_down")
    small_names = ["b_ada", "g_norm1", "w_fg2", "b_fg2", "g_gla_out", "g_norm2", "w_conv", "b_conv", "g_final"]
    small_w = [b_ada, g_norm1, w_fg2, b_fg2, g_gla_out, g_norm2, w_conv, b_conv, g_final]
    small_m = [m_b_ada, m_g_norm1, m_w_fg2, m_b_fg2, m_g_gla_out, m_g_norm2, m_w_conv, m_b_conv, m_g_final]
    small_v = [v_b_ada, v_g_norm1, v_w_fg2, v_b_fg2, v_g_gla_out, v_g_norm2, v_w_conv, v_b_conv, v_g_final]
    small_gr = [g_b_ada, g_g1, g_wfg, g_bfg, g_ggla, g_g2, g_wconv, g_bconv, g_gf]

    def pack(arrs):
        flat = jnp.concatenate([t.reshape(-1) for t in arrs])
        return jnp.concatenate([flat, jnp.zeros((-flat.shape[0]) % 1024, F32)]).reshape(-1, 128)

    packed = _adamw(pack(small_w), pack(small_gr), None, pack(small_m), pack(small_v), "adamw_small")
    off = 0
    for nm, wt in zip(small_names, small_w):
        n = wt.size
        out[nm] = [t.reshape(-1)[off:off + n].reshape(wt.shape) for t in packed]
        off += n
    for nm in ("w_ada", "w_in", "w_out", "w_up", "w_down"):
        out[nm] = [t[None] for t in out[nm]]

    names = ["w_ada", "b_ada", "g_norm1", "w_in", "w_fg2", "b_fg2", "g_gla_out", "w_out", "g_norm2", "w_up",
             "w_conv", "b_conv", "w_down", "g_final"]
    res = [loss, gx[None]]
    for k in range(4):
        res += [out[nm][k] for nm in names]
    return tuple(res)
```

```python
import functools

import jax
import jax.numpy as jnp
from jax import lax
from jax.experimental import pallas as pl
from jax.experimental.pallas import tpu as pltpu

F32 = jnp.float32
BF16 = jnp.bfloat16
MESH = pl.DeviceIdType.MESH

D = 1024
H_SB = 8
DK = 64
DV = 128
H_GLA = 4
CHUNK = 64
RANK = 16
N_IN = 3088
N_IN_P = 3200
D_FF = 2816
HB = D_FF // 2
LANES = 128
EPS = 1e-6
QB = 128
SB_SKIP = -120.0
SB_HEAD_ROWS = 64

LR, B1, B2, EPS_A, WD, STEP = 0.001, 0.9, 0.999, 1e-08, 0.01, 10

ANY = pl.BlockSpec(memory_space=pl.ANY)
VMEM_SPEC = pl.BlockSpec(memory_space=pltpu.VMEM)
ONE_BUF = pl.Buffered(1)


def _cp(ndim=0, vmem_mb=None):
    kw = {}
    if ndim:
        kw["dimension_semantics"] = ("arbitrary",) * ndim
    if vmem_mb:
        kw["vmem_limit_bytes"] = vmem_mb * 1024 * 1024
    return pltpu.CompilerParams(**kw)


def _dot(a, b):
    return jnp.dot(a, b, preferred_element_type=F32)


def _dot_nt(a, b):
    return lax.dot_general(a, b, (((1,), (1,)), ((), ())), preferred_element_type=F32)


def _dot_tn(a, b):
    return lax.dot_general(a, b, (((0,), (0,)), ((), ())), preferred_element_type=F32)


def _split(x):
    hi = x.astype(BF16)
    lo = (x - hi.astype(F32)).astype(BF16)
    return hi, lo


def _sigmoid(x):
    return jax.nn.sigmoid(x)


def _sigmoid_fast(x):
    return pl.reciprocal(1.0 + jnp.exp(-x), approx=True)


def _log_sigmoid_parts(z):
    e = jnp.exp(-jnp.abs(z))
    sp = jnp.log1p(e)
    return -(jnp.maximum(z, 0.0) + sp), jnp.minimum(z, 0.0) - sp, e


def _tile2d(rows, cols, budget=512 * 1024):
    best = None
    for t in range(8, rows + 1, 8):
        if rows % t == 0 and t * cols * 4 <= budget:
            best = t
    if best is not None:
        return best, cols
    best = LANES if cols % LANES == 0 else cols
    for t in range(LANES, cols + 1, LANES):
        if cols % t == 0 and rows * t * 4 <= budget:
            best = t
    return rows, best


def _flip(v, bit):
    return 1 - v if bit else v


def _allgather8(a, name):
    def body(a_ref, o_ref, ssem, rsem, lsem):
        x, y, c = lax.axis_index("x"), lax.axis_index("y"), lax.axis_index("c")
        me = 4 * x + 2 * y + c
        loc = pltpu.make_async_copy(a_ref, o_ref.at[me], lsem)
        loc.start()
        sends = []
        for r in range(1, 8):
            peer = (_flip(x, r & 4), _flip(y, r & 2), _flip(c, r & 1))
            cp = pltpu.make_async_remote_copy(
                src_ref=a_ref, dst_ref=o_ref.at[me], send_sem=ssem.at[r - 1], recv_sem=rsem.at[r - 1],
                device_id=peer, device_id_type=MESH)
            cp.start()
            sends.append(cp)
        for r in range(1, 8):
            peer = (_flip(x, r & 4), _flip(y, r & 2), _flip(c, r & 1))
            pidx = 4 * peer[0] + 2 * peer[1] + peer[2]
            pltpu.make_async_remote_copy(
                src_ref=a_ref, dst_ref=o_ref.at[pidx], send_sem=ssem.at[r - 1], recv_sem=rsem.at[r - 1],
                device_id=peer, device_id_type=MESH).wait_recv()
        for cp in sends:
            cp.wait_send()
        loc.wait()

    return pl.pallas_call(
        body, name=name,
        out_shape=jax.ShapeDtypeStruct((8,) + a.shape, a.dtype),
        in_specs=[VMEM_SPEC], out_specs=VMEM_SPEC,
        scratch_shapes=[pltpu.SemaphoreType.DMA((7,)), pltpu.SemaphoreType.DMA((7,)), pltpu.SemaphoreType.DMA],
    )(a)


def _allgather4(a, name):
    def body(a_ref, o_ref, ssem, rsem, lsem):
        x, y, c = lax.axis_index("x"), lax.axis_index("y"), lax.axis_index("c")
        me = 2 * x + y
        loc = pltpu.make_async_copy(a_ref, o_ref.at[me], lsem)
        loc.start()
        sends = []
        for r in range(1, 4):
            peer = (_flip(x, r & 2), _flip(y, r & 1), c)
            cp = pltpu.make_async_remote_copy(
                src_ref=a_ref, dst_ref=o_ref.at[me], send_sem=ssem.at[r - 1], recv_sem=rsem.at[r - 1],
                device_id=peer, device_id_type=MESH)
            cp.start()
            sends.append(cp)
        for r in range(1, 4):
            peer = (_flip(x, r & 2), _flip(y, r & 1), c)
            pidx = 2 * peer[0] + peer[1]
            pltpu.make_async_remote_copy(
                src_ref=a_ref, dst_ref=o_ref.at[pidx], send_sem=ssem.at[r - 1], recv_sem=rsem.at[r - 1],
                device_id=peer, device_id_type=MESH).wait_recv()
        for cp in sends:
            cp.wait_send()
        loc.wait()

    return pl.pallas_call(
        body, name=name,
        out_shape=jax.ShapeDtypeStruct((4,) + a.shape, a.dtype),
        in_specs=[ANY], out_specs=ANY,
        scratch_shapes=[pltpu.SemaphoreType.DMA((3,)), pltpu.SemaphoreType.DMA((3,)), pltpu.SemaphoreType.DMA],
    )(a)


def _allgather4_split(a, name):
    R, C = a.shape
    hc = C // 2

    def body(a_ref, o_ref, ssem, rsem, fssem, frsem, lsem):
        x, y, c = lax.axis_index("x"), lax.axis_index("y"), lax.axis_index("c")
        me = 2 * x + y
        sibling = (x, y, 1 - c)
        mine = pl.ds(pl.multiple_of(c * hc, hc), hc)
        theirs = pl.ds(pl.multiple_of((1 - c) * hc, hc), hc)
        loc = pltpu.make_async_copy(a_ref, o_ref.at[me], lsem)
        loc.start()
        peers = [(_flip(x, r & 2), _flip(y, r & 1), c) for r in range(1, 4)]
        pidx = [2 * p[0] + p[1] for p in peers]
        sends = []
        for k in range(3):
            cp = pltpu.make_async_remote_copy(
                src_ref=a_ref.at[:, mine], dst_ref=o_ref.at[me, :, mine], send_sem=ssem.at[k], recv_sem=rsem.at[k],
                device_id=peers[k], device_id_type=MESH)
            cp.start()
            sends.append(cp)
        for k in range(3):
            landed = o_ref.at[pidx[k], :, mine]
            pltpu.make_async_remote_copy(
                src_ref=landed, dst_ref=landed, send_sem=ssem.at[k], recv_sem=rsem.at[k],
                device_id=peers[k], device_id_type=MESH).wait_recv()
            cp = pltpu.make_async_remote_copy(
                src_ref=landed, dst_ref=landed, send_sem=fssem.at[k], recv_sem=frsem.at[k],
                device_id=sibling, device_id_type=MESH)
            cp.start()
            sends.append(cp)
        for k in range(3):
            got = o_ref.at[pidx[k], :, theirs]
            pltpu.make_async_remote_copy(
                src_ref=got, dst_ref=got, send_sem=fssem.at[k], recv_sem=frsem.at[k],
                device_id=sibling, device_id_type=MESH).wait_recv()
        for cp in sends:
            cp.wait_send()
        loc.wait()

    return pl.pallas_call(
        body, name=name,
        out_shape=jax.ShapeDtypeStruct((4,) + a.shape, a.dtype),
        in_specs=[ANY], out_specs=ANY,
        scratch_shapes=[pltpu.SemaphoreType.DMA((3,))] * 4 + [pltpu.SemaphoreType.DMA],
    )(a)


def _slot(chip, swapped):
    return 2 * (chip % 2) + chip // 2 if swapped else chip


def _pair_swap(ps, name):
    n = len(ps)

    def body(*refs):
        x, y, c = lax.axis_index("x"), lax.axis_index("y"), lax.axis_index("c")
        ssem, rsem = refs[2 * n], refs[2 * n + 1]
        cps = [pltpu.make_async_remote_copy(
            src_ref=refs[k], dst_ref=refs[n + k], send_sem=ssem.at[k], recv_sem=rsem.at[k],
            device_id=(x, y, 1 - c), device_id_type=MESH) for k in range(n)]
        for cp in cps:
            cp.start()
        for cp in cps:
            cp.wait()

    return pl.pallas_call(
        body, name=name,
        out_shape=[jax.ShapeDtypeStruct(p.shape, p.dtype) for p in ps],
        in_specs=[ANY] * n, out_specs=[ANY] * n,
        scratch_shapes=[pltpu.SemaphoreType.DMA((n,)), pltpu.SemaphoreType.DMA((n,))],
    )(*ps)


def _carried_copies(kind, src_ref, dst_ref, sems, swapped):
    ssem, rsem, lsem = sems
    x, y, c = lax.axis_index("x"), lax.axis_index("y"), lax.axis_index("c")
    me = 2 * x + y
    starts, recvs = [], []
    if kind == "gather":
        starts.append(pltpu.make_async_copy(src_ref, dst_ref.at[me], lsem))
    for r in range(1, 4):
        peer = (_flip(x, r & 2), _flip(y, r & 1), c)
        pidx = 2 * peer[0] + peer[1]
        if kind == "gather":
            src, dst, landed = src_ref, dst_ref.at[me], dst_ref.at[pidx]
        else:
            src = src_ref.at[2 * peer[1] + peer[0] if swapped else pidx]
            dst = landed = dst_ref.at[r - 1]
        starts.append(pltpu.make_async_remote_copy(
            src_ref=src, dst_ref=dst, send_sem=ssem.at[r - 1], recv_sem=rsem.at[r - 1],
            device_id=peer, device_id_type=MESH))
        recvs.append(pltpu.make_async_remote_copy(
            src_ref=src, dst_ref=landed, send_sem=ssem.at[r - 1], recv_sem=rsem.at[r - 1],
            device_id=peer, device_id_type=MESH))
    return starts, recvs


def _call(body, carried, operands, *, name, grid, in_specs, out_specs, out_shape, scratch_shapes=(),
          compiler_params=None):
    single = not isinstance(out_shape, (list, tuple))
    out_specs = [out_specs] if single else list(out_specs)
    out_shape = [out_shape] if single else list(out_shape)
    n_in, n_out, n_sc, nh = len(operands), len(out_shape), len(scratch_shapes), len(carried)

    def full(*refs):
        ins, h_in = refs[:n_in], refs[n_in:n_in + nh]
        o0 = n_in + nh
        outs, h_out = refs[o0:o0 + n_out], refs[o0 + n_out:o0 + n_out + nh]
        s0 = o0 + n_out + nh
        scratch, sems = refs[s0:s0 + n_sc], refs[s0 + n_sc:]
        first = last = None
        for d in range(len(grid)):
            f = pl.program_id(d) == 0
            l = pl.program_id(d) == pl.num_programs(d) - 1
            first = f if first is None else jnp.logical_and(first, f)
            last = l if last is None else jnp.logical_and(last, l)

        def copies(t):
            return _carried_copies(carried[t][0], h_in[t], h_out[t], sems[3 * t:3 * t + 3], carried[t][2])

        if nh:
            @pl.when(first)
            def _():
                for t in range(nh):
                    for cp in copies(t)[0]:
                        cp.start()

        body(*ins, *outs, *scratch)

        if nh:
            @pl.when(last)
            def _():
                for t in range(nh):
                    starts, recvs = copies(t)
                    for cp in recvs:
                        cp.wait_recv()
                    for cp in starts:
                        if carried[t][0] == "gather" and cp is starts[0]:
                            cp.wait()
                        else:
                            cp.wait_send()

    h_shapes = [jax.ShapeDtypeStruct(((4,) + arr.shape) if kind == "gather" else ((3,) + arr.shape[1:]), arr.dtype)
                for kind, arr, _ in carried]
    sem_shapes = [pltpu.SemaphoreType.DMA((3,)), pltpu.SemaphoreType.DMA((3,)), pltpu.SemaphoreType.DMA] * nh
    res = pl.pallas_call(
        full, name=name, grid=grid, in_specs=list(in_specs) + [ANY] * nh, out_specs=out_specs + [ANY] * nh,
        out_shape=out_shape + h_shapes, scratch_shapes=list(scratch_shapes) + sem_shapes,
        compiler_params=compiler_params,
    )(*operands, *[arr for _, arr, _ in carried])
    main = res[:n_out]
    return (main[0] if single else main), list(res[n_out:])


def _mm(pairs, out_dtype, name, tm, tn, vmem_mb=None, carried=()):
    S = pairs[0][0].shape[0]
    N = pairs[0][1].shape[1]
    tm = min(tm, S)
    np_ = len(pairs)

    def body(*refs):
        acc = _dot(refs[0][...], refs[1][...])
        for t in range(1, np_):
            acc = acc + _dot(refs[2 * t][...], refs[2 * t + 1][...])
        refs[-1][...] = acc.astype(refs[-1].dtype)

    in_specs, ops = [], []
    for a, w in pairs:
        in_specs += [pl.BlockSpec((tm, a.shape[1]), lambda n, i: (i, 0)),
                     pl.BlockSpec((w.shape[0], tn), lambda n, i: (0, n))]
        ops += [a, w]
    out, got = _call(
        body, carried, ops, name=name, grid=(N // tn, S // tm), in_specs=in_specs,
        out_specs=pl.BlockSpec((tm, tn), lambda n, i: (i, n)),
        out_shape=jax.ShapeDtypeStruct((S, N), out_dtype),
        compiler_params=_cp(2, vmem_mb))
    return (out, got) if carried else out


def _mm_tn(a_list, b, name, bn, tk, vmem_mb=None):
    S, N = b.shape
    ms = [a.shape[1] for a in a_list]
    M = sum(ms)
    tk = min(tk, S)
    na = len(a_list)

    def body(*refs):
        b_ref, o_ref, o16_ref = refs[na], refs[na + 1], refs[na + 2]

        @pl.when(pl.program_id(1) == 0)
        def _():
            o_ref[...] = jnp.zeros_like(o_ref)
        off = 0
        for t in range(na):
            o_ref[off:off + ms[t], :] += _dot_tn(refs[t][...], b_ref[...])
            off += ms[t]

        @pl.when(pl.program_id(1) == pl.num_programs(1) - 1)
        def _():
            o16_ref[...] = o_ref[...].astype(BF16)

    spec = pl.BlockSpec((None, M, bn), lambda n, k: (n, 0, 0), pipeline_mode=ONE_BUF)
    return pl.pallas_call(
        body, name=name, grid=(N // bn, S // tk),
        in_specs=[pl.BlockSpec((tk, m), lambda n, k: (k, 0)) for m in ms] + [pl.BlockSpec((tk, bn), lambda n, k: (k, n))],
        out_specs=[spec, spec],
        out_shape=[jax.ShapeDtypeStruct((N // bn, M, bn), F32), jax.ShapeDtypeStruct((N // bn, M, bn), BF16)],
        compiler_params=_cp(2, vmem_mb),
    )(*a_list, b)


def _ada_fwd(c_all, w_sh, b_sh):
    def body(c_ref, w_ref, b_ref, o_ref):
        cv = c_ref[...]
        sc = (cv * _sigmoid(cv)).astype(BF16)
        o_ref[...] = _dot(sc, w_ref[...].astype(BF16)) + b_ref[...]

    return pl.pallas_call(
        body, name="ada_fwd", out_shape=jax.ShapeDtypeStruct((c_all.shape[0], w_sh.shape[1]), F32),
        in_specs=[VMEM_SPEC] * 3, out_specs=VMEM_SPEC, compiler_params=_cp(0, 40),
    )(c_all, w_sh, b_sh)


def _ada_bwd(c_all, dmod_sh):
    def body(c_ref, d_ref, o_ref):
        cv = c_ref[...]
        sc = (cv * _sigmoid(cv)).astype(BF16)
        o_ref[...] = _dot_tn(sc, d_ref[...].astype(BF16))

    return pl.pallas_call(
        body, name="ada_bwd", out_shape=jax.ShapeDtypeStruct((c_all.shape[1], dmod_sh.shape[1]), F32),
        in_specs=[VMEM_SPEC] * 2, out_specs=VMEM_SPEC, compiler_params=_cp(0, 40),
    )(c_all, dmod_sh)


def _vec(tm_unused=None):
    return pl.BlockSpec((1, D), lambda i: (0, 0))


def _rows(tm, width=D):
    return pl.BlockSpec((tm, width), lambda i: (i, 0))


def _norm_mod_mm(x, g, shift, scale, w, tm=256, carried=()):
    S = x.shape[0]
    tm = min(tm, S)
    N = w.shape[1]

    def body(x_ref, g_ref, sh_ref, sc_ref, w_ref, h_ref, p_ref):
        xv = x_ref[...]
        r = lax.rsqrt(jnp.mean(xv * xv, axis=-1, keepdims=True) + EPS)
        hn = (xv * r) * g_ref[...]
        h = (hn * (1.0 + sc_ref[...]) + sh_ref[...]).astype(BF16)
        h_ref[...] = h
        p_ref[...] = _dot(h, w_ref[...]).astype(BF16)

    return _call(
        body, carried, [x, g, shift, scale, w], name="norm1_mod_mm_in", grid=(S // tm,),
        in_specs=[_rows(tm), _vec(), _vec(), _vec(), pl.BlockSpec(w.shape, lambda i: (0, 0), pipeline_mode=ONE_BUF)],
        out_specs=[_rows(tm), _rows(tm, N)],
        out_shape=[jax.ShapeDtypeStruct((S, D), BF16), jax.ShapeDtypeStruct((S, N), BF16)],
        compiler_params=_cp(1, 48))


def _mm_rows(pairs, tm):
    ops, specs = [], []
    for a, w in pairs:
        ops += [a, w]
        specs += [pl.BlockSpec((tm, a.shape[1]), lambda i: (i, 0)),
                  pl.BlockSpec(w.shape, lambda i: (0, 0), pipeline_mode=ONE_BUF)]
    return ops, specs


def _mm_rows_value(refs, npairs):
    acc = _dot(refs[0][...], refs[1][...])
    for t in range(1, npairs):
        acc = acc + _dot(refs[2 * t][...], refs[2 * t + 1][...])
    return acc


def _resid_norm_mod(x, mm, gate, g, shift, scale, tm=256):
    S = x.shape[0]
    tm = min(tm, S)
    skip = 2 * len(mm)

    def body(*refs):
        x_ref, gt_ref, g_ref, sh_ref, sc_ref, x1_ref, h_ref, m_ref = refs[skip:]
        mixed = _mm_rows_value(refs, len(mm))
        m_ref[...] = mixed
        x1 = x_ref[...] + (1.0 + gt_ref[...]) * mixed
        x1_ref[...] = x1
        r = lax.rsqrt(jnp.mean(x1 * x1, axis=-1, keepdims=True) + EPS)
        hn = (x1 * r) * g_ref[...]
        h_ref[...] = (hn * (1.0 + sc_ref[...]) + sh_ref[...]).astype(BF16)

    ops, specs = _mm_rows(mm, tm)
    return pl.pallas_call(
        body, name="mm_out_resid_norm2_mod", grid=(S // tm,),
        in_specs=specs + [_rows(tm), _vec(), _vec(), _vec(), _vec()],
        out_specs=[_rows(tm), _rows(tm), _rows(tm)],
        out_shape=[jax.ShapeDtypeStruct((S, D), F32), jax.ShapeDtypeStruct((S, D), BF16),
                   jax.ShapeDtypeStruct((S, D), F32)],
        compiler_params=_cp(1, 40),
    )(*ops, x, gate, g, shift, scale)


def _conv3(ext, w_ref, b_ref, cs):
    e1 = pltpu.roll(ext, 1, 0)
    e2 = pltpu.roll(ext, 2, 0)
    u = b_ref[:, cs] + w_ref[0:1, cs] * e2
    u = u + w_ref[1:2, cs] * e1
    u = u + w_ref[2:3, cs] * ext
    return u, e1, e2


def _up_conv_glu(h2, w_up_p, wc_p, bc_p, tm=256):
    S = h2.shape[0]
    tm = min(tm, S)
    widths = [2 * LANES] * (HB // (2 * LANES)) + ([LANES] if HB % (2 * LANES) else [])

    def body(h_ref, wu_ref, w_ref, b_ref, u_ref, a_ref, prev_ref):
        @pl.when(pl.program_id(0) == 0)
        def _():
            prev_ref[...] = jnp.zeros_like(prev_ref)

        hv = h_ref[...]
        for j in range(2):
            base = 0
            for wd in widths:
                us = []
                for off in (2 * j * HB + base, 2 * j * HB + HB + base):
                    cb = _dot(hv, wu_ref[:, off:off + wd]).astype(BF16)
                    u_ref[:, off:off + wd] = cb
                    for q in range(wd // LANES):
                        cs = slice(off + q * LANES, off + (q + 1) * LANES)
                        cq = cb[:, q * LANES:(q + 1) * LANES]
                        ext = jnp.concatenate([prev_ref[:, cs].astype(F32), cq.astype(F32)], axis=0)
                        us.append(_conv3(ext, w_ref, b_ref, cs)[0][16:])
                        prev_ref[:, cs] = cq[tm - 16:]
                nq = wd // LANES
                for q in range(nq):
                    val, gt = us[q], us[nq + q]
                    a_ref[:, j * HB + base + q * LANES:j * HB + base + (q + 1) * LANES] = (
                        val * (gt * _sigmoid_fast(gt))).astype(BF16)
                base += wd

    return pl.pallas_call(
        body, name="mm_up_conv_glu", grid=(S // tm,),
        in_specs=[pl.BlockSpec((tm, D), lambda i: (i, 0)),
                  pl.BlockSpec((D, 2 * D_FF), lambda i: (0, 0), pipeline_mode=ONE_BUF),
                  pl.BlockSpec((3, 2 * D_FF), lambda i: (0, 0)),
                  pl.BlockSpec((1, 2 * D_FF), lambda i: (0, 0))],
        out_specs=[pl.BlockSpec((tm, 2 * D_FF), lambda i: (i, 0)), pl.BlockSpec((tm, D_FF), lambda i: (i, 0))],
        out_shape=[jax.ShapeDtypeStruct((S, 2 * D_FF), BF16), jax.ShapeDtypeStruct((S, D_FF), BF16)],
        scratch_shapes=[pltpu.VMEM((16, 2 * D_FF), BF16)],
        compiler_params=_cp(1, 48),
    )(h2, w_up_p, wc_p, bc_p)


def _conv_glu_up_norm2_bwd(da, u0p, wc_p, bc_p, w_up_t, x1, dx2, g, scale, mixed, gate, tm=256, carried=()):
    S = u0p.shape[0]
    tm = min(tm, S)
    hb = tm // 16
    nlast = S // 16 - 1
    widths = [2 * LANES] * (HB // (2 * LANES)) + ([LANES] if HB % (2 * LANES) else [])

    def body(da_ref, dan_ref, u_ref, p_ref, n_ref, w_ref, b_ref, wt_ref, x_ref, dr_ref, g_ref, sc_ref, m_ref, gt_ref,
             o_ref, s_ref, dx_ref, dm_ref, s2_ref, acc_ref):
        i = pl.program_id(0)
        first = i == 0
        last = i == pl.num_programs(0) - 1

        @pl.when(first)
        def _():
            s_ref[...] = jnp.zeros_like(s_ref)
            s2_ref[...] = jnp.zeros_like(s2_ref)

        n = tm + 16
        started = False
        for j in range(2):
            base = 0
            for wd in widths:
                du0s = ([], [])
                for q in range(wd // LANES):
                    k0 = base + q * LANES
                    kc = slice(j * HB + k0, j * HB + k0 + LANES)
                    dae = jnp.concatenate([da_ref[:, kc].astype(F32),
                                           jnp.where(last, 0.0, dan_ref[:, kc].astype(F32))], axis=0)
                    halves = []
                    for off in (2 * j * HB + k0, 2 * j * HB + HB + k0):
                        cs = slice(off, off + LANES)
                        ext = jnp.concatenate([jnp.where(first, 0.0, p_ref[:, cs].astype(F32)),
                                               u_ref[:, cs].astype(F32), n_ref[:, cs].astype(F32)], axis=0)
                        u, e1, e2 = _conv3(ext, w_ref, b_ref, cs)
                        halves.append((u[16:], ext[16:16 + tm], e1[16:16 + tm], e2[16:16 + tm], cs))
                    val, gt = halves[0][0], halves[1][0]
                    sg = _sigmoid_fast(gt)
                    dus = (dae * (gt * sg), dae * val * (sg * (1.0 + gt * (1.0 - sg))))
                    for t, (du, (_, x0, x1_, x2, cs)) in enumerate(zip(dus, halves)):
                        du0 = (w_ref[2:3, cs] * du + w_ref[1:2, cs] * pltpu.roll(du, n - 1, 0)
                               + w_ref[0:1, cs] * pltpu.roll(du, n - 2, 0))[:tm].astype(BF16)
                        o_ref[:, cs] = du0
                        du0s[t].append(du0)
                        dut = du[:tm]
                        s_ref[0:1, cs] += jnp.sum(dut, axis=0, keepdims=True)
                        s_ref[1:2, cs] += jnp.sum(dut * x2, axis=0, keepdims=True)
                        s_ref[2:3, cs] += jnp.sum(dut * x1_, axis=0, keepdims=True)
                        s_ref[3:4, cs] += jnp.sum(dut * x0, axis=0, keepdims=True)
                for t, off in enumerate((2 * j * HB + base, 2 * j * HB + HB + base)):
                    lhs = du0s[t][0] if len(du0s[t]) == 1 else jnp.concatenate(du0s[t], axis=1)
                    part = _dot(lhs, wt_ref[off:off + wd, :])
                    if started:
                        acc_ref[...] += part
                    else:
                        acc_ref[...] = part
                        started = True
                base += wd

        xv = x_ref[...]
        dhv = acc_ref[...]
        r = lax.rsqrt(jnp.mean(xv * xv, axis=-1, keepdims=True) + EPS)
        nv = xv * r
        gv = g_ref[...]
        hn = nv * gv
        dhn = dhv * (1.0 + sc_ref[...])
        dn = dhn * gv
        dx = dr_ref[...] + r * (dn - nv * jnp.mean(dn * nv, axis=-1, keepdims=True))
        dx_ref[...] = dx
        dm_ref[...] = (dx * (1.0 + gt_ref[...])).astype(BF16)
        s2_ref[0:1, :] += jnp.sum(dhv, axis=0, keepdims=True)
        s2_ref[1:2, :] += jnp.sum(dhv * hn, axis=0, keepdims=True)
        s2_ref[2:3, :] += jnp.sum(dhn * nv, axis=0, keepdims=True)
        s2_ref[3:4, :] += jnp.sum(dx * m_ref[...], axis=0, keepdims=True)

    full = 2 * D_FF
    return _call(
        body, carried, [da, da, u0p, u0p, u0p, wc_p, bc_p, w_up_t, x1, dx2, g, scale, mixed, gate],
        name="conv_glu_up_norm2_bwd", grid=(S // tm,),
        in_specs=[pl.BlockSpec((tm, D_FF), lambda i: (i, 0)),
                  pl.BlockSpec((16, D_FF), lambda i: (jnp.minimum((i + 1) * hb, nlast), 0)),
                  pl.BlockSpec((tm, full), lambda i: (i, 0)),
                  pl.BlockSpec((16, full), lambda i: (jnp.maximum(i * hb - 1, 0), 0)),
                  pl.BlockSpec((16, full), lambda i: (jnp.minimum((i + 1) * hb, nlast), 0)),
                  pl.BlockSpec((3, full), lambda i: (0, 0)),
                  pl.BlockSpec((1, full), lambda i: (0, 0)),
                  pl.BlockSpec((full, D), lambda i: (0, 0), pipeline_mode=ONE_BUF),
                  _rows(tm), _rows(tm), _vec(), _vec(), _rows(tm), _vec()],
        out_specs=[pl.BlockSpec((tm, full), lambda i: (i, 0)), pl.BlockSpec((8, full), lambda i: (0, 0)),
                   _rows(tm), _rows(tm), pl.BlockSpec((8, D), lambda i: (0, 0))],
        out_shape=[jax.ShapeDtypeStruct((S, full), BF16), jax.ShapeDtypeStruct((8, full), F32),
                   jax.ShapeDtypeStruct((S, D), F32), jax.ShapeDtypeStruct((S, D), BF16),
                   jax.ShapeDtypeStruct((8, D), F32)],
        scratch_shapes=[pltpu.VMEM((tm, D), F32)],
        compiler_params=_cp(1, 56))


def _final_loss(x1, mm, gate2, g_final, target, tm=256):
    S = x1.shape[0]
    tm = min(tm, S)
    skip = 2 * len(mm)

    def body(*refs):
        x1_ref, gt_ref, g_ref, t_ref, dx_ref, dy_ref, s_ref = refs[skip:]

        @pl.when(pl.program_id(0) == 0)
        def _():
            s_ref[...] = jnp.zeros_like(s_ref)

        y2 = _mm_rows_value(refs, len(mm))
        og = 1.0 + gt_ref[...]
        x2 = x1_ref[...] + og * y2
        r = lax.rsqrt(jnp.mean(x2 * x2, axis=-1, keepdims=True) + EPS)
        n = x2 * r
        g = g_ref[...]
        err = n * g - t_ref[...]
        dy = err * (1.0 / D)
        dn = dy * g
        dx2 = r * (dn - n * jnp.mean(dn * n, axis=-1, keepdims=True))
        dx_ref[...] = dx2
        dy_ref[...] = (dx2 * og).astype(BF16)
        s_ref[0:1, :] += jnp.sum(dy * n, axis=0, keepdims=True)
        s_ref[1:2, :] += jnp.sum(dx2 * y2, axis=0, keepdims=True)
        s_ref[2:3, :] += jnp.sum(err * err, axis=0, keepdims=True)

    ops, specs = _mm_rows(mm, tm)
    return pl.pallas_call(
        body, name="mm_down_final_loss", grid=(S // tm,),
        in_specs=specs + [_rows(tm), _vec(), _vec(), _rows(tm)],
        out_specs=[_rows(tm), _rows(tm), pl.BlockSpec((8, D), lambda i: (0, 0))],
        out_shape=[jax.ShapeDtypeStruct((S, D), F32), jax.ShapeDtypeStruct((S, D), BF16),
                   jax.ShapeDtypeStruct((8, D), F32)],
        compiler_params=_cp(1, 40),
    )(*ops, x1, gate2, g_final, target)


def _norm_mod_bwd(dh, xin, dres, g, scale, mixed, gate, name, tm=256, carried=()):
    S = xin.shape[0]
    tm = min(tm, S)
    with_gate = mixed is not None
    fused = isinstance(dh, list)
    skip = 2 * len(dh) if fused else 1

    def body(*refs):
        if with_gate:
            x_ref, dr_ref, g_ref, sc_ref, m_ref, gt_ref, dx_ref, dm_ref, s_ref = refs[skip:]
        else:
            x_ref, dr_ref, g_ref, sc_ref, dx_ref, s_ref = refs[skip:]

        @pl.when(pl.program_id(0) == 0)
        def _():
            s_ref[...] = jnp.zeros_like(s_ref)

        xv = x_ref[...]
        dhv = _mm_rows_value(refs, len(dh)) if fused else refs[0][...]
        r = lax.rsqrt(jnp.mean(xv * xv, axis=-1, keepdims=True) + EPS)
        n = xv * r
        g = g_ref[...]
        hn = n * g
        dhn = dhv * (1.0 + sc_ref[...])
        dn = dhn * g
        dx = dr_ref[...] + r * (dn - n * jnp.mean(dn * n, axis=-1, keepdims=True))
        dx_ref[...] = dx
        s_ref[0:1, :] += jnp.sum(dhv, axis=0, keepdims=True)
        s_ref[1:2, :] += jnp.sum(dhv * hn, axis=0, keepdims=True)
        s_ref[2:3, :] += jnp.sum(dhn * n, axis=0, keepdims=True)
        if with_gate:
            dm_ref[...] = (dx * (1.0 + gt_ref[...])).astype(BF16)
            s_ref[3:4, :] += jnp.sum(dx * m_ref[...], axis=0, keepdims=True)

    ins, in_specs = _mm_rows(dh, tm) if fused else ([dh], [_rows(tm)])
    ins += [xin, dres, g, scale]
    in_specs += [_rows(tm), _rows(tm), _vec(), _vec()]
    out_specs = [_rows(tm)]
    out_shape = [jax.ShapeDtypeStruct((S, D), F32)]
    if with_gate:
        ins += [mixed, gate]
        in_specs += [_rows(tm), _vec()]
        out_specs.append(_rows(tm))
        out_shape.append(jax.ShapeDtypeStruct((S, D), BF16))
    out_specs.append(pl.BlockSpec((8, D), lambda i: (0, 0)))
    out_shape.append(jax.ShapeDtypeStruct((8, D), F32))
    return _call(body, carried, ins, name=name, grid=(S // tm,), in_specs=in_specs, out_specs=out_specs,
                 out_shape=out_shape, compiler_params=_cp(1, 48 if fused else None))


def _tri(n, rel):
    row = lax.broadcasted_iota(jnp.int32, (n, n), 0)
    col = lax.broadcasted_iota(jnp.int32, (n, n), 1)
    return {"gt": row > col, "ge": row >= col, "lt": row < col, "le": row <= col}[rel]


def _pair_diag(mask):
    u = jnp.where(mask, 1.0, 0.0).astype(BF16)
    z = jnp.zeros_like(u)
    return jnp.concatenate([jnp.concatenate([u, z], axis=1), jnp.concatenate([z, u], axis=1)], axis=0)


def _pair_rows(xp, lo_half):
    z = jnp.zeros_like(xp)
    return jnp.concatenate([jnp.where(lo_half, xp, z), jnp.where(lo_half, z, xp)], axis=0)


def _sb_scores(z, causal, diag):
    ls, ps, es = [], [], []
    for hh in range(2):
        zz = z[:, hh * QB:(hh + 1) * QB]
        e = jnp.exp(-jnp.abs(zz))
        l = -(jnp.maximum(zz, 0.0) + jnp.log(1.0 + e))
        ps.append(l + zz)
        ls.append(jnp.where(causal, l, 0.0) if diag else l)
        es.append(e)
    return ls, ps, es


def _sb_fwd(proj, carried=()):
    S = proj.shape[0]
    nq = S // QB

    def body(q_ref, k_ref, v_ref, o_ref, t_ref, c_ref, acc_ref, qs_ref):
        i = pl.program_id(0)
        causal = _tri(QB, "gt")
        usuf = _pair_diag(_tri(QB, "gt"))
        lo_half = lax.broadcasted_iota(jnp.int32, (QB, 128), 1) < DK
        qs_ref[...] = q_ref[...] * 0.125

        def block(j, diag, nr):
            rows = pl.ds(pl.multiple_of(j * QB, QB), QB)
            rs = slice(0, nr)
            pairs = range(H_SB // 2)
            cols = [slice(pr * 128, (pr + 1) * 128) for pr in pairs]
            zs = [_dot_nt(qs_ref[rs, cols[pr]], _pair_rows(k_ref[rows, cols[pr]], lo_half)) for pr in pairs]
            sc = [_sb_scores(zs[pr], causal, diag) for pr in pairs]
            sufs = []
            for pr in pairs:
                lh, ll = _split(jnp.concatenate(sc[pr][0], axis=1))
                sufs.append(_dot(lh, usuf) + _dot(ll, usuf))
            cmax = None
            wps = []
            for pr in pairs:
                ws = []
                for hh in range(2):
                    h = 2 * pr + hh
                    b = sufs[pr][:, hh * QB:(hh + 1) * QB]
                    if not diag:
                        b = b + c_ref[h, rs, 0:1]
                    w = jnp.exp(sc[pr][1][hh] + b)
                    ws.append((jnp.where(causal, w, 0.0) if diag else w).astype(BF16))
                    cn = b[:, 0:1] + sc[pr][0][hh][:, 0:1]
                    c_ref[h, rs, 0:1] = cn
                    cmax = cn if cmax is None else jnp.maximum(cmax, cn)
                wps.append(jnp.concatenate(ws, axis=1))
            for pr in pairs:
                upd = _dot(wps[pr], _pair_rows(v_ref[rows, cols[pr]], lo_half))
                if diag:
                    acc_ref[rs, cols[pr]] = upd
                else:
                    acc_ref[rs, cols[pr]] += upd
            lo = jnp.max(cmax[:SB_HEAD_ROWS])
            return (jnp.max(cmax[SB_HEAD_ROWS:]) if nr > SB_HEAD_ROWS else None), lo

        def cond_full(st):
            return jnp.logical_and(st[0] >= 0, st[1] > SB_SKIP)

        def step_full(st):
            return (st[0] - 1,) + block(st[0], False, QB)

        def cond_head(st):
            return jnp.logical_and(st[0] >= 0, st[1] > SB_SKIP)

        def step_head(st):
            return st[0] - 1, block(st[0], False, SB_HEAD_ROWS)[1]

        def first_two():
            has_prev = i > 0
            rows_d = pl.ds(pl.multiple_of(i * QB, QB), QB)
            rows_p = pl.ds(pl.multiple_of(jnp.maximum(i - 1, 0) * QB, QB), QB)
            pairs = range(H_SB // 2)
            cols = [slice(pr * 128, (pr + 1) * 128) for pr in pairs]

            def suffix(ls):
                lh, ll = _split(jnp.concatenate(ls, axis=1))
                return _dot(lh, usuf) + _dot(ll, usuf)

            zd = [_dot_nt(qs_ref[:, cols[pr]], _pair_rows(k_ref[rows_d, cols[pr]], lo_half)) for pr in pairs]
            zp = [_dot_nt(qs_ref[:, cols[pr]], _pair_rows(k_ref[rows_p, cols[pr]], lo_half)) for pr in pairs]
            scd = [_sb_scores(zd[pr], causal, True) for pr in pairs]
            scp = [_sb_scores(zp[pr], causal, False) for pr in pairs]
            lps = [[jnp.where(has_prev, l, 0.0) for l in scp[pr][0]] for pr in pairs]
            sufd = [suffix(scd[pr][0]) for pr in pairs]
            sufp = [suffix(lps[pr]) for pr in pairs]
            cmax = None
            wds, wps = [], []
            for pr in pairs:
                wd, wp = [], []
                for hh in range(2):
                    h = 2 * pr + hh
                    half = slice(hh * QB, (hh + 1) * QB)
                    bd = sufd[pr][:, half]
                    wd.append(jnp.where(causal, jnp.exp(scd[pr][1][hh] + bd), 0.0).astype(BF16))
                    bp = sufp[pr][:, half] + (bd[:, 0:1] + scd[pr][0][hh][:, 0:1])
                    wp.append(jnp.where(has_prev, jnp.exp(scp[pr][1][hh] + bp), 0.0).astype(BF16))
                    cn = bp[:, 0:1] + lps[pr][hh][:, 0:1]
                    c_ref[h, :, 0:1] = cn
                    cmax = cn if cmax is None else jnp.maximum(cmax, cn)
                wds.append(jnp.concatenate(wd, axis=1))
                wps.append(jnp.concatenate(wp, axis=1))
            for pr in pairs:
                acc_ref[:, cols[pr]] = (_dot(wds[pr], _pair_rows(v_ref[rows_d, cols[pr]], lo_half))
                                        + _dot(wps[pr], _pair_rows(v_ref[rows_p, cols[pr]], lo_half)))
            return jnp.max(cmax[SB_HEAD_ROWS:]), jnp.max(cmax[:SB_HEAD_ROWS])

        j, _, lo = lax.while_loop(cond_full, step_full, (i - 2,) + first_two())
        jfull = jnp.maximum(j + 1, 0)
        j, _ = lax.while_loop(cond_head, step_head, (j, lo))
        o_ref[...] = acc_ref[...].astype(BF16)
        t_ref[...] = jnp.zeros_like(t_ref)
        for h in range(H_SB):
            t_ref[h // 4, :, h % 4:h % 4 + 1] = c_ref[h, :, 0:1]
        t_ref[:, :, 8:9] = jnp.zeros((2, QB, 1), F32) + jnp.maximum(j + 1, 0).astype(F32)
        t_ref[:, :, 9:10] = jnp.zeros((2, QB, 1), F32) + jfull.astype(F32)

    return _call(
        body, carried, [proj, proj, proj], name="sb_fwd", grid=(nq,),
        in_specs=[pl.BlockSpec((QB, 512), lambda i: (i, 0)),
                  pl.BlockSpec((S, 512), lambda i: (0, 1), pipeline_mode=ONE_BUF),
                  pl.BlockSpec((S, 512), lambda i: (0, 2), pipeline_mode=ONE_BUF)],
        out_specs=[pl.BlockSpec((QB, 512), lambda i: (i, 0)),
                   pl.BlockSpec((2, QB, 128), lambda i: (0, i, 0))],
        out_shape=[jax.ShapeDtypeStruct((S, 512), BF16), jax.ShapeDtypeStruct((2, S, 128), F32)],
        scratch_shapes=[pltpu.VMEM((H_SB, QB, 128), F32), pltpu.VMEM((QB, 512), F32), pltpu.VMEM((QB, 512), BF16)],
        compiler_params=_cp(1, 40))


def _sb_bwd(proj, dcat, stats, carried=()):
    S = proj.shape[0]
    nq = S // QB

    def body(q_ref, k_ref, v_ref, do_ref, t_ref, dq_ref, dk_ref, dv_ref, dk_acc, dv_acc, dq_acc, pc_ref, qs_ref,
             qt_ref, dot_ref):
        i = pl.program_id(1)

        @pl.when(i == 0)
        def _():
            dk_acc[...] = jnp.zeros_like(dk_acc)
            dv_acc[...] = jnp.zeros_like(dv_acc)

        causal = _tri(QB, "gt")
        uin = _pair_diag(_tri(QB, "le"))
        uex = _pair_diag(_tri(QB, "lt"))
        lo_half = lax.broadcasted_iota(jnp.int32, (QB, 128), 1) < DK
        qs_ref[...] = q_ref[...] * 0.125
        lo_rows = lax.broadcasted_iota(jnp.int32, (128, QB), 0) < DK
        for pr in range(2):
            qt_ref[pr] = (q_ref[:, pr * 128:(pr + 1) * 128].astype(F32) * 0.125).T.astype(BF16)
            dot_ref[pr] = do_ref[:, pr * 128:(pr + 1) * 128].astype(F32).T.astype(BF16)
        pc_ref[...] = jnp.zeros_like(pc_ref)
        dq_acc[...] = jnp.zeros_like(dq_acc)
        jstart = jnp.max(t_ref[:, 8:9]).astype(jnp.int32)
        jfull = jnp.max(t_ref[:, 9:10]).astype(jnp.int32)

        def block(j, diag, nr):
            rows = pl.ds(pl.multiple_of(j * QB, QB), QB)
            rs = slice(0, nr)
            pairs = range(2)
            cols = [slice(pr * 128, (pr + 1) * 128) for pr in pairs]
            kbds = [_pair_rows(k_ref[rows, cols[pr]], lo_half) for pr in pairs]
            zs = [_dot_nt(qs_ref[rs, cols[pr]], kbds[pr]) for pr in pairs]
            dws = [_dot_nt(do_ref[rs, cols[pr]], _pair_rows(v_ref[rows, cols[pr]], lo_half)) for pr in pairs]
            sc = [_sb_scores(zs[pr], causal, diag) for pr in pairs]
            plins = []
            for pr in pairs:
                lh, ll = _split(jnp.concatenate(sc[pr][0], axis=1))
                plins.append(_dot(lh, uin) + _dot(ll, uin))
            wss, gss, gexs = [], [], []
            for pr in pairs:
                ws, gs = [], []
                for hh in range(2):
                    h = 2 * pr + hh
                    half = slice(hh * QB, (hh + 1) * QB)
                    b = (t_ref[rs, h:h + 1] - pc_ref[h, rs, 0:1]) - plins[pr][:, half]
                    w = jnp.exp(sc[pr][1][hh] + b)
                    if diag:
                        w = jnp.where(causal, w, 0.0)
                    ws.append(w)
                    gs.append(dws[pr][:, half] * w)
                wss.append(ws)
                gss.append(gs)
            for pr in pairs:
                gh, gl = _split(jnp.concatenate(gss[pr], axis=1))
                gexs.append(_dot(gh, uex) + _dot(gl, uex))
            dzbs = []
            for pr in pairs:
                dzs = []
                for hh in range(2):
                    h = 2 * pr + hh
                    half = slice(hh * QB, (hh + 1) * QB)
                    e = sc[pr][2][hh]
                    r = pl.reciprocal(1.0 + e, approx=True)
                    er = e * r
                    pos = zs[pr][:, half] >= 0.0
                    gx = gexs[pr][:, half]
                    g = gss[pr][hh]
                    dz = g * jnp.where(pos, er, r) - (gx + pc_ref[4 + h, rs, 0:1]) * jnp.where(pos, r, er)
                    dzs.append(jnp.where(causal, dz, 0.0) if diag else dz)
                    pc_ref[h, rs, 0:1] += plins[pr][:, half][:, QB - 1:QB]
                    pc_ref[4 + h, rs, 0:1] += gx[:, QB - 1:QB] + g[:, QB - 1:QB]
                dzbs.append(jnp.concatenate(dzs, axis=1).astype(BF16))
            for pr in pairs:
                dq_acc[rs, cols[pr]] += _dot(dzbs[pr], kbds[pr])
                r1 = _dot(qt_ref[pr, :, rs], dzbs[pr])
                dk_acc[pr, j] += jnp.where(lo_rows, r1[:, :QB], r1[:, QB:])
                r2 = _dot(dot_ref[pr, :, rs], jnp.concatenate(wss[pr], axis=1).astype(BF16))
                dv_acc[pr, j] += jnp.where(lo_rows, r2[:, :QB], r2[:, QB:])

        def step_head(j, carry):
            block(j, False, SB_HEAD_ROWS)
            return carry

        def step_full(j, carry):
            block(j, False, QB)
            return carry

        def last_two():
            has_prev = i > 0
            jp = jnp.maximum(i - 1, 0)
            rows_p = pl.ds(pl.multiple_of(jp * QB, QB), QB)
            rows_d = pl.ds(pl.multiple_of(i * QB, QB), QB)
            pairs = range(2)
            cols = [slice(pr * 128, (pr + 1) * 128) for pr in pairs]

            def sums(vals, u):
                hi, lo = _split(jnp.concatenate(vals, axis=1))
                return _dot(hi, u) + _dot(lo, u)

            kp = [_pair_rows(k_ref[rows_p, cols[pr]], lo_half) for pr in pairs]
            kd = [_pair_rows(k_ref[rows_d, cols[pr]], lo_half) for pr in pairs]
            zp = [_dot_nt(qs_ref[:, cols[pr]], kp[pr]) for pr in pairs]
            zd = [_dot_nt(qs_ref[:, cols[pr]], kd[pr]) for pr in pairs]
            dwp = [_dot_nt(do_ref[:, cols[pr]], _pair_rows(v_ref[rows_p, cols[pr]], lo_half)) for pr in pairs]
            dwd = [_dot_nt(do_ref[:, cols[pr]], _pair_rows(v_ref[rows_d, cols[pr]], lo_half)) for pr in pairs]
            scp = [_sb_scores(zp[pr], causal, False) for pr in pairs]
            scd = [_sb_scores(zd[pr], causal, True) for pr in pairs]
            lps = [[jnp.where(has_prev, l, 0.0) for l in scp[pr][0]] for pr in pairs]
            plp = [sums(lps[pr], uin) for pr in pairs]
            pld = [sums(scd[pr][0], uin) for pr in pairs]
            wps, wds, gps, gds = [], [], [], []
            for pr in pairs:
                wp, wd, gp, gd = [], [], [], []
                for hh in range(2):
                    h = 2 * pr + hh
                    half = slice(hh * QB, (hh + 1) * QB)
                    rest = t_ref[:, h:h + 1] - pc_ref[h, :, 0:1]
                    w = jnp.where(has_prev, jnp.exp(scp[pr][1][hh] + (rest - plp[pr][:, half])), 0.0)
                    wp.append(w)
                    gp.append(dwp[pr][:, half] * w)
                    rest = rest - plp[pr][:, half][:, QB - 1:QB]
                    w = jnp.where(causal, jnp.exp(scd[pr][1][hh] + (rest - pld[pr][:, half])), 0.0)
                    wd.append(w)
                    gd.append(dwd[pr][:, half] * w)
                wps.append(wp)
                wds.append(wd)
                gps.append(gp)
                gds.append(gd)
            gxp = [sums(gps[pr], uex) for pr in pairs]
            gxd = [sums(gds[pr], uex) for pr in pairs]
            for pr in pairs:
                dzp, dzd = [], []
                for hh in range(2):
                    h = 2 * pr + hh
                    half = slice(hh * QB, (hh + 1) * QB)
                    pg = pc_ref[4 + h, :, 0:1]
                    for z, e, g, gx, out, mask in ((zp[pr], scp[pr][2][hh], gps[pr][hh], gxp[pr], dzp, has_prev),
                                                   (zd[pr], scd[pr][2][hh], gds[pr][hh], gxd[pr], dzd, causal)):
                        r = pl.reciprocal(1.0 + e, approx=True)
                        er = e * r
                        pos = z[:, half] >= 0.0
                        dz = g * jnp.where(pos, er, r) - (gx[:, half] + pg) * jnp.where(pos, r, er)
                        out.append(jnp.where(mask, dz, 0.0))
                        pg = pg + gx[:, half][:, QB - 1:QB] + g[:, QB - 1:QB]
                dzpb = jnp.concatenate(dzp, axis=1).astype(BF16)
                dzdb = jnp.concatenate(dzd, axis=1).astype(BF16)
                dq_acc[:, cols[pr]] += _dot(dzpb, kp[pr]) + _dot(dzdb, kd[pr])
                for jj, dzb, ws in ((jp, dzpb, wps[pr]), (i, dzdb, wds[pr])):
                    r1 = _dot(qt_ref[pr], dzb)
                    dk_acc[pr, jj] += jnp.where(lo_rows, r1[:, :QB], r1[:, QB:])
                    r2 = _dot(dot_ref[pr], jnp.concatenate(ws, axis=1).astype(BF16))
                    dv_acc[pr, jj] += jnp.where(lo_rows, r2[:, :QB], r2[:, QB:])

        lax.fori_loop(jstart, jfull, step_head, 0)
        lax.fori_loop(jfull, i - 1, step_full, 0)
        last_two()
        dq_ref[...] = (dq_acc[...] * 0.125).astype(BF16)

        @pl.when(i == nq - 1)
        def _():
            def put(jj, carry):
                krows = pl.ds(pl.multiple_of(jj * QB, QB), QB)
                for pr in range(2):
                    dk_ref[krows, pr * 128:(pr + 1) * 128] = dk_acc[pr, jj].T.astype(BF16)
                    dv_ref[krows, pr * 128:(pr + 1) * 128] = dv_acc[pr, jj].T.astype(BF16)
                return carry
            lax.fori_loop(0, nq, put, 0)

    return _call(
        body, carried, [proj, proj, proj, dcat, stats], name="sb_bwd", grid=(2, nq),
        in_specs=[pl.BlockSpec((QB, 256), lambda g, i: (i, g)),
                  pl.BlockSpec((S, 256), lambda g, i: (0, 2 + g), pipeline_mode=ONE_BUF),
                  pl.BlockSpec((S, 256), lambda g, i: (0, 4 + g), pipeline_mode=ONE_BUF),
                  pl.BlockSpec((QB, 256), lambda g, i: (i, g)),
                  pl.BlockSpec((None, QB, 128), lambda g, i: (g, i, 0))],
        out_specs=[pl.BlockSpec((QB, 256), lambda g, i: (i, g)),
                   pl.BlockSpec((S, 256), lambda g, i: (0, g)),
                   pl.BlockSpec((S, 256), lambda g, i: (0, g))],
        out_shape=[jax.ShapeDtypeStruct((S, 512), BF16)] * 3,
        scratch_shapes=[pltpu.VMEM((2, nq, 128, QB), F32), pltpu.VMEM((2, nq, 128, QB), F32),
                        pltpu.VMEM((QB, 256), F32), pltpu.VMEM((8, QB, 128), F32), pltpu.VMEM((QB, 256), BF16),
                        pltpu.VMEM((2, 128, QB), BF16), pltpu.VMEM((2, 128, QB), BF16)],
        compiler_params=_cp(2, 56))


GLA_NC = 8
GLA_R = GLA_NC * CHUNK


def _chunk_tri(strict):
    row = lax.broadcasted_iota(jnp.int32, (GLA_R, GLA_R), 0)
    col = lax.broadcasted_iota(jnp.int32, (GLA_R, GLA_R), 1)
    m = jnp.logical_and(row // CHUNK == col // CHUNK, row > col if strict else row >= col)
    u = jnp.where(m, 1.0, 0.0).astype(BF16)
    return jnp.concatenate([u, u], axis=1)


def _per_chunk_rows(vals):
    return jnp.concatenate([jnp.broadcast_to(v, (CHUNK, v.shape[1])) for v in vals], axis=0)


def _head_blocks(st):
    row = lax.broadcasted_iota(jnp.int32, (H_GLA * DV, H_GLA * DK), 0)
    col = lax.broadcasted_iota(jnp.int32, (H_GLA * DV, H_GLA * DK), 1)
    t = jnp.concatenate([st.astype(BF16)] * H_GLA, axis=0)
    return jnp.where(row // DV == col // DK, t, jnp.zeros_like(t))


def _head_diag(big):
    head = lax.broadcasted_iota(jnp.int32, (DV, H_GLA * DK), 1) // DK
    out = big[0:DV]
    for h in range(1, H_GLA):
        out = jnp.where(head == h, big[h * DV:(h + 1) * DV], out)
    return out


def _gla_gate4(gf_ref, wfg_ref, bfg_ref):
    f = _dot(gf_ref[...], wfg_ref[...]) + bfg_ref[...]
    _, la, _ = _log_sigmoid_parts(f)
    lah, lal = _split(la * (1.0 / 16.0))
    cum = _dot(_chunk_tri(False), jnp.concatenate([lah, lal], axis=0))
    tots = [cum[(c + 1) * CHUNK - 1:(c + 1) * CHUNK, :] for c in range(GLA_NC)]
    return f, jnp.exp(_per_chunk_rows(tots) - cum), [jnp.exp(t) for t in tots]


def _gla_specs4(ns, rev):
    def ix(i):
        return ns - 1 - i if rev else i
    return [pl.BlockSpec((GLA_R, 256), lambda i: (ix(i), 6)),
            pl.BlockSpec((GLA_R, 256), lambda i: (ix(i), 7)),
            pl.BlockSpec((GLA_R, 512), lambda i: (ix(i), 4)),
            pl.BlockSpec((GLA_R, 512), lambda i: (ix(i), 5)),
            pl.BlockSpec((GLA_R, 128), lambda i: (ix(i), 24))]


def _gla_fwd(proj, wfg_p, bfg, ggla, carried=()):
    S = proj.shape[0]
    ns = S // GLA_R

    def body(q_ref, k_ref, v_ref, gg_ref, gf_ref, wfg_ref, bfg_ref, ggla_ref, o_ref, st_ref, state):
        @pl.when(pl.program_id(0) == 0)
        def _():
            state[...] = jnp.zeros_like(state)

        _, e, decs = _gla_gate4(gf_ref, wfg_ref, bfg_ref)
        kdec = (k_ref[...].astype(F32) * e).astype(BF16)
        rows = [slice(c * CHUNK, (c + 1) * CHUNK) for c in range(GLA_NC)]
        kvs = [_head_diag(_dot_tn(v_ref[rows[c], :], kdec[rows[c]])) for c in range(GLA_NC)]
        st = state[...]
        sts = []
        for c in range(GLA_NC):
            st = st * decs[c] + kvs[c]
            st_ref[c] = st
            sts.append(st)
        state[...] = st
        o = jnp.concatenate([_dot_nt(q_ref[rows[c], :] * 0.125, _head_blocks(sts[c])) for c in range(GLA_NC)], axis=0)
        for h in range(H_GLA):
            vs = slice(h * DV, (h + 1) * DV)
            oh = o[:, vs]
            ohn = oh * lax.rsqrt(jnp.mean(oh * oh, axis=-1, keepdims=True) + EPS)
            gg = gg_ref[:, vs].astype(F32)
            o_ref[:, vs] = ((ohn * ggla_ref[:, vs]) * (gg * _sigmoid(gg))).astype(BF16)

    return _call(
        body, carried, [proj, proj, proj, proj, proj, wfg_p, bfg, ggla], name="gla_fwd", grid=(ns,),
        in_specs=_gla_specs4(ns, False) + [pl.BlockSpec((128, 256), lambda i: (0, 0)),
                                           pl.BlockSpec((1, 256), lambda i: (0, 0)),
                                           pl.BlockSpec((1, 512), lambda i: (0, 0))],
        out_specs=[pl.BlockSpec((GLA_R, 512), lambda i: (i, 0)),
                   pl.BlockSpec((GLA_NC, 128, 256), lambda i: (i, 0, 0))],
        out_shape=[jax.ShapeDtypeStruct((S, 512), BF16), jax.ShapeDtypeStruct((S // CHUNK, 128, 256), F32)],
        scratch_shapes=[pltpu.VMEM((128, 256), F32)],
        compiler_params=_cp(1))


def _gla_bwd(dcat, proj, states, wfg_p, bfg, ggla, carried=()):
    S = proj.shape[0]
    ns = S // GLA_R

    def body(do_ref, q_ref, k_ref, v_ref, gg_ref, gf_ref, sc_ref, sp_ref, wfg_ref, bfg_ref, ggla_ref,
             dp_ref, s_ref, dw_ref, carry):
        sr = pl.program_id(0)

        @pl.when(sr == 0)
        def _():
            carry[...] = jnp.zeros_like(carry)
            s_ref[...] = jnp.zeros_like(s_ref)
            dw_ref[...] = jnp.zeros_like(dw_ref)

        f, e, decs = _gla_gate4(gf_ref, wfg_ref, bfg_ref)
        kf = k_ref[...].astype(F32) * e
        kdec = kf.astype(BF16)
        rows = [slice(c * CHUNK, (c + 1) * CHUNK) for c in range(GLA_NC)]
        sts = [sc_ref[c] for c in range(GLA_NC)]
        st_before = jnp.where(sr < ns - 1, sp_ref[0], 0.0)
        sbd = [_head_blocks(sts[c]) for c in range(GLA_NC)]
        qs = q_ref[...] * 0.125
        o = jnp.concatenate([_dot_nt(qs[rows[c]], sbd[c]) for c in range(GLA_NC)], axis=0)
        dobs = []
        for h in range(H_GLA):
            vs = slice(h * DV, (h + 1) * DV)
            oh = o[:, vs]
            rr = lax.rsqrt(jnp.mean(oh * oh, axis=-1, keepdims=True) + EPS)
            ohn = oh * rr
            gg = gg_ref[:, vs].astype(F32)
            sg = _sigmoid(gg)
            dout = do_ref[:, vs].astype(F32)
            gl = ggla_ref[:, vs]
            dp_ref[:, 1024 + h * DV:1024 + (h + 1) * DV] = (
                dout * (ohn * gl) * (sg * (1.0 + gg * (1.0 - sg)))).astype(BF16)
            dt1 = dout * (gg * sg)
            s_ref[0:1, vs] += jnp.sum(dt1 * ohn, axis=0, keepdims=True)
            dohn = dt1 * gl
            dobs.append((rr * (dohn - ohn * jnp.mean(dohn * ohn, axis=-1, keepdims=True))).astype(BF16))
        dob = jnp.concatenate(dobs, axis=1)
        dsout = []
        for c in range(GLA_NC):
            dp_ref[rows[c], 0:256] = (_dot(dob[rows[c]], sbd[c]) * 0.125).astype(BF16)
            dsout.append(_head_diag(_dot_tn(dob[rows[c]], qs[rows[c]])))
        g = carry[...]
        gts, ddecs = [None] * GLA_NC, [None] * GLA_NC
        for c in reversed(range(GLA_NC)):
            gts[c] = dsout[c] + g
            ddecs[c] = jnp.sum(gts[c] * (sts[c - 1] if c > 0 else st_before), axis=0, keepdims=True) * decs[c]
            g = gts[c] * decs[c]
        carry[...] = g
        dkds = []
        for c in range(GLA_NC):
            gbd = _head_blocks(gts[c])
            dkds.append(_dot(v_ref[rows[c], :], gbd))
            dp_ref[rows[c], 512:1024] = _dot_nt(kdec[rows[c]], gbd).astype(BF16)
        dkd = jnp.concatenate(dkds, axis=0)
        dp_ref[:, 256:512] = (dkd * e).astype(BF16)
        wh, wl = _split(dkd * kf)
        dla = _dot(_chunk_tri(True), jnp.concatenate([wh, wl], axis=0)) + _per_chunk_rows(ddecs)
        df = dla * _sigmoid(-f) * (1.0 / 16.0)
        dfb = df.astype(BF16)
        s_ref[1:2, 0:256] += jnp.sum(df, axis=0, keepdims=True)
        dw_ref[...] += _dot_tn(gf_ref[...], dfb)
        dp_ref[:, 1536:1664] = _dot_nt(dfb, wfg_ref[...]).astype(BF16)

    return _call(
        body, carried, [dcat, proj, proj, proj, proj, proj, states, states, wfg_p, bfg, ggla],
        name="gla_bwd", grid=(ns,),
        in_specs=[pl.BlockSpec((GLA_R, 512), lambda i: (ns - 1 - i, 1))] + _gla_specs4(ns, True) + [
            pl.BlockSpec((GLA_NC, 128, 256), lambda i: (ns - 1 - i, 0, 0)),
            pl.BlockSpec((1, 128, 256), lambda i: (jnp.maximum((ns - 1 - i) * GLA_NC - 1, 0), 0, 0)),
            pl.BlockSpec((128, 256), lambda i: (0, 0)),
            pl.BlockSpec((1, 256), lambda i: (0, 0)),
            pl.BlockSpec((1, 512), lambda i: (0, 0))],
        out_specs=[pl.BlockSpec((GLA_R, 1664), lambda i: (ns - 1 - i, 0)),
                   pl.BlockSpec((8, 512), lambda i: (0, 0)),
                   pl.BlockSpec((128, 256), lambda i: (0, 0))],
        out_shape=[jax.ShapeDtypeStruct((S, 1664), BF16), jax.ShapeDtypeStruct((8, 512), F32),
                   jax.ShapeDtypeStruct((128, 256), F32)],
        scratch_shapes=[pltpu.VMEM((128, 256), F32)],
        compiler_params=_cp(1))


def _sum_leading(a, name):
    n = a.shape[0]

    def body(a_ref, o_ref):
        acc = a_ref[0]
        for k in range(1, n):
            acc = acc + a_ref[k]
        o_ref[...] = acc

    return pl.pallas_call(
        body, name=name, out_shape=jax.ShapeDtypeStruct(a.shape[1:], F32),
        in_specs=[VMEM_SPEC], out_specs=VMEM_SPEC,
    )(a)


def _sum_chip(own, recv, name):
    R, C = own.shape
    tr, tc = _tile2d(R, C, 1024 * 1024)

    def body(o_ref, r_ref, p_ref):
        acc = o_ref[...]
        for k in range(3):
            acc = acc + r_ref[k].astype(F32)
        p_ref[...] = acc

    return pl.pallas_call(
        body, name=name, grid=(R // tr, C // tc),
        in_specs=[pl.BlockSpec((tr, tc), lambda i, j: (i, j)), pl.BlockSpec((3, tr, tc), lambda i, j: (0, i, j))],
        out_specs=pl.BlockSpec((tr, tc), lambda i, j: (i, j)),
        out_shape=jax.ShapeDtypeStruct((R, C), F32), compiler_params=_cp(2, 40),
    )(own, recv)


def _adamw(w, p, q, m, v, name):
    R, C = w.shape
    tr, tc = _tile2d(R, C, 1024 * 1024)
    two = q is not None

    def body(*refs):
        if two:
            w_ref, p_ref, q_ref, m_ref, v_ref, g_out, d_out, m_out, v_out = refs
            g = p_ref[...] + q_ref[...]
        else:
            w_ref, p_ref, m_ref, v_ref, g_out, d_out, m_out, v_out = refs
            g = p_ref[...]
        m2 = B1 * m_ref[...] + (1.0 - B1) * g
        v2 = B2 * v_ref[...] + (1.0 - B2) * (g * g)
        m_hat = m2 / (1.0 - B1 ** STEP)
        v_hat = v2 / (1.0 - B2 ** STEP)
        g_out[...] = g
        d_out[...] = -LR * (m_hat / (jnp.sqrt(v_hat) + EPS_A) + WD * w_ref[...])
        m_out[...] = m2
        v_out[...] = v2

    spec = pl.BlockSpec((tr, tc), lambda i, j: (i, j))
    ins = [w, p, q, m, v] if two else [w, p, m, v]
    return pl.pallas_call(
        body, name=name, grid=(R // tr, C // tc),
        in_specs=[spec] * len(ins), out_specs=[spec] * 4,
        out_shape=[jax.ShapeDtypeStruct((R, C), F32)] * 4, compiler_params=_cp(2, 40),
    )(*ins)


def _cols_to_chips(a, width):
    return a.reshape(a.shape[0], 4, width).swapaxes(0, 1)


def _chips_to_cols(a):
    return a.swapaxes(0, 1).reshape(a.shape[1], 4 * a.shape[2])


def _swap_mid(a):
    lead = a.shape[:-1]
    return a.reshape(lead + (2, 2, HB)).swapaxes(-3, -2).reshape(lead + (4 * HB,))


def kernel(x, c, w_ada, b_ada, g_norm1, w_in, w_fg2, b_fg2, g_gla_out, w_out, g_norm2, w_up, w_conv, b_conv, w_down, g_final, loss_target, m_w_ada, m_b_ada, m_g_norm1, m_w_in, m_w_fg2, m_b_fg2, m_g_gla_out, m_w_out, m_g_norm2, m_w_up, m_w_conv, m_b_conv, m_w_down, m_g_final, v_w_ada, v_b_ada, v_g_norm1, v_w_in, v_w_fg2, v_b_fg2, v_g_gla_out, v_w_out, v_g_norm2, v_w_up, v_w_conv, v_b_conv, v_w_down, v_g_final):
    xi, yi, ci = lax.axis_index("x"), lax.axis_index("y"), lax.axis_index("c")
    cidx = 2 * xi + yi
    didx = 4 * xi + 2 * yi + ci
    xs = x[0]
    tgt = loss_target[0]
    gfin = g_final.reshape(1, D)
    AW = D * 6 // 4

    c_all = _allgather8(c, "gather_c").reshape(8, D)
    c_pad = jnp.concatenate([c_all, jnp.zeros((8, D), F32)], axis=0)
    mod_part = _ada_fwd(c_pad, w_ada[0], lax.dynamic_slice(b_ada, (0, cidx * AW), (1, AW)))[:8]
    small = jnp.concatenate([mod_part.reshape(-1), w_conv.reshape(-1), w_fg2.reshape(-1)]).reshape(-1, 128)
    small_g = _allgather4(small, "gather_small").reshape(4, -1)
    mod = lax.dynamic_index_in_dim(small_g[:, :8 * AW].reshape(4, 8, AW), didx, axis=1, keepdims=False).reshape(1, 6 * D)
    shift1, scale1, gate1, shift2, scale2, gate2 = [mod[:, k * D:(k + 1) * D] for k in range(6)]
    o1 = 8 * AW
    o2 = o1 + 3 * HB
    wc_p = _swap_mid(_chips_to_cols(small_g[:, o1:o2].reshape(4, 3, HB)))
    bc_p = _swap_mid(b_conv)
    wfg_full = _chips_to_cols(small_g[:, o2:].reshape(4, RANK, 64))
    wfg_p = jnp.concatenate([wfg_full, jnp.zeros((128 - RANK, 256), F32)], axis=0).astype(BF16)

    w_in_t = _allgather4_split(w_in[0].T.astype(BF16), "gather_w_in").reshape(N_IN, D)
    w_in_t = jnp.concatenate([w_in_t, jnp.zeros((N_IN_P - N_IN, D), BF16)], axis=0)
    w_in_p = w_in_t.T

    (h, proj), (w_down_g,) = _norm_mod_mm(xs, g_norm1, shift1, scale1, w_in_p,
                                          carried=[("gather", w_down[0].astype(BF16), False)])
    w_down_f = w_down_g.reshape(D_FF, D)
    (o_gla, states), (w_out_g,) = _gla_fwd(proj, wfg_p, b_fg2, g_gla_out,
                                           carried=[("gather", w_out[0].astype(BF16), False)])
    w_out_f = w_out_g.reshape(D, D)
    (o_sb, stats), (w_up_g,) = _sb_fwd(proj, carried=[("gather", w_up[0].astype(BF16), False)])
    w_up_p = _swap_mid(_chips_to_cols(w_up_g))
    x1, h2, mixed = _resid_norm_mod(xs, [(o_sb, w_out_f[:512]), (o_gla, w_out_f[512:])],
                                    gate1, g_norm2, shift2, scale2)
    u0p, a = _up_conv_glu(h2, w_up_p, wc_p, bc_p)
    dx2, dy2, s_fin = _final_loss(x1, [(a, w_down_f)], gate2, gfin, tgt)

    da = _mm([(dy2, w_down_f.T)], BF16, "mm_down_t", 512, D_FF, 48)
    dw_down, dw_down_h = [t.reshape(4, D_FF // 4, D) for t in _mm_tn([a], dy2, "mm_dw_down", D, 1024, 60)]
    (du0p, s_conv, dx1, dmixed, s_n2), (rc_down,) = _conv_glu_up_norm2_bwd(
        da, u0p, wc_p, bc_p, w_up_p.T, x1, dx2, g_norm2, scale2, mixed, gate1,
        carried=[("scatter", dw_down_h, False)])
    dw_up, dw_up_h = _mm_tn([h2], du0p, "mm_dw_up", HB, 1024, 56)
    dcat = _mm([(dmixed, w_out_f.T)], BF16, "mm_out_t", 512, D)
    dw_out, dw_out_h = [t.reshape(4, D // 4, D) for t in _mm_tn([o_sb, o_gla], dmixed, "mm_dw_out", D, 512)]
    (dq, dk, dv), (rc_up,) = _sb_bwd(proj, dcat, stats, carried=[("scatter", dw_up_h, True)])
    (dp_gla, s_gla, dwfg), (rc_out,) = _gla_bwd(dcat, proj, states, wfg_p, b_fg2, g_gla_out,
                                                carried=[("scatter", dw_out_h, False)])
    dw_in, dw_in_h = _mm_tn([dq, dk, dv, dp_gla], h, "mm_dw_in", D, 1024, 60)
    dw_in_h = dw_in_h[0, :N_IN].reshape(4, N_IN // 4, D)
    dw_in_own = lax.dynamic_slice(dw_in[0], (cidx * (N_IN // 4), 0), (N_IN // 4, D))
    (gx, s_n1), (rc_in,) = _norm_mod_bwd(
        [(dq, w_in_t[:512]), (dk, w_in_t[512:1024]), (dv, w_in_t[1024:1536]), (dp_gla, w_in_t[1536:])],
        xs, dx1, g_norm1, scale1, None, None, "mm_in_t_norm1_bwd", carried=[("scatter", dw_in_h, False)])

    dmod = jnp.concatenate([s_n1[0], s_n1[1], s_n2[3], s_n2[0], s_n2[1], s_fin[1]])
    s_conv_n = _swap_mid(s_conv[:4])
    part = jnp.concatenate([dmod, s_n1[2], s_n2[2], s_fin[0], s_gla[0], s_gla[1, :256], s_conv_n[0],
                            s_conv_n[1:4].reshape(-1), dwfg[:RANK].reshape(-1),
                            jnp.broadcast_to(jnp.sum(s_fin[2]), (128,))]).reshape(-1, 128)
    parts = _allgather8(part, "gather_small_grads")
    tot = _sum_leading(parts, "sum_small_grads").reshape(-1)
    loss = 0.5 / D * tot[-1]
    dmod_all = parts.reshape(8, -1)[:, :6 * D]
    offs = [0]
    for n in (6 * D, D, D, D, 512, 256, 2 * D_FF, 3 * 2 * D_FF, RANK * 256):
        offs.append(offs[-1] + n)
    g_b_ada, g_g1, g_g2, g_gf, g_ggla, g_bfg, g_bconv, g_wconv_full, g_wfg_full = [
        tot[offs[k]:offs[k + 1]] for k in range(9)]
    g_wconv = lax.dynamic_index_in_dim(_cols_to_chips(g_wconv_full.reshape(3, 2 * D_FF), HB), cidx, 0, keepdims=False)
    g_wfg = lax.dynamic_index_in_dim(_cols_to_chips(g_wfg_full.reshape(RANK, 256), 64), cidx, 0, keepdims=False)

    dmod_pad = jnp.concatenate([dmod_all, jnp.zeros((8, 6 * D), F32)], axis=0)
    g_w_ada = _ada_bwd(c_pad, lax.dynamic_slice(dmod_pad, (0, cidx * AW), (16, AW)))

    def own(blocks, swapped=False):
        return lax.dynamic_index_in_dim(blocks, _slot(cidx, swapped), axis=0, keepdims=False)

    p_in = _sum_chip(dw_in_own, rc_in, "rs_w_in_sum")
    p_out = _sum_chip(own(dw_out), rc_out, "rs_w_out_sum")
    p_up = _sum_chip(own(dw_up, True), rc_up, "rs_w_up_sum")
    p_down = _sum_chip(own(dw_down), rc_down, "rs_w_down_sum")
    q_in, q_out, q_up, q_down = _pair_swap([p_in, p_out, p_up, p_down], "rs_swap")

    out = {}
    out["w_ada"] = _adamw(w_ada[0], g_w_ada, None, m_w_ada[0], v_w_ada[0], "adamw_w_ada")
    out["w_in"] = [t.T for t in _adamw(w_in[0].T, p_in, q_in, m_w_in[0].T, v_w_in[0].T, "adamw_w_in")]
    out["w_out"] = _adamw(w_out[0], p_out, q_out, m_w_out[0], v_w_out[0], "adamw_w_out")
    out["w_up"] = _adamw(w_up[0], p_up, q_up, m_w_up[0], v_w_up[0], "adamw_w_up")
    out["w_down"] = _adamw(w_down[0], p_down, q_down, m_w_down[0], v_w_down[0], "adamw_w_down")
    small_names = ["b_ada", "g_norm1", "w_fg2", "b_fg2", "g_gla_out", "g_norm2", "w_conv", "b_conv", "g_final"]
    small_w = [b_ada, g_norm1, w_fg2, b_fg2, g_gla_out, g_norm2, w_conv, b_conv, g_final]
    small_m = [m_b_ada, m_g_norm1, m_w_fg2, m_b_fg2, m_g_gla_out, m_g_norm2, m_w_conv, m_b_conv, m_g_final]
    small_v = [v_b_ada, v_g_norm1, v_w_fg2, v_b_fg2, v_g_gla_out, v_g_norm2, v_w_conv, v_b_conv, v_g_final]
    small_gr = [g_b_ada, g_g1, g_wfg, g_bfg, g_ggla, g_g2, g_wconv, g_bconv, g_gf]

    def pack(arrs):
        flat = jnp.concatenate([t.reshape(-1) for t in arrs])
        return jnp.concatenate([flat, jnp.zeros((-flat.shape[0]) % 1024, F32)]).reshape(-1, 128)

    packed = _adamw(pack(small_w), pack(small_gr), None, pack(small_m), pack(small_v), "adamw_small")
    off = 0
    for nm, wt in zip(small_names, small_w):
        n = wt.size
        out[nm] = [t.reshape(-1)[off:off + n].reshape(wt.shape) for t in packed]
        off += n
    for nm in ("w_ada", "w_in", "w_out", "w_up", "w_down"):
        out[nm] = [t[None] for t in out[nm]]

    names = ["w_ada", "b_ada", "g_norm1", "w_in", "w_fg2", "b_fg2", "g_gla_out", "w_out", "g_norm2", "w_up",
             "w_conv", "b_conv", "w_down", "g_final"]
    res = [loss, gx[None]]
    for k in range(4):
        res += [out[nm][k] for nm in names]
    return tuple(res)
```

```python
import functools

import jax
import jax.numpy as jnp
from jax import lax
from jax.experimental import pallas as pl
from jax.experimental.pallas import tpu as pltpu

F32 = jnp.float32
BF16 = jnp.bfloat16
MESH = pl.DeviceIdType.MESH

D = 1024
H_SB = 8
DK = 64
DV = 128
H_GLA = 4
CHUNK = 64
RANK = 16
N_IN = 3088
N_IN_P = 3200
D_FF = 2816
HB = D_FF // 2
LANES = 128
EPS = 1e-6
QB = 128
SB_SKIP = -120.0
SB_HEAD_ROWS = 64

LR, B1, B2, EPS_A, WD, STEP = 0.001, 0.9, 0.999, 1e-08, 0.01, 10

ANY = pl.BlockSpec(memory_space=pl.ANY)
VMEM_SPEC = pl.BlockSpec(memory_space=pltpu.VMEM)
ONE_BUF = pl.Buffered(1)


def _cp(ndim=0, vmem_mb=None):
    kw = {}
    if ndim:
        kw["dimension_semantics"] = ("arbitrary",) * ndim
    if vmem_mb:
        kw["vmem_limit_bytes"] = vmem_mb * 1024 * 1024
    return pltpu.CompilerParams(**kw)


def _dot(a, b):
    return jnp.dot(a, b, preferred_element_type=F32)


def _dot_nt(a, b):
    return lax.dot_general(a, b, (((1,), (1,)), ((), ())), preferred_element_type=F32)


def _dot_tn(a, b):
    return lax.dot_general(a, b, (((0,), (0,)), ((), ())), preferred_element_type=F32)


def _split(x):
    hi = x.astype(BF16)
    lo = (x - hi.astype(F32)).astype(BF16)
    return hi, lo


def _sigmoid(x):
    return jax.nn.sigmoid(x)


def _sigmoid_fast(x):
    return pl.reciprocal(1.0 + jnp.exp(-x), approx=True)


def _log_sigmoid_parts(z):
    e = jnp.exp(-jnp.abs(z))
    sp = jnp.log1p(e)
    return -(jnp.maximum(z, 0.0) + sp), jnp.minimum(z, 0.0) - sp, e


def _tile2d(rows, cols, budget=512 * 1024):
    best = None
    for t in range(8, rows + 1, 8):
        if rows % t == 0 and t * cols * 4 <= budget:
            best = t
    if best is not None:
        return best, cols
    best = LANES if cols % LANES == 0 else cols
    for t in range(LANES, cols + 1, LANES):
        if cols % t == 0 and rows * t * 4 <= budget:
            best = t
    return rows, best


def _flip(v, bit):
    return 1 - v if bit else v


def _allgather8(a, name):
    def body(a_ref, o_ref, ssem, rsem, lsem):
        x, y, c = lax.axis_index("x"), lax.axis_index("y"), lax.axis_index("c")
        me = 4 * x + 2 * y + c
        loc = pltpu.make_async_copy(a_ref, o_ref.at[me], lsem)
        loc.start()
        sends = []
        for r in range(1, 8):
            peer = (_flip(x, r & 4), _flip(y, r & 2), _flip(c, r & 1))
            cp = pltpu.make_async_remote_copy(
                src_ref=a_ref, dst_ref=o_ref.at[me], send_sem=ssem.at[r - 1], recv_sem=rsem.at[r - 1],
                device_id=peer, device_id_type=MESH)
            cp.start()
            sends.append(cp)
        for r in range(1, 8):
            peer = (_flip(x, r & 4), _flip(y, r & 2), _flip(c, r & 1))
            pidx = 4 * peer[0] + 2 * peer[1] + peer[2]
            pltpu.make_async_remote_copy(
                src_ref=a_ref, dst_ref=o_ref.at[pidx], send_sem=ssem.at[r - 1], recv_sem=rsem.at[r - 1],
                device_id=peer, device_id_type=MESH).wait_recv()
        for cp in sends:
            cp.wait_send()
        loc.wait()

    return pl.pallas_call(
        body, name=name,
        out_shape=jax.ShapeDtypeStruct((8,) + a.shape, a.dtype),
        in_specs=[VMEM_SPEC], out_specs=VMEM_SPEC,
        scratch_shapes=[pltpu.SemaphoreType.DMA((7,)), pltpu.SemaphoreType.DMA((7,)), pltpu.SemaphoreType.DMA],
    )(a)


def _allgather4(a, name):
    def body(a_ref, o_ref, ssem, rsem, lsem):
        x, y, c = lax.axis_index("x"), lax.axis_index("y"), lax.axis_index("c")
        me = 2 * x + y
        loc = pltpu.make_async_copy(a_ref, o_ref.at[me], lsem)
        loc.start()
        sends = []
        for r in range(1, 4):
            peer = (_flip(x, r & 2), _flip(y, r & 1), c)
            cp = pltpu.make_async_remote_copy(
                src_ref=a_ref, dst_ref=o_ref.at[me], send_sem=ssem.at[r - 1], recv_sem=rsem.at[r - 1],
                device_id=peer, device_id_type=MESH)
            cp.start()
            sends.append(cp)
        for r in range(1, 4):
            peer = (_flip(x, r & 2), _flip(y, r & 1), c)
            pidx = 2 * peer[0] + peer[1]
            pltpu.make_async_remote_copy(
                src_ref=a_ref, dst_ref=o_ref.at[pidx], send_sem=ssem.at[r - 1], recv_sem=rsem.at[r - 1],
                device_id=peer, device_id_type=MESH).wait_recv()
        for cp in sends:
            cp.wait_send()
        loc.wait()

    return pl.pallas_call(
        body, name=name,
        out_shape=jax.ShapeDtypeStruct((4,) + a.shape, a.dtype),
        in_specs=[ANY], out_specs=ANY,
        scratch_shapes=[pltpu.SemaphoreType.DMA((3,)), pltpu.SemaphoreType.DMA((3,)), pltpu.SemaphoreType.DMA],
    )(a)


def _allgather4_split(a, name):
    R, C = a.shape
    hc = C // 2

    def body(a_ref, o_ref, ssem, rsem, fssem, frsem, lsem):
        x, y, c = lax.axis_index("x"), lax.axis_index("y"), lax.axis_index("c")
        me = 2 * x + y
        sibling = (x, y, 1 - c)
        mine = pl.ds(pl.multiple_of(c * hc, hc), hc)
        theirs = pl.ds(pl.multiple_of((1 - c) * hc, hc), hc)
        loc = pltpu.make_async_copy(a_ref, o_ref.at[me], lsem)
        loc.start()
        peers = [(_flip(x, r & 2), _flip(y, r & 1), c) for r in range(1, 4)]
        pidx = [2 * p[0] + p[1] for p in peers]
        sends = []
        for k in range(3):
            cp = pltpu.make_async_remote_copy(
                src_ref=a_ref.at[:, mine], dst_ref=o_ref.at[me, :, mine], send_sem=ssem.at[k], recv_sem=rsem.at[k],
                device_id=peers[k], device_id_type=MESH)
            cp.start()
            sends.append(cp)
        for k in range(3):
            landed = o_ref.at[pidx[k], :, mine]
            pltpu.make_async_remote_copy(
                src_ref=landed, dst_ref=landed, send_sem=ssem.at[k], recv_sem=rsem.at[k],
                device_id=peers[k], device_id_type=MESH).wait_recv()
            cp = pltpu.make_async_remote_copy(
                src_ref=landed, dst_ref=landed, send_sem=fssem.at[k], recv_sem=frsem.at[k],
                device_id=sibling, device_id_type=MESH)
            cp.start()
            sends.append(cp)
        for k in range(3):
            got = o_ref.at[pidx[k], :, theirs]
            pltpu.make_async_remote_copy(
                src_ref=got, dst_ref=got, send_sem=fssem.at[k], recv_sem=frsem.at[k],
                device_id=sibling, device_id_type=MESH).wait_recv()
        for cp in sends:
            cp.wait_send()
        loc.wait()

    return pl.pallas_call(
        body, name=name,
        out_shape=jax.ShapeDtypeStruct((4,) + a.shape, a.dtype),
        in_specs=[ANY], out_specs=ANY,
        scratch_shapes=[pltpu.SemaphoreType.DMA((3,))] * 4 + [pltpu.SemaphoreType.DMA],
    )(a)


def _slot(chip, swapped):
    return 2 * (chip % 2) + chip // 2 if swapped else chip


def _pair_swap(ps, name):
    n = len(ps)

    def body(*refs):
        x, y, c = lax.axis_index("x"), lax.axis_index("y"), lax.axis_index("c")
        ssem, rsem = refs[2 * n], refs[2 * n + 1]
        cps = [pltpu.make_async_remote_copy(
            src_ref=refs[k], dst_ref=refs[n + k], send_sem=ssem.at[k], recv_sem=rsem.at[k],
            device_id=(x, y, 1 - c), device_id_type=MESH) for k in range(n)]
        for cp in cps:
            cp.start()
        for cp in cps:
            cp.wait()

    return pl.pallas_call(
        body, name=name,
        out_shape=[jax.ShapeDtypeStruct(p.shape, p.dtype) for p in ps],
        in_specs=[ANY] * n, out_specs=[ANY] * n,
        scratch_shapes=[pltpu.SemaphoreType.DMA((n,)), pltpu.SemaphoreType.DMA((n,))],
    )(*ps)


def _carried_copies(kind, src_ref, dst_ref, sems, swapped):
    ssem, rsem, lsem = sems
    x, y, c = lax.axis_index("x"), lax.axis_index("y"), lax.axis_index("c")
    me = 2 * x + y
    starts, recvs = [], []
    if kind == "gather":
        starts.append(pltpu.make_async_copy(src_ref, dst_ref.at[me], lsem))
    for r in range(1, 4):
        peer = (_flip(x, r & 2), _flip(y, r & 1), c)
        pidx = 2 * peer[0] + peer[1]
        if kind == "gather":
            src, dst, landed = src_ref, dst_ref.at[me], dst_ref.at[pidx]
        else:
            src = src_ref.at[2 * peer[1] + peer[0] if swapped else pidx]
            dst = landed = dst_ref.at[r - 1]
        starts.append(pltpu.make_async_remote_copy(
            src_ref=src, dst_ref=dst, send_sem=ssem.at[r - 1], recv_sem=rsem.at[r - 1],
            device_id=peer, device_id_type=MESH))
        recvs.append(pltpu.make_async_remote_copy(
            src_ref=src, dst_ref=landed, send_sem=ssem.at[r - 1], recv_sem=rsem.at[r - 1],
            device_id=peer, device_id_type=MESH))
    return starts, recvs


def _call(body, carried, operands, *, name, grid, in_specs, out_specs, out_shape, scratch_shapes=(),
          compiler_params=None):
    single = not isinstance(out_shape, (list, tuple))
    out_specs = [out_specs] if single else list(out_specs)
    out_shape = [out_shape] if single else list(out_shape)
    n_in, n_out, n_sc, nh = len(operands), len(out_shape), len(scratch_shapes), len(carried)

    def full(*refs):
        ins, h_in = refs[:n_in], refs[n_in:n_in + nh]
        o0 = n_in + nh
        outs, h_out = refs[o0:o0 + n_out], refs[o0 + n_out:o0 + n_out + nh]
        s0 = o0 + n_out + nh
        scratch, sems = refs[s0:s0 + n_sc], refs[s0 + n_sc:]
        first = last = None
        for d in range(len(grid)):
            f = pl.program_id(d) == 0
            l = pl.program_id(d) == pl.num_programs(d) - 1
            first = f if first is None else jnp.logical_and(first, f)
            last = l if last is None else jnp.logical_and(last, l)

        def copies(t):
            return _carried_copies(carried[t][0], h_in[t], h_out[t], sems[3 * t:3 * t + 3], carried[t][2])

        if nh:
            @pl.when(first)
            def _():
                for t in range(nh):
                    for cp in copies(t)[0]:
                        cp.start()

        body(*ins, *outs, *scratch)

        if nh:
            @pl.when(last)
            def _():
                for t in range(nh):
                    starts, recvs = copies(t)
                    for cp in recvs:
                        cp.wait_recv()
                    for cp in starts:
                        if carried[t][0] == "gather" and cp is starts[0]:
                            cp.wait()
                        else:
                            cp.wait_send()

    h_shapes = [jax.ShapeDtypeStruct(((4,) + arr.shape) if kind == "gather" else ((3,) + arr.shape[1:]), arr.dtype)
                for kind, arr, _ in carried]
    sem_shapes = [pltpu.SemaphoreType.DMA((3,)), pltpu.SemaphoreType.DMA((3,)), pltpu.SemaphoreType.DMA] * nh
    res = pl.pallas_call(
        full, name=name, grid=grid, in_specs=list(in_specs) + [ANY] * nh, out_specs=out_specs + [ANY] * nh,
        out_shape=out_shape + h_shapes, scratch_shapes=list(scratch_shapes) + sem_shapes,
        compiler_params=compiler_params,
    )(*operands, *[arr for _, arr, _ in carried])
    main = res[:n_out]
    return (main[0] if single else main), list(res[n_out:])


def _mm(pairs, out_dtype, name, tm, tn, vmem_mb=None, carried=()):
    S = pairs[0][0].shape[0]
    N = pairs[0][1].shape[1]
    tm = min(tm, S)
    np_ = len(pairs)

    def body(*refs):
        acc = _dot(refs[0][...], refs[1][...])
        for t in range(1, np_):
            acc = acc + _dot(refs[2 * t][...], refs[2 * t + 1][...])
        refs[-1][...] = acc.astype(refs[-1].dtype)

    in_specs, ops = [], []
    for a, w in pairs:
        in_specs += [pl.BlockSpec((tm, a.shape[1]), lambda n, i: (i, 0)),
                     pl.BlockSpec((w.shape[0], tn), lambda n, i: (0, n))]
        ops += [a, w]
    out, got = _call(
        body, carried, ops, name=name, grid=(N // tn, S // tm), in_specs=in_specs,
        out_specs=pl.BlockSpec((tm, tn), lambda n, i: (i, n)),
        out_shape=jax.ShapeDtypeStruct((S, N), out_dtype),
        compiler_params=_cp(2, vmem_mb))
    return (out, got) if carried else out


def _mm_tn(a_list, b, name, bn, tk, vmem_mb=None):
    S, N = b.shape
    ms = [a.shape[1] for a in a_list]
    M = sum(ms)
    tk = min(tk, S)
    na = len(a_list)

    def body(*refs):
        b_ref, o_ref, o16_ref = refs[na], refs[na + 1], refs[na + 2]

        @pl.when(pl.program_id(1) == 0)
        def _():
            o_ref[...] = jnp.zeros_like(o_ref)
        off = 0
        for t in range(na):
            o_ref[off:off + ms[t], :] += _dot_tn(refs[t][...], b_ref[...])
            off += ms[t]

        @pl.when(pl.program_id(1) == pl.num_programs(1) - 1)
        def _():
            o16_ref[...] = o_ref[...].astype(BF16)

    spec = pl.BlockSpec((None, M, bn), lambda n, k: (n, 0, 0), pipeline_mode=ONE_BUF)
    return pl.pallas_call(
        body, name=name, grid=(N // bn, S // tk),
        in_specs=[pl.BlockSpec((tk, m), lambda n, k: (k, 0)) for m in ms] + [pl.BlockSpec((tk, bn), lambda n, k: (k, n))],
        out_specs=[spec, spec],
        out_shape=[jax.ShapeDtypeStruct((N // bn, M, bn), F32), jax.ShapeDtypeStruct((N // bn, M, bn), BF16)],
        compiler_params=_cp(2, vmem_mb),
    )(*a_list, b)


def _ada_fwd(c_all, w_sh, b_sh):
    def body(c_ref, w_ref, b_ref, o_ref):
        cv = c_ref[...]
        sc = (cv * _sigmoid(cv)).astype(BF16)
        o_ref[...] = _dot(sc, w_ref[...].astype(BF16)) + b_ref[...]

    return pl.pallas_call(
        body, name="ada_fwd", out_shape=jax.ShapeDtypeStruct((c_all.shape[0], w_sh.shape[1]), F32),
        in_specs=[VMEM_SPEC] * 3, out_specs=VMEM_SPEC, compiler_params=_cp(0, 40),
    )(c_all, w_sh, b_sh)


def _ada_bwd(c_all, dmod_sh):
    def body(c_ref, d_ref, o_ref):
        cv = c_ref[...]
        sc = (cv * _sigmoid(cv)).astype(BF16)
        o_ref[...] = _dot_tn(sc, d_ref[...].astype(BF16))

    return pl.pallas_call(
        body, name="ada_bwd", out_shape=jax.ShapeDtypeStruct((c_all.shape[1], dmod_sh.shape[1]), F32),
        in_specs=[VMEM_SPEC] * 2, out_specs=VMEM_SPEC, compiler_params=_cp(0, 40),
    )(c_all, dmod_sh)


def _vec(tm_unused=None):
    return pl.BlockSpec((1, D), lambda i: (0, 0))


def _rows(tm, width=D):
    return pl.BlockSpec((tm, width), lambda i: (i, 0))


def _norm_mod_mm(x, g, shift, scale, w, tm=256, carried=()):
    S = x.shape[0]
    tm = min(tm, S)
    N = w.shape[1]

    def body(x_ref, g_ref, sh_ref, sc_ref, w_ref, h_ref, p_ref):
        xv = x_ref[...]
        r = lax.rsqrt(jnp.mean(xv * xv, axis=-1, keepdims=True) + EPS)
        hn = (xv * r) * g_ref[...]
        h = (hn * (1.0 + sc_ref[...]) + sh_ref[...]).astype(BF16)
        h_ref[...] = h
        p_ref[...] = _dot(h, w_ref[...]).astype(BF16)

    return _call(
        body, carried, [x, g, shift, scale, w], name="norm1_mod_mm_in", grid=(S // tm,),
        in_specs=[_rows(tm), _vec(), _vec(), _vec(), pl.BlockSpec(w.shape, lambda i: (0, 0), pipeline_mode=ONE_BUF)],
        out_specs=[_rows(tm), _rows(tm, N)],
        out_shape=[jax.ShapeDtypeStruct((S, D), BF16), jax.ShapeDtypeStruct((S, N), BF16)],
        compiler_params=_cp(1, 48))


def _mm_rows(pairs, tm):
    ops, specs = [], []
    for a, w in pairs:
        ops += [a, w]
        specs += [pl.BlockSpec((tm, a.shape[1]), lambda i: (i, 0)),
                  pl.BlockSpec(w.shape, lambda i: (0, 0), pipeline_mode=ONE_BUF)]
    return ops, specs


def _mm_rows_value(refs, npairs):
    acc = _dot(refs[0][...], refs[1][...])
    for t in range(1, npairs):
        acc = acc + _dot(refs[2 * t][...], refs[2 * t + 1][...])
    return acc


def _resid_norm_mod(x, mm, gate, g, shift, scale, tm=256):
    S = x.shape[0]
    tm = min(tm, S)
    skip = 2 * len(mm)

    def body(*refs):
        x_ref, gt_ref, g_ref, sh_ref, sc_ref, x1_ref, h_ref, m_ref = refs[skip:]
        mixed = _mm_rows_value(refs, len(mm))
        m_ref[...] = mixed.astype(BF16)
        x1 = x_ref[...] + (1.0 + gt_ref[...]) * mixed
        x1_ref[...] = x1
        r = lax.rsqrt(jnp.mean(x1 * x1, axis=-1, keepdims=True) + EPS)
        hn = (x1 * r) * g_ref[...]
        h_ref[...] = (hn * (1.0 + sc_ref[...]) + sh_ref[...]).astype(BF16)

    ops, specs = _mm_rows(mm, tm)
    return pl.pallas_call(
        body, name="mm_out_resid_norm2_mod", grid=(S // tm,),
        in_specs=specs + [_rows(tm), _vec(), _vec(), _vec(), _vec()],
        out_specs=[_rows(tm), _rows(tm), _rows(tm)],
        out_shape=[jax.ShapeDtypeStruct((S, D), F32), jax.ShapeDtypeStruct((S, D), BF16),
                   jax.ShapeDtypeStruct((S, D), BF16)],
        compiler_params=_cp(1, 40),
    )(*ops, x, gate, g, shift, scale)


def _conv3(ext, w_ref, b_ref, cs):
    e1 = pltpu.roll(ext, 1, 0)
    e2 = pltpu.roll(ext, 2, 0)
    u = b_ref[:, cs] + w_ref[0:1, cs] * e2
    u = u + w_ref[1:2, cs] * e1
    u = u + w_ref[2:3, cs] * ext
    return u, e1, e2


def _up_conv_glu(h2, w_up_p, wc_p, bc_p, tm=256):
    S = h2.shape[0]
    tm = min(tm, S)
    widths = [2 * LANES] * (HB // (2 * LANES)) + ([LANES] if HB % (2 * LANES) else [])

    def body(h_ref, wu_ref, w_ref, b_ref, u_ref, a_ref, prev_ref):
        @pl.when(pl.program_id(0) == 0)
        def _():
            prev_ref[...] = jnp.zeros_like(prev_ref)

        hv = h_ref[...]
        for j in range(2):
            base = 0
            for wd in widths:
                us = []
                for off in (2 * j * HB + base, 2 * j * HB + HB + base):
                    cb = _dot(hv, wu_ref[:, off:off + wd]).astype(BF16)
                    u_ref[:, off:off + wd] = cb
                    for q in range(wd // LANES):
                        cs = slice(off + q * LANES, off + (q + 1) * LANES)
                        cq = cb[:, q * LANES:(q + 1) * LANES]
                        ext = jnp.concatenate([prev_ref[:, cs].astype(F32), cq.astype(F32)], axis=0)
                        us.append(_conv3(ext, w_ref, b_ref, cs)[0][16:])
                        prev_ref[:, cs] = cq[tm - 16:]
                nq = wd // LANES
                for q in range(nq):
                    val, gt = us[q], us[nq + q]
                    a_ref[:, j * HB + base + q * LANES:j * HB + base + (q + 1) * LANES] = (
                        val * (gt * _sigmoid_fast(gt))).astype(BF16)
                base += wd

    return pl.pallas_call(
        body, name="mm_up_conv_glu", grid=(S // tm,),
        in_specs=[pl.BlockSpec((tm, D), lambda i: (i, 0)),
                  pl.BlockSpec((D, 2 * D_FF), lambda i: (0, 0), pipeline_mode=ONE_BUF),
                  pl.BlockSpec((3, 2 * D_FF), lambda i: (0, 0)),
                  pl.BlockSpec((1, 2 * D_FF), lambda i: (0, 0))],
        out_specs=[pl.BlockSpec((tm, 2 * D_FF), lambda i: (i, 0)), pl.BlockSpec((tm, D_FF), lambda i: (i, 0))],
        out_shape=[jax.ShapeDtypeStruct((S, 2 * D_FF), BF16), jax.ShapeDtypeStruct((S, D_FF), BF16)],
        scratch_shapes=[pltpu.VMEM((16, 2 * D_FF), BF16)],
        compiler_params=_cp(1, 48),
    )(h2, w_up_p, wc_p, bc_p)


def _conv_glu_up_norm2_bwd(da, u0p, wc_p, bc_p, w_up_t, x1, dx2, g, scale, mixed, gate, tm=256, carried=()):
    S = u0p.shape[0]
    tm = min(tm, S)
    hb = tm // 16
    nlast = S // 16 - 1
    widths = [2 * LANES] * (HB // (2 * LANES)) + ([LANES] if HB % (2 * LANES) else [])

    def body(da_ref, dan_ref, u_ref, p_ref, n_ref, w_ref, b_ref, wt_ref, x_ref, dr_ref, g_ref, sc_ref, m_ref, gt_ref,
             o_ref, s_ref, dx_ref, dm_ref, s2_ref, acc_ref):
        i = pl.program_id(0)
        first = i == 0
        last = i == pl.num_programs(0) - 1

        @pl.when(first)
        def _():
            s_ref[...] = jnp.zeros_like(s_ref)
            s2_ref[...] = jnp.zeros_like(s2_ref)

        n = tm + 16
        started = False
        for j in range(2):
            base = 0
            for wd in widths:
                du0s = ([], [])
                for q in range(wd // LANES):
                    k0 = base + q * LANES
                    kc = slice(j * HB + k0, j * HB + k0 + LANES)
                    dae = jnp.concatenate([da_ref[:, kc].astype(F32),
                                           jnp.where(last, 0.0, dan_ref[:, kc].astype(F32))], axis=0)
                    halves = []
                    for off in (2 * j * HB + k0, 2 * j * HB + HB + k0):
                        cs = slice(off, off + LANES)
                        ext = jnp.concatenate([jnp.where(first, 0.0, p_ref[:, cs].astype(F32)),
                                               u_ref[:, cs].astype(F32), n_ref[:, cs].astype(F32)], axis=0)
                        u, e1, e2 = _conv3(ext, w_ref, b_ref, cs)
                        halves.append((u[16:], ext[16:16 + tm], e1[16:16 + tm], e2[16:16 + tm], cs))
                    val, gt = halves[0][0], halves[1][0]
                    sg = _sigmoid_fast(gt)
                    dus = (dae * (gt * sg), dae * val * (sg * (1.0 + gt * (1.0 - sg))))
                    for t, (du, (_, x0, x1_, x2, cs)) in enumerate(zip(dus, halves)):
                        du0 = (w_ref[2:3, cs] * du + w_ref[1:2, cs] * pltpu.roll(du, n - 1, 0)
                               + w_ref[0:1, cs] * pltpu.roll(du, n - 2, 0))[:tm].astype(BF16)
                        o_ref[:, cs] = du0
                        du0s[t].append(du0)
                        dut = du[:tm]
                        s_ref[0:1, cs] += jnp.sum(dut, axis=0, keepdims=True)
                        s_ref[1:2, cs] += jnp.sum(dut * x2, axis=0, keepdims=True)
                        s_ref[2:3, cs] += jnp.sum(dut * x1_, axis=0, keepdims=True)
                        s_ref[3:4, cs] += jnp.sum(dut * x0, axis=0, keepdims=True)
                for t, off in enumerate((2 * j * HB + base, 2 * j * HB + HB + base)):
                    lhs = du0s[t][0] if len(du0s[t]) == 1 else jnp.concatenate(du0s[t], axis=1)
                    part = _dot(lhs, wt_ref[off:off + wd, :])
                    if started:
                        acc_ref[...] += part
                    else:
                        acc_ref[...] = part
                        started = True
                base += wd

        xv = x_ref[...]
        dhv = acc_ref[...]
        r = lax.rsqrt(jnp.mean(xv * xv, axis=-1, keepdims=True) + EPS)
        nv = xv * r
        gv = g_ref[...]
        hn = nv * gv
        dhn = dhv * (1.0 + sc_ref[...])
        dn = dhn * gv
        dx = dr_ref[...] + r * (dn - nv * jnp.mean(dn * nv, axis=-1, keepdims=True))
        dx_ref[...] = dx
        dm_ref[...] = (dx * (1.0 + gt_ref[...])).astype(BF16)
        s2_ref[0:1, :] += jnp.sum(dhv, axis=0, keepdims=True)
        s2_ref[1:2, :] += jnp.sum(dhv * hn, axis=0, keepdims=True)
        s2_ref[2:3, :] += jnp.sum(dhn * nv, axis=0, keepdims=True)
        s2_ref[3:4, :] += jnp.sum(dx * m_ref[...], axis=0, keepdims=True)

    full = 2 * D_FF
    return _call(
        body, carried, [da, da, u0p, u0p, u0p, wc_p, bc_p, w_up_t, x1, dx2, g, scale, mixed, gate],
        name="conv_glu_up_norm2_bwd", grid=(S // tm,),
        in_specs=[pl.BlockSpec((tm, D_FF), lambda i: (i, 0)),
                  pl.BlockSpec((16, D_FF), lambda i: (jnp.minimum((i + 1) * hb, nlast), 0)),
                  pl.BlockSpec((tm, full), lambda i: (i, 0)),
                  pl.BlockSpec((16, full), lambda i: (jnp.maximum(i * hb - 1, 0), 0)),
                  pl.BlockSpec((16, full), lambda i: (jnp.minimum((i + 1) * hb, nlast), 0)),
                  pl.BlockSpec((3, full), lambda i: (0, 0)),
                  pl.BlockSpec((1, full), lambda i: (0, 0)),
                  pl.BlockSpec((full, D), lambda i: (0, 0), pipeline_mode=ONE_BUF),
                  _rows(tm), _rows(tm), _vec(), _vec(), _rows(tm), _vec()],
        out_specs=[pl.BlockSpec((tm, full), lambda i: (i, 0)), pl.BlockSpec((8, full), lambda i: (0, 0)),
                   _rows(tm), _rows(tm), pl.BlockSpec((8, D), lambda i: (0, 0))],
        out_shape=[jax.ShapeDtypeStruct((S, full), BF16), jax.ShapeDtypeStruct((8, full), F32),
                   jax.ShapeDtypeStruct((S, D), F32), jax.ShapeDtypeStruct((S, D), BF16),
                   jax.ShapeDtypeStruct((8, D), F32)],
        scratch_shapes=[pltpu.VMEM((tm, D), F32)],
        compiler_params=_cp(1, 56))


def _final_loss(x1, mm, gate2, g_final, target, tm=256):
    S = x1.shape[0]
    tm = min(tm, S)
    skip = 2 * len(mm)

    def body(*refs):
        x1_ref, gt_ref, g_ref, t_ref, dx_ref, dy_ref, s_ref = refs[skip:]

        @pl.when(pl.program_id(0) == 0)
        def _():
            s_ref[...] = jnp.zeros_like(s_ref)

        y2 = _mm_rows_value(refs, len(mm))
        og = 1.0 + gt_ref[...]
        x2 = x1_ref[...] + og * y2
        r = lax.rsqrt(jnp.mean(x2 * x2, axis=-1, keepdims=True) + EPS)
        n = x2 * r
        g = g_ref[...]
        err = n * g - t_ref[...]
        dy = err * (1.0 / D)
        dn = dy * g
        dx2 = r * (dn - n * jnp.mean(dn * n, axis=-1, keepdims=True))
        dx_ref[...] = dx2
        dy_ref[...] = (dx2 * og).astype(BF16)
        s_ref[0:1, :] += jnp.sum(dy * n, axis=0, keepdims=True)
        s_ref[1:2, :] += jnp.sum(dx2 * y2, axis=0, keepdims=True)
        s_ref[2:3, :] += jnp.sum(err * err, axis=0, keepdims=True)

    ops, specs = _mm_rows(mm, tm)
    return pl.pallas_call(
        body, name="mm_down_final_loss", grid=(S // tm,),
        in_specs=specs + [_rows(tm), _vec(), _vec(), _rows(tm)],
        out_specs=[_rows(tm), _rows(tm), pl.BlockSpec((8, D), lambda i: (0, 0))],
        out_shape=[jax.ShapeDtypeStruct((S, D), F32), jax.ShapeDtypeStruct((S, D), BF16),
                   jax.ShapeDtypeStruct((8, D), F32)],
        compiler_params=_cp(1, 40),
    )(*ops, x1, gate2, g_final, target)


def _norm_mod_bwd(dh, xin, dres, g, scale, mixed, gate, name, tm=256, carried=()):
    S = xin.shape[0]
    tm = min(tm, S)
    with_gate = mixed is not None
    fused = isinstance(dh, list)
    skip = 2 * len(dh) if fused else 1

    def body(*refs):
        if with_gate:
            x_ref, dr_ref, g_ref, sc_ref, m_ref, gt_ref, dx_ref, dm_ref, s_ref = refs[skip:]
        else:
            x_ref, dr_ref, g_ref, sc_ref, dx_ref, s_ref = refs[skip:]

        @pl.when(pl.program_id(0) == 0)
        def _():
            s_ref[...] = jnp.zeros_like(s_ref)

        xv = x_ref[...]
        dhv = _mm_rows_value(refs, len(dh)) if fused else refs[0][...]
        r = lax.rsqrt(jnp.mean(xv * xv, axis=-1, keepdims=True) + EPS)
        n = xv * r
        g = g_ref[...]
        hn = n * g
        dhn = dhv * (1.0 + sc_ref[...])
        dn = dhn * g
        dx = dr_ref[...] + r * (dn - n * jnp.mean(dn * n, axis=-1, keepdims=True))
        dx_ref[...] = dx
        s_ref[0:1, :] += jnp.sum(dhv, axis=0, keepdims=True)
        s_ref[1:2, :] += jnp.sum(dhv * hn, axis=0, keepdims=True)
        s_ref[2:3, :] += jnp.sum(dhn * n, axis=0, keepdims=True)
        if with_gate:
            dm_ref[...] = (dx * (1.0 + gt_ref[...])).astype(BF16)
            s_ref[3:4, :] += jnp.sum(dx * m_ref[...], axis=0, keepdims=True)

    ins, in_specs = _mm_rows(dh, tm) if fused else ([dh], [_rows(tm)])
    ins += [xin, dres, g, scale]
    in_specs += [_rows(tm), _rows(tm), _vec(), _vec()]
    out_specs = [_rows(tm)]
    out_shape = [jax.ShapeDtypeStruct((S, D), F32)]
    if with_gate:
        ins += [mixed, gate]
        in_specs += [_rows(tm), _vec()]
        out_specs.append(_rows(tm))
        out_shape.append(jax.ShapeDtypeStruct((S, D), BF16))
    out_specs.append(pl.BlockSpec((8, D), lambda i: (0, 0)))
    out_shape.append(jax.ShapeDtypeStruct((8, D), F32))
    return _call(body, carried, ins, name=name, grid=(S // tm,), in_specs=in_specs, out_specs=out_specs,
                 out_shape=out_shape, compiler_params=_cp(1, 48 if fused else None))


def _tri(n, rel):
    row = lax.broadcasted_iota(jnp.int32, (n, n), 0)
    col = lax.broadcasted_iota(jnp.int32, (n, n), 1)
    return {"gt": row > col, "ge": row >= col, "lt": row < col, "le": row <= col}[rel]


def _pair_diag(mask):
    u = jnp.where(mask, 1.0, 0.0).astype(BF16)
    z = jnp.zeros_like(u)
    return jnp.concatenate([jnp.concatenate([u, z], axis=1), jnp.concatenate([z, u], axis=1)], axis=0)


def _pair_rows(xp, lo_half):
    z = jnp.zeros_like(xp)
    return jnp.concatenate([jnp.where(lo_half, xp, z), jnp.where(lo_half, z, xp)], axis=0)


def _sb_scores(z, causal, diag):
    ls, ps, es = [], [], []
    for hh in range(2):
        zz = z[:, hh * QB:(hh + 1) * QB]
        e = jnp.exp(-jnp.abs(zz))
        l = -(jnp.maximum(zz, 0.0) + jnp.log(1.0 + e))
        ps.append(l + zz)
        ls.append(jnp.where(causal, l, 0.0) if diag else l)
        es.append(e)
    return ls, ps, es


def _sb_fwd(proj, carried=()):
    S = proj.shape[0]
    nq = S // QB

    def body(q_ref, k_ref, v_ref, o_ref, t_ref, c_ref, acc_ref, qs_ref):
        i = pl.program_id(0)
        causal = _tri(QB, "gt")
        usuf = _pair_diag(_tri(QB, "gt"))
        lo_half = lax.broadcasted_iota(jnp.int32, (QB, 128), 1) < DK
        qs_ref[...] = q_ref[...] * 0.125

        def block(j, diag, nr):
            rows = pl.ds(pl.multiple_of(j * QB, QB), QB)
            rs = slice(0, nr)
            pairs = range(H_SB // 2)
            cols = [slice(pr * 128, (pr + 1) * 128) for pr in pairs]
            zs = [_dot_nt(qs_ref[rs, cols[pr]], _pair_rows(k_ref[rows, cols[pr]], lo_half)) for pr in pairs]
            sc = [_sb_scores(zs[pr], causal, diag) for pr in pairs]
            sufs = []
            for pr in pairs:
                lh, ll = _split(jnp.concatenate(sc[pr][0], axis=1))
                sufs.append(_dot(lh, usuf) + _dot(ll, usuf))
            cmax = None
            wps = []
            for pr in pairs:
                ws = []
                for hh in range(2):
                    h = 2 * pr + hh
                    b = sufs[pr][:, hh * QB:(hh + 1) * QB]
                    if not diag:
                        b = b + c_ref[h, rs, 0:1]
                    w = jnp.exp(sc[pr][1][hh] + b)
                    ws.append((jnp.where(causal, w, 0.0) if diag else w).astype(BF16))
                    cn = b[:, 0:1] + sc[pr][0][hh][:, 0:1]
                    c_ref[h, rs, 0:1] = cn
                    cmax = cn if cmax is None else jnp.maximum(cmax, cn)
                wps.append(jnp.concatenate(ws, axis=1))
            for pr in pairs:
                upd = _dot(wps[pr], _pair_rows(v_ref[rows, cols[pr]], lo_half))
                if diag:
                    acc_ref[rs, cols[pr]] = upd
                else:
                    acc_ref[rs, cols[pr]] += upd
            lo = jnp.max(cmax[:SB_HEAD_ROWS])
            return (jnp.max(cmax[SB_HEAD_ROWS:]) if nr > SB_HEAD_ROWS else None), lo

        def cond_full(st):
            return jnp.logical_and(st[0] >= 0, st[1] > SB_SKIP)

        def step_full(st):
            return (st[0] - 1,) + block(st[0], False, QB)

        def cond_head(st):
            return jnp.logical_and(st[0] >= 0, st[1] > SB_SKIP)

        def step_head(st):
            return st[0] - 1, block(st[0], False, SB_HEAD_ROWS)[1]

        def first_two():
            has_prev = i > 0
            rows_d = pl.ds(pl.multiple_of(i * QB, QB), QB)
            rows_p = pl.ds(pl.multiple_of(jnp.maximum(i - 1, 0) * QB, QB), QB)
            pairs = range(H_SB // 2)
            cols = [slice(pr * 128, (pr + 1) * 128) for pr in pairs]

            def suffix(ls):
                lh, ll = _split(jnp.concatenate(ls, axis=1))
                return _dot(lh, usuf) + _dot(ll, usuf)

            zd = [_dot_nt(qs_ref[:, cols[pr]], _pair_rows(k_ref[rows_d, cols[pr]], lo_half)) for pr in pairs]
            zp = [_dot_nt(qs_ref[:, cols[pr]], _pair_rows(k_ref[rows_p, cols[pr]], lo_half)) for pr in pairs]
            scd = [_sb_scores(zd[pr], causal, True) for pr in pairs]
            scp = [_sb_scores(zp[pr], causal, False) for pr in pairs]
            lps = [[jnp.where(has_prev, l, 0.0) for l in scp[pr][0]] for pr in pairs]
            sufd = [suffix(scd[pr][0]) for pr in pairs]
            sufp = [suffix(lps[pr]) for pr in pairs]
            cmax = None
            wds, wps = [], []
            for pr in pairs:
                wd, wp = [], []
                for hh in range(2):
                    h = 2 * pr + hh
                    half = slice(hh * QB, (hh + 1) * QB)
                    bd = sufd[pr][:, half]
                    wd.append(jnp.where(causal, jnp.exp(scd[pr][1][hh] + bd), 0.0).astype(BF16))
                    bp = sufp[pr][:, half] + (bd[:, 0:1] + scd[pr][0][hh][:, 0:1])
                    wp.append(jnp.where(has_prev, jnp.exp(scp[pr][1][hh] + bp), 0.0).astype(BF16))
                    cn = bp[:, 0:1] + lps[pr][hh][:, 0:1]
                    c_ref[h, :, 0:1] = cn
                    cmax = cn if cmax is None else jnp.maximum(cmax, cn)
                wds.append(jnp.concatenate(wd, axis=1))
                wps.append(jnp.concatenate(wp, axis=1))
            for pr in pairs:
                acc_ref[:, cols[pr]] = (_dot(wds[pr], _pair_rows(v_ref[rows_d, cols[pr]], lo_half))
                                        + _dot(wps[pr], _pair_rows(v_ref[rows_p, cols[pr]], lo_half)))
            return jnp.max(cmax[SB_HEAD_ROWS:]), jnp.max(cmax[:SB_HEAD_ROWS])

        j, _, lo = lax.while_loop(cond_full, step_full, (i - 2,) + first_two())
        jfull = jnp.maximum(j + 1, 0)
        j, _ = lax.while_loop(cond_head, step_head, (j, lo))
        o_ref[...] = acc_ref[...].astype(BF16)
        t_ref[...] = jnp.zeros_like(t_ref)
        for h in range(H_SB):
            t_ref[h // 4, :, h % 4:h % 4 + 1] = c_ref[h, :, 0:1]
        t_ref[:, :, 8:9] = jnp.zeros((2, QB, 1), F32) + jnp.maximum(j + 1, 0).astype(F32)
        t_ref[:, :, 9:10] = jnp.zeros((2, QB, 1), F32) + jfull.astype(F32)

    return _call(
        body, carried, [proj, proj, proj], name="sb_fwd", grid=(nq,),
        in_specs=[pl.BlockSpec((QB, 512), lambda i: (i, 0)),
                  pl.BlockSpec((S, 512), lambda i: (0, 1), pipeline_mode=ONE_BUF),
                  pl.BlockSpec((S, 512), lambda i: (0, 2), pipeline_mode=ONE_BUF)],
        out_specs=[pl.BlockSpec((QB, 512), lambda i: (i, 0)),
                   pl.BlockSpec((2, QB, 128), lambda i: (0, i, 0))],
        out_shape=[jax.ShapeDtypeStruct((S, 512), BF16), jax.ShapeDtypeStruct((2, S, 128), F32)],
        scratch_shapes=[pltpu.VMEM((H_SB, QB, 128), F32), pltpu.VMEM((QB, 512), F32), pltpu.VMEM((QB, 512), BF16)],
        compiler_params=_cp(1, 40))


def _sb_bwd(proj, dcat, stats, carried=()):
    S = proj.shape[0]
    nq = S // QB

    def body(q_ref, k_ref, v_ref, do_ref, t_ref, dq_ref, dk_ref, dv_ref, dk_acc, dv_acc, dq_acc, pc_ref, qs_ref,
             qt_ref, dot_ref):
        i = pl.program_id(1)

        @pl.when(i == 0)
        def _():
            dk_acc[...] = jnp.zeros_like(dk_acc)
            dv_acc[...] = jnp.zeros_like(dv_acc)

        causal = _tri(QB, "gt")
        uin = _pair_diag(_tri(QB, "le"))
        uex = _pair_diag(_tri(QB, "lt"))
        lo_half = lax.broadcasted_iota(jnp.int32, (QB, 128), 1) < DK
        qs_ref[...] = q_ref[...] * 0.125
        lo_rows = lax.broadcasted_iota(jnp.int32, (128, QB), 0) < DK
        for pr in range(2):
            qt_ref[pr] = (q_ref[:, pr * 128:(pr + 1) * 128].astype(F32) * 0.125).T.astype(BF16)
            dot_ref[pr] = do_ref[:, pr * 128:(pr + 1) * 128].astype(F32).T.astype(BF16)
        pc_ref[...] = jnp.zeros_like(pc_ref)
        dq_acc[...] = jnp.zeros_like(dq_acc)
        jstart = jnp.max(t_ref[:, 8:9]).astype(jnp.int32)
        jfull = jnp.max(t_ref[:, 9:10]).astype(jnp.int32)

        def block(j, diag, nr):
            rows = pl.ds(pl.multiple_of(j * QB, QB), QB)
            rs = slice(0, nr)
            pairs = range(2)
            cols = [slice(pr * 128, (pr + 1) * 128) for pr in pairs]
            kbds = [_pair_rows(k_ref[rows, cols[pr]], lo_half) for pr in pairs]
            zs = [_dot_nt(qs_ref[rs, cols[pr]], kbds[pr]) for pr in pairs]
            dws = [_dot_nt(do_ref[rs, cols[pr]], _pair_rows(v_ref[rows, cols[pr]], lo_half)) for pr in pairs]
            sc = [_sb_scores(zs[pr], causal, diag) for pr in pairs]
            plins = []
            for pr in pairs:
                lh, ll = _split(jnp.concatenate(sc[pr][0], axis=1))
                plins.append(_dot(lh, uin) + _dot(ll, uin))
            wss, gss, gexs = [], [], []
            for pr in pairs:
                ws, gs = [], []
                for hh in range(2):
                    h = 2 * pr + hh
                    half = slice(hh * QB, (hh + 1) * QB)
                    b = (t_ref[rs, h:h + 1] - pc_ref[h, rs, 0:1]) - plins[pr][:, half]
                    w = jnp.exp(sc[pr][1][hh] + b)
                    if diag:
                        w = jnp.where(causal, w, 0.0)
                    ws.append(w)
                    gs.append(dws[pr][:, half] * w)
                wss.append(ws)
                gss.append(gs)
            for pr in pairs:
                gh, gl = _split(jnp.concatenate(gss[pr], axis=1))
                gexs.append(_dot(gh, uex) + _dot(gl, uex))
            dzbs = []
            for pr in pairs:
                dzs = []
                for hh in range(2):
                    h = 2 * pr + hh
                    half = slice(hh * QB, (hh + 1) * QB)
                    e = sc[pr][2][hh]
                    r = pl.reciprocal(1.0 + e, approx=True)
                    er = e * r
                    pos = zs[pr][:, half] >= 0.0
                    gx = gexs[pr][:, half]
                    g = gss[pr][hh]
                    dz = g * jnp.where(pos, er, r) - (gx + pc_ref[4 + h, rs, 0:1]) * jnp.where(pos, r, er)
                    dzs.append(jnp.where(causal, dz, 0.0) if diag else dz)
                    pc_ref[h, rs, 0:1] += plins[pr][:, half][:, QB - 1:QB]
                    pc_ref[4 + h, rs, 0:1] += gx[:, QB - 1:QB] + g[:, QB - 1:QB]
                dzbs.append(jnp.concatenate(dzs, axis=1).astype(BF16))
            for pr in pairs:
                dq_acc[rs, cols[pr]] += _dot(dzbs[pr], kbds[pr])
                r1 = _dot(qt_ref[pr, :, rs], dzbs[pr])
                dk_acc[pr, j] += jnp.where(lo_rows, r1[:, :QB], r1[:, QB:])
                r2 = _dot(dot_ref[pr, :, rs], jnp.concatenate(wss[pr], axis=1).astype(BF16))
                dv_acc[pr, j] += jnp.where(lo_rows, r2[:, :QB], r2[:, QB:])

        def step_head(j, carry):
            block(j, False, SB_HEAD_ROWS)
            return carry

        def step_full(j, carry):
            block(j, False, QB)
            return carry

        def last_two():
            has_prev = i > 0
            jp = jnp.maximum(i - 1, 0)
            rows_p = pl.ds(pl.multiple_of(jp * QB, QB), QB)
            rows_d = pl.ds(pl.multiple_of(i * QB, QB), QB)
            pairs = range(2)
            cols = [slice(pr * 128, (pr + 1) * 128) for pr in pairs]

            def sums(vals, u):
                hi, lo = _split(jnp.concatenate(vals, axis=1))
                return _dot(hi, u) + _dot(lo, u)

            kp = [_pair_rows(k_ref[rows_p, cols[pr]], lo_half) for pr in pairs]
            kd = [_pair_rows(k_ref[rows_d, cols[pr]], lo_half) for pr in pairs]
            zp = [_dot_nt(qs_ref[:, cols[pr]], kp[pr]) for pr in pairs]
            zd = [_dot_nt(qs_ref[:, cols[pr]], kd[pr]) for pr in pairs]
            dwp = [_dot_nt(do_ref[:, cols[pr]], _pair_rows(v_ref[rows_p, cols[pr]], lo_half)) for pr in pairs]
            dwd = [_dot_nt(do_ref[:, cols[pr]], _pair_rows(v_ref[rows_d, cols[pr]], lo_half)) for pr in pairs]
            scp = [_sb_scores(zp[pr], causal, False) for pr in pairs]
            scd = [_sb_scores(zd[pr], causal, True) for pr in pairs]
            lps = [[jnp.where(has_prev, l, 0.0) for l in scp[pr][0]] for pr in pairs]
            plp = [sums(lps[pr], uin) for pr in pairs]
            pld = [sums(scd[pr][0], uin) for pr in pairs]
            wps, wds, gps, gds = [], [], [], []
            for pr in pairs:
                wp, wd, gp, gd = [], [], [], []
                for hh in range(2):
                    h = 2 * pr + hh
                    half = slice(hh * QB, (hh + 1) * QB)
                    rest = t_ref[:, h:h + 1] - pc_ref[h, :, 0:1]
                    w = jnp.where(has_prev, jnp.exp(scp[pr][1][hh] + (rest - plp[pr][:, half])), 0.0)
                    wp.append(w)
                    gp.append(dwp[pr][:, half] * w)
                    rest = rest - plp[pr][:, half][:, QB - 1:QB]
                    w = jnp.where(causal, jnp.exp(scd[pr][1][hh] + (rest - pld[pr][:, half])), 0.0)
                    wd.append(w)
                    gd.append(dwd[pr][:, half] * w)
                wps.append(wp)
                wds.append(wd)
                gps.append(gp)
                gds.append(gd)
            gxp = [sums(gps[pr], uex) for pr in pairs]
            gxd = [sums(gds[pr], uex) for pr in pairs]
            for pr in pairs:
                dzp, dzd = [], []
                for hh in range(2):
                    h = 2 * pr + hh
                    half = slice(hh * QB, (hh + 1) * QB)
                    pg = pc_ref[4 + h, :, 0:1]
                    for z, e, g, gx, out, mask in ((zp[pr], scp[pr][2][hh], gps[pr][hh], gxp[pr], dzp, has_prev),
                                                   (zd[pr], scd[pr][2][hh], gds[pr][hh], gxd[pr], dzd, causal)):
                        r = pl.reciprocal(1.0 + e, approx=True)
                        er = e * r
                        pos = z[:, half] >= 0.0
                        dz = g * jnp.where(pos, er, r) - (gx[:, half] + pg) * jnp.where(pos, r, er)
                        out.append(jnp.where(mask, dz, 0.0))
                        pg = pg + gx[:, half][:, QB - 1:QB] + g[:, QB - 1:QB]
                dzpb = jnp.concatenate(dzp, axis=1).astype(BF16)
                dzdb = jnp.concatenate(dzd, axis=1).astype(BF16)
                dq_acc[:, cols[pr]] += _dot(dzpb, kp[pr]) + _dot(dzdb, kd[pr])
                for jj, dzb, ws in ((jp, dzpb, wps[pr]), (i, dzdb, wds[pr])):
                    r1 = _dot(qt_ref[pr], dzb)
                    dk_acc[pr, jj] += jnp.where(lo_rows, r1[:, :QB], r1[:, QB:])
                    r2 = _dot(dot_ref[pr], jnp.concatenate(ws, axis=1).astype(BF16))
                    dv_acc[pr, jj] += jnp.where(lo_rows, r2[:, :QB], r2[:, QB:])

        lax.fori_loop(jstart, jfull, step_head, 0)
        lax.fori_loop(jfull, i - 1, step_full, 0)
        last_two()
        dq_ref[...] = (dq_acc[...] * 0.125).astype(BF16)

        @pl.when(i == nq - 1)
        def _():
            def put(jj, carry):
                krows = pl.ds(pl.multiple_of(jj * QB, QB), QB)
                for pr in range(2):
                    dk_ref[krows, pr * 128:(pr + 1) * 128] = dk_acc[pr, jj].T.astype(BF16)
                    dv_ref[krows, pr * 128:(pr + 1) * 128] = dv_acc[pr, jj].T.astype(BF16)
                return carry
            lax.fori_loop(0, nq, put, 0)

    return _call(
        body, carried, [proj, proj, proj, dcat, stats], name="sb_bwd", grid=(2, nq),
        in_specs=[pl.BlockSpec((QB, 256), lambda g, i: (i, g)),
                  pl.BlockSpec((S, 256), lambda g, i: (0, 2 + g), pipeline_mode=ONE_BUF),
                  pl.BlockSpec((S, 256), lambda g, i: (0, 4 + g), pipeline_mode=ONE_BUF),
                  pl.BlockSpec((QB, 256), lambda g, i: (i, g)),
                  pl.BlockSpec((None, QB, 128), lambda g, i: (g, i, 0))],
        out_specs=[pl.BlockSpec((QB, 256), lambda g, i: (i, g)),
                   pl.BlockSpec((S, 256), lambda g, i: (0, g)),
                   pl.BlockSpec((S, 256), lambda g, i: (0, g))],
        out_shape=[jax.ShapeDtypeStruct((S, 512), BF16)] * 3,
        scratch_shapes=[pltpu.VMEM((2, nq, 128, QB), F32), pltpu.VMEM((2, nq, 128, QB), F32),
                        pltpu.VMEM((QB, 256), F32), pltpu.VMEM((8, QB, 128), F32), pltpu.VMEM((QB, 256), BF16),
                        pltpu.VMEM((2, 128, QB), BF16), pltpu.VMEM((2, 128, QB), BF16)],
        compiler_params=_cp(2, 56))


GLA_NC = 8
GLA_R = GLA_NC * CHUNK


def _chunk_tri(strict):
    row = lax.broadcasted_iota(jnp.int32, (GLA_R, GLA_R), 0)
    col = lax.broadcasted_iota(jnp.int32, (GLA_R, GLA_R), 1)
    m = jnp.logical_and(row // CHUNK == col // CHUNK, row > col if strict else row >= col)
    u = jnp.where(m, 1.0, 0.0).astype(BF16)
    return jnp.concatenate([u, u], axis=1)


def _per_chunk_rows(vals):
    return jnp.concatenate([jnp.broadcast_to(v, (CHUNK, v.shape[1])) for v in vals], axis=0)


def _head_blocks(st):
    row = lax.broadcasted_iota(jnp.int32, (H_GLA * DV, H_GLA * DK), 0)
    col = lax.broadcasted_iota(jnp.int32, (H_GLA * DV, H_GLA * DK), 1)
    t = jnp.concatenate([st.astype(BF16)] * H_GLA, axis=0)
    return jnp.where(row // DV == col // DK, t, jnp.zeros_like(t))


def _head_diag(big):
    head = lax.broadcasted_iota(jnp.int32, (DV, H_GLA * DK), 1) // DK
    out = big[0:DV]
    for h in range(1, H_GLA):
        out = jnp.where(head == h, big[h * DV:(h + 1) * DV], out)
    return out


def _gla_gate4(gf_ref, wfg_ref, bfg_ref):
    f = _dot(gf_ref[...], wfg_ref[...]) + bfg_ref[...]
    _, la, _ = _log_sigmoid_parts(f)
    lah, lal = _split(la * (1.0 / 16.0))
    cum = _dot(_chunk_tri(False), jnp.concatenate([lah, lal], axis=0))
    tots = [cum[(c + 1) * CHUNK - 1:(c + 1) * CHUNK, :] for c in range(GLA_NC)]
    return f, jnp.exp(_per_chunk_rows(tots) - cum), [jnp.exp(t) for t in tots]


def _gla_specs4(ns, rev):
    def ix(i):
        return ns - 1 - i if rev else i
    return [pl.BlockSpec((GLA_R, 256), lambda i: (ix(i), 6)),
            pl.BlockSpec((GLA_R, 256), lambda i: (ix(i), 7)),
            pl.BlockSpec((GLA_R, 512), lambda i: (ix(i), 4)),
            pl.BlockSpec((GLA_R, 512), lambda i: (ix(i), 5)),
            pl.BlockSpec((GLA_R, 128), lambda i: (ix(i), 24))]


def _gla_fwd(proj, wfg_p, bfg, ggla, carried=()):
    S = proj.shape[0]
    ns = S // GLA_R

    def body(q_ref, k_ref, v_ref, gg_ref, gf_ref, wfg_ref, bfg_ref, ggla_ref, o_ref, st_ref, state):
        @pl.when(pl.program_id(0) == 0)
        def _():
            state[...] = jnp.zeros_like(state)

        _, e, decs = _gla_gate4(gf_ref, wfg_ref, bfg_ref)
        kdec = (k_ref[...].astype(F32) * e).astype(BF16)
        rows = [slice(c * CHUNK, (c + 1) * CHUNK) for c in range(GLA_NC)]
        kvs = [_head_diag(_dot_tn(v_ref[rows[c], :], kdec[rows[c]])) for c in range(GLA_NC)]
        st = state[...]
        sts = []
        for c in range(GLA_NC):
            st = st * decs[c] + kvs[c]
            st_ref[c] = st
            sts.append(st)
        state[...] = st
        o = jnp.concatenate([_dot_nt(q_ref[rows[c], :] * 0.125, _head_blocks(sts[c])) for c in range(GLA_NC)], axis=0)
        for h in range(H_GLA):
            vs = slice(h * DV, (h + 1) * DV)
            oh = o[:, vs]
            ohn = oh * lax.rsqrt(jnp.mean(oh * oh, axis=-1, keepdims=True) + EPS)
            gg = gg_ref[:, vs].astype(F32)
            o_ref[:, vs] = ((ohn * ggla_ref[:, vs]) * (gg * _sigmoid(gg))).astype(BF16)

    return _call(
        body, carried, [proj, proj, proj, proj, proj, wfg_p, bfg, ggla], name="gla_fwd", grid=(ns,),
        in_specs=_gla_specs4(ns, False) + [pl.BlockSpec((128, 256), lambda i: (0, 0)),
                                           pl.BlockSpec((1, 256), lambda i: (0, 0)),
                                           pl.BlockSpec((1, 512), lambda i: (0, 0))],
        out_specs=[pl.BlockSpec((GLA_R, 512), lambda i: (i, 0)),
                   pl.BlockSpec((GLA_NC, 128, 256), lambda i: (i, 0, 0))],
        out_shape=[jax.ShapeDtypeStruct((S, 512), BF16), jax.ShapeDtypeStruct((S // CHUNK, 128, 256), F32)],
        scratch_shapes=[pltpu.VMEM((128, 256), F32)],
        compiler_params=_cp(1))


def _gla_bwd(dcat, proj, states, wfg_p, bfg, ggla, carried=()):
    S = proj.shape[0]
    ns = S // GLA_R

    def body(do_ref, q_ref, k_ref, v_ref, gg_ref, gf_ref, sc_ref, sp_ref, wfg_ref, bfg_ref, ggla_ref,
             dp_ref, s_ref, dw_ref, carry):
        sr = pl.program_id(0)

        @pl.when(sr == 0)
        def _():
            carry[...] = jnp.zeros_like(carry)
            s_ref[...] = jnp.zeros_like(s_ref)
            dw_ref[...] = jnp.zeros_like(dw_ref)

        f, e, decs = _gla_gate4(gf_ref, wfg_ref, bfg_ref)
        kf = k_ref[...].astype(F32) * e
        kdec = kf.astype(BF16)
        rows = [slice(c * CHUNK, (c + 1) * CHUNK) for c in range(GLA_NC)]
        sts = [sc_ref[c] for c in range(GLA_NC)]
        st_before = jnp.where(sr < ns - 1, sp_ref[0], 0.0)
        sbd = [_head_blocks(sts[c]) for c in range(GLA_NC)]
        qs = q_ref[...] * 0.125
        o = jnp.concatenate([_dot_nt(qs[rows[c]], sbd[c]) for c in range(GLA_NC)], axis=0)
        dobs = []
        for h in range(H_GLA):
            vs = slice(h * DV, (h + 1) * DV)
            oh = o[:, vs]
            rr = lax.rsqrt(jnp.mean(oh * oh, axis=-1, keepdims=True) + EPS)
            ohn = oh * rr
            gg = gg_ref[:, vs].astype(F32)
            sg = _sigmoid(gg)
            dout = do_ref[:, vs].astype(F32)
            gl = ggla_ref[:, vs]
            dp_ref[:, 1024 + h * DV:1024 + (h + 1) * DV] = (
                dout * (ohn * gl) * (sg * (1.0 + gg * (1.0 - sg)))).astype(BF16)
            dt1 = dout * (gg * sg)
            s_ref[0:1, vs] += jnp.sum(dt1 * ohn, axis=0, keepdims=True)
            dohn = dt1 * gl
            dobs.append((rr * (dohn - ohn * jnp.mean(dohn * ohn, axis=-1, keepdims=True))).astype(BF16))
        dob = jnp.concatenate(dobs, axis=1)
        dsout = []
        for c in range(GLA_NC):
            dp_ref[rows[c], 0:256] = (_dot(dob[rows[c]], sbd[c]) * 0.125).astype(BF16)
            dsout.append(_head_diag(_dot_tn(dob[rows[c]], qs[rows[c]])))
        g = carry[...]
        gts, ddecs = [None] * GLA_NC, [None] * GLA_NC
        for c in reversed(range(GLA_NC)):
            gts[c] = dsout[c] + g
            ddecs[c] = jnp.sum(gts[c] * (sts[c - 1] if c > 0 else st_before), axis=0, keepdims=True) * decs[c]
            g = gts[c] * decs[c]
        carry[...] = g
        dkds = []
        for c in range(GLA_NC):
            gbd = _head_blocks(gts[c])
            dkds.append(_dot(v_ref[rows[c], :], gbd))
            dp_ref[rows[c], 512:1024] = _dot_nt(kdec[rows[c]], gbd).astype(BF16)
        dkd = jnp.concatenate(dkds, axis=0)
        dp_ref[:, 256:512] = (dkd * e).astype(BF16)
        wh, wl = _split(dkd * kf)
        dla = _dot(_chunk_tri(True), jnp.concatenate([wh, wl], axis=0)) + _per_chunk_rows(ddecs)
        df = dla * _sigmoid(-f) * (1.0 / 16.0)
        dfb = df.astype(BF16)
        s_ref[1:2, 0:256] += jnp.sum(df, axis=0, keepdims=True)
        dw_ref[...] += _dot_tn(gf_ref[...], dfb)
        dp_ref[:, 1536:1664] = _dot_nt(dfb, wfg_ref[...]).astype(BF16)

    return _call(
        body, carried, [dcat, proj, proj, proj, proj, proj, states, states, wfg_p, bfg, ggla],
        name="gla_bwd", grid=(ns,),
        in_specs=[pl.BlockSpec((GLA_R, 512), lambda i: (ns - 1 - i, 1))] + _gla_specs4(ns, True) + [
            pl.BlockSpec((GLA_NC, 128, 256), lambda i: (ns - 1 - i, 0, 0)),
            pl.BlockSpec((1, 128, 256), lambda i: (jnp.maximum((ns - 1 - i) * GLA_NC - 1, 0), 0, 0)),
            pl.BlockSpec((128, 256), lambda i: (0, 0)),
            pl.BlockSpec((1, 256), lambda i: (0, 0)),
            pl.BlockSpec((1, 512), lambda i: (0, 0))],
        out_specs=[pl.BlockSpec((GLA_R, 1664), lambda i: (ns - 1 - i, 0)),
                   pl.BlockSpec((8, 512), lambda i: (0, 0)),
                   pl.BlockSpec((128, 256), lambda i: (0, 0))],
        out_shape=[jax.ShapeDtypeStruct((S, 1664), BF16), jax.ShapeDtypeStruct((8, 512), F32),
                   jax.ShapeDtypeStruct((128, 256), F32)],
        scratch_shapes=[pltpu.VMEM((128, 256), F32)],
        compiler_params=_cp(1))


def _sum_leading(a, name):
    n = a.shape[0]

    def body(a_ref, o_ref):
        acc = a_ref[0]
        for k in range(1, n):
            acc = acc + a_ref[k]
        o_ref[...] = acc

    return pl.pallas_call(
        body, name=name, out_shape=jax.ShapeDtypeStruct(a.shape[1:], F32),
        in_specs=[VMEM_SPEC], out_specs=VMEM_SPEC,
    )(a)


def _sum_chip(own, recv, name):
    R, C = own.shape
    tr, tc = _tile2d(R, C, 1024 * 1024)

    def body(o_ref, r_ref, p_ref):
        acc = o_ref[...]
        for k in range(3):
            acc = acc + r_ref[k].astype(F32)
        p_ref[...] = acc

    return pl.pallas_call(
        body, name=name, grid=(R // tr, C // tc),
        in_specs=[pl.BlockSpec((tr, tc), lambda i, j: (i, j)), pl.BlockSpec((3, tr, tc), lambda i, j: (0, i, j))],
        out_specs=pl.BlockSpec((tr, tc), lambda i, j: (i, j)),
        out_shape=jax.ShapeDtypeStruct((R, C), F32), compiler_params=_cp(2, 40),
    )(own, recv)


def _adamw(w, p, q, m, v, name):
    R, C = w.shape
    tr, tc = _tile2d(R, C, 1024 * 1024)
    two = q is not None

    def body(*refs):
        if two:
            w_ref, p_ref, q_ref, m_ref, v_ref, g_out, d_out, m_out, v_out = refs
            g = p_ref[...] + q_ref[...]
        else:
            w_ref, p_ref, m_ref, v_ref, g_out, d_out, m_out, v_out = refs
            g = p_ref[...]
        m2 = B1 * m_ref[...] + (1.0 - B1) * g
        v2 = B2 * v_ref[...] + (1.0 - B2) * (g * g)
        m_hat = m2 / (1.0 - B1 ** STEP)
        v_hat = v2 / (1.0 - B2 ** STEP)
        g_out[...] = g
        d_out[...] = -LR * (m_hat / (jnp.sqrt(v_hat) + EPS_A) + WD * w_ref[...])
        m_out[...] = m2
        v_out[...] = v2

    spec = pl.BlockSpec((tr, tc), lambda i, j: (i, j))
    ins = [w, p, q, m, v] if two else [w, p, m, v]
    return pl.pallas_call(
        body, name=name, grid=(R // tr, C // tc),
        in_specs=[spec] * len(ins), out_specs=[spec] * 4,
        out_shape=[jax.ShapeDtypeStruct((R, C), F32)] * 4, compiler_params=_cp(2, 40),
    )(*ins)


def _cols_to_chips(a, width):
    return a.reshape(a.shape[0], 4, width).swapaxes(0, 1)


def _chips_to_cols(a):
    return a.swapaxes(0, 1).reshape(a.shape[1], 4 * a.shape[2])


def _swap_mid(a):
    lead = a.shape[:-1]
    return a.reshape(lead + (2, 2, HB)).swapaxes(-3, -2).reshape(lead + (4 * HB,))


def kernel(x, c, w_ada, b_ada, g_norm1, w_in, w_fg2, b_fg2, g_gla_out, w_out, g_norm2, w_up, w_conv, b_conv, w_down, g_final, loss_target, m_w_ada, m_b_ada, m_g_norm1, m_w_in, m_w_fg2, m_b_fg2, m_g_gla_out, m_w_out, m_g_norm2, m_w_up, m_w_conv, m_b_conv, m_w_down, m_g_final, v_w_ada, v_b_ada, v_g_norm1, v_w_in, v_w_fg2, v_b_fg2, v_g_gla_out, v_w_out, v_g_norm2, v_w_up, v_w_conv, v_b_conv, v_w_down, v_g_final):
    xi, yi, ci = lax.axis_index("x"), lax.axis_index("y"), lax.axis_index("c")
    cidx = 2 * xi + yi
    didx = 4 * xi + 2 * yi + ci
    xs = x[0]
    tgt = loss_target[0]
    gfin = g_final.reshape(1, D)
    AW = D * 6 // 4

    c_all = _allgather8(c, "gather_c").reshape(8, D)
    c_pad = jnp.concatenate([c_all, jnp.zeros((8, D), F32)], axis=0)
    mod_part = _ada_fwd(c_pad, w_ada[0], lax.dynamic_slice(b_ada, (0, cidx * AW), (1, AW)))[:8]
    small = jnp.concatenate([mod_part.reshape(-1), w_conv.reshape(-1), w_fg2.reshape(-1)]).reshape(-1, 128)
    small_g = _allgather4(small, "gather_small").reshape(4, -1)
    mod = lax.dynamic_index_in_dim(small_g[:, :8 * AW].reshape(4, 8, AW), didx, axis=1, keepdims=False).reshape(1, 6 * D)
    shift1, scale1, gate1, shift2, scale2, gate2 = [mod[:, k * D:(k + 1) * D] for k in range(6)]
    o1 = 8 * AW
    o2 = o1 + 3 * HB
    wc_p = _swap_mid(_chips_to_cols(small_g[:, o1:o2].reshape(4, 3, HB)))
    bc_p = _swap_mid(b_conv)
    wfg_full = _chips_to_cols(small_g[:, o2:].reshape(4, RANK, 64))
    wfg_p = jnp.concatenate([wfg_full, jnp.zeros((128 - RANK, 256), F32)], axis=0).astype(BF16)

    w_in_t = _allgather4_split(w_in[0].T.astype(BF16), "gather_w_in").reshape(N_IN, D)
    w_in_t = jnp.concatenate([w_in_t, jnp.zeros((N_IN_P - N_IN, D), BF16)], axis=0)
    w_in_p = w_in_t.T

    (h, proj), (w_down_g,) = _norm_mod_mm(xs, g_norm1, shift1, scale1, w_in_p,
                                          carried=[("gather", w_down[0].astype(BF16), False)])
    w_down_f = w_down_g.reshape(D_FF, D)
    (o_gla, states), (w_out_g,) = _gla_fwd(proj, wfg_p, b_fg2, g_gla_out,
                                           carried=[("gather", w_out[0].astype(BF16), False)])
    w_out_f = w_out_g.reshape(D, D)
    (o_sb, stats), (w_up_g,) = _sb_fwd(proj, carried=[("gather", w_up[0].astype(BF16), False)])
    w_up_p = _swap_mid(_chips_to_cols(w_up_g))
    x1, h2, mixed = _resid_norm_mod(xs, [(o_sb, w_out_f[:512]), (o_gla, w_out_f[512:])],
                                    gate1, g_norm2, shift2, scale2)
    u0p, a = _up_conv_glu(h2, w_up_p, wc_p, bc_p)
    dx2, dy2, s_fin = _final_loss(x1, [(a, w_down_f)], gate2, gfin, tgt)

    da = _mm([(dy2, w_down_f.T)], BF16, "mm_down_t", 512, D_FF, 48)
    dw_down, dw_down_h = [t.reshape(4, D_FF // 4, D) for t in _mm_tn([a], dy2, "mm_dw_down", D, 1024, 60)]
    (du0p, s_conv, dx1, dmixed, s_n2), (rc_down,) = _conv_glu_up_norm2_bwd(
        da, u0p, wc_p, bc_p, w_up_p.T, x1, dx2, g_norm2, scale2, mixed, gate1,
        carried=[("scatter", dw_down_h, False)])
    dw_up, dw_up_h = _mm_tn([h2], du0p, "mm_dw_up", HB, 2048, 56)
    dcat = _mm([(dmixed, w_out_f.T)], BF16, "mm_out_t", 512, D)
    dw_out, dw_out_h = [t.reshape(4, D // 4, D) for t in _mm_tn([o_sb, o_gla], dmixed, "mm_dw_out", D, 512)]
    (dq, dk, dv), (rc_up,) = _sb_bwd(proj, dcat, stats, carried=[("scatter", dw_up_h, True)])
    (dp_gla, s_gla, dwfg), (rc_out,) = _gla_bwd(dcat, proj, states, wfg_p, b_fg2, g_gla_out,
                                                carried=[("scatter", dw_out_h, False)])
    dw_in, dw_in_h = _mm_tn([dq, dk, dv, dp_gla], h, "mm_dw_in", D, 1024, 60)
    dw_in_h = dw_in_h[0, :N_IN].reshape(4, N_IN // 4, D)
    dw_in_own = lax.dynamic_slice(dw_in[0], (cidx * (N_IN // 4), 0), (N_IN // 4, D))
    (gx, s_n1), (rc_in,) = _norm_mod_bwd(
        [(dq, w_in_t[:512]), (dk, w_in_t[512:1024]), (dv, w_in_t[1024:1536]), (dp_gla, w_in_t[1536:])],
        xs, dx1, g_norm1, scale1, None, None, "mm_in_t_norm1_bwd", carried=[("scatter", dw_in_h, False)])

    dmod = jnp.concatenate([s_n1[0], s_n1[1], s_n2[3], s_n2[0], s_n2[1], s_fin[1]])
    s_conv_n = _swap_mid(s_conv[:4])
    part = jnp.concatenate([dmod, s_n1[2], s_n2[2], s_fin[0], s_gla[0], s_gla[1, :256], s_conv_n[0],
                            s_conv_n[1:4].reshape(-1), dwfg[:RANK].reshape(-1),
                            jnp.broadcast_to(jnp.sum(s_fin[2]), (128,))]).reshape(-1, 128)
    parts = _allgather8(part, "gather_small_grads")
    tot = _sum_leading(parts, "sum_small_grads").reshape(-1)
    loss = 0.5 / D * tot[-1]
    dmod_all = parts.reshape(8, -1)[:, :6 * D]
    offs = [0]
    for n in (6 * D, D, D, D, 512, 256, 2 * D_FF, 3 * 2 * D_FF, RANK * 256):
        offs.append(offs[-1] + n)
    g_b_ada, g_g1, g_g2, g_gf, g_ggla, g_bfg, g_bconv, g_wconv_full, g_wfg_full = [
        tot[offs[k]:offs[k + 1]] for k in range(9)]
    g_wconv = lax.dynamic_index_in_dim(_cols_to_chips(g_wconv_full.reshape(3, 2 * D_FF), HB), cidx, 0, keepdims=False)
    g_wfg = lax.dynamic_index_in_dim(_cols_to_chips(g_wfg_full.reshape(RANK, 256), 64), cidx, 0, keepdims=False)

    dmod_pad = jnp.concatenate([dmod_all, jnp.zeros((8, 6 * D), F32)], axis=0)
    g_w_ada = _ada_bwd(c_pad, lax.dynamic_slice(dmod_pad, (0, cidx * AW), (16, AW)))

    def own(blocks, swapped=False):
        return lax.dynamic_index_in_dim(blocks, _slot(cidx, swapped), axis=0, keepdims=False)

    p_in = _sum_chip(dw_in_own, rc_in, "rs_w_in_sum")
    p_out = _sum_chip(own(dw_out), rc_out, "rs_w_out_sum")
    p_up = _sum_chip(own(dw_up, True), rc_up, "rs_w_up_sum")
    p_down = _sum_chip(own(dw_down), rc_down, "rs_w_down_sum")
    q_in, q_out, q_up, q_down = _pair_swap([p_in, p_out, p_up, p_down], "rs_swap")

    out = {}
    out["w_ada"] = _adamw(w_ada[0], g_w_ada, None, m_w_ada[0], v_w_ada[0], "adamw_w_ada")
    out["w_in"] = [t.T for t in _adamw(w_in[0].T, p_in, q_in, m_w_in[0].T, v_w_in[0].T, "adamw_w_in")]
    out["w_out"] = _adamw(w_out[0], p_out, q_out, m_w_out[0], v_w_out[0], "adamw_w_out")
    out["w_up"] = _adamw(w_up[0], p_up, q_up, m_w_up[0], v_w_up[0], "adamw_w_up")
    out["w_down"] = _adamw(w_down[0], p_down, q_down, m_w_down[0], v_w_down[0], "adamw_w_down")
    small_names = ["b_ada", "g_norm1", "w_fg2", "b_fg2", "g_gla_out", "g_norm2", "w_conv", "b_conv", "g_final"]
    small_w = [b_ada, g_norm1, w_fg2, b_fg2, g_gla_out, g_norm2, w_conv, b_conv, g_final]
    small_m = [m_b_ada, m_g_norm1, m_w_fg2, m_b_fg2, m_g_gla_out, m_g_norm2, m_w_conv, m_b_conv, m_g_final]
    small_v = [v_b_ada, v_g_norm1, v_w_fg2, v_b_fg2, v_g_gla_out, v_g_norm2, v_w_conv, v_b_conv, v_g_final]
    small_gr = [g_b_ada, g_g1, g_wfg, g_bfg, g_ggla, g_g2, g_wconv, g_bconv, g_gf]

    def pack(arrs):
        flat = jnp.concatenate([t.reshape(-1) for t in arrs])
        return jnp.concatenate([flat, jnp.zeros((-flat.shape[0]) % 1024, F32)]).reshape(-1, 128)

    packed = _adamw(pack(small_w), pack(small_gr), None, pack(small_m), pack(small_v), "adamw_small")
    off = 0
    for nm, wt in zip(small_names, small_w):
        n = wt.size
        out[nm] = [t.reshape(-1)[off:off + n].reshape(wt.shape) for t in packed]
        off += n
    for nm in ("w_ada", "w_in", "w_out", "w_up", "w_down"):
        out[nm] = [t[None] for t in out[nm]]

    names = ["w_ada", "b_ada", "g_norm1", "w_in", "w_fg2", "b_fg2", "g_gla_out", "w_out", "g_norm2", "w_up",
             "w_conv", "b_conv", "w_down", "g_final"]
    res = [loss, gx[None]]
    for k in range(4):
        res += [out[nm][k] for nm in names]
    return tuple(res)
```
